```python
import math
import jax, jax.numpy as jnp
from jax import lax
import numpy as np

D_MODEL = 1024
BATCH = 8
SEQ = 4096
DEPTH = 2

GRID_W = 64
CTX_LEN = 256

GDN_HEADS = 4
GDN_DK = 128
GDN_DV = 128
GDN_QK = GDN_HEADS * GDN_DK
GDN_V = GDN_HEADS * GDN_DV
GDN_CHUNK = 64
GDN_SHORT_CONV = 5
POOL_WINDOWS = (2, 4, 8, 16)
N_POOL = len(POOL_WINDOWS)
POOL_DIM = D_MODEL // 2
POOL_GROUP = POOL_DIM // N_POOL
E_K = GDN_QK
E_V = 2 * GDN_QK
E_GATE = E_V + GDN_V
E_POOL = E_GATE + GDN_V
E_SCAL = E_POOL + POOL_DIM
EVEN_IN = E_SCAL + 4 * GDN_HEADS
EVEN_MIX = GDN_V + POOL_DIM

SC_DIM = D_MODEL // 2
SC_WIDTH = 3
CF_DIM = D_MODEL // 2
CF_WIDTH = 31
ODD_IN = 3 * SC_DIM + 2 * CF_DIM
ODD_MIX = SC_DIM + CF_DIM

D_FF = ((8 * D_MODEL // 3 + 127) // 128) * 128
FFN_CONV = 3

ALPHA = (2 * DEPTH) ** 0.25
BETA = (8 * DEPTH) ** -0.25
LN_EPS = 1e-5
RMS_EPS = 1e-6

kernel_name = 'hybrid_gdn_pool_conv_diffusion_block'


def _layernorm(x, g, b):
    xf = x.astype(jnp.float32)
    mu = jnp.mean(xf, -1, keepdims=True)
    var = jnp.mean(jnp.square(xf - mu), -1, keepdims=True)
    return ((xf - mu) * lax.rsqrt(var + LN_EPS)).astype(x.dtype) * g + b


def _modulate(h, shift, scale):
    return h * (1 + scale) + shift


def _dwconv1d(x, w):
    width, ch = w.shape
    return lax.conv_general_dilated(x, w[:, None, :], window_strides=(1,), padding=[(width // 2, width // 2)], dimension_numbers=('NWC', 'WIO', 'NWC'), feature_group_count=ch)


def _dwconv2d_grid(x, w):
    b, t, ch = x.shape
    rows = t // GRID_W
    xg = x.reshape(b, rows, GRID_W, ch)
    y = lax.conv_general_dilated(xg, w[:, :, None, :], window_strides=(1, 1), padding=[(FFN_CONV // 2, FFN_CONV // 2)] * 2, dimension_numbers=('NHWC', 'HWIO', 'NHWC'), feature_group_count=ch)
    return y.reshape(b, t, ch)


def _heads(t):
    b, n, hd = t.shape
    return t.reshape(b, n, GDN_HEADS, hd // GDN_HEADS).transpose(0, 2, 1, 3).astype(jnp.float32)


def _l2norm(t):
    return t * lax.rsqrt(jnp.sum(t * t, -1, keepdims=True) + RMS_EPS)


def _gdn_gates(s, a_log, dt_bias):
    b, n, _ = s.shape
    s = s.astype(jnp.float32).reshape(b, n, 4, GDN_HEADS).transpose(2, 0, 3, 1)
    beta = jax.nn.sigmoid(s[:2])
    g = -jnp.exp(a_log.astype(jnp.float32))[:, None, :, None] * jax.nn.softplus(s[2:] + dt_bias.astype(jnp.float32)[:, None, :, None])
    return beta, g


def _chunk_masks():
    idx = jnp.arange(GDN_CHUNK)
    return idx[:, None] >= idx[None, :], idx[:, None] > idx[None, :]


def _gdn_chunk_terms(k, v, g, beta):
    b, h, n, _ = k.shape
    nc = n // GDN_CHUNK
    k = k.reshape(b, h, nc, GDN_CHUNK, GDN_DK)
    v = v.reshape(b, h, nc, GDN_CHUNK, GDN_DV)
    beta = beta.reshape(b, h, nc, GDN_CHUNK, 1)
    gc = jnp.cumsum(g.reshape(b, h, nc, GDN_CHUNK), axis=-1)
    lower, strict = _chunk_masks()
    decay = jnp.exp(jnp.where(lower, gc[..., :, None] - gc[..., None, :], -jnp.inf))
    kb = k * beta
    a_mat = jnp.where(strict, jnp.einsum('bhnck,bhnsk->bhncs', kb, k) * decay, 0.0)
    rhs = jnp.concatenate([v * beta, kb * jnp.exp(gc)[..., None]], axis=-1)
    sol = lax.linalg.triangular_solve(a_mat + jnp.eye(GDN_CHUNK, dtype=a_mat.dtype), rhs, left_side=True, lower=True, unit_diagonal=True)
    u, w = sol[..., :GDN_DV], sol[..., GDN_DV:]
    k_dec = k * jnp.exp(gc[..., -1:] - gc)[..., None]
    g_last = jnp.exp(gc[..., -1])
    return gc, decay, u, w, k_dec, g_last


def _state_step(s, u_i, w_i, kd_i, gl_i):
    v_new = u_i - jnp.einsum('bhck,bhkv->bhcv', w_i, s)
    s_next = s * gl_i[..., None, None] + jnp.einsum('bhck,bhcv->bhkv', kd_i, v_new)
    return s_next, v_new


def _chunks_first(*arrays):
    return tuple(jnp.moveaxis(a, 2, 0) for a in arrays)


def _gdn_final_state(k, v, g, beta, s0):
    _, _, u, w, k_dec, g_last = _gdn_chunk_terms(k, v, g, beta)

    def step(s, xs):
        return _state_step(s, *xs)[0], None

    s, _ = lax.scan(step, s0, _chunks_first(u, w, k_dec, g_last))
    return s


def _gdn_attend(q, k, v, g, beta, s0):
    b, h, n, _ = q.shape
    gc, decay, u, w, k_dec, g_last = _gdn_chunk_terms(k, v, g, beta)
    qc = q.reshape(b, h, n // GDN_CHUNK, GDN_CHUNK, GDN_DK)
    kc = k.reshape(b, h, n // GDN_CHUNK, GDN_CHUNK, GDN_DK)
    lower, _ = _chunk_masks()
    attn = jnp.where(lower, jnp.einsum('bhnck,bhnsk->bhncs', qc, kc) * decay, 0.0)
    q_dec = qc * jnp.exp(gc)[..., None]

    def step(s, xs):
        u_i, w_i, kd_i, gl_i, qd_i, at_i = xs
        s_next, v_new = _state_step(s, u_i, w_i, kd_i, gl_i)
        o_i = jnp.einsum('bhck,bhkv->bhcv', qd_i, s) + jnp.einsum('bhcs,bhsv->bhcv', at_i, v_new)
        return s_next, o_i

    _, o = lax.scan(step, s0, _chunks_first(u, w, k_dec, g_last, q_dec, attn))
    return jnp.moveaxis(o, 0, 2).reshape(b, h, n, GDN_DV)


def _flip(a):
    return jnp.flip(a, axis=2)


def _multiscale_pool(p, pool_w, pool_scale):
    b, t, _ = p.shape
    pg = p.astype(jnp.float32).reshape(b, t, N_POOL, POOL_GROUP)
    cs = jnp.concatenate([jnp.zeros((b, 1, N_POOL, POOL_GROUP), jnp.float32), jnp.cumsum(pg, axis=1)], axis=1)
    pos = jnp.arange(t)
    groups = []
    for gi, win in enumerate(POOL_WINDOWS):
        lo = jnp.clip(pos - win // 2, 0, t)
        hi = jnp.clip(pos - win // 2 + win, 0, t)
        csg = cs[:, :, gi]
        mean = (csg[:, hi] - csg[:, lo]) / (hi - lo).astype(jnp.float32)[None, :, None]
        groups.append(mean - pg[:, :, gi])
    pooled = jnp.stack(groups, axis=2).astype(p.dtype)
    y = jnp.einsum('btgc,gcd->btgd', pooled, pool_w)
    return y.reshape(b, t, POOL_DIM) * pool_scale


def _even_mixer(u, ctx_u, w_in, w_out, conv_w, a_log, dt_bias, norm_w, pool_w, pool_scale):
    b, n, _ = u.shape
    p = u @ w_in
    qkv = jax.nn.silu(_dwconv1d(p[..., :E_GATE], conv_w))
    q = _l2norm(_heads(qkv[..., :E_K])) * GDN_DK ** -0.5
    k = _l2norm(_heads(qkv[..., E_K:E_V]))
    v = _heads(qkv[..., E_V:])
    beta, g = _gdn_gates(p[..., E_SCAL:], a_log, dt_bias)
    pc = ctx_u @ jnp.concatenate([w_in[:, E_K:E_GATE], w_in[:, E_SCAL:]], axis=1)
    kv_c = jax.nn.silu(_dwconv1d(pc[..., :E_GATE - E_K], conv_w[:, E_K:E_GATE]))
    k_c = _l2norm(_heads(kv_c[..., :GDN_QK]))
    v_c = _heads(kv_c[..., GDN_QK:])
    beta_c, g_c = _gdn_gates(pc[..., E_GATE - E_K:], a_log, dt_bias)
    s0 = jnp.zeros((b, GDN_HEADS, GDN_DK, GDN_DV), jnp.float32)
    s_fwd = _gdn_final_state(k_c, v_c, g_c[0], beta_c[0], s0)
    s_bwd = _gdn_final_state(_flip(k_c), _flip(v_c), _flip(g_c[1]), _flip(beta_c[1]), s0)
    o = _gdn_attend(q, k, v, g[0], beta[0], s_fwd) + _flip(_gdn_attend(_flip(q), _flip(k), _flip(v), _flip(g[1]), _flip(beta[1]), s_bwd))
    o = o.transpose(0, 2, 1, 3)
    o = o * lax.rsqrt(jnp.mean(o * o, -1, keepdims=True) + RMS_EPS) * norm_w.astype(jnp.float32)
    o = o.reshape(b, n, GDN_V).astype(u.dtype) * jax.nn.silu(p[..., E_GATE:E_POOL])
    y_pool = _multiscale_pool(p[..., E_POOL:E_SCAL], pool_w, pool_scale)
    return jnp.concatenate([o, y_pool], axis=-1) @ w_out


def _odd_mixer(u, w_in, w_out, sconv_w, conf_conv_w, conf_ln_g, conf_ln_b):
    p = u @ w_in
    g_b, g_c, h, glu_a, glu_b = jnp.split(p, [SC_DIM, 2 * SC_DIM, 3 * SC_DIM, 3 * SC_DIM + CF_DIM], axis=-1)
    y_sc = g_b * _dwconv1d(g_c * h, sconv_w)
    z = _dwconv1d(glu_a * jax.nn.sigmoid(glu_b), conf_conv_w)
    z = jax.nn.silu(_layernorm(z, conf_ln_g, conf_ln_b))
    return jnp.concatenate([y_sc, z], axis=-1) @ w_out


def _conv_ffn(u, w_up, conv_w, w_down):
    a, gate = jnp.split(u @ w_up, 2, axis=-1)
    return (jax.nn.silu(_dwconv2d_grid(a, conv_w)) * gate) @ w_down


def _fwd_setup_inputs(seed: int = 0) -> dict:
    key = jax.random.key(seed)
    ks = jax.random.split(key, 26)
    D = D_MODEL

    def nrm(k, shape, scale=1.0):
        return jax.random.normal(k, shape, jnp.float32) * scale

    dt = jnp.exp(jax.random.uniform(ks[10], (2, GDN_HEADS), jnp.float32, math.log(1e-3), math.log(1e-1)))
    return {
        'x': nrm(ks[0], (BATCH, SEQ, D)),
        'c': nrm(ks[1], (BATCH, D)),
        'ctx': nrm(ks[2], (BATCH, CTX_LEN, D)),
        'c_ctx': nrm(ks[3], (D,)),
        'ada_w': nrm(ks[4], (DEPTH, D, 6 * D), 0.5 * D ** -0.5),
        'ada_b': nrm(ks[5], (DEPTH, 6 * D), 0.02),
        'ln_g': 1.0 + nrm(ks[6], (DEPTH, 2, D), 0.02),
        'ln_b': nrm(ks[7], (DEPTH, 2, D), 0.02),
        'even_w_in': nrm(ks[8], (D, EVEN_IN), D ** -0.5),
        'even_w_out': nrm(ks[9], (EVEN_MIX, D), BETA * EVEN_MIX ** -0.5),
        'gdn_conv_w': nrm(ks[11], (GDN_SHORT_CONV, E_GATE), GDN_SHORT_CONV ** -0.5),
        'gdn_a_log': jnp.log(jax.random.uniform(ks[12], (2, GDN_HEADS), jnp.float32, 1.0, 16.0)),
        'gdn_dt_bias': dt + jnp.log(-jnp.expm1(-dt)),
        'gdn_norm_w': 1.0 + nrm(ks[13], (GDN_DV,), 0.02),
        'pool_w': nrm(ks[14], (N_POOL, POOL_GROUP, POOL_GROUP), POOL_GROUP ** -0.5),
        'pool_scale': 1.0 + nrm(ks[15], (POOL_DIM,), 0.1),
        'odd_w_in': nrm(ks[16], (D, ODD_IN), D ** -0.5),
        'odd_w_out': nrm(ks[17], (ODD_MIX, D), BETA * ODD_MIX ** -0.5),
        'sconv_w': nrm(ks[18], (SC_WIDTH, SC_DIM), SC_WIDTH ** -0.5),
        'conf_conv_w': nrm(ks[19], (CF_WIDTH, CF_DIM), CF_WIDTH ** -0.5),
        'conf_ln_g': 1.0 + nrm(ks[20], (CF_DIM,), 0.02),
        'conf_ln_b': nrm(ks[21], (CF_DIM,), 0.02),
        'ffn_w_up': nrm(ks[22], (DEPTH, D, 2 * D_FF), D ** -0.5),
        'ffn_conv_w': nrm(ks[23], (DEPTH, FFN_CONV, FFN_CONV, D_FF), 1.0 / FFN_CONV),
        'ffn_w_down': nrm(ks[24], (DEPTH, D_FF, D), BETA * D_FF ** -0.5),
    }


def _fwd_reference(x, c, ctx, c_ctx, ada_w, ada_b, ln_g, ln_b, even_w_in, even_w_out, gdn_conv_w, gdn_a_log, gdn_dt_bias, gdn_norm_w, pool_w, pool_scale, odd_w_in, odd_w_out, sconv_w, conf_conv_w, conf_ln_g, conf_ln_b, ffn_w_up, ffn_conv_w, ffn_w_down):
    D = D_MODEL
    silu_c = jax.nn.silu(c)
    silu_cc = jax.nn.silu(c_ctx)
    for layer in range(DEPTH):
        mod = silu_c @ ada_w[layer] + ada_b[layer]
        sh_m, sc_m, gt_m, sh_f, sc_f, gt_f = [m[:, None, :] for m in jnp.split(mod, 6, axis=-1)]
        u = _modulate(x, sh_m, sc_m)
        if layer % 2 == 0:
            mod_c = silu_cc @ ada_w[layer][:, :2 * D] + ada_b[layer][:2 * D]
            ctx_u = _modulate(ctx, mod_c[:D], mod_c[D:])
            y = _even_mixer(u, ctx_u, even_w_in, even_w_out, gdn_conv_w, gdn_a_log, gdn_dt_bias, gdn_norm_w, pool_w, pool_scale)
        else:
            y = _odd_mixer(u, odd_w_in, odd_w_out, sconv_w, conf_conv_w, conf_ln_g, conf_ln_b)
        x = _layernorm(ALPHA * x + gt_m * y, ln_g[layer, 0], ln_b[layer, 0])
        u = _modulate(x, sh_f, sc_f)
        y = _conv_ffn(u, ffn_w_up[layer], ffn_conv_w[layer], ffn_w_down[layer])
        x = _layernorm(ALPHA * x + gt_f * y, ln_g[layer, 1], ln_b[layer, 1])
    return x


import jax as _jax
import jax.numpy as _jnp

TWIN_FORMAT = 'train_step'
FWD_PARAMS = ['x', 'c', 'ctx', 'c_ctx', 'ada_w', 'ada_b', 'ln_g', 'ln_b', 'even_w_in', 'even_w_out', 'gdn_conv_w', 'gdn_a_log', 'gdn_dt_bias', 'gdn_norm_w', 'pool_w', 'pool_scale', 'odd_w_in', 'odd_w_out', 'sconv_w', 'conf_conv_w', 'conf_ln_g', 'conf_ln_b', 'ffn_w_up', 'ffn_conv_w', 'ffn_w_down']
TWIN_WEIGHTS = ['c_ctx', 'ada_w', 'ada_b', 'ln_g', 'ln_b', 'even_w_in', 'even_w_out', 'gdn_conv_w', 'gdn_a_log', 'gdn_dt_bias', 'gdn_norm_w', 'pool_w', 'pool_scale', 'odd_w_in', 'odd_w_out', 'sconv_w', 'conf_conv_w', 'conf_ln_g', 'conf_ln_b', 'ffn_w_up', 'ffn_conv_w', 'ffn_w_down']
TWIN_DIFF_INPUT = 'x'
TWIN_INPUTS = ['x', 'c', 'ctx', 'c_ctx', 'ada_w', 'ada_b', 'ln_g', 'ln_b', 'even_w_in', 'even_w_out', 'gdn_conv_w', 'gdn_a_log', 'gdn_dt_bias', 'gdn_norm_w', 'pool_w', 'pool_scale', 'odd_w_in', 'odd_w_out', 'sconv_w', 'conf_conv_w', 'conf_ln_g', 'conf_ln_b', 'ffn_w_up', 'ffn_conv_w', 'ffn_w_down', 'loss_target', 'm_c_ctx', 'm_ada_w', 'm_ada_b', 'm_ln_g', 'm_ln_b', 'm_even_w_in', 'm_even_w_out', 'm_gdn_conv_w', 'm_gdn_a_log', 'm_gdn_dt_bias', 'm_gdn_norm_w', 'm_pool_w', 'm_pool_scale', 'm_odd_w_in', 'm_odd_w_out', 'm_sconv_w', 'm_conf_conv_w', 'm_conf_ln_g', 'm_conf_ln_b', 'm_ffn_w_up', 'm_ffn_conv_w', 'm_ffn_w_down', 'v_c_ctx', 'v_ada_w', 'v_ada_b', 'v_ln_g', 'v_ln_b', 'v_even_w_in', 'v_even_w_out', 'v_gdn_conv_w', 'v_gdn_a_log', 'v_gdn_dt_bias', 'v_gdn_norm_w', 'v_pool_w', 'v_pool_scale', 'v_odd_w_in', 'v_odd_w_out', 'v_sconv_w', 'v_conf_conv_w', 'v_conf_ln_g', 'v_conf_ln_b', 'v_ffn_w_up', 'v_ffn_conv_w', 'v_ffn_w_down']
TWIN_OUTPUTS = ['loss', 'grad_x', 'grad_c_ctx', 'grad_ada_w', 'grad_ada_b', 'grad_ln_g', 'grad_ln_b', 'grad_even_w_in', 'grad_even_w_out', 'grad_gdn_conv_w', 'grad_gdn_a_log', 'grad_gdn_dt_bias', 'grad_gdn_norm_w', 'grad_pool_w', 'grad_pool_scale', 'grad_odd_w_in', 'grad_odd_w_out', 'grad_sconv_w', 'grad_conf_conv_w', 'grad_conf_ln_g', 'grad_conf_ln_b', 'grad_ffn_w_up', 'grad_ffn_conv_w', 'grad_ffn_w_down', 'delta_c_ctx', 'delta_ada_w', 'delta_ada_b', 'delta_ln_g', 'delta_ln_b', 'delta_even_w_in', 'delta_even_w_out', 'delta_gdn_conv_w', 'delta_gdn_a_log', 'delta_gdn_dt_bias', 'delta_gdn_norm_w', 'delta_pool_w', 'delta_pool_scale', 'delta_odd_w_in', 'delta_odd_w_out', 'delta_sconv_w', 'delta_conf_conv_w', 'delta_conf_ln_g', 'delta_conf_ln_b', 'delta_ffn_w_up', 'delta_ffn_conv_w', 'delta_ffn_w_down', 'new_m_c_ctx', 'new_m_ada_w', 'new_m_ada_b', 'new_m_ln_g', 'new_m_ln_b', 'new_m_even_w_in', 'new_m_even_w_out', 'new_m_gdn_conv_w', 'new_m_gdn_a_log', 'new_m_gdn_dt_bias', 'new_m_gdn_norm_w', 'new_m_pool_w', 'new_m_pool_scale', 'new_m_odd_w_in', 'new_m_odd_w_out', 'new_m_sconv_w', 'new_m_conf_conv_w', 'new_m_conf_ln_g', 'new_m_conf_ln_b', 'new_m_ffn_w_up', 'new_m_ffn_conv_w', 'new_m_ffn_w_down', 'new_v_c_ctx', 'new_v_ada_w', 'new_v_ada_b', 'new_v_ln_g', 'new_v_ln_b', 'new_v_even_w_in', 'new_v_even_w_out', 'new_v_gdn_conv_w', 'new_v_gdn_a_log', 'new_v_gdn_dt_bias', 'new_v_gdn_norm_w', 'new_v_pool_w', 'new_v_pool_scale', 'new_v_odd_w_in', 'new_v_odd_w_out', 'new_v_sconv_w', 'new_v_conf_conv_w', 'new_v_conf_ln_g', 'new_v_conf_ln_b', 'new_v_ffn_w_up', 'new_v_ffn_conv_w', 'new_v_ffn_w_down']
TWIN_LEAF_KINDS = {'loss': 'loss', 'grad_x': 'grad_x', 'grad_c_ctx': 'grad_w', 'grad_ada_w': 'grad_w', 'grad_ada_b': 'grad_w', 'grad_ln_g': 'grad_w', 'grad_ln_b': 'grad_w', 'grad_even_w_in': 'grad_w', 'grad_even_w_out': 'grad_w', 'grad_gdn_conv_w': 'grad_w', 'grad_gdn_a_log': 'grad_w', 'grad_gdn_dt_bias': 'grad_w', 'grad_gdn_norm_w': 'grad_w', 'grad_pool_w': 'grad_w', 'grad_pool_scale': 'grad_w', 'grad_odd_w_in': 'grad_w', 'grad_odd_w_out': 'grad_w', 'grad_sconv_w': 'grad_w', 'grad_conf_conv_w': 'grad_w', 'grad_conf_ln_g': 'grad_w', 'grad_conf_ln_b': 'grad_w', 'grad_ffn_w_up': 'grad_w', 'grad_ffn_conv_w': 'grad_w', 'grad_ffn_w_down': 'grad_w', 'delta_c_ctx': 'delta_w', 'delta_ada_w': 'delta_w', 'delta_ada_b': 'delta_w', 'delta_ln_g': 'delta_w', 'delta_ln_b': 'delta_w', 'delta_even_w_in': 'delta_w', 'delta_even_w_out': 'delta_w', 'delta_gdn_conv_w': 'delta_w', 'delta_gdn_a_log': 'delta_w', 'delta_gdn_dt_bias': 'delta_w', 'delta_gdn_norm_w': 'delta_w', 'delta_pool_w': 'delta_w', 'delta_pool_scale': 'delta_w', 'delta_odd_w_in': 'delta_w', 'delta_odd_w_out': 'delta_w', 'delta_sconv_w': 'delta_w', 'delta_conf_conv_w': 'delta_w', 'delta_conf_ln_g': 'delta_w', 'delta_conf_ln_b': 'delta_w', 'delta_ffn_w_up': 'delta_w', 'delta_ffn_conv_w': 'delta_w', 'delta_ffn_w_down': 'delta_w', 'new_m_c_ctx': 'new_m', 'new_m_ada_w': 'new_m', 'new_m_ada_b': 'new_m', 'new_m_ln_g': 'new_m', 'new_m_ln_b': 'new_m', 'new_m_even_w_in': 'new_m', 'new_m_even_w_out': 'new_m', 'new_m_gdn_conv_w': 'new_m', 'new_m_gdn_a_log': 'new_m', 'new_m_gdn_dt_bias': 'new_m', 'new_m_gdn_norm_w': 'new_m', 'new_m_pool_w': 'new_m', 'new_m_pool_scale': 'new_m', 'new_m_odd_w_in': 'new_m', 'new_m_odd_w_out': 'new_m', 'new_m_sconv_w': 'new_m', 'new_m_conf_conv_w': 'new_m', 'new_m_conf_ln_g': 'new_m', 'new_m_conf_ln_b': 'new_m', 'new_m_ffn_w_up': 'new_m', 'new_m_ffn_conv_w': 'new_m', 'new_m_ffn_w_down': 'new_m', 'new_v_c_ctx': 'new_v', 'new_v_ada_w': 'new_v', 'new_v_ada_b': 'new_v', 'new_v_ln_g': 'new_v', 'new_v_ln_b': 'new_v', 'new_v_even_w_in': 'new_v', 'new_v_even_w_out': 'new_v', 'new_v_gdn_conv_w': 'new_v', 'new_v_gdn_a_log': 'new_v', 'new_v_gdn_dt_bias': 'new_v', 'new_v_gdn_norm_w': 'new_v', 'new_v_pool_w': 'new_v', 'new_v_pool_scale': 'new_v', 'new_v_odd_w_in': 'new_v', 'new_v_odd_w_out': 'new_v', 'new_v_sconv_w': 'new_v', 'new_v_conf_conv_w': 'new_v', 'new_v_conf_ln_g': 'new_v', 'new_v_conf_ln_b': 'new_v', 'new_v_ffn_w_up': 'new_v', 'new_v_ffn_conv_w': 'new_v', 'new_v_ffn_w_down': 'new_v'}


def _forward(args):
    return _fwd_reference(*[args[k] for k in FWD_PARAMS])


def _output_shape():
    out = _jax.eval_shape(lambda: _forward(_fwd_setup_inputs(0)))
    return out.shape, out.dtype

N_MICROBATCH = 1
ADAM_LR = 0.001
ADAM_B1 = 0.9
ADAM_B2 = 0.999
ADAM_EPS = 1e-08
ADAM_WD = 0.01
ADAM_STEP = 10
PER_EXAMPLE_BATCH_AXIS = {'x': 0, 'c': 0, 'ctx': 0, 'loss_target': 0}
SHARED_INPUTS = []
_WEIGHT_DTYPES = {'c_ctx': _jnp.float32, 'ada_w': _jnp.float32, 'ada_b': _jnp.float32, 'ln_g': _jnp.float32, 'ln_b': _jnp.float32, 'even_w_in': _jnp.float32, 'even_w_out': _jnp.float32, 'gdn_conv_w': _jnp.float32, 'gdn_a_log': _jnp.float32, 'gdn_dt_bias': _jnp.float32, 'gdn_norm_w': _jnp.float32, 'pool_w': _jnp.float32, 'pool_scale': _jnp.float32, 'odd_w_in': _jnp.float32, 'odd_w_out': _jnp.float32, 'sconv_w': _jnp.float32, 'conf_conv_w': _jnp.float32, 'conf_ln_g': _jnp.float32, 'conf_ln_b': _jnp.float32, 'ffn_w_up': _jnp.float32, 'ffn_conv_w': _jnp.float32, 'ffn_w_down': _jnp.float32}
MOMENT_SCALE = {'c_ctx': 7.430907e-04, 'ada_w': 2.271141e-02, 'ada_b': 3.842452e-02, 'ln_g': 1.603767e+01, 'ln_b': 6.454058e-01, 'even_w_in': 1.205015e-02, 'even_w_out': 3.130659e-02, 'gdn_conv_w': 9.532502e-03, 'gdn_a_log': 2.154214e-02, 'gdn_dt_bias': 2.132936e-02, 'gdn_norm_w': 3.412352e-02, 'pool_w': 1.818130e-02, 'pool_scale': 2.001304e-02, 'odd_w_in': 2.020217e-02, 'odd_w_out': 3.981045e-02, 'sconv_w': 2.481945e-02, 'conf_conv_w': 1.230817e-02, 'conf_ln_g': 1.452347e-02, 'conf_ln_b': 1.444242e-02, 'ffn_w_up': 9.652336e-03, 'ffn_conv_w': 9.616776e-03, 'ffn_w_down': 3.153783e-02}


def _to_microbatches(a, axis):
    t = _jnp.moveaxis(a, axis, 0)
    t = t.reshape((N_MICROBATCH, t.shape[0] // N_MICROBATCH) + t.shape[1:])
    return _jnp.moveaxis(t, 1, axis + 1)


def setup_inputs(seed: int = 0) -> dict:
    inp = _fwd_setup_inputs(seed)
    key = _jax.random.fold_in(_jax.random.key(seed), 7919)
    shape, _ = _output_shape()
    out = dict(inp)
    out["loss_target"] = _jax.random.normal(_jax.random.fold_in(key, 0), shape, _jnp.float32)
    for i, name in enumerate(TWIN_WEIGHTS):
        w = inp[name].astype(_jnp.float32)
        if MOMENT_SCALE is None:
            s = _jnp.sqrt(_jnp.mean(_jnp.square(w)) + 1e-30)
        else:
            s = MOMENT_SCALE[name]
        km, kv = _jax.random.split(_jax.random.fold_in(key, i + 1))
        out[name] = w
        out["m_" + name] = s * _jax.random.normal(km, w.shape, _jnp.float32)
        out["v_" + name] = (s * s) * _jax.random.uniform(kv, w.shape, _jnp.float32, 0.5, 1.5)
    if N_MICROBATCH > 1:
        for name, axis in PER_EXAMPLE_BATCH_AXIS.items():
            out[name] = _to_microbatches(out[name], axis)
    return {'x': out['x'], 'c': out['c'], 'ctx': out['ctx'], 'c_ctx': out['c_ctx'], 'ada_w': out['ada_w'], 'ada_b': out['ada_b'], 'ln_g': out['ln_g'], 'ln_b': out['ln_b'], 'even_w_in': out['even_w_in'], 'even_w_out': out['even_w_out'], 'gdn_conv_w': out['gdn_conv_w'], 'gdn_a_log': out['gdn_a_log'], 'gdn_dt_bias': out['gdn_dt_bias'], 'gdn_norm_w': out['gdn_norm_w'], 'pool_w': out['pool_w'], 'pool_scale': out['pool_scale'], 'odd_w_in': out['odd_w_in'], 'odd_w_out': out['odd_w_out'], 'sconv_w': out['sconv_w'], 'conf_conv_w': out['conf_conv_w'], 'conf_ln_g': out['conf_ln_g'], 'conf_ln_b': out['conf_ln_b'], 'ffn_w_up': out['ffn_w_up'], 'ffn_conv_w': out['ffn_conv_w'], 'ffn_w_down': out['ffn_w_down'], 'loss_target': out['loss_target'], 'm_c_ctx': out['m_c_ctx'], 'm_ada_w': out['m_ada_w'], 'm_ada_b': out['m_ada_b'], 'm_ln_g': out['m_ln_g'], 'm_ln_b': out['m_ln_b'], 'm_even_w_in': out['m_even_w_in'], 'm_even_w_out': out['m_even_w_out'], 'm_gdn_conv_w': out['m_gdn_conv_w'], 'm_gdn_a_log': out['m_gdn_a_log'], 'm_gdn_dt_bias': out['m_gdn_dt_bias'], 'm_gdn_norm_w': out['m_gdn_norm_w'], 'm_pool_w': out['m_pool_w'], 'm_pool_scale': out['m_pool_scale'], 'm_odd_w_in': out['m_odd_w_in'], 'm_odd_w_out': out['m_odd_w_out'], 'm_sconv_w': out['m_sconv_w'], 'm_conf_conv_w': out['m_conf_conv_w'], 'm_conf_ln_g': out['m_conf_ln_g'], 'm_conf_ln_b': out['m_conf_ln_b'], 'm_ffn_w_up': out['m_ffn_w_up'], 'm_ffn_conv_w': out['m_ffn_conv_w'], 'm_ffn_w_down': out['m_ffn_w_down'], 'v_c_ctx': out['v_c_ctx'], 'v_ada_w': out['v_ada_w'], 'v_ada_b': out['v_ada_b'], 'v_ln_g': out['v_ln_g'], 'v_ln_b': out['v_ln_b'], 'v_even_w_in': out['v_even_w_in'], 'v_even_w_out': out['v_even_w_out'], 'v_gdn_conv_w': out['v_gdn_conv_w'], 'v_gdn_a_log': out['v_gdn_a_log'], 'v_gdn_dt_bias': out['v_gdn_dt_bias'], 'v_gdn_norm_w': out['v_gdn_norm_w'], 'v_pool_w': out['v_pool_w'], 'v_pool_scale': out['v_pool_scale'], 'v_odd_w_in': out['v_odd_w_in'], 'v_odd_w_out': out['v_odd_w_out'], 'v_sconv_w': out['v_sconv_w'], 'v_conf_conv_w': out['v_conf_conv_w'], 'v_conf_ln_g': out['v_conf_ln_g'], 'v_conf_ln_b': out['v_conf_ln_b'], 'v_ffn_w_up': out['v_ffn_w_up'], 'v_ffn_conv_w': out['v_ffn_conv_w'], 'v_ffn_w_down': out['v_ffn_w_down']}


def _loss(weights, diff, rest, loss_target):
    with _jax.named_scope("forward"):
        args = {**rest, TWIN_DIFF_INPUT: diff, **{k: w.astype(_WEIGHT_DTYPES[k]) for k, w in weights.items()}}
        y = _forward(args)
    with _jax.named_scope("loss_head"):
        err = _jnp.square(y.astype(_jnp.float32) - loss_target)
        return 0.5 * _jnp.sum(_jnp.mean(err, axis=-1)) if err.ndim else 0.5 * err


def _adamw(w, g, m, v):
    m = ADAM_B1 * m + (1.0 - ADAM_B1) * g
    v = ADAM_B2 * v + (1.0 - ADAM_B2) * _jnp.square(g)
    m_hat = m / (1.0 - ADAM_B1 ** ADAM_STEP)
    v_hat = v / (1.0 - ADAM_B2 ** ADAM_STEP)
    delta = -ADAM_LR * (m_hat / (_jnp.sqrt(v_hat) + ADAM_EPS) + ADAM_WD * w)
    return delta, m, v


def reference(x, c, ctx, c_ctx, ada_w, ada_b, ln_g, ln_b, even_w_in, even_w_out, gdn_conv_w, gdn_a_log, gdn_dt_bias, gdn_norm_w, pool_w, pool_scale, odd_w_in, odd_w_out, sconv_w, conf_conv_w, conf_ln_g, conf_ln_b, ffn_w_up, ffn_conv_w, ffn_w_down, loss_target, m_c_ctx, m_ada_w, m_ada_b, m_ln_g, m_ln_b, m_even_w_in, m_even_w_out, m_gdn_conv_w, m_gdn_a_log, m_gdn_dt_bias, m_gdn_norm_w, m_pool_w, m_pool_scale, m_odd_w_in, m_odd_w_out, m_sconv_w, m_conf_conv_w, m_conf_ln_g, m_conf_ln_b, m_ffn_w_up, m_ffn_conv_w, m_ffn_w_down, v_c_ctx, v_ada_w, v_ada_b, v_ln_g, v_ln_b, v_even_w_in, v_even_w_out, v_gdn_conv_w, v_gdn_a_log, v_gdn_dt_bias, v_gdn_norm_w, v_pool_w, v_pool_scale, v_odd_w_in, v_odd_w_out, v_sconv_w, v_conf_conv_w, v_conf_ln_g, v_conf_ln_b, v_ffn_w_up, v_ffn_conv_w, v_ffn_w_down):
    given = dict(x=x, c=c, ctx=ctx, c_ctx=c_ctx, ada_w=ada_w, ada_b=ada_b, ln_g=ln_g, ln_b=ln_b, even_w_in=even_w_in, even_w_out=even_w_out, gdn_conv_w=gdn_conv_w, gdn_a_log=gdn_a_log, gdn_dt_bias=gdn_dt_bias, gdn_norm_w=gdn_norm_w, pool_w=pool_w, pool_scale=pool_scale, odd_w_in=odd_w_in, odd_w_out=odd_w_out, sconv_w=sconv_w, conf_conv_w=conf_conv_w, conf_ln_g=conf_ln_g, conf_ln_b=conf_ln_b, ffn_w_up=ffn_w_up, ffn_conv_w=ffn_conv_w, ffn_w_down=ffn_w_down, loss_target=loss_target, m_c_ctx=m_c_ctx, m_ada_w=m_ada_w, m_ada_b=m_ada_b, m_ln_g=m_ln_g, m_ln_b=m_ln_b, m_even_w_in=m_even_w_in, m_even_w_out=m_even_w_out, m_gdn_conv_w=m_gdn_conv_w, m_gdn_a_log=m_gdn_a_log, m_gdn_dt_bias=m_gdn_dt_bias, m_gdn_norm_w=m_gdn_norm_w, m_pool_w=m_pool_w, m_pool_scale=m_pool_scale, m_odd_w_in=m_odd_w_in, m_odd_w_out=m_odd_w_out, m_sconv_w=m_sconv_w, m_conf_conv_w=m_conf_conv_w, m_conf_ln_g=m_conf_ln_g, m_conf_ln_b=m_conf_ln_b, m_ffn_w_up=m_ffn_w_up, m_ffn_conv_w=m_ffn_conv_w, m_ffn_w_down=m_ffn_w_down, v_c_ctx=v_c_ctx, v_ada_w=v_ada_w, v_ada_b=v_ada_b, v_ln_g=v_ln_g, v_ln_b=v_ln_b, v_even_w_in=v_even_w_in, v_even_w_out=v_even_w_out, v_gdn_conv_w=v_gdn_conv_w, v_gdn_a_log=v_gdn_a_log, v_gdn_dt_bias=v_gdn_dt_bias, v_gdn_norm_w=v_gdn_norm_w, v_pool_w=v_pool_w, v_pool_scale=v_pool_scale, v_odd_w_in=v_odd_w_in, v_odd_w_out=v_odd_w_out, v_sconv_w=v_sconv_w, v_conf_conv_w=v_conf_conv_w, v_conf_ln_g=v_conf_ln_g, v_conf_ln_b=v_conf_ln_b, v_ffn_w_up=v_ffn_w_up, v_ffn_conv_w=v_ffn_conv_w, v_ffn_w_down=v_ffn_w_down)
    weights = {n: given[n] for n in TWIN_WEIGHTS}
    shared = {n: given[n] for n in SHARED_INPUTS}
    per_example = {n: given[n] for n in ['x', 'c', 'ctx']}
    grad_fn = _jax.value_and_grad(_loss, argnums=(0, 1))

    def one_microbatch(ex, loss_target):
        ex = dict(ex)
        diff = ex.pop(TWIN_DIFF_INPUT)
        return grad_fn(weights, diff, {**shared, **ex}, loss_target)

    if N_MICROBATCH == 1:
        loss, (grad_w, grad_x) = one_microbatch(per_example, given["loss_target"])
    else:
        def body(carry, xs):
            loss_sum, grad_sum = carry
            l_k, (gw_k, gx_k) = one_microbatch(xs[0], xs[1])
            with _jax.named_scope("update"):
                return (loss_sum + l_k, _jax.tree.map(_jnp.add, grad_sum, gw_k)), gx_k

        init = (_jnp.zeros((), _jnp.float32), _jax.tree.map(_jnp.zeros_like, weights))
        (loss, grad_w), grad_x = _jax.lax.scan(body, init, (per_example, given["loss_target"]))
    with _jax.named_scope("update"):
        delta_w, new_m, new_v = {}, {}, {}
        for n in TWIN_WEIGHTS:
            delta_w[n], new_m[n], new_v[n] = _adamw(weights[n], grad_w[n], given["m_" + n], given["v_" + n])
    return (loss, grad_x, *[grad_w[n] for n in TWIN_WEIGHTS], *[delta_w[n] for n in TWIN_WEIGHTS],
            *[new_m[n] for n in TWIN_WEIGHTS], *[new_v[n] for n in TWIN_WEIGHTS])
```

```python
import functools
import math

import jax
import jax.numpy as jnp
from jax import lax
from jax.experimental import pallas as pl
from jax.experimental.pallas import tpu as pltpu

F32 = jnp.float32
MXU_DTYPE = jnp.bfloat16
HIGHEST = lax.Precision.HIGHEST
MESH = pl.DeviceIdType.MESH

D_MODEL = 1024
GRID_W = 64
GDN_HEADS = 4
HEAD_DIM = 128
GDN_CHUNK = 64
POOL_WINDOWS = (2, 4, 8, 16)
GDN_CONV = 5
SC_WIDTH = 3
CF_WIDTH = 31
ALPHA = 4.0 ** 0.25
LN_EPS = 1e-5
RMS_EPS = 1e-6
LANES = 128
VMEM_LIMIT_BYTES = 58 * 1024 * 1024

ADAM_LR, ADAM_B1, ADAM_B2, ADAM_EPS, ADAM_WD, ADAM_STEP = 0.001, 0.9, 0.999, 1e-08, 0.01, 10


def _blocked(name, fn, grid, ins, in_specs, out_shapes, out_specs, acc=(), acc_axis=None, scratch=()):
    n_in = len(ins)
    n_out = len(out_shapes)

    def body(*refs):
        vals = [r[...] for r in refs[:n_in]]
        res = fn(*vals, *refs[n_in + n_out:])
        if not isinstance(res, (tuple, list)):
            res = (res,)
        for k, (r, v) in enumerate(zip(refs[n_in:n_in + n_out], res)):
            if k in acc:
                first = pl.program_id(acc_axis) == 0

                @pl.when(first)
                def _(r=r, v=v):
                    r[...] = v.astype(r.dtype)

                @pl.when(jnp.logical_not(first))
                def _(r=r, v=v):
                    r[...] += v.astype(r.dtype)
            else:
                r[...] = v.astype(r.dtype)

    return pl.pallas_call(
        body, name=name, grid=grid, in_specs=in_specs, out_specs=out_specs, out_shape=out_shapes,
        scratch_shapes=list(scratch),
        compiler_params=pltpu.CompilerParams(dimension_semantics=("arbitrary",) * len(grid),
                                             vmem_limit_bytes=VMEM_LIMIT_BYTES),
    )(*ins)


def _sds(shape, dtype=F32):
    return jax.ShapeDtypeStruct(tuple(shape), dtype)


def _tok(tm, n, col=0):
    return pl.BlockSpec((tm, n), lambda t: (t, col))


def _res(shape):
    nd = len(shape)
    return pl.BlockSpec(tuple(shape), lambda t: (0,) * nd)


def _silu(x):
    return x * jax.nn.sigmoid(x)


def _layernorm(r, g, b):
    mu = jnp.mean(r, -1, keepdims=True)
    d = r - mu
    var = jnp.mean(d * d, -1, keepdims=True)
    return d * lax.rsqrt(var + LN_EPS) * g + b


def _mm_nn_impl(a, b):
    return jnp.dot(a.astype(MXU_DTYPE), b.astype(MXU_DTYPE), preferred_element_type=F32)


def _mm_nt_impl(a, b):
    return lax.dot_general(a.astype(MXU_DTYPE), b.astype(MXU_DTYPE), (((1,), (1,)), ((), ())), preferred_element_type=F32)


def _mm_tn_impl(a, b):
    return lax.dot_general(a.astype(MXU_DTYPE), b.astype(MXU_DTYPE), (((0,), (0,)), ((), ())), preferred_element_type=F32)


def _mm_hi_impl(a, b):
    return jnp.dot(a, b, preferred_element_type=F32, precision=HIGHEST)


def _mm_vjp(mm, mm_da, mm_db):
    f = jax.custom_vjp(mm)
    f.defvjp(lambda a, b: (mm(a, b), (a, b)), lambda res, g: (mm_da(g, res[1]), mm_db(res[0], g)))
    return f


_mm = _mm_vjp(_mm_nn_impl, lambda g, b: _mm_nt_impl(g, b), lambda a, g: _mm_tn_impl(a, g))
_mm_nt = _mm_vjp(_mm_nt_impl, lambda g, b: _mm_nn_impl(g, b), lambda a, g: _mm_tn_impl(g, a))
_mm_tn = _mm_vjp(_mm_tn_impl, lambda g, b: _mm_nt_impl(b, g), lambda a, g: _mm_nn_impl(a, g))
_mm_hi = _mm_vjp(
    _mm_hi_impl,
    lambda g, b: lax.dot_general(g, b, (((1,), (1,)), ((), ())), preferred_element_type=F32, precision=HIGHEST),
    lambda a, g: lax.dot_general(a, g, (((0,), (0,)), ((), ())), preferred_element_type=F32, precision=HIGHEST))


def _row(w, k):
    rows = lax.broadcasted_iota(jnp.int32, w.shape, 0)
    return jnp.sum(jnp.where(rows == k, w, 0.0), axis=0, keepdims=True)


def _col_mask(shape, dc):
    col = lax.broadcasted_iota(jnp.int32, shape, 0) & (GRID_W - 1)
    return (col + dc >= 0) & (col + dc < GRID_W)


def _center(x_ext, halo):
    return x_ext[halo:x_ext.shape[0] - halo]


def _make_dwconv(taps, halo):
    assert all(abs(s) <= halo for s, _ in taps)

    def shifted(x_ext, s):
        r = x_ext if s == 0 else pltpu.roll(x_ext, (-s) % x_ext.shape[0], 0)
        return _center(r, halo)

    @jax.custom_vjp
    def conv(x_ext, w):
        acc = None
        for k, (s, dc) in enumerate(taps):
            r = shifted(x_ext, s)
            if dc != 0:
                r = jnp.where(_col_mask(r.shape, dc), r, 0.0)
            term = r * _row(w, k)
            acc = term if acc is None else acc + term
        return acc

    def fwd(x_ext, w):
        return conv(x_ext, w), (x_ext, w)

    def bwd(res, dy):
        x_ext, w = res
        n = x_ext.shape[0]
        rows = lax.broadcasted_iota(jnp.int32, w.shape, 0)
        pad = jnp.zeros((halo, dy.shape[1]), F32)
        dx = None
        dw = jnp.zeros(w.shape, F32)
        for k, (s, dc) in enumerate(taps):
            dym = dy if dc == 0 else jnp.where(_col_mask(dy.shape, dc), dy, 0.0)
            dw = dw + jnp.where(rows == k, jnp.sum(dym * shifted(x_ext, s), axis=0, keepdims=True), 0.0)
            t = jnp.concatenate([pad, dym * _row(w, k), pad], axis=0)
            if s != 0:
                t = pltpu.roll(t, s % n, 0)
            dx = t if dx is None else dx + t
        return dx, dw

    conv.defvjp(fwd, bwd)
    return conv


def _taps_1d(width):
    return tuple((k - width // 2, 0) for k in range(width))


HALO_SHORT = 8
HALO_CONF = 16
HALO_GRID = 72
_conv5 = _make_dwconv(_taps_1d(GDN_CONV), HALO_SHORT)
_conv3 = _make_dwconv(_taps_1d(SC_WIDTH), HALO_SHORT)
_conv31 = _make_dwconv(_taps_1d(CF_WIDTH), HALO_CONF)
_conv3x3 = _make_dwconv(tuple((dr * GRID_W + dc, dc) for dr in (-1, 0, 1) for dc in (-1, 0, 1)), HALO_GRID)
_pool_sums = {w: _make_dwconv(tuple((s, 0) for s in range(-(w // 2), w - w // 2)), HALO_SHORT) for w in POOL_WINDOWS}


def _all_gather8(name, blk, in_vmem):
    m_per, n = blk.shape
    space = pltpu.VMEM if in_vmem else pl.ANY

    def body(x_ref, out_ref, send_sems, recv_sems, local_sem):
        x, y, c = lax.axis_index("x"), lax.axis_index("y"), lax.axis_index("c")
        me, sibling = (x, y, c), (x, y, 1 - c)
        chips = [(1 - x, y), (x, 1 - y), (1 - x, 1 - y)]

        def rows(px, py, pc):
            return out_ref.at[pl.ds((4 * px + 2 * py + pc) * m_per, m_per), :]

        def copy(k, block, to, src=None):
            return pltpu.make_async_remote_copy(
                src_ref=rows(*block) if src is None else src, dst_ref=rows(*block),
                send_sem=send_sems.at[k], recv_sem=recv_sems.at[k], device_id=to, device_id_type=MESH)

        mine = pltpu.make_async_copy(x_ref, rows(*me), local_sem)
        mine.start()
        first = [copy(0, me, sibling, src=x_ref)]
        first += [copy(1 + j, me, (*chip, c), src=x_ref) for j, chip in enumerate(chips)]
        for cp in first:
            cp.start()
        passed = [copy(4 + j, (*chip, c), sibling) for j, chip in enumerate(chips)]
        for j, chip in enumerate(chips):
            copy(1 + j, (*chip, c), me).wait_recv()
            passed[j].start()
        copy(0, sibling, me).wait_recv()
        for j, chip in enumerate(chips):
            copy(4 + j, (*chip, 1 - c), me).wait_recv()
        for cp in first + passed:
            cp.wait_send()
        mine.wait()

    return pl.pallas_call(
        body, name=name, out_shape=_sds((8 * m_per, n), blk.dtype),
        in_specs=[pl.BlockSpec(memory_space=space)], out_specs=pl.BlockSpec(memory_space=space),
        scratch_shapes=[pltpu.SemaphoreType.DMA((7,)), pltpu.SemaphoreType.DMA((7,)), pltpu.SemaphoreType.DMA],
        compiler_params=pltpu.CompilerParams(vmem_limit_bytes=VMEM_LIMIT_BYTES),
    )(blk)


def _sibling_exchange(name, send):
    def body(s_ref, out_ref, send_sem, recv_sem):
        x, y, c = lax.axis_index("x"), lax.axis_index("y"), lax.axis_index("c")
        cp = pltpu.make_async_remote_copy(src_ref=s_ref, dst_ref=out_ref, send_sem=send_sem, recv_sem=recv_sem,
                                          device_id=(x, y, 1 - c), device_id_type=MESH)
        cp.start()
        cp.wait()

    return pl.pallas_call(
        body, name=name, out_shape=_sds(send.shape, send.dtype),
        in_specs=[pl.BlockSpec(memory_space=pl.ANY)], out_specs=pl.BlockSpec(memory_space=pl.ANY),
        scratch_shapes=[pltpu.SemaphoreType.DMA, pltpu.SemaphoreType.DMA],
    )(send)


def _chip_scatter(name, part):
    _, r, n = part.shape

    def body(p_ref, out_ref, send_sems, recv_sems, local_sem):
        x, y, c = lax.axis_index("x"), lax.axis_index("y"), lax.axis_index("c")
        sibling = (x, y, 1 - c)
        me_chip = 2 * x + y
        chips = [(1 - x, y), (x, 1 - y), (1 - x, 1 - y)]

        def chip_id(chip):
            return 2 * chip[0] + chip[1]

        def copy(k, src, dst, to):
            return pltpu.make_async_remote_copy(src_ref=src, dst_ref=dst, send_sem=send_sems.at[k],
                                                recv_sem=recv_sems.at[k], device_id=to, device_id_type=MESH)

        mine = pltpu.make_async_copy(p_ref.at[me_chip], out_ref.at[c, me_chip], local_sem)
        mine.start()
        first = [copy(0, p_ref.at[me_chip], out_ref.at[c, me_chip], sibling)]
        first += [copy(1 + j, p_ref.at[chip_id(chip)], out_ref.at[c, me_chip], (*chip, c))
                  for j, chip in enumerate(chips)]
        for cp in first:
            cp.start()
        passed = [copy(4 + j, out_ref.at[c, chip_id(chip)], out_ref.at[c, chip_id(chip)], sibling)
                  for j, chip in enumerate(chips)]
        for j, chip in enumerate(chips):
            copy(1 + j, p_ref.at[0], out_ref.at[c, chip_id(chip)], sibling).wait_recv()
            passed[j].start()
        copy(0, p_ref.at[0], out_ref.at[1 - c, me_chip], sibling).wait_recv()
        for j, chip in enumerate(chips):
            copy(4 + j, p_ref.at[0], out_ref.at[1 - c, chip_id(chip)], sibling).wait_recv()
        for cp in first + passed:
            cp.wait_send()
        mine.wait()

    return pl.pallas_call(
        body, name=name, out_shape=_sds((2, 4, r, n), part.dtype),
        in_specs=[pl.BlockSpec(memory_space=pl.ANY)], out_specs=pl.BlockSpec(memory_space=pl.ANY),
        scratch_shapes=[pltpu.SemaphoreType.DMA((7,)), pltpu.SemaphoreType.DMA((7,)), pltpu.SemaphoreType.DMA],
    )(part)


def _pack_flat(arrays, multiple, dtype):
    flat = jnp.concatenate([a.reshape(-1).astype(dtype) for a in arrays])
    pad = (-flat.shape[0]) % multiple
    if pad:
        flat = jnp.concatenate([flat, jnp.zeros((pad,), dtype)])
    return flat


def _unpack_flat(flat, shapes):
    out, off = [], 0
    for s in shapes:
        n = math.prod(s)
        out.append(flat[off:off + n].reshape(s))
        off += n
    return out


def _row_tile(r, cap, mult=16):
    for cand in range(min(cap, r) // mult * mult, 0, -mult):
        if r % cand == 0:
            return cand
    return r


def _sum_leading(name, stack, out_dtype=F32):
    k, r, n = stack.shape
    tr = _row_tile(r, 2048)

    def fn(s):
        acc = s[0].astype(F32)
        for i in range(1, k):
            acc = acc + s[i].astype(F32)
        return acc

    return _blocked(name, fn, (r // tr,), [stack], [pl.BlockSpec((k, tr, n), lambda t: (0, t, 0))],
                    [_sds((r, n), out_dtype)], [pl.BlockSpec((tr, n), lambda t: (t, 0))])[0]


def _adamw(name, w, g, m, v):
    r, c = w.shape
    tr = _row_tile(r, max(8, (1 << 21) // (4 * c)), 8)
    bc1 = 1.0 - ADAM_B1 ** ADAM_STEP
    bc2 = 1.0 - ADAM_B2 ** ADAM_STEP

    def fn(w, g, m, v):
        nm = ADAM_B1 * m + (1.0 - ADAM_B1) * g
        nv = ADAM_B2 * v + (1.0 - ADAM_B2) * (g * g)
        delta = -ADAM_LR * ((nm / bc1) / (jnp.sqrt(nv / bc2) + ADAM_EPS) + ADAM_WD * w)
        return delta, nm, nv

    spec = pl.BlockSpec((tr, c), lambda t: (t, 0))
    return _blocked(name, fn, (r // tr,), [w, g, m, v], [spec] * 4, [_sds((r, c))] * 3, [spec] * 3)


def _mod_rows(c16, ada_w):
    depth, d, n = ada_w.shape

    def fn(c, w):
        return _mm(_silu(c), w)

    return _blocked("mod_rows", fn, (depth,), [c16, ada_w],
                    [_res(c16.shape), pl.BlockSpec((None, d, n), lambda l: (l, 0, 0))],
                    [_sds((depth, 16, n))], [pl.BlockSpec((None, 16, n), lambda l: (l, 0, 0))])[0]


def _ada_grads(c16, dmod16, ada_w):
    depth, d, n = ada_w.shape

    def fn(c, dm, w):
        return _mm_tn(_silu(c), dm), _mm_nt(dm, w)

    return _blocked("ada_grads", fn, (depth,), [c16, dmod16, ada_w],
                    [_res(c16.shape), pl.BlockSpec((None, 16, n), lambda l: (l, 0, 0)),
                     pl.BlockSpec((None, d, n), lambda l: (l, 0, 0))],
                    [_sds((depth, d, n)), _sds((depth, 16, d))],
                    [pl.BlockSpec((None, d, n), lambda l: (l, 0, 0)), pl.BlockSpec((None, 16, d), lambda l: (l, 0, 0))])


def _cctx_grad(parts, c_ctx_row):
    def fn(p, c):
        tot = ((p[0] + p[1]) + p[2]) + p[3]
        _, vjp = jax.vjp(_silu, c)
        return vjp(tot)[0]

    return _blocked("cctx_grad", fn, (1,), [parts, c_ctx_row], [_res(parts.shape), _res(c_ctx_row.shape)],
                    [_sds(c_ctx_row.shape)], [_res(c_ctx_row.shape)])[0]


def _modulate(x, sh, sc):
    return x * (1.0 + sc) + sh


def _inproj(name, x, sh, sc, w, tm):
    t, d = x.shape
    n = w.shape[1]
    tm = min(tm, t)

    def fn(x, sh, sc, w):
        u = _modulate(x, sh, sc).astype(MXU_DTYPE)
        return jnp.dot(u, w, preferred_element_type=F32), u

    return _blocked(name, fn, (t // tm,), [x, sh, sc, w],
                    [_tok(tm, d), _res(sh.shape), _res(sc.shape), _res(w.shape)],
                    [_sds((t, n)), _sds((t, d), MXU_DTYPE)], [_tok(tm, n), _tok(tm, d)])


def _inproj_bwd(name, dp, x, sh, sc, w, add, tm):
    t, d = x.shape
    n = w.shape[1]
    tm = min(tm, t)
    has_add = add is not None

    def fn(dp, x, sh, sc, w, *rest):
        du = _mm_nt(dp, w)
        _, vjp = jax.vjp(_modulate, x, sh, sc)
        dx, dsh, dsc = vjp(du)
        if has_add:
            dx = dx + rest[0]
        return dx, dsh, dsc

    ins = [dp, x, sh, sc, w] + ([add] if has_add else [])
    specs = [_tok(tm, n), _tok(tm, d), _res(sh.shape), _res(sc.shape), _res(w.shape)] + ([_tok(tm, d)] if has_add else [])
    return _blocked(name, fn, (t // tm,), ins, specs,
                    [_sds((t, d)), _sds(sh.shape), _sds(sc.shape)], [_tok(tm, d), _res(sh.shape), _res(sc.shape)],
                    acc=(1, 2), acc_axis=0)


def _residual_ln(y, x, gt, lng, lnb):
    return _layernorm(ALPHA * x + gt * y, lng, lnb)


def _gdn_mix(o2, pg, yp, normw):
    o = o2[0] + o2[1]
    heads = []
    for h in range(GDN_HEADS):
        oh = o[:, h * HEAD_DIM:(h + 1) * HEAD_DIM]
        heads.append(oh * lax.rsqrt(jnp.mean(oh * oh, -1, keepdims=True) + RMS_EPS) * normw)
    on = jnp.concatenate(heads, axis=-1) * _silu(pg)
    return jnp.concatenate([on, yp], axis=-1)


def _even_out(o, p, ypool, x, normw, gt, lng, lnb, w, tm):
    t, d = x.shape
    tm = min(tm, t)
    gate_blk = 3

    def fn(o2, pg, yp, x, normw, gt, lng, lnb, w):
        mix = _gdn_mix(o2, pg, yp, normw).astype(MXU_DTYPE)
        y = jnp.dot(mix, w, preferred_element_type=F32)
        return _residual_ln(y, x, gt, lng, lnb), mix, y

    return _blocked("even_out", fn, (t // tm,), [o, p, ypool, x, normw, gt, lng, lnb, w],
                    [pl.BlockSpec((2, tm, 512), lambda i: (0, i, 0)), _tok(tm, 512, gate_blk), _tok(tm, 512), _tok(tm, d),
                     _res(normw.shape), _res(gt.shape), _res(lng.shape), _res(lnb.shape), _res(w.shape)],
                    [_sds((t, d)), _sds((t, d), MXU_DTYPE), _sds((t, d))], [_tok(tm, d)] * 3)


def _even_out_bwd(dx1, o, p, ypool, x, y, normw, gt, lng, lnb, w, tm):
    t, d = x.shape
    tm = min(tm, t)

    def fn(dx1, o2, pg, yp, x, y, normw, gt, lng, lnb, w):
        _, vjp2 = jax.vjp(_residual_ln, y, x, gt, lng, lnb)
        dy, dx, dgt, dlng, dlnb = vjp2(dx1)
        dyb = dy.astype(MXU_DTYPE)
        dmix = _mm_nt(dyb, w)
        _, vjp1 = jax.vjp(_gdn_mix, o2, pg, yp, normw)
        do2, dpg, dyp, dnormw = vjp1(dmix)
        return do2[0], dpg, dyp, dx, dyb, dnormw, dgt, dlng, dlnb

    return _blocked("even_out_bwd", fn, (t // tm,), [dx1, o, p, ypool, x, y, normw, gt, lng, lnb, w],
                    [_tok(tm, d), pl.BlockSpec((2, tm, 512), lambda i: (0, i, 0)), _tok(tm, 512, 3), _tok(tm, 512),
                     _tok(tm, d), _tok(tm, d), _res(normw.shape), _res(gt.shape), _res(lng.shape), _res(lnb.shape),
                     _res(w.shape)],
                    [_sds((t, 512)), _sds((t, 512), MXU_DTYPE), _sds((t, 512)), _sds((t, d)), _sds((t, d), MXU_DTYPE),
                     _sds(normw.shape), _sds(gt.shape), _sds(lng.shape), _sds(lnb.shape)],
                    [_tok(tm, 512), _tok(tm, 512), _tok(tm, 512), _tok(tm, d), _tok(tm, d),
                     _res(normw.shape), _res(gt.shape), _res(lng.shape), _res(lnb.shape)],
                    acc=(5, 6, 7, 8), acc_axis=0)


def _odd_mix(ysc, z0, cg, cb):
    z = _silu(_layernorm(z0, cg, cb))
    return jnp.concatenate([ysc, z], axis=-1)


def _odd_out(ysc, z0, x, cg, cb, gt, lng, lnb, w, tm):
    t, d = x.shape
    tm = min(tm, t)

    def fn(ysc, z0, x, cg, cb, gt, lng, lnb, w):
        mix = _odd_mix(ysc, z0, cg, cb).astype(MXU_DTYPE)
        y = jnp.dot(mix, w, preferred_element_type=F32)
        return _residual_ln(y, x, gt, lng, lnb), mix, y

    return _blocked("odd_out", fn, (t // tm,), [ysc, z0, x, cg, cb, gt, lng, lnb, w],
                    [_tok(tm, 512), _tok(tm, 512), _tok(tm, d), _res(cg.shape), _res(cb.shape), _res(gt.shape),
                     _res(lng.shape), _res(lnb.shape), _res(w.shape)],
                    [_sds((t, d)), _sds((t, d), MXU_DTYPE), _sds((t, d))], [_tok(tm, d)] * 3)


def _odd_out_bwd(dx3, ysc, z0, x, y, cg, cb, gt, lng, lnb, w, tm):
    t, d = x.shape
    tm = min(tm, t)

    def fn(dx3, ysc, z0, x, y, cg, cb, gt, lng, lnb, w):
        _, vjp2 = jax.vjp(_residual_ln, y, x, gt, lng, lnb)
        dy, dx, dgt, dlng, dlnb = vjp2(dx3)
        dyb = dy.astype(MXU_DTYPE)
        dmix = _mm_nt(dyb, w)
        _, vjp1 = jax.vjp(_odd_mix, ysc, z0, cg, cb)
        dysc, dz0, dcg, dcb = vjp1(dmix)
        return dysc, dz0, dx, dyb, dcg, dcb, dgt, dlng, dlnb

    return _blocked("odd_out_bwd", fn, (t // tm,), [dx3, ysc, z0, x, y, cg, cb, gt, lng, lnb, w],
                    [_tok(tm, d), _tok(tm, 512), _tok(tm, 512), _tok(tm, d), _tok(tm, d), _res(cg.shape), _res(cb.shape),
                     _res(gt.shape), _res(lng.shape), _res(lnb.shape), _res(w.shape)],
                    [_sds((t, 512)), _sds((t, 512)), _sds((t, d)), _sds((t, d), MXU_DTYPE),
                     _sds(cg.shape), _sds(cb.shape), _sds(gt.shape), _sds(lng.shape), _sds(lnb.shape)],
                    [_tok(tm, 512), _tok(tm, 512), _tok(tm, d), _tok(tm, d),
                     _res(cg.shape), _res(cb.shape), _res(gt.shape), _res(lng.shape), _res(lnb.shape)],
                    acc=(4, 5, 6, 7, 8), acc_axis=0)


def _ffn_down(name, s, x, gt, lng, lnb, w, tm):
    t, d = x.shape
    f = s.shape[1]
    tm = min(tm, t)

    def fn(s, x, gt, lng, lnb, w):
        y = jnp.dot(s, w, preferred_element_type=F32)
        return _residual_ln(y, x, gt, lng, lnb), y

    return _blocked(name, fn, (t // tm,), [s, x, gt, lng, lnb, w],
                    [_tok(tm, f), _tok(tm, d), _res(gt.shape), _res(lng.shape), _res(lnb.shape), _res(w.shape)],
                    [_sds((t, d)), _sds((t, d))], [_tok(tm, d)] * 2)


def _ffn_down_bwd(name, dx2, x, y, gt, lng, lnb, w, tm):
    t, d = x.shape
    f = w.shape[0]
    tm = min(tm, t)

    def fn(dx2, x, y, gt, lng, lnb, w):
        _, vjp2 = jax.vjp(_residual_ln, y, x, gt, lng, lnb)
        dy, dx, dgt, dlng, dlnb = vjp2(dx2)
        dyb = dy.astype(MXU_DTYPE)
        return _mm_nt(dyb, w), dx, dyb, dgt, dlng, dlnb

    return _blocked(name, fn, (t // tm,), [dx2, x, y, gt, lng, lnb, w],
                    [_tok(tm, d), _tok(tm, d), _tok(tm, d), _res(gt.shape), _res(lng.shape), _res(lnb.shape), _res(w.shape)],
                    [_sds((t, f), MXU_DTYPE), _sds((t, d)), _sds((t, d), MXU_DTYPE), _sds(gt.shape), _sds(lng.shape), _sds(lnb.shape)],
                    [_tok(tm, f), _tok(tm, d), _tok(tm, d), _res(gt.shape), _res(lng.shape), _res(lnb.shape)],
                    acc=(3, 4, 5), acc_axis=0)


def _loss_and_grad(x4, target, tm):
    t, d = x4.shape
    tm = min(tm, t)

    def fn(y, tg):
        e = y - tg
        part = 0.5 * jnp.sum(jnp.mean(e * e, axis=-1, keepdims=True), axis=0, keepdims=True)
        return jnp.broadcast_to(part, (1, LANES)), e * (1.0 / d)

    return _blocked("loss", fn, (t // tm,), [x4, target], [_tok(tm, d), _tok(tm, d)],
                    [_sds((1, LANES)), _sds((t, d))], [_res((1, LANES)), _tok(tm, d)], acc=(0,), acc_axis=0)


def _matmul_tn(name, a, b):
    t, k = a.shape
    n = b.shape[1]
    tt = math.gcd(t, 512)
    tn = 512 if n % 512 == 0 else (384 if n % 384 == 0 else 128)

    def body(a_ref, b_ref, o_ref):
        @pl.when(pl.program_id(1) == 0)
        def _():
            o_ref[...] = jnp.zeros(o_ref.shape, F32)

        o_ref[...] += lax.dot_general(a_ref[...], b_ref[...], (((0,), (0,)), ((), ())), preferred_element_type=F32)

    return pl.pallas_call(
        body, name=name, grid=(n // tn, t // tt),
        in_specs=[pl.BlockSpec((tt, k), lambda j, i: (i, 0)), pl.BlockSpec((tt, tn), lambda j, i: (i, j))],
        out_specs=pl.BlockSpec((k, tn), lambda j, i: (0, j)), out_shape=_sds((k, n)),
        compiler_params=pltpu.CompilerParams(dimension_semantics=("arbitrary", "arbitrary"),
                                             vmem_limit_bytes=VMEM_LIMIT_BYTES),
    )(a, b)


def _chan(t, col_of):
    return pl.BlockSpec((t, LANES), lambda i: (0, col_of(i)))


def _wblk(k, col_of=lambda i: i):
    return pl.BlockSpec((k, LANES), lambda i: (0, col_of(i)))


SEQ_TILE = 256


def _load_ext(ref, r0, rows, halo, total):
    lo, hi = max(r0 - halo, 0), min(r0 + rows + halo, total)
    parts = []
    if lo > r0 - halo:
        parts.append(jnp.zeros((lo - (r0 - halo), ref.shape[1]), F32))
    parts.append(ref[lo:hi, :].astype(F32))
    if hi < r0 + rows + halo:
        parts.append(jnp.zeros((r0 + rows + halo - hi, ref.shape[1]), F32))
    return parts[0] if len(parts) == 1 else jnp.concatenate(parts, axis=0)


def _seq_stage(name, tile_fn, halo, grid_n, seqs, pars, outs):
    total = seqs[0][0].shape[0]
    rows = min(SEQ_TILE, total)
    ns, npar = len(seqs), len(pars)

    def body(*refs):
        seq_refs, par_refs, out_refs = refs[:ns], refs[ns:ns + npar], refs[ns + npar:]
        par_vals = [r[...] for r in par_refs]
        for r0 in range(0, total, rows):
            exts = [_load_ext(r, r0, rows, halo, total) for r in seq_refs]
            res = tile_fn(r0, *exts, *par_vals)
            for o_ref, v in zip(out_refs, res):
                o_ref[r0:r0 + rows, :] = v.astype(o_ref.dtype)

    return pl.pallas_call(
        body, name=name, grid=(grid_n,), in_specs=[s for _, s in seqs] + [s for _, s in pars],
        out_specs=[s for _, s in outs], out_shape=[o for o, _ in outs],
        compiler_params=pltpu.CompilerParams(dimension_semantics=("arbitrary",), vmem_limit_bytes=VMEM_LIMIT_BYTES),
    )(*[a for a, _ in seqs], *[a for a, _ in pars])


def _seq_stage_bwd(name, tile_fn, halo, grid_n, douts, seqs, pars, dseq_outs, dpar_outs):
    total = seqs[0][0].shape[0]
    rows = min(SEQ_TILE, total)
    nd, ns, npar = len(douts), len(seqs), len(pars)
    widths = [s.block_shape[-1] for _, s in seqs]

    def dtile(d, r0):
        if len(d.shape) == 2:
            return d[r0:r0 + rows, :].astype(F32)
        return functools.reduce(lambda a, b: a + b, [d[k, r0:r0 + rows, :].astype(F32) for k in range(d.shape[0])])

    def body(*refs):
        dout_refs, seq_refs, par_refs = refs[:nd], refs[nd:nd + ns], refs[nd + ns:nd + ns + npar]
        dseq_refs = refs[nd + ns + npar:nd + 2 * ns + npar]
        dpar_refs = refs[nd + 2 * ns + npar:nd + 2 * ns + 2 * npar]
        accs = refs[nd + 2 * ns + 2 * npar:]
        par_vals = [r[...] for r in par_refs]
        for a in accs:
            for r0 in range(0, total, rows):
                a[r0:r0 + rows, :] = jnp.zeros((rows, a.shape[1]), F32)
        dpars = [jnp.zeros(p.shape, F32) for p in par_vals]
        for r0 in range(0, total, rows):
            exts = [_load_ext(r, r0, rows, halo, total) for r in seq_refs]
            _, vjp = jax.vjp(functools.partial(tile_fn, r0), *exts, *par_vals)
            cts = vjp(tuple(dtile(d, r0) for d in dout_refs))
            lo, hi = max(r0 - halo, 0), min(r0 + rows + halo, total)
            for a, dx in zip(accs, cts[:ns]):
                a[lo:hi, :] += dx[lo - (r0 - halo):hi - (r0 - halo)]
            dpars = [acc + g for acc, g in zip(dpars, cts[ns:])]
        for o_ref, a in zip(dseq_refs, accs):
            for r0 in range(0, total, rows):
                o_ref[r0:r0 + rows, :] = a[r0:r0 + rows, :].astype(o_ref.dtype)
        for o_ref, g in zip(dpar_refs, dpars):
            o_ref[...] = g

    return pl.pallas_call(
        body, name=name, grid=(grid_n,),
        in_specs=[s for _, s in douts] + [s for _, s in seqs] + [s for _, s in pars],
        out_specs=[s for _, s in dseq_outs] + [s for _, s in dpar_outs],
        out_shape=[o for o, _ in dseq_outs] + [o for o, _ in dpar_outs],
        scratch_shapes=[pltpu.VMEM((total, w), F32) for w in widths],
        compiler_params=pltpu.CompilerParams(dimension_semantics=("arbitrary",), vmem_limit_bytes=VMEM_LIMIT_BYTES),
    )(*[a for a, _ in douts], *[a for a, _ in seqs], *[a for a, _ in pars])


def _qkv_tile(r0, p_ext, w):
    kind = pl.program_id(0)
    a = _silu(_conv5(p_ext, w))
    nrm = a * lax.rsqrt(jnp.sum(a * a, -1, keepdims=True) + RMS_EPS)
    scale = jnp.where(kind < GDN_HEADS, HEAD_DIM ** -0.5, 1.0).astype(F32)
    sel = (kind < 2 * GDN_HEADS).astype(F32)
    return (sel * (nrm * scale) + (1.0 - sel) * a,)


def _qkv_conv(name, p, w):
    t = p.shape[0]
    nb = w.shape[1] // LANES
    ident = lambda i: i
    return _seq_stage(name, _qkv_tile, HALO_SHORT, nb, [(p, _chan(t, ident))], [(w, _wblk(GDN_CONV))],
                      [(_sds((t, nb * LANES)), _chan(t, ident))])[0]


def _qkv_conv_bwd(name, dqkv, p, w):
    t = p.shape[0]
    nb = w.shape[1] // LANES
    ident = lambda i: i
    both_dirs = pl.BlockSpec((2, t, LANES), lambda i: (0, 0, i))
    return _seq_stage_bwd(name, _qkv_tile, HALO_SHORT, nb, [(dqkv, both_dirs)], [(p, _chan(t, ident))],
                          [(w, _wblk(GDN_CONV))], [(_sds((t, nb * LANES), MXU_DTYPE), _chan(t, ident))],
                          [(_sds(w.shape), _wblk(GDN_CONV))])


def _gates_fn(s, avec, dtvec):
    lane = lax.broadcasted_iota(jnp.int32, s.shape, 1)
    beta = jax.nn.sigmoid(s)
    g = -jnp.exp(avec) * jax.nn.softplus(s + dtvec)
    nh = 2 * GDN_HEADS
    return jnp.where(lane < nh, beta, jnp.where(lane < 2 * nh, g, 0.0))


def _gates(name, p, avec, dtvec, col):
    t = p.shape[0]
    tm = min(512, t)
    return _blocked(name, _gates_fn, (t // tm,), [p, avec, dtvec], [_tok(tm, LANES, col), _res(avec.shape), _res(dtvec.shape)],
                    [_sds((t, LANES))], [_tok(tm, LANES)])[0]


def _gates_bwd(name, dgb, p, avec, dtvec, col):
    t = p.shape[0]
    tm = min(512, t)

    def fn(dgb, s, avec, dtvec):
        _, vjp = jax.vjp(_gates_fn, s, avec, dtvec)
        return vjp(dgb)

    return _blocked(name, fn, (t // tm,), [dgb, p, avec, dtvec],
                    [_tok(tm, LANES), _tok(tm, LANES, col), _res(avec.shape), _res(dtvec.shape)],
                    [_sds((t, LANES), MXU_DTYPE), _sds(avec.shape), _sds(dtvec.shape)],
                    [_tok(tm, LANES), _res(avec.shape), _res(dtvec.shape)], acc=(1, 2), acc_axis=0)


def _make_pool_tile(total):
    def tile(r0, x_ext, scale, *ws):
        ys = []
        for g, win in enumerate(POOL_WINDOWS):
            xg = x_ext[:, g * LANES:(g + 1) * LANES]
            rows = xg.shape[0] - 2 * HALO_SHORT
            pos = r0 + lax.broadcasted_iota(jnp.int32, (rows, LANES), 0)
            lo = jnp.clip(pos - win // 2, 0, total)
            hi = jnp.clip(pos - win // 2 + win, 0, total)
            window_sum = _pool_sums[win](xg, jnp.ones((win, LANES), F32))
            pooled = window_sum / (hi - lo).astype(F32) - _center(xg, HALO_SHORT)
            ys.append(_mm(pooled, ws[g]))
        return (jnp.concatenate(ys, axis=-1) * scale,)

    return tile


def _pool_specs(p, pool_w, pool_scale, col0):
    t = p.shape[0]
    n = len(POOL_WINDOWS) * LANES
    seq = (p, pl.BlockSpec((t, n), lambda i: (0, col0)))
    pars = [(pool_scale, _res(pool_scale.shape))]
    pars += [(pool_w, pl.BlockSpec((None, LANES, LANES), lambda i, g=g: (g, 0, 0))) for g in range(len(POOL_WINDOWS))]
    return t, n, seq, pars


def _pool(p, pool_w, pool_scale, col0):
    t, n, seq, pars = _pool_specs(p, pool_w, pool_scale, col0)
    return _seq_stage("pool", _make_pool_tile(t), HALO_SHORT, 1, [seq], pars, [(_sds((t, n)), _res((t, n)))])[0]


def _pool_bwd(dy, p, pool_w, pool_scale, col0):
    t, n, seq, pars = _pool_specs(p, pool_w, pool_scale, col0)
    wspec = (_sds((LANES, LANES)), _res((LANES, LANES)))
    return _seq_stage_bwd("pool_bwd", _make_pool_tile(t), HALO_SHORT, 1, [(dy, _res(dy.shape))], [seq], pars,
                          [(_sds((t, n), MXU_DTYPE), _res((t, n)))],
                          [(_sds(pool_scale.shape), _res(pool_scale.shape))] + [wspec] * len(POOL_WINDOWS))


def _ffn_conv_tile(r0, a_ext, gt_ext, w):
    return (_silu(_conv3x3(a_ext, w)) * _center(gt_ext, HALO_GRID),)


def _ffn_conv(name, h, w9):
    t = h.shape[0]
    nb = w9.shape[1] // LANES
    ident = lambda i: i
    return _seq_stage(name, _ffn_conv_tile, HALO_GRID, nb, [(h, _chan(t, ident)), (h, _chan(t, lambda i: nb + i))],
                      [(w9, _wblk(9))], [(_sds((t, nb * LANES), MXU_DTYPE), _chan(t, ident))])[0]


def _ffn_conv_bwd(name, ds, h, w9):
    t = h.shape[0]
    nb = w9.shape[1] // LANES
    ident = lambda i: i
    o = (_sds((t, nb * LANES), MXU_DTYPE), _chan(t, ident))
    return _seq_stage_bwd(name, _ffn_conv_tile, HALO_GRID, nb, [(ds, _chan(t, ident))],
                          [(h, _chan(t, ident)), (h, _chan(t, lambda i: nb + i))], [(w9, _wblk(9))],
                          [o, o], [(_sds(w9.shape), _wblk(9))])


def _sconv_tile(r0, gb_ext, gc_ext, h_ext, w):
    return (_center(gb_ext, HALO_SHORT) * _conv3(gc_ext * h_ext, w),)


def _sconv_specs(p1, w3):
    t = p1.shape[0]
    nb = w3.shape[1] // LANES
    seqs = [(p1, _chan(t, lambda i: i)), (p1, _chan(t, lambda i: nb + i)), (p1, _chan(t, lambda i: 2 * nb + i))]
    return t, nb, seqs, [(w3, _wblk(SC_WIDTH))]


def _sconv(p1, w3):
    t, nb, seqs, pars = _sconv_specs(p1, w3)
    return _seq_stage("sconv", _sconv_tile, HALO_SHORT, nb, seqs, pars, [(_sds((t, nb * LANES)), _chan(t, lambda i: i))])[0]


def _sconv_bwd(dysc, p1, w3):
    t, nb, seqs, pars = _sconv_specs(p1, w3)
    o = (_sds((t, nb * LANES), MXU_DTYPE), _chan(t, lambda i: i))
    return _seq_stage_bwd("sconv_bwd", _sconv_tile, HALO_SHORT, nb, [(dysc, _chan(t, lambda i: i))], seqs, pars,
                          [o, o, o], [(_sds(w3.shape), _wblk(SC_WIDTH))])


def _conf_tile(r0, ga_ext, gb_ext, w):
    return (_conv31(ga_ext * jax.nn.sigmoid(gb_ext), w),)


def _conf_specs(p1, w31, blk0):
    t = p1.shape[0]
    nb = w31.shape[1] // LANES
    seqs = [(p1, _chan(t, lambda i: blk0 + i)), (p1, _chan(t, lambda i: blk0 + nb + i))]
    return t, nb, seqs, [(w31, _wblk(CF_WIDTH))]


def _confconv(p1, w31, blk0):
    t, nb, seqs, pars = _conf_specs(p1, w31, blk0)
    return _seq_stage("confconv", _conf_tile, HALO_CONF, nb, seqs, pars, [(_sds((t, nb * LANES)), _chan(t, lambda i: i))])[0]


def _confconv_bwd(dz0, p1, w31, blk0):
    t, nb, seqs, pars = _conf_specs(p1, w31, blk0)
    o = (_sds((t, nb * LANES), MXU_DTYPE), _chan(t, lambda i: i))
    return _seq_stage_bwd("confconv_bwd", _conf_tile, HALO_CONF, nb, [(dz0, _chan(t, lambda i: i))], seqs, pars,
                          [o, o], [(_sds(w31.shape), _wblk(CF_WIDTH))])


def _unit_tri_inverse(a):
    c = a.shape[0]
    eye = (lax.broadcasted_iota(jnp.int32, a.shape, 0) == lax.broadcasted_iota(jnp.int32, a.shape, 1)).astype(F32)
    m = -a
    p = eye + m
    steps = max(1, int(math.ceil(math.log2(c))) - 1)
    for _ in range(steps):
        m = _mm_hi(m, m)
        p = p + _mm_hi(p, m)
    return p


def _gdn_chunk(q, k, v, gcol, grow, bcol, s, rev):
    c = q.shape[0]
    i = lax.broadcasted_iota(jnp.int32, (c, c), 0)
    j = lax.broadcasted_iota(jnp.int32, (c, c), 1)
    order = (i - j) * jnp.where(rev > 0, -1, 1)
    incl = order >= 0
    strict = order > 0
    gc_col = jnp.sum(jnp.where(incl, grow, 0.0), axis=1, keepdims=True)
    gc_row = jnp.sum(jnp.where(order <= 0, gcol, 0.0), axis=0, keepdims=True)
    gtot = jnp.sum(grow, axis=1, keepdims=True)
    decay = jnp.exp(jnp.where(incl, gc_col - gc_row, -1e30))
    kb = k * bcol
    a = jnp.where(strict, _mm_nt(kb, k) * decay, 0.0)
    tinv = _unit_tri_inverse(a)
    e_col = jnp.exp(gc_col)
    u = _mm_hi(tinv, v * bcol)
    w = _mm_hi(tinv, kb * e_col)
    k_dec = k * jnp.exp(gtot - gc_col)
    v_new = u - _mm(w, s)
    attn = jnp.where(incl, _mm_nt(q, k) * decay, 0.0)
    o = _mm(q * e_col, s) + _mm(attn, v_new)
    s_next = s * jnp.exp(gtot) + _mm_tn(k_dec, v_new)
    return o, s_next


def _scan_specs(nc):
    c = GDN_CHUNK
    nh = GDN_HEADS

    def cidx(hd, i):
        return jnp.where(hd // nh == 0, i, nc - 1 - i)

    def seq(col0):
        return lambda step: pl.BlockSpec((c, HEAD_DIM), lambda hd, i: (cidx(hd, step(i)), col0 + hd % nh))

    def colv(step):
        return pl.BlockSpec((None, None, c, 1), lambda hd, i: (hd, cidx(hd, step(i)), 0, 0))

    def rowv(step):
        return pl.BlockSpec((None, None, 1, c), lambda hd, i: (hd, cidx(hd, step(i)), 0, 0))

    def dirseq(step):
        return pl.BlockSpec((None, c, HEAD_DIM), lambda hd, i: (hd // nh, cidx(hd, step(i)), hd % nh))

    state = pl.BlockSpec((None, HEAD_DIM, HEAD_DIM), lambda hd, i: (hd, 0, 0))

    def saved(step):
        return pl.BlockSpec((None, None, HEAD_DIM, HEAD_DIM), lambda hd, i: (hd, step(i), 0, 0))

    return seq, colv, rowv, dirseq, state, saved


def _scan_fwd(name, qkv, bcol, gcol, grow, s0):
    t = qkv.shape[0]
    nc = t // GDN_CHUNK
    nh = GDN_HEADS
    seq, colv, rowv, dirseq, state, saved = _scan_specs(nc)
    ident = lambda i: i

    def body(q_ref, k_ref, v_ref, b_ref, gc_ref, gr_ref, s0_ref, o_ref, save_ref, fin_ref, s_ref):
        i = pl.program_id(1)

        @pl.when(i == 0)
        def _():
            s_ref[...] = s0_ref[...]

        s = s_ref[...]
        save_ref[...] = s
        o, s_next = _gdn_chunk(q_ref[...], k_ref[...], v_ref[...], gc_ref[...], gr_ref[...], b_ref[...], s,
                               pl.program_id(0) // nh)
        o_ref[...] = o
        s_ref[...] = s_next
        fin_ref[...] = s_next

    return pl.pallas_call(
        body, name=name, grid=(2 * nh, nc),
        in_specs=[seq(0)(ident), seq(nh)(ident), seq(2 * nh)(ident), colv(ident), colv(ident), rowv(ident), state],
        out_specs=[dirseq(ident), saved(ident), state],
        out_shape=[_sds((2, t, nh * HEAD_DIM)), _sds((2 * nh, nc, HEAD_DIM, HEAD_DIM)), _sds((2 * nh, HEAD_DIM, HEAD_DIM))],
        scratch_shapes=[pltpu.VMEM((HEAD_DIM, HEAD_DIM), F32)],
        compiler_params=pltpu.CompilerParams(dimension_semantics=("arbitrary", "arbitrary"),
                                             vmem_limit_bytes=VMEM_LIMIT_BYTES),
    )(qkv, qkv, qkv, bcol, gcol, grow, s0)


def _scan_bwd(name, do, qkv, bcol, gcol, grow, saved_s, ds_last):
    t = qkv.shape[0]
    nc = t // GDN_CHUNK
    nh = GDN_HEADS
    c = GDN_CHUNK
    seq, colv, rowv, dirseq, state, saved = _scan_specs(nc)
    back = lambda i: nc - 1 - i

    def body(do_ref, q_ref, k_ref, v_ref, b_ref, gc_ref, gr_ref, s_ref, dsl_ref,
             dq_ref, dk_ref, dv_ref, db_ref, dgc_ref, dgr_ref, ds0_ref, ds_ref):
        i = pl.program_id(1)
        rev = pl.program_id(0) // nh

        @pl.when(i == 0)
        def _():
            ds_ref[...] = dsl_ref[...]

        _, vjp = jax.vjp(functools.partial(_gdn_chunk, rev=rev), q_ref[...], k_ref[...], v_ref[...],
                         gc_ref[...], gr_ref[...], b_ref[...], s_ref[...])
        dq, dk, dv, dgc, dgr, db, ds = vjp((do_ref[...], ds_ref[...]))
        dq_ref[...] = dq
        dk_ref[...] = dk
        dv_ref[...] = dv
        db_ref[...] = db
        dgc_ref[...] = dgc
        dgr_ref[...] = dgr
        ds_ref[...] = ds
        ds0_ref[...] = ds

    do_spec = pl.BlockSpec((c, HEAD_DIM), lambda hd, i: (jnp.where(hd // nh == 0, back(i), i), hd % nh))
    seq_out = _sds((2, t, nh * HEAD_DIM))
    vec_c = _sds((2 * nh, nc, c, 1))
    vec_r = _sds((2 * nh, nc, 1, c))
    return pl.pallas_call(
        body, name=name, grid=(2 * nh, nc),
        in_specs=[do_spec, seq(0)(back), seq(nh)(back), seq(2 * nh)(back), colv(back), colv(back), rowv(back),
                  saved(back), state],
        out_specs=[dirseq(back), dirseq(back), dirseq(back), colv(back), colv(back), rowv(back), state],
        out_shape=[seq_out, seq_out, seq_out, vec_c, vec_c, vec_r, _sds((2 * nh, HEAD_DIM, HEAD_DIM))],
        scratch_shapes=[pltpu.VMEM((HEAD_DIM, HEAD_DIM), F32)],
        compiler_params=pltpu.CompilerParams(dimension_semantics=("arbitrary", "arbitrary"),
                                             vmem_limit_bytes=VMEM_LIMIT_BYTES),
    )(do, qkv, qkv, qkv, bcol, gcol, grow, saved_s, ds_last)


def _gate_layouts(gb):
    t = gb.shape[0]
    nc = t // GDN_CHUNK
    nh2 = 2 * GDN_HEADS
    beta = gb[:, :nh2].T.reshape(nh2, nc, GDN_CHUNK)
    g = gb[:, nh2:2 * nh2].T.reshape(nh2, nc, GDN_CHUNK)
    return beta[..., None], g[..., None], g[:, :, None, :]


def _gate_layouts_bwd(dbcol, dgcol, dgrow):
    nh2, nc, c, _ = dbcol.shape
    t = nc * c
    dbeta = dbcol.reshape(nh2, t).T
    dg = (dgcol[..., 0] + dgrow[:, :, 0, :]).reshape(nh2, t).T
    return jnp.concatenate([dbeta, dg, jnp.zeros((t, LANES - 2 * nh2), F32)], axis=1)


def _shard_axis(full_shape, shard_shape):
    return [i for i, (a, b) in enumerate(zip(full_shape, shard_shape)) if a != b][0]


def kernel(x, c, ctx, c_ctx, ada_w, ada_b, ln_g, ln_b, even_w_in, even_w_out, gdn_conv_w, gdn_a_log, gdn_dt_bias, gdn_norm_w, pool_w, pool_scale, odd_w_in, odd_w_out, sconv_w, conf_conv_w, conf_ln_g, conf_ln_b, ffn_w_up, ffn_conv_w, ffn_w_down, loss_target, m_c_ctx, m_ada_w, m_ada_b, m_ln_g, m_ln_b, m_even_w_in, m_even_w_out, m_gdn_conv_w, m_gdn_a_log, m_gdn_dt_bias, m_gdn_norm_w, m_pool_w, m_pool_scale, m_odd_w_in, m_odd_w_out, m_sconv_w, m_conf_conv_w, m_conf_ln_g, m_conf_ln_b, m_ffn_w_up, m_ffn_conv_w, m_ffn_w_down, v_c_ctx, v_ada_w, v_ada_b, v_ln_g, v_ln_b, v_even_w_in, v_even_w_out, v_gdn_conv_w, v_gdn_a_log, v_gdn_dt_bias, v_gdn_norm_w, v_pool_w, v_pool_scale, v_odd_w_in, v_odd_w_out, v_sconv_w, v_conf_conv_w, v_conf_ln_g, v_conf_ln_b, v_ffn_w_up, v_ffn_conv_w, v_ffn_w_down):
    names = ['c_ctx', 'ada_w', 'ada_b', 'ln_g', 'ln_b', 'even_w_in', 'even_w_out', 'gdn_conv_w', 'gdn_a_log',
             'gdn_dt_bias', 'gdn_norm_w', 'pool_w', 'pool_scale', 'odd_w_in', 'odd_w_out', 'sconv_w', 'conf_conv_w',
             'conf_ln_g', 'conf_ln_b', 'ffn_w_up', 'ffn_conv_w', 'ffn_w_down']
    loc = locals()
    wts = {n: loc[n] for n in names}
    mom = {n: loc['m_' + n] for n in names}
    var = {n: loc['v_' + n] for n in names}

    ix, iy, ic = lax.axis_index("x"), lax.axis_index("y"), lax.axis_index("c")
    chip = 2 * ix + iy
    dev = 2 * chip + ic
    d = D_MODEL
    x0, ctx0, tgt = x[0], ctx[0], loss_target[0]
    t, tc = x0.shape[0], ctx0.shape[0]
    depth = ada_w.shape[0]
    n_ada = ada_w.shape[2]

    small_sharded = [gdn_conv_w, sconv_w, conf_conv_w, ffn_conv_w, ln_g, ln_b]
    s1_items = [c] + small_sharded
    s1 = _pack_flat(s1_items, 8 * LANES, F32).reshape(-1, LANES)
    g1 = _all_gather8("gather_small_in", s1, True).reshape(8, -1)
    c_all = g1[:, :d]
    per_chip = [_unpack_flat(g1[2 * k], [a.shape for a in s1_items])[1:] for k in range(4)]
    gdn_conv_f, sconv_f, conf_conv_f, ffn_conv_f, ln_g_f, ln_b_f = [
        jnp.concatenate([per_chip[k][i] for k in range(4)], axis=-1) for i in range(len(small_sharded))]
    c16 = jnp.concatenate([c_all, c_ctx[None], jnp.zeros((7, d), F32)], axis=0)

    mod_part = _mod_rows(c16, ada_w)
    g2 = _all_gather8("gather_mod", mod_part.reshape(-1, LANES), True).reshape(4, 2, depth, 16, n_ada)[:, 0]
    mod_all = jnp.transpose(g2, (1, 2, 0, 3)).reshape(depth, 16, 4 * n_ada) + ada_b[:, None, :]
    mod_me = lax.dynamic_index_in_dim(mod_all, dev, axis=1, keepdims=False).reshape(depth, 6, 1, d)
    sh_c, sc_c = mod_all[0, 8, :d][None], mod_all[0, 8, d:2 * d][None]

    big_names = ['even_w_in', 'even_w_out', 'odd_w_in', 'odd_w_out', 'ffn_w_up', 'ffn_w_down']
    big_full_shapes = {'even_w_in': (d, 4 * even_w_in.shape[1]), 'even_w_out': (4 * even_w_out.shape[0], d),
                       'odd_w_in': (d, 4 * odd_w_in.shape[1]), 'odd_w_out': (4 * odd_w_out.shape[0], d),
                       'ffn_w_up': (depth, d, 4 * ffn_w_up.shape[2]), 'ffn_w_down': (depth, 4 * ffn_w_down.shape[1], d)}
    big_shard_shapes = [wts[n].shape for n in big_names]
    big_axes = [_shard_axis(big_full_shapes[n], wts[n].shape) for n in big_names]
    half_mult = 2 * 16 * LANES
    wpack = _pack_flat([wts[n] for n in big_names], half_mult, MXU_DTYPE).reshape(2, -1, LANES)
    rh = wpack.shape[1]
    my_half = lax.dynamic_index_in_dim(wpack, ic, axis=0, keepdims=False)
    wg = _all_gather8("gather_weights", my_half, False).reshape(4, -1)
    pieces = [_unpack_flat(wg[k], big_shard_shapes) for k in range(4)]
    full = {n: jnp.concatenate([pieces[k][i] for k in range(4)], axis=big_axes[i]) for i, n in enumerate(big_names)}
    n_even = full['even_w_in'].shape[1]
    n_even_pad = -(-n_even // LANES) * LANES
    w_in = jnp.concatenate([full['even_w_in'], jnp.zeros((d, n_even_pad - n_even), MXU_DTYPE)], axis=1)
    w_out, w_oin, w_oout = full['even_w_out'], full['odd_w_in'], full['odd_w_out']
    w_up, w_down = full['ffn_w_up'], full['ffn_w_down']
    scal_blk = (n_even // LANES)
    n_scal = n_even - scal_blk * LANES

    def mod(layer, k):
        return mod_me[layer, k]

    avec = jnp.zeros((1, LANES), F32).at[0, n_scal // 2:n_scal].set(gdn_a_log.reshape(-1))
    dtvec = jnp.zeros((1, LANES), F32).at[0, n_scal // 2:n_scal].set(gdn_dt_bias.reshape(-1))
    normw = gdn_norm_w[None]
    pscale = pool_scale[None]
    cg, cb = conf_ln_g[None], conf_ln_b[None]
    lng = lambda l, k: ln_g_f[l, k][None]
    lnb = lambda l, k: ln_b_f[l, k][None]
    convw9 = ffn_conv_f.reshape(depth, 9, -1)
    nqkv = gdn_conv_f.shape[1]

    p, ub0 = _inproj("even_in", x0, mod(0, 0), mod(0, 1), w_in, 512)
    pc, ucb = _inproj("even_in_ctx", ctx0, sh_c, sc_c, w_in, 256)
    qkv = _qkv_conv("qkv_conv", p, gdn_conv_f)
    qkv_c = _qkv_conv("qkv_conv_ctx", pc, gdn_conv_f)
    gb = _gates("gates", p, avec, dtvec, scal_blk)
    gb_c = _gates("gates_ctx", pc, avec, dtvec, scal_blk)
    lay = _gate_layouts(gb)
    lay_c = _gate_layouts(gb_c)
    s_zero = jnp.zeros((2 * GDN_HEADS, HEAD_DIM, HEAD_DIM), F32)
    _, save_c, s_ctx = _scan_fwd("scan_ctx", qkv_c, *lay_c, s_zero)
    o, save_l, _ = _scan_fwd("scan", qkv, *lay, s_ctx)
    pool_blk = (nqkv + GDN_HEADS * HEAD_DIM) // 512
    ypool = _pool(p, pool_w, pscale, pool_blk)
    x1, mix0, y0 = _even_out(o, p, ypool, x0, normw, mod(0, 2), lng(0, 0), lnb(0, 0), w_out, 256)

    def ffn_fwd(l, xin):
        h, ub = _inproj(f"ffn_up{l}", xin, mod(l, 3), mod(l, 4), w_up[l], 256)
        s = _ffn_conv(f"ffn_conv{l}", h, convw9[l])
        xo, y = _ffn_down(f"ffn_down{l}", s, xin, mod(l, 5), lng(l, 1), lnb(l, 1), w_down[l], 256)
        return xo, (h, ub, s, y)

    x2, ffn0 = ffn_fwd(0, x1)
    p1, ub1 = _inproj("odd_in", x2, mod(1, 0), mod(1, 1), w_oin, 512)
    nsc = sconv_f.shape[1] // LANES
    ysc = _sconv(p1, sconv_f)
    z0 = _confconv(p1, conf_conv_f, 3 * nsc)
    x3, mix1, y1 = _odd_out(ysc, z0, x2, cg, cb, mod(1, 2), lng(1, 0), lnb(1, 0), w_oout, 256)
    x4, ffn1 = ffn_fwd(1, x3)
    loss_part, dx4 = _loss_and_grad(x4, tgt, 512)
    loss = lax.psum(loss_part[0, 0], ("x", "y", "c"))

    dmod = [[None] * 6 for _ in range(depth)]
    dlng = [[None, None] for _ in range(depth)]
    dlnb = [[None, None] for _ in range(depth)]
    gbig = {}
    dconv9 = [None] * depth

    def ffn_bwd(l, dxo, xin, saved):
        h, ub, s, y = saved
        ds, dxa, dyb, dgt, dg_, db_ = _ffn_down_bwd(f"ffn_down_bwd{l}", dxo, xin, y, mod(l, 5), lng(l, 1), lnb(l, 1), w_down[l], 256)
        da, dgate, dw9 = _ffn_conv_bwd(f"ffn_conv_bwd{l}", ds, h, convw9[l])
        dh = jnp.concatenate([da, dgate], axis=1)
        dxin, dsh, dsc = _inproj_bwd(f"ffn_up_bwd{l}", dh, xin, mod(l, 3), mod(l, 4), w_up[l], dxa, 256)
        dmod[l][3], dmod[l][4], dmod[l][5] = dsh, dsc, dgt
        dlng[l][1], dlnb[l][1] = dg_, db_
        dconv9[l] = dw9
        return dxin, _matmul_tn(f"dw_up{l}", ub, dh), _matmul_tn(f"dw_down{l}", s, dyb)

    dx3, dwu1, dwd1 = ffn_bwd(1, dx4, x3, ffn1)
    dysc, dz0, dx2a, dyb1, dcg, dcb, dgt, dg_, db_ = _odd_out_bwd(dx3, ysc, z0, x2, y1, cg, cb, mod(1, 2), lng(1, 0), lnb(1, 0), w_oout, 256)
    dmod[1][2], dlng[1][0], dlnb[1][0] = dgt, dg_, db_
    d_gb, d_gc, d_h, dsconv = _sconv_bwd(dysc, p1, sconv_f)
    d_ga, d_gbb, dconf = _confconv_bwd(dz0, p1, conf_conv_f, 3 * nsc)
    dp1 = jnp.concatenate([d_gb, d_gc, d_h, d_ga, d_gbb], axis=1)
    dx2, dsh, dsc = _inproj_bwd("odd_in_bwd", dp1, x2, mod(1, 0), mod(1, 1), w_oin, dx2a, 512)
    dmod[1][0], dmod[1][1] = dsh, dsc
    gbig['odd_w_out'] = _matmul_tn("dw_oout", mix1, dyb1)
    gbig['odd_w_in'] = _matmul_tn("dw_oin", ub1, dp1)

    dx1, dwu0, dwd0 = ffn_bwd(0, dx2, x1, ffn0)
    gbig['ffn_w_up'] = jnp.stack([dwu0, dwu1])
    gbig['ffn_w_down'] = jnp.stack([dwd0, dwd1])

    do, dpg, dypool, dx0a, dyb0, dnormw, dgt, dg_, db_ = _even_out_bwd(dx1, o, p, ypool, x0, y0, normw, mod(0, 2), lng(0, 0), lnb(0, 0), w_out, 256)
    dmod[0][2], dlng[0][0], dlnb[0][0] = dgt, dg_, db_
    pool_cts = _pool_bwd(dypool, p, pool_w, pscale, pool_blk)
    dpp, dpool_scale, dpool_w = pool_cts[0], pool_cts[1], jnp.stack(pool_cts[2:])
    dq, dk, dv, dbcol, dgcol, dgrow, ds0 = _scan_bwd("scan_bwd", do, qkv, *lay, save_l, s_zero)
    zero_do = jnp.zeros((tc, GDN_HEADS * HEAD_DIM), F32)
    dq_c, dk_c, dv_c, dbcol_c, dgcol_c, dgrow_c, _ = _scan_bwd("scan_bwd_ctx", zero_do, qkv_c, *lay_c, save_c, ds0)
    dgb = _gate_layouts_bwd(dbcol, dgcol, dgrow)
    dgb_c = _gate_layouts_bwd(dbcol_c, dgcol_c, dgrow_c)
    dps, davec, ddtvec = _gates_bwd("gates_bwd", dgb, p, avec, dtvec, scal_blk)
    dps_c, davec_c, ddtvec_c = _gates_bwd("gates_bwd_ctx", dgb_c, pc, avec, dtvec, scal_blk)
    dpqkv, dconv5 = _qkv_conv_bwd("qkv_conv_bwd", jnp.concatenate([dq, dk, dv], axis=2), p, gdn_conv_f)
    dpqkv_c, dconv5_c = _qkv_conv_bwd("qkv_conv_bwd_ctx", jnp.concatenate([dq_c, dk_c, dv_c], axis=2), pc, gdn_conv_f)
    dp = jnp.concatenate([dpqkv, dpg, dpp, dps], axis=1)
    dpc = jnp.concatenate([dpqkv_c, jnp.zeros((tc, n_even_pad - nqkv - LANES), MXU_DTYPE), dps_c], axis=1)
    grad_x, dsh, dsc = _inproj_bwd("even_in_bwd", dp, x0, mod(0, 0), mod(0, 1), w_in, dx0a, 512)
    dmod[0][0], dmod[0][1] = dsh, dsc
    _, dsh_c, dsc_c = _inproj_bwd("even_in_bwd_ctx", dpc, ctx0, sh_c, sc_c, w_in, None, 256)
    gbig['even_w_in'] = _matmul_tn("dw_in", jnp.concatenate([ub0, ucb], axis=0), jnp.concatenate([dp, dpc], axis=0))[:, :n_even]
    gbig['even_w_out'] = _matmul_tn("dw_out", mix0, dyb0)

    def shard_of(g, k, axis, n):
        return lax.slice_in_dim(g, k * n, (k + 1) * n, axis=axis)

    gpack = jnp.stack([
        _pack_flat([shard_of(gbig[n], k, big_axes[i], big_shard_shapes[i][big_axes[i]]) for i, n in enumerate(big_names)],
                   half_mult, MXU_DTYPE).reshape(2, rh, LANES) for k in range(4)], axis=1)
    keep = lax.dynamic_index_in_dim(gpack, ic, axis=0, keepdims=False)
    give = lax.dynamic_index_in_dim(gpack, 1 - ic, axis=0, keepdims=False)
    got = _sibling_exchange("grad_pair_exchange", give)
    pair = _sum_leading("grad_pair_sum", jnp.stack([keep.reshape(-1, LANES), got.reshape(-1, LANES)]), MXU_DTYPE)
    parts = _chip_scatter("grad_chip_scatter", pair.reshape(4, rh, LANES))
    gsum = _sum_leading("grad_chip_sum", jnp.transpose(parts, (1, 0, 2, 3)).reshape(4, 2 * rh, LANES)).reshape(-1)
    g_shards = dict(zip(big_names, _unpack_flat(gsum, big_shard_shapes)))

    dmod_rows = jnp.stack([jnp.concatenate(dmod[l], axis=1)[0] for l in range(depth)])
    dmod_c = jnp.concatenate([dsh_c[0], dsc_c[0], jnp.zeros((4 * d,), F32)])
    dmod_c_rows = jnp.stack([dmod_c] + [jnp.zeros_like(dmod_c)] * (depth - 1))
    small_g = {
        'ln_g': jnp.stack([jnp.stack([dlng[l][k][0] for k in range(2)]) for l in range(depth)]),
        'ln_b': jnp.stack([jnp.stack([dlnb[l][k][0] for k in range(2)]) for l in range(depth)]),
        'gdn_conv_w': dconv5 + dconv5_c,
        'gdn_a_log': (davec + davec_c)[0, n_scal // 2:n_scal].reshape(gdn_a_log.shape),
        'gdn_dt_bias': (ddtvec + ddtvec_c)[0, n_scal // 2:n_scal].reshape(gdn_dt_bias.shape),
        'gdn_norm_w': dnormw[0], 'pool_w': dpool_w, 'pool_scale': dpool_scale[0],
        'sconv_w': dsconv, 'conf_conv_w': dconf, 'conf_ln_g': dcg[0], 'conf_ln_b': dcb[0],
        'ffn_conv_w': jnp.stack(dconv9).reshape(depth, 3, 3, -1),
    }
    small_names = list(small_g)
    s3_items = [dmod_rows, dmod_c_rows] + [small_g[n] for n in small_names]
    s3 = _pack_flat(s3_items, 8 * LANES, F32).reshape(-1, LANES)
    g3 = _all_gather8("gather_small_grads", s3, True).reshape(8, -1, LANES)
    tot3 = _sum_leading("small_grad_sum", g3).reshape(-1)
    tot_items = _unpack_flat(tot3, [a.shape for a in s3_items])
    dmod_sum, dmod_c_sum = tot_items[0], tot_items[1]
    small_tot = dict(zip(small_names, tot_items[2:]))
    grad_ada_b = dmod_sum + dmod_c_sum
    g3f = g3.reshape(8, -1)
    rows_all = g3f[:, :depth * 6 * d].reshape(8, depth, 6 * d)
    cols = lax.dynamic_slice_in_dim(rows_all, chip * n_ada, n_ada, axis=2)
    crow = lax.dynamic_slice_in_dim(dmod_c_sum, chip * n_ada, n_ada, axis=1)
    dmod16 = jnp.concatenate([jnp.transpose(cols, (1, 0, 2)), crow[:, None, :], jnp.zeros((depth, 7, n_ada), F32)], axis=1)
    grad_ada_w, dsil = _ada_grads(c16, dmod16, ada_w)
    s4 = jnp.concatenate([dsil[0, 8][None], jnp.zeros((7, d), F32)], axis=0).reshape(-1, LANES)
    g4 = _all_gather8("gather_cctx", s4, True).reshape(8, 8, d)
    grad_c_ctx = _cctx_grad(g4[0::2, 0][:, None, :], c_ctx[None])[0]

    def my_cols(a, n):
        return lax.dynamic_slice_in_dim(a, chip * n, n, axis=a.ndim - 1)

    grads = dict(g_shards)
    grads['c_ctx'] = grad_c_ctx
    grads['ada_w'] = grad_ada_w
    grads['ada_b'] = grad_ada_b
    for n in ['ln_g', 'ln_b', 'gdn_conv_w', 'sconv_w', 'conf_conv_w', 'ffn_conv_w']:
        grads[n] = my_cols(small_tot[n], wts[n].shape[-1])
    for n in ['gdn_a_log', 'gdn_dt_bias', 'gdn_norm_w', 'pool_w', 'pool_scale', 'conf_ln_g', 'conf_ln_b']:
        grads[n] = small_tot[n]

    delta, new_m, new_v = {}, {}, {}
    big_adam = big_names + ['ada_w']
    for n in big_adam:
        shp = wts[n].shape
        as2d = lambda a: a.reshape(-1, shp[-1])
        dl, nm, nv = _adamw("adamw_" + n, as2d(wts[n]), as2d(grads[n]), as2d(mom[n]), as2d(var[n]))
        delta[n], new_m[n], new_v[n] = dl.reshape(shp), nm.reshape(shp), nv.reshape(shp)
    small_adam = [n for n in names if n not in big_adam]
    packs = [_pack_flat([src[n] for n in small_adam], 8 * LANES, F32).reshape(-1, LANES) for src in (wts, grads, mom, var)]
    outs = _adamw("adamw_small", *packs)
    shapes = [wts[n].shape for n in small_adam]
    for res, dst in zip(outs, (delta, new_m, new_v)):
        for n, a in zip(small_adam, _unpack_flat(res.reshape(-1), shapes)):
            dst[n] = a

    return (loss, grad_x[None], *[grads[n] for n in names], *[delta[n] for n in names],
            *[new_m[n] for n in names], *[new_v[n] for n in names])
```

```python
import functools
import math

import jax
import jax.numpy as jnp
from jax import lax
from jax.experimental import pallas as pl
from jax.experimental.pallas import tpu as pltpu

F32 = jnp.float32
MXU_DTYPE = jnp.bfloat16
HIGHEST = lax.Precision.HIGHEST
MESH = pl.DeviceIdType.MESH

D_MODEL = 1024
GRID_W = 64
GDN_HEADS = 4
HEAD_DIM = 128
GDN_CHUNK = 64
POOL_WINDOWS = (2, 4, 8, 16)
GDN_CONV = 5
SC_WIDTH = 3
CF_WIDTH = 31
ALPHA = 4.0 ** 0.25
LN_EPS = 1e-5
RMS_EPS = 1e-6
LANES = 128
VMEM_LIMIT_BYTES = 58 * 1024 * 1024

ADAM_LR, ADAM_B1, ADAM_B2, ADAM_EPS, ADAM_WD, ADAM_STEP = 0.001, 0.9, 0.999, 1e-08, 0.01, 10


def _blocked(name, fn, grid, ins, in_specs, out_shapes, out_specs, acc=(), acc_axis=None, scratch=()):
    n_in = len(ins)
    n_out = len(out_shapes)

    def body(*refs):
        vals = [r[...] for r in refs[:n_in]]
        res = fn(*vals, *refs[n_in + n_out:])
        if not isinstance(res, (tuple, list)):
            res = (res,)
        for k, (r, v) in enumerate(zip(refs[n_in:n_in + n_out], res)):
            if k in acc:
                first = pl.program_id(acc_axis) == 0

                @pl.when(first)
                def _(r=r, v=v):
                    r[...] = v.astype(r.dtype)

                @pl.when(jnp.logical_not(first))
                def _(r=r, v=v):
                    r[...] += v.astype(r.dtype)
            else:
                r[...] = v.astype(r.dtype)

    return pl.pallas_call(
        body, name=name, grid=grid, in_specs=in_specs, out_specs=out_specs, out_shape=out_shapes,
        scratch_shapes=list(scratch),
        compiler_params=pltpu.CompilerParams(dimension_semantics=("arbitrary",) * len(grid),
                                             vmem_limit_bytes=VMEM_LIMIT_BYTES),
    )(*ins)


def _sds(shape, dtype=F32):
    return jax.ShapeDtypeStruct(tuple(shape), dtype)


def _tok(tm, n, col=0):
    return pl.BlockSpec((tm, n), lambda t: (t, col))


def _res(shape):
    nd = len(shape)
    return pl.BlockSpec(tuple(shape), lambda t: (0,) * nd)


def _silu(x):
    return x * jax.nn.sigmoid(x)


def _layernorm(r, g, b):
    mu = jnp.mean(r, -1, keepdims=True)
    d = r - mu
    var = jnp.mean(d * d, -1, keepdims=True)
    return d * lax.rsqrt(var + LN_EPS) * g + b


def _mm_nn_impl(a, b):
    return jnp.dot(a.astype(MXU_DTYPE), b.astype(MXU_DTYPE), preferred_element_type=F32)


def _mm_nt_impl(a, b):
    return lax.dot_general(a.astype(MXU_DTYPE), b.astype(MXU_DTYPE), (((1,), (1,)), ((), ())), preferred_element_type=F32)


def _mm_tn_impl(a, b):
    return lax.dot_general(a.astype(MXU_DTYPE), b.astype(MXU_DTYPE), (((0,), (0,)), ((), ())), preferred_element_type=F32)


def _mm_hi_impl(a, b):
    return jnp.dot(a, b, preferred_element_type=F32, precision=HIGHEST)


def _mm_vjp(mm, mm_da, mm_db):
    f = jax.custom_vjp(mm)
    f.defvjp(lambda a, b: (mm(a, b), (a, b)), lambda res, g: (mm_da(g, res[1]), mm_db(res[0], g)))
    return f


_mm = _mm_vjp(_mm_nn_impl, lambda g, b: _mm_nt_impl(g, b), lambda a, g: _mm_tn_impl(a, g))
_mm_nt = _mm_vjp(_mm_nt_impl, lambda g, b: _mm_nn_impl(g, b), lambda a, g: _mm_tn_impl(g, a))
_mm_tn = _mm_vjp(_mm_tn_impl, lambda g, b: _mm_nt_impl(b, g), lambda a, g: _mm_nn_impl(a, g))
_mm_hi = _mm_vjp(
    _mm_hi_impl,
    lambda g, b: lax.dot_general(g, b, (((1,), (1,)), ((), ())), preferred_element_type=F32, precision=HIGHEST),
    lambda a, g: lax.dot_general(a, g, (((0,), (0,)), ((), ())), preferred_element_type=F32, precision=HIGHEST))


def _row(w, k):
    rows = lax.broadcasted_iota(jnp.int32, w.shape, 0)
    return jnp.sum(jnp.where(rows == k, w, 0.0), axis=0, keepdims=True)


def _col_mask(shape, dc):
    col = lax.broadcasted_iota(jnp.int32, shape, 0) & (GRID_W - 1)
    return (col + dc >= 0) & (col + dc < GRID_W)


def _center(x_ext, halo):
    return x_ext[halo:x_ext.shape[0] - halo]


def _make_dwconv(taps, halo):
    assert all(abs(s) <= halo for s, _ in taps)

    def shifted(x_ext, s):
        r = x_ext if s == 0 else pltpu.roll(x_ext, (-s) % x_ext.shape[0], 0)
        return _center(r, halo)

    @jax.custom_vjp
    def conv(x_ext, w):
        acc = None
        for k, (s, dc) in enumerate(taps):
            r = shifted(x_ext, s)
            if dc != 0:
                r = jnp.where(_col_mask(r.shape, dc), r, 0.0)
            term = r * _row(w, k)
            acc = term if acc is None else acc + term
        return acc

    def fwd(x_ext, w):
        return conv(x_ext, w), (x_ext, w)

    def bwd(res, dy):
        x_ext, w = res
        n = x_ext.shape[0]
        rows = lax.broadcasted_iota(jnp.int32, w.shape, 0)
        pad = jnp.zeros((halo, dy.shape[1]), F32)
        dx = None
        dw = jnp.zeros(w.shape, F32)
        for k, (s, dc) in enumerate(taps):
            dym = dy if dc == 0 else jnp.where(_col_mask(dy.shape, dc), dy, 0.0)
            dw = dw + jnp.where(rows == k, jnp.sum(dym * shifted(x_ext, s), axis=0, keepdims=True), 0.0)
            t = jnp.concatenate([pad, dym * _row(w, k), pad], axis=0)
            if s != 0:
                t = pltpu.roll(t, s % n, 0)
            dx = t if dx is None else dx + t
        return dx, dw

    conv.defvjp(fwd, bwd)
    return conv


def _taps_1d(width):
    return tuple((k - width // 2, 0) for k in range(width))


HALO_SHORT = 8
HALO_CONF = 16
HALO_GRID = 72
_conv5 = _make_dwconv(_taps_1d(GDN_CONV), HALO_SHORT)
_conv3 = _make_dwconv(_taps_1d(SC_WIDTH), HALO_SHORT)
_conv31 = _make_dwconv(_taps_1d(CF_WIDTH), HALO_CONF)
_conv3x3 = _make_dwconv(tuple((dr * GRID_W + dc, dc) for dr in (-1, 0, 1) for dc in (-1, 0, 1)), HALO_GRID)
_pool_sums = {w: _make_dwconv(tuple((s, 0) for s in range(-(w // 2), w - w // 2)), HALO_SHORT) for w in POOL_WINDOWS}


def _all_gather8(name, blk, in_vmem):
    m_per, n = blk.shape
    space = pltpu.VMEM if in_vmem else pl.ANY

    def body(x_ref, out_ref, send_sems, recv_sems, local_sem):
        x, y, c = lax.axis_index("x"), lax.axis_index("y"), lax.axis_index("c")
        me, sibling = (x, y, c), (x, y, 1 - c)
        chips = [(1 - x, y), (x, 1 - y), (1 - x, 1 - y)]

        def rows(px, py, pc):
            return out_ref.at[pl.ds((4 * px + 2 * py + pc) * m_per, m_per), :]

        def copy(k, block, to, src=None):
            return pltpu.make_async_remote_copy(
                src_ref=rows(*block) if src is None else src, dst_ref=rows(*block),
                send_sem=send_sems.at[k], recv_sem=recv_sems.at[k], device_id=to, device_id_type=MESH)

        mine = pltpu.make_async_copy(x_ref, rows(*me), local_sem)
        mine.start()
        first = [copy(0, me, sibling, src=x_ref)]
        first += [copy(1 + j, me, (*chip, c), src=x_ref) for j, chip in enumerate(chips)]
        for cp in first:
            cp.start()
        passed = [copy(4 + j, (*chip, c), sibling) for j, chip in enumerate(chips)]
        for j, chip in enumerate(chips):
            copy(1 + j, (*chip, c), me).wait_recv()
            passed[j].start()
        copy(0, sibling, me).wait_recv()
        for j, chip in enumerate(chips):
            copy(4 + j, (*chip, 1 - c), me).wait_recv()
        for cp in first + passed:
            cp.wait_send()
        mine.wait()

    return pl.pallas_call(
        body, name=name, out_shape=_sds((8 * m_per, n), blk.dtype),
        in_specs=[pl.BlockSpec(memory_space=space)], out_specs=pl.BlockSpec(memory_space=space),
        scratch_shapes=[pltpu.SemaphoreType.DMA((7,)), pltpu.SemaphoreType.DMA((7,)), pltpu.SemaphoreType.DMA],
        compiler_params=pltpu.CompilerParams(vmem_limit_bytes=VMEM_LIMIT_BYTES),
    )(blk)


def _sibling_exchange(name, send):
    def body(s_ref, out_ref, send_sem, recv_sem):
        x, y, c = lax.axis_index("x"), lax.axis_index("y"), lax.axis_index("c")
        cp = pltpu.make_async_remote_copy(src_ref=s_ref, dst_ref=out_ref, send_sem=send_sem, recv_sem=recv_sem,
                                          device_id=(x, y, 1 - c), device_id_type=MESH)
        cp.start()
        cp.wait()

    return pl.pallas_call(
        body, name=name, out_shape=_sds(send.shape, send.dtype),
        in_specs=[pl.BlockSpec(memory_space=pl.ANY)], out_specs=pl.BlockSpec(memory_space=pl.ANY),
        scratch_shapes=[pltpu.SemaphoreType.DMA, pltpu.SemaphoreType.DMA],
    )(send)


def _chip_scatter(name, part):
    _, r, n = part.shape

    def body(p_ref, out_ref, send_sems, recv_sems, local_sem):
        x, y, c = lax.axis_index("x"), lax.axis_index("y"), lax.axis_index("c")
        sibling = (x, y, 1 - c)
        me_chip = 2 * x + y
        chips = [(1 - x, y), (x, 1 - y), (1 - x, 1 - y)]

        def chip_id(chip):
            return 2 * chip[0] + chip[1]

        def copy(k, src, dst, to):
            return pltpu.make_async_remote_copy(src_ref=src, dst_ref=dst, send_sem=send_sems.at[k],
                                                recv_sem=recv_sems.at[k], device_id=to, device_id_type=MESH)

        mine = pltpu.make_async_copy(p_ref.at[me_chip], out_ref.at[c, me_chip], local_sem)
        mine.start()
        first = [copy(0, p_ref.at[me_chip], out_ref.at[c, me_chip], sibling)]
        first += [copy(1 + j, p_ref.at[chip_id(chip)], out_ref.at[c, me_chip], (*chip, c))
                  for j, chip in enumerate(chips)]
        for cp in first:
            cp.start()
        passed = [copy(4 + j, out_ref.at[c, chip_id(chip)], out_ref.at[c, chip_id(chip)], sibling)
                  for j, chip in enumerate(chips)]
        for j, chip in enumerate(chips):
            copy(1 + j, p_ref.at[0], out_ref.at[c, chip_id(chip)], sibling).wait_recv()
            passed[j].start()
        copy(0, p_ref.at[0], out_ref.at[1 - c, me_chip], sibling).wait_recv()
        for j, chip in enumerate(chips):
            copy(4 + j, p_ref.at[0], out_ref.at[1 - c, chip_id(chip)], sibling).wait_recv()
        for cp in first + passed:
            cp.wait_send()
        mine.wait()

    return pl.pallas_call(
        body, name=name, out_shape=_sds((2, 4, r, n), part.dtype),
        in_specs=[pl.BlockSpec(memory_space=pl.ANY)], out_specs=pl.BlockSpec(memory_space=pl.ANY),
        scratch_shapes=[pltpu.SemaphoreType.DMA((7,)), pltpu.SemaphoreType.DMA((7,)), pltpu.SemaphoreType.DMA],
    )(part)


def _pack_flat(arrays, multiple, dtype):
    flat = jnp.concatenate([a.reshape(-1).astype(dtype) for a in arrays])
    pad = (-flat.shape[0]) % multiple
    if pad:
        flat = jnp.concatenate([flat, jnp.zeros((pad,), dtype)])
    return flat


def _unpack_flat(flat, shapes):
    out, off = [], 0
    for s in shapes:
        n = math.prod(s)
        out.append(flat[off:off + n].reshape(s))
        off += n
    return out


def _row_tile(r, cap, mult=16):
    for cand in range(min(cap, r) // mult * mult, 0, -mult):
        if r % cand == 0:
            return cand
    return r


def _sum_leading(name, stack, out_dtype=F32):
    k, r, n = stack.shape
    tr = _row_tile(r, 2048)

    def fn(s):
        acc = s[0].astype(F32)
        for i in range(1, k):
            acc = acc + s[i].astype(F32)
        return acc

    return _blocked(name, fn, (r // tr,), [stack], [pl.BlockSpec((k, tr, n), lambda t: (0, t, 0))],
                    [_sds((r, n), out_dtype)], [pl.BlockSpec((tr, n), lambda t: (t, 0))])[0]


def _adamw(name, w, g, m, v):
    r, c = w.shape
    tr = _row_tile(r, max(8, (1 << 21) // (4 * c)), 8)
    bc1 = 1.0 - ADAM_B1 ** ADAM_STEP
    bc2 = 1.0 - ADAM_B2 ** ADAM_STEP

    def fn(w, g, m, v):
        nm = ADAM_B1 * m + (1.0 - ADAM_B1) * g
        nv = ADAM_B2 * v + (1.0 - ADAM_B2) * (g * g)
        delta = -ADAM_LR * ((nm / bc1) / (jnp.sqrt(nv / bc2) + ADAM_EPS) + ADAM_WD * w)
        return delta, nm, nv

    spec = pl.BlockSpec((tr, c), lambda t: (t, 0))
    return _blocked(name, fn, (r // tr,), [w, g, m, v], [spec] * 4, [_sds((r, c))] * 3, [spec] * 3)


def _mod_rows(c16, ada_w):
    depth, d, n = ada_w.shape

    def fn(c, w):
        return _mm(_silu(c), w)

    return _blocked("mod_rows", fn, (depth,), [c16, ada_w],
                    [_res(c16.shape), pl.BlockSpec((None, d, n), lambda l: (l, 0, 0))],
                    [_sds((depth, 16, n))], [pl.BlockSpec((None, 16, n), lambda l: (l, 0, 0))])[0]


def _ada_grads(c16, dmod16, ada_w):
    depth, d, n = ada_w.shape

    def fn(c, dm, w):
        return _mm_tn(_silu(c), dm), _mm_nt(dm, w)

    return _blocked("ada_grads", fn, (depth,), [c16, dmod16, ada_w],
                    [_res(c16.shape), pl.BlockSpec((None, 16, n), lambda l: (l, 0, 0)),
                     pl.BlockSpec((None, d, n), lambda l: (l, 0, 0))],
                    [_sds((depth, d, n)), _sds((depth, 16, d))],
                    [pl.BlockSpec((None, d, n), lambda l: (l, 0, 0)), pl.BlockSpec((None, 16, d), lambda l: (l, 0, 0))])


def _cctx_grad(parts, c_ctx_row):
    def fn(p, c):
        tot = ((p[0] + p[1]) + p[2]) + p[3]
        _, vjp = jax.vjp(_silu, c)
        return vjp(tot)[0]

    return _blocked("cctx_grad", fn, (1,), [parts, c_ctx_row], [_res(parts.shape), _res(c_ctx_row.shape)],
                    [_sds(c_ctx_row.shape)], [_res(c_ctx_row.shape)])[0]


def _modulate(x, sh, sc):
    return x * (1.0 + sc) + sh


def _inproj(name, x, sh, sc, w, tm):
    t, d = x.shape
    n = w.shape[1]
    tm = min(tm, t)

    def fn(x, sh, sc, w):
        u = _modulate(x, sh, sc).astype(MXU_DTYPE)
        return jnp.dot(u, w, preferred_element_type=F32), u

    return _blocked(name, fn, (t // tm,), [x, sh, sc, w],
                    [_tok(tm, d), _res(sh.shape), _res(sc.shape), _res(w.shape)],
                    [_sds((t, n)), _sds((t, d), MXU_DTYPE)], [_tok(tm, n), _tok(tm, d)])


def _inproj_bwd(name, dp, x, sh, sc, w, add, tm):
    t, d = x.shape
    n = w.shape[1]
    tm = min(tm, t)
    has_add = add is not None

    def fn(dp, x, sh, sc, w, *rest):
        du = _mm_nt(dp, w)
        _, vjp = jax.vjp(_modulate, x, sh, sc)
        dx, dsh, dsc = vjp(du)
        if has_add:
            dx = dx + rest[0]
        return dx, dsh, dsc

    ins = [dp, x, sh, sc, w] + ([add] if has_add else [])
    specs = [_tok(tm, n), _tok(tm, d), _res(sh.shape), _res(sc.shape), _res(w.shape)] + ([_tok(tm, d)] if has_add else [])
    return _blocked(name, fn, (t // tm,), ins, specs,
                    [_sds((t, d)), _sds(sh.shape), _sds(sc.shape)], [_tok(tm, d), _res(sh.shape), _res(sc.shape)],
                    acc=(1, 2), acc_axis=0)


def _residual_ln(y, x, gt, lng, lnb):
    return _layernorm(ALPHA * x + gt * y, lng, lnb)


def _gdn_mix(o0, o1, pg, yp, normw):
    o = o0 + o1
    heads = []
    for h in range(GDN_HEADS):
        oh = o[:, h * HEAD_DIM:(h + 1) * HEAD_DIM]
        heads.append(oh * lax.rsqrt(jnp.mean(oh * oh, -1, keepdims=True) + RMS_EPS) * normw)
    on = jnp.concatenate(heads, axis=-1) * _silu(pg)
    return jnp.concatenate([on, yp], axis=-1)


def _even_out(o0, o1, p, ypool, x, normw, gt, lng, lnb, w, tm):
    t, d = x.shape
    tm = min(tm, t)
    gate_blk = 3

    def fn(o0, o1, pg, yp, x, normw, gt, lng, lnb, w):
        mix = _gdn_mix(o0, o1, pg, yp, normw).astype(MXU_DTYPE)
        y = jnp.dot(mix, w, preferred_element_type=F32)
        return _residual_ln(y, x, gt, lng, lnb), mix, y

    return _blocked("even_out", fn, (t // tm,), [o0, o1, p, ypool, x, normw, gt, lng, lnb, w],
                    [_tok(tm, 512), _tok(tm, 512), _tok(tm, 512, gate_blk), _tok(tm, 512), _tok(tm, d),
                     _res(normw.shape), _res(gt.shape), _res(lng.shape), _res(lnb.shape), _res(w.shape)],
                    [_sds((t, d)), _sds((t, d), MXU_DTYPE), _sds((t, d))], [_tok(tm, d)] * 3)


def _even_out_bwd(dx1, o0, o1, p, ypool, x, y, normw, gt, lng, lnb, w, tm):
    t, d = x.shape
    tm = min(tm, t)

    def fn(dx1, o0, o1, pg, yp, x, y, normw, gt, lng, lnb, w):
        _, vjp2 = jax.vjp(_residual_ln, y, x, gt, lng, lnb)
        dy, dx, dgt, dlng, dlnb = vjp2(dx1)
        dyb = dy.astype(MXU_DTYPE)
        dmix = _mm_nt(dyb, w)
        _, vjp1 = jax.vjp(_gdn_mix, o0, o1, pg, yp, normw)
        do, _, dpg, dyp, dnormw = vjp1(dmix)
        return do, dpg, dyp, dx, dyb, dnormw, dgt, dlng, dlnb

    return _blocked("even_out_bwd", fn, (t // tm,), [dx1, o0, o1, p, ypool, x, y, normw, gt, lng, lnb, w],
                    [_tok(tm, d), _tok(tm, 512), _tok(tm, 512), _tok(tm, 512, 3), _tok(tm, 512),
                     _tok(tm, d), _tok(tm, d), _res(normw.shape), _res(gt.shape), _res(lng.shape), _res(lnb.shape),
                     _res(w.shape)],
                    [_sds((t, 512)), _sds((t, 512), MXU_DTYPE), _sds((t, 512)), _sds((t, d)), _sds((t, d), MXU_DTYPE),
                     _sds(normw.shape), _sds(gt.shape), _sds(lng.shape), _sds(lnb.shape)],
                    [_tok(tm, 512), _tok(tm, 512), _tok(tm, 512), _tok(tm, d), _tok(tm, d),
                     _res(normw.shape), _res(gt.shape), _res(lng.shape), _res(lnb.shape)],
                    acc=(5, 6, 7, 8), acc_axis=0)


def _odd_mix(ysc, z0, cg, cb):
    z = _silu(_layernorm(z0, cg, cb))
    return jnp.concatenate([ysc, z], axis=-1)


def _odd_out(ysc, z0, x, cg, cb, gt, lng, lnb, w, tm):
    t, d = x.shape
    tm = min(tm, t)

    def fn(ysc, z0, x, cg, cb, gt, lng, lnb, w):
        mix = _odd_mix(ysc, z0, cg, cb).astype(MXU_DTYPE)
        y = jnp.dot(mix, w, preferred_element_type=F32)
        return _residual_ln(y, x, gt, lng, lnb), mix, y

    return _blocked("odd_out", fn, (t // tm,), [ysc, z0, x, cg, cb, gt, lng, lnb, w],
                    [_tok(tm, 512), _tok(tm, 512), _tok(tm, d), _res(cg.shape), _res(cb.shape), _res(gt.shape),
                     _res(lng.shape), _res(lnb.shape), _res(w.shape)],
                    [_sds((t, d)), _sds((t, d), MXU_DTYPE), _sds((t, d))], [_tok(tm, d)] * 3)


def _odd_out_bwd(dx3, ysc, z0, x, y, cg, cb, gt, lng, lnb, w, tm):
    t, d = x.shape
    tm = min(tm, t)

    def fn(dx3, ysc, z0, x, y, cg, cb, gt, lng, lnb, w):
        _, vjp2 = jax.vjp(_residual_ln, y, x, gt, lng, lnb)
        dy, dx, dgt, dlng, dlnb = vjp2(dx3)
        dyb = dy.astype(MXU_DTYPE)
        dmix = _mm_nt(dyb, w)
        _, vjp1 = jax.vjp(_odd_mix, ysc, z0, cg, cb)
        dysc, dz0, dcg, dcb = vjp1(dmix)
        return dysc, dz0, dx, dyb, dcg, dcb, dgt, dlng, dlnb

    return _blocked("odd_out_bwd", fn, (t // tm,), [dx3, ysc, z0, x, y, cg, cb, gt, lng, lnb, w],
                    [_tok(tm, d), _tok(tm, 512), _tok(tm, 512), _tok(tm, d), _tok(tm, d), _res(cg.shape), _res(cb.shape),
                     _res(gt.shape), _res(lng.shape), _res(lnb.shape), _res(w.shape)],
                    [_sds((t, 512)), _sds((t, 512)), _sds((t, d)), _sds((t, d), MXU_DTYPE),
                     _sds(cg.shape), _sds(cb.shape), _sds(gt.shape), _sds(lng.shape), _sds(lnb.shape)],
                    [_tok(tm, 512), _tok(tm, 512), _tok(tm, d), _tok(tm, d),
                     _res(cg.shape), _res(cb.shape), _res(gt.shape), _res(lng.shape), _res(lnb.shape)],
                    acc=(4, 5, 6, 7, 8), acc_axis=0)


def _ffn_down(name, s, x, gt, lng, lnb, w, tm):
    t, d = x.shape
    f = s.shape[1]
    tm = min(tm, t)

    def fn(s, x, gt, lng, lnb, w):
        y = jnp.dot(s, w, preferred_element_type=F32)
        return _residual_ln(y, x, gt, lng, lnb), y

    return _blocked(name, fn, (t // tm,), [s, x, gt, lng, lnb, w],
                    [_tok(tm, f), _tok(tm, d), _res(gt.shape), _res(lng.shape), _res(lnb.shape), _res(w.shape)],
                    [_sds((t, d)), _sds((t, d))], [_tok(tm, d)] * 2)


def _ffn_down_bwd(name, dx2, x, y, gt, lng, lnb, w, tm):
    t, d = x.shape
    f = w.shape[0]
    tm = min(tm, t)

    def fn(dx2, x, y, gt, lng, lnb, w):
        _, vjp2 = jax.vjp(_residual_ln, y, x, gt, lng, lnb)
        dy, dx, dgt, dlng, dlnb = vjp2(dx2)
        dyb = dy.astype(MXU_DTYPE)
        return _mm_nt(dyb, w), dx, dyb, dgt, dlng, dlnb

    return _blocked(name, fn, (t // tm,), [dx2, x, y, gt, lng, lnb, w],
                    [_tok(tm, d), _tok(tm, d), _tok(tm, d), _res(gt.shape), _res(lng.shape), _res(lnb.shape), _res(w.shape)],
                    [_sds((t, f), MXU_DTYPE), _sds((t, d)), _sds((t, d), MXU_DTYPE), _sds(gt.shape), _sds(lng.shape), _sds(lnb.shape)],
                    [_tok(tm, f), _tok(tm, d), _tok(tm, d), _res(gt.shape), _res(lng.shape), _res(lnb.shape)],
                    acc=(3, 4, 5), acc_axis=0)


def _loss_and_grad(x4, target, tm):
    t, d = x4.shape
    tm = min(tm, t)

    def fn(y, tg):
        e = y - tg
        part = 0.5 * jnp.sum(jnp.mean(e * e, axis=-1, keepdims=True), axis=0, keepdims=True)
        return jnp.broadcast_to(part, (1, LANES)), e * (1.0 / d)

    return _blocked("loss", fn, (t // tm,), [x4, target], [_tok(tm, d), _tok(tm, d)],
                    [_sds((1, LANES)), _sds((t, d))], [_res((1, LANES)), _tok(tm, d)], acc=(0,), acc_axis=0)


def _matmul_tn(name, a, b):
    t, k = a.shape
    n = b.shape[1]
    tt = math.gcd(t, 512)
    tn = 512 if n % 512 == 0 else (384 if n % 384 == 0 else 128)

    def body(a_ref, b_ref, o_ref):
        @pl.when(pl.program_id(1) == 0)
        def _():
            o_ref[...] = jnp.zeros(o_ref.shape, F32)

        o_ref[...] += lax.dot_general(a_ref[...], b_ref[...], (((0,), (0,)), ((), ())), preferred_element_type=F32)

    return pl.pallas_call(
        body, name=name, grid=(n // tn, t // tt),
        in_specs=[pl.BlockSpec((tt, k), lambda j, i: (i, 0)), pl.BlockSpec((tt, tn), lambda j, i: (i, j))],
        out_specs=pl.BlockSpec((k, tn), lambda j, i: (0, j)), out_shape=_sds((k, n)),
        compiler_params=pltpu.CompilerParams(dimension_semantics=("arbitrary", "arbitrary"),
                                             vmem_limit_bytes=VMEM_LIMIT_BYTES),
    )(a, b)


def _chan(t, col_of):
    return pl.BlockSpec((t, LANES), lambda i: (0, col_of(i)))


def _wblk(k, col_of=lambda i: i):
    return pl.BlockSpec((k, LANES), lambda i: (0, col_of(i)))


SEQ_TILE = 256


def _load_ext(ref, r0, rows, halo, total):
    lo, hi = max(r0 - halo, 0), min(r0 + rows + halo, total)
    parts = []
    if lo > r0 - halo:
        parts.append(jnp.zeros((lo - (r0 - halo), ref.shape[1]), F32))
    parts.append(ref[lo:hi, :].astype(F32))
    if hi < r0 + rows + halo:
        parts.append(jnp.zeros((r0 + rows + halo - hi, ref.shape[1]), F32))
    return parts[0] if len(parts) == 1 else jnp.concatenate(parts, axis=0)


def _seq_stage(name, tile_fn, halo, grid_n, seqs, pars, outs):
    total = seqs[0][0].shape[0]
    rows = min(SEQ_TILE, total)
    ns, npar = len(seqs), len(pars)

    def body(*refs):
        seq_refs, par_refs, out_refs = refs[:ns], refs[ns:ns + npar], refs[ns + npar:]
        par_vals = [r[...] for r in par_refs]
        for r0 in range(0, total, rows):
            exts = [_load_ext(r, r0, rows, halo, total) for r in seq_refs]
            res = tile_fn(r0, *exts, *par_vals)
            for o_ref, v in zip(out_refs, res):
                o_ref[r0:r0 + rows, :] = v.astype(o_ref.dtype)

    return pl.pallas_call(
        body, name=name, grid=(grid_n,), in_specs=[s for _, s in seqs] + [s for _, s in pars],
        out_specs=[s for _, s in outs], out_shape=[o for o, _ in outs],
        compiler_params=pltpu.CompilerParams(dimension_semantics=("arbitrary",), vmem_limit_bytes=VMEM_LIMIT_BYTES),
    )(*[a for a, _ in seqs], *[a for a, _ in pars])


def _seq_stage_bwd(name, tile_fn, halo, grid_n, douts, seqs, pars, dseq_outs, dpar_outs):
    total = seqs[0][0].shape[0]
    rows = min(SEQ_TILE, total)
    groups = [d if isinstance(d, list) else [d] for d in douts]
    douts = [d for g in groups for d in g]
    starts = [sum(len(g) for g in groups[:k]) for k in range(len(groups))]
    nd, ns, npar = len(douts), len(seqs), len(pars)
    widths = [s.block_shape[-1] for _, s in seqs]

    def dtile(dout_refs, k, r0):
        terms = [d[r0:r0 + rows, :].astype(F32) for d in dout_refs[starts[k]:starts[k] + len(groups[k])]]
        return functools.reduce(lambda a, b: a + b, terms)

    def body(*refs):
        dout_refs, seq_refs, par_refs = refs[:nd], refs[nd:nd + ns], refs[nd + ns:nd + ns + npar]
        dseq_refs = refs[nd + ns + npar:nd + 2 * ns + npar]
        dpar_refs = refs[nd + 2 * ns + npar:nd + 2 * ns + 2 * npar]
        accs = refs[nd + 2 * ns + 2 * npar:]
        par_vals = [r[...] for r in par_refs]
        for a in accs:
            for r0 in range(0, total, rows):
                a[r0:r0 + rows, :] = jnp.zeros((rows, a.shape[1]), F32)
        dpars = [jnp.zeros(p.shape, F32) for p in par_vals]
        for r0 in range(0, total, rows):
            exts = [_load_ext(r, r0, rows, halo, total) for r in seq_refs]
            _, vjp = jax.vjp(functools.partial(tile_fn, r0), *exts, *par_vals)
            cts = vjp(tuple(dtile(dout_refs, k, r0) for k in range(len(groups))))
            lo, hi = max(r0 - halo, 0), min(r0 + rows + halo, total)
            for a, dx in zip(accs, cts[:ns]):
                a[lo:hi, :] += dx[lo - (r0 - halo):hi - (r0 - halo)]
            dpars = [acc + g for acc, g in zip(dpars, cts[ns:])]
        for o_ref, a in zip(dseq_refs, accs):
            for r0 in range(0, total, rows):
                o_ref[r0:r0 + rows, :] = a[r0:r0 + rows, :].astype(o_ref.dtype)
        for o_ref, g in zip(dpar_refs, dpars):
            o_ref[...] = g

    return pl.pallas_call(
        body, name=name, grid=(grid_n,),
        in_specs=[s for _, s in douts] + [s for _, s in seqs] + [s for _, s in pars],
        out_specs=[s for _, s in dseq_outs] + [s for _, s in dpar_outs],
        out_shape=[o for o, _ in dseq_outs] + [o for o, _ in dpar_outs],
        scratch_shapes=[pltpu.VMEM((total, w), F32) for w in widths],
        compiler_params=pltpu.CompilerParams(dimension_semantics=("arbitrary",), vmem_limit_bytes=VMEM_LIMIT_BYTES),
    )(*[a for a, _ in douts], *[a for a, _ in seqs], *[a for a, _ in pars])


def _qkv_tile(r0, p_ext, w):
    kind = pl.program_id(0)
    a = _silu(_conv5(p_ext, w))
    nrm = a * lax.rsqrt(jnp.sum(a * a, -1, keepdims=True) + RMS_EPS)
    scale = jnp.where(kind < GDN_HEADS, HEAD_DIM ** -0.5, 1.0).astype(F32)
    sel = (kind < 2 * GDN_HEADS).astype(F32)
    return (sel * (nrm * scale) + (1.0 - sel) * a,)


def _qkv_conv(name, p, w):
    t = p.shape[0]
    nb = w.shape[1] // LANES
    ident = lambda i: i
    return _seq_stage(name, _qkv_tile, HALO_SHORT, nb, [(p, _chan(t, ident))], [(w, _wblk(GDN_CONV))],
                      [(_sds((t, nb * LANES)), _chan(t, ident))])[0]


def _qkv_conv_bwd(name, dqkv0, dqkv1, p, w):
    t = p.shape[0]
    nb = w.shape[1] // LANES
    ident = lambda i: i
    return _seq_stage_bwd(name, _qkv_tile, HALO_SHORT, nb, [[(dqkv0, _chan(t, ident)), (dqkv1, _chan(t, ident))]],
                          [(p, _chan(t, ident))],
                          [(w, _wblk(GDN_CONV))], [(_sds((t, nb * LANES), MXU_DTYPE), _chan(t, ident))],
                          [(_sds(w.shape), _wblk(GDN_CONV))])


def _gates_fn(s, avec, dtvec):
    lane = lax.broadcasted_iota(jnp.int32, s.shape, 1)
    beta = jax.nn.sigmoid(s)
    g = -jnp.exp(avec) * jax.nn.softplus(s + dtvec)
    nh = 2 * GDN_HEADS
    return jnp.where(lane < nh, beta, jnp.where(lane < 2 * nh, g, 0.0))


def _gates(name, p, avec, dtvec, col):
    t = p.shape[0]
    tm = min(512, t)
    return _blocked(name, _gates_fn, (t // tm,), [p, avec, dtvec], [_tok(tm, LANES, col), _res(avec.shape), _res(dtvec.shape)],
                    [_sds((t, LANES))], [_tok(tm, LANES)])[0]


def _gates_bwd(name, dgb, p, avec, dtvec, col):
    t = p.shape[0]
    tm = min(512, t)

    def fn(dgb, s, avec, dtvec):
        _, vjp = jax.vjp(_gates_fn, s, avec, dtvec)
        return vjp(dgb)

    return _blocked(name, fn, (t // tm,), [dgb, p, avec, dtvec],
                    [_tok(tm, LANES), _tok(tm, LANES, col), _res(avec.shape), _res(dtvec.shape)],
                    [_sds((t, LANES), MXU_DTYPE), _sds(avec.shape), _sds(dtvec.shape)],
                    [_tok(tm, LANES), _res(avec.shape), _res(dtvec.shape)], acc=(1, 2), acc_axis=0)


def _make_pool_tile(total):
    def tile(r0, x_ext, scale, *ws):
        ys = []
        for g, win in enumerate(POOL_WINDOWS):
            xg = x_ext[:, g * LANES:(g + 1) * LANES]
            rows = xg.shape[0] - 2 * HALO_SHORT
            pos = r0 + lax.broadcasted_iota(jnp.int32, (rows, LANES), 0)
            lo = jnp.clip(pos - win // 2, 0, total)
            hi = jnp.clip(pos - win // 2 + win, 0, total)
            window_sum = _pool_sums[win](xg, jnp.ones((win, LANES), F32))
            pooled = window_sum / (hi - lo).astype(F32) - _center(xg, HALO_SHORT)
            ys.append(_mm(pooled, ws[g]))
        return (jnp.concatenate(ys, axis=-1) * scale,)

    return tile


def _pool_specs(p, pool_w, pool_scale, col0):
    t = p.shape[0]
    n = len(POOL_WINDOWS) * LANES
    seq = (p, pl.BlockSpec((t, n), lambda i: (0, col0)))
    pars = [(pool_scale, _res(pool_scale.shape))]
    pars += [(pool_w, pl.BlockSpec((None, LANES, LANES), lambda i, g=g: (g, 0, 0))) for g in range(len(POOL_WINDOWS))]
    return t, n, seq, pars


def _pool(p, pool_w, pool_scale, col0):
    t, n, seq, pars = _pool_specs(p, pool_w, pool_scale, col0)
    return _seq_stage("pool", _make_pool_tile(t), HALO_SHORT, 1, [seq], pars, [(_sds((t, n)), _res((t, n)))])[0]


def _pool_bwd(dy, p, pool_w, pool_scale, col0):
    t, n, seq, pars = _pool_specs(p, pool_w, pool_scale, col0)
    wspec = (_sds((LANES, LANES)), _res((LANES, LANES)))
    return _seq_stage_bwd("pool_bwd", _make_pool_tile(t), HALO_SHORT, 1, [(dy, _res(dy.shape))], [seq], pars,
                          [(_sds((t, n), MXU_DTYPE), _res((t, n)))],
                          [(_sds(pool_scale.shape), _res(pool_scale.shape))] + [wspec] * len(POOL_WINDOWS))


def _ffn_conv_tile(r0, a_ext, gt_ext, w):
    return (_silu(_conv3x3(a_ext, w)) * _center(gt_ext, HALO_GRID),)


def _ffn_conv(name, h, w9):
    t = h.shape[0]
    nb = w9.shape[1] // LANES
    ident = lambda i: i
    return _seq_stage(name, _ffn_conv_tile, HALO_GRID, nb, [(h, _chan(t, ident)), (h, _chan(t, lambda i: nb + i))],
                      [(w9, _wblk(9))], [(_sds((t, nb * LANES), MXU_DTYPE), _chan(t, ident))])[0]


def _ffn_conv_bwd(name, ds, h, w9):
    t = h.shape[0]
    nb = w9.shape[1] // LANES
    ident = lambda i: i
    o = (_sds((t, nb * LANES), MXU_DTYPE), _chan(t, ident))
    return _seq_stage_bwd(name, _ffn_conv_tile, HALO_GRID, nb, [(ds, _chan(t, ident))],
                          [(h, _chan(t, ident)), (h, _chan(t, lambda i: nb + i))], [(w9, _wblk(9))],
                          [o, o], [(_sds(w9.shape), _wblk(9))])


def _sconv_tile(r0, gb_ext, gc_ext, h_ext, w):
    return (_center(gb_ext, HALO_SHORT) * _conv3(gc_ext * h_ext, w),)


def _sconv_specs(p1, w3):
    t = p1.shape[0]
    nb = w3.shape[1] // LANES
    seqs = [(p1, _chan(t, lambda i: i)), (p1, _chan(t, lambda i: nb + i)), (p1, _chan(t, lambda i: 2 * nb + i))]
    return t, nb, seqs, [(w3, _wblk(SC_WIDTH))]


def _sconv(p1, w3):
    t, nb, seqs, pars = _sconv_specs(p1, w3)
    return _seq_stage("sconv", _sconv_tile, HALO_SHORT, nb, seqs, pars, [(_sds((t, nb * LANES)), _chan(t, lambda i: i))])[0]


def _sconv_bwd(dysc, p1, w3):
    t, nb, seqs, pars = _sconv_specs(p1, w3)
    o = (_sds((t, nb * LANES), MXU_DTYPE), _chan(t, lambda i: i))
    return _seq_stage_bwd("sconv_bwd", _sconv_tile, HALO_SHORT, nb, [(dysc, _chan(t, lambda i: i))], seqs, pars,
                          [o, o, o], [(_sds(w3.shape), _wblk(SC_WIDTH))])


def _conf_tile(r0, ga_ext, gb_ext, w):
    return (_conv31(ga_ext * jax.nn.sigmoid(gb_ext), w),)


def _conf_specs(p1, w31, blk0):
    t = p1.shape[0]
    nb = w31.shape[1] // LANES
    seqs = [(p1, _chan(t, lambda i: blk0 + i)), (p1, _chan(t, lambda i: blk0 + nb + i))]
    return t, nb, seqs, [(w31, _wblk(CF_WIDTH))]


def _confconv(p1, w31, blk0):
    t, nb, seqs, pars = _conf_specs(p1, w31, blk0)
    return _seq_stage("confconv", _conf_tile, HALO_CONF, nb, seqs, pars, [(_sds((t, nb * LANES)), _chan(t, lambda i: i))])[0]


def _confconv_bwd(dz0, p1, w31, blk0):
    t, nb, seqs, pars = _conf_specs(p1, w31, blk0)
    o = (_sds((t, nb * LANES), MXU_DTYPE), _chan(t, lambda i: i))
    return _seq_stage_bwd("confconv_bwd", _conf_tile, HALO_CONF, nb, [(dz0, _chan(t, lambda i: i))], seqs, pars,
                          [o, o], [(_sds(w31.shape), _wblk(CF_WIDTH))])


def _unit_tri_inverse(a):
    c = a.shape[0]
    eye = (lax.broadcasted_iota(jnp.int32, a.shape, 0) == lax.broadcasted_iota(jnp.int32, a.shape, 1)).astype(F32)
    m = -a
    p = eye + m
    steps = max(1, int(math.ceil(math.log2(c))) - 1)
    for _ in range(steps):
        m = _mm_hi(m, m)
        p = p + _mm_hi(p, m)
    return p


def _gdn_chunk(q, k, v, gcol, grow, bcol, s, rev):
    c = q.shape[0]
    i = lax.broadcasted_iota(jnp.int32, (c, c), 0)
    j = lax.broadcasted_iota(jnp.int32, (c, c), 1)
    order = (j - i) if rev else (i - j)
    incl = order >= 0
    strict = order > 0
    gc_col = jnp.sum(jnp.where(incl, grow, 0.0), axis=1, keepdims=True)
    gc_row = jnp.sum(jnp.where(order <= 0, gcol, 0.0), axis=0, keepdims=True)
    gtot = jnp.sum(grow, axis=1, keepdims=True)
    decay = jnp.exp(jnp.where(incl, gc_col - gc_row, -1e30))
    kb = k * bcol
    a = jnp.where(strict, _mm_nt(kb, k) * decay, 0.0)
    tinv = _unit_tri_inverse(a)
    e_col = jnp.exp(gc_col)
    u = _mm_hi(tinv, v * bcol)
    w = _mm_hi(tinv, kb * e_col)
    k_dec = k * jnp.exp(gtot - gc_col)
    v_new = u - _mm(w, s)
    attn = jnp.where(incl, _mm_nt(q, k) * decay, 0.0)
    o = _mm(q * e_col, s) + _mm(attn, v_new)
    s_next = s * jnp.exp(gtot) + _mm_tn(k_dec, v_new)
    return o, s_next


N_CHAINS = 2 * GDN_HEADS
QKV_W = 3 * GDN_HEADS * HEAD_DIM


def _scan_specs(nc, step):
    c = GDN_CHUNK
    seq = [pl.BlockSpec((c, QKV_W), lambda i: (step(i), 0)), pl.BlockSpec((c, QKV_W), lambda i: (nc - 1 - step(i), 0))]
    heads = [pl.BlockSpec((c, GDN_HEADS * HEAD_DIM), lambda i: (step(i), 0)),
             pl.BlockSpec((c, GDN_HEADS * HEAD_DIM), lambda i: (nc - 1 - step(i), 0))]
    colv = pl.BlockSpec((N_CHAINS, None, c, 1), lambda i: (0, step(i), 0, 0))
    rowv = pl.BlockSpec((N_CHAINS, None, 1, c), lambda i: (0, step(i), 0, 0))
    state = pl.BlockSpec((N_CHAINS, HEAD_DIM, HEAD_DIM), lambda i: (0, 0, 0))
    saved = pl.BlockSpec((N_CHAINS, None, HEAD_DIM, HEAD_DIM), lambda i: (0, step(i), 0, 0))
    return seq, heads, colv, rowv, state, saved


def _head_cols(h, part):
    lo = (part * GDN_HEADS + h) * HEAD_DIM
    return slice(lo, lo + HEAD_DIM)


def _scan_fwd(name, qkv, bcol, gcol, grow, s0):
    t = qkv.shape[0]
    nc = t // GDN_CHUNK
    seq, heads, colv, rowv, state, saved = _scan_specs(nc, lambda i: i)

    def body(qkv0_ref, qkv1_ref, b_ref, gc_ref, gr_ref, s0_ref, o0_ref, o1_ref, save_ref, fin_ref, s_ref):
        @pl.when(pl.program_id(0) == 0)
        def _():
            s_ref[...] = s0_ref[...]

        for n in range(N_CHAINS):
            d, h = divmod(n, GDN_HEADS)
            x_ref, o_ref = (qkv0_ref, o0_ref) if d == 0 else (qkv1_ref, o1_ref)
            s = s_ref[n]
            save_ref[n] = s
            o, s_next = _gdn_chunk(x_ref[:, _head_cols(h, 0)], x_ref[:, _head_cols(h, 1)], x_ref[:, _head_cols(h, 2)],
                                   gc_ref[n], gr_ref[n], b_ref[n], s, d)
            o_ref[:, _head_cols(h, 0)] = o
            s_ref[n] = s_next
            fin_ref[n] = s_next

    hw = GDN_HEADS * HEAD_DIM
    return pl.pallas_call(
        body, name=name, grid=(nc,), in_specs=seq + [colv, colv, rowv, state],
        out_specs=heads + [saved, state],
        out_shape=[_sds((t, hw)), _sds((t, hw)), _sds((N_CHAINS, nc, HEAD_DIM, HEAD_DIM)), _sds((N_CHAINS, HEAD_DIM, HEAD_DIM))],
        scratch_shapes=[pltpu.VMEM((N_CHAINS, HEAD_DIM, HEAD_DIM), F32)],
        compiler_params=pltpu.CompilerParams(dimension_semantics=("arbitrary",), vmem_limit_bytes=VMEM_LIMIT_BYTES),
    )(qkv, qkv, bcol, gcol, grow, s0)


def _scan_bwd(name, do, qkv, bcol, gcol, grow, saved_s, ds_last):
    t = qkv.shape[0]
    nc = t // GDN_CHUNK
    c = GDN_CHUNK
    seq, heads, colv, rowv, state, saved = _scan_specs(nc, lambda i: nc - 1 - i)

    def body(do0_ref, do1_ref, qkv0_ref, qkv1_ref, b_ref, gc_ref, gr_ref, s_ref, dsl_ref,
             dx0_ref, dx1_ref, db_ref, dgc_ref, dgr_ref, ds0_ref, ds_ref):
        @pl.when(pl.program_id(0) == 0)
        def _():
            ds_ref[...] = dsl_ref[...]

        for n in range(N_CHAINS):
            d, h = divmod(n, GDN_HEADS)
            x_ref, do_ref, dx_ref = (qkv0_ref, do0_ref, dx0_ref) if d == 0 else (qkv1_ref, do1_ref, dx1_ref)
            _, vjp = jax.vjp(functools.partial(_gdn_chunk, rev=d), x_ref[:, _head_cols(h, 0)], x_ref[:, _head_cols(h, 1)],
                             x_ref[:, _head_cols(h, 2)], gc_ref[n], gr_ref[n], b_ref[n], s_ref[n])
            dq, dk, dv, dgc, dgr, db, ds = vjp((do_ref[:, _head_cols(h, 0)], ds_ref[n]))
            dx_ref[:, _head_cols(h, 0)] = dq
            dx_ref[:, _head_cols(h, 1)] = dk
            dx_ref[:, _head_cols(h, 2)] = dv
            db_ref[n] = db
            dgc_ref[n] = dgc
            dgr_ref[n] = dgr
            ds_ref[n] = ds
            ds0_ref[n] = ds

    vec_c = _sds((N_CHAINS, nc, c, 1))
    vec_r = _sds((N_CHAINS, nc, 1, c))
    return pl.pallas_call(
        body, name=name, grid=(nc,),
        in_specs=heads + seq + [colv, colv, rowv, saved, state],
        out_specs=seq + [colv, colv, rowv, state],
        out_shape=[_sds((t, QKV_W)), _sds((t, QKV_W)), vec_c, vec_c, vec_r, _sds((N_CHAINS, HEAD_DIM, HEAD_DIM))],
        scratch_shapes=[pltpu.VMEM((N_CHAINS, HEAD_DIM, HEAD_DIM), F32)],
        compiler_params=pltpu.CompilerParams(dimension_semantics=("arbitrary",), vmem_limit_bytes=VMEM_LIMIT_BYTES),
    )(do, do, qkv, qkv, bcol, gcol, grow, saved_s, ds_last)


def _gate_layouts(gb):
    t = gb.shape[0]
    nc = t // GDN_CHUNK
    nh = GDN_HEADS

    def by_scan_position(a):
        a = a.T.reshape(2 * nh, nc, GDN_CHUNK)
        return jnp.concatenate([a[:nh], a[nh:, ::-1]], axis=0)

    beta = by_scan_position(gb[:, :2 * nh])
    g = by_scan_position(gb[:, 2 * nh:4 * nh])
    return beta[..., None], g[..., None], g[:, :, None, :]


def _gate_layouts_bwd(dbcol, dgcol, dgrow):
    n2, nc, c, _ = dbcol.shape
    nh = n2 // 2
    t = nc * c

    def by_token(a):
        return jnp.concatenate([a[:nh], a[nh:, ::-1]], axis=0).reshape(n2, t).T

    dbeta = by_token(dbcol[..., 0])
    dg = by_token(dgcol[..., 0] + dgrow[:, :, 0, :])
    return jnp.concatenate([dbeta, dg, jnp.zeros((t, LANES - 2 * n2), F32)], axis=1)


def _shard_axis(full_shape, shard_shape):
    return [i for i, (a, b) in enumerate(zip(full_shape, shard_shape)) if a != b][0]


def kernel(x, c, ctx, c_ctx, ada_w, ada_b, ln_g, ln_b, even_w_in, even_w_out, gdn_conv_w, gdn_a_log, gdn_dt_bias, gdn_norm_w, pool_w, pool_scale, odd_w_in, odd_w_out, sconv_w, conf_conv_w, conf_ln_g, conf_ln_b, ffn_w_up, ffn_conv_w, ffn_w_down, loss_target, m_c_ctx, m_ada_w, m_ada_b, m_ln_g, m_ln_b, m_even_w_in, m_even_w_out, m_gdn_conv_w, m_gdn_a_log, m_gdn_dt_bias, m_gdn_norm_w, m_pool_w, m_pool_scale, m_odd_w_in, m_odd_w_out, m_sconv_w, m_conf_conv_w, m_conf_ln_g, m_conf_ln_b, m_ffn_w_up, m_ffn_conv_w, m_ffn_w_down, v_c_ctx, v_ada_w, v_ada_b, v_ln_g, v_ln_b, v_even_w_in, v_even_w_out, v_gdn_conv_w, v_gdn_a_log, v_gdn_dt_bias, v_gdn_norm_w, v_pool_w, v_pool_scale, v_odd_w_in, v_odd_w_out, v_sconv_w, v_conf_conv_w, v_conf_ln_g, v_conf_ln_b, v_ffn_w_up, v_ffn_conv_w, v_ffn_w_down):
    names = ['c_ctx', 'ada_w', 'ada_b', 'ln_g', 'ln_b', 'even_w_in', 'even_w_out', 'gdn_conv_w', 'gdn_a_log',
             'gdn_dt_bias', 'gdn_norm_w', 'pool_w', 'pool_scale', 'odd_w_in', 'odd_w_out', 'sconv_w', 'conf_conv_w',
             'conf_ln_g', 'conf_ln_b', 'ffn_w_up', 'ffn_conv_w', 'ffn_w_down']
    loc = locals()
    wts = {n: loc[n] for n in names}
    mom = {n: loc['m_' + n] for n in names}
    var = {n: loc['v_' + n] for n in names}

    ix, iy, ic = lax.axis_index("x"), lax.axis_index("y"), lax.axis_index("c")
    chip = 2 * ix + iy
    dev = 2 * chip + ic
    d = D_MODEL
    x0, ctx0, tgt = x[0], ctx[0], loss_target[0]
    t, tc = x0.shape[0], ctx0.shape[0]
    depth = ada_w.shape[0]
    n_ada = ada_w.shape[2]

    small_sharded = [gdn_conv_w, sconv_w, conf_conv_w, ffn_conv_w, ln_g, ln_b]
    s1_items = [c] + small_sharded
    s1 = _pack_flat(s1_items, 8 * LANES, F32).reshape(-1, LANES)
    g1 = _all_gather8("gather_small_in", s1, True).reshape(8, -1)
    c_all = g1[:, :d]
    per_chip = [_unpack_flat(g1[2 * k], [a.shape for a in s1_items])[1:] for k in range(4)]
    gdn_conv_f, sconv_f, conf_conv_f, ffn_conv_f, ln_g_f, ln_b_f = [
        jnp.concatenate([per_chip[k][i] for k in range(4)], axis=-1) for i in range(len(small_sharded))]
    c16 = jnp.concatenate([c_all, c_ctx[None], jnp.zeros((7, d), F32)], axis=0)

    mod_part = _mod_rows(c16, ada_w)
    g2 = _all_gather8("gather_mod", mod_part.reshape(-1, LANES), True).reshape(4, 2, depth, 16, n_ada)[:, 0]
    mod_all = jnp.transpose(g2, (1, 2, 0, 3)).reshape(depth, 16, 4 * n_ada) + ada_b[:, None, :]
    mod_me = lax.dynamic_index_in_dim(mod_all, dev, axis=1, keepdims=False).reshape(depth, 6, 1, d)
    sh_c, sc_c = mod_all[0, 8, :d][None], mod_all[0, 8, d:2 * d][None]

    big_names = ['even_w_in', 'even_w_out', 'odd_w_in', 'odd_w_out', 'ffn_w_up', 'ffn_w_down']
    big_full_shapes = {'even_w_in': (d, 4 * even_w_in.shape[1]), 'even_w_out': (4 * even_w_out.shape[0], d),
                       'odd_w_in': (d, 4 * odd_w_in.shape[1]), 'odd_w_out': (4 * odd_w_out.shape[0], d),
                       'ffn_w_up': (depth, d, 4 * ffn_w_up.shape[2]), 'ffn_w_down': (depth, 4 * ffn_w_down.shape[1], d)}
    big_shard_shapes = [wts[n].shape for n in big_names]
    big_axes = [_shard_axis(big_full_shapes[n], wts[n].shape) for n in big_names]
    half_mult = 2 * 16 * LANES
    wpack = _pack_flat([wts[n] for n in big_names], half_mult, MXU_DTYPE).reshape(2, -1, LANES)
    rh = wpack.shape[1]
    my_half = lax.dynamic_index_in_dim(wpack, ic, axis=0, keepdims=False)
    wg = _all_gather8("gather_weights", my_half, False).reshape(4, -1)
    pieces = [_unpack_flat(wg[k], big_shard_shapes) for k in range(4)]
    full = {n: jnp.concatenate([pieces[k][i] for k in range(4)], axis=big_axes[i]) for i, n in enumerate(big_names)}
    n_even = full['even_w_in'].shape[1]
    n_even_pad = -(-n_even // LANES) * LANES
    w_in = jnp.concatenate([full['even_w_in'], jnp.zeros((d, n_even_pad - n_even), MXU_DTYPE)], axis=1)
    w_out, w_oin, w_oout = full['even_w_out'], full['odd_w_in'], full['odd_w_out']
    w_up, w_down = full['ffn_w_up'], full['ffn_w_down']
    scal_blk = (n_even // LANES)
    n_scal = n_even - scal_blk * LANES

    def mod(layer, k):
        return mod_me[layer, k]

    avec = jnp.zeros((1, LANES), F32).at[0, n_scal // 2:n_scal].set(gdn_a_log.reshape(-1))
    dtvec = jnp.zeros((1, LANES), F32).at[0, n_scal // 2:n_scal].set(gdn_dt_bias.reshape(-1))
    normw = gdn_norm_w[None]
    pscale = pool_scale[None]
    cg, cb = conf_ln_g[None], conf_ln_b[None]
    lng = lambda l, k: ln_g_f[l, k][None]
    lnb = lambda l, k: ln_b_f[l, k][None]
    convw9 = ffn_conv_f.reshape(depth, 9, -1)
    nqkv = gdn_conv_f.shape[1]

    p, ub0 = _inproj("even_in", x0, mod(0, 0), mod(0, 1), w_in, 512)
    pc, ucb = _inproj("even_in_ctx", ctx0, sh_c, sc_c, w_in, 256)
    qkv = _qkv_conv("qkv_conv", p, gdn_conv_f)
    qkv_c = _qkv_conv("qkv_conv_ctx", pc, gdn_conv_f)
    gb = _gates("gates", p, avec, dtvec, scal_blk)
    gb_c = _gates("gates_ctx", pc, avec, dtvec, scal_blk)
    lay = _gate_layouts(gb)
    lay_c = _gate_layouts(gb_c)
    s_zero = jnp.zeros((2 * GDN_HEADS, HEAD_DIM, HEAD_DIM), F32)
    _, _, save_c, s_ctx = _scan_fwd("scan_ctx", qkv_c, *lay_c, s_zero)
    o0, o1, save_l, _ = _scan_fwd("scan", qkv, *lay, s_ctx)
    pool_blk = (nqkv + GDN_HEADS * HEAD_DIM) // 512
    ypool = _pool(p, pool_w, pscale, pool_blk)
    x1, mix0, y0 = _even_out(o0, o1, p, ypool, x0, normw, mod(0, 2), lng(0, 0), lnb(0, 0), w_out, 256)

    def ffn_fwd(l, xin):
        h, ub = _inproj(f"ffn_up{l}", xin, mod(l, 3), mod(l, 4), w_up[l], 256)
        s = _ffn_conv(f"ffn_conv{l}", h, convw9[l])
        xo, y = _ffn_down(f"ffn_down{l}", s, xin, mod(l, 5), lng(l, 1), lnb(l, 1), w_down[l], 256)
        return xo, (h, ub, s, y)

    x2, ffn0 = ffn_fwd(0, x1)
    p1, ub1 = _inproj("odd_in", x2, mod(1, 0), mod(1, 1), w_oin, 512)
    nsc = sconv_f.shape[1] // LANES
    ysc = _sconv(p1, sconv_f)
    z0 = _confconv(p1, conf_conv_f, 3 * nsc)
    x3, mix1, y1 = _odd_out(ysc, z0, x2, cg, cb, mod(1, 2), lng(1, 0), lnb(1, 0), w_oout, 256)
    x4, ffn1 = ffn_fwd(1, x3)
    loss_part, dx4 = _loss_and_grad(x4, tgt, 512)
    loss = lax.psum(loss_part[0, 0], ("x", "y", "c"))

    dmod = [[None] * 6 for _ in range(depth)]
    dlng = [[None, None] for _ in range(depth)]
    dlnb = [[None, None] for _ in range(depth)]
    gbig = {}
    dconv9 = [None] * depth

    def ffn_bwd(l, dxo, xin, saved):
        h, ub, s, y = saved
        ds, dxa, dyb, dgt, dg_, db_ = _ffn_down_bwd(f"ffn_down_bwd{l}", dxo, xin, y, mod(l, 5), lng(l, 1), lnb(l, 1), w_down[l], 256)
        da, dgate, dw9 = _ffn_conv_bwd(f"ffn_conv_bwd{l}", ds, h, convw9[l])
        dh = jnp.concatenate([da, dgate], axis=1)
        dxin, dsh, dsc = _inproj_bwd(f"ffn_up_bwd{l}", dh, xin, mod(l, 3), mod(l, 4), w_up[l], dxa, 256)
        dmod[l][3], dmod[l][4], dmod[l][5] = dsh, dsc, dgt
        dlng[l][1], dlnb[l][1] = dg_, db_
        dconv9[l] = dw9
        return dxin, _matmul_tn(f"dw_up{l}", ub, dh), _matmul_tn(f"dw_down{l}", s, dyb)

    dx3, dwu1, dwd1 = ffn_bwd(1, dx4, x3, ffn1)
    dysc, dz0, dx2a, dyb1, dcg, dcb, dgt, dg_, db_ = _odd_out_bwd(dx3, ysc, z0, x2, y1, cg, cb, mod(1, 2), lng(1, 0), lnb(1, 0), w_oout, 256)
    dmod[1][2], dlng[1][0], dlnb[1][0] = dgt, dg_, db_
    d_gb, d_gc, d_h, dsconv = _sconv_bwd(dysc, p1, sconv_f)
    d_ga, d_gbb, dconf = _confconv_bwd(dz0, p1, conf_conv_f, 3 * nsc)
    dp1 = jnp.concatenate([d_gb, d_gc, d_h, d_ga, d_gbb], axis=1)
    dx2, dsh, dsc = _inproj_bwd("odd_in_bwd", dp1, x2, mod(1, 0), mod(1, 1), w_oin, dx2a, 512)
    dmod[1][0], dmod[1][1] = dsh, dsc
    gbig['odd_w_out'] = _matmul_tn("dw_oout", mix1, dyb1)
    gbig['odd_w_in'] = _matmul_tn("dw_oin", ub1, dp1)

    dx1, dwu0, dwd0 = ffn_bwd(0, dx2, x1, ffn0)
    gbig['ffn_w_up'] = jnp.stack([dwu0, dwu1])
    gbig['ffn_w_down'] = jnp.stack([dwd0, dwd1])

    do, dpg, dypool, dx0a, dyb0, dnormw, dgt, dg_, db_ = _even_out_bwd(dx1, o0, o1, p, ypool, x0, y0, normw, mod(0, 2), lng(0, 0), lnb(0, 0), w_out, 256)
    dmod[0][2], dlng[0][0], dlnb[0][0] = dgt, dg_, db_
    pool_cts = _pool_bwd(dypool, p, pool_w, pscale, pool_blk)
    dpp, dpool_scale, dpool_w = pool_cts[0], pool_cts[1], jnp.stack(pool_cts[2:])
    dqkv0, dqkv1, dbcol, dgcol, dgrow, ds0 = _scan_bwd("scan_bwd", do, qkv, *lay, save_l, s_zero)
    zero_do = jnp.zeros((tc, GDN_HEADS * HEAD_DIM), F32)
    dqkv0_c, dqkv1_c, dbcol_c, dgcol_c, dgrow_c, _ = _scan_bwd("scan_bwd_ctx", zero_do, qkv_c, *lay_c, save_c, ds0)
    dgb = _gate_layouts_bwd(dbcol, dgcol, dgrow)
    dgb_c = _gate_layouts_bwd(dbcol_c, dgcol_c, dgrow_c)
    dps, davec, ddtvec = _gates_bwd("gates_bwd", dgb, p, avec, dtvec, scal_blk)
    dps_c, davec_c, ddtvec_c = _gates_bwd("gates_bwd_ctx", dgb_c, pc, avec, dtvec, scal_blk)
    dpqkv, dconv5 = _qkv_conv_bwd("qkv_conv_bwd", dqkv0, dqkv1, p, gdn_conv_f)
    dpqkv_c, dconv5_c = _qkv_conv_bwd("qkv_conv_bwd_ctx", dqkv0_c, dqkv1_c, pc, gdn_conv_f)
    dp = jnp.concatenate([dpqkv, dpg, dpp, dps], axis=1)
    dpc = jnp.concatenate([dpqkv_c, jnp.zeros((tc, n_even_pad - nqkv - LANES), MXU_DTYPE), dps_c], axis=1)
    grad_x, dsh, dsc = _inproj_bwd("even_in_bwd", dp, x0, mod(0, 0), mod(0, 1), w_in, dx0a, 512)
    dmod[0][0], dmod[0][1] = dsh, dsc
    _, dsh_c, dsc_c = _inproj_bwd("even_in_bwd_ctx", dpc, ctx0, sh_c, sc_c, w_in, None, 256)
    gbig['even_w_in'] = _matmul_tn("dw_in", jnp.concatenate([ub0, ucb], axis=0), jnp.concatenate([dp, dpc], axis=0))[:, :n_even]
    gbig['even_w_out'] = _matmul_tn("dw_out", mix0, dyb0)

    def shard_of(g, k, axis, n):
        return lax.slice_in_dim(g, k * n, (k + 1) * n, axis=axis)

    gpack = jnp.stack([
        _pack_flat([shard_of(gbig[n], k, big_axes[i], big_shard_shapes[i][big_axes[i]]) for i, n in enumerate(big_names)],
                   half_mult, MXU_DTYPE).reshape(2, rh, LANES) for k in range(4)], axis=1)
    keep = lax.dynamic_index_in_dim(gpack, ic, axis=0, keepdims=False)
    give = lax.dynamic_index_in_dim(gpack, 1 - ic, axis=0, keepdims=False)
    got = _sibling_exchange("grad_pair_exchange", give)
    pair = _sum_leading("grad_pair_sum", jnp.stack([keep.reshape(-1, LANES), got.reshape(-1, LANES)]), MXU_DTYPE)
    parts = _chip_scatter("grad_chip_scatter", pair.reshape(4, rh, LANES))
    gsum = _sum_leading("grad_chip_sum", jnp.transpose(parts, (1, 0, 2, 3)).reshape(4, 2 * rh, LANES)).reshape(-1)
    g_shards = dict(zip(big_names, _unpack_flat(gsum, big_shard_shapes)))

    dmod_rows = jnp.stack([jnp.concatenate(dmod[l], axis=1)[0] for l in range(depth)])
    dmod_c = jnp.concatenate([dsh_c[0], dsc_c[0], jnp.zeros((4 * d,), F32)])
    dmod_c_rows = jnp.stack([dmod_c] + [jnp.zeros_like(dmod_c)] * (depth - 1))
    small_g = {
        'ln_g': jnp.stack([jnp.stack([dlng[l][k][0] for k in range(2)]) for l in range(depth)]),
        'ln_b': jnp.stack([jnp.stack([dlnb[l][k][0] for k in range(2)]) for l in range(depth)]),
        'gdn_conv_w': dconv5 + dconv5_c,
        'gdn_a_log': (davec + davec_c)[0, n_scal // 2:n_scal].reshape(gdn_a_log.shape),
        'gdn_dt_bias': (ddtvec + ddtvec_c)[0, n_scal // 2:n_scal].reshape(gdn_dt_bias.shape),
        'gdn_norm_w': dnormw[0], 'pool_w': dpool_w, 'pool_scale': dpool_scale[0],
        'sconv_w': dsconv, 'conf_conv_w': dconf, 'conf_ln_g': dcg[0], 'conf_ln_b': dcb[0],
        'ffn_conv_w': jnp.stack(dconv9).reshape(depth, 3, 3, -1),
    }
    small_names = list(small_g)
    s3_items = [dmod_rows, dmod_c_rows] + [small_g[n] for n in small_names]
    s3 = _pack_flat(s3_items, 8 * LANES, F32).reshape(-1, LANES)
    g3 = _all_gather8("gather_small_grads", s3, True).reshape(8, -1, LANES)
    tot3 = _sum_leading("small_grad_sum", g3).reshape(-1)
    tot_items = _unpack_flat(tot3, [a.shape for a in s3_items])
    dmod_sum, dmod_c_sum = tot_items[0], tot_items[1]
    small_tot = dict(zip(small_names, tot_items[2:]))
    grad_ada_b = dmod_sum + dmod_c_sum
    g3f = g3.reshape(8, -1)
    rows_all = g3f[:, :depth * 6 * d].reshape(8, depth, 6 * d)
    cols = lax.dynamic_slice_in_dim(rows_all, chip * n_ada, n_ada, axis=2)
    crow = lax.dynamic_slice_in_dim(dmod_c_sum, chip * n_ada, n_ada, axis=1)
    dmod16 = jnp.concatenate([jnp.transpose(cols, (1, 0, 2)), crow[:, None, :], jnp.zeros((depth, 7, n_ada), F32)], axis=1)
    grad_ada_w, dsil = _ada_grads(c16, dmod16, ada_w)
    s4 = jnp.concatenate([dsil[0, 8][None], jnp.zeros((7, d), F32)], axis=0).reshape(-1, LANES)
    g4 = _all_gather8("gather_cctx", s4, True).reshape(8, 8, d)
    grad_c_ctx = _cctx_grad(g4[0::2, 0][:, None, :], c_ctx[None])[0]

    def my_cols(a, n):
        return lax.dynamic_slice_in_dim(a, chip * n, n, axis=a.ndim - 1)

    grads = dict(g_shards)
    grads['c_ctx'] = grad_c_ctx
    grads['ada_w'] = grad_ada_w
    grads['ada_b'] = grad_ada_b
    for n in ['ln_g', 'ln_b', 'gdn_conv_w', 'sconv_w', 'conf_conv_w', 'ffn_conv_w']:
        grads[n] = my_cols(small_tot[n], wts[n].shape[-1])
    for n in ['gdn_a_log', 'gdn_dt_bias', 'gdn_norm_w', 'pool_w', 'pool_scale', 'conf_ln_g', 'conf_ln_b']:
        grads[n] = small_tot[n]

    delta, new_m, new_v = {}, {}, {}
    big_adam = big_names + ['ada_w']
    for n in big_adam:
        shp = wts[n].shape
        as2d = lambda a: a.reshape(-1, shp[-1])
        dl, nm, nv = _adamw("adamw_" + n, as2d(wts[n]), as2d(grads[n]), as2d(mom[n]), as2d(var[n]))
        delta[n], new_m[n], new_v[n] = dl.reshape(shp), nm.reshape(shp), nv.reshape(shp)
    small_adam = [n for n in names if n not in big_adam]
    packs = [_pack_flat([src[n] for n in small_adam], 8 * LANES, F32).reshape(-1, LANES) for src in (wts, grads, mom, var)]
    outs = _adamw("adamw_small", *packs)
    shapes = [wts[n].shape for n in small_adam]
    for res, dst in zip(outs, (delta, new_m, new_v)):
        for n, a in zip(small_adam, _unpack_flat(res.reshape(-1), shapes)):
            dst[n] = a

    return (loss, grad_x[None], *[grads[n] for n in names], *[delta[n] for n in names],
            *[new_m[n] for n in names], *[new_v[n] for n in names])
```

```python
import functools
import math

import jax
import jax.numpy as jnp
from jax import lax
from jax.experimental import pallas as pl
from jax.experimental.pallas import tpu as pltpu

F32 = jnp.float32
MXU_DTYPE = jnp.bfloat16
HIGHEST = lax.Precision.HIGHEST
MESH = pl.DeviceIdType.MESH

D_MODEL = 1024
GRID_W = 64
GDN_HEADS = 4
HEAD_DIM = 128
GDN_CHUNK = 64
POOL_WINDOWS = (2, 4, 8, 16)
GDN_CONV = 5
SC_WIDTH = 3
CF_WIDTH = 31
ALPHA = 4.0 ** 0.25
LN_EPS = 1e-5
RMS_EPS = 1e-6
LANES = 128
VMEM_LIMIT_BYTES = 58 * 1024 * 1024

ADAM_LR, ADAM_B1, ADAM_B2, ADAM_EPS, ADAM_WD, ADAM_STEP = 0.001, 0.9, 0.999, 1e-08, 0.01, 10


def _blocked(name, fn, grid, ins, in_specs, out_shapes, out_specs, acc=(), acc_axis=None, scratch=()):
    n_in = len(ins)
    n_out = len(out_shapes)

    def body(*refs):
        vals = [r[...] for r in refs[:n_in]]
        res = fn(*vals, *refs[n_in + n_out:])
        if not isinstance(res, (tuple, list)):
            res = (res,)
        for k, (r, v) in enumerate(zip(refs[n_in:n_in + n_out], res)):
            if k in acc:
                first = pl.program_id(acc_axis) == 0

                @pl.when(first)
                def _(r=r, v=v):
                    r[...] = v.astype(r.dtype)

                @pl.when(jnp.logical_not(first))
                def _(r=r, v=v):
                    r[...] += v.astype(r.dtype)
            else:
                r[...] = v.astype(r.dtype)

    return pl.pallas_call(
        body, name=name, grid=grid, in_specs=in_specs, out_specs=out_specs, out_shape=out_shapes,
        scratch_shapes=list(scratch),
        compiler_params=pltpu.CompilerParams(dimension_semantics=("arbitrary",) * len(grid),
                                             vmem_limit_bytes=VMEM_LIMIT_BYTES),
    )(*ins)


def _sds(shape, dtype=F32):
    return jax.ShapeDtypeStruct(tuple(shape), dtype)


def _tok(tm, n, col=0):
    return pl.BlockSpec((tm, n), lambda t: (t, col))


def _res(shape):
    nd = len(shape)
    return pl.BlockSpec(tuple(shape), lambda t: (0,) * nd)


def _silu(x):
    return x * jax.nn.sigmoid(x)


def _layernorm(r, g, b):
    mu = jnp.mean(r, -1, keepdims=True)
    d = r - mu
    var = jnp.mean(d * d, -1, keepdims=True)
    return d * lax.rsqrt(var + LN_EPS) * g + b


def _mm_nn_impl(a, b):
    return jnp.dot(a.astype(MXU_DTYPE), b.astype(MXU_DTYPE), preferred_element_type=F32)


def _mm_nt_impl(a, b):
    return lax.dot_general(a.astype(MXU_DTYPE), b.astype(MXU_DTYPE), (((1,), (1,)), ((), ())), preferred_element_type=F32)


def _mm_tn_impl(a, b):
    return lax.dot_general(a.astype(MXU_DTYPE), b.astype(MXU_DTYPE), (((0,), (0,)), ((), ())), preferred_element_type=F32)


def _mm_hi_impl(a, b):
    return jnp.dot(a, b, preferred_element_type=F32, precision=HIGHEST)


def _mm_vjp(mm, mm_da, mm_db):
    f = jax.custom_vjp(mm)
    f.defvjp(lambda a, b: (mm(a, b), (a, b)), lambda res, g: (mm_da(g, res[1]), mm_db(res[0], g)))
    return f


_mm = _mm_vjp(_mm_nn_impl, lambda g, b: _mm_nt_impl(g, b), lambda a, g: _mm_tn_impl(a, g))
_mm_nt = _mm_vjp(_mm_nt_impl, lambda g, b: _mm_nn_impl(g, b), lambda a, g: _mm_tn_impl(g, a))
_mm_tn = _mm_vjp(_mm_tn_impl, lambda g, b: _mm_nt_impl(b, g), lambda a, g: _mm_nn_impl(a, g))
_mm_hi = _mm_vjp(
    _mm_hi_impl,
    lambda g, b: lax.dot_general(g, b, (((1,), (1,)), ((), ())), preferred_element_type=F32, precision=HIGHEST),
    lambda a, g: lax.dot_general(a, g, (((0,), (0,)), ((), ())), preferred_element_type=F32, precision=HIGHEST))


def _row(w, k):
    rows = lax.broadcasted_iota(jnp.int32, w.shape, 0)
    return jnp.sum(jnp.where(rows == k, w, 0.0), axis=0, keepdims=True)


def _col_mask(shape, dc):
    col = lax.broadcasted_iota(jnp.int32, shape, 0) & (GRID_W - 1)
    return (col + dc >= 0) & (col + dc < GRID_W)


def _center(x_ext, halo):
    return x_ext[halo:x_ext.shape[0] - halo]


def _make_dwconv(taps, halo):
    assert all(abs(s) <= halo for s, _ in taps)

    def shifted(x_ext, s):
        r = x_ext if s == 0 else pltpu.roll(x_ext, (-s) % x_ext.shape[0], 0)
        return _center(r, halo)

    @jax.custom_vjp
    def conv(x_ext, w):
        acc = None
        for k, (s, dc) in enumerate(taps):
            r = shifted(x_ext, s)
            if dc != 0:
                r = jnp.where(_col_mask(r.shape, dc), r, 0.0)
            term = r * _row(w, k)
            acc = term if acc is None else acc + term
        return acc

    def fwd(x_ext, w):
        return conv(x_ext, w), (x_ext, w)

    def bwd(res, dy):
        x_ext, w = res
        n = x_ext.shape[0]
        rows = lax.broadcasted_iota(jnp.int32, w.shape, 0)
        pad = jnp.zeros((halo, dy.shape[1]), F32)
        dx = None
        dw = jnp.zeros(w.shape, F32)
        for k, (s, dc) in enumerate(taps):
            dym = dy if dc == 0 else jnp.where(_col_mask(dy.shape, dc), dy, 0.0)
            dw = dw + jnp.where(rows == k, jnp.sum(dym * shifted(x_ext, s), axis=0, keepdims=True), 0.0)
            t = jnp.concatenate([pad, dym * _row(w, k), pad], axis=0)
            if s != 0:
                t = pltpu.roll(t, s % n, 0)
            dx = t if dx is None else dx + t
        return dx, dw

    conv.defvjp(fwd, bwd)
    return conv


def _taps_1d(width):
    return tuple((k - width // 2, 0) for k in range(width))


HALO_SHORT = 8
HALO_CONF = 16
HALO_GRID = 72
_conv5 = _make_dwconv(_taps_1d(GDN_CONV), HALO_SHORT)
_conv3 = _make_dwconv(_taps_1d(SC_WIDTH), HALO_SHORT)
_conv31 = _make_dwconv(_taps_1d(CF_WIDTH), HALO_CONF)
_conv3x3 = _make_dwconv(tuple((dr * GRID_W + dc, dc) for dr in (-1, 0, 1) for dc in (-1, 0, 1)), HALO_GRID)
_pool_sums = {w: _make_dwconv(tuple((s, 0) for s in range(-(w // 2), w - w // 2)), HALO_SHORT) for w in POOL_WINDOWS}


def _all_gather8(name, blk, in_vmem):
    m_per, n = blk.shape
    space = pltpu.VMEM if in_vmem else pl.ANY

    def body(x_ref, out_ref, send_sems, recv_sems, local_sem):
        x, y, c = lax.axis_index("x"), lax.axis_index("y"), lax.axis_index("c")
        me, sibling = (x, y, c), (x, y, 1 - c)
        chips = [(1 - x, y), (x, 1 - y), (1 - x, 1 - y)]

        def rows(px, py, pc):
            return out_ref.at[pl.ds((4 * px + 2 * py + pc) * m_per, m_per), :]

        def copy(k, block, to, src=None):
            return pltpu.make_async_remote_copy(
                src_ref=rows(*block) if src is None else src, dst_ref=rows(*block),
                send_sem=send_sems.at[k], recv_sem=recv_sems.at[k], device_id=to, device_id_type=MESH)

        mine = pltpu.make_async_copy(x_ref, rows(*me), local_sem)
        mine.start()
        first = [copy(0, me, sibling, src=x_ref)]
        first += [copy(1 + j, me, (*chip, c), src=x_ref) for j, chip in enumerate(chips)]
        for cp in first:
            cp.start()
        passed = [copy(4 + j, (*chip, c), sibling) for j, chip in enumerate(chips)]
        for j, chip in enumerate(chips):
            copy(1 + j, (*chip, c), me).wait_recv()
            passed[j].start()
        copy(0, sibling, me).wait_recv()
        for j, chip in enumerate(chips):
            copy(4 + j, (*chip, 1 - c), me).wait_recv()
        for cp in first + passed:
            cp.wait_send()
        mine.wait()

    return pl.pallas_call(
        body, name=name, out_shape=_sds((8 * m_per, n), blk.dtype),
        in_specs=[pl.BlockSpec(memory_space=space)], out_specs=pl.BlockSpec(memory_space=space),
        scratch_shapes=[pltpu.SemaphoreType.DMA((7,)), pltpu.SemaphoreType.DMA((7,)), pltpu.SemaphoreType.DMA],
        compiler_params=pltpu.CompilerParams(vmem_limit_bytes=VMEM_LIMIT_BYTES),
    )(blk)


def _sibling_exchange(name, send):
    def body(s_ref, out_ref, send_sem, recv_sem):
        x, y, c = lax.axis_index("x"), lax.axis_index("y"), lax.axis_index("c")
        cp = pltpu.make_async_remote_copy(src_ref=s_ref, dst_ref=out_ref, send_sem=send_sem, recv_sem=recv_sem,
                                          device_id=(x, y, 1 - c), device_id_type=MESH)
        cp.start()
        cp.wait()

    return pl.pallas_call(
        body, name=name, out_shape=_sds(send.shape, send.dtype),
        in_specs=[pl.BlockSpec(memory_space=pl.ANY)], out_specs=pl.BlockSpec(memory_space=pl.ANY),
        scratch_shapes=[pltpu.SemaphoreType.DMA, pltpu.SemaphoreType.DMA],
    )(send)


def _chip_scatter(name, part):
    _, r, n = part.shape

    def body(p_ref, out_ref, send_sems, recv_sems, local_sem):
        x, y, c = lax.axis_index("x"), lax.axis_index("y"), lax.axis_index("c")
        sibling = (x, y, 1 - c)
        me_chip = 2 * x + y
        chips = [(1 - x, y), (x, 1 - y), (1 - x, 1 - y)]

        def chip_id(chip):
            return 2 * chip[0] + chip[1]

        def copy(k, src, dst, to):
            return pltpu.make_async_remote_copy(src_ref=src, dst_ref=dst, send_sem=send_sems.at[k],
                                                recv_sem=recv_sems.at[k], device_id=to, device_id_type=MESH)

        mine = pltpu.make_async_copy(p_ref.at[me_chip], out_ref.at[c, me_chip], local_sem)
        mine.start()
        first = [copy(0, p_ref.at[me_chip], out_ref.at[c, me_chip], sibling)]
        first += [copy(1 + j, p_ref.at[chip_id(chip)], out_ref.at[c, me_chip], (*chip, c))
                  for j, chip in enumerate(chips)]
        for cp in first:
            cp.start()
        passed = [copy(4 + j, out_ref.at[c, chip_id(chip)], out_ref.at[c, chip_id(chip)], sibling)
                  for j, chip in enumerate(chips)]
        for j, chip in enumerate(chips):
            copy(1 + j, p_ref.at[0], out_ref.at[c, chip_id(chip)], sibling).wait_recv()
            passed[j].start()
        copy(0, p_ref.at[0], out_ref.at[1 - c, me_chip], sibling).wait_recv()
        for j, chip in enumerate(chips):
            copy(4 + j, p_ref.at[0], out_ref.at[1 - c, chip_id(chip)], sibling).wait_recv()
        for cp in first + passed:
            cp.wait_send()
        mine.wait()

    return pl.pallas_call(
        body, name=name, out_shape=_sds((2, 4, r, n), part.dtype),
        in_specs=[pl.BlockSpec(memory_space=pl.ANY)], out_specs=pl.BlockSpec(memory_space=pl.ANY),
        scratch_shapes=[pltpu.SemaphoreType.DMA((7,)), pltpu.SemaphoreType.DMA((7,)), pltpu.SemaphoreType.DMA],
    )(part)


def _pack_flat(arrays, multiple, dtype):
    flat = jnp.concatenate([a.reshape(-1).astype(dtype) for a in arrays])
    pad = (-flat.shape[0]) % multiple
    if pad:
        flat = jnp.concatenate([flat, jnp.zeros((pad,), dtype)])
    return flat


def _unpack_flat(flat, shapes):
    out, off = [], 0
    for s in shapes:
        n = math.prod(s)
        out.append(flat[off:off + n].reshape(s))
        off += n
    return out


def _row_tile(r, cap, mult=16):
    for cand in range(min(cap, r) // mult * mult, 0, -mult):
        if r % cand == 0:
            return cand
    return r


def _sum_leading(name, stack, out_dtype=F32):
    k, r, n = stack.shape
    tr = _row_tile(r, 2048)

    def fn(s):
        acc = s[0].astype(F32)
        for i in range(1, k):
            acc = acc + s[i].astype(F32)
        return acc

    return _blocked(name, fn, (r // tr,), [stack], [pl.BlockSpec((k, tr, n), lambda t: (0, t, 0))],
                    [_sds((r, n), out_dtype)], [pl.BlockSpec((tr, n), lambda t: (t, 0))])[0]


def _adamw(name, w, g, m, v):
    r, c = w.shape
    tr = _row_tile(r, max(8, (1 << 21) // (4 * c)), 8)
    bc1 = 1.0 - ADAM_B1 ** ADAM_STEP
    bc2 = 1.0 - ADAM_B2 ** ADAM_STEP

    def fn(w, g, m, v):
        nm = ADAM_B1 * m + (1.0 - ADAM_B1) * g
        nv = ADAM_B2 * v + (1.0 - ADAM_B2) * (g * g)
        delta = -ADAM_LR * ((nm / bc1) / (jnp.sqrt(nv / bc2) + ADAM_EPS) + ADAM_WD * w)
        return delta, nm, nv

    spec = pl.BlockSpec((tr, c), lambda t: (t, 0))
    return _blocked(name, fn, (r // tr,), [w, g, m, v], [spec] * 4, [_sds((r, c))] * 3, [spec] * 3)


def _mod_rows(c16, ada_w):
    depth, d, n = ada_w.shape

    def fn(c, w):
        return _mm(_silu(c), w)

    return _blocked("mod_rows", fn, (depth,), [c16, ada_w],
                    [_res(c16.shape), pl.BlockSpec((None, d, n), lambda l: (l, 0, 0))],
                    [_sds((depth, 16, n))], [pl.BlockSpec((None, 16, n), lambda l: (l, 0, 0))])[0]


def _ada_grads(c16, dmod16, ada_w):
    depth, d, n = ada_w.shape

    def fn(c, dm, w):
        return _mm_tn(_silu(c), dm), _mm_nt(dm, w)

    return _blocked("ada_grads", fn, (depth,), [c16, dmod16, ada_w],
                    [_res(c16.shape), pl.BlockSpec((None, 16, n), lambda l: (l, 0, 0)),
                     pl.BlockSpec((None, d, n), lambda l: (l, 0, 0))],
                    [_sds((depth, d, n)), _sds((depth, 16, d))],
                    [pl.BlockSpec((None, d, n), lambda l: (l, 0, 0)), pl.BlockSpec((None, 16, d), lambda l: (l, 0, 0))])


def _cctx_grad(parts, c_ctx_row):
    def fn(p, c):
        tot = ((p[0] + p[1]) + p[2]) + p[3]
        _, vjp = jax.vjp(_silu, c)
        return vjp(tot)[0]

    return _blocked("cctx_grad", fn, (1,), [parts, c_ctx_row], [_res(parts.shape), _res(c_ctx_row.shape)],
                    [_sds(c_ctx_row.shape)], [_res(c_ctx_row.shape)])[0]


def _modulate(x, sh, sc):
    return x * (1.0 + sc) + sh


def _inproj(name, x, sh, sc, w, tm):
    t, d = x.shape
    n = w.shape[1]
    tm = min(tm, t)

    def fn(x, sh, sc, w):
        u = _modulate(x, sh, sc).astype(MXU_DTYPE)
        return jnp.dot(u, w, preferred_element_type=F32), u

    return _blocked(name, fn, (t // tm,), [x, sh, sc, w],
                    [_tok(tm, d), _res(sh.shape), _res(sc.shape), _res(w.shape)],
                    [_sds((t, n)), _sds((t, d), MXU_DTYPE)], [_tok(tm, n), _tok(tm, d)])


def _inproj_bwd(name, dp, x, sh, sc, w, add, tm):
    t, d = x.shape
    n = w.shape[1]
    tm = min(tm, t)
    has_add = add is not None

    def fn(dp, x, sh, sc, w, *rest):
        du = _mm_nt(dp, w)
        _, vjp = jax.vjp(_modulate, x, sh, sc)
        dx, dsh, dsc = vjp(du)
        if has_add:
            dx = dx + rest[0]
        return dx, dsh, dsc

    ins = [dp, x, sh, sc, w] + ([add] if has_add else [])
    specs = [_tok(tm, n), _tok(tm, d), _res(sh.shape), _res(sc.shape), _res(w.shape)] + ([_tok(tm, d)] if has_add else [])
    return _blocked(name, fn, (t // tm,), ins, specs,
                    [_sds((t, d)), _sds(sh.shape), _sds(sc.shape)], [_tok(tm, d), _res(sh.shape), _res(sc.shape)],
                    acc=(1, 2), acc_axis=0)


def _residual_ln(y, x, gt, lng, lnb):
    return _layernorm(ALPHA * x + gt * y, lng, lnb)


def _gdn_mix(o0, o1, pg, yp, normw):
    o = o0 + o1
    heads = []
    for h in range(GDN_HEADS):
        oh = o[:, h * HEAD_DIM:(h + 1) * HEAD_DIM]
        heads.append(oh * lax.rsqrt(jnp.mean(oh * oh, -1, keepdims=True) + RMS_EPS) * normw)
    on = jnp.concatenate(heads, axis=-1) * _silu(pg)
    return jnp.concatenate([on, yp], axis=-1)


def _even_out(o0, o1, p, ypool, x, normw, gt, lng, lnb, w, tm):
    t, d = x.shape
    tm = min(tm, t)
    gate_blk = 3

    def fn(o0, o1, pg, yp, x, normw, gt, lng, lnb, w):
        mix = _gdn_mix(o0, o1, pg, yp, normw).astype(MXU_DTYPE)
        y = jnp.dot(mix, w, preferred_element_type=F32)
        return _residual_ln(y, x, gt, lng, lnb), mix, y

    return _blocked("even_out", fn, (t // tm,), [o0, o1, p, ypool, x, normw, gt, lng, lnb, w],
                    [_tok(tm, 512), _tok(tm, 512), _tok(tm, 512, gate_blk), _tok(tm, 512), _tok(tm, d),
                     _res(normw.shape), _res(gt.shape), _res(lng.shape), _res(lnb.shape), _res(w.shape)],
                    [_sds((t, d)), _sds((t, d), MXU_DTYPE), _sds((t, d))], [_tok(tm, d)] * 3)


def _even_out_bwd(dx1, o0, o1, p, ypool, x, y, normw, gt, lng, lnb, w, tm):
    t, d = x.shape
    tm = min(tm, t)

    def fn(dx1, o0, o1, pg, yp, x, y, normw, gt, lng, lnb, w):
        _, vjp2 = jax.vjp(_residual_ln, y, x, gt, lng, lnb)
        dy, dx, dgt, dlng, dlnb = vjp2(dx1)
        dyb = dy.astype(MXU_DTYPE)
        dmix = _mm_nt(dyb, w)
        _, vjp1 = jax.vjp(_gdn_mix, o0, o1, pg, yp, normw)
        do, _, dpg, dyp, dnormw = vjp1(dmix)
        return do, dpg, dyp, dx, dyb, dnormw, dgt, dlng, dlnb

    return _blocked("even_out_bwd", fn, (t // tm,), [dx1, o0, o1, p, ypool, x, y, normw, gt, lng, lnb, w],
                    [_tok(tm, d), _tok(tm, 512), _tok(tm, 512), _tok(tm, 512, 3), _tok(tm, 512),
                     _tok(tm, d), _tok(tm, d), _res(normw.shape), _res(gt.shape), _res(lng.shape), _res(lnb.shape),
                     _res(w.shape)],
                    [_sds((t, 512)), _sds((t, 512), MXU_DTYPE), _sds((t, 512)), _sds((t, d)), _sds((t, d), MXU_DTYPE),
                     _sds(normw.shape), _sds(gt.shape), _sds(lng.shape), _sds(lnb.shape)],
                    [_tok(tm, 512), _tok(tm, 512), _tok(tm, 512), _tok(tm, d), _tok(tm, d),
                     _res(normw.shape), _res(gt.shape), _res(lng.shape), _res(lnb.shape)],
                    acc=(5, 6, 7, 8), acc_axis=0)


def _odd_mix(ysc, z0, cg, cb):
    z = _silu(_layernorm(z0, cg, cb))
    return jnp.concatenate([ysc, z], axis=-1)


def _odd_out(ysc, z0, x, cg, cb, gt, lng, lnb, w, tm):
    t, d = x.shape
    tm = min(tm, t)

    def fn(ysc, z0, x, cg, cb, gt, lng, lnb, w):
        mix = _odd_mix(ysc, z0, cg, cb).astype(MXU_DTYPE)
        y = jnp.dot(mix, w, preferred_element_type=F32)
        return _residual_ln(y, x, gt, lng, lnb), mix, y

    return _blocked("odd_out", fn, (t // tm,), [ysc, z0, x, cg, cb, gt, lng, lnb, w],
                    [_tok(tm, 512), _tok(tm, 512), _tok(tm, d), _res(cg.shape), _res(cb.shape), _res(gt.shape),
                     _res(lng.shape), _res(lnb.shape), _res(w.shape)],
                    [_sds((t, d)), _sds((t, d), MXU_DTYPE), _sds((t, d))], [_tok(tm, d)] * 3)


def _odd_out_bwd(dx3, ysc, z0, x, y, cg, cb, gt, lng, lnb, w, tm):
    t, d = x.shape
    tm = min(tm, t)

    def fn(dx3, ysc, z0, x, y, cg, cb, gt, lng, lnb, w):
        _, vjp2 = jax.vjp(_residual_ln, y, x, gt, lng, lnb)
        dy, dx, dgt, dlng, dlnb = vjp2(dx3)
        dyb = dy.astype(MXU_DTYPE)
        dmix = _mm_nt(dyb, w)
        _, vjp1 = jax.vjp(_odd_mix, ysc, z0, cg, cb)
        dysc, dz0, dcg, dcb = vjp1(dmix)
        return dysc, dz0, dx, dyb, dcg, dcb, dgt, dlng, dlnb

    return _blocked("odd_out_bwd", fn, (t // tm,), [dx3, ysc, z0, x, y, cg, cb, gt, lng, lnb, w],
                    [_tok(tm, d), _tok(tm, 512), _tok(tm, 512), _tok(tm, d), _tok(tm, d), _res(cg.shape), _res(cb.shape),
                     _res(gt.shape), _res(lng.shape), _res(lnb.shape), _res(w.shape)],
                    [_sds((t, 512)), _sds((t, 512)), _sds((t, d)), _sds((t, d), MXU_DTYPE),
                     _sds(cg.shape), _sds(cb.shape), _sds(gt.shape), _sds(lng.shape), _sds(lnb.shape)],
                    [_tok(tm, 512), _tok(tm, 512), _tok(tm, d), _tok(tm, d),
                     _res(cg.shape), _res(cb.shape), _res(gt.shape), _res(lng.shape), _res(lnb.shape)],
                    acc=(4, 5, 6, 7, 8), acc_axis=0)


def _ffn_down(name, s, x, gt, lng, lnb, w, tm):
    t, d = x.shape
    f = s.shape[1]
    tm = min(tm, t)

    def fn(s, x, gt, lng, lnb, w):
        y = jnp.dot(s, w, preferred_element_type=F32)
        return _residual_ln(y, x, gt, lng, lnb), y

    return _blocked(name, fn, (t // tm,), [s, x, gt, lng, lnb, w],
                    [_tok(tm, f), _tok(tm, d), _res(gt.shape), _res(lng.shape), _res(lnb.shape), _res(w.shape)],
                    [_sds((t, d)), _sds((t, d))], [_tok(tm, d)] * 2)


def _ffn_down_bwd(name, dx2, x, y, gt, lng, lnb, w, tm):
    t, d = x.shape
    f = w.shape[0]
    tm = min(tm, t)

    def fn(dx2, x, y, gt, lng, lnb, w):
        _, vjp2 = jax.vjp(_residual_ln, y, x, gt, lng, lnb)
        dy, dx, dgt, dlng, dlnb = vjp2(dx2)
        dyb = dy.astype(MXU_DTYPE)
        return _mm_nt(dyb, w), dx, dyb, dgt, dlng, dlnb

    return _blocked(name, fn, (t // tm,), [dx2, x, y, gt, lng, lnb, w],
                    [_tok(tm, d), _tok(tm, d), _tok(tm, d), _res(gt.shape), _res(lng.shape), _res(lnb.shape), _res(w.shape)],
                    [_sds((t, f), MXU_DTYPE), _sds((t, d)), _sds((t, d), MXU_DTYPE), _sds(gt.shape), _sds(lng.shape), _sds(lnb.shape)],
                    [_tok(tm, f), _tok(tm, d), _tok(tm, d), _res(gt.shape), _res(lng.shape), _res(lnb.shape)],
                    acc=(3, 4, 5), acc_axis=0)


def _loss_and_grad(x4, target, tm):
    t, d = x4.shape
    tm = min(tm, t)

    def fn(y, tg):
        e = y - tg
        part = 0.5 * jnp.sum(jnp.mean(e * e, axis=-1, keepdims=True), axis=0, keepdims=True)
        return jnp.broadcast_to(part, (1, LANES)), e * (1.0 / d)

    return _blocked("loss", fn, (t // tm,), [x4, target], [_tok(tm, d), _tok(tm, d)],
                    [_sds((1, LANES)), _sds((t, d))], [_res((1, LANES)), _tok(tm, d)], acc=(0,), acc_axis=0)


def _matmul_tn(name, a, b):
    t, k = a.shape
    n = b.shape[1]
    tt = math.gcd(t, 512)
    tn = 512 if n % 512 == 0 else (384 if n % 384 == 0 else 128)

    def body(a_ref, b_ref, o_ref):
        @pl.when(pl.program_id(1) == 0)
        def _():
            o_ref[...] = jnp.zeros(o_ref.shape, F32)

        o_ref[...] += lax.dot_general(a_ref[...], b_ref[...], (((0,), (0,)), ((), ())), preferred_element_type=F32)

    return pl.pallas_call(
        body, name=name, grid=(n // tn, t // tt),
        in_specs=[pl.BlockSpec((tt, k), lambda j, i: (i, 0)), pl.BlockSpec((tt, tn), lambda j, i: (i, j))],
        out_specs=pl.BlockSpec((k, tn), lambda j, i: (0, j)), out_shape=_sds((k, n)),
        compiler_params=pltpu.CompilerParams(dimension_semantics=("arbitrary", "arbitrary"),
                                             vmem_limit_bytes=VMEM_LIMIT_BYTES),
    )(a, b)


def _chan(t, col_of):
    return pl.BlockSpec((t, LANES), lambda i: (0, col_of(i)))


def _wblk(k, col_of=lambda i: i):
    return pl.BlockSpec((k, LANES), lambda i: (0, col_of(i)))


SEQ_TILE = 256


def _load_ext(ref, r0, rows, halo, total):
    lo, hi = max(r0 - halo, 0), min(r0 + rows + halo, total)
    parts = []
    if lo > r0 - halo:
        parts.append(jnp.zeros((lo - (r0 - halo), ref.shape[1]), F32))
    parts.append(ref[lo:hi, :].astype(F32))
    if hi < r0 + rows + halo:
        parts.append(jnp.zeros((r0 + rows + halo - hi, ref.shape[1]), F32))
    return parts[0] if len(parts) == 1 else jnp.concatenate(parts, axis=0)


def _seq_stage(name, tile_fn, halo, grid_n, seqs, pars, outs):
    total = seqs[0][0].shape[0]
    rows = min(SEQ_TILE, total)
    ns, npar = len(seqs), len(pars)

    def body(*refs):
        seq_refs, par_refs, out_refs = refs[:ns], refs[ns:ns + npar], refs[ns + npar:]
        par_vals = [r[...] for r in par_refs]
        for r0 in range(0, total, rows):
            exts = [_load_ext(r, r0, rows, halo, total) for r in seq_refs]
            res = tile_fn(r0, *exts, *par_vals)
            for o_ref, v in zip(out_refs, res):
                o_ref[r0:r0 + rows, :] = v.astype(o_ref.dtype)

    return pl.pallas_call(
        body, name=name, grid=(grid_n,), in_specs=[s for _, s in seqs] + [s for _, s in pars],
        out_specs=[s for _, s in outs], out_shape=[o for o, _ in outs],
        compiler_params=pltpu.CompilerParams(dimension_semantics=("arbitrary",), vmem_limit_bytes=VMEM_LIMIT_BYTES),
    )(*[a for a, _ in seqs], *[a for a, _ in pars])


def _seq_stage_bwd(name, tile_fn, halo, grid_n, douts, seqs, pars, dseq_outs, dpar_outs):
    total = seqs[0][0].shape[0]
    rows = min(SEQ_TILE, total)
    groups = [d if isinstance(d, list) else [d] for d in douts]
    douts = [d for g in groups for d in g]
    starts = [sum(len(g) for g in groups[:k]) for k in range(len(groups))]
    nd, ns, npar = len(douts), len(seqs), len(pars)
    widths = [s.block_shape[-1] for _, s in seqs]

    def dtile(dout_refs, k, r0):
        terms = [d[r0:r0 + rows, :].astype(F32) for d in dout_refs[starts[k]:starts[k] + len(groups[k])]]
        return functools.reduce(lambda a, b: a + b, terms)

    def body(*refs):
        dout_refs, seq_refs, par_refs = refs[:nd], refs[nd:nd + ns], refs[nd + ns:nd + ns + npar]
        dseq_refs = refs[nd + ns + npar:nd + 2 * ns + npar]
        dpar_refs = refs[nd + 2 * ns + npar:nd + 2 * ns + 2 * npar]
        accs = refs[nd + 2 * ns + 2 * npar:]
        par_vals = [r[...] for r in par_refs]
        for a in accs:
            for r0 in range(0, total, rows):
                a[r0:r0 + rows, :] = jnp.zeros((rows, a.shape[1]), F32)
        dpars = [jnp.zeros(p.shape, F32) for p in par_vals]
        for r0 in range(0, total, rows):
            exts = [_load_ext(r, r0, rows, halo, total) for r in seq_refs]
            _, vjp = jax.vjp(functools.partial(tile_fn, r0), *exts, *par_vals)
            cts = vjp(tuple(dtile(dout_refs, k, r0) for k in range(len(groups))))
            lo, hi = max(r0 - halo, 0), min(r0 + rows + halo, total)
            for a, dx in zip(accs, cts[:ns]):
                a[lo:hi, :] += dx[lo - (r0 - halo):hi - (r0 - halo)]
            dpars = [acc + g for acc, g in zip(dpars, cts[ns:])]
        for o_ref, a in zip(dseq_refs, accs):
            for r0 in range(0, total, rows):
                o_ref[r0:r0 + rows, :] = a[r0:r0 + rows, :].astype(o_ref.dtype)
        for o_ref, g in zip(dpar_refs, dpars):
            o_ref[...] = g

    return pl.pallas_call(
        body, name=name, grid=(grid_n,),
        in_specs=[s for _, s in douts] + [s for _, s in seqs] + [s for _, s in pars],
        out_specs=[s for _, s in dseq_outs] + [s for _, s in dpar_outs],
        out_shape=[o for o, _ in dseq_outs] + [o for o, _ in dpar_outs],
        scratch_shapes=[pltpu.VMEM((total, w), F32) for w in widths],
        compiler_params=pltpu.CompilerParams(dimension_semantics=("arbitrary",), vmem_limit_bytes=VMEM_LIMIT_BYTES),
    )(*[a for a, _ in douts], *[a for a, _ in seqs], *[a for a, _ in pars])


def _qkv_tile(r0, p_ext, w):
    kind = pl.program_id(0)
    a = _silu(_conv5(p_ext, w))
    nrm = a * lax.rsqrt(jnp.sum(a * a, -1, keepdims=True) + RMS_EPS)
    scale = jnp.where(kind < GDN_HEADS, HEAD_DIM ** -0.5, 1.0).astype(F32)
    sel = (kind < 2 * GDN_HEADS).astype(F32)
    return (sel * (nrm * scale) + (1.0 - sel) * a,)


def _qkv_conv(name, p, w):
    t = p.shape[0]
    nb = w.shape[1] // LANES
    ident = lambda i: i
    return _seq_stage(name, _qkv_tile, HALO_SHORT, nb, [(p, _chan(t, ident))], [(w, _wblk(GDN_CONV))],
                      [(_sds((t, nb * LANES)), _chan(t, ident))])[0]


def _qkv_conv_bwd(name, dqkv0, dqkv1, p, w):
    t = p.shape[0]
    nb = w.shape[1] // LANES
    ident = lambda i: i
    return _seq_stage_bwd(name, _qkv_tile, HALO_SHORT, nb, [[(dqkv0, _chan(t, ident)), (dqkv1, _chan(t, ident))]],
                          [(p, _chan(t, ident))],
                          [(w, _wblk(GDN_CONV))], [(_sds((t, nb * LANES), MXU_DTYPE), _chan(t, ident))],
                          [(_sds(w.shape), _wblk(GDN_CONV))])


def _gates_fn(s, avec, dtvec):
    lane = lax.broadcasted_iota(jnp.int32, s.shape, 1)
    beta = jax.nn.sigmoid(s)
    g = -jnp.exp(avec) * jax.nn.softplus(s + dtvec)
    nh = 2 * GDN_HEADS
    return jnp.where(lane < nh, beta, jnp.where(lane < 2 * nh, g, 0.0))


def _gates(name, p, avec, dtvec, col):
    t = p.shape[0]
    tm = min(512, t)
    return _blocked(name, _gates_fn, (t // tm,), [p, avec, dtvec], [_tok(tm, LANES, col), _res(avec.shape), _res(dtvec.shape)],
                    [_sds((t, LANES))], [_tok(tm, LANES)])[0]


def _gates_bwd(name, dgb, p, avec, dtvec, col):
    t = p.shape[0]
    tm = min(512, t)

    def fn(dgb, s, avec, dtvec):
        _, vjp = jax.vjp(_gates_fn, s, avec, dtvec)
        return vjp(dgb)

    return _blocked(name, fn, (t // tm,), [dgb, p, avec, dtvec],
                    [_tok(tm, LANES), _tok(tm, LANES, col), _res(avec.shape), _res(dtvec.shape)],
                    [_sds((t, LANES), MXU_DTYPE), _sds(avec.shape), _sds(dtvec.shape)],
                    [_tok(tm, LANES), _res(avec.shape), _res(dtvec.shape)], acc=(1, 2), acc_axis=0)


def _make_pool_tile(total):
    def tile(r0, x_ext, scale, *ws):
        ys = []
        for g, win in enumerate(POOL_WINDOWS):
            xg = x_ext[:, g * LANES:(g + 1) * LANES]
            rows = xg.shape[0] - 2 * HALO_SHORT
            pos = r0 + lax.broadcasted_iota(jnp.int32, (rows, LANES), 0)
            lo = jnp.clip(pos - win // 2, 0, total)
            hi = jnp.clip(pos - win // 2 + win, 0, total)
            window_sum = _pool_sums[win](xg, jnp.ones((win, LANES), F32))
            pooled = window_sum / (hi - lo).astype(F32) - _center(xg, HALO_SHORT)
            ys.append(_mm(pooled, ws[g]))
        return (jnp.concatenate(ys, axis=-1) * scale,)

    return tile


def _pool_specs(p, pool_w, pool_scale, col0):
    t = p.shape[0]
    n = len(POOL_WINDOWS) * LANES
    seq = (p, pl.BlockSpec((t, n), lambda i: (0, col0)))
    pars = [(pool_scale, _res(pool_scale.shape))]
    pars += [(pool_w, pl.BlockSpec((None, LANES, LANES), lambda i, g=g: (g, 0, 0))) for g in range(len(POOL_WINDOWS))]
    return t, n, seq, pars


def _pool(p, pool_w, pool_scale, col0):
    t, n, seq, pars = _pool_specs(p, pool_w, pool_scale, col0)
    return _seq_stage("pool", _make_pool_tile(t), HALO_SHORT, 1, [seq], pars, [(_sds((t, n)), _res((t, n)))])[0]


def _pool_bwd(dy, p, pool_w, pool_scale, col0):
    t, n, seq, pars = _pool_specs(p, pool_w, pool_scale, col0)
    wspec = (_sds((LANES, LANES)), _res((LANES, LANES)))
    return _seq_stage_bwd("pool_bwd", _make_pool_tile(t), HALO_SHORT, 1, [(dy, _res(dy.shape))], [seq], pars,
                          [(_sds((t, n), MXU_DTYPE), _res((t, n)))],
                          [(_sds(pool_scale.shape), _res(pool_scale.shape))] + [wspec] * len(POOL_WINDOWS))


def _ffn_conv_tile(r0, a_ext, gt_ext, w):
    return (_silu(_conv3x3(a_ext, w)) * _center(gt_ext, HALO_GRID),)


def _ffn_conv(name, h, w9):
    t = h.shape[0]
    nb = w9.shape[1] // LANES
    ident = lambda i: i
    return _seq_stage(name, _ffn_conv_tile, HALO_GRID, nb, [(h, _chan(t, ident)), (h, _chan(t, lambda i: nb + i))],
                      [(w9, _wblk(9))], [(_sds((t, nb * LANES), MXU_DTYPE), _chan(t, ident))])[0]


def _ffn_conv_bwd(name, ds, h, w9):
    t = h.shape[0]
    nb = w9.shape[1] // LANES
    ident = lambda i: i
    o = (_sds((t, nb * LANES), MXU_DTYPE), _chan(t, ident))
    return _seq_stage_bwd(name, _ffn_conv_tile, HALO_GRID, nb, [(ds, _chan(t, ident))],
                          [(h, _chan(t, ident)), (h, _chan(t, lambda i: nb + i))], [(w9, _wblk(9))],
                          [o, o], [(_sds(w9.shape), _wblk(9))])


def _sconv_tile(r0, gb_ext, gc_ext, h_ext, w):
    return (_center(gb_ext, HALO_SHORT) * _conv3(gc_ext * h_ext, w),)


def _sconv_specs(p1, w3):
    t = p1.shape[0]
    nb = w3.shape[1] // LANES
    seqs = [(p1, _chan(t, lambda i: i)), (p1, _chan(t, lambda i: nb + i)), (p1, _chan(t, lambda i: 2 * nb + i))]
    return t, nb, seqs, [(w3, _wblk(SC_WIDTH))]


def _sconv(p1, w3):
    t, nb, seqs, pars = _sconv_specs(p1, w3)
    return _seq_stage("sconv", _sconv_tile, HALO_SHORT, nb, seqs, pars, [(_sds((t, nb * LANES)), _chan(t, lambda i: i))])[0]


def _sconv_bwd(dysc, p1, w3):
    t, nb, seqs, pars = _sconv_specs(p1, w3)
    o = (_sds((t, nb * LANES), MXU_DTYPE), _chan(t, lambda i: i))
    return _seq_stage_bwd("sconv_bwd", _sconv_tile, HALO_SHORT, nb, [(dysc, _chan(t, lambda i: i))], seqs, pars,
                          [o, o, o], [(_sds(w3.shape), _wblk(SC_WIDTH))])


def _conf_tile(r0, ga_ext, gb_ext, w):
    return (_conv31(ga_ext * jax.nn.sigmoid(gb_ext), w),)


def _conf_specs(p1, w31, blk0):
    t = p1.shape[0]
    nb = w31.shape[1] // LANES
    seqs = [(p1, _chan(t, lambda i: blk0 + i)), (p1, _chan(t, lambda i: blk0 + nb + i))]
    return t, nb, seqs, [(w31, _wblk(CF_WIDTH))]


def _confconv(p1, w31, blk0):
    t, nb, seqs, pars = _conf_specs(p1, w31, blk0)
    return _seq_stage("confconv", _conf_tile, HALO_CONF, nb, seqs, pars, [(_sds((t, nb * LANES)), _chan(t, lambda i: i))])[0]


def _confconv_bwd(dz0, p1, w31, blk0):
    t, nb, seqs, pars = _conf_specs(p1, w31, blk0)
    o = (_sds((t, nb * LANES), MXU_DTYPE), _chan(t, lambda i: i))
    return _seq_stage_bwd("confconv_bwd", _conf_tile, HALO_CONF, nb, [(dz0, _chan(t, lambda i: i))], seqs, pars,
                          [o, o], [(_sds(w31.shape), _wblk(CF_WIDTH))])


def _bdot(a, b, ca, cb, hi):
    if hi:
        return lax.dot_general(a, b, (((ca,), (cb,)), ((0,), (0,))), preferred_element_type=F32, precision=HIGHEST)
    return lax.dot_general(a.astype(MXU_DTYPE), b.astype(MXU_DTYPE), (((ca,), (cb,)), ((0,), (0,))), preferred_element_type=F32)


def _make_bmm(hi):
    nn_i = lambda a, b: _bdot(a, b, 2, 1, hi)
    nt_i = lambda a, b: _bdot(a, b, 2, 2, hi)
    tn_i = lambda a, b: _bdot(a, b, 1, 1, hi)
    nn = _mm_vjp(nn_i, lambda g, b: nt_i(g, b), lambda a, g: tn_i(a, g))
    nt = _mm_vjp(nt_i, lambda g, b: nn_i(g, b), lambda a, g: tn_i(g, a))
    tn = _mm_vjp(tn_i, lambda g, b: nt_i(b, g), lambda a, g: nn_i(a, g))
    return nn, nt, tn


_bmm, _bmm_nt, _bmm_tn = _make_bmm(False)
_bmm_hi, _bmm_hi_nt, _bmm_hi_tn = _make_bmm(True)


def _unit_tri_inverse(a):
    c = a.shape[-1]
    eye = (lax.broadcasted_iota(jnp.int32, a.shape, 1) == lax.broadcasted_iota(jnp.int32, a.shape, 2)).astype(F32)
    levels = max(1, int(math.ceil(math.log2(c))) - 1)
    p = eye - a
    m = _bdot(a, a, 2, 1, True)
    for level in range(levels):
        p_next = p + _bdot(p, m, 2, 1, True)
        if level + 1 < levels:
            m = _bdot(m, m, 2, 1, True)
        p = p_next
    return p


@jax.custom_vjp
def _known_inverse(a, tinv):
    return tinv


def _known_inverse_fwd(a, tinv):
    return tinv, tinv


def _known_inverse_bwd(tinv, dt):
    da = -_bdot(_bdot(tinv, dt, 1, 1, True), tinv, 2, 2, True)
    return da, jnp.zeros_like(tinv)


_known_inverse.defvjp(_known_inverse_fwd, _known_inverse_bwd)


def _gdn_chunk(q, k, v, gcol, grow, bcol, s, tinv=None):
    n, c, _ = q.shape
    shape = (n, c, c)
    fwd_dir = lax.broadcasted_iota(jnp.int32, shape, 0) < n // 2
    i = lax.broadcasted_iota(jnp.int32, shape, 1)
    j = lax.broadcasted_iota(jnp.int32, shape, 2)
    order = jnp.where(fwd_dir, i - j, j - i)
    incl = order >= 0
    gc_col = jnp.sum(jnp.where(incl, grow, 0.0), axis=2, keepdims=True)
    gc_row = jnp.sum(jnp.where(order <= 0, gcol, 0.0), axis=1, keepdims=True)
    gtot = jnp.sum(grow, axis=2, keepdims=True)
    decay = jnp.exp(jnp.where(incl, gc_col - gc_row, -1e30))
    kb = k * bcol
    a = jnp.where(order > 0, _bmm_nt(kb, k) * decay, 0.0)
    tinv = _unit_tri_inverse(a) if tinv is None else _known_inverse(a, tinv)
    e_col = jnp.exp(gc_col)
    u = _bmm_hi(tinv, v * bcol)
    w = _bmm_hi(tinv, kb * e_col)
    k_dec = k * jnp.exp(gtot - gc_col)
    v_new = u - _bmm(w, s)
    attn = jnp.where(incl, _bmm_nt(q, k) * decay, 0.0)
    o = _bmm(q * e_col, s) + _bmm(attn, v_new)
    s_next = s * jnp.exp(gtot) + _bmm_tn(k_dec, v_new)
    return o, s_next, tinv


N_CHAINS = 2 * GDN_HEADS
QKV_W = 3 * GDN_HEADS * HEAD_DIM


def _scan_specs(nc, step):
    c = GDN_CHUNK
    seq = [pl.BlockSpec((c, QKV_W), lambda i: (step(i), 0)), pl.BlockSpec((c, QKV_W), lambda i: (nc - 1 - step(i), 0))]
    heads = [pl.BlockSpec((c, GDN_HEADS * HEAD_DIM), lambda i: (step(i), 0)),
             pl.BlockSpec((c, GDN_HEADS * HEAD_DIM), lambda i: (nc - 1 - step(i), 0))]
    colv = pl.BlockSpec((N_CHAINS, None, c, 1), lambda i: (0, step(i), 0, 0))
    rowv = pl.BlockSpec((N_CHAINS, None, 1, c), lambda i: (0, step(i), 0, 0))
    state = pl.BlockSpec((N_CHAINS, HEAD_DIM, HEAD_DIM), lambda i: (0, 0, 0))
    saved = pl.BlockSpec((N_CHAINS, None, HEAD_DIM, HEAD_DIM), lambda i: (0, step(i), 0, 0))
    inv = pl.BlockSpec((N_CHAINS, None, c, c), lambda i: (0, step(i), 0, 0))
    return seq, heads, colv, rowv, state, saved, inv


def _head_cols(h, part):
    lo = (part * GDN_HEADS + h) * HEAD_DIM
    return slice(lo, lo + HEAD_DIM)


def _chain_stack(x0_ref, x1_ref, part):
    return jnp.stack([(x0_ref if n < GDN_HEADS else x1_ref)[:, _head_cols(n % GDN_HEADS, part)] for n in range(N_CHAINS)])


def _scan_fwd(name, qkv, bcol, gcol, grow, s0):
    t = qkv.shape[0]
    nc = t // GDN_CHUNK
    seq, heads, colv, rowv, state, saved, inv = _scan_specs(nc, lambda i: i)

    def body(qkv0_ref, qkv1_ref, b_ref, gc_ref, gr_ref, s0_ref, o0_ref, o1_ref, save_ref, tinv_ref, fin_ref, s_ref):
        @pl.when(pl.program_id(0) == 0)
        def _():
            s_ref[...] = s0_ref[...]

        s = s_ref[...]
        save_ref[...] = s
        q, k, v = [_chain_stack(qkv0_ref, qkv1_ref, part) for part in range(3)]
        o, s_next, tinv = _gdn_chunk(q, k, v, gc_ref[...], gr_ref[...], b_ref[...], s)
        for n in range(N_CHAINS):
            d, h = divmod(n, GDN_HEADS)
            (o0_ref if d == 0 else o1_ref)[:, _head_cols(h, 0)] = o[n]
        tinv_ref[...] = tinv
        s_ref[...] = s_next
        fin_ref[...] = s_next

    hw = GDN_HEADS * HEAD_DIM
    return pl.pallas_call(
        body, name=name, grid=(nc,), in_specs=seq + [colv, colv, rowv, state],
        out_specs=heads + [saved, inv, state],
        out_shape=[_sds((t, hw)), _sds((t, hw)), _sds((N_CHAINS, nc, HEAD_DIM, HEAD_DIM)),
                   _sds((N_CHAINS, nc, GDN_CHUNK, GDN_CHUNK)), _sds((N_CHAINS, HEAD_DIM, HEAD_DIM))],
        scratch_shapes=[pltpu.VMEM((N_CHAINS, HEAD_DIM, HEAD_DIM), F32)],
        compiler_params=pltpu.CompilerParams(dimension_semantics=("arbitrary",), vmem_limit_bytes=VMEM_LIMIT_BYTES),
    )(qkv, qkv, bcol, gcol, grow, s0)


def _scan_bwd(name, do, qkv, bcol, gcol, grow, saved_s, saved_inv, ds_last):
    t = qkv.shape[0]
    nc = t // GDN_CHUNK
    c = GDN_CHUNK
    seq, heads, colv, rowv, state, saved, inv = _scan_specs(nc, lambda i: nc - 1 - i)

    def body(do0_ref, do1_ref, qkv0_ref, qkv1_ref, b_ref, gc_ref, gr_ref, s_ref, tinv_ref, dsl_ref,
             dx0_ref, dx1_ref, db_ref, dgc_ref, dgr_ref, ds0_ref, ds_ref):
        @pl.when(pl.program_id(0) == 0)
        def _():
            ds_ref[...] = dsl_ref[...]

        tinv = tinv_ref[...]
        q, k, v = [_chain_stack(qkv0_ref, qkv1_ref, part) for part in range(3)]
        do = jnp.stack([(do0_ref if n < GDN_HEADS else do1_ref)[:, _head_cols(n % GDN_HEADS, 0)] for n in range(N_CHAINS)])
        _, vjp = jax.vjp(lambda *a: _gdn_chunk(*a, tinv=tinv)[:2], q, k, v, gc_ref[...], gr_ref[...], b_ref[...], s_ref[...])
        dq, dk, dv, dgc, dgr, db, ds = vjp((do, ds_ref[...]))
        for n in range(N_CHAINS):
            d, h = divmod(n, GDN_HEADS)
            dx_ref = dx0_ref if d == 0 else dx1_ref
            dx_ref[:, _head_cols(h, 0)] = dq[n]
            dx_ref[:, _head_cols(h, 1)] = dk[n]
            dx_ref[:, _head_cols(h, 2)] = dv[n]
        db_ref[...] = db
        dgc_ref[...] = dgc
        dgr_ref[...] = dgr
        ds_ref[...] = ds
        ds0_ref[...] = ds

    vec_c = _sds((N_CHAINS, nc, c, 1))
    vec_r = _sds((N_CHAINS, nc, 1, c))
    return pl.pallas_call(
        body, name=name, grid=(nc,),
        in_specs=heads + seq + [colv, colv, rowv, saved, inv, state],
        out_specs=seq + [colv, colv, rowv, state],
        out_shape=[_sds((t, QKV_W)), _sds((t, QKV_W)), vec_c, vec_c, vec_r, _sds((N_CHAINS, HEAD_DIM, HEAD_DIM))],
        scratch_shapes=[pltpu.VMEM((N_CHAINS, HEAD_DIM, HEAD_DIM), F32)],
        compiler_params=pltpu.CompilerParams(dimension_semantics=("arbitrary",), vmem_limit_bytes=VMEM_LIMIT_BYTES),
    )(do, do, qkv, qkv, bcol, gcol, grow, saved_s, saved_inv, ds_last)


def _gate_layouts(gb):
    t = gb.shape[0]
    nc = t // GDN_CHUNK
    nh = GDN_HEADS

    def by_scan_position(a):
        a = a.T.reshape(2 * nh, nc, GDN_CHUNK)
        return jnp.concatenate([a[:nh], a[nh:, ::-1]], axis=0)

    beta = by_scan_position(gb[:, :2 * nh])
    g = by_scan_position(gb[:, 2 * nh:4 * nh])
    return beta[..., None], g[..., None], g[:, :, None, :]


def _gate_layouts_bwd(dbcol, dgcol, dgrow):
    n2, nc, c, _ = dbcol.shape
    nh = n2 // 2
    t = nc * c

    def by_token(a):
        return jnp.concatenate([a[:nh], a[nh:, ::-1]], axis=0).reshape(n2, t).T

    dbeta = by_token(dbcol[..., 0])
    dg = by_token(dgcol[..., 0] + dgrow[:, :, 0, :])
    return jnp.concatenate([dbeta, dg, jnp.zeros((t, LANES - 2 * n2), F32)], axis=1)


def _shard_axis(full_shape, shard_shape):
    return [i for i, (a, b) in enumerate(zip(full_shape, shard_shape)) if a != b][0]


def kernel(x, c, ctx, c_ctx, ada_w, ada_b, ln_g, ln_b, even_w_in, even_w_out, gdn_conv_w, gdn_a_log, gdn_dt_bias, gdn_norm_w, pool_w, pool_scale, odd_w_in, odd_w_out, sconv_w, conf_conv_w, conf_ln_g, conf_ln_b, ffn_w_up, ffn_conv_w, ffn_w_down, loss_target, m_c_ctx, m_ada_w, m_ada_b, m_ln_g, m_ln_b, m_even_w_in, m_even_w_out, m_gdn_conv_w, m_gdn_a_log, m_gdn_dt_bias, m_gdn_norm_w, m_pool_w, m_pool_scale, m_odd_w_in, m_odd_w_out, m_sconv_w, m_conf_conv_w, m_conf_ln_g, m_conf_ln_b, m_ffn_w_up, m_ffn_conv_w, m_ffn_w_down, v_c_ctx, v_ada_w, v_ada_b, v_ln_g, v_ln_b, v_even_w_in, v_even_w_out, v_gdn_conv_w, v_gdn_a_log, v_gdn_dt_bias, v_gdn_norm_w, v_pool_w, v_pool_scale, v_odd_w_in, v_odd_w_out, v_sconv_w, v_conf_conv_w, v_conf_ln_g, v_conf_ln_b, v_ffn_w_up, v_ffn_conv_w, v_ffn_w_down):
    names = ['c_ctx', 'ada_w', 'ada_b', 'ln_g', 'ln_b', 'even_w_in', 'even_w_out', 'gdn_conv_w', 'gdn_a_log',
             'gdn_dt_bias', 'gdn_norm_w', 'pool_w', 'pool_scale', 'odd_w_in', 'odd_w_out', 'sconv_w', 'conf_conv_w',
             'conf_ln_g', 'conf_ln_b', 'ffn_w_up', 'ffn_conv_w', 'ffn_w_down']
    loc = locals()
    wts = {n: loc[n] for n in names}
    mom = {n: loc['m_' + n] for n in names}
    var = {n: loc['v_' + n] for n in names}

    ix, iy, ic = lax.axis_index("x"), lax.axis_index("y"), lax.axis_index("c")
    chip = 2 * ix + iy
    dev = 2 * chip + ic
    d = D_MODEL
    x0, ctx0, tgt = x[0], ctx[0], loss_target[0]
    t, tc = x0.shape[0], ctx0.shape[0]
    depth = ada_w.shape[0]
    n_ada = ada_w.shape[2]

    small_sharded = [gdn_conv_w, sconv_w, conf_conv_w, ffn_conv_w, ln_g, ln_b]
    s1_items = [c] + small_sharded
    s1 = _pack_flat(s1_items, 8 * LANES, F32).reshape(-1, LANES)
    g1 = _all_gather8("gather_small_in", s1, True).reshape(8, -1)
    c_all = g1[:, :d]
    per_chip = [_unpack_flat(g1[2 * k], [a.shape for a in s1_items])[1:] for k in range(4)]
    gdn_conv_f, sconv_f, conf_conv_f, ffn_conv_f, ln_g_f, ln_b_f = [
        jnp.concatenate([per_chip[k][i] for k in range(4)], axis=-1) for i in range(len(small_sharded))]
    c16 = jnp.concatenate([c_all, c_ctx[None], jnp.zeros((7, d), F32)], axis=0)

    mod_part = _mod_rows(c16, ada_w)
    g2 = _all_gather8("gather_mod", mod_part.reshape(-1, LANES), True).reshape(4, 2, depth, 16, n_ada)[:, 0]
    mod_all = jnp.transpose(g2, (1, 2, 0, 3)).reshape(depth, 16, 4 * n_ada) + ada_b[:, None, :]
    mod_me = lax.dynamic_index_in_dim(mod_all, dev, axis=1, keepdims=False).reshape(depth, 6, 1, d)
    sh_c, sc_c = mod_all[0, 8, :d][None], mod_all[0, 8, d:2 * d][None]

    big_names = ['even_w_in', 'even_w_out', 'odd_w_in', 'odd_w_out', 'ffn_w_up', 'ffn_w_down']
    big_full_shapes = {'even_w_in': (d, 4 * even_w_in.shape[1]), 'even_w_out': (4 * even_w_out.shape[0], d),
                       'odd_w_in': (d, 4 * odd_w_in.shape[1]), 'odd_w_out': (4 * odd_w_out.shape[0], d),
                       'ffn_w_up': (depth, d, 4 * ffn_w_up.shape[2]), 'ffn_w_down': (depth, 4 * ffn_w_down.shape[1], d)}
    big_shard_shapes = [wts[n].shape for n in big_names]
    big_axes = [_shard_axis(big_full_shapes[n], wts[n].shape) for n in big_names]
    half_mult = 2 * 16 * LANES
    wpack = _pack_flat([wts[n] for n in big_names], half_mult, MXU_DTYPE).reshape(2, -1, LANES)
    rh = wpack.shape[1]
    my_half = lax.dynamic_index_in_dim(wpack, ic, axis=0, keepdims=False)
    wg = _all_gather8("gather_weights", my_half, False).reshape(4, -1)
    pieces = [_unpack_flat(wg[k], big_shard_shapes) for k in range(4)]
    full = {n: jnp.concatenate([pieces[k][i] for k in range(4)], axis=big_axes[i]) for i, n in enumerate(big_names)}
    n_even = full['even_w_in'].shape[1]
    n_even_pad = -(-n_even // LANES) * LANES
    w_in = jnp.concatenate([full['even_w_in'], jnp.zeros((d, n_even_pad - n_even), MXU_DTYPE)], axis=1)
    w_out, w_oin, w_oout = full['even_w_out'], full['odd_w_in'], full['odd_w_out']
    w_up, w_down = full['ffn_w_up'], full['ffn_w_down']
    scal_blk = (n_even // LANES)
    n_scal = n_even - scal_blk * LANES

    def mod(layer, k):
        return mod_me[layer, k]

    avec = jnp.zeros((1, LANES), F32).at[0, n_scal // 2:n_scal].set(gdn_a_log.reshape(-1))
    dtvec = jnp.zeros((1, LANES), F32).at[0, n_scal // 2:n_scal].set(gdn_dt_bias.reshape(-1))
    normw = gdn_norm_w[None]
    pscale = pool_scale[None]
    cg, cb = conf_ln_g[None], conf_ln_b[None]
    lng = lambda l, k: ln_g_f[l, k][None]
    lnb = lambda l, k: ln_b_f[l, k][None]
    convw9 = ffn_conv_f.reshape(depth, 9, -1)
    nqkv = gdn_conv_f.shape[1]

    p, ub0 = _inproj("even_in", x0, mod(0, 0), mod(0, 1), w_in, 512)
    pc, ucb = _inproj("even_in_ctx", ctx0, sh_c, sc_c, w_in, 256)
    qkv = _qkv_conv("qkv_conv", p, gdn_conv_f)
    qkv_c = _qkv_conv("qkv_conv_ctx", pc, gdn_conv_f)
    gb = _gates("gates", p, avec, dtvec, scal_blk)
    gb_c = _gates("gates_ctx", pc, avec, dtvec, scal_blk)
    lay = _gate_layouts(gb)
    lay_c = _gate_layouts(gb_c)
    s_zero = jnp.zeros((2 * GDN_HEADS, HEAD_DIM, HEAD_DIM), F32)
    _, _, save_c, inv_c, s_ctx = _scan_fwd("scan_ctx", qkv_c, *lay_c, s_zero)
    o0, o1, save_l, inv_l, _ = _scan_fwd("scan", qkv, *lay, s_ctx)
    pool_blk = (nqkv + GDN_HEADS * HEAD_DIM) // 512
    ypool = _pool(p, pool_w, pscale, pool_blk)
    x1, mix0, y0 = _even_out(o0, o1, p, ypool, x0, normw, mod(0, 2), lng(0, 0), lnb(0, 0), w_out, 256)

    def ffn_fwd(l, xin):
        h, ub = _inproj(f"ffn_up{l}", xin, mod(l, 3), mod(l, 4), w_up[l], 256)
        s = _ffn_conv(f"ffn_conv{l}", h, convw9[l])
        xo, y = _ffn_down(f"ffn_down{l}", s, xin, mod(l, 5), lng(l, 1), lnb(l, 1), w_down[l], 256)
        return xo, (h, ub, s, y)

    x2, ffn0 = ffn_fwd(0, x1)
    p1, ub1 = _inproj("odd_in", x2, mod(1, 0), mod(1, 1), w_oin, 512)
    nsc = sconv_f.shape[1] // LANES
    ysc = _sconv(p1, sconv_f)
    z0 = _confconv(p1, conf_conv_f, 3 * nsc)
    x3, mix1, y1 = _odd_out(ysc, z0, x2, cg, cb, mod(1, 2), lng(1, 0), lnb(1, 0), w_oout, 256)
    x4, ffn1 = ffn_fwd(1, x3)
    loss_part, dx4 = _loss_and_grad(x4, tgt, 512)
    loss = lax.psum(loss_part[0, 0], ("x", "y", "c"))

    dmod = [[None] * 6 for _ in range(depth)]
    dlng = [[None, None] for _ in range(depth)]
    dlnb = [[None, None] for _ in range(depth)]
    gbig = {}
    dconv9 = [None] * depth

    def ffn_bwd(l, dxo, xin, saved):
        h, ub, s, y = saved
        ds, dxa, dyb, dgt, dg_, db_ = _ffn_down_bwd(f"ffn_down_bwd{l}", dxo, xin, y, mod(l, 5), lng(l, 1), lnb(l, 1), w_down[l], 256)
        da, dgate, dw9 = _ffn_conv_bwd(f"ffn_conv_bwd{l}", ds, h, convw9[l])
        dh = jnp.concatenate([da, dgate], axis=1)
        dxin, dsh, dsc = _inproj_bwd(f"ffn_up_bwd{l}", dh, xin, mod(l, 3), mod(l, 4), w_up[l], dxa, 256)
        dmod[l][3], dmod[l][4], dmod[l][5] = dsh, dsc, dgt
        dlng[l][1], dlnb[l][1] = dg_, db_
        dconv9[l] = dw9
        return dxin, _matmul_tn(f"dw_up{l}", ub, dh), _matmul_tn(f"dw_down{l}", s, dyb)

    dx3, dwu1, dwd1 = ffn_bwd(1, dx4, x3, ffn1)
    dysc, dz0, dx2a, dyb1, dcg, dcb, dgt, dg_, db_ = _odd_out_bwd(dx3, ysc, z0, x2, y1, cg, cb, mod(1, 2), lng(1, 0), lnb(1, 0), w_oout, 256)
    dmod[1][2], dlng[1][0], dlnb[1][0] = dgt, dg_, db_
    d_gb, d_gc, d_h, dsconv = _sconv_bwd(dysc, p1, sconv_f)
    d_ga, d_gbb, dconf = _confconv_bwd(dz0, p1, conf_conv_f, 3 * nsc)
    dp1 = jnp.concatenate([d_gb, d_gc, d_h, d_ga, d_gbb], axis=1)
    dx2, dsh, dsc = _inproj_bwd("odd_in_bwd", dp1, x2, mod(1, 0), mod(1, 1), w_oin, dx2a, 512)
    dmod[1][0], dmod[1][1] = dsh, dsc
    gbig['odd_w_out'] = _matmul_tn("dw_oout", mix1, dyb1)
    gbig['odd_w_in'] = _matmul_tn("dw_oin", ub1, dp1)

    dx1, dwu0, dwd0 = ffn_bwd(0, dx2, x1, ffn0)
    gbig['ffn_w_up'] = jnp.stack([dwu0, dwu1])
    gbig['ffn_w_down'] = jnp.stack([dwd0, dwd1])

    do, dpg, dypool, dx0a, dyb0, dnormw, dgt, dg_, db_ = _even_out_bwd(dx1, o0, o1, p, ypool, x0, y0, normw, mod(0, 2), lng(0, 0), lnb(0, 0), w_out, 256)
    dmod[0][2], dlng[0][0], dlnb[0][0] = dgt, dg_, db_
    pool_cts = _pool_bwd(dypool, p, pool_w, pscale, pool_blk)
    dpp, dpool_scale, dpool_w = pool_cts[0], pool_cts[1], jnp.stack(pool_cts[2:])
    dqkv0, dqkv1, dbcol, dgcol, dgrow, ds0 = _scan_bwd("scan_bwd", do, qkv, *lay, save_l, inv_l, s_zero)
    zero_do = jnp.zeros((tc, GDN_HEADS * HEAD_DIM), F32)
    dqkv0_c, dqkv1_c, dbcol_c, dgcol_c, dgrow_c, _ = _scan_bwd("scan_bwd_ctx", zero_do, qkv_c, *lay_c, save_c, inv_c, ds0)
    dgb = _gate_layouts_bwd(dbcol, dgcol, dgrow)
    dgb_c = _gate_layouts_bwd(dbcol_c, dgcol_c, dgrow_c)
    dps, davec, ddtvec = _gates_bwd("gates_bwd", dgb, p, avec, dtvec, scal_blk)
    dps_c, davec_c, ddtvec_c = _gates_bwd("gates_bwd_ctx", dgb_c, pc, avec, dtvec, scal_blk)
    dpqkv, dconv5 = _qkv_conv_bwd("qkv_conv_bwd", dqkv0, dqkv1, p, gdn_conv_f)
    dpqkv_c, dconv5_c = _qkv_conv_bwd("qkv_conv_bwd_ctx", dqkv0_c, dqkv1_c, pc, gdn_conv_f)
    dp = jnp.concatenate([dpqkv, dpg, dpp, dps], axis=1)
    dpc = jnp.concatenate([dpqkv_c, jnp.zeros((tc, n_even_pad - nqkv - LANES), MXU_DTYPE), dps_c], axis=1)
    grad_x, dsh, dsc = _inproj_bwd("even_in_bwd", dp, x0, mod(0, 0), mod(0, 1), w_in, dx0a, 512)
    dmod[0][0], dmod[0][1] = dsh, dsc
    _, dsh_c, dsc_c = _inproj_bwd("even_in_bwd_ctx", dpc, ctx0, sh_c, sc_c, w_in, None, 256)
    gbig['even_w_in'] = _matmul_tn("dw_in", jnp.concatenate([ub0, ucb], axis=0), jnp.concatenate([dp, dpc], axis=0))[:, :n_even]
    gbig['even_w_out'] = _matmul_tn("dw_out", mix0, dyb0)

    def shard_of(g, k, axis, n):
        return lax.slice_in_dim(g, k * n, (k + 1) * n, axis=axis)

    gpack = jnp.stack([
        _pack_flat([shard_of(gbig[n], k, big_axes[i], big_shard_shapes[i][big_axes[i]]) for i, n in enumerate(big_names)],
                   half_mult, MXU_DTYPE).reshape(2, rh, LANES) for k in range(4)], axis=1)
    keep = lax.dynamic_index_in_dim(gpack, ic, axis=0, keepdims=False)
    give = lax.dynamic_index_in_dim(gpack, 1 - ic, axis=0, keepdims=False)
    got = _sibling_exchange("grad_pair_exchange", give)
    pair = _sum_leading("grad_pair_sum", jnp.stack([keep.reshape(-1, LANES), got.reshape(-1, LANES)]), MXU_DTYPE)
    parts = _chip_scatter("grad_chip_scatter", pair.reshape(4, rh, LANES))
    gsum = _sum_leading("grad_chip_sum", jnp.transpose(parts, (1, 0, 2, 3)).reshape(4, 2 * rh, LANES)).reshape(-1)
    g_shards = dict(zip(big_names, _unpack_flat(gsum, big_shard_shapes)))

    dmod_rows = jnp.stack([jnp.concatenate(dmod[l], axis=1)[0] for l in range(depth)])
    dmod_c = jnp.concatenate([dsh_c[0], dsc_c[0], jnp.zeros((4 * d,), F32)])
    dmod_c_rows = jnp.stack([dmod_c] + [jnp.zeros_like(dmod_c)] * (depth - 1))
    small_g = {
        'ln_g': jnp.stack([jnp.stack([dlng[l][k][0] for k in range(2)]) for l in range(depth)]),
        'ln_b': jnp.stack([jnp.stack([dlnb[l][k][0] for k in range(2)]) for l in range(depth)]),
        'gdn_conv_w': dconv5 + dconv5_c,
        'gdn_a_log': (davec + davec_c)[0, n_scal // 2:n_scal].reshape(gdn_a_log.shape),
        'gdn_dt_bias': (ddtvec + ddtvec_c)[0, n_scal // 2:n_scal].reshape(gdn_dt_bias.shape),
        'gdn_norm_w': dnormw[0], 'pool_w': dpool_w, 'pool_scale': dpool_scale[0],
        'sconv_w': dsconv, 'conf_conv_w': dconf, 'conf_ln_g': dcg[0], 'conf_ln_b': dcb[0],
        'ffn_conv_w': jnp.stack(dconv9).reshape(depth, 3, 3, -1),
    }
    small_names = list(small_g)
    s3_items = [dmod_rows, dmod_c_rows] + [small_g[n] for n in small_names]
    s3 = _pack_flat(s3_items, 8 * LANES, F32).reshape(-1, LANES)
    g3 = _all_gather8("gather_small_grads", s3, True).reshape(8, -1, LANES)
    tot3 = _sum_leading("small_grad_sum", g3).reshape(-1)
    tot_items = _unpack_flat(tot3, [a.shape for a in s3_items])
    dmod_sum, dmod_c_sum = tot_items[0], tot_items[1]
    small_tot = dict(zip(small_names, tot_items[2:]))
    grad_ada_b = dmod_sum + dmod_c_sum
    g3f = g3.reshape(8, -1)
    rows_all = g3f[:, :depth * 6 * d].reshape(8, depth, 6 * d)
    cols = lax.dynamic_slice_in_dim(rows_all, chip * n_ada, n_ada, axis=2)
    crow = lax.dynamic_slice_in_dim(dmod_c_sum, chip * n_ada, n_ada, axis=1)
    dmod16 = jnp.concatenate([jnp.transpose(cols, (1, 0, 2)), crow[:, None, :], jnp.zeros((depth, 7, n_ada), F32)], axis=1)
    grad_ada_w, dsil = _ada_grads(c16, dmod16, ada_w)
    s4 = jnp.concatenate([dsil[0, 8][None], jnp.zeros((7, d), F32)], axis=0).reshape(-1, LANES)
    g4 = _all_gather8("gather_cctx", s4, True).reshape(8, 8, d)
    grad_c_ctx = _cctx_grad(g4[0::2, 0][:, None, :], c_ctx[None])[0]

    def my_cols(a, n):
        return lax.dynamic_slice_in_dim(a, chip * n, n, axis=a.ndim - 1)

    grads = dict(g_shards)
    grads['c_ctx'] = grad_c_ctx
    grads['ada_w'] = grad_ada_w
    grads['ada_b'] = grad_ada_b
    for n in ['ln_g', 'ln_b', 'gdn_conv_w', 'sconv_w', 'conf_conv_w', 'ffn_conv_w']:
        grads[n] = my_cols(small_tot[n], wts[n].shape[-1])
    for n in ['gdn_a_log', 'gdn_dt_bias', 'gdn_norm_w', 'pool_w', 'pool_scale', 'conf_ln_g', 'conf_ln_b']:
        grads[n] = small_tot[n]

    delta, new_m, new_v = {}, {}, {}
    big_adam = big_names + ['ada_w']
    for n in big_adam:
        shp = wts[n].shape
        as2d = lambda a: a.reshape(-1, shp[-1])
        dl, nm, nv = _adamw("adamw_" + n, as2d(wts[n]), as2d(grads[n]), as2d(mom[n]), as2d(var[n]))
        delta[n], new_m[n], new_v[n] = dl.reshape(shp), nm.reshape(shp), nv.reshape(shp)
    small_adam = [n for n in names if n not in big_adam]
    packs = [_pack_flat([src[n] for n in small_adam], 8 * LANES, F32).reshape(-1, LANES) for src in (wts, grads, mom, var)]
    outs = _adamw("adamw_small", *packs)
    shapes = [wts[n].shape for n in small_adam]
    for res, dst in zip(outs, (delta, new_m, new_v)):
        for n, a in zip(small_adam, _unpack_flat(res.reshape(-1), shapes)):
            dst[n] = a

    return (loss, grad_x[None], *[grads[n] for n in names], *[delta[n] for n in names],
            *[new_m[n] for n in names], *[new_v[n] for n in names])
```

```python
import functools
import math

import jax
import jax.numpy as jnp
from jax import lax
from jax.experimental import pallas as pl
from jax.experimental.pallas import tpu as pltpu

F32 = jnp.float32
MXU_DTYPE = jnp.bfloat16
HIGHEST = lax.Precision.HIGHEST
MESH = pl.DeviceIdType.MESH

D_MODEL = 1024
GRID_W = 64
GDN_HEADS = 4
HEAD_DIM = 128
GDN_CHUNK = 64
POOL_WINDOWS = (2, 4, 8, 16)
GDN_CONV = 5
SC_WIDTH = 3
CF_WIDTH = 31
ALPHA = 4.0 ** 0.25
LN_EPS = 1e-5
RMS_EPS = 1e-6
LANES = 128
VMEM_LIMIT_BYTES = 58 * 1024 * 1024

ADAM_LR, ADAM_B1, ADAM_B2, ADAM_EPS, ADAM_WD, ADAM_STEP = 0.001, 0.9, 0.999, 1e-08, 0.01, 10


def _blocked(name, fn, grid, ins, in_specs, out_shapes, out_specs, acc=(), acc_axis=None, scratch=()):
    n_in = len(ins)
    n_out = len(out_shapes)

    def body(*refs):
        vals = [r[...] for r in refs[:n_in]]
        res = fn(*vals, *refs[n_in + n_out:])
        if not isinstance(res, (tuple, list)):
            res = (res,)
        for k, (r, v) in enumerate(zip(refs[n_in:n_in + n_out], res)):
            if k in acc:
                first = pl.program_id(acc_axis) == 0

                @pl.when(first)
                def _(r=r, v=v):
                    r[...] = v.astype(r.dtype)

                @pl.when(jnp.logical_not(first))
                def _(r=r, v=v):
                    r[...] += v.astype(r.dtype)
            else:
                r[...] = v.astype(r.dtype)

    return pl.pallas_call(
        body, name=name, grid=grid, in_specs=in_specs, out_specs=out_specs, out_shape=out_shapes,
        scratch_shapes=list(scratch),
        compiler_params=pltpu.CompilerParams(dimension_semantics=("arbitrary",) * len(grid),
                                             vmem_limit_bytes=VMEM_LIMIT_BYTES),
    )(*ins)


def _sds(shape, dtype=F32):
    return jax.ShapeDtypeStruct(tuple(shape), dtype)


def _tok(tm, n, col=0):
    return pl.BlockSpec((tm, n), lambda t: (t, col))


def _res(shape):
    nd = len(shape)
    return pl.BlockSpec(tuple(shape), lambda t: (0,) * nd)


def _silu(x):
    return x * jax.nn.sigmoid(x)


def _layernorm(r, g, b):
    mu = jnp.mean(r, -1, keepdims=True)
    d = r - mu
    var = jnp.mean(d * d, -1, keepdims=True)
    return d * lax.rsqrt(var + LN_EPS) * g + b


def _mm_nn_impl(a, b):
    return jnp.dot(a.astype(MXU_DTYPE), b.astype(MXU_DTYPE), preferred_element_type=F32)


def _mm_nt_impl(a, b):
    return lax.dot_general(a.astype(MXU_DTYPE), b.astype(MXU_DTYPE), (((1,), (1,)), ((), ())), preferred_element_type=F32)


def _mm_tn_impl(a, b):
    return lax.dot_general(a.astype(MXU_DTYPE), b.astype(MXU_DTYPE), (((0,), (0,)), ((), ())), preferred_element_type=F32)


def _mm_vjp(mm, mm_da, mm_db):
    f = jax.custom_vjp(mm)
    f.defvjp(lambda a, b: (mm(a, b), (a, b)), lambda res, g: (mm_da(g, res[1]), mm_db(res[0], g)))
    return f


_mm = _mm_vjp(_mm_nn_impl, lambda g, b: _mm_nt_impl(g, b), lambda a, g: _mm_tn_impl(a, g))
_mm_nt = _mm_vjp(_mm_nt_impl, lambda g, b: _mm_nn_impl(g, b), lambda a, g: _mm_tn_impl(g, a))
_mm_tn = _mm_vjp(_mm_tn_impl, lambda g, b: _mm_nt_impl(b, g), lambda a, g: _mm_nn_impl(a, g))


def _row(w, k):
    rows = lax.broadcasted_iota(jnp.int32, w.shape, 0)
    return jnp.sum(jnp.where(rows == k, w, 0.0), axis=0, keepdims=True)


def _col_mask(shape, dc):
    col = lax.broadcasted_iota(jnp.int32, shape, 0) & (GRID_W - 1)
    return (col + dc >= 0) & (col + dc < GRID_W)


def _center(x_ext, halo):
    return x_ext[halo:x_ext.shape[0] - halo]


def _make_dwconv(taps, halo):
    assert all(abs(s) <= halo for s, _ in taps)

    def shifted(x_ext, s):
        r = x_ext if s == 0 else pltpu.roll(x_ext, (-s) % x_ext.shape[0], 0)
        return _center(r, halo)

    @jax.custom_vjp
    def conv(x_ext, w):
        acc = None
        for k, (s, dc) in enumerate(taps):
            r = shifted(x_ext, s)
            if dc != 0:
                r = jnp.where(_col_mask(r.shape, dc), r, 0.0)
            term = r * _row(w, k)
            acc = term if acc is None else acc + term
        return acc

    def fwd(x_ext, w):
        return conv(x_ext, w), (x_ext, w)

    def bwd(res, dy):
        x_ext, w = res
        n = x_ext.shape[0]
        rows = lax.broadcasted_iota(jnp.int32, w.shape, 0)
        pad = jnp.zeros((halo, dy.shape[1]), F32)
        dx = None
        dw = jnp.zeros(w.shape, F32)
        for k, (s, dc) in enumerate(taps):
            dym = dy if dc == 0 else jnp.where(_col_mask(dy.shape, dc), dy, 0.0)
            dw = dw + jnp.where(rows == k, jnp.sum(dym * shifted(x_ext, s), axis=0, keepdims=True), 0.0)
            t = jnp.concatenate([pad, dym * _row(w, k), pad], axis=0)
            if s != 0:
                t = pltpu.roll(t, s % n, 0)
            dx = t if dx is None else dx + t
        return dx, dw

    conv.defvjp(fwd, bwd)
    return conv


def _taps_1d(width):
    return tuple((k - width // 2, 0) for k in range(width))


HALO_SHORT = 8
HALO_CONF = 16
HALO_GRID = 72
_conv5 = _make_dwconv(_taps_1d(GDN_CONV), HALO_SHORT)
_conv3 = _make_dwconv(_taps_1d(SC_WIDTH), HALO_SHORT)
_conv31 = _make_dwconv(_taps_1d(CF_WIDTH), HALO_CONF)
_conv3x3 = _make_dwconv(tuple((dr * GRID_W + dc, dc) for dr in (-1, 0, 1) for dc in (-1, 0, 1)), HALO_GRID)
_pool_sums = {w: _make_dwconv(tuple((s, 0) for s in range(-(w // 2), w - w // 2)), HALO_SHORT) for w in POOL_WINDOWS}


def _all_gather8(name, blk, in_vmem):
    m_per, n = blk.shape
    space = pltpu.VMEM if in_vmem else pl.ANY

    def body(x_ref, out_ref, send_sems, recv_sems, local_sem):
        x, y, c = lax.axis_index("x"), lax.axis_index("y"), lax.axis_index("c")
        me, sibling = (x, y, c), (x, y, 1 - c)
        chips = [(1 - x, y), (x, 1 - y), (1 - x, 1 - y)]

        def rows(px, py, pc):
            return out_ref.at[pl.ds((4 * px + 2 * py + pc) * m_per, m_per), :]

        def copy(k, block, to, src=None):
            return pltpu.make_async_remote_copy(
                src_ref=rows(*block) if src is None else src, dst_ref=rows(*block),
                send_sem=send_sems.at[k], recv_sem=recv_sems.at[k], device_id=to, device_id_type=MESH)

        mine = pltpu.make_async_copy(x_ref, rows(*me), local_sem)
        mine.start()
        first = [copy(0, me, sibling, src=x_ref)]
        first += [copy(1 + j, me, (*chip, c), src=x_ref) for j, chip in enumerate(chips)]
        for cp in first:
            cp.start()
        passed = [copy(4 + j, (*chip, c), sibling) for j, chip in enumerate(chips)]
        for j, chip in enumerate(chips):
            copy(1 + j, (*chip, c), me).wait_recv()
            passed[j].start()
        copy(0, sibling, me).wait_recv()
        for j, chip in enumerate(chips):
            copy(4 + j, (*chip, 1 - c), me).wait_recv()
        for cp in first + passed:
            cp.wait_send()
        mine.wait()

    return pl.pallas_call(
        body, name=name, out_shape=_sds((8 * m_per, n), blk.dtype),
        in_specs=[pl.BlockSpec(memory_space=space)], out_specs=pl.BlockSpec(memory_space=space),
        scratch_shapes=[pltpu.SemaphoreType.DMA((7,)), pltpu.SemaphoreType.DMA((7,)), pltpu.SemaphoreType.DMA],
        compiler_params=pltpu.CompilerParams(vmem_limit_bytes=VMEM_LIMIT_BYTES),
    )(blk)


def _sibling_exchange(name, send):
    def body(s_ref, out_ref, send_sem, recv_sem):
        x, y, c = lax.axis_index("x"), lax.axis_index("y"), lax.axis_index("c")
        cp = pltpu.make_async_remote_copy(src_ref=s_ref, dst_ref=out_ref, send_sem=send_sem, recv_sem=recv_sem,
                                          device_id=(x, y, 1 - c), device_id_type=MESH)
        cp.start()
        cp.wait()

    return pl.pallas_call(
        body, name=name, out_shape=_sds(send.shape, send.dtype),
        in_specs=[pl.BlockSpec(memory_space=pl.ANY)], out_specs=pl.BlockSpec(memory_space=pl.ANY),
        scratch_shapes=[pltpu.SemaphoreType.DMA, pltpu.SemaphoreType.DMA],
    )(send)


def _chip_scatter(name, part):
    _, r, n = part.shape

    def body(p_ref, out_ref, send_sems, recv_sems, local_sem):
        x, y, c = lax.axis_index("x"), lax.axis_index("y"), lax.axis_index("c")
        sibling = (x, y, 1 - c)
        me_chip = 2 * x + y
        chips = [(1 - x, y), (x, 1 - y), (1 - x, 1 - y)]

        def chip_id(chip):
            return 2 * chip[0] + chip[1]

        def copy(k, src, dst, to):
            return pltpu.make_async_remote_copy(src_ref=src, dst_ref=dst, send_sem=send_sems.at[k],
                                                recv_sem=recv_sems.at[k], device_id=to, device_id_type=MESH)

        mine = pltpu.make_async_copy(p_ref.at[me_chip], out_ref.at[c, me_chip], local_sem)
        mine.start()
        first = [copy(0, p_ref.at[me_chip], out_ref.at[c, me_chip], sibling)]
        first += [copy(1 + j, p_ref.at[chip_id(chip)], out_ref.at[c, me_chip], (*chip, c))
                  for j, chip in enumerate(chips)]
        for cp in first:
            cp.start()
        passed = [copy(4 + j, out_ref.at[c, chip_id(chip)], out_ref.at[c, chip_id(chip)], sibling)
                  for j, chip in enumerate(chips)]
        for j, chip in enumerate(chips):
            copy(1 + j, p_ref.at[0], out_ref.at[c, chip_id(chip)], sibling).wait_recv()
            passed[j].start()
        copy(0, p_ref.at[0], out_ref.at[1 - c, me_chip], sibling).wait_recv()
        for j, chip in enumerate(chips):
            copy(4 + j, p_ref.at[0], out_ref.at[1 - c, chip_id(chip)], sibling).wait_recv()
        for cp in first + passed:
            cp.wait_send()
        mine.wait()

    return pl.pallas_call(
        body, name=name, out_shape=_sds((2, 4, r, n), part.dtype),
        in_specs=[pl.BlockSpec(memory_space=pl.ANY)], out_specs=pl.BlockSpec(memory_space=pl.ANY),
        scratch_shapes=[pltpu.SemaphoreType.DMA((7,)), pltpu.SemaphoreType.DMA((7,)), pltpu.SemaphoreType.DMA],
    )(part)


def _pack_flat(arrays, multiple, dtype):
    flat = jnp.concatenate([a.reshape(-1).astype(dtype) for a in arrays])
    pad = (-flat.shape[0]) % multiple
    if pad:
        flat = jnp.concatenate([flat, jnp.zeros((pad,), dtype)])
    return flat


def _unpack_flat(flat, shapes):
    out, off = [], 0
    for s in shapes:
        n = math.prod(s)
        out.append(flat[off:off + n].reshape(s))
        off += n
    return out


def _row_tile(r, cap, mult=16):
    for cand in range(min(cap, r) // mult * mult, 0, -mult):
        if r % cand == 0:
            return cand
    return r


def _sum_leading(name, stack, out_dtype=F32):
    k, r, n = stack.shape
    tr = _row_tile(r, 2048)

    def fn(s):
        acc = s[0].astype(F32)
        for i in range(1, k):
            acc = acc + s[i].astype(F32)
        return acc

    return _blocked(name, fn, (r // tr,), [stack], [pl.BlockSpec((k, tr, n), lambda t: (0, t, 0))],
                    [_sds((r, n), out_dtype)], [pl.BlockSpec((tr, n), lambda t: (t, 0))])[0]


def _adamw(name, w, g, m, v):
    r, c = w.shape
    tr = _row_tile(r, max(8, (1 << 21) // (4 * c)), 8)
    bc1 = 1.0 - ADAM_B1 ** ADAM_STEP
    bc2 = 1.0 - ADAM_B2 ** ADAM_STEP

    def fn(w, g, m, v):
        nm = ADAM_B1 * m + (1.0 - ADAM_B1) * g
        nv = ADAM_B2 * v + (1.0 - ADAM_B2) * (g * g)
        delta = -ADAM_LR * ((nm / bc1) / (jnp.sqrt(nv / bc2) + ADAM_EPS) + ADAM_WD * w)
        return delta, nm, nv

    spec = pl.BlockSpec((tr, c), lambda t: (t, 0))
    return _blocked(name, fn, (r // tr,), [w, g, m, v], [spec] * 4, [_sds((r, c))] * 3, [spec] * 3)


def _mod_rows(c16, ada_w):
    depth, d, n = ada_w.shape

    def fn(c, w):
        return _mm(_silu(c), w)

    return _blocked("mod_rows", fn, (depth,), [c16, ada_w],
                    [_res(c16.shape), pl.BlockSpec((None, d, n), lambda l: (l, 0, 0))],
                    [_sds((depth, 16, n))], [pl.BlockSpec((None, 16, n), lambda l: (l, 0, 0))])[0]


def _ada_grads(c16, dmod16, ada_w):
    depth, d, n = ada_w.shape

    def fn(c, dm, w):
        return _mm_tn(_silu(c), dm), _mm_nt(dm, w)

    return _blocked("ada_grads", fn, (depth,), [c16, dmod16, ada_w],
                    [_res(c16.shape), pl.BlockSpec((None, 16, n), lambda l: (l, 0, 0)),
                     pl.BlockSpec((None, d, n), lambda l: (l, 0, 0))],
                    [_sds((depth, d, n)), _sds((depth, 16, d))],
                    [pl.BlockSpec((None, d, n), lambda l: (l, 0, 0)), pl.BlockSpec((None, 16, d), lambda l: (l, 0, 0))])


def _cctx_grad(parts, c_ctx_row):
    def fn(p, c):
        tot = ((p[0] + p[1]) + p[2]) + p[3]
        _, vjp = jax.vjp(_silu, c)
        return vjp(tot)[0]

    return _blocked("cctx_grad", fn, (1,), [parts, c_ctx_row], [_res(parts.shape), _res(c_ctx_row.shape)],
                    [_sds(c_ctx_row.shape)], [_res(c_ctx_row.shape)])[0]


def _modulate(x, sh, sc):
    return x * (1.0 + sc) + sh


def _inproj(name, x, sh, sc, w, tm):
    t, d = x.shape
    n = w.shape[1]
    tm = min(tm, t)

    def fn(x, sh, sc, w):
        u = _modulate(x, sh, sc).astype(MXU_DTYPE)
        return jnp.dot(u, w, preferred_element_type=F32), u

    return _blocked(name, fn, (t // tm,), [x, sh, sc, w],
                    [_tok(tm, d), _res(sh.shape), _res(sc.shape), _res(w.shape)],
                    [_sds((t, n)), _sds((t, d), MXU_DTYPE)], [_tok(tm, n), _tok(tm, d)])


def _inproj_bwd(name, dp, x, sh, sc, w, add, tm):
    t, d = x.shape
    n = w.shape[1]
    tm = min(tm, t)
    has_add = add is not None

    def fn(dp, x, sh, sc, w, *rest):
        du = _mm_nt(dp, w)
        _, vjp = jax.vjp(_modulate, x, sh, sc)
        dx, dsh, dsc = vjp(du)
        if has_add:
            dx = dx + rest[0]
        return dx, dsh, dsc

    ins = [dp, x, sh, sc, w] + ([add] if has_add else [])
    specs = [_tok(tm, n), _tok(tm, d), _res(sh.shape), _res(sc.shape), _res(w.shape)] + ([_tok(tm, d)] if has_add else [])
    return _blocked(name, fn, (t // tm,), ins, specs,
                    [_sds((t, d)), _sds(sh.shape), _sds(sc.shape)], [_tok(tm, d), _res(sh.shape), _res(sc.shape)],
                    acc=(1, 2), acc_axis=0)


def _residual_ln(y, x, gt, lng, lnb):
    return _layernorm(ALPHA * x + gt * y, lng, lnb)


def _gdn_mix(o0, o1, pg, yp, normw):
    o = o0 + o1
    heads = []
    for h in range(GDN_HEADS):
        oh = o[:, h * HEAD_DIM:(h + 1) * HEAD_DIM]
        heads.append(oh * lax.rsqrt(jnp.mean(oh * oh, -1, keepdims=True) + RMS_EPS) * normw)
    on = jnp.concatenate(heads, axis=-1) * _silu(pg)
    return jnp.concatenate([on, yp], axis=-1)


def _even_out(o0, o1, p, ypool, x, normw, gt, lng, lnb, w, tm):
    t, d = x.shape
    tm = min(tm, t)
    gate_blk = 3

    def fn(o0, o1, pg, yp, x, normw, gt, lng, lnb, w):
        mix = _gdn_mix(o0, o1, pg, yp, normw).astype(MXU_DTYPE)
        y = jnp.dot(mix, w, preferred_element_type=F32)
        return _residual_ln(y, x, gt, lng, lnb), mix, y

    return _blocked("even_out", fn, (t // tm,), [o0, o1, p, ypool, x, normw, gt, lng, lnb, w],
                    [_tok(tm, 512), _tok(tm, 512), _tok(tm, 512, gate_blk), _tok(tm, 512), _tok(tm, d),
                     _res(normw.shape), _res(gt.shape), _res(lng.shape), _res(lnb.shape), _res(w.shape)],
                    [_sds((t, d)), _sds((t, d), MXU_DTYPE), _sds((t, d))], [_tok(tm, d)] * 3)


def _even_out_bwd(dx1, o0, o1, p, ypool, x, y, normw, gt, lng, lnb, w, tm):
    t, d = x.shape
    tm = min(tm, t)

    def fn(dx1, o0, o1, pg, yp, x, y, normw, gt, lng, lnb, w):
        _, vjp2 = jax.vjp(_residual_ln, y, x, gt, lng, lnb)
        dy, dx, dgt, dlng, dlnb = vjp2(dx1)
        dyb = dy.astype(MXU_DTYPE)
        dmix = _mm_nt(dyb, w)
        _, vjp1 = jax.vjp(_gdn_mix, o0, o1, pg, yp, normw)
        do, _, dpg, dyp, dnormw = vjp1(dmix)
        return do, dpg, dyp, dx, dyb, dnormw, dgt, dlng, dlnb

    return _blocked("even_out_bwd", fn, (t // tm,), [dx1, o0, o1, p, ypool, x, y, normw, gt, lng, lnb, w],
                    [_tok(tm, d), _tok(tm, 512), _tok(tm, 512), _tok(tm, 512, 3), _tok(tm, 512),
                     _tok(tm, d), _tok(tm, d), _res(normw.shape), _res(gt.shape), _res(lng.shape), _res(lnb.shape),
                     _res(w.shape)],
                    [_sds((t, 512)), _sds((t, 512), MXU_DTYPE), _sds((t, 512)), _sds((t, d)), _sds((t, d), MXU_DTYPE),
                     _sds(normw.shape), _sds(gt.shape), _sds(lng.shape), _sds(lnb.shape)],
                    [_tok(tm, 512), _tok(tm, 512), _tok(tm, 512), _tok(tm, d), _tok(tm, d),
                     _res(normw.shape), _res(gt.shape), _res(lng.shape), _res(lnb.shape)],
                    acc=(5, 6, 7, 8), acc_axis=0)


def _odd_mix(ysc, z0, cg, cb):
    z = _silu(_layernorm(z0, cg, cb))
    return jnp.concatenate([ysc, z], axis=-1)


def _odd_out(ysc, z0, x, cg, cb, gt, lng, lnb, w, tm):
    t, d = x.shape
    tm = min(tm, t)

    def fn(ysc, z0, x, cg, cb, gt, lng, lnb, w):
        mix = _odd_mix(ysc, z0, cg, cb).astype(MXU_DTYPE)
        y = jnp.dot(mix, w, preferred_element_type=F32)
        return _residual_ln(y, x, gt, lng, lnb), mix, y

    return _blocked("odd_out", fn, (t // tm,), [ysc, z0, x, cg, cb, gt, lng, lnb, w],
                    [_tok(tm, 512), _tok(tm, 512), _tok(tm, d), _res(cg.shape), _res(cb.shape), _res(gt.shape),
                     _res(lng.shape), _res(lnb.shape), _res(w.shape)],
                    [_sds((t, d)), _sds((t, d), MXU_DTYPE), _sds((t, d))], [_tok(tm, d)] * 3)


def _odd_out_bwd(dx3, ysc, z0, x, y, cg, cb, gt, lng, lnb, w, tm):
    t, d = x.shape
    tm = min(tm, t)

    def fn(dx3, ysc, z0, x, y, cg, cb, gt, lng, lnb, w):
        _, vjp2 = jax.vjp(_residual_ln, y, x, gt, lng, lnb)
        dy, dx, dgt, dlng, dlnb = vjp2(dx3)
        dyb = dy.astype(MXU_DTYPE)
        dmix = _mm_nt(dyb, w)
        _, vjp1 = jax.vjp(_odd_mix, ysc, z0, cg, cb)
        dysc, dz0, dcg, dcb = vjp1(dmix)
        return dysc, dz0, dx, dyb, dcg, dcb, dgt, dlng, dlnb

    return _blocked("odd_out_bwd", fn, (t // tm,), [dx3, ysc, z0, x, y, cg, cb, gt, lng, lnb, w],
                    [_tok(tm, d), _tok(tm, 512), _tok(tm, 512), _tok(tm, d), _tok(tm, d), _res(cg.shape), _res(cb.shape),
                     _res(gt.shape), _res(lng.shape), _res(lnb.shape), _res(w.shape)],
                    [_sds((t, 512)), _sds((t, 512)), _sds((t, d)), _sds((t, d), MXU_DTYPE),
                     _sds(cg.shape), _sds(cb.shape), _sds(gt.shape), _sds(lng.shape), _sds(lnb.shape)],
                    [_tok(tm, 512), _tok(tm, 512), _tok(tm, d), _tok(tm, d),
                     _res(cg.shape), _res(cb.shape), _res(gt.shape), _res(lng.shape), _res(lnb.shape)],
                    acc=(4, 5, 6, 7, 8), acc_axis=0)


def _ffn_down(name, s, x, gt, lng, lnb, w, tm):
    t, d = x.shape
    f = s.shape[1]
    tm = min(tm, t)

    def fn(s, x, gt, lng, lnb, w):
        y = jnp.dot(s, w, preferred_element_type=F32)
        return _residual_ln(y, x, gt, lng, lnb), y

    return _blocked(name, fn, (t // tm,), [s, x, gt, lng, lnb, w],
                    [_tok(tm, f), _tok(tm, d), _res(gt.shape), _res(lng.shape), _res(lnb.shape), _res(w.shape)],
                    [_sds((t, d)), _sds((t, d))], [_tok(tm, d)] * 2)


def _ffn_down_bwd(name, dx2, x, y, gt, lng, lnb, w, tm):
    t, d = x.shape
    f = w.shape[0]
    tm = min(tm, t)

    def fn(dx2, x, y, gt, lng, lnb, w):
        _, vjp2 = jax.vjp(_residual_ln, y, x, gt, lng, lnb)
        dy, dx, dgt, dlng, dlnb = vjp2(dx2)
        dyb = dy.astype(MXU_DTYPE)
        return _mm_nt(dyb, w), dx, dyb, dgt, dlng, dlnb

    return _blocked(name, fn, (t // tm,), [dx2, x, y, gt, lng, lnb, w],
                    [_tok(tm, d), _tok(tm, d), _tok(tm, d), _res(gt.shape), _res(lng.shape), _res(lnb.shape), _res(w.shape)],
                    [_sds((t, f), MXU_DTYPE), _sds((t, d)), _sds((t, d), MXU_DTYPE), _sds(gt.shape), _sds(lng.shape), _sds(lnb.shape)],
                    [_tok(tm, f), _tok(tm, d), _tok(tm, d), _res(gt.shape), _res(lng.shape), _res(lnb.shape)],
                    acc=(3, 4, 5), acc_axis=0)


def _loss_and_grad(x4, target, tm):
    t, d = x4.shape
    tm = min(tm, t)

    def fn(y, tg):
        e = y - tg
        part = 0.5 * jnp.sum(jnp.mean(e * e, axis=-1, keepdims=True), axis=0, keepdims=True)
        return jnp.broadcast_to(part, (1, LANES)), e * (1.0 / d)

    return _blocked("loss", fn, (t // tm,), [x4, target], [_tok(tm, d), _tok(tm, d)],
                    [_sds((1, LANES)), _sds((t, d))], [_res((1, LANES)), _tok(tm, d)], acc=(0,), acc_axis=0)


def _matmul_tn(name, pairs):
    k = pairs[0][0].shape[1]
    n = pairs[0][1].shape[1]
    tk = k if k <= 1024 else k // 2
    tn = 512 if n % 512 == 0 else (384 if n % 384 == 0 else 128)

    def body(*refs):
        acc = None
        for p in range(len(pairs)):
            term = lax.dot_general(refs[2 * p][...], refs[2 * p + 1][...], (((0,), (0,)), ((), ())), preferred_element_type=F32)
            acc = term if acc is None else acc + term
        refs[-1][...] = acc.astype(refs[-1].dtype)

    in_specs, args = [], []
    for a, b in pairs:
        t = a.shape[0]
        in_specs += [pl.BlockSpec((t, tk), lambda i, j: (0, i)), pl.BlockSpec((t, tn), lambda i, j: (0, j))]
        args += [a, b]
    return pl.pallas_call(
        body, name=name, grid=(k // tk, n // tn), in_specs=in_specs,
        out_specs=pl.BlockSpec((tk, tn), lambda i, j: (i, j)), out_shape=_sds((k, n), MXU_DTYPE),
        compiler_params=pltpu.CompilerParams(dimension_semantics=("arbitrary", "arbitrary"),
                                             vmem_limit_bytes=VMEM_LIMIT_BYTES),
    )(*args)


def _chan(t, col_of):
    return pl.BlockSpec((t, LANES), lambda i: (0, col_of(i)))


def _wblk(k, col_of=lambda i: i):
    return pl.BlockSpec((k, LANES), lambda i: (0, col_of(i)))


SEQ_TILE = 256
GRID_SEQ_TILE = 512


def _load_ext(ref, r0, rows, halo, total):
    lo, hi = max(r0 - halo, 0), min(r0 + rows + halo, total)
    parts = []
    if lo > r0 - halo:
        parts.append(jnp.zeros((lo - (r0 - halo), ref.shape[1]), F32))
    parts.append(ref[lo:hi, :].astype(F32))
    if hi < r0 + rows + halo:
        parts.append(jnp.zeros((r0 + rows + halo - hi, ref.shape[1]), F32))
    return parts[0] if len(parts) == 1 else jnp.concatenate(parts, axis=0)


def _seq_stage(name, tile_fn, halo, grid_n, seqs, pars, outs, tile_rows=SEQ_TILE):
    total = seqs[0][0].shape[0]
    rows = min(tile_rows, total)
    ns, npar = len(seqs), len(pars)

    def body(*refs):
        seq_refs, par_refs, out_refs = refs[:ns], refs[ns:ns + npar], refs[ns + npar:]
        par_vals = [r[...] for r in par_refs]
        for r0 in range(0, total, rows):
            exts = [_load_ext(r, r0, rows, halo, total) for r in seq_refs]
            res = tile_fn(r0, *exts, *par_vals)
            for o_ref, v in zip(out_refs, res):
                o_ref[r0:r0 + rows, :] = v.astype(o_ref.dtype)

    return pl.pallas_call(
        body, name=name, grid=(grid_n,), in_specs=[s for _, s in seqs] + [s for _, s in pars],
        out_specs=[s for _, s in outs], out_shape=[o for o, _ in outs],
        compiler_params=pltpu.CompilerParams(dimension_semantics=("arbitrary",), vmem_limit_bytes=VMEM_LIMIT_BYTES),
    )(*[a for a, _ in seqs], *[a for a, _ in pars])


def _seq_stage_bwd(name, tile_fn, halo, grid_n, douts, seqs, pars, dseq_outs, dpar_outs, tile_rows=SEQ_TILE):
    total = seqs[0][0].shape[0]
    rows = min(tile_rows, total)
    groups = [d if isinstance(d, list) else [d] for d in douts]
    douts = [d for g in groups for d in g]
    starts = [sum(len(g) for g in groups[:k]) for k in range(len(groups))]
    nd, ns, npar = len(douts), len(seqs), len(pars)
    widths = [s.block_shape[-1] for _, s in seqs]

    def dtile(dout_refs, k, r0):
        terms = [d[r0:r0 + rows, :].astype(F32) for d in dout_refs[starts[k]:starts[k] + len(groups[k])]]
        return functools.reduce(lambda a, b: a + b, terms)

    def body(*refs):
        dout_refs, seq_refs, par_refs = refs[:nd], refs[nd:nd + ns], refs[nd + ns:nd + ns + npar]
        dseq_refs = refs[nd + ns + npar:nd + 2 * ns + npar]
        dpar_refs = refs[nd + 2 * ns + npar:nd + 2 * ns + 2 * npar]
        accs = refs[nd + 2 * ns + 2 * npar:]
        par_vals = [r[...] for r in par_refs]
        for a in accs:
            for r0 in range(0, total, rows):
                a[r0:r0 + rows, :] = jnp.zeros((rows, a.shape[1]), F32)
        dpars = [jnp.zeros(p.shape, F32) for p in par_vals]
        for r0 in range(0, total, rows):
            exts = [_load_ext(r, r0, rows, halo, total) for r in seq_refs]
            _, vjp = jax.vjp(functools.partial(tile_fn, r0), *exts, *par_vals)
            cts = vjp(tuple(dtile(dout_refs, k, r0) for k in range(len(groups))))
            lo, hi = max(r0 - halo, 0), min(r0 + rows + halo, total)
            for a, dx in zip(accs, cts[:ns]):
                a[lo:hi, :] += dx[lo - (r0 - halo):hi - (r0 - halo)]
            dpars = [acc + g for acc, g in zip(dpars, cts[ns:])]
        for o_ref, a in zip(dseq_refs, accs):
            for r0 in range(0, total, rows):
                o_ref[r0:r0 + rows, :] = a[r0:r0 + rows, :].astype(o_ref.dtype)
        for o_ref, g in zip(dpar_refs, dpars):
            o_ref[...] = g

    return pl.pallas_call(
        body, name=name, grid=(grid_n,),
        in_specs=[s for _, s in douts] + [s for _, s in seqs] + [s for _, s in pars],
        out_specs=[s for _, s in dseq_outs] + [s for _, s in dpar_outs],
        out_shape=[o for o, _ in dseq_outs] + [o for o, _ in dpar_outs],
        scratch_shapes=[pltpu.VMEM((total, w), F32) for w in widths],
        compiler_params=pltpu.CompilerParams(dimension_semantics=("arbitrary",), vmem_limit_bytes=VMEM_LIMIT_BYTES),
    )(*[a for a, _ in douts], *[a for a, _ in seqs], *[a for a, _ in pars])


def _qkv_tile(r0, p_ext, w):
    kind = pl.program_id(0)
    a = _silu(_conv5(p_ext, w))
    nrm = a * lax.rsqrt(jnp.sum(a * a, -1, keepdims=True) + RMS_EPS)
    scale = jnp.where(kind < GDN_HEADS, HEAD_DIM ** -0.5, 1.0).astype(F32)
    sel = (kind < 2 * GDN_HEADS).astype(F32)
    return (sel * (nrm * scale) + (1.0 - sel) * a,)


def _qkv_conv(name, p, w):
    t = p.shape[0]
    nb = w.shape[1] // LANES
    ident = lambda i: i
    return _seq_stage(name, _qkv_tile, HALO_SHORT, nb, [(p, _chan(t, ident))], [(w, _wblk(GDN_CONV))],
                      [(_sds((t, nb * LANES)), _chan(t, ident))])[0]


def _qkv_conv_bwd(name, dqkv0, dqkv1, p, w):
    t = p.shape[0]
    nb = w.shape[1] // LANES
    ident = lambda i: i
    return _seq_stage_bwd(name, _qkv_tile, HALO_SHORT, nb, [[(dqkv0, _chan(t, ident)), (dqkv1, _chan(t, ident))]],
                          [(p, _chan(t, ident))],
                          [(w, _wblk(GDN_CONV))], [(_sds((t, nb * LANES), MXU_DTYPE), _chan(t, ident))],
                          [(_sds(w.shape), _wblk(GDN_CONV))])


def _gates_fn(s, avec, dtvec):
    lane = lax.broadcasted_iota(jnp.int32, s.shape, 1)
    beta = jax.nn.sigmoid(s)
    g = -jnp.exp(avec) * jax.nn.softplus(s + dtvec)
    nh = 2 * GDN_HEADS
    return jnp.where(lane < nh, beta, jnp.where(lane < 2 * nh, g, 0.0))


def _gates(name, p, avec, dtvec, col):
    t = p.shape[0]
    tm = min(512, t)
    return _blocked(name, _gates_fn, (t // tm,), [p, avec, dtvec], [_tok(tm, LANES, col), _res(avec.shape), _res(dtvec.shape)],
                    [_sds((t, LANES))], [_tok(tm, LANES)])[0]


def _gates_bwd(name, dgb, p, avec, dtvec, col):
    t = p.shape[0]
    tm = min(512, t)

    def fn(dgb, s, avec, dtvec):
        _, vjp = jax.vjp(_gates_fn, s, avec, dtvec)
        return vjp(dgb)

    return _blocked(name, fn, (t // tm,), [dgb, p, avec, dtvec],
                    [_tok(tm, LANES), _tok(tm, LANES, col), _res(avec.shape), _res(dtvec.shape)],
                    [_sds((t, LANES), MXU_DTYPE), _sds(avec.shape), _sds(dtvec.shape)],
                    [_tok(tm, LANES), _res(avec.shape), _res(dtvec.shape)], acc=(1, 2), acc_axis=0)


def _make_pool_tile(total):
    def tile(r0, x_ext, scale, *ws):
        ys = []
        for g, win in enumerate(POOL_WINDOWS):
            xg = x_ext[:, g * LANES:(g + 1) * LANES]
            rows = xg.shape[0] - 2 * HALO_SHORT
            pos = r0 + lax.broadcasted_iota(jnp.int32, (rows, LANES), 0)
            lo = jnp.clip(pos - win // 2, 0, total)
            hi = jnp.clip(pos - win // 2 + win, 0, total)
            window_sum = _pool_sums[win](xg, jnp.ones((win, LANES), F32))
            pooled = window_sum / (hi - lo).astype(F32) - _center(xg, HALO_SHORT)
            ys.append(_mm(pooled, ws[g]))
        return (jnp.concatenate(ys, axis=-1) * scale,)

    return tile


def _pool_specs(p, pool_w, pool_scale, col0):
    t = p.shape[0]
    n = len(POOL_WINDOWS) * LANES
    seq = (p, pl.BlockSpec((t, n), lambda i: (0, col0)))
    pars = [(pool_scale, _res(pool_scale.shape))]
    pars += [(pool_w, pl.BlockSpec((None, LANES, LANES), lambda i, g=g: (g, 0, 0))) for g in range(len(POOL_WINDOWS))]
    return t, n, seq, pars


def _pool(p, pool_w, pool_scale, col0):
    t, n, seq, pars = _pool_specs(p, pool_w, pool_scale, col0)
    return _seq_stage("pool", _make_pool_tile(t), HALO_SHORT, 1, [seq], pars, [(_sds((t, n)), _res((t, n)))])[0]


def _pool_bwd(dy, p, pool_w, pool_scale, col0):
    t, n, seq, pars = _pool_specs(p, pool_w, pool_scale, col0)
    wspec = (_sds((LANES, LANES)), _res((LANES, LANES)))
    return _seq_stage_bwd("pool_bwd", _make_pool_tile(t), HALO_SHORT, 1, [(dy, _res(dy.shape))], [seq], pars,
                          [(_sds((t, n), MXU_DTYPE), _res((t, n)))],
                          [(_sds(pool_scale.shape), _res(pool_scale.shape))] + [wspec] * len(POOL_WINDOWS))


def _ffn_conv_tile(r0, a_ext, gt_ext, w):
    return (_silu(_conv3x3(a_ext, w)) * _center(gt_ext, HALO_GRID),)


def _ffn_conv(name, h, w9):
    t = h.shape[0]
    nb = w9.shape[1] // LANES
    ident = lambda i: i
    return _seq_stage(name, _ffn_conv_tile, HALO_GRID, nb, [(h, _chan(t, ident)), (h, _chan(t, lambda i: nb + i))],
                      [(w9, _wblk(9))], [(_sds((t, nb * LANES), MXU_DTYPE), _chan(t, ident))], GRID_SEQ_TILE)[0]


def _ffn_conv_bwd(name, ds, h, w9):
    t = h.shape[0]
    nb = w9.shape[1] // LANES
    ident = lambda i: i
    o = (_sds((t, nb * LANES), MXU_DTYPE), _chan(t, ident))
    return _seq_stage_bwd(name, _ffn_conv_tile, HALO_GRID, nb, [(ds, _chan(t, ident))],
                          [(h, _chan(t, ident)), (h, _chan(t, lambda i: nb + i))], [(w9, _wblk(9))],
                          [o, o], [(_sds(w9.shape), _wblk(9))], GRID_SEQ_TILE)


def _sconv_tile(r0, gb_ext, gc_ext, h_ext, w):
    return (_center(gb_ext, HALO_SHORT) * _conv3(gc_ext * h_ext, w),)


def _sconv_specs(p1, w3):
    t = p1.shape[0]
    nb = w3.shape[1] // LANES
    seqs = [(p1, _chan(t, lambda i: i)), (p1, _chan(t, lambda i: nb + i)), (p1, _chan(t, lambda i: 2 * nb + i))]
    return t, nb, seqs, [(w3, _wblk(SC_WIDTH))]


def _sconv(p1, w3):
    t, nb, seqs, pars = _sconv_specs(p1, w3)
    return _seq_stage("sconv", _sconv_tile, HALO_SHORT, nb, seqs, pars, [(_sds((t, nb * LANES)), _chan(t, lambda i: i))])[0]


def _sconv_bwd(dysc, p1, w3):
    t, nb, seqs, pars = _sconv_specs(p1, w3)
    o = (_sds((t, nb * LANES), MXU_DTYPE), _chan(t, lambda i: i))
    return _seq_stage_bwd("sconv_bwd", _sconv_tile, HALO_SHORT, nb, [(dysc, _chan(t, lambda i: i))], seqs, pars,
                          [o, o, o], [(_sds(w3.shape), _wblk(SC_WIDTH))])


def _conf_tile(r0, ga_ext, gb_ext, w):
    return (_conv31(ga_ext * jax.nn.sigmoid(gb_ext), w),)


def _conf_specs(p1, w31, blk0):
    t = p1.shape[0]
    nb = w31.shape[1] // LANES
    seqs = [(p1, _chan(t, lambda i: blk0 + i)), (p1, _chan(t, lambda i: blk0 + nb + i))]
    return t, nb, seqs, [(w31, _wblk(CF_WIDTH))]


def _confconv(p1, w31, blk0):
    t, nb, seqs, pars = _conf_specs(p1, w31, blk0)
    return _seq_stage("confconv", _conf_tile, HALO_CONF, nb, seqs, pars, [(_sds((t, nb * LANES)), _chan(t, lambda i: i))])[0]


def _confconv_bwd(dz0, p1, w31, blk0):
    t, nb, seqs, pars = _conf_specs(p1, w31, blk0)
    o = (_sds((t, nb * LANES), MXU_DTYPE), _chan(t, lambda i: i))
    return _seq_stage_bwd("confconv_bwd", _conf_tile, HALO_CONF, nb, [(dz0, _chan(t, lambda i: i))], seqs, pars,
                          [o, o], [(_sds(w31.shape), _wblk(CF_WIDTH))])


def _bdot(a, b, ca, cb, hi):
    if hi:
        return lax.dot_general(a, b, (((ca,), (cb,)), ((0,), (0,))), preferred_element_type=F32, precision=HIGHEST)
    return lax.dot_general(a.astype(MXU_DTYPE), b.astype(MXU_DTYPE), (((ca,), (cb,)), ((0,), (0,))), preferred_element_type=F32)


def _make_bmm(hi):
    nn_i = lambda a, b: _bdot(a, b, 2, 1, hi)
    nt_i = lambda a, b: _bdot(a, b, 2, 2, hi)
    tn_i = lambda a, b: _bdot(a, b, 1, 1, hi)
    nn = _mm_vjp(nn_i, lambda g, b: nt_i(g, b), lambda a, g: tn_i(a, g))
    nt = _mm_vjp(nt_i, lambda g, b: nn_i(g, b), lambda a, g: tn_i(g, a))
    tn = _mm_vjp(tn_i, lambda g, b: nt_i(b, g), lambda a, g: nn_i(a, g))
    return nn, nt, tn


_bmm, _bmm_nt, _bmm_tn = _make_bmm(False)
_bmm_hi, _bmm_hi_nt, _bmm_hi_tn = _make_bmm(True)


def _unit_tri_inverse(a):
    c = a.shape[-1]
    eye = (lax.broadcasted_iota(jnp.int32, a.shape, 1) == lax.broadcasted_iota(jnp.int32, a.shape, 2)).astype(F32)
    levels = max(1, int(math.ceil(math.log2(c))) - 1)
    p = eye - a
    m = _bdot(a, a, 2, 1, True)
    for level in range(levels):
        p_next = p + _bdot(p, m, 2, 1, True)
        if level + 1 < levels:
            m = _bdot(m, m, 2, 1, True)
        p = p_next
    return p


@jax.custom_vjp
def _known_inverse(a, tinv):
    return tinv


def _known_inverse_fwd(a, tinv):
    return tinv, tinv


def _known_inverse_bwd(tinv, dt):
    da = -_bdot(_bdot(tinv, dt, 1, 1, True), tinv, 2, 2, True)
    return da, jnp.zeros_like(tinv)


_known_inverse.defvjp(_known_inverse_fwd, _known_inverse_bwd)


def _gdn_chunk(q, k, v, gcol, grow, bcol, s, tinv=None):
    n, c, _ = q.shape
    shape = (n, c, c)
    fwd_dir = lax.broadcasted_iota(jnp.int32, shape, 0) < n // 2
    i = lax.broadcasted_iota(jnp.int32, shape, 1)
    j = lax.broadcasted_iota(jnp.int32, shape, 2)
    order = jnp.where(fwd_dir, i - j, j - i)
    incl = order >= 0
    gc_col = jnp.sum(jnp.where(incl, grow, 0.0), axis=2, keepdims=True)
    gc_row = jnp.sum(jnp.where(order <= 0, gcol, 0.0), axis=1, keepdims=True)
    gtot = jnp.sum(grow, axis=2, keepdims=True)
    decay = jnp.exp(jnp.where(incl, gc_col - gc_row, -1e30))
    kb = k * bcol
    a = jnp.where(order > 0, _bmm_nt(kb, k) * decay, 0.0)
    tinv = _unit_tri_inverse(a) if tinv is None else _known_inverse(a, tinv)
    e_col = jnp.exp(gc_col)
    u = _bmm_hi(tinv, v * bcol)
    w = _bmm_hi(tinv, kb * e_col)
    k_dec = k * jnp.exp(gtot - gc_col)
    v_new = u - _bmm(w, s)
    attn = jnp.where(incl, _bmm_nt(q, k) * decay, 0.0)
    o = _bmm(q * e_col, s) + _bmm(attn, v_new)
    s_next = s * jnp.exp(gtot) + _bmm_tn(k_dec, v_new)
    return o, s_next, tinv


N_CHAINS = 2 * GDN_HEADS
QKV_W = 3 * GDN_HEADS * HEAD_DIM


def _scan_specs(nc, step):
    c = GDN_CHUNK
    seq = [pl.BlockSpec((c, QKV_W), lambda i: (step(i), 0)), pl.BlockSpec((c, QKV_W), lambda i: (nc - 1 - step(i), 0))]
    heads = [pl.BlockSpec((c, GDN_HEADS * HEAD_DIM), lambda i: (step(i), 0)),
             pl.BlockSpec((c, GDN_HEADS * HEAD_DIM), lambda i: (nc - 1 - step(i), 0))]
    colv = pl.BlockSpec((N_CHAINS, None, c, 1), lambda i: (0, step(i), 0, 0))
    rowv = pl.BlockSpec((N_CHAINS, None, 1, c), lambda i: (0, step(i), 0, 0))
    state = pl.BlockSpec((N_CHAINS, HEAD_DIM, HEAD_DIM), lambda i: (0, 0, 0))
    saved = pl.BlockSpec((N_CHAINS, None, HEAD_DIM, HEAD_DIM), lambda i: (0, step(i), 0, 0))
    inv = pl.BlockSpec((N_CHAINS, None, c, c), lambda i: (0, step(i), 0, 0))
    return seq, heads, colv, rowv, state, saved, inv


def _head_cols(h, part):
    lo = (part * GDN_HEADS + h) * HEAD_DIM
    return slice(lo, lo + HEAD_DIM)


def _chain_stack(x0_ref, x1_ref, part):
    return jnp.stack([(x0_ref if n < GDN_HEADS else x1_ref)[:, _head_cols(n % GDN_HEADS, part)] for n in range(N_CHAINS)])


def _scan_fwd(name, qkv, bcol, gcol, grow, s0):
    t = qkv.shape[0]
    nc = t // GDN_CHUNK
    seq, heads, colv, rowv, state, saved, inv = _scan_specs(nc, lambda i: i)

    def body(qkv0_ref, qkv1_ref, b_ref, gc_ref, gr_ref, s0_ref, o0_ref, o1_ref, save_ref, tinv_ref, fin_ref, s_ref):
        @pl.when(pl.program_id(0) == 0)
        def _():
            s_ref[...] = s0_ref[...]

        s = s_ref[...]
        save_ref[...] = s
        q, k, v = [_chain_stack(qkv0_ref, qkv1_ref, part) for part in range(3)]
        o, s_next, tinv = _gdn_chunk(q, k, v, gc_ref[...], gr_ref[...], b_ref[...], s)
        for n in range(N_CHAINS):
            d, h = divmod(n, GDN_HEADS)
            (o0_ref if d == 0 else o1_ref)[:, _head_cols(h, 0)] = o[n]
        tinv_ref[...] = tinv
        s_ref[...] = s_next
        fin_ref[...] = s_next

    hw = GDN_HEADS * HEAD_DIM
    return pl.pallas_call(
        body, name=name, grid=(nc,), in_specs=seq + [colv, colv, rowv, state],
        out_specs=heads + [saved, inv, state],
        out_shape=[_sds((t, hw)), _sds((t, hw)), _sds((N_CHAINS, nc, HEAD_DIM, HEAD_DIM)),
                   _sds((N_CHAINS, nc, GDN_CHUNK, GDN_CHUNK)), _sds((N_CHAINS, HEAD_DIM, HEAD_DIM))],
        scratch_shapes=[pltpu.VMEM((N_CHAINS, HEAD_DIM, HEAD_DIM), F32)],
        compiler_params=pltpu.CompilerParams(dimension_semantics=("arbitrary",), vmem_limit_bytes=VMEM_LIMIT_BYTES),
    )(qkv, qkv, bcol, gcol, grow, s0)


def _scan_bwd(name, do, qkv, bcol, gcol, grow, saved_s, saved_inv, ds_last):
    t = qkv.shape[0]
    nc = t // GDN_CHUNK
    c = GDN_CHUNK
    seq, heads, colv, rowv, state, saved, inv = _scan_specs(nc, lambda i: nc - 1 - i)

    def body(do0_ref, do1_ref, qkv0_ref, qkv1_ref, b_ref, gc_ref, gr_ref, s_ref, tinv_ref, dsl_ref,
             dx0_ref, dx1_ref, db_ref, dgc_ref, dgr_ref, ds0_ref, ds_ref):
        @pl.when(pl.program_id(0) == 0)
        def _():
            ds_ref[...] = dsl_ref[...]

        tinv = tinv_ref[...]
        q, k, v = [_chain_stack(qkv0_ref, qkv1_ref, part) for part in range(3)]
        do = jnp.stack([(do0_ref if n < GDN_HEADS else do1_ref)[:, _head_cols(n % GDN_HEADS, 0)] for n in range(N_CHAINS)])
        _, vjp = jax.vjp(lambda *a: _gdn_chunk(*a, tinv=tinv)[:2], q, k, v, gc_ref[...], gr_ref[...], b_ref[...], s_ref[...])
        dq, dk, dv, dgc, dgr, db, ds = vjp((do, ds_ref[...]))
        for n in range(N_CHAINS):
            d, h = divmod(n, GDN_HEADS)
            dx_ref = dx0_ref if d == 0 else dx1_ref
            dx_ref[:, _head_cols(h, 0)] = dq[n]
            dx_ref[:, _head_cols(h, 1)] = dk[n]
            dx_ref[:, _head_cols(h, 2)] = dv[n]
        db_ref[...] = db
        dgc_ref[...] = dgc
        dgr_ref[...] = dgr
        ds_ref[...] = ds
        ds0_ref[...] = ds

    vec_c = _sds((N_CHAINS, nc, c, 1))
    vec_r = _sds((N_CHAINS, nc, 1, c))
    return pl.pallas_call(
        body, name=name, grid=(nc,),
        in_specs=heads + seq + [colv, colv, rowv, saved, inv, state],
        out_specs=seq + [colv, colv, rowv, state],
        out_shape=[_sds((t, QKV_W)), _sds((t, QKV_W)), vec_c, vec_c, vec_r, _sds((N_CHAINS, HEAD_DIM, HEAD_DIM))],
        scratch_shapes=[pltpu.VMEM((N_CHAINS, HEAD_DIM, HEAD_DIM), F32)],
        compiler_params=pltpu.CompilerParams(dimension_semantics=("arbitrary",), vmem_limit_bytes=VMEM_LIMIT_BYTES),
    )(do, do, qkv, qkv, bcol, gcol, grow, saved_s, saved_inv, ds_last)


def _gate_layouts(gb):
    t = gb.shape[0]
    nc = t // GDN_CHUNK
    nh = GDN_HEADS

    def by_scan_position(a):
        a = a.T.reshape(2 * nh, nc, GDN_CHUNK)
        return jnp.concatenate([a[:nh], a[nh:, ::-1]], axis=0)

    beta = by_scan_position(gb[:, :2 * nh])
    g = by_scan_position(gb[:, 2 * nh:4 * nh])
    return beta[..., None], g[..., None], g[:, :, None, :]


def _gate_layouts_bwd(dbcol, dgcol, dgrow):
    n2, nc, c, _ = dbcol.shape
    nh = n2 // 2
    t = nc * c

    def by_token(a):
        return jnp.concatenate([a[:nh], a[nh:, ::-1]], axis=0).reshape(n2, t).T

    dbeta = by_token(dbcol[..., 0])
    dg = by_token(dgcol[..., 0] + dgrow[:, :, 0, :])
    return jnp.concatenate([dbeta, dg, jnp.zeros((t, LANES - 2 * n2), F32)], axis=1)


def _shard_axis(full_shape, shard_shape):
    return [i for i, (a, b) in enumerate(zip(full_shape, shard_shape)) if a != b][0]


def kernel(x, c, ctx, c_ctx, ada_w, ada_b, ln_g, ln_b, even_w_in, even_w_out, gdn_conv_w, gdn_a_log, gdn_dt_bias, gdn_norm_w, pool_w, pool_scale, odd_w_in, odd_w_out, sconv_w, conf_conv_w, conf_ln_g, conf_ln_b, ffn_w_up, ffn_conv_w, ffn_w_down, loss_target, m_c_ctx, m_ada_w, m_ada_b, m_ln_g, m_ln_b, m_even_w_in, m_even_w_out, m_gdn_conv_w, m_gdn_a_log, m_gdn_dt_bias, m_gdn_norm_w, m_pool_w, m_pool_scale, m_odd_w_in, m_odd_w_out, m_sconv_w, m_conf_conv_w, m_conf_ln_g, m_conf_ln_b, m_ffn_w_up, m_ffn_conv_w, m_ffn_w_down, v_c_ctx, v_ada_w, v_ada_b, v_ln_g, v_ln_b, v_even_w_in, v_even_w_out, v_gdn_conv_w, v_gdn_a_log, v_gdn_dt_bias, v_gdn_norm_w, v_pool_w, v_pool_scale, v_odd_w_in, v_odd_w_out, v_sconv_w, v_conf_conv_w, v_conf_ln_g, v_conf_ln_b, v_ffn_w_up, v_ffn_conv_w, v_ffn_w_down):
    names = ['c_ctx', 'ada_w', 'ada_b', 'ln_g', 'ln_b', 'even_w_in', 'even_w_out', 'gdn_conv_w', 'gdn_a_log',
             'gdn_dt_bias', 'gdn_norm_w', 'pool_w', 'pool_scale', 'odd_w_in', 'odd_w_out', 'sconv_w', 'conf_conv_w',
             'conf_ln_g', 'conf_ln_b', 'ffn_w_up', 'ffn_conv_w', 'ffn_w_down']
    loc = locals()
    wts = {n: loc[n] for n in names}
    mom = {n: loc['m_' + n] for n in names}
    var = {n: loc['v_' + n] for n in names}

    ix, iy, ic = lax.axis_index("x"), lax.axis_index("y"), lax.axis_index("c")
    chip = 2 * ix + iy
    dev = 2 * chip + ic
    d = D_MODEL
    x0, ctx0, tgt = x[0], ctx[0], loss_target[0]
    t, tc = x0.shape[0], ctx0.shape[0]
    depth = ada_w.shape[0]
    n_ada = ada_w.shape[2]

    small_sharded = [gdn_conv_w, sconv_w, conf_conv_w, ffn_conv_w, ln_g, ln_b]
    s1_items = [c] + small_sharded
    s1 = _pack_flat(s1_items, 8 * LANES, F32).reshape(-1, LANES)
    g1 = _all_gather8("gather_small_in", s1, True).reshape(8, -1)
    c_all = g1[:, :d]
    per_chip = [_unpack_flat(g1[2 * k], [a.shape for a in s1_items])[1:] for k in range(4)]
    gdn_conv_f, sconv_f, conf_conv_f, ffn_conv_f, ln_g_f, ln_b_f = [
        jnp.concatenate([per_chip[k][i] for k in range(4)], axis=-1) for i in range(len(small_sharded))]
    c16 = jnp.concatenate([c_all, c_ctx[None], jnp.zeros((7, d), F32)], axis=0)

    mod_part = _mod_rows(c16, ada_w)
    g2 = _all_gather8("gather_mod", mod_part.reshape(-1, LANES), True).reshape(4, 2, depth, 16, n_ada)[:, 0]
    mod_all = jnp.transpose(g2, (1, 2, 0, 3)).reshape(depth, 16, 4 * n_ada) + ada_b[:, None, :]
    mod_me = lax.dynamic_index_in_dim(mod_all, dev, axis=1, keepdims=False).reshape(depth, 6, 1, d)
    sh_c, sc_c = mod_all[0, 8, :d][None], mod_all[0, 8, d:2 * d][None]

    big_names = ['even_w_in', 'even_w_out', 'odd_w_in', 'odd_w_out', 'ffn_w_up', 'ffn_w_down']
    big_full_shapes = {'even_w_in': (d, 4 * even_w_in.shape[1]), 'even_w_out': (4 * even_w_out.shape[0], d),
                       'odd_w_in': (d, 4 * odd_w_in.shape[1]), 'odd_w_out': (4 * odd_w_out.shape[0], d),
                       'ffn_w_up': (depth, d, 4 * ffn_w_up.shape[2]), 'ffn_w_down': (depth, 4 * ffn_w_down.shape[1], d)}
    big_shard_shapes = [wts[n].shape for n in big_names]
    big_axes = [_shard_axis(big_full_shapes[n], wts[n].shape) for n in big_names]
    half_mult = 2 * 16 * LANES
    wpack = _pack_flat([wts[n] for n in big_names], half_mult, MXU_DTYPE).reshape(2, -1, LANES)
    rh = wpack.shape[1]
    my_half = lax.dynamic_index_in_dim(wpack, ic, axis=0, keepdims=False)
    wg = _all_gather8("gather_weights", my_half, False).reshape(4, -1)
    pieces = [_unpack_flat(wg[k], big_shard_shapes) for k in range(4)]
    full = {n: jnp.concatenate([pieces[k][i] for k in range(4)], axis=big_axes[i]) for i, n in enumerate(big_names)}
    n_even = full['even_w_in'].shape[1]
    n_even_pad = -(-n_even // LANES) * LANES
    w_in = jnp.concatenate([full['even_w_in'], jnp.zeros((d, n_even_pad - n_even), MXU_DTYPE)], axis=1)
    w_out, w_oin, w_oout = full['even_w_out'], full['odd_w_in'], full['odd_w_out']
    w_up, w_down = full['ffn_w_up'], full['ffn_w_down']
    scal_blk = (n_even // LANES)
    n_scal = n_even - scal_blk * LANES

    def mod(layer, k):
        return mod_me[layer, k]

    avec = jnp.zeros((1, LANES), F32).at[0, n_scal // 2:n_scal].set(gdn_a_log.reshape(-1))
    dtvec = jnp.zeros((1, LANES), F32).at[0, n_scal // 2:n_scal].set(gdn_dt_bias.reshape(-1))
    normw = gdn_norm_w[None]
    pscale = pool_scale[None]
    cg, cb = conf_ln_g[None], conf_ln_b[None]
    lng = lambda l, k: ln_g_f[l, k][None]
    lnb = lambda l, k: ln_b_f[l, k][None]
    convw9 = ffn_conv_f.reshape(depth, 9, -1)
    nqkv = gdn_conv_f.shape[1]

    p, ub0 = _inproj("even_in", x0, mod(0, 0), mod(0, 1), w_in, 512)
    pc, ucb = _inproj("even_in_ctx", ctx0, sh_c, sc_c, w_in, 256)
    qkv = _qkv_conv("qkv_conv", p, gdn_conv_f)
    qkv_c = _qkv_conv("qkv_conv_ctx", pc, gdn_conv_f)
    gb = _gates("gates", p, avec, dtvec, scal_blk)
    gb_c = _gates("gates_ctx", pc, avec, dtvec, scal_blk)
    lay = _gate_layouts(gb)
    lay_c = _gate_layouts(gb_c)
    s_zero = jnp.zeros((2 * GDN_HEADS, HEAD_DIM, HEAD_DIM), F32)
    _, _, save_c, inv_c, s_ctx = _scan_fwd("scan_ctx", qkv_c, *lay_c, s_zero)
    o0, o1, save_l, inv_l, _ = _scan_fwd("scan", qkv, *lay, s_ctx)
    pool_blk = (nqkv + GDN_HEADS * HEAD_DIM) // 512
    ypool = _pool(p, pool_w, pscale, pool_blk)
    x1, mix0, y0 = _even_out(o0, o1, p, ypool, x0, normw, mod(0, 2), lng(0, 0), lnb(0, 0), w_out, 256)

    def ffn_fwd(l, xin):
        h, ub = _inproj(f"ffn_up{l}", xin, mod(l, 3), mod(l, 4), w_up[l], 256)
        s = _ffn_conv(f"ffn_conv{l}", h, convw9[l])
        xo, y = _ffn_down(f"ffn_down{l}", s, xin, mod(l, 5), lng(l, 1), lnb(l, 1), w_down[l], 256)
        return xo, (h, ub, s, y)

    x2, ffn0 = ffn_fwd(0, x1)
    p1, ub1 = _inproj("odd_in", x2, mod(1, 0), mod(1, 1), w_oin, 512)
    nsc = sconv_f.shape[1] // LANES
    ysc = _sconv(p1, sconv_f)
    z0 = _confconv(p1, conf_conv_f, 3 * nsc)
    x3, mix1, y1 = _odd_out(ysc, z0, x2, cg, cb, mod(1, 2), lng(1, 0), lnb(1, 0), w_oout, 256)
    x4, ffn1 = ffn_fwd(1, x3)
    loss_part, dx4 = _loss_and_grad(x4, tgt, 512)
    loss = lax.psum(loss_part[0, 0], ("x", "y", "c"))

    dmod = [[None] * 6 for _ in range(depth)]
    dlng = [[None, None] for _ in range(depth)]
    dlnb = [[None, None] for _ in range(depth)]
    gbig = {}
    dconv9 = [None] * depth

    def ffn_bwd(l, dxo, xin, saved):
        h, ub, s, y = saved
        ds, dxa, dyb, dgt, dg_, db_ = _ffn_down_bwd(f"ffn_down_bwd{l}", dxo, xin, y, mod(l, 5), lng(l, 1), lnb(l, 1), w_down[l], 256)
        da, dgate, dw9 = _ffn_conv_bwd(f"ffn_conv_bwd{l}", ds, h, convw9[l])
        dh = jnp.concatenate([da, dgate], axis=1)
        dxin, dsh, dsc = _inproj_bwd(f"ffn_up_bwd{l}", dh, xin, mod(l, 3), mod(l, 4), w_up[l], dxa, 256)
        dmod[l][3], dmod[l][4], dmod[l][5] = dsh, dsc, dgt
        dlng[l][1], dlnb[l][1] = dg_, db_
        dconv9[l] = dw9
        return dxin, _matmul_tn(f"dw_up{l}", [(ub, dh)]), _matmul_tn(f"dw_down{l}", [(s, dyb)])

    dx3, dwu1, dwd1 = ffn_bwd(1, dx4, x3, ffn1)
    dysc, dz0, dx2a, dyb1, dcg, dcb, dgt, dg_, db_ = _odd_out_bwd(dx3, ysc, z0, x2, y1, cg, cb, mod(1, 2), lng(1, 0), lnb(1, 0), w_oout, 256)
    dmod[1][2], dlng[1][0], dlnb[1][0] = dgt, dg_, db_
    d_gb, d_gc, d_h, dsconv = _sconv_bwd(dysc, p1, sconv_f)
    d_ga, d_gbb, dconf = _confconv_bwd(dz0, p1, conf_conv_f, 3 * nsc)
    dp1 = jnp.concatenate([d_gb, d_gc, d_h, d_ga, d_gbb], axis=1)
    dx2, dsh, dsc = _inproj_bwd("odd_in_bwd", dp1, x2, mod(1, 0), mod(1, 1), w_oin, dx2a, 512)
    dmod[1][0], dmod[1][1] = dsh, dsc
    gbig['odd_w_out'] = [_matmul_tn("dw_oout", [(mix1, dyb1)])]
    gbig['odd_w_in'] = [_matmul_tn("dw_oin", [(ub1, dp1)])]

    dx1, dwu0, dwd0 = ffn_bwd(0, dx2, x1, ffn0)
    gbig['ffn_w_up'] = [dwu0, dwu1]
    gbig['ffn_w_down'] = [dwd0, dwd1]

    do, dpg, dypool, dx0a, dyb0, dnormw, dgt, dg_, db_ = _even_out_bwd(dx1, o0, o1, p, ypool, x0, y0, normw, mod(0, 2), lng(0, 0), lnb(0, 0), w_out, 256)
    dmod[0][2], dlng[0][0], dlnb[0][0] = dgt, dg_, db_
    pool_cts = _pool_bwd(dypool, p, pool_w, pscale, pool_blk)
    dpp, dpool_scale, dpool_w = pool_cts[0], pool_cts[1], jnp.stack(pool_cts[2:])
    dqkv0, dqkv1, dbcol, dgcol, dgrow, ds0 = _scan_bwd("scan_bwd", do, qkv, *lay, save_l, inv_l, s_zero)
    zero_do = jnp.zeros((tc, GDN_HEADS * HEAD_DIM), F32)
    dqkv0_c, dqkv1_c, dbcol_c, dgcol_c, dgrow_c, _ = _scan_bwd("scan_bwd_ctx", zero_do, qkv_c, *lay_c, save_c, inv_c, ds0)
    dgb = _gate_layouts_bwd(dbcol, dgcol, dgrow)
    dgb_c = _gate_layouts_bwd(dbcol_c, dgcol_c, dgrow_c)
    dps, davec, ddtvec = _gates_bwd("gates_bwd", dgb, p, avec, dtvec, scal_blk)
    dps_c, davec_c, ddtvec_c = _gates_bwd("gates_bwd_ctx", dgb_c, pc, avec, dtvec, scal_blk)
    dpqkv, dconv5 = _qkv_conv_bwd("qkv_conv_bwd", dqkv0, dqkv1, p, gdn_conv_f)
    dpqkv_c, dconv5_c = _qkv_conv_bwd("qkv_conv_bwd_ctx", dqkv0_c, dqkv1_c, pc, gdn_conv_f)
    dp = jnp.concatenate([dpqkv, dpg, dpp, dps], axis=1)
    dpc = jnp.concatenate([dpqkv_c, jnp.zeros((tc, n_even_pad - nqkv - LANES), MXU_DTYPE), dps_c], axis=1)
    grad_x, dsh, dsc = _inproj_bwd("even_in_bwd", dp, x0, mod(0, 0), mod(0, 1), w_in, dx0a, 512)
    dmod[0][0], dmod[0][1] = dsh, dsc
    _, dsh_c, dsc_c = _inproj_bwd("even_in_bwd_ctx", dpc, ctx0, sh_c, sc_c, w_in, None, 256)
    gbig['even_w_in'] = [_matmul_tn("dw_in", [(ub0, dp), (ucb, dpc)])[:, :n_even]]
    gbig['even_w_out'] = [_matmul_tn("dw_out", [(mix0, dyb0)])]

    def shards_of(name, i, k):
        n = big_shard_shapes[i][big_axes[i]]
        axis = big_axes[i] - (wts[name].ndim - 2)
        return [lax.slice_in_dim(g, k * n, (k + 1) * n, axis=axis) for g in gbig[name]]

    gpack = jnp.stack([
        _pack_flat([g for i, n in enumerate(big_names) for g in shards_of(n, i, k)], half_mult, MXU_DTYPE).reshape(2, rh, LANES)
        for k in range(4)], axis=1)
    keep = lax.dynamic_index_in_dim(gpack, ic, axis=0, keepdims=False)
    give = lax.dynamic_index_in_dim(gpack, 1 - ic, axis=0, keepdims=False)
    got = _sibling_exchange("grad_pair_exchange", give)
    pair = _sum_leading("grad_pair_sum", jnp.stack([keep.reshape(-1, LANES), got.reshape(-1, LANES)]), MXU_DTYPE)
    parts = _chip_scatter("grad_chip_scatter", pair.reshape(4, rh, LANES))
    gsum = _sum_leading("grad_chip_sum", jnp.transpose(parts, (1, 0, 2, 3)).reshape(4, 2 * rh, LANES)).reshape(-1)
    g_shards = dict(zip(big_names, _unpack_flat(gsum, big_shard_shapes)))

    dmod_rows = jnp.stack([jnp.concatenate(dmod[l], axis=1)[0] for l in range(depth)])
    dmod_c = jnp.concatenate([dsh_c[0], dsc_c[0], jnp.zeros((4 * d,), F32)])
    dmod_c_rows = jnp.stack([dmod_c] + [jnp.zeros_like(dmod_c)] * (depth - 1))
    small_g = {
        'ln_g': jnp.stack([jnp.stack([dlng[l][k][0] for k in range(2)]) for l in range(depth)]),
        'ln_b': jnp.stack([jnp.stack([dlnb[l][k][0] for k in range(2)]) for l in range(depth)]),
        'gdn_conv_w': dconv5 + dconv5_c,
        'gdn_a_log': (davec + davec_c)[0, n_scal // 2:n_scal].reshape(gdn_a_log.shape),
        'gdn_dt_bias': (ddtvec + ddtvec_c)[0, n_scal // 2:n_scal].reshape(gdn_dt_bias.shape),
        'gdn_norm_w': dnormw[0], 'pool_w': dpool_w, 'pool_scale': dpool_scale[0],
        'sconv_w': dsconv, 'conf_conv_w': dconf, 'conf_ln_g': dcg[0], 'conf_ln_b': dcb[0],
        'ffn_conv_w': jnp.stack(dconv9).reshape(depth, 3, 3, -1),
    }
    small_names = list(small_g)
    s3_items = [dmod_rows, dmod_c_rows] + [small_g[n] for n in small_names]
    s3 = _pack_flat(s3_items, 8 * LANES, F32).reshape(-1, LANES)
    g3 = _all_gather8("gather_small_grads", s3, True).reshape(8, -1, LANES)
    tot3 = _sum_leading("small_grad_sum", g3).reshape(-1)
    tot_items = _unpack_flat(tot3, [a.shape for a in s3_items])
    dmod_sum, dmod_c_sum = tot_items[0], tot_items[1]
    small_tot = dict(zip(small_names, tot_items[2:]))
    grad_ada_b = dmod_sum + dmod_c_sum
    g3f = g3.reshape(8, -1)
    rows_all = g3f[:, :depth * 6 * d].reshape(8, depth, 6 * d)
    cols = lax.dynamic_slice_in_dim(rows_all, chip * n_ada, n_ada, axis=2)
    crow = lax.dynamic_slice_in_dim(dmod_c_sum, chip * n_ada, n_ada, axis=1)
    dmod16 = jnp.concatenate([jnp.transpose(cols, (1, 0, 2)), crow[:, None, :], jnp.zeros((depth, 7, n_ada), F32)], axis=1)
    grad_ada_w, dsil = _ada_grads(c16, dmod16, ada_w)
    s4 = jnp.concatenate([dsil[0, 8][None], jnp.zeros((7, d), F32)], axis=0).reshape(-1, LANES)
    g4 = _all_gather8("gather_cctx", s4, True).reshape(8, 8, d)
    grad_c_ctx = _cctx_grad(g4[0::2, 0][:, None, :], c_ctx[None])[0]

    def my_cols(a, n):
        return lax.dynamic_slice_in_dim(a, chip * n, n, axis=a.ndim - 1)

    grads = dict(g_shards)
    grads['c_ctx'] = grad_c_ctx
    grads['ada_w'] = grad_ada_w
    grads['ada_b'] = grad_ada_b
    for n in ['ln_g', 'ln_b', 'gdn_conv_w', 'sconv_w', 'conf_conv_w', 'ffn_conv_w']:
        grads[n] = my_cols(small_tot[n], wts[n].shape[-1])
    for n in ['gdn_a_log', 'gdn_dt_bias', 'gdn_norm_w', 'pool_w', 'pool_scale', 'conf_ln_g', 'conf_ln_b']:
        grads[n] = small_tot[n]

    delta, new_m, new_v = {}, {}, {}
    big_adam = big_names + ['ada_w']
    for n in big_adam:
        shp = wts[n].shape
        as2d = lambda a: a.reshape(-1, shp[-1])
        dl, nm, nv = _adamw("adamw_" + n, as2d(wts[n]), as2d(grads[n]), as2d(mom[n]), as2d(var[n]))
        delta[n], new_m[n], new_v[n] = dl.reshape(shp), nm.reshape(shp), nv.reshape(shp)
    small_adam = [n for n in names if n not in big_adam]
    packs = [_pack_flat([src[n] for n in small_adam], 8 * LANES, F32).reshape(-1, LANES) for src in (wts, grads, mom, var)]
    outs = _adamw("adamw_small", *packs)
    shapes = [wts[n].shape for n in small_adam]
    for res, dst in zip(outs, (delta, new_m, new_v)):
        for n, a in zip(small_adam, _unpack_flat(res.reshape(-1), shapes)):
            dst[n] = a

    return (loss, grad_x[None], *[grads[n] for n in names], *[delta[n] for n in names],
            *[new_m[n] for n in names], *[new_v[n] for n in names])
```

```python
import functools
import math

import jax
import jax.numpy as jnp
from jax import lax
from jax.experimental import pallas as pl
from jax.experimental.pallas import tpu as pltpu

F32 = jnp.float32
MXU_DTYPE = jnp.bfloat16
HIGHEST = lax.Precision.HIGHEST
MESH = pl.DeviceIdType.MESH

D_MODEL = 1024
GRID_W = 64
GDN_HEADS = 4
HEAD_DIM = 128
GDN_CHUNK = 64
POOL_WINDOWS = (2, 4, 8, 16)
GDN_CONV = 5
SC_WIDTH = 3
CF_WIDTH = 31
ALPHA = 4.0 ** 0.25
LN_EPS = 1e-5
RMS_EPS = 1e-6
LANES = 128
VMEM_LIMIT_BYTES = 58 * 1024 * 1024

ADAM_LR, ADAM_B1, ADAM_B2, ADAM_EPS, ADAM_WD, ADAM_STEP = 0.001, 0.9, 0.999, 1e-08, 0.01, 10


def _blocked(name, fn, grid, ins, in_specs, out_shapes, out_specs, acc=(), acc_axis=None, scratch=()):
    n_in = len(ins)
    n_out = len(out_shapes)

    def body(*refs):
        vals = [r[...] for r in refs[:n_in]]
        res = fn(*vals, *refs[n_in + n_out:])
        if not isinstance(res, (tuple, list)):
            res = (res,)
        for k, (r, v) in enumerate(zip(refs[n_in:n_in + n_out], res)):
            if k in acc:
                first = pl.program_id(acc_axis) == 0

                @pl.when(first)
                def _(r=r, v=v):
                    r[...] = v.astype(r.dtype)

                @pl.when(jnp.logical_not(first))
                def _(r=r, v=v):
                    r[...] += v.astype(r.dtype)
            else:
                r[...] = v.astype(r.dtype)

    return pl.pallas_call(
        body, name=name, grid=grid, in_specs=in_specs, out_specs=out_specs, out_shape=out_shapes,
        scratch_shapes=list(scratch),
        compiler_params=pltpu.CompilerParams(dimension_semantics=("arbitrary",) * len(grid),
                                             vmem_limit_bytes=VMEM_LIMIT_BYTES),
    )(*ins)


def _sds(shape, dtype=F32):
    return jax.ShapeDtypeStruct(tuple(shape), dtype)


def _tok(tm, n, col=0):
    return pl.BlockSpec((tm, n), lambda t: (t, col))


def _res(shape):
    nd = len(shape)
    return pl.BlockSpec(tuple(shape), lambda t: (0,) * nd)


def _silu(x):
    return x * jax.nn.sigmoid(x)


def _layernorm(r, g, b):
    mu = jnp.mean(r, -1, keepdims=True)
    d = r - mu
    var = jnp.mean(d * d, -1, keepdims=True)
    return d * lax.rsqrt(var + LN_EPS) * g + b


def _mm_nn_impl(a, b):
    return jnp.dot(a.astype(MXU_DTYPE), b.astype(MXU_DTYPE), preferred_element_type=F32)


def _mm_nt_impl(a, b):
    return lax.dot_general(a.astype(MXU_DTYPE), b.astype(MXU_DTYPE), (((1,), (1,)), ((), ())), preferred_element_type=F32)


def _mm_tn_impl(a, b):
    return lax.dot_general(a.astype(MXU_DTYPE), b.astype(MXU_DTYPE), (((0,), (0,)), ((), ())), preferred_element_type=F32)


def _mm_vjp(mm, mm_da, mm_db):
    f = jax.custom_vjp(mm)
    f.defvjp(lambda a, b: (mm(a, b), (a, b)), lambda res, g: (mm_da(g, res[1]), mm_db(res[0], g)))
    return f


_mm = _mm_vjp(_mm_nn_impl, lambda g, b: _mm_nt_impl(g, b), lambda a, g: _mm_tn_impl(a, g))
_mm_nt = _mm_vjp(_mm_nt_impl, lambda g, b: _mm_nn_impl(g, b), lambda a, g: _mm_tn_impl(g, a))
_mm_tn = _mm_vjp(_mm_tn_impl, lambda g, b: _mm_nt_impl(b, g), lambda a, g: _mm_nn_impl(a, g))


def _row(w, k):
    rows = lax.broadcasted_iota(jnp.int32, w.shape, 0)
    return jnp.sum(jnp.where(rows == k, w, 0.0), axis=0, keepdims=True)


def _col_mask(shape, dc):
    col = lax.broadcasted_iota(jnp.int32, shape, 0) & (GRID_W - 1)
    return (col + dc >= 0) & (col + dc < GRID_W)


def _center(x_ext, halo):
    return x_ext[halo:x_ext.shape[0] - halo]


def _make_dwconv(taps, halo):
    assert all(abs(s) <= halo for s, _ in taps)

    def shifted(x_ext, s):
        r = x_ext if s == 0 else pltpu.roll(x_ext, (-s) % x_ext.shape[0], 0)
        return _center(r, halo)

    @jax.custom_vjp
    def conv(x_ext, w):
        acc = None
        for k, (s, dc) in enumerate(taps):
            r = shifted(x_ext, s)
            if dc != 0:
                r = jnp.where(_col_mask(r.shape, dc), r, 0.0)
            term = r * _row(w, k)
            acc = term if acc is None else acc + term
        return acc

    def fwd(x_ext, w):
        return conv(x_ext, w), (x_ext, w)

    def bwd(res, dy):
        x_ext, w = res
        n = x_ext.shape[0]
        rows = lax.broadcasted_iota(jnp.int32, w.shape, 0)
        pad = jnp.zeros((halo, dy.shape[1]), F32)
        dx = None
        dw = jnp.zeros(w.shape, F32)
        for k, (s, dc) in enumerate(taps):
            dym = dy if dc == 0 else jnp.where(_col_mask(dy.shape, dc), dy, 0.0)
            dw = dw + jnp.where(rows == k, jnp.sum(dym * shifted(x_ext, s), axis=0, keepdims=True), 0.0)
            t = jnp.concatenate([pad, dym * _row(w, k), pad], axis=0)
            if s != 0:
                t = pltpu.roll(t, s % n, 0)
            dx = t if dx is None else dx + t
        return dx, dw

    conv.defvjp(fwd, bwd)
    return conv


def _taps_1d(width):
    return tuple((k - width // 2, 0) for k in range(width))


HALO_SHORT = 8
HALO_CONF = 16
HALO_GRID = 72
_conv5 = _make_dwconv(_taps_1d(GDN_CONV), HALO_SHORT)
_conv3 = _make_dwconv(_taps_1d(SC_WIDTH), HALO_SHORT)
_conv31 = _make_dwconv(_taps_1d(CF_WIDTH), HALO_CONF)
_conv3x3 = _make_dwconv(tuple((dr * GRID_W + dc, dc) for dr in (-1, 0, 1) for dc in (-1, 0, 1)), HALO_GRID)
_pool_sums = {w: _make_dwconv(tuple((s, 0) for s in range(-(w // 2), w - w // 2)), HALO_SHORT) for w in POOL_WINDOWS}


def _all_gather8(name, blk, in_vmem):
    m_per, n = blk.shape
    space = pltpu.VMEM if in_vmem else pl.ANY

    def body(x_ref, out_ref, send_sems, recv_sems, local_sem):
        x, y, c = lax.axis_index("x"), lax.axis_index("y"), lax.axis_index("c")
        me, sibling = (x, y, c), (x, y, 1 - c)
        chips = [(1 - x, y), (x, 1 - y), (1 - x, 1 - y)]

        def rows(px, py, pc):
            return out_ref.at[pl.ds((4 * px + 2 * py + pc) * m_per, m_per), :]

        def copy(k, block, to, src=None):
            return pltpu.make_async_remote_copy(
                src_ref=rows(*block) if src is None else src, dst_ref=rows(*block),
                send_sem=send_sems.at[k], recv_sem=recv_sems.at[k], device_id=to, device_id_type=MESH)

        mine = pltpu.make_async_copy(x_ref, rows(*me), local_sem)
        mine.start()
        first = [copy(0, me, sibling, src=x_ref)]
        first += [copy(1 + j, me, (*chip, c), src=x_ref) for j, chip in enumerate(chips)]
        for cp in first:
            cp.start()
        passed = [copy(4 + j, (*chip, c), sibling) for j, chip in enumerate(chips)]
        for j, chip in enumerate(chips):
            copy(1 + j, (*chip, c), me).wait_recv()
            passed[j].start()
        copy(0, sibling, me).wait_recv()
        for j, chip in enumerate(chips):
            copy(4 + j, (*chip, 1 - c), me).wait_recv()
        for cp in first + passed:
            cp.wait_send()
        mine.wait()

    return pl.pallas_call(
        body, name=name, out_shape=_sds((8 * m_per, n), blk.dtype),
        in_specs=[pl.BlockSpec(memory_space=space)], out_specs=pl.BlockSpec(memory_space=space),
        scratch_shapes=[pltpu.SemaphoreType.DMA((7,)), pltpu.SemaphoreType.DMA((7,)), pltpu.SemaphoreType.DMA],
        compiler_params=pltpu.CompilerParams(vmem_limit_bytes=VMEM_LIMIT_BYTES),
    )(blk)


def _sibling_exchange(name, send):
    def body(s_ref, out_ref, send_sem, recv_sem):
        x, y, c = lax.axis_index("x"), lax.axis_index("y"), lax.axis_index("c")
        cp = pltpu.make_async_remote_copy(src_ref=s_ref, dst_ref=out_ref, send_sem=send_sem, recv_sem=recv_sem,
                                          device_id=(x, y, 1 - c), device_id_type=MESH)
        cp.start()
        cp.wait()

    return pl.pallas_call(
        body, name=name, out_shape=_sds(send.shape, send.dtype),
        in_specs=[pl.BlockSpec(memory_space=pl.ANY)], out_specs=pl.BlockSpec(memory_space=pl.ANY),
        scratch_shapes=[pltpu.SemaphoreType.DMA, pltpu.SemaphoreType.DMA],
    )(send)


def _chip_scatter(name, part):
    _, r, n = part.shape

    def body(p_ref, out_ref, send_sems, recv_sems, local_sem):
        x, y, c = lax.axis_index("x"), lax.axis_index("y"), lax.axis_index("c")
        sibling = (x, y, 1 - c)
        me_chip = 2 * x + y
        chips = [(1 - x, y), (x, 1 - y), (1 - x, 1 - y)]

        def chip_id(chip):
            return 2 * chip[0] + chip[1]

        def copy(k, src, dst, to):
            return pltpu.make_async_remote_copy(src_ref=src, dst_ref=dst, send_sem=send_sems.at[k],
                                                recv_sem=recv_sems.at[k], device_id=to, device_id_type=MESH)

        mine = pltpu.make_async_copy(p_ref.at[me_chip], out_ref.at[c, me_chip], local_sem)
        mine.start()
        first = [copy(0, p_ref.at[me_chip], out_ref.at[c, me_chip], sibling)]
        first += [copy(1 + j, p_ref.at[chip_id(chip)], out_ref.at[c, me_chip], (*chip, c))
                  for j, chip in enumerate(chips)]
        for cp in first:
            cp.start()
        passed = [copy(4 + j, out_ref.at[c, chip_id(chip)], out_ref.at[c, chip_id(chip)], sibling)
                  for j, chip in enumerate(chips)]
        for j, chip in enumerate(chips):
            copy(1 + j, p_ref.at[0], out_ref.at[c, chip_id(chip)], sibling).wait_recv()
            passed[j].start()
        copy(0, p_ref.at[0], out_ref.at[1 - c, me_chip], sibling).wait_recv()
        for j, chip in enumerate(chips):
            copy(4 + j, p_ref.at[0], out_ref.at[1 - c, chip_id(chip)], sibling).wait_recv()
        for cp in first + passed:
            cp.wait_send()
        mine.wait()

    return pl.pallas_call(
        body, name=name, out_shape=_sds((2, 4, r, n), part.dtype),
        in_specs=[pl.BlockSpec(memory_space=pl.ANY)], out_specs=pl.BlockSpec(memory_space=pl.ANY),
        scratch_shapes=[pltpu.SemaphoreType.DMA((7,)), pltpu.SemaphoreType.DMA((7,)), pltpu.SemaphoreType.DMA],
    )(part)


_HBM = pl.BlockSpec(memory_space=pltpu.HBM)
_SEM = pl.BlockSpec(memory_space=pltpu.SEMAPHORE)
_N_PEER_CHIPS = 3


def _peer_chips():
    x, y = lax.axis_index("x"), lax.axis_index("y")
    return [(1 - x, y), (x, 1 - y), (1 - x, 1 - y)]


def _chip_gather_start(name, blk, before):
    n_sem = 2 * _N_PEER_CHIPS

    def body(x_ref, land_ref, before_ref, *rest):
        sems, token = rest[:n_sem], rest[-1]
        c = lax.axis_index("c")
        me_chip = 2 * lax.axis_index("x") + lax.axis_index("y")
        for j, chip in enumerate(_peer_chips()):
            pltpu.make_async_remote_copy(src_ref=x_ref, dst_ref=land_ref.at[me_chip], send_sem=sems[j],
                                         recv_sem=sems[_N_PEER_CHIPS + j], device_id=(*chip, c), device_id_type=MESH).start()
        token[...] = jnp.zeros(token.shape, token.dtype)

    land = lax.empty((4,) + blk.shape, blk.dtype)
    outs = pl.pallas_call(
        body, name=name,
        out_shape=tuple([pltpu.SemaphoreType.DMA(())] * n_sem) + (pltpu.HBM(blk.shape, blk.dtype), pltpu.HBM(land.shape, land.dtype),
                                                                 _sds((8, LANES))),
        in_specs=(_HBM, _HBM, pl.BlockSpec(memory_space=pl.ANY)),
        out_specs=tuple([_SEM] * n_sem) + (_HBM, _HBM, pl.BlockSpec(memory_space=pltpu.VMEM)),
        input_output_aliases={0: n_sem, 1: n_sem + 1},
        compiler_params=pltpu.CompilerParams(has_side_effects=pltpu.SideEffectType.DATAFLOW_SIDE_EFFECTING),
    )(pltpu.with_memory_space_constraint(blk, pltpu.HBM), pltpu.with_memory_space_constraint(land, pltpu.HBM), before)
    return outs[:n_sem], outs[n_sem], outs[n_sem + 1], outs[n_sem + 2]


def _chip_gather_wait(name, sems, blk_thru, land_thru, after):
    n_sem = 2 * _N_PEER_CHIPS

    def body(x_ref, land_ref, *rest):
        sems = rest[:n_sem]
        c = lax.axis_index("c")
        for j, chip in enumerate(_peer_chips()):
            cp = pltpu.make_async_remote_copy(src_ref=x_ref, dst_ref=land_ref.at[2 * chip[0] + chip[1]], send_sem=sems[j],
                                              recv_sem=sems[_N_PEER_CHIPS + j], device_id=(*chip, c), device_id_type=MESH)
            cp.wait_send()
            cp.wait_recv()

    return pl.pallas_call(
        body, name=name, out_shape=(pltpu.HBM(blk_thru.shape, blk_thru.dtype), pltpu.HBM(land_thru.shape, land_thru.dtype)),
        in_specs=(_HBM, _HBM) + tuple([_SEM] * n_sem) + (pl.BlockSpec(memory_space=pl.ANY),), out_specs=(_HBM, _HBM),
        input_output_aliases={0: 0, 1: 1},
        compiler_params=pltpu.CompilerParams(has_side_effects=pltpu.SideEffectType.DATAFLOW_SIDE_EFFECTING),
    )(blk_thru, land_thru, *sems, after)


def _sibling_merge(name, mine, land):
    m, n = mine.shape

    def body(x_ref, land_ref, out_ref, send_sems, recv_sems, local_sems):
        x, y, c = lax.axis_index("x"), lax.axis_index("y"), lax.axis_index("c")
        sibling = (x, y, 1 - c)
        me_chip = 2 * x + y
        chip_ids = [me_chip] + [2 * chip[0] + chip[1] for chip in _peer_chips()]
        srcs = [x_ref] + [land_ref.at[k] for k in chip_ids[1:]]
        local = [pltpu.make_async_copy(src, out_ref.at[k, c], local_sems.at[j]) for j, (src, k) in enumerate(zip(srcs, chip_ids))]
        remote = [pltpu.make_async_remote_copy(src_ref=src, dst_ref=out_ref.at[k, c], send_sem=send_sems.at[j], recv_sem=recv_sems.at[j],
                                               device_id=sibling, device_id_type=MESH) for j, (src, k) in enumerate(zip(srcs, chip_ids))]
        for cp in local + remote:
            cp.start()
        for j, k in enumerate(chip_ids):
            pltpu.make_async_remote_copy(src_ref=srcs[j], dst_ref=out_ref.at[k, 1 - c], send_sem=send_sems.at[j],
                                         recv_sem=recv_sems.at[j], device_id=sibling, device_id_type=MESH).wait_recv()
        for cp in remote:
            cp.wait_send()
        for cp in local:
            cp.wait()

    return pl.pallas_call(
        body, name=name, out_shape=_sds((4, 2, m, n), mine.dtype),
        in_specs=[pl.BlockSpec(memory_space=pl.ANY)] * 2, out_specs=pl.BlockSpec(memory_space=pl.ANY),
        scratch_shapes=[pltpu.SemaphoreType.DMA((4,)), pltpu.SemaphoreType.DMA((4,)), pltpu.SemaphoreType.DMA((4,))],
    )(mine, land)


def _pack_flat(arrays, multiple, dtype):
    flat = jnp.concatenate([a.reshape(-1).astype(dtype) for a in arrays])
    pad = (-flat.shape[0]) % multiple
    if pad:
        flat = jnp.concatenate([flat, jnp.zeros((pad,), dtype)])
    return flat


def _unpack_flat(flat, shapes):
    out, off = [], 0
    for s in shapes:
        n = math.prod(s)
        out.append(flat[off:off + n].reshape(s))
        off += n
    return out


def _row_tile(r, cap, mult=16):
    for cand in range(min(cap, r) // mult * mult, 0, -mult):
        if r % cand == 0:
            return cand
    return r


def _sum_axis1(name, x, out_dtype=F32):
    h, k, r, n = x.shape
    tr = _row_tile(r, 2048)

    def fn(s):
        acc = s[0].astype(F32)
        for i in range(1, k):
            acc = acc + s[i].astype(F32)
        return acc

    return _blocked(name, fn, (h, r // tr), [x], [pl.BlockSpec((None, k, tr, n), lambda i, t: (i, 0, t, 0))],
                    [_sds((h, r, n), out_dtype)], [pl.BlockSpec((None, tr, n), lambda i, t: (i, t, 0))])[0]


def _sum_pair(name, a, b, out_dtype):
    r, n = a.shape
    tr = _row_tile(r, 2048)
    spec = pl.BlockSpec((tr, n), lambda t: (t, 0))
    return _blocked(name, lambda x, y: x.astype(F32) + y.astype(F32), (r // tr,), [a, b], [spec, spec],
                    [_sds((r, n), out_dtype)], [spec])[0]


def _adamw(name, w, g, m, v):
    r, c = w.shape
    tr = _row_tile(r, max(8, (1 << 21) // (4 * c)), 8)
    bc1 = 1.0 - ADAM_B1 ** ADAM_STEP
    bc2 = 1.0 - ADAM_B2 ** ADAM_STEP

    def fn(w, g, m, v):
        nm = ADAM_B1 * m + (1.0 - ADAM_B1) * g
        nv = ADAM_B2 * v + (1.0 - ADAM_B2) * (g * g)
        delta = -ADAM_LR * ((nm / bc1) / (jnp.sqrt(nv / bc2) + ADAM_EPS) + ADAM_WD * w)
        return delta, nm, nv

    spec = pl.BlockSpec((tr, c), lambda t: (t, 0))
    return _blocked(name, fn, (r // tr,), [w, g, m, v], [spec] * 4, [_sds((r, c))] * 3, [spec] * 3)


def _mod_rows(c16, ada_w):
    depth, d, n = ada_w.shape

    def fn(c, w):
        return _mm(_silu(c), w)

    return _blocked("mod_rows", fn, (depth,), [c16, ada_w],
                    [_res(c16.shape), pl.BlockSpec((None, d, n), lambda l: (l, 0, 0))],
                    [_sds((depth, 16, n))], [pl.BlockSpec((None, 16, n), lambda l: (l, 0, 0))])[0]


def _ada_grads(c16, dmod16, ada_w):
    depth, d, n = ada_w.shape

    def fn(c, dm, w):
        return _mm_tn(_silu(c), dm), _mm_nt(dm, w)

    return _blocked("ada_grads", fn, (depth,), [c16, dmod16, ada_w],
                    [_res(c16.shape), pl.BlockSpec((None, 16, n), lambda l: (l, 0, 0)),
                     pl.BlockSpec((None, d, n), lambda l: (l, 0, 0))],
                    [_sds((depth, d, n)), _sds((depth, 16, d))],
                    [pl.BlockSpec((None, d, n), lambda l: (l, 0, 0)), pl.BlockSpec((None, 16, d), lambda l: (l, 0, 0))])


def _cctx_grad(parts, c_ctx_row):
    def fn(p, c):
        tot = ((p[0] + p[1]) + p[2]) + p[3]
        _, vjp = jax.vjp(_silu, c)
        return vjp(tot)[0]

    return _blocked("cctx_grad", fn, (1,), [parts, c_ctx_row], [_res(parts.shape), _res(c_ctx_row.shape)],
                    [_sds(c_ctx_row.shape)], [_res(c_ctx_row.shape)])[0]


def _modulate(x, sh, sc):
    return x * (1.0 + sc) + sh


def _inproj(name, x, sh, sc, w, tm):
    t, d = x.shape
    n = w.shape[1]
    tm = min(tm, t)

    def fn(x, sh, sc, w):
        u = _modulate(x, sh, sc).astype(MXU_DTYPE)
        return jnp.dot(u, w, preferred_element_type=F32), u

    return _blocked(name, fn, (t // tm,), [x, sh, sc, w],
                    [_tok(tm, d), _res(sh.shape), _res(sc.shape), _res(w.shape)],
                    [_sds((t, n)), _sds((t, d), MXU_DTYPE)], [_tok(tm, n), _tok(tm, d)])


def _inproj_bwd(name, dp, x, sh, sc, w, add, tm):
    t, d = x.shape
    n = w.shape[1]
    tm = min(tm, t)
    has_add = add is not None

    def fn(dp, x, sh, sc, w, *rest):
        du = _mm_nt(dp, w)
        _, vjp = jax.vjp(_modulate, x, sh, sc)
        dx, dsh, dsc = vjp(du)
        if has_add:
            dx = dx + rest[0]
        return dx, dsh, dsc

    ins = [dp, x, sh, sc, w] + ([add] if has_add else [])
    specs = [_tok(tm, n), _tok(tm, d), _res(sh.shape), _res(sc.shape), _res(w.shape)] + ([_tok(tm, d)] if has_add else [])
    return _blocked(name, fn, (t // tm,), ins, specs,
                    [_sds((t, d)), _sds(sh.shape), _sds(sc.shape)], [_tok(tm, d), _res(sh.shape), _res(sc.shape)],
                    acc=(1, 2), acc_axis=0)


def _residual_ln(y, x, gt, lng, lnb):
    return _layernorm(ALPHA * x + gt * y, lng, lnb)


def _gdn_mix(o0, o1, pg, yp, normw):
    o = o0 + o1
    heads = []
    for h in range(GDN_HEADS):
        oh = o[:, h * HEAD_DIM:(h + 1) * HEAD_DIM]
        heads.append(oh * lax.rsqrt(jnp.mean(oh * oh, -1, keepdims=True) + RMS_EPS) * normw)
    on = jnp.concatenate(heads, axis=-1) * _silu(pg)
    return jnp.concatenate([on, yp], axis=-1)


def _even_out(o0, o1, p, ypool, x, normw, gt, lng, lnb, w, tm):
    t, d = x.shape
    tm = min(tm, t)
    gate_blk = 3

    def fn(o0, o1, pg, yp, x, normw, gt, lng, lnb, w):
        mix = _gdn_mix(o0, o1, pg, yp, normw).astype(MXU_DTYPE)
        y = jnp.dot(mix, w, preferred_element_type=F32)
        return _residual_ln(y, x, gt, lng, lnb), mix, y

    return _blocked("even_out", fn, (t // tm,), [o0, o1, p, ypool, x, normw, gt, lng, lnb, w],
                    [_tok(tm, 512), _tok(tm, 512), _tok(tm, 512, gate_blk), _tok(tm, 512), _tok(tm, d),
                     _res(normw.shape), _res(gt.shape), _res(lng.shape), _res(lnb.shape), _res(w.shape)],
                    [_sds((t, d)), _sds((t, d), MXU_DTYPE), _sds((t, d))], [_tok(tm, d)] * 3)


def _even_out_bwd(dx1, o0, o1, p, ypool, x, y, normw, gt, lng, lnb, w, tm):
    t, d = x.shape
    tm = min(tm, t)

    def fn(dx1, o0, o1, pg, yp, x, y, normw, gt, lng, lnb, w):
        _, vjp2 = jax.vjp(_residual_ln, y, x, gt, lng, lnb)
        dy, dx, dgt, dlng, dlnb = vjp2(dx1)
        dyb = dy.astype(MXU_DTYPE)
        dmix = _mm_nt(dyb, w)
        _, vjp1 = jax.vjp(_gdn_mix, o0, o1, pg, yp, normw)
        do, _, dpg, dyp, dnormw = vjp1(dmix)
        return do, dpg, dyp, dx, dyb, dnormw, dgt, dlng, dlnb

    return _blocked("even_out_bwd", fn, (t // tm,), [dx1, o0, o1, p, ypool, x, y, normw, gt, lng, lnb, w],
                    [_tok(tm, d), _tok(tm, 512), _tok(tm, 512), _tok(tm, 512, 3), _tok(tm, 512),
                     _tok(tm, d), _tok(tm, d), _res(normw.shape), _res(gt.shape), _res(lng.shape), _res(lnb.shape),
                     _res(w.shape)],
                    [_sds((t, 512)), _sds((t, 512), MXU_DTYPE), _sds((t, 512)), _sds((t, d)), _sds((t, d), MXU_DTYPE),
                     _sds(normw.shape), _sds(gt.shape), _sds(lng.shape), _sds(lnb.shape)],
                    [_tok(tm, 512), _tok(tm, 512), _tok(tm, 512), _tok(tm, d), _tok(tm, d),
                     _res(normw.shape), _res(gt.shape), _res(lng.shape), _res(lnb.shape)],
                    acc=(5, 6, 7, 8), acc_axis=0)


def _odd_mix(ysc, z0, cg, cb):
    z = _silu(_layernorm(z0, cg, cb))
    return jnp.concatenate([ysc, z], axis=-1)


def _odd_out(ysc, z0, x, cg, cb, gt, lng, lnb, w, tm):
    t, d = x.shape
    tm = min(tm, t)

    def fn(ysc, z0, x, cg, cb, gt, lng, lnb, w):
        mix = _odd_mix(ysc, z0, cg, cb).astype(MXU_DTYPE)
        y = jnp.dot(mix, w, preferred_element_type=F32)
        return _residual_ln(y, x, gt, lng, lnb), mix, y

    return _blocked("odd_out", fn, (t // tm,), [ysc, z0, x, cg, cb, gt, lng, lnb, w],
                    [_tok(tm, 512), _tok(tm, 512), _tok(tm, d), _res(cg.shape), _res(cb.shape), _res(gt.shape),
                     _res(lng.shape), _res(lnb.shape), _res(w.shape)],
                    [_sds((t, d)), _sds((t, d), MXU_DTYPE), _sds((t, d))], [_tok(tm, d)] * 3)


def _odd_out_bwd(dx3, ysc, z0, x, y, cg, cb, gt, lng, lnb, w, tm):
    t, d = x.shape
    tm = min(tm, t)

    def fn(dx3, ysc, z0, x, y, cg, cb, gt, lng, lnb, w):
        _, vjp2 = jax.vjp(_residual_ln, y, x, gt, lng, lnb)
        dy, dx, dgt, dlng, dlnb = vjp2(dx3)
        dyb = dy.astype(MXU_DTYPE)
        dmix = _mm_nt(dyb, w)
        _, vjp1 = jax.vjp(_odd_mix, ysc, z0, cg, cb)
        dysc, dz0, dcg, dcb = vjp1(dmix)
        return dysc, dz0, dx, dyb, dcg, dcb, dgt, dlng, dlnb

    return _blocked("odd_out_bwd", fn, (t // tm,), [dx3, ysc, z0, x, y, cg, cb, gt, lng, lnb, w],
                    [_tok(tm, d), _tok(tm, 512), _tok(tm, 512), _tok(tm, d), _tok(tm, d), _res(cg.shape), _res(cb.shape),
                     _res(gt.shape), _res(lng.shape), _res(lnb.shape), _res(w.shape)],
                    [_sds((t, 512)), _sds((t, 512)), _sds((t, d)), _sds((t, d), MXU_DTYPE),
                     _sds(cg.shape), _sds(cb.shape), _sds(gt.shape), _sds(lng.shape), _sds(lnb.shape)],
                    [_tok(tm, 512), _tok(tm, 512), _tok(tm, d), _tok(tm, d),
                     _res(cg.shape), _res(cb.shape), _res(gt.shape), _res(lng.shape), _res(lnb.shape)],
                    acc=(4, 5, 6, 7, 8), acc_axis=0)


def _ffn_down(name, s, x, gt, lng, lnb, w, tm):
    t, d = x.shape
    f = s.shape[1]
    tm = min(tm, t)

    def fn(s, x, gt, lng, lnb, w):
        y = jnp.dot(s, w, preferred_element_type=F32)
        return _residual_ln(y, x, gt, lng, lnb), y

    return _blocked(name, fn, (t // tm,), [s, x, gt, lng, lnb, w],
                    [_tok(tm, f), _tok(tm, d), _res(gt.shape), _res(lng.shape), _res(lnb.shape), _res(w.shape)],
                    [_sds((t, d)), _sds((t, d))], [_tok(tm, d)] * 2)


def _ffn_down_bwd(name, dx2, x, y, gt, lng, lnb, w, tm):
    t, d = x.shape
    f = w.shape[0]
    tm = min(tm, t)

    def fn(dx2, x, y, gt, lng, lnb, w):
        _, vjp2 = jax.vjp(_residual_ln, y, x, gt, lng, lnb)
        dy, dx, dgt, dlng, dlnb = vjp2(dx2)
        dyb = dy.astype(MXU_DTYPE)
        return _mm_nt(dyb, w), dx, dyb, dgt, dlng, dlnb

    return _blocked(name, fn, (t // tm,), [dx2, x, y, gt, lng, lnb, w],
                    [_tok(tm, d), _tok(tm, d), _tok(tm, d), _res(gt.shape), _res(lng.shape), _res(lnb.shape), _res(w.shape)],
                    [_sds((t, f), MXU_DTYPE), _sds((t, d)), _sds((t, d), MXU_DTYPE), _sds(gt.shape), _sds(lng.shape), _sds(lnb.shape)],
                    [_tok(tm, f), _tok(tm, d), _tok(tm, d), _res(gt.shape), _res(lng.shape), _res(lnb.shape)],
                    acc=(3, 4, 5), acc_axis=0)


def _loss_and_grad(x4, target, tm):
    t, d = x4.shape
    tm = min(tm, t)

    def fn(y, tg):
        e = y - tg
        part = 0.5 * jnp.sum(jnp.mean(e * e, axis=-1, keepdims=True), axis=0, keepdims=True)
        return jnp.broadcast_to(part, (1, LANES)), e * (1.0 / d)

    return _blocked("loss", fn, (t // tm,), [x4, target], [_tok(tm, d), _tok(tm, d)],
                    [_sds((1, LANES)), _sds((t, d))], [_res((1, LANES)), _tok(tm, d)], acc=(0,), acc_axis=0)


def _matmul_tn(name, pairs):
    k = pairs[0][0].shape[1]
    n = pairs[0][1].shape[1]
    tk = k if k <= 1024 else k // 2
    tn = 512 if n % 512 == 0 else (384 if n % 384 == 0 else 128)

    def body(*refs):
        acc = None
        for p in range(len(pairs)):
            term = lax.dot_general(refs[2 * p][...], refs[2 * p + 1][...], (((0,), (0,)), ((), ())), preferred_element_type=F32)
            acc = term if acc is None else acc + term
        refs[-1][...] = acc.astype(refs[-1].dtype)

    in_specs, args = [], []
    for a, b in pairs:
        t = a.shape[0]
        in_specs += [pl.BlockSpec((t, tk), lambda i, j: (0, i)), pl.BlockSpec((t, tn), lambda i, j: (0, j))]
        args += [a, b]
    return pl.pallas_call(
        body, name=name, grid=(k // tk, n // tn), in_specs=in_specs,
        out_specs=pl.BlockSpec((tk, tn), lambda i, j: (i, j)), out_shape=_sds((k, n), MXU_DTYPE),
        compiler_params=pltpu.CompilerParams(dimension_semantics=("arbitrary", "arbitrary"),
                                             vmem_limit_bytes=VMEM_LIMIT_BYTES),
    )(*args)


def _chan(t, col_of):
    return pl.BlockSpec((t, LANES), lambda i: (0, col_of(i)))


def _wblk(k, col_of=lambda i: i):
    return pl.BlockSpec((k, LANES), lambda i: (0, col_of(i)))


SEQ_TILE = 256
GRID_SEQ_TILE = 512


def _load_ext(ref, r0, rows, halo, total):
    lo, hi = max(r0 - halo, 0), min(r0 + rows + halo, total)
    parts = []
    if lo > r0 - halo:
        parts.append(jnp.zeros((lo - (r0 - halo), ref.shape[1]), F32))
    parts.append(ref[lo:hi, :].astype(F32))
    if hi < r0 + rows + halo:
        parts.append(jnp.zeros((r0 + rows + halo - hi, ref.shape[1]), F32))
    return parts[0] if len(parts) == 1 else jnp.concatenate(parts, axis=0)


def _seq_stage(name, tile_fn, halo, grid_n, seqs, pars, outs, tile_rows=SEQ_TILE):
    total = seqs[0][0].shape[0]
    rows = min(tile_rows, total)
    ns, npar = len(seqs), len(pars)

    def body(*refs):
        seq_refs, par_refs, out_refs = refs[:ns], refs[ns:ns + npar], refs[ns + npar:]
        par_vals = [r[...] for r in par_refs]
        for r0 in range(0, total, rows):
            exts = [_load_ext(r, r0, rows, halo, total) for r in seq_refs]
            res = tile_fn(r0, *exts, *par_vals)
            for o_ref, v in zip(out_refs, res):
                o_ref[r0:r0 + rows, :] = v.astype(o_ref.dtype)

    return pl.pallas_call(
        body, name=name, grid=(grid_n,), in_specs=[s for _, s in seqs] + [s for _, s in pars],
        out_specs=[s for _, s in outs], out_shape=[o for o, _ in outs],
        compiler_params=pltpu.CompilerParams(dimension_semantics=("arbitrary",), vmem_limit_bytes=VMEM_LIMIT_BYTES),
    )(*[a for a, _ in seqs], *[a for a, _ in pars])


def _seq_stage_bwd(name, tile_fn, halo, grid_n, douts, seqs, pars, dseq_outs, dpar_outs, tile_rows=SEQ_TILE):
    total = seqs[0][0].shape[0]
    rows = min(tile_rows, total)
    groups = [d if isinstance(d, list) else [d] for d in douts]
    douts = [d for g in groups for d in g]
    starts = [sum(len(g) for g in groups[:k]) for k in range(len(groups))]
    nd, ns, npar = len(douts), len(seqs), len(pars)
    widths = [s.block_shape[-1] for _, s in seqs]

    def dtile(dout_refs, k, r0):
        terms = [d[r0:r0 + rows, :].astype(F32) for d in dout_refs[starts[k]:starts[k] + len(groups[k])]]
        return functools.reduce(lambda a, b: a + b, terms)

    def body(*refs):
        dout_refs, seq_refs, par_refs = refs[:nd], refs[nd:nd + ns], refs[nd + ns:nd + ns + npar]
        dseq_refs = refs[nd + ns + npar:nd + 2 * ns + npar]
        dpar_refs = refs[nd + 2 * ns + npar:nd + 2 * ns + 2 * npar]
        accs = refs[nd + 2 * ns + 2 * npar:]
        par_vals = [r[...] for r in par_refs]
        for a in accs:
            for r0 in range(0, total, rows):
                a[r0:r0 + rows, :] = jnp.zeros((rows, a.shape[1]), F32)
        dpars = [jnp.zeros(p.shape, F32) for p in par_vals]
        for r0 in range(0, total, rows):
            exts = [_load_ext(r, r0, rows, halo, total) for r in seq_refs]
            _, vjp = jax.vjp(functools.partial(tile_fn, r0), *exts, *par_vals)
            cts = vjp(tuple(dtile(dout_refs, k, r0) for k in range(len(groups))))
            lo, hi = max(r0 - halo, 0), min(r0 + rows + halo, total)
            for a, dx in zip(accs, cts[:ns]):
                a[lo:hi, :] += dx[lo - (r0 - halo):hi - (r0 - halo)]
            dpars = [acc + g for acc, g in zip(dpars, cts[ns:])]
        for o_ref, a in zip(dseq_refs, accs):
            for r0 in range(0, total, rows):
                o_ref[r0:r0 + rows, :] = a[r0:r0 + rows, :].astype(o_ref.dtype)
        for o_ref, g in zip(dpar_refs, dpars):
            o_ref[...] = g

    return pl.pallas_call(
        body, name=name, grid=(grid_n,),
        in_specs=[s for _, s in douts] + [s for _, s in seqs] + [s for _, s in pars],
        out_specs=[s for _, s in dseq_outs] + [s for _, s in dpar_outs],
        out_shape=[o for o, _ in dseq_outs] + [o for o, _ in dpar_outs],
        scratch_shapes=[pltpu.VMEM((total, w), F32) for w in widths],
        compiler_params=pltpu.CompilerParams(dimension_semantics=("arbitrary",), vmem_limit_bytes=VMEM_LIMIT_BYTES),
    )(*[a for a, _ in douts], *[a for a, _ in seqs], *[a for a, _ in pars])


def _qkv_tile(r0, p_ext, w):
    kind = pl.program_id(0)
    a = _silu(_conv5(p_ext, w))
    nrm = a * lax.rsqrt(jnp.sum(a * a, -1, keepdims=True) + RMS_EPS)
    scale = jnp.where(kind < GDN_HEADS, HEAD_DIM ** -0.5, 1.0).astype(F32)
    sel = (kind < 2 * GDN_HEADS).astype(F32)
    return (sel * (nrm * scale) + (1.0 - sel) * a,)


def _qkv_conv(name, p, w):
    t = p.shape[0]
    nb = w.shape[1] // LANES
    ident = lambda i: i
    return _seq_stage(name, _qkv_tile, HALO_SHORT, nb, [(p, _chan(t, ident))], [(w, _wblk(GDN_CONV))],
                      [(_sds((t, nb * LANES)), _chan(t, ident))])[0]


def _qkv_conv_bwd(name, dqkv0, dqkv1, p, w):
    t = p.shape[0]
    nb = w.shape[1] // LANES
    ident = lambda i: i
    return _seq_stage_bwd(name, _qkv_tile, HALO_SHORT, nb, [[(dqkv0, _chan(t, ident)), (dqkv1, _chan(t, ident))]],
                          [(p, _chan(t, ident))],
                          [(w, _wblk(GDN_CONV))], [(_sds((t, nb * LANES), MXU_DTYPE), _chan(t, ident))],
                          [(_sds(w.shape), _wblk(GDN_CONV))])


def _gates_fn(s, avec, dtvec):
    lane = lax.broadcasted_iota(jnp.int32, s.shape, 1)
    beta = jax.nn.sigmoid(s)
    g = -jnp.exp(avec) * jax.nn.softplus(s + dtvec)
    nh = 2 * GDN_HEADS
    return jnp.where(lane < nh, beta, jnp.where(lane < 2 * nh, g, 0.0))


def _gates(name, p, avec, dtvec, col):
    t = p.shape[0]
    tm = min(512, t)
    return _blocked(name, _gates_fn, (t // tm,), [p, avec, dtvec], [_tok(tm, LANES, col), _res(avec.shape), _res(dtvec.shape)],
                    [_sds((t, LANES))], [_tok(tm, LANES)])[0]


def _gates_bwd(name, dgb, p, avec, dtvec, col):
    t = p.shape[0]
    tm = min(512, t)

    def fn(dgb, s, avec, dtvec):
        _, vjp = jax.vjp(_gates_fn, s, avec, dtvec)
        return vjp(dgb)

    return _blocked(name, fn, (t // tm,), [dgb, p, avec, dtvec],
                    [_tok(tm, LANES), _tok(tm, LANES, col), _res(avec.shape), _res(dtvec.shape)],
                    [_sds((t, LANES), MXU_DTYPE), _sds(avec.shape), _sds(dtvec.shape)],
                    [_tok(tm, LANES), _res(avec.shape), _res(dtvec.shape)], acc=(1, 2), acc_axis=0)


def _make_pool_tile(total):
    def tile(r0, x_ext, scale, *ws):
        ys = []
        for g, win in enumerate(POOL_WINDOWS):
            xg = x_ext[:, g * LANES:(g + 1) * LANES]
            rows = xg.shape[0] - 2 * HALO_SHORT
            pos = r0 + lax.broadcasted_iota(jnp.int32, (rows, LANES), 0)
            lo = jnp.clip(pos - win // 2, 0, total)
            hi = jnp.clip(pos - win // 2 + win, 0, total)
            window_sum = _pool_sums[win](xg, jnp.ones((win, LANES), F32))
            pooled = window_sum / (hi - lo).astype(F32) - _center(xg, HALO_SHORT)
            ys.append(_mm(pooled, ws[g]))
        return (jnp.concatenate(ys, axis=-1) * scale,)

    return tile


def _pool_specs(p, pool_w, pool_scale, col0):
    t = p.shape[0]
    n = len(POOL_WINDOWS) * LANES
    seq = (p, pl.BlockSpec((t, n), lambda i: (0, col0)))
    pars = [(pool_scale, _res(pool_scale.shape))]
    pars += [(pool_w, pl.BlockSpec((None, LANES, LANES), lambda i, g=g: (g, 0, 0))) for g in range(len(POOL_WINDOWS))]
    return t, n, seq, pars


def _pool(p, pool_w, pool_scale, col0):
    t, n, seq, pars = _pool_specs(p, pool_w, pool_scale, col0)
    return _seq_stage("pool", _make_pool_tile(t), HALO_SHORT, 1, [seq], pars, [(_sds((t, n)), _res((t, n)))])[0]


def _pool_bwd(dy, p, pool_w, pool_scale, col0):
    t, n, seq, pars = _pool_specs(p, pool_w, pool_scale, col0)
    wspec = (_sds((LANES, LANES)), _res((LANES, LANES)))
    return _seq_stage_bwd("pool_bwd", _make_pool_tile(t), HALO_SHORT, 1, [(dy, _res(dy.shape))], [seq], pars,
                          [(_sds((t, n), MXU_DTYPE), _res((t, n)))],
                          [(_sds(pool_scale.shape), _res(pool_scale.shape))] + [wspec] * len(POOL_WINDOWS))


def _ffn_conv_tile(r0, a_ext, gt_ext, w):
    return (_silu(_conv3x3(a_ext, w)) * _center(gt_ext, HALO_GRID),)


def _ffn_conv(name, h, w9):
    t = h.shape[0]
    nb = w9.shape[1] // LANES
    ident = lambda i: i
    return _seq_stage(name, _ffn_conv_tile, HALO_GRID, nb, [(h, _chan(t, ident)), (h, _chan(t, lambda i: nb + i))],
                      [(w9, _wblk(9))], [(_sds((t, nb * LANES), MXU_DTYPE), _chan(t, ident))], GRID_SEQ_TILE)[0]


def _ffn_conv_bwd(name, ds, h, w9):
    t = h.shape[0]
    nb = w9.shape[1] // LANES
    ident = lambda i: i
    o = (_sds((t, nb * LANES), MXU_DTYPE), _chan(t, ident))
    return _seq_stage_bwd(name, _ffn_conv_tile, HALO_GRID, nb, [(ds, _chan(t, ident))],
                          [(h, _chan(t, ident)), (h, _chan(t, lambda i: nb + i))], [(w9, _wblk(9))],
                          [o, o], [(_sds(w9.shape), _wblk(9))], GRID_SEQ_TILE)


def _sconv_tile(r0, gb_ext, gc_ext, h_ext, w):
    return (_center(gb_ext, HALO_SHORT) * _conv3(gc_ext * h_ext, w),)


def _sconv_specs(p1, w3):
    t = p1.shape[0]
    nb = w3.shape[1] // LANES
    seqs = [(p1, _chan(t, lambda i: i)), (p1, _chan(t, lambda i: nb + i)), (p1, _chan(t, lambda i: 2 * nb + i))]
    return t, nb, seqs, [(w3, _wblk(SC_WIDTH))]


def _sconv(p1, w3):
    t, nb, seqs, pars = _sconv_specs(p1, w3)
    return _seq_stage("sconv", _sconv_tile, HALO_SHORT, nb, seqs, pars, [(_sds((t, nb * LANES)), _chan(t, lambda i: i))])[0]


def _sconv_bwd(dysc, p1, w3):
    t, nb, seqs, pars = _sconv_specs(p1, w3)
    o = (_sds((t, nb * LANES), MXU_DTYPE), _chan(t, lambda i: i))
    return _seq_stage_bwd("sconv_bwd", _sconv_tile, HALO_SHORT, nb, [(dysc, _chan(t, lambda i: i))], seqs, pars,
                          [o, o, o], [(_sds(w3.shape), _wblk(SC_WIDTH))])


def _conf_tile(r0, ga_ext, gb_ext, w):
    return (_conv31(ga_ext * jax.nn.sigmoid(gb_ext), w),)


def _conf_specs(p1, w31, blk0):
    t = p1.shape[0]
    nb = w31.shape[1] // LANES
    seqs = [(p1, _chan(t, lambda i: blk0 + i)), (p1, _chan(t, lambda i: blk0 + nb + i))]
    return t, nb, seqs, [(w31, _wblk(CF_WIDTH))]


def _confconv(p1, w31, blk0):
    t, nb, seqs, pars = _conf_specs(p1, w31, blk0)
    return _seq_stage("confconv", _conf_tile, HALO_CONF, nb, seqs, pars, [(_sds((t, nb * LANES)), _chan(t, lambda i: i))])[0]


def _confconv_bwd(dz0, p1, w31, blk0):
    t, nb, seqs, pars = _conf_specs(p1, w31, blk0)
    o = (_sds((t, nb * LANES), MXU_DTYPE), _chan(t, lambda i: i))
    return _seq_stage_bwd("confconv_bwd", _conf_tile, HALO_CONF, nb, [(dz0, _chan(t, lambda i: i))], seqs, pars,
                          [o, o], [(_sds(w31.shape), _wblk(CF_WIDTH))])


def _bdot(a, b, ca, cb, hi):
    if hi:
        return lax.dot_general(a, b, (((ca,), (cb,)), ((0,), (0,))), preferred_element_type=F32, precision=HIGHEST)
    return lax.dot_general(a.astype(MXU_DTYPE), b.astype(MXU_DTYPE), (((ca,), (cb,)), ((0,), (0,))), preferred_element_type=F32)


def _make_bmm(hi):
    nn_i = lambda a, b: _bdot(a, b, 2, 1, hi)
    nt_i = lambda a, b: _bdot(a, b, 2, 2, hi)
    tn_i = lambda a, b: _bdot(a, b, 1, 1, hi)
    nn = _mm_vjp(nn_i, lambda g, b: nt_i(g, b), lambda a, g: tn_i(a, g))
    nt = _mm_vjp(nt_i, lambda g, b: nn_i(g, b), lambda a, g: tn_i(g, a))
    tn = _mm_vjp(tn_i, lambda g, b: nt_i(b, g), lambda a, g: nn_i(a, g))
    return nn, nt, tn


_bmm, _bmm_nt, _bmm_tn = _make_bmm(False)
_bmm_hi, _bmm_hi_nt, _bmm_hi_tn = _make_bmm(True)


def _unit_tri_inverse(a):
    c = a.shape[-1]
    eye = (lax.broadcasted_iota(jnp.int32, a.shape, 1) == lax.broadcasted_iota(jnp.int32, a.shape, 2)).astype(F32)
    levels = max(1, int(math.ceil(math.log2(c))) - 1)
    p = eye - a
    m = _bdot(a, a, 2, 1, True)
    for level in range(levels):
        p_next = p + _bdot(p, m, 2, 1, True)
        if level + 1 < levels:
            m = _bdot(m, m, 2, 1, True)
        p = p_next
    return p


@jax.custom_vjp
def _known_inverse(a, tinv):
    return tinv


def _known_inverse_fwd(a, tinv):
    return tinv, tinv


def _known_inverse_bwd(tinv, dt):
    da = -_bdot(_bdot(tinv, dt, 1, 1, True), tinv, 2, 2, True)
    return da, jnp.zeros_like(tinv)


_known_inverse.defvjp(_known_inverse_fwd, _known_inverse_bwd)


def _gdn_chunk(q, k, v, gcol, grow, bcol, s, tinv=None):
    n, c, _ = q.shape
    shape = (n, c, c)
    fwd_dir = lax.broadcasted_iota(jnp.int32, shape, 0) < n // 2
    i = lax.broadcasted_iota(jnp.int32, shape, 1)
    j = lax.broadcasted_iota(jnp.int32, shape, 2)
    order = jnp.where(fwd_dir, i - j, j - i)
    incl = order >= 0
    gc_col = jnp.sum(jnp.where(incl, grow, 0.0), axis=2, keepdims=True)
    gc_row = jnp.sum(jnp.where(order <= 0, gcol, 0.0), axis=1, keepdims=True)
    gtot = jnp.sum(grow, axis=2, keepdims=True)
    decay = jnp.exp(jnp.where(incl, gc_col - gc_row, -1e30))
    kb = k * bcol
    a = jnp.where(order > 0, _bmm_nt(kb, k) * decay, 0.0)
    tinv = _unit_tri_inverse(a) if tinv is None else _known_inverse(a, tinv)
    e_col = jnp.exp(gc_col)
    u = _bmm_hi(tinv, v * bcol)
    w = _bmm_hi(tinv, kb * e_col)
    k_dec = k * jnp.exp(gtot - gc_col)
    v_new = u - _bmm(w, s)
    attn = jnp.where(incl, _bmm_nt(q, k) * decay, 0.0)
    o = _bmm(q * e_col, s) + _bmm(attn, v_new)
    s_next = s * jnp.exp(gtot) + _bmm_tn(k_dec, v_new)
    return o, s_next, tinv


N_CHAINS = 2 * GDN_HEADS
QKV_W = 3 * GDN_HEADS * HEAD_DIM


def _scan_specs(nc, step):
    c = GDN_CHUNK
    seq = [pl.BlockSpec((c, QKV_W), lambda i: (step(i), 0)), pl.BlockSpec((c, QKV_W), lambda i: (nc - 1 - step(i), 0))]
    heads = [pl.BlockSpec((c, GDN_HEADS * HEAD_DIM), lambda i: (step(i), 0)),
             pl.BlockSpec((c, GDN_HEADS * HEAD_DIM), lambda i: (nc - 1 - step(i), 0))]
    colv = pl.BlockSpec((N_CHAINS, None, c, 1), lambda i: (0, step(i), 0, 0))
    rowv = pl.BlockSpec((N_CHAINS, None, 1, c), lambda i: (0, step(i), 0, 0))
    state = pl.BlockSpec((N_CHAINS, HEAD_DIM, HEAD_DIM), lambda i: (0, 0, 0))
    saved = pl.BlockSpec((N_CHAINS, None, HEAD_DIM, HEAD_DIM), lambda i: (0, step(i), 0, 0))
    inv = pl.BlockSpec((N_CHAINS, None, c, c), lambda i: (0, step(i), 0, 0))
    return seq, heads, colv, rowv, state, saved, inv


def _head_cols(h, part):
    lo = (part * GDN_HEADS + h) * HEAD_DIM
    return slice(lo, lo + HEAD_DIM)


def _chain_stack(x0_ref, x1_ref, part):
    return jnp.stack([(x0_ref if n < GDN_HEADS else x1_ref)[:, _head_cols(n % GDN_HEADS, part)] for n in range(N_CHAINS)])


def _scan_fwd(name, qkv, bcol, gcol, grow, s0):
    t = qkv.shape[0]
    nc = t // GDN_CHUNK
    seq, heads, colv, rowv, state, saved, inv = _scan_specs(nc, lambda i: i)

    def body(qkv0_ref, qkv1_ref, b_ref, gc_ref, gr_ref, s0_ref, o0_ref, o1_ref, save_ref, tinv_ref, fin_ref, s_ref):
        @pl.when(pl.program_id(0) == 0)
        def _():
            s_ref[...] = s0_ref[...]

        s = s_ref[...]
        save_ref[...] = s
        q, k, v = [_chain_stack(qkv0_ref, qkv1_ref, part) for part in range(3)]
        o, s_next, tinv = _gdn_chunk(q, k, v, gc_ref[...], gr_ref[...], b_ref[...], s)
        for n in range(N_CHAINS):
            d, h = divmod(n, GDN_HEADS)
            (o0_ref if d == 0 else o1_ref)[:, _head_cols(h, 0)] = o[n]
        tinv_ref[...] = tinv
        s_ref[...] = s_next
        fin_ref[...] = s_next

    hw = GDN_HEADS * HEAD_DIM
    return pl.pallas_call(
        body, name=name, grid=(nc,), in_specs=seq + [colv, colv, rowv, state],
        out_specs=heads + [saved, inv, state],
        out_shape=[_sds((t, hw)), _sds((t, hw)), _sds((N_CHAINS, nc, HEAD_DIM, HEAD_DIM)),
                   _sds((N_CHAINS, nc, GDN_CHUNK, GDN_CHUNK)), _sds((N_CHAINS, HEAD_DIM, HEAD_DIM))],
        scratch_shapes=[pltpu.VMEM((N_CHAINS, HEAD_DIM, HEAD_DIM), F32)],
        compiler_params=pltpu.CompilerParams(dimension_semantics=("arbitrary",), vmem_limit_bytes=VMEM_LIMIT_BYTES),
    )(qkv, qkv, bcol, gcol, grow, s0)


def _scan_bwd(name, do, qkv, bcol, gcol, grow, saved_s, saved_inv, ds_last):
    t = qkv.shape[0]
    nc = t // GDN_CHUNK
    c = GDN_CHUNK
    seq, heads, colv, rowv, state, saved, inv = _scan_specs(nc, lambda i: nc - 1 - i)

    def body(do0_ref, do1_ref, qkv0_ref, qkv1_ref, b_ref, gc_ref, gr_ref, s_ref, tinv_ref, dsl_ref,
             dx0_ref, dx1_ref, db_ref, dgc_ref, dgr_ref, ds0_ref, ds_ref):
        @pl.when(pl.program_id(0) == 0)
        def _():
            ds_ref[...] = dsl_ref[...]

        tinv = tinv_ref[...]
        q, k, v = [_chain_stack(qkv0_ref, qkv1_ref, part) for part in range(3)]
        do = jnp.stack([(do0_ref if n < GDN_HEADS else do1_ref)[:, _head_cols(n % GDN_HEADS, 0)] for n in range(N_CHAINS)])
        _, vjp = jax.vjp(lambda *a: _gdn_chunk(*a, tinv=tinv)[:2], q, k, v, gc_ref[...], gr_ref[...], b_ref[...], s_ref[...])
        dq, dk, dv, dgc, dgr, db, ds = vjp((do, ds_ref[...]))
        for n in range(N_CHAINS):
            d, h = divmod(n, GDN_HEADS)
            dx_ref = dx0_ref if d == 0 else dx1_ref
            dx_ref[:, _head_cols(h, 0)] = dq[n]
            dx_ref[:, _head_cols(h, 1)] = dk[n]
            dx_ref[:, _head_cols(h, 2)] = dv[n]
        db_ref[...] = db
        dgc_ref[...] = dgc
        dgr_ref[...] = dgr
        ds_ref[...] = ds
        ds0_ref[...] = ds

    vec_c = _sds((N_CHAINS, nc, c, 1))
    vec_r = _sds((N_CHAINS, nc, 1, c))
    return pl.pallas_call(
        body, name=name, grid=(nc,),
        in_specs=heads + seq + [colv, colv, rowv, saved, inv, state],
        out_specs=seq + [colv, colv, rowv, state],
        out_shape=[_sds((t, QKV_W)), _sds((t, QKV_W)), vec_c, vec_c, vec_r, _sds((N_CHAINS, HEAD_DIM, HEAD_DIM))],
        scratch_shapes=[pltpu.VMEM((N_CHAINS, HEAD_DIM, HEAD_DIM), F32)],
        compiler_params=pltpu.CompilerParams(dimension_semantics=("arbitrary",), vmem_limit_bytes=VMEM_LIMIT_BYTES),
    )(do, do, qkv, qkv, bcol, gcol, grow, saved_s, saved_inv, ds_last)


def _gate_layouts(gb):
    t = gb.shape[0]
    nc = t // GDN_CHUNK
    nh = GDN_HEADS

    def by_scan_position(a):
        a = a.T.reshape(2 * nh, nc, GDN_CHUNK)
        return jnp.concatenate([a[:nh], a[nh:, ::-1]], axis=0)

    beta = by_scan_position(gb[:, :2 * nh])
    g = by_scan_position(gb[:, 2 * nh:4 * nh])
    return beta[..., None], g[..., None], g[:, :, None, :]


def _gate_layouts_bwd(dbcol, dgcol, dgrow):
    n2, nc, c, _ = dbcol.shape
    nh = n2 // 2
    t = nc * c

    def by_token(a):
        return jnp.concatenate([a[:nh], a[nh:, ::-1]], axis=0).reshape(n2, t).T

    dbeta = by_token(dbcol[..., 0])
    dg = by_token(dgcol[..., 0] + dgrow[:, :, 0, :])
    return jnp.concatenate([dbeta, dg, jnp.zeros((t, LANES - 2 * n2), F32)], axis=1)


def _shard_axis(full_shape, shard_shape):
    return [i for i, (a, b) in enumerate(zip(full_shape, shard_shape)) if a != b][0]


def kernel(x, c, ctx, c_ctx, ada_w, ada_b, ln_g, ln_b, even_w_in, even_w_out, gdn_conv_w, gdn_a_log, gdn_dt_bias, gdn_norm_w, pool_w, pool_scale, odd_w_in, odd_w_out, sconv_w, conf_conv_w, conf_ln_g, conf_ln_b, ffn_w_up, ffn_conv_w, ffn_w_down, loss_target, m_c_ctx, m_ada_w, m_ada_b, m_ln_g, m_ln_b, m_even_w_in, m_even_w_out, m_gdn_conv_w, m_gdn_a_log, m_gdn_dt_bias, m_gdn_norm_w, m_pool_w, m_pool_scale, m_odd_w_in, m_odd_w_out, m_sconv_w, m_conf_conv_w, m_conf_ln_g, m_conf_ln_b, m_ffn_w_up, m_ffn_conv_w, m_ffn_w_down, v_c_ctx, v_ada_w, v_ada_b, v_ln_g, v_ln_b, v_even_w_in, v_even_w_out, v_gdn_conv_w, v_gdn_a_log, v_gdn_dt_bias, v_gdn_norm_w, v_pool_w, v_pool_scale, v_odd_w_in, v_odd_w_out, v_sconv_w, v_conf_conv_w, v_conf_ln_g, v_conf_ln_b, v_ffn_w_up, v_ffn_conv_w, v_ffn_w_down):
    names = ['c_ctx', 'ada_w', 'ada_b', 'ln_g', 'ln_b', 'even_w_in', 'even_w_out', 'gdn_conv_w', 'gdn_a_log',
             'gdn_dt_bias', 'gdn_norm_w', 'pool_w', 'pool_scale', 'odd_w_in', 'odd_w_out', 'sconv_w', 'conf_conv_w',
             'conf_ln_g', 'conf_ln_b', 'ffn_w_up', 'ffn_conv_w', 'ffn_w_down']
    loc = locals()
    wts = {n: loc[n] for n in names}
    mom = {n: loc['m_' + n] for n in names}
    var = {n: loc['v_' + n] for n in names}

    ix, iy, ic = lax.axis_index("x"), lax.axis_index("y"), lax.axis_index("c")
    chip = 2 * ix + iy
    dev = 2 * chip + ic
    d = D_MODEL
    x0, ctx0, tgt = x[0], ctx[0], loss_target[0]
    t, tc = x0.shape[0], ctx0.shape[0]
    depth = ada_w.shape[0]
    n_ada = ada_w.shape[2]

    small_sharded = [gdn_conv_w, sconv_w, conf_conv_w, ffn_conv_w, ln_g, ln_b]
    s1_items = [c] + small_sharded
    s1 = _pack_flat(s1_items, 8 * LANES, F32).reshape(-1, LANES)
    g1 = _all_gather8("gather_small_in", s1, True).reshape(8, -1)
    c_all = g1[:, :d]
    per_chip = [_unpack_flat(g1[2 * k], [a.shape for a in s1_items])[1:] for k in range(4)]
    gdn_conv_f, sconv_f, conf_conv_f, ffn_conv_f, ln_g_f, ln_b_f = [
        jnp.concatenate([per_chip[k][i] for k in range(4)], axis=-1) for i in range(len(small_sharded))]
    c16 = jnp.concatenate([c_all, c_ctx[None], jnp.zeros((7, d), F32)], axis=0)

    mod_part = _mod_rows(c16, ada_w)
    g2 = _all_gather8("gather_mod", mod_part.reshape(-1, LANES), True).reshape(4, 2, depth, 16, n_ada)[:, 0]
    mod_all = jnp.transpose(g2, (1, 2, 0, 3)).reshape(depth, 16, 4 * n_ada) + ada_b[:, None, :]
    mod_me = lax.dynamic_index_in_dim(mod_all, dev, axis=1, keepdims=False).reshape(depth, 6, 1, d)
    sh_c, sc_c = mod_all[0, 8, :d][None], mod_all[0, 8, d:2 * d][None]

    big_names = ['even_w_in', 'even_w_out', 'odd_w_in', 'odd_w_out', 'ffn_w_up', 'ffn_w_down']
    big_full_shapes = {'even_w_in': (d, 4 * even_w_in.shape[1]), 'even_w_out': (4 * even_w_out.shape[0], d),
                       'odd_w_in': (d, 4 * odd_w_in.shape[1]), 'odd_w_out': (4 * odd_w_out.shape[0], d),
                       'ffn_w_up': (depth, d, 4 * ffn_w_up.shape[2]), 'ffn_w_down': (depth, 4 * ffn_w_down.shape[1], d)}
    big_shard_shapes = [wts[n].shape for n in big_names]
    big_axes = [_shard_axis(big_full_shapes[n], wts[n].shape) for n in big_names]
    half_mult = 2 * 16 * LANES
    rh = -(-sum(math.prod(s) for s in big_shard_shapes) // half_mult) * half_mult // (2 * LANES)

    def my_half_pack(group):
        pack = _pack_flat([wts[n] for n in group], half_mult, MXU_DTYPE).reshape(2, -1, LANES)
        return lax.dynamic_index_in_dim(pack, ic, axis=0, keepdims=False)

    def unpack_full(flat4, group):
        per_chip = [_unpack_flat(flat4[k], [wts[n].shape for n in group]) for k in range(4)]
        return {n: jnp.concatenate([per_chip[k][i] for k in range(4)], axis=big_axes[big_names.index(n)])
                for i, n in enumerate(group)}

    early, late = big_names[:2], big_names[2:]
    wg_early = _all_gather8("gather_weights", my_half_pack(early), False).reshape(4, -1)
    late_sems, late_mine, late_land, late_token = _chip_gather_start("gather_late_start", my_half_pack(late), wg_early)
    full = unpack_full(wg_early, early)
    n_even = full['even_w_in'].shape[1]
    n_even_pad = -(-n_even // LANES) * LANES
    w_in = jnp.concatenate([full['even_w_in'], jnp.zeros((d, n_even_pad - n_even), MXU_DTYPE)], axis=1)
    w_out = full['even_w_out']
    scal_blk = (n_even // LANES)
    n_scal = n_even - scal_blk * LANES

    def mod(layer, k):
        return mod_me[layer, k]

    avec = jnp.zeros((1, LANES), F32).at[0, n_scal // 2:n_scal].set(gdn_a_log.reshape(-1))
    dtvec = jnp.zeros((1, LANES), F32).at[0, n_scal // 2:n_scal].set(gdn_dt_bias.reshape(-1))
    normw = gdn_norm_w[None]
    pscale = pool_scale[None]
    cg, cb = conf_ln_g[None], conf_ln_b[None]
    lng = lambda l, k: ln_g_f[l, k][None]
    lnb = lambda l, k: ln_b_f[l, k][None]
    convw9 = ffn_conv_f.reshape(depth, 9, -1)
    nqkv = gdn_conv_f.shape[1]

    shift0 = mod(0, 0) + late_token[:1, :1]
    p, ub0 = _inproj("even_in", x0, shift0, mod(0, 1), w_in, 512)
    pc, ucb = _inproj("even_in_ctx", ctx0, sh_c, sc_c, w_in, 256)
    qkv = _qkv_conv("qkv_conv", p, gdn_conv_f)
    qkv_c = _qkv_conv("qkv_conv_ctx", pc, gdn_conv_f)
    gb = _gates("gates", p, avec, dtvec, scal_blk)
    gb_c = _gates("gates_ctx", pc, avec, dtvec, scal_blk)
    lay = _gate_layouts(gb)
    lay_c = _gate_layouts(gb_c)
    s_zero = jnp.zeros((2 * GDN_HEADS, HEAD_DIM, HEAD_DIM), F32)
    _, _, save_c, inv_c, s_ctx = _scan_fwd("scan_ctx", qkv_c, *lay_c, s_zero)
    o0, o1, save_l, inv_l, _ = _scan_fwd("scan", qkv, *lay, s_ctx)
    pool_blk = (nqkv + GDN_HEADS * HEAD_DIM) // 512
    ypool = _pool(p, pool_w, pscale, pool_blk)
    x1, mix0, y0 = _even_out(o0, o1, p, ypool, x0, normw, mod(0, 2), lng(0, 0), lnb(0, 0), w_out, 256)

    late_mine, late_land = _chip_gather_wait("gather_late_wait", late_sems, late_mine, late_land, x1)
    full.update(unpack_full(_sibling_merge("gather_late_merge", late_mine, late_land).reshape(4, -1), late))
    w_oin, w_oout, w_up, w_down = full['odd_w_in'], full['odd_w_out'], full['ffn_w_up'], full['ffn_w_down']

    def ffn_fwd(l, xin):
        h, ub = _inproj(f"ffn_up{l}", xin, mod(l, 3), mod(l, 4), w_up[l], 256)
        s = _ffn_conv(f"ffn_conv{l}", h, convw9[l])
        xo, y = _ffn_down(f"ffn_down{l}", s, xin, mod(l, 5), lng(l, 1), lnb(l, 1), w_down[l], 256)
        return xo, (h, ub, s, y)

    x2, ffn0 = ffn_fwd(0, x1)
    p1, ub1 = _inproj("odd_in", x2, mod(1, 0), mod(1, 1), w_oin, 512)
    nsc = sconv_f.shape[1] // LANES
    ysc = _sconv(p1, sconv_f)
    z0 = _confconv(p1, conf_conv_f, 3 * nsc)
    x3, mix1, y1 = _odd_out(ysc, z0, x2, cg, cb, mod(1, 2), lng(1, 0), lnb(1, 0), w_oout, 256)
    x4, ffn1 = ffn_fwd(1, x3)
    loss_part, dx4 = _loss_and_grad(x4, tgt, 512)
    loss = lax.psum(loss_part[0, 0], ("x", "y", "c"))

    dmod = [[None] * 6 for _ in range(depth)]
    dlng = [[None, None] for _ in range(depth)]
    dlnb = [[None, None] for _ in range(depth)]
    gbig = {}
    dconv9 = [None] * depth

    def ffn_bwd(l, dxo, xin, saved):
        h, ub, s, y = saved
        ds, dxa, dyb, dgt, dg_, db_ = _ffn_down_bwd(f"ffn_down_bwd{l}", dxo, xin, y, mod(l, 5), lng(l, 1), lnb(l, 1), w_down[l], 256)
        da, dgate, dw9 = _ffn_conv_bwd(f"ffn_conv_bwd{l}", ds, h, convw9[l])
        dh = jnp.concatenate([da, dgate], axis=1)
        dxin, dsh, dsc = _inproj_bwd(f"ffn_up_bwd{l}", dh, xin, mod(l, 3), mod(l, 4), w_up[l], dxa, 256)
        dmod[l][3], dmod[l][4], dmod[l][5] = dsh, dsc, dgt
        dlng[l][1], dlnb[l][1] = dg_, db_
        dconv9[l] = dw9
        return dxin, _matmul_tn(f"dw_up{l}", [(ub, dh)]), _matmul_tn(f"dw_down{l}", [(s, dyb)])

    dx3, dwu1, dwd1 = ffn_bwd(1, dx4, x3, ffn1)
    dysc, dz0, dx2a, dyb1, dcg, dcb, dgt, dg_, db_ = _odd_out_bwd(dx3, ysc, z0, x2, y1, cg, cb, mod(1, 2), lng(1, 0), lnb(1, 0), w_oout, 256)
    dmod[1][2], dlng[1][0], dlnb[1][0] = dgt, dg_, db_
    d_gb, d_gc, d_h, dsconv = _sconv_bwd(dysc, p1, sconv_f)
    d_ga, d_gbb, dconf = _confconv_bwd(dz0, p1, conf_conv_f, 3 * nsc)
    dp1 = jnp.concatenate([d_gb, d_gc, d_h, d_ga, d_gbb], axis=1)
    dx2, dsh, dsc = _inproj_bwd("odd_in_bwd", dp1, x2, mod(1, 0), mod(1, 1), w_oin, dx2a, 512)
    dmod[1][0], dmod[1][1] = dsh, dsc
    gbig['odd_w_out'] = [_matmul_tn("dw_oout", [(mix1, dyb1)])]
    gbig['odd_w_in'] = [_matmul_tn("dw_oin", [(ub1, dp1)])]

    dx1, dwu0, dwd0 = ffn_bwd(0, dx2, x1, ffn0)
    gbig['ffn_w_up'] = [dwu0, dwu1]
    gbig['ffn_w_down'] = [dwd0, dwd1]

    do, dpg, dypool, dx0a, dyb0, dnormw, dgt, dg_, db_ = _even_out_bwd(dx1, o0, o1, p, ypool, x0, y0, normw, mod(0, 2), lng(0, 0), lnb(0, 0), w_out, 256)
    dmod[0][2], dlng[0][0], dlnb[0][0] = dgt, dg_, db_
    pool_cts = _pool_bwd(dypool, p, pool_w, pscale, pool_blk)
    dpp, dpool_scale, dpool_w = pool_cts[0], pool_cts[1], jnp.stack(pool_cts[2:])
    dqkv0, dqkv1, dbcol, dgcol, dgrow, ds0 = _scan_bwd("scan_bwd", do, qkv, *lay, save_l, inv_l, s_zero)
    zero_do = jnp.zeros((tc, GDN_HEADS * HEAD_DIM), F32)
    dqkv0_c, dqkv1_c, dbcol_c, dgcol_c, dgrow_c, _ = _scan_bwd("scan_bwd_ctx", zero_do, qkv_c, *lay_c, save_c, inv_c, ds0)
    dgb = _gate_layouts_bwd(dbcol, dgcol, dgrow)
    dgb_c = _gate_layouts_bwd(dbcol_c, dgcol_c, dgrow_c)
    dps, davec, ddtvec = _gates_bwd("gates_bwd", dgb, p, avec, dtvec, scal_blk)
    dps_c, davec_c, ddtvec_c = _gates_bwd("gates_bwd_ctx", dgb_c, pc, avec, dtvec, scal_blk)
    dpqkv, dconv5 = _qkv_conv_bwd("qkv_conv_bwd", dqkv0, dqkv1, p, gdn_conv_f)
    dpqkv_c, dconv5_c = _qkv_conv_bwd("qkv_conv_bwd_ctx", dqkv0_c, dqkv1_c, pc, gdn_conv_f)
    dp = jnp.concatenate([dpqkv, dpg, dpp, dps], axis=1)
    dpc = jnp.concatenate([dpqkv_c, jnp.zeros((tc, n_even_pad - nqkv - LANES), MXU_DTYPE), dps_c], axis=1)
    grad_x, dsh, dsc = _inproj_bwd("even_in_bwd", dp, x0, mod(0, 0), mod(0, 1), w_in, dx0a, 512)
    dmod[0][0], dmod[0][1] = dsh, dsc
    _, dsh_c, dsc_c = _inproj_bwd("even_in_bwd_ctx", dpc, ctx0, sh_c, sc_c, w_in, None, 256)
    gbig['even_w_in'] = [_matmul_tn("dw_in", [(ub0, dp), (ucb, dpc)])[:, :n_even]]
    gbig['even_w_out'] = [_matmul_tn("dw_out", [(mix0, dyb0)])]

    def shards_of(name, i, k):
        n = big_shard_shapes[i][big_axes[i]]
        axis = big_axes[i] - (wts[name].ndim - 2)
        return [lax.slice_in_dim(g, k * n, (k + 1) * n, axis=axis) for g in gbig[name]]

    gpack = jnp.stack([
        _pack_flat([g for i, n in enumerate(big_names) for g in shards_of(n, i, k)], half_mult, MXU_DTYPE).reshape(2, rh, LANES)
        for k in range(4)], axis=1)
    keep = lax.dynamic_index_in_dim(gpack, ic, axis=0, keepdims=False)
    give = lax.dynamic_index_in_dim(gpack, 1 - ic, axis=0, keepdims=False)
    got = _sibling_exchange("grad_pair_exchange", give)
    pair = _sum_pair("grad_pair_sum", keep.reshape(-1, LANES), got.reshape(-1, LANES), MXU_DTYPE)
    parts = _chip_scatter("grad_chip_scatter", pair.reshape(4, rh, LANES))
    gsum = _sum_axis1("grad_chip_sum", parts).reshape(-1)
    g_shards = dict(zip(big_names, _unpack_flat(gsum, big_shard_shapes)))

    dmod_rows = jnp.stack([jnp.concatenate(dmod[l], axis=1)[0] for l in range(depth)])
    dmod_c = jnp.concatenate([dsh_c[0], dsc_c[0], jnp.zeros((4 * d,), F32)])
    dmod_c_rows = jnp.stack([dmod_c] + [jnp.zeros_like(dmod_c)] * (depth - 1))
    small_g = {
        'ln_g': jnp.stack([jnp.stack([dlng[l][k][0] for k in range(2)]) for l in range(depth)]),
        'ln_b': jnp.stack([jnp.stack([dlnb[l][k][0] for k in range(2)]) for l in range(depth)]),
        'gdn_conv_w': dconv5 + dconv5_c,
        'gdn_a_log': (davec + davec_c)[0, n_scal // 2:n_scal].reshape(gdn_a_log.shape),
        'gdn_dt_bias': (ddtvec + ddtvec_c)[0, n_scal // 2:n_scal].reshape(gdn_dt_bias.shape),
        'gdn_norm_w': dnormw[0], 'pool_w': dpool_w, 'pool_scale': dpool_scale[0],
        'sconv_w': dsconv, 'conf_conv_w': dconf, 'conf_ln_g': dcg[0], 'conf_ln_b': dcb[0],
        'ffn_conv_w': jnp.stack(dconv9).reshape(depth, 3, 3, -1),
    }
    small_names = list(small_g)
    s3_items = [dmod_rows, dmod_c_rows] + [small_g[n] for n in small_names]
    s3 = _pack_flat(s3_items, 8 * LANES, F32).reshape(-1, LANES)
    g3 = _all_gather8("gather_small_grads", s3, True).reshape(8, -1, LANES)
    tot3 = _sum_axis1("small_grad_sum", g3[None]).reshape(-1)
    tot_items = _unpack_flat(tot3, [a.shape for a in s3_items])
    dmod_sum, dmod_c_sum = tot_items[0], tot_items[1]
    small_tot = dict(zip(small_names, tot_items[2:]))
    grad_ada_b = dmod_sum + dmod_c_sum
    g3f = g3.reshape(8, -1)
    rows_all = g3f[:, :depth * 6 * d].reshape(8, depth, 6 * d)
    cols = lax.dynamic_slice_in_dim(rows_all, chip * n_ada, n_ada, axis=2)
    crow = lax.dynamic_slice_in_dim(dmod_c_sum, chip * n_ada, n_ada, axis=1)
    dmod16 = jnp.concatenate([jnp.transpose(cols, (1, 0, 2)), crow[:, None, :], jnp.zeros((depth, 7, n_ada), F32)], axis=1)
    grad_ada_w, dsil = _ada_grads(c16, dmod16, ada_w)
    s4 = jnp.concatenate([dsil[0, 8][None], jnp.zeros((7, d), F32)], axis=0).reshape(-1, LANES)
    g4 = _all_gather8("gather_cctx", s4, True).reshape(8, 8, d)
    grad_c_ctx = _cctx_grad(g4[0::2, 0][:, None, :], c_ctx[None])[0]

    def my_cols(a, n):
        return lax.dynamic_slice_in_dim(a, chip * n, n, axis=a.ndim - 1)

    grads = dict(g_shards)
    grads['c_ctx'] = grad_c_ctx
    grads['ada_w'] = grad_ada_w
    grads['ada_b'] = grad_ada_b
    for n in ['ln_g', 'ln_b', 'gdn_conv_w', 'sconv_w', 'conf_conv_w', 'ffn_conv_w']:
        grads[n] = my_cols(small_tot[n], wts[n].shape[-1])
    for n in ['gdn_a_log', 'gdn_dt_bias', 'gdn_norm_w', 'pool_w', 'pool_scale', 'conf_ln_g', 'conf_ln_b']:
        grads[n] = small_tot[n]

    delta, new_m, new_v = {}, {}, {}
    big_adam = big_names + ['ada_w']
    for n in big_adam:
        shp = wts[n].shape
        as2d = lambda a: a.reshape(-1, shp[-1])
        dl, nm, nv = _adamw("adamw_" + n, as2d(wts[n]), as2d(grads[n]), as2d(mom[n]), as2d(var[n]))
        delta[n], new_m[n], new_v[n] = dl.reshape(shp), nm.reshape(shp), nv.reshape(shp)
    small_adam = [n for n in names if n not in big_adam]
    packs = [_pack_flat([src[n] for n in small_adam], 8 * LANES, F32).reshape(-1, LANES) for src in (wts, grads, mom, var)]
    outs = _adamw("adamw_small", *packs)
    shapes = [wts[n].shape for n in small_adam]
    for res, dst in zip(outs, (delta, new_m, new_v)):
        for n, a in zip(small_adam, _unpack_flat(res.reshape(-1), shapes)):
            dst[n] = a

    return (loss, grad_x[None], *[grads[n] for n in names], *[delta[n] for n in names],
            *[new_m[n] for n in names], *[new_v[n] for n in names])
```

```python
import functools
import math

import jax
import jax.numpy as jnp
from jax import lax
from jax.experimental import pallas as pl
from jax.experimental.pallas import tpu as pltpu

F32 = jnp.float32
MXU_DTYPE = jnp.bfloat16
HIGHEST = lax.Precision.HIGHEST
MESH = pl.DeviceIdType.MESH

D_MODEL = 1024
GRID_W = 64
GDN_HEADS = 4
HEAD_DIM = 128
GDN_CHUNK = 64
POOL_WINDOWS = (2, 4, 8, 16)
GDN_CONV = 5
SC_WIDTH = 3
CF_WIDTH = 31
ALPHA = 4.0 ** 0.25
LN_EPS = 1e-5
RMS_EPS = 1e-6
LANES = 128
VMEM_LIMIT_BYTES = 58 * 1024 * 1024

ADAM_LR, ADAM_B1, ADAM_B2, ADAM_EPS, ADAM_WD, ADAM_STEP = 0.001, 0.9, 0.999, 1e-08, 0.01, 10


def _blocked(name, fn, grid, ins, in_specs, out_shapes, out_specs, acc=(), acc_axis=None, scratch=()):
    n_in = len(ins)
    n_out = len(out_shapes)

    def body(*refs):
        vals = [r[...] for r in refs[:n_in]]
        res = fn(*vals, *refs[n_in + n_out:])
        if not isinstance(res, (tuple, list)):
            res = (res,)
        for k, (r, v) in enumerate(zip(refs[n_in:n_in + n_out], res)):
            if k in acc:
                first = pl.program_id(acc_axis) == 0

                @pl.when(first)
                def _(r=r, v=v):
                    r[...] = v.astype(r.dtype)

                @pl.when(jnp.logical_not(first))
                def _(r=r, v=v):
                    r[...] += v.astype(r.dtype)
            else:
                r[...] = v.astype(r.dtype)

    return pl.pallas_call(
        body, name=name, grid=grid, in_specs=in_specs, out_specs=out_specs, out_shape=out_shapes,
        scratch_shapes=list(scratch),
        compiler_params=pltpu.CompilerParams(dimension_semantics=("arbitrary",) * len(grid),
                                             vmem_limit_bytes=VMEM_LIMIT_BYTES),
    )(*ins)


def _sds(shape, dtype=F32):
    return jax.ShapeDtypeStruct(tuple(shape), dtype)


def _tok(tm, n, col=0):
    return pl.BlockSpec((tm, n), lambda t: (t, col))


def _res(shape):
    nd = len(shape)
    return pl.BlockSpec(tuple(shape), lambda t: (0,) * nd)


def _silu(x):
    return x * jax.nn.sigmoid(x)


def _layernorm(r, g, b):
    mu = jnp.mean(r, -1, keepdims=True)
    d = r - mu
    var = jnp.mean(d * d, -1, keepdims=True)
    return d * lax.rsqrt(var + LN_EPS) * g + b


def _mm_nn_impl(a, b):
    return jnp.dot(a.astype(MXU_DTYPE), b.astype(MXU_DTYPE), preferred_element_type=F32)


def _mm_nt_impl(a, b):
    return lax.dot_general(a.astype(MXU_DTYPE), b.astype(MXU_DTYPE), (((1,), (1,)), ((), ())), preferred_element_type=F32)


def _mm_tn_impl(a, b):
    return lax.dot_general(a.astype(MXU_DTYPE), b.astype(MXU_DTYPE), (((0,), (0,)), ((), ())), preferred_element_type=F32)


def _mm_vjp(mm, mm_da, mm_db):
    f = jax.custom_vjp(mm)
    f.defvjp(lambda a, b: (mm(a, b), (a, b)), lambda res, g: (mm_da(g, res[1]), mm_db(res[0], g)))
    return f


_mm = _mm_vjp(_mm_nn_impl, lambda g, b: _mm_nt_impl(g, b), lambda a, g: _mm_tn_impl(a, g))
_mm_nt = _mm_vjp(_mm_nt_impl, lambda g, b: _mm_nn_impl(g, b), lambda a, g: _mm_tn_impl(g, a))
_mm_tn = _mm_vjp(_mm_tn_impl, lambda g, b: _mm_nt_impl(b, g), lambda a, g: _mm_nn_impl(a, g))


def _row(w, k):
    rows = lax.broadcasted_iota(jnp.int32, w.shape, 0)
    return jnp.sum(jnp.where(rows == k, w, 0.0), axis=0, keepdims=True)


def _col_mask(shape, dc):
    col = lax.broadcasted_iota(jnp.int32, shape, 0) & (GRID_W - 1)
    return (col + dc >= 0) & (col + dc < GRID_W)


def _center(x_ext, halo):
    return x_ext[halo:x_ext.shape[0] - halo]


def _make_dwconv(taps, halo):
    assert all(abs(s) <= halo for s, _ in taps)

    def shifted(x_ext, s):
        r = x_ext if s == 0 else pltpu.roll(x_ext, (-s) % x_ext.shape[0], 0)
        return _center(r, halo)

    @jax.custom_vjp
    def conv(x_ext, w):
        acc = None
        for k, (s, dc) in enumerate(taps):
            r = shifted(x_ext, s)
            if dc != 0:
                r = jnp.where(_col_mask(r.shape, dc), r, 0.0)
            term = r * _row(w, k)
            acc = term if acc is None else acc + term
        return acc

    def fwd(x_ext, w):
        return conv(x_ext, w), (x_ext, w)

    def bwd(res, dy):
        x_ext, w = res
        n = x_ext.shape[0]
        rows = lax.broadcasted_iota(jnp.int32, w.shape, 0)
        pad = jnp.zeros((halo, dy.shape[1]), F32)
        dx = None
        dw = jnp.zeros(w.shape, F32)
        for k, (s, dc) in enumerate(taps):
            dym = dy if dc == 0 else jnp.where(_col_mask(dy.shape, dc), dy, 0.0)
            dw = dw + jnp.where(rows == k, jnp.sum(dym * shifted(x_ext, s), axis=0, keepdims=True), 0.0)
            t = jnp.concatenate([pad, dym * _row(w, k), pad], axis=0)
            if s != 0:
                t = pltpu.roll(t, s % n, 0)
            dx = t if dx is None else dx + t
        return dx, dw

    conv.defvjp(fwd, bwd)
    return conv


def _taps_1d(width):
    return tuple((k - width // 2, 0) for k in range(width))


HALO_SHORT = 8
HALO_CONF = 16
HALO_GRID = 72
_conv5 = _make_dwconv(_taps_1d(GDN_CONV), HALO_SHORT)
_conv3 = _make_dwconv(_taps_1d(SC_WIDTH), HALO_SHORT)
_conv31 = _make_dwconv(_taps_1d(CF_WIDTH), HALO_CONF)
_conv3x3 = _make_dwconv(tuple((dr * GRID_W + dc, dc) for dr in (-1, 0, 1) for dc in (-1, 0, 1)), HALO_GRID)
_pool_sums = {w: _make_dwconv(tuple((s, 0) for s in range(-(w // 2), w - w // 2)), HALO_SHORT) for w in POOL_WINDOWS}


def _all_gather8(name, blk, in_vmem):
    m_per, n = blk.shape
    space = pltpu.VMEM if in_vmem else pl.ANY

    def body(x_ref, out_ref, send_sems, recv_sems, local_sem):
        x, y, c = lax.axis_index("x"), lax.axis_index("y"), lax.axis_index("c")
        me, sibling = (x, y, c), (x, y, 1 - c)
        chips = [(1 - x, y), (x, 1 - y), (1 - x, 1 - y)]

        def rows(px, py, pc):
            return out_ref.at[pl.ds((4 * px + 2 * py + pc) * m_per, m_per), :]

        def copy(k, block, to, src=None):
            return pltpu.make_async_remote_copy(
                src_ref=rows(*block) if src is None else src, dst_ref=rows(*block),
                send_sem=send_sems.at[k], recv_sem=recv_sems.at[k], device_id=to, device_id_type=MESH)

        mine = pltpu.make_async_copy(x_ref, rows(*me), local_sem)
        mine.start()
        first = [copy(0, me, sibling, src=x_ref)]
        first += [copy(1 + j, me, (*chip, c), src=x_ref) for j, chip in enumerate(chips)]
        for cp in first:
            cp.start()
        passed = [copy(4 + j, (*chip, c), sibling) for j, chip in enumerate(chips)]
        for j, chip in enumerate(chips):
            copy(1 + j, (*chip, c), me).wait_recv()
            passed[j].start()
        copy(0, sibling, me).wait_recv()
        for j, chip in enumerate(chips):
            copy(4 + j, (*chip, 1 - c), me).wait_recv()
        for cp in first + passed:
            cp.wait_send()
        mine.wait()

    return pl.pallas_call(
        body, name=name, out_shape=_sds((8 * m_per, n), blk.dtype),
        in_specs=[pl.BlockSpec(memory_space=space)], out_specs=pl.BlockSpec(memory_space=space),
        scratch_shapes=[pltpu.SemaphoreType.DMA((7,)), pltpu.SemaphoreType.DMA((7,)), pltpu.SemaphoreType.DMA],
        compiler_params=pltpu.CompilerParams(vmem_limit_bytes=VMEM_LIMIT_BYTES),
    )(blk)


_ANY = pl.BlockSpec(memory_space=pl.ANY)


def _all_gather8_multi(name, blocks):
    n = len(blocks)

    def body(*refs):
        x_refs, out_refs = refs[:n], refs[n:2 * n]
        send_sems, recv_sems, local_sems = refs[2 * n:]
        x, y, c = lax.axis_index("x"), lax.axis_index("y"), lax.axis_index("c")
        me, sibling = (x, y, c), (x, y, 1 - c)
        chips = [(1 - x, y), (x, 1 - y), (1 - x, 1 - y)]

        def slot(i, px, py, pc):
            return out_refs[i].at[4 * px + 2 * py + pc]

        def copy(i, k, block, to, src=None):
            return pltpu.make_async_remote_copy(
                src_ref=slot(i, *block) if src is None else src, dst_ref=slot(i, *block),
                send_sem=send_sems.at[i, k], recv_sem=recv_sems.at[i, k], device_id=to, device_id_type=MESH)

        mine = [pltpu.make_async_copy(x_refs[i], slot(i, *me), local_sems.at[i]) for i in range(n)]
        first = []
        for i in range(n):
            first.append(copy(i, 0, me, sibling, src=x_refs[i]))
            first += [copy(i, 1 + j, me, (*chip, c), src=x_refs[i]) for j, chip in enumerate(chips)]
        for cp in mine + first:
            cp.start()
        passed = []
        for j, chip in enumerate(chips):
            for i in range(n):
                copy(i, 1 + j, (*chip, c), me).wait_recv()
                passed.append(copy(i, 4 + j, (*chip, c), sibling))
                passed[-1].start()
        for i in range(n):
            copy(i, 0, sibling, me).wait_recv()
            for j, chip in enumerate(chips):
                copy(i, 4 + j, (*chip, 1 - c), me).wait_recv()
        for cp in first + passed:
            cp.wait_send()
        for cp in mine:
            cp.wait()

    return pl.pallas_call(
        body, name=name, out_shape=[_sds((8,) + b.shape, b.dtype) for b in blocks],
        in_specs=[_ANY] * n, out_specs=[_ANY] * n,
        scratch_shapes=[pltpu.SemaphoreType.DMA((n, 7)), pltpu.SemaphoreType.DMA((n, 7)), pltpu.SemaphoreType.DMA((n,))],
    )(*blocks)


def _sibling_exchange(name, sends):
    n = len(sends)

    def body(*refs):
        x, y, c = lax.axis_index("x"), lax.axis_index("y"), lax.axis_index("c")
        send_sems, recv_sems = refs[2 * n:]
        copies = [pltpu.make_async_remote_copy(src_ref=refs[i], dst_ref=refs[n + i], send_sem=send_sems.at[i],
                                               recv_sem=recv_sems.at[i], device_id=(x, y, 1 - c), device_id_type=MESH)
                  for i in range(n)]
        for cp in copies:
            cp.start()
        for cp in copies:
            cp.wait()

    return pl.pallas_call(
        body, name=name, out_shape=[_sds(s.shape, s.dtype) for s in sends],
        in_specs=[_ANY] * n, out_specs=[_ANY] * n,
        scratch_shapes=[pltpu.SemaphoreType.DMA((n,)), pltpu.SemaphoreType.DMA((n,))],
    )(*sends)


def _chip_scatter(name, parts):
    n = len(parts)

    def body(*refs):
        p_refs, out_refs = refs[:n], refs[n:2 * n]
        send_sems, recv_sems, local_sems = refs[2 * n:]
        x, y, c = lax.axis_index("x"), lax.axis_index("y"), lax.axis_index("c")
        sibling = (x, y, 1 - c)
        me_chip = 2 * x + y
        chips = [(1 - x, y), (x, 1 - y), (1 - x, 1 - y)]

        def chip_id(chip):
            return 2 * chip[0] + chip[1]

        def copy(i, k, src, dst, to):
            return pltpu.make_async_remote_copy(src_ref=src, dst_ref=dst, send_sem=send_sems.at[i, k],
                                                recv_sem=recv_sems.at[i, k], device_id=to, device_id_type=MESH)

        mine = [pltpu.make_async_copy(p_refs[i].at[me_chip], out_refs[i].at[c, me_chip], local_sems.at[i]) for i in range(n)]
        first = []
        for i in range(n):
            first.append(copy(i, 0, p_refs[i].at[me_chip], out_refs[i].at[c, me_chip], sibling))
            first += [copy(i, 1 + j, p_refs[i].at[chip_id(chip)], out_refs[i].at[c, me_chip], (*chip, c))
                      for j, chip in enumerate(chips)]
        for cp in mine + first:
            cp.start()
        passed = []
        for j, chip in enumerate(chips):
            for i in range(n):
                landed = out_refs[i].at[c, chip_id(chip)]
                copy(i, 1 + j, p_refs[i].at[0], landed, sibling).wait_recv()
                passed.append(copy(i, 4 + j, landed, landed, sibling))
                passed[-1].start()
        for i in range(n):
            copy(i, 0, p_refs[i].at[0], out_refs[i].at[1 - c, me_chip], sibling).wait_recv()
            for j, chip in enumerate(chips):
                copy(i, 4 + j, p_refs[i].at[0], out_refs[i].at[1 - c, chip_id(chip)], sibling).wait_recv()
        for cp in first + passed:
            cp.wait_send()
        for cp in mine:
            cp.wait()

    return pl.pallas_call(
        body, name=name, out_shape=[_sds((2,) + p.shape, p.dtype) for p in parts],
        in_specs=[_ANY] * n, out_specs=[_ANY] * n,
        scratch_shapes=[pltpu.SemaphoreType.DMA((n, 7)), pltpu.SemaphoreType.DMA((n, 7)), pltpu.SemaphoreType.DMA((n,))],
    )(*parts)


def _pack_flat(arrays, multiple, dtype):
    flat = jnp.concatenate([a.reshape(-1).astype(dtype) for a in arrays])
    pad = (-flat.shape[0]) % multiple
    if pad:
        flat = jnp.concatenate([flat, jnp.zeros((pad,), dtype)])
    return flat


def _unpack_flat(flat, shapes):
    out, off = [], 0
    for s in shapes:
        n = math.prod(s)
        out.append(flat[off:off + n].reshape(s))
        off += n
    return out


def _row_tile(r, cap, mult=16):
    for cand in range(min(cap, r) // mult * mult, 0, -mult):
        if r % cand == 0:
            return cand
    return r


ELEMENTWISE_BLOCK_ELEMS = 1 << 20


def _sum_axis1(name, x, out_dtype=F32):
    h, k, r, n = x.shape
    tr = _row_tile(r, max(16, ELEMENTWISE_BLOCK_ELEMS // (k * n)))

    def fn(s):
        acc = s[0].astype(F32)
        for i in range(1, k):
            acc = acc + s[i].astype(F32)
        return acc

    return _blocked(name, fn, (h, r // tr), [x], [pl.BlockSpec((None, k, tr, n), lambda i, t: (i, 0, t, 0))],
                    [_sds((h, r, n), out_dtype)], [pl.BlockSpec((None, tr, n), lambda i, t: (i, t, 0))])[0]


def _sum_pair(name, a, b, out_dtype):
    lead, r, n = a.shape
    tr = _row_tile(r, max(16, ELEMENTWISE_BLOCK_ELEMS // n))
    spec = pl.BlockSpec((None, tr, n), lambda i, t: (i, t, 0))
    return _blocked(name, lambda x, y: x.astype(F32) + y.astype(F32), (lead, r // tr), [a, b], [spec, spec],
                    [_sds((lead, r, n), out_dtype)], [spec])[0]


def _adamw(name, w, g, m, v):
    r, c = w.shape
    tr = _row_tile(r, max(8, (1 << 21) // (4 * c)), 8)
    bc1 = 1.0 - ADAM_B1 ** ADAM_STEP
    bc2 = 1.0 - ADAM_B2 ** ADAM_STEP

    def fn(w, g, m, v):
        nm = ADAM_B1 * m + (1.0 - ADAM_B1) * g
        nv = ADAM_B2 * v + (1.0 - ADAM_B2) * (g * g)
        delta = -ADAM_LR * ((nm / bc1) / (jnp.sqrt(nv / bc2) + ADAM_EPS) + ADAM_WD * w)
        return delta, nm, nv

    spec = pl.BlockSpec((tr, c), lambda t: (t, 0))
    return _blocked(name, fn, (r // tr,), [w, g, m, v], [spec] * 4, [_sds((r, c))] * 3, [spec] * 3)


def _mod_rows(c16, ada_w):
    depth, d, n = ada_w.shape

    def fn(c, w):
        return _mm(_silu(c), w)

    return _blocked("mod_rows", fn, (depth,), [c16, ada_w],
                    [_res(c16.shape), pl.BlockSpec((None, d, n), lambda l: (l, 0, 0))],
                    [_sds((depth, 16, n))], [pl.BlockSpec((None, 16, n), lambda l: (l, 0, 0))])[0]


def _ada_grads(c16, dmod16, ada_w):
    depth, d, n = ada_w.shape

    def fn(c, dm, w):
        return _mm_tn(_silu(c), dm), _mm_nt(dm, w)

    return _blocked("ada_grads", fn, (depth,), [c16, dmod16, ada_w],
                    [_res(c16.shape), pl.BlockSpec((None, 16, n), lambda l: (l, 0, 0)),
                     pl.BlockSpec((None, d, n), lambda l: (l, 0, 0))],
                    [_sds((depth, d, n)), _sds((depth, 16, d))],
                    [pl.BlockSpec((None, d, n), lambda l: (l, 0, 0)), pl.BlockSpec((None, 16, d), lambda l: (l, 0, 0))])


def _cctx_grad(parts, c_ctx_row):
    def fn(p, c):
        tot = ((p[0] + p[1]) + p[2]) + p[3]
        _, vjp = jax.vjp(_silu, c)
        return vjp(tot)[0]

    return _blocked("cctx_grad", fn, (1,), [parts, c_ctx_row], [_res(parts.shape), _res(c_ctx_row.shape)],
                    [_sds(c_ctx_row.shape)], [_res(c_ctx_row.shape)])[0]


def _modulate(x, sh, sc):
    return x * (1.0 + sc) + sh


def _weight_operand(w):
    if isinstance(w, tuple):
        return w[0], w[1], tuple(b for b in w[1].block_shape if b is not None)
    return w, _res(w.shape), w.shape


def _matmul_cols(u, w):
    if w.ndim == 2:
        return jnp.dot(u, w, preferred_element_type=F32)
    return jnp.concatenate([jnp.dot(u, w[j], preferred_element_type=F32) for j in range(w.shape[0])], axis=1)


def _matmul_cols_t(dp, w):
    if w.ndim == 2:
        return _mm_nt(dp, w)
    nj = w.shape[2]
    return functools.reduce(lambda a, b: a + b, [_mm_nt(dp[:, j * nj:(j + 1) * nj], w[j]) for j in range(w.shape[0])])


def _rows_weight(w):
    return w if w.ndim == 2 else w.reshape(w.shape[0] * w.shape[1], w.shape[2])


def _inproj(name, x, sh, sc, w, tm):
    t, d = x.shape
    w, w_spec, w_blk = _weight_operand(w)
    n = w_blk[-1] * (w_blk[0] if len(w_blk) == 3 else 1)
    tm = min(tm, t)

    def fn(x, sh, sc, w):
        u = _modulate(x, sh, sc).astype(MXU_DTYPE)
        return _matmul_cols(u, w), u

    return _blocked(name, fn, (t // tm,), [x, sh, sc, w],
                    [_tok(tm, d), _res(sh.shape), _res(sc.shape), w_spec],
                    [_sds((t, n)), _sds((t, d), MXU_DTYPE)], [_tok(tm, n), _tok(tm, d)])


def _inproj_bwd(name, dp, x, sh, sc, w, add, tm):
    t, d = x.shape
    w, w_spec, w_blk = _weight_operand(w)
    n = w_blk[-1] * (w_blk[0] if len(w_blk) == 3 else 1)
    tm = min(tm, t)
    has_add = add is not None

    def fn(dp, x, sh, sc, w, *rest):
        du = _matmul_cols_t(dp, w)
        _, vjp = jax.vjp(_modulate, x, sh, sc)
        dx, dsh, dsc = vjp(du)
        if has_add:
            dx = dx + rest[0]
        return dx, dsh, dsc

    ins = [dp, x, sh, sc, w] + ([add] if has_add else [])
    specs = [_tok(tm, n), _tok(tm, d), _res(sh.shape), _res(sc.shape), w_spec] + ([_tok(tm, d)] if has_add else [])
    return _blocked(name, fn, (t // tm,), ins, specs,
                    [_sds((t, d)), _sds(sh.shape), _sds(sc.shape)], [_tok(tm, d), _res(sh.shape), _res(sc.shape)],
                    acc=(1, 2), acc_axis=0)


def _residual_ln(y, x, gt, lng, lnb):
    return _layernorm(ALPHA * x + gt * y, lng, lnb)


def _gdn_mix(o0, o1, pg, yp, normw):
    o = o0 + o1
    heads = []
    for h in range(GDN_HEADS):
        oh = o[:, h * HEAD_DIM:(h + 1) * HEAD_DIM]
        heads.append(oh * lax.rsqrt(jnp.mean(oh * oh, -1, keepdims=True) + RMS_EPS) * normw)
    on = jnp.concatenate(heads, axis=-1) * _silu(pg)
    return jnp.concatenate([on, yp], axis=-1)


def _even_out(o0, o1, p, ypool, x, normw, gt, lng, lnb, w, tm):
    t, d = x.shape
    tm = min(tm, t)
    gate_blk = 3

    def fn(o0, o1, pg, yp, x, normw, gt, lng, lnb, w):
        mix = _gdn_mix(o0, o1, pg, yp, normw).astype(MXU_DTYPE)
        y = jnp.dot(mix, w, preferred_element_type=F32)
        return _residual_ln(y, x, gt, lng, lnb), mix, y

    return _blocked("even_out", fn, (t // tm,), [o0, o1, p, ypool, x, normw, gt, lng, lnb, w],
                    [_tok(tm, 512), _tok(tm, 512), _tok(tm, 512, gate_blk), _tok(tm, 512), _tok(tm, d),
                     _res(normw.shape), _res(gt.shape), _res(lng.shape), _res(lnb.shape), _res(w.shape)],
                    [_sds((t, d)), _sds((t, d), MXU_DTYPE), _sds((t, d))], [_tok(tm, d)] * 3)


def _even_out_bwd(dx1, o0, o1, p, ypool, x, y, normw, gt, lng, lnb, w, tm):
    t, d = x.shape
    tm = min(tm, t)

    def fn(dx1, o0, o1, pg, yp, x, y, normw, gt, lng, lnb, w):
        _, vjp2 = jax.vjp(_residual_ln, y, x, gt, lng, lnb)
        dy, dx, dgt, dlng, dlnb = vjp2(dx1)
        dyb = dy.astype(MXU_DTYPE)
        dmix = _mm_nt(dyb, w)
        _, vjp1 = jax.vjp(_gdn_mix, o0, o1, pg, yp, normw)
        do, _, dpg, dyp, dnormw = vjp1(dmix)
        return do, dpg, dyp, dx, dyb, dnormw, dgt, dlng, dlnb

    return _blocked("even_out_bwd", fn, (t // tm,), [dx1, o0, o1, p, ypool, x, y, normw, gt, lng, lnb, w],
                    [_tok(tm, d), _tok(tm, 512), _tok(tm, 512), _tok(tm, 512, 3), _tok(tm, 512),
                     _tok(tm, d), _tok(tm, d), _res(normw.shape), _res(gt.shape), _res(lng.shape), _res(lnb.shape),
                     _res(w.shape)],
                    [_sds((t, 512)), _sds((t, 512), MXU_DTYPE), _sds((t, 512)), _sds((t, d)), _sds((t, d), MXU_DTYPE),
                     _sds(normw.shape), _sds(gt.shape), _sds(lng.shape), _sds(lnb.shape)],
                    [_tok(tm, 512), _tok(tm, 512), _tok(tm, 512), _tok(tm, d), _tok(tm, d),
                     _res(normw.shape), _res(gt.shape), _res(lng.shape), _res(lnb.shape)],
                    acc=(5, 6, 7, 8), acc_axis=0)


def _odd_mix(ysc, z0, cg, cb):
    z = _silu(_layernorm(z0, cg, cb))
    return jnp.concatenate([ysc, z], axis=-1)


def _odd_out(ysc, z0, x, cg, cb, gt, lng, lnb, w, tm):
    t, d = x.shape
    tm = min(tm, t)

    def fn(ysc, z0, x, cg, cb, gt, lng, lnb, w):
        mix = _odd_mix(ysc, z0, cg, cb).astype(MXU_DTYPE)
        y = jnp.dot(mix, w, preferred_element_type=F32)
        return _residual_ln(y, x, gt, lng, lnb), mix, y

    return _blocked("odd_out", fn, (t // tm,), [ysc, z0, x, cg, cb, gt, lng, lnb, w],
                    [_tok(tm, 512), _tok(tm, 512), _tok(tm, d), _res(cg.shape), _res(cb.shape), _res(gt.shape),
                     _res(lng.shape), _res(lnb.shape), _res(w.shape)],
                    [_sds((t, d)), _sds((t, d), MXU_DTYPE), _sds((t, d))], [_tok(tm, d)] * 3)


def _odd_out_bwd(dx3, ysc, z0, x, y, cg, cb, gt, lng, lnb, w, tm):
    t, d = x.shape
    tm = min(tm, t)

    def fn(dx3, ysc, z0, x, y, cg, cb, gt, lng, lnb, w):
        _, vjp2 = jax.vjp(_residual_ln, y, x, gt, lng, lnb)
        dy, dx, dgt, dlng, dlnb = vjp2(dx3)
        dyb = dy.astype(MXU_DTYPE)
        dmix = _mm_nt(dyb, w)
        _, vjp1 = jax.vjp(_odd_mix, ysc, z0, cg, cb)
        dysc, dz0, dcg, dcb = vjp1(dmix)
        return dysc, dz0, dx, dyb, dcg, dcb, dgt, dlng, dlnb

    return _blocked("odd_out_bwd", fn, (t // tm,), [dx3, ysc, z0, x, y, cg, cb, gt, lng, lnb, w],
                    [_tok(tm, d), _tok(tm, 512), _tok(tm, 512), _tok(tm, d), _tok(tm, d), _res(cg.shape), _res(cb.shape),
                     _res(gt.shape), _res(lng.shape), _res(lnb.shape), _res(w.shape)],
                    [_sds((t, 512)), _sds((t, 512)), _sds((t, d)), _sds((t, d), MXU_DTYPE),
                     _sds(cg.shape), _sds(cb.shape), _sds(gt.shape), _sds(lng.shape), _sds(lnb.shape)],
                    [_tok(tm, 512), _tok(tm, 512), _tok(tm, d), _tok(tm, d),
                     _res(cg.shape), _res(cb.shape), _res(gt.shape), _res(lng.shape), _res(lnb.shape)],
                    acc=(4, 5, 6, 7, 8), acc_axis=0)


def _ffn_down(name, s, x, gt, lng, lnb, w, tm):
    t, d = x.shape
    f = s.shape[1]
    w, w_spec, _ = _weight_operand(w)
    tm = min(tm, t)

    def fn(s, x, gt, lng, lnb, w):
        y = jnp.dot(s, _rows_weight(w), preferred_element_type=F32)
        return _residual_ln(y, x, gt, lng, lnb), y

    return _blocked(name, fn, (t // tm,), [s, x, gt, lng, lnb, w],
                    [_tok(tm, f), _tok(tm, d), _res(gt.shape), _res(lng.shape), _res(lnb.shape), w_spec],
                    [_sds((t, d)), _sds((t, d))], [_tok(tm, d)] * 2)


def _ffn_down_bwd(name, dx2, x, y, gt, lng, lnb, w, tm):
    t, d = x.shape
    w, w_spec, w_blk = _weight_operand(w)
    f = math.prod(w_blk[:-1])
    tm = min(tm, t)

    def fn(dx2, x, y, gt, lng, lnb, w):
        _, vjp2 = jax.vjp(_residual_ln, y, x, gt, lng, lnb)
        dy, dx, dgt, dlng, dlnb = vjp2(dx2)
        dyb = dy.astype(MXU_DTYPE)
        return _mm_nt(dyb, _rows_weight(w)), dx, dyb, dgt, dlng, dlnb

    return _blocked(name, fn, (t // tm,), [dx2, x, y, gt, lng, lnb, w],
                    [_tok(tm, d), _tok(tm, d), _tok(tm, d), _res(gt.shape), _res(lng.shape), _res(lnb.shape), w_spec],
                    [_sds((t, f), MXU_DTYPE), _sds((t, d)), _sds((t, d), MXU_DTYPE), _sds(gt.shape), _sds(lng.shape), _sds(lnb.shape)],
                    [_tok(tm, f), _tok(tm, d), _tok(tm, d), _res(gt.shape), _res(lng.shape), _res(lnb.shape)],
                    acc=(3, 4, 5), acc_axis=0)


def _loss_and_grad(x4, target, tm):
    t, d = x4.shape
    tm = min(tm, t)

    def fn(y, tg):
        e = y - tg
        part = 0.5 * jnp.sum(jnp.mean(e * e, axis=-1, keepdims=True), axis=0, keepdims=True)
        return jnp.broadcast_to(part, (1, LANES)), e * (1.0 / d)

    return _blocked("loss", fn, (t // tm,), [x4, target], [_tok(tm, d), _tok(tm, d)],
                    [_sds((1, LANES)), _sds((t, d))], [_res((1, LANES)), _tok(tm, d)], acc=(0,), acc_axis=0)


def _matmul_tn(name, pairs, chip_cols=None):
    k = pairs[0][0].shape[1]
    n = pairs[0][1].shape[1]
    tk = k if k <= 1024 else k // 2
    if chip_cols is None:
        tn = 512 if n % 512 == 0 else (384 if n % 384 == 0 else 128)
        out_spec, out_shape = pl.BlockSpec((tk, tn), lambda i, j: (i, j)), _sds((k, n), MXU_DTYPE)
    else:
        tn = chip_cols
        tk = min(tk, 512) if tn > 1024 else tk
        out_spec = pl.BlockSpec((None, tk, tn), lambda i, j: (j, i, 0))
        out_shape = _sds((n // chip_cols, k, chip_cols), MXU_DTYPE)

    def body(*refs):
        acc = None
        for p in range(len(pairs)):
            term = lax.dot_general(refs[2 * p][...], refs[2 * p + 1][...], (((0,), (0,)), ((), ())), preferred_element_type=F32)
            acc = term if acc is None else acc + term
        refs[-1][...] = acc.astype(refs[-1].dtype)

    in_specs, args = [], []
    for a, b in pairs:
        t = a.shape[0]
        in_specs += [pl.BlockSpec((t, tk), lambda i, j: (0, i)), pl.BlockSpec((t, tn), lambda i, j: (0, j))]
        args += [a, b]
    return pl.pallas_call(
        body, name=name, grid=(k // tk, n // tn), in_specs=in_specs,
        out_specs=out_spec, out_shape=out_shape,
        compiler_params=pltpu.CompilerParams(dimension_semantics=("arbitrary", "arbitrary"),
                                             vmem_limit_bytes=VMEM_LIMIT_BYTES),
    )(*args)


def _chan(t, col_of):
    return pl.BlockSpec((t, LANES), lambda i: (0, col_of(i)))


def _wblk(k, col_of=lambda i: i):
    return pl.BlockSpec((k, LANES), lambda i: (0, col_of(i)))


SEQ_TILE = 256
GRID_SEQ_TILE = 512


def _load_ext(ref, r0, rows, halo, total):
    lo, hi = max(r0 - halo, 0), min(r0 + rows + halo, total)
    parts = []
    if lo > r0 - halo:
        parts.append(jnp.zeros((lo - (r0 - halo), ref.shape[1]), F32))
    parts.append(ref[lo:hi, :].astype(F32))
    if hi < r0 + rows + halo:
        parts.append(jnp.zeros((r0 + rows + halo - hi, ref.shape[1]), F32))
    return parts[0] if len(parts) == 1 else jnp.concatenate(parts, axis=0)


def _seq_stage(name, tile_fn, halo, grid_n, seqs, pars, outs, tile_rows=SEQ_TILE):
    total = seqs[0][0].shape[0]
    rows = min(tile_rows, total)
    ns, npar = len(seqs), len(pars)

    def body(*refs):
        seq_refs, par_refs, out_refs = refs[:ns], refs[ns:ns + npar], refs[ns + npar:]
        par_vals = [r[...] for r in par_refs]
        for r0 in range(0, total, rows):
            exts = [_load_ext(r, r0, rows, halo, total) for r in seq_refs]
            res = tile_fn(r0, *exts, *par_vals)
            for o_ref, v in zip(out_refs, res):
                o_ref[r0:r0 + rows, :] = v.astype(o_ref.dtype)

    return pl.pallas_call(
        body, name=name, grid=(grid_n,), in_specs=[s for _, s in seqs] + [s for _, s in pars],
        out_specs=[s for _, s in outs], out_shape=[o for o, _ in outs],
        compiler_params=pltpu.CompilerParams(dimension_semantics=("arbitrary",), vmem_limit_bytes=VMEM_LIMIT_BYTES),
    )(*[a for a, _ in seqs], *[a for a, _ in pars])


def _seq_stage_bwd(name, tile_fn, halo, grid_n, douts, seqs, pars, dseq_outs, dpar_outs, tile_rows=SEQ_TILE):
    total = seqs[0][0].shape[0]
    rows = min(tile_rows, total)
    groups = [d if isinstance(d, list) else [d] for d in douts]
    douts = [d for g in groups for d in g]
    starts = [sum(len(g) for g in groups[:k]) for k in range(len(groups))]
    nd, ns, npar = len(douts), len(seqs), len(pars)
    widths = [s.block_shape[-1] for _, s in seqs]

    def dtile(dout_refs, k, r0):
        terms = [d[r0:r0 + rows, :].astype(F32) for d in dout_refs[starts[k]:starts[k] + len(groups[k])]]
        return functools.reduce(lambda a, b: a + b, terms)

    def body(*refs):
        dout_refs, seq_refs, par_refs = refs[:nd], refs[nd:nd + ns], refs[nd + ns:nd + ns + npar]
        dseq_refs = refs[nd + ns + npar:nd + 2 * ns + npar]
        dpar_refs = refs[nd + 2 * ns + npar:nd + 2 * ns + 2 * npar]
        accs = refs[nd + 2 * ns + 2 * npar:]
        par_vals = [r[...] for r in par_refs]
        for a in accs:
            for r0 in range(0, total, rows):
                a[r0:r0 + rows, :] = jnp.zeros((rows, a.shape[1]), F32)
        dpars = [jnp.zeros(p.shape, F32) for p in par_vals]
        for r0 in range(0, total, rows):
            exts = [_load_ext(r, r0, rows, halo, total) for r in seq_refs]
            _, vjp = jax.vjp(functools.partial(tile_fn, r0), *exts, *par_vals)
            cts = vjp(tuple(dtile(dout_refs, k, r0) for k in range(len(groups))))
            lo, hi = max(r0 - halo, 0), min(r0 + rows + halo, total)
            for a, dx in zip(accs, cts[:ns]):
                a[lo:hi, :] += dx[lo - (r0 - halo):hi - (r0 - halo)]
            dpars = [acc + g for acc, g in zip(dpars, cts[ns:])]
        for o_ref, a in zip(dseq_refs, accs):
            for r0 in range(0, total, rows):
                o_ref[r0:r0 + rows, :] = a[r0:r0 + rows, :].astype(o_ref.dtype)
        for o_ref, g in zip(dpar_refs, dpars):
            o_ref[...] = g

    return pl.pallas_call(
        body, name=name, grid=(grid_n,),
        in_specs=[s for _, s in douts] + [s for _, s in seqs] + [s for _, s in pars],
        out_specs=[s for _, s in dseq_outs] + [s for _, s in dpar_outs],
        out_shape=[o for o, _ in dseq_outs] + [o for o, _ in dpar_outs],
        scratch_shapes=[pltpu.VMEM((total, w), F32) for w in widths],
        compiler_params=pltpu.CompilerParams(dimension_semantics=("arbitrary",), vmem_limit_bytes=VMEM_LIMIT_BYTES),
    )(*[a for a, _ in douts], *[a for a, _ in seqs], *[a for a, _ in pars])


def _qkv_tile(r0, p_ext, w):
    kind = pl.program_id(0)
    a = _silu(_conv5(p_ext, w))
    nrm = a * lax.rsqrt(jnp.sum(a * a, -1, keepdims=True) + RMS_EPS)
    scale = jnp.where(kind < GDN_HEADS, HEAD_DIM ** -0.5, 1.0).astype(F32)
    sel = (kind < 2 * GDN_HEADS).astype(F32)
    return (sel * (nrm * scale) + (1.0 - sel) * a,)


def _qkv_conv(name, p, w):
    t = p.shape[0]
    nb = w.shape[1] // LANES
    ident = lambda i: i
    return _seq_stage(name, _qkv_tile, HALO_SHORT, nb, [(p, _chan(t, ident))], [(w, _wblk(GDN_CONV))],
                      [(_sds((t, nb * LANES)), _chan(t, ident))])[0]


def _qkv_conv_bwd(name, dqkv0, dqkv1, p, w):
    t = p.shape[0]
    nb = w.shape[1] // LANES
    ident = lambda i: i
    return _seq_stage_bwd(name, _qkv_tile, HALO_SHORT, nb, [[(dqkv0, _chan(t, ident)), (dqkv1, _chan(t, ident))]],
                          [(p, _chan(t, ident))],
                          [(w, _wblk(GDN_CONV))], [(_sds((t, nb * LANES), MXU_DTYPE), _chan(t, ident))],
                          [(_sds(w.shape), _wblk(GDN_CONV))])


def _gates_fn(s, avec, dtvec):
    lane = lax.broadcasted_iota(jnp.int32, s.shape, 1)
    beta = jax.nn.sigmoid(s)
    g = -jnp.exp(avec) * jax.nn.softplus(s + dtvec)
    nh = 2 * GDN_HEADS
    return jnp.where(lane < nh, beta, jnp.where(lane < 2 * nh, g, 0.0))


def _gates(name, p, avec, dtvec, col):
    t = p.shape[0]
    tm = min(512, t)
    return _blocked(name, _gates_fn, (t // tm,), [p, avec, dtvec], [_tok(tm, LANES, col), _res(avec.shape), _res(dtvec.shape)],
                    [_sds((t, LANES))], [_tok(tm, LANES)])[0]


def _gates_bwd(name, dgb, p, avec, dtvec, col):
    t = p.shape[0]
    tm = min(512, t)

    def fn(dgb, s, avec, dtvec):
        _, vjp = jax.vjp(_gates_fn, s, avec, dtvec)
        return vjp(dgb)

    return _blocked(name, fn, (t // tm,), [dgb, p, avec, dtvec],
                    [_tok(tm, LANES), _tok(tm, LANES, col), _res(avec.shape), _res(dtvec.shape)],
                    [_sds((t, LANES), MXU_DTYPE), _sds(avec.shape), _sds(dtvec.shape)],
                    [_tok(tm, LANES), _res(avec.shape), _res(dtvec.shape)], acc=(1, 2), acc_axis=0)


def _make_pool_tile(total):
    def tile(r0, x_ext, scale, *ws):
        ys = []
        for g, win in enumerate(POOL_WINDOWS):
            xg = x_ext[:, g * LANES:(g + 1) * LANES]
            rows = xg.shape[0] - 2 * HALO_SHORT
            pos = r0 + lax.broadcasted_iota(jnp.int32, (rows, LANES), 0)
            lo = jnp.clip(pos - win // 2, 0, total)
            hi = jnp.clip(pos - win // 2 + win, 0, total)
            window_sum = _pool_sums[win](xg, jnp.ones((win, LANES), F32))
            pooled = window_sum / (hi - lo).astype(F32) - _center(xg, HALO_SHORT)
            ys.append(_mm(pooled, ws[g]))
        return (jnp.concatenate(ys, axis=-1) * scale,)

    return tile


def _pool_specs(p, pool_w, pool_scale, col0):
    t = p.shape[0]
    n = len(POOL_WINDOWS) * LANES
    seq = (p, pl.BlockSpec((t, n), lambda i: (0, col0)))
    pars = [(pool_scale, _res(pool_scale.shape))]
    pars += [(pool_w, pl.BlockSpec((None, LANES, LANES), lambda i, g=g: (g, 0, 0))) for g in range(len(POOL_WINDOWS))]
    return t, n, seq, pars


def _pool(p, pool_w, pool_scale, col0):
    t, n, seq, pars = _pool_specs(p, pool_w, pool_scale, col0)
    return _seq_stage("pool", _make_pool_tile(t), HALO_SHORT, 1, [seq], pars, [(_sds((t, n)), _res((t, n)))])[0]


def _pool_bwd(dy, p, pool_w, pool_scale, col0):
    t, n, seq, pars = _pool_specs(p, pool_w, pool_scale, col0)
    wspec = (_sds((LANES, LANES)), _res((LANES, LANES)))
    return _seq_stage_bwd("pool_bwd", _make_pool_tile(t), HALO_SHORT, 1, [(dy, _res(dy.shape))], [seq], pars,
                          [(_sds((t, n), MXU_DTYPE), _res((t, n)))],
                          [(_sds(pool_scale.shape), _res(pool_scale.shape))] + [wspec] * len(POOL_WINDOWS))


def _ffn_conv_tile(r0, a_ext, gt_ext, w):
    return (_silu(_conv3x3(a_ext, w)) * _center(gt_ext, HALO_GRID),)


def _ffn_conv(name, h, w9):
    t = h.shape[0]
    nb = w9.shape[1] // LANES
    ident = lambda i: i
    return _seq_stage(name, _ffn_conv_tile, HALO_GRID, nb, [(h, _chan(t, ident)), (h, _chan(t, lambda i: nb + i))],
                      [(w9, _wblk(9))], [(_sds((t, nb * LANES), MXU_DTYPE), _chan(t, ident))], GRID_SEQ_TILE)[0]


def _ffn_conv_bwd(name, ds, h, w9):
    t = h.shape[0]
    nb = w9.shape[1] // LANES
    ident = lambda i: i
    o = (_sds((t, nb * LANES), MXU_DTYPE), _chan(t, ident))
    return _seq_stage_bwd(name, _ffn_conv_tile, HALO_GRID, nb, [(ds, _chan(t, ident))],
                          [(h, _chan(t, ident)), (h, _chan(t, lambda i: nb + i))], [(w9, _wblk(9))],
                          [o, o], [(_sds(w9.shape), _wblk(9))], GRID_SEQ_TILE)


def _sconv_tile(r0, gb_ext, gc_ext, h_ext, w):
    return (_center(gb_ext, HALO_SHORT) * _conv3(gc_ext * h_ext, w),)


def _sconv_specs(p1, w3):
    t = p1.shape[0]
    nb = w3.shape[1] // LANES
    seqs = [(p1, _chan(t, lambda i: i)), (p1, _chan(t, lambda i: nb + i)), (p1, _chan(t, lambda i: 2 * nb + i))]
    return t, nb, seqs, [(w3, _wblk(SC_WIDTH))]


def _sconv(p1, w3):
    t, nb, seqs, pars = _sconv_specs(p1, w3)
    return _seq_stage("sconv", _sconv_tile, HALO_SHORT, nb, seqs, pars, [(_sds((t, nb * LANES)), _chan(t, lambda i: i))])[0]


def _sconv_bwd(dysc, p1, w3):
    t, nb, seqs, pars = _sconv_specs(p1, w3)
    o = (_sds((t, nb * LANES), MXU_DTYPE), _chan(t, lambda i: i))
    return _seq_stage_bwd("sconv_bwd", _sconv_tile, HALO_SHORT, nb, [(dysc, _chan(t, lambda i: i))], seqs, pars,
                          [o, o, o], [(_sds(w3.shape), _wblk(SC_WIDTH))])


def _conf_tile(r0, ga_ext, gb_ext, w):
    return (_conv31(ga_ext * jax.nn.sigmoid(gb_ext), w),)


def _conf_specs(p1, w31, blk0):
    t = p1.shape[0]
    nb = w31.shape[1] // LANES
    seqs = [(p1, _chan(t, lambda i: blk0 + i)), (p1, _chan(t, lambda i: blk0 + nb + i))]
    return t, nb, seqs, [(w31, _wblk(CF_WIDTH))]


def _confconv(p1, w31, blk0):
    t, nb, seqs, pars = _conf_specs(p1, w31, blk0)
    return _seq_stage("confconv", _conf_tile, HALO_CONF, nb, seqs, pars, [(_sds((t, nb * LANES)), _chan(t, lambda i: i))])[0]


def _confconv_bwd(dz0, p1, w31, blk0):
    t, nb, seqs, pars = _conf_specs(p1, w31, blk0)
    o = (_sds((t, nb * LANES), MXU_DTYPE), _chan(t, lambda i: i))
    return _seq_stage_bwd("confconv_bwd", _conf_tile, HALO_CONF, nb, [(dz0, _chan(t, lambda i: i))], seqs, pars,
                          [o, o], [(_sds(w31.shape), _wblk(CF_WIDTH))])


def _bdot(a, b, ca, cb, hi):
    if hi:
        return lax.dot_general(a, b, (((ca,), (cb,)), ((0,), (0,))), preferred_element_type=F32, precision=HIGHEST)
    return lax.dot_general(a.astype(MXU_DTYPE), b.astype(MXU_DTYPE), (((ca,), (cb,)), ((0,), (0,))), preferred_element_type=F32)


def _make_bmm(hi):
    nn_i = lambda a, b: _bdot(a, b, 2, 1, hi)
    nt_i = lambda a, b: _bdot(a, b, 2, 2, hi)
    tn_i = lambda a, b: _bdot(a, b, 1, 1, hi)
    nn = _mm_vjp(nn_i, lambda g, b: nt_i(g, b), lambda a, g: tn_i(a, g))
    nt = _mm_vjp(nt_i, lambda g, b: nn_i(g, b), lambda a, g: tn_i(g, a))
    tn = _mm_vjp(tn_i, lambda g, b: nt_i(b, g), lambda a, g: nn_i(a, g))
    return nn, nt, tn


_bmm, _bmm_nt, _bmm_tn = _make_bmm(False)
_bmm_hi, _bmm_hi_nt, _bmm_hi_tn = _make_bmm(True)


def _unit_tri_inverse(a):
    c = a.shape[-1]
    eye = (lax.broadcasted_iota(jnp.int32, a.shape, 1) == lax.broadcasted_iota(jnp.int32, a.shape, 2)).astype(F32)
    levels = max(1, int(math.ceil(math.log2(c))) - 1)
    p = eye - a
    m = _bdot(a, a, 2, 1, True)
    for level in range(levels):
        p_next = p + _bdot(p, m, 2, 1, True)
        if level + 1 < levels:
            m = _bdot(m, m, 2, 1, True)
        p = p_next
    return p


@jax.custom_vjp
def _known_inverse(a, tinv):
    return tinv


def _known_inverse_fwd(a, tinv):
    return tinv, tinv


def _known_inverse_bwd(tinv, dt):
    da = -_bdot(_bdot(tinv, dt, 1, 1, True), tinv, 2, 2, True)
    return da, jnp.zeros_like(tinv)


_known_inverse.defvjp(_known_inverse_fwd, _known_inverse_bwd)


def _gdn_chunk(q, k, v, gcol, grow, bcol, s, tinv=None):
    n, c, _ = q.shape
    shape = (n, c, c)
    fwd_dir = lax.broadcasted_iota(jnp.int32, shape, 0) < n // 2
    i = lax.broadcasted_iota(jnp.int32, shape, 1)
    j = lax.broadcasted_iota(jnp.int32, shape, 2)
    order = jnp.where(fwd_dir, i - j, j - i)
    incl = order >= 0
    gc_col = jnp.sum(jnp.where(incl, grow, 0.0), axis=2, keepdims=True)
    gc_row = jnp.sum(jnp.where(order <= 0, gcol, 0.0), axis=1, keepdims=True)
    gtot = jnp.sum(grow, axis=2, keepdims=True)
    decay = jnp.exp(jnp.where(incl, gc_col - gc_row, -1e30))
    kb = k * bcol
    a = jnp.where(order > 0, _bmm_nt(kb, k) * decay, 0.0)
    tinv = _unit_tri_inverse(a) if tinv is None else _known_inverse(a, tinv)
    e_col = jnp.exp(gc_col)
    u = _bmm_hi(tinv, v * bcol)
    w = _bmm_hi(tinv, kb * e_col)
    k_dec = k * jnp.exp(gtot - gc_col)
    v_new = u - _bmm(w, s)
    attn = jnp.where(incl, _bmm_nt(q, k) * decay, 0.0)
    o = _bmm(q * e_col, s) + _bmm(attn, v_new)
    s_next = s * jnp.exp(gtot) + _bmm_tn(k_dec, v_new)
    return o, s_next, tinv


N_CHAINS = 2 * GDN_HEADS
QKV_W = 3 * GDN_HEADS * HEAD_DIM


def _scan_specs(nc, step):
    c = GDN_CHUNK
    seq = [pl.BlockSpec((c, QKV_W), lambda i: (step(i), 0)), pl.BlockSpec((c, QKV_W), lambda i: (nc - 1 - step(i), 0))]
    heads = [pl.BlockSpec((c, GDN_HEADS * HEAD_DIM), lambda i: (step(i), 0)),
             pl.BlockSpec((c, GDN_HEADS * HEAD_DIM), lambda i: (nc - 1 - step(i), 0))]
    colv = pl.BlockSpec((N_CHAINS, None, c, 1), lambda i: (0, step(i), 0, 0))
    rowv = pl.BlockSpec((N_CHAINS, None, 1, c), lambda i: (0, step(i), 0, 0))
    state = pl.BlockSpec((N_CHAINS, HEAD_DIM, HEAD_DIM), lambda i: (0, 0, 0))
    saved = pl.BlockSpec((N_CHAINS, None, HEAD_DIM, HEAD_DIM), lambda i: (0, step(i), 0, 0))
    inv = pl.BlockSpec((N_CHAINS, None, c, c), lambda i: (0, step(i), 0, 0))
    return seq, heads, colv, rowv, state, saved, inv


def _head_cols(h, part):
    lo = (part * GDN_HEADS + h) * HEAD_DIM
    return slice(lo, lo + HEAD_DIM)


def _chain_stack(x0_ref, x1_ref, part):
    return jnp.stack([(x0_ref if n < GDN_HEADS else x1_ref)[:, _head_cols(n % GDN_HEADS, part)] for n in range(N_CHAINS)])


def _scan_fwd(name, qkv, bcol, gcol, grow, s0):
    t = qkv.shape[0]
    nc = t // GDN_CHUNK
    seq, heads, colv, rowv, state, saved, inv = _scan_specs(nc, lambda i: i)

    def body(qkv0_ref, qkv1_ref, b_ref, gc_ref, gr_ref, s0_ref, o0_ref, o1_ref, save_ref, tinv_ref, fin_ref, s_ref):
        @pl.when(pl.program_id(0) == 0)
        def _():
            s_ref[...] = s0_ref[...]

        s = s_ref[...]
        save_ref[...] = s
        q, k, v = [_chain_stack(qkv0_ref, qkv1_ref, part) for part in range(3)]
        o, s_next, tinv = _gdn_chunk(q, k, v, gc_ref[...], gr_ref[...], b_ref[...], s)
        for n in range(N_CHAINS):
            d, h = divmod(n, GDN_HEADS)
            (o0_ref if d == 0 else o1_ref)[:, _head_cols(h, 0)] = o[n]
        tinv_ref[...] = tinv
        s_ref[...] = s_next
        fin_ref[...] = s_next

    hw = GDN_HEADS * HEAD_DIM
    return pl.pallas_call(
        body, name=name, grid=(nc,), in_specs=seq + [colv, colv, rowv, state],
        out_specs=heads + [saved, inv, state],
        out_shape=[_sds((t, hw)), _sds((t, hw)), _sds((N_CHAINS, nc, HEAD_DIM, HEAD_DIM)),
                   _sds((N_CHAINS, nc, GDN_CHUNK, GDN_CHUNK)), _sds((N_CHAINS, HEAD_DIM, HEAD_DIM))],
        scratch_shapes=[pltpu.VMEM((N_CHAINS, HEAD_DIM, HEAD_DIM), F32)],
        compiler_params=pltpu.CompilerParams(dimension_semantics=("arbitrary",), vmem_limit_bytes=VMEM_LIMIT_BYTES),
    )(qkv, qkv, bcol, gcol, grow, s0)


def _scan_bwd(name, do, qkv, bcol, gcol, grow, saved_s, saved_inv, ds_last):
    t = qkv.shape[0]
    nc = t // GDN_CHUNK
    c = GDN_CHUNK
    seq, heads, colv, rowv, state, saved, inv = _scan_specs(nc, lambda i: nc - 1 - i)

    def body(do0_ref, do1_ref, qkv0_ref, qkv1_ref, b_ref, gc_ref, gr_ref, s_ref, tinv_ref, dsl_ref,
             dx0_ref, dx1_ref, db_ref, dgc_ref, dgr_ref, ds0_ref, ds_ref):
        @pl.when(pl.program_id(0) == 0)
        def _():
            ds_ref[...] = dsl_ref[...]

        tinv = tinv_ref[...]
        q, k, v = [_chain_stack(qkv0_ref, qkv1_ref, part) for part in range(3)]
        do = jnp.stack([(do0_ref if n < GDN_HEADS else do1_ref)[:, _head_cols(n % GDN_HEADS, 0)] for n in range(N_CHAINS)])
        _, vjp = jax.vjp(lambda *a: _gdn_chunk(*a, tinv=tinv)[:2], q, k, v, gc_ref[...], gr_ref[...], b_ref[...], s_ref[...])
        dq, dk, dv, dgc, dgr, db, ds = vjp((do, ds_ref[...]))
        for n in range(N_CHAINS):
            d, h = divmod(n, GDN_HEADS)
            dx_ref = dx0_ref if d == 0 else dx1_ref
            dx_ref[:, _head_cols(h, 0)] = dq[n]
            dx_ref[:, _head_cols(h, 1)] = dk[n]
            dx_ref[:, _head_cols(h, 2)] = dv[n]
        db_ref[...] = db
        dgc_ref[...] = dgc
        dgr_ref[...] = dgr
        ds_ref[...] = ds
        ds0_ref[...] = ds

    vec_c = _sds((N_CHAINS, nc, c, 1))
    vec_r = _sds((N_CHAINS, nc, 1, c))
    return pl.pallas_call(
        body, name=name, grid=(nc,),
        in_specs=heads + seq + [colv, colv, rowv, saved, inv, state],
        out_specs=seq + [colv, colv, rowv, state],
        out_shape=[_sds((t, QKV_W)), _sds((t, QKV_W)), vec_c, vec_c, vec_r, _sds((N_CHAINS, HEAD_DIM, HEAD_DIM))],
        scratch_shapes=[pltpu.VMEM((N_CHAINS, HEAD_DIM, HEAD_DIM), F32)],
        compiler_params=pltpu.CompilerParams(dimension_semantics=("arbitrary",), vmem_limit_bytes=VMEM_LIMIT_BYTES),
    )(do, do, qkv, qkv, bcol, gcol, grow, saved_s, saved_inv, ds_last)


def _gate_layouts(gb):
    t = gb.shape[0]
    nc = t // GDN_CHUNK
    nh = GDN_HEADS

    def by_scan_position(a):
        a = a.T.reshape(2 * nh, nc, GDN_CHUNK)
        return jnp.concatenate([a[:nh], a[nh:, ::-1]], axis=0)

    beta = by_scan_position(gb[:, :2 * nh])
    g = by_scan_position(gb[:, 2 * nh:4 * nh])
    return beta[..., None], g[..., None], g[:, :, None, :]


def _gate_layouts_bwd(dbcol, dgcol, dgrow):
    n2, nc, c, _ = dbcol.shape
    nh = n2 // 2
    t = nc * c

    def by_token(a):
        return jnp.concatenate([a[:nh], a[nh:, ::-1]], axis=0).reshape(n2, t).T

    dbeta = by_token(dbcol[..., 0])
    dg = by_token(dgcol[..., 0] + dgrow[:, :, 0, :])
    return jnp.concatenate([dbeta, dg, jnp.zeros((t, LANES - 2 * n2), F32)], axis=1)


def kernel(x, c, ctx, c_ctx, ada_w, ada_b, ln_g, ln_b, even_w_in, even_w_out, gdn_conv_w, gdn_a_log, gdn_dt_bias, gdn_norm_w, pool_w, pool_scale, odd_w_in, odd_w_out, sconv_w, conf_conv_w, conf_ln_g, conf_ln_b, ffn_w_up, ffn_conv_w, ffn_w_down, loss_target, m_c_ctx, m_ada_w, m_ada_b, m_ln_g, m_ln_b, m_even_w_in, m_even_w_out, m_gdn_conv_w, m_gdn_a_log, m_gdn_dt_bias, m_gdn_norm_w, m_pool_w, m_pool_scale, m_odd_w_in, m_odd_w_out, m_sconv_w, m_conf_conv_w, m_conf_ln_g, m_conf_ln_b, m_ffn_w_up, m_ffn_conv_w, m_ffn_w_down, v_c_ctx, v_ada_w, v_ada_b, v_ln_g, v_ln_b, v_even_w_in, v_even_w_out, v_gdn_conv_w, v_gdn_a_log, v_gdn_dt_bias, v_gdn_norm_w, v_pool_w, v_pool_scale, v_odd_w_in, v_odd_w_out, v_sconv_w, v_conf_conv_w, v_conf_ln_g, v_conf_ln_b, v_ffn_w_up, v_ffn_conv_w, v_ffn_w_down):
    names = ['c_ctx', 'ada_w', 'ada_b', 'ln_g', 'ln_b', 'even_w_in', 'even_w_out', 'gdn_conv_w', 'gdn_a_log',
             'gdn_dt_bias', 'gdn_norm_w', 'pool_w', 'pool_scale', 'odd_w_in', 'odd_w_out', 'sconv_w', 'conf_conv_w',
             'conf_ln_g', 'conf_ln_b', 'ffn_w_up', 'ffn_conv_w', 'ffn_w_down']
    loc = locals()
    wts = {n: loc[n] for n in names}
    mom = {n: loc['m_' + n] for n in names}
    var = {n: loc['v_' + n] for n in names}

    ix, iy, ic = lax.axis_index("x"), lax.axis_index("y"), lax.axis_index("c")
    chip = 2 * ix + iy
    dev = 2 * chip + ic
    d = D_MODEL
    x0, ctx0, tgt = x[0], ctx[0], loss_target[0]
    t, tc = x0.shape[0], ctx0.shape[0]
    depth = ada_w.shape[0]
    n_ada = ada_w.shape[2]

    small_sharded = [gdn_conv_w, sconv_w, conf_conv_w, ffn_conv_w, ln_g, ln_b]
    s1_items = [c] + small_sharded
    s1 = _pack_flat(s1_items, 8 * LANES, F32).reshape(-1, LANES)
    g1 = _all_gather8("gather_small_in", s1, True).reshape(8, -1)
    c_all = g1[:, :d]
    per_chip = [_unpack_flat(g1[2 * k], [a.shape for a in s1_items])[1:] for k in range(4)]
    gdn_conv_f, sconv_f, conf_conv_f, ffn_conv_f, ln_g_f, ln_b_f = [
        jnp.concatenate([per_chip[k][i] for k in range(4)], axis=-1) for i in range(len(small_sharded))]
    c16 = jnp.concatenate([c_all, c_ctx[None], jnp.zeros((7, d), F32)], axis=0)

    mod_part = _mod_rows(c16, ada_w)
    g2 = _all_gather8("gather_mod", mod_part.reshape(-1, LANES), True).reshape(4, 2, depth, 16, n_ada)[:, 0]
    mod_all = jnp.transpose(g2, (1, 2, 0, 3)).reshape(depth, 16, 4 * n_ada) + ada_b[:, None, :]
    mod_me = lax.dynamic_index_in_dim(mod_all, dev, axis=1, keepdims=False).reshape(depth, 6, 1, d)
    sh_c, sc_c = mod_all[0, 8, :d][None], mod_all[0, 8, d:2 * d][None]

    flat_names = ['even_w_in', 'even_w_out']
    flat_shapes = [wts[n].shape for n in flat_names]
    half_mult = 2 * 16 * LANES
    rh = -(-sum(math.prod(s) for s in flat_shapes) // half_mult) * half_mult // (2 * LANES)
    flat_pack = _pack_flat([wts[n] for n in flat_names], half_mult, MXU_DTYPE).reshape(2, rh, LANES)
    wg_flat = _all_gather8("gather_weights_even", lax.dynamic_index_in_dim(flat_pack, ic, axis=0, keepdims=False), True).reshape(4, -1)
    per_chip = [_unpack_flat(wg_flat[k], flat_shapes) for k in range(4)]
    n_even = 4 * even_w_in.shape[1]
    n_even_pad = -(-n_even // LANES) * LANES
    w_in = jnp.concatenate([per_chip[k][0] for k in range(4)] + [jnp.zeros((d, n_even_pad - n_even), MXU_DTYPE)], axis=1)
    w_out = jnp.concatenate([per_chip[k][1] for k in range(4)], axis=0)

    def row_half(a, core, axis):
        n = a.shape[axis] // 2
        return lax.dynamic_slice_in_dim(a, core * n, n, axis=axis)

    def layer_of(a, core):
        return lax.dynamic_index_in_dim(a, core, axis=0, keepdims=False)

    mine = [row_half(odd_w_in, ic, 0), row_half(odd_w_out, ic, 0), layer_of(ffn_w_up, ic), layer_of(ffn_w_down, ic)]
    g_oin, g_oout, g_up, g_down = [g.reshape((4, 2) + g.shape[1:]) for g in
                                   _all_gather8_multi("gather_weights", [m.astype(MXU_DTYPE) for m in mine])]
    w_oin = g_oin.reshape(4, d, -1)
    w_oout = g_oout.reshape(-1, d)
    w_up = [(g_up, pl.BlockSpec((4, None) + g_up.shape[2:], lambda t, l=l: (0, l, 0, 0))) for l in range(depth)]
    w_down = [(g_down, pl.BlockSpec((4, None) + g_down.shape[2:], lambda t, l=l: (0, l, 0, 0))) for l in range(depth)]
    scal_blk = (n_even // LANES)
    n_scal = n_even - scal_blk * LANES

    def mod(layer, k):
        return mod_me[layer, k]

    avec = jnp.zeros((1, LANES), F32).at[0, n_scal // 2:n_scal].set(gdn_a_log.reshape(-1))
    dtvec = jnp.zeros((1, LANES), F32).at[0, n_scal // 2:n_scal].set(gdn_dt_bias.reshape(-1))
    normw = gdn_norm_w[None]
    pscale = pool_scale[None]
    cg, cb = conf_ln_g[None], conf_ln_b[None]
    lng = lambda l, k: ln_g_f[l, k][None]
    lnb = lambda l, k: ln_b_f[l, k][None]
    convw9 = ffn_conv_f.reshape(depth, 9, -1)
    nqkv = gdn_conv_f.shape[1]

    p, ub0 = _inproj("even_in", x0, mod(0, 0), mod(0, 1), w_in, 512)
    pc, ucb = _inproj("even_in_ctx", ctx0, sh_c, sc_c, w_in, 256)
    qkv = _qkv_conv("qkv_conv", p, gdn_conv_f)
    qkv_c = _qkv_conv("qkv_conv_ctx", pc, gdn_conv_f)
    gb = _gates("gates", p, avec, dtvec, scal_blk)
    gb_c = _gates("gates_ctx", pc, avec, dtvec, scal_blk)
    lay = _gate_layouts(gb)
    lay_c = _gate_layouts(gb_c)
    s_zero = jnp.zeros((2 * GDN_HEADS, HEAD_DIM, HEAD_DIM), F32)
    _, _, save_c, inv_c, s_ctx = _scan_fwd("scan_ctx", qkv_c, *lay_c, s_zero)
    o0, o1, save_l, inv_l, _ = _scan_fwd("scan", qkv, *lay, s_ctx)
    pool_blk = (nqkv + GDN_HEADS * HEAD_DIM) // 512
    ypool = _pool(p, pool_w, pscale, pool_blk)
    x1, mix0, y0 = _even_out(o0, o1, p, ypool, x0, normw, mod(0, 2), lng(0, 0), lnb(0, 0), w_out, 256)

    def ffn_fwd(l, xin):
        h, ub = _inproj(f"ffn_up{l}", xin, mod(l, 3), mod(l, 4), w_up[l], 256)
        s = _ffn_conv(f"ffn_conv{l}", h, convw9[l])
        xo, y = _ffn_down(f"ffn_down{l}", s, xin, mod(l, 5), lng(l, 1), lnb(l, 1), w_down[l], 256)
        return xo, (h, ub, s, y)

    x2, ffn0 = ffn_fwd(0, x1)
    p1, ub1 = _inproj("odd_in", x2, mod(1, 0), mod(1, 1), w_oin, 512)
    nsc = sconv_f.shape[1] // LANES
    ysc = _sconv(p1, sconv_f)
    z0 = _confconv(p1, conf_conv_f, 3 * nsc)
    x3, mix1, y1 = _odd_out(ysc, z0, x2, cg, cb, mod(1, 2), lng(1, 0), lnb(1, 0), w_oout, 256)
    x4, ffn1 = ffn_fwd(1, x3)
    loss_part, dx4 = _loss_and_grad(x4, tgt, 512)
    loss = lax.psum(loss_part[0, 0], ("x", "y", "c"))

    dmod = [[None] * 6 for _ in range(depth)]
    dlng = [[None, None] for _ in range(depth)]
    dlnb = [[None, None] for _ in range(depth)]
    gbig = {}
    dconv9 = [None] * depth

    def ffn_bwd(l, dxo, xin, saved):
        h, ub, s, y = saved
        ds, dxa, dyb, dgt, dg_, db_ = _ffn_down_bwd(f"ffn_down_bwd{l}", dxo, xin, y, mod(l, 5), lng(l, 1), lnb(l, 1), w_down[l], 256)
        da, dgate, dw9 = _ffn_conv_bwd(f"ffn_conv_bwd{l}", ds, h, convw9[l])
        dh = jnp.concatenate([da, dgate], axis=1)
        dxin, dsh, dsc = _inproj_bwd(f"ffn_up_bwd{l}", dh, xin, mod(l, 3), mod(l, 4), w_up[l], dxa, 256)
        dmod[l][3], dmod[l][4], dmod[l][5] = dsh, dsc, dgt
        dlng[l][1], dlnb[l][1] = dg_, db_
        dconv9[l] = dw9
        dw_up = _matmul_tn(f"dw_up{l}", [(ub, dh)], chip_cols=ffn_w_up.shape[2])
        dw_down = _matmul_tn(f"dw_down{l}", [(s, dyb)]).reshape(4, -1, d)
        return dxin, dw_up, dw_down

    dx3, dwu1, dwd1 = ffn_bwd(1, dx4, x3, ffn1)
    dysc, dz0, dx2a, dyb1, dcg, dcb, dgt, dg_, db_ = _odd_out_bwd(dx3, ysc, z0, x2, y1, cg, cb, mod(1, 2), lng(1, 0), lnb(1, 0), w_oout, 256)
    dmod[1][2], dlng[1][0], dlnb[1][0] = dgt, dg_, db_
    d_gb, d_gc, d_h, dsconv = _sconv_bwd(dysc, p1, sconv_f)
    d_ga, d_gbb, dconf = _confconv_bwd(dz0, p1, conf_conv_f, 3 * nsc)
    dp1 = jnp.concatenate([d_gb, d_gc, d_h, d_ga, d_gbb], axis=1)
    dx2, dsh, dsc = _inproj_bwd("odd_in_bwd", dp1, x2, mod(1, 0), mod(1, 1), w_oin, dx2a, 512)
    dmod[1][0], dmod[1][1] = dsh, dsc
    dw_oout = _matmul_tn("dw_oout", [(mix1, dyb1)]).reshape(4, -1, d)
    dw_oin = _matmul_tn("dw_oin", [(ub1, dp1)], chip_cols=odd_w_in.shape[1])

    dx1, dwu0, dwd0 = ffn_bwd(0, dx2, x1, ffn0)

    do, dpg, dypool, dx0a, dyb0, dnormw, dgt, dg_, db_ = _even_out_bwd(dx1, o0, o1, p, ypool, x0, y0, normw, mod(0, 2), lng(0, 0), lnb(0, 0), w_out, 256)
    dmod[0][2], dlng[0][0], dlnb[0][0] = dgt, dg_, db_
    pool_cts = _pool_bwd(dypool, p, pool_w, pscale, pool_blk)
    dpp, dpool_scale, dpool_w = pool_cts[0], pool_cts[1], jnp.stack(pool_cts[2:])
    dqkv0, dqkv1, dbcol, dgcol, dgrow, ds0 = _scan_bwd("scan_bwd", do, qkv, *lay, save_l, inv_l, s_zero)
    zero_do = jnp.zeros((tc, GDN_HEADS * HEAD_DIM), F32)
    dqkv0_c, dqkv1_c, dbcol_c, dgcol_c, dgrow_c, _ = _scan_bwd("scan_bwd_ctx", zero_do, qkv_c, *lay_c, save_c, inv_c, ds0)
    dgb = _gate_layouts_bwd(dbcol, dgcol, dgrow)
    dgb_c = _gate_layouts_bwd(dbcol_c, dgcol_c, dgrow_c)
    dps, davec, ddtvec = _gates_bwd("gates_bwd", dgb, p, avec, dtvec, scal_blk)
    dps_c, davec_c, ddtvec_c = _gates_bwd("gates_bwd_ctx", dgb_c, pc, avec, dtvec, scal_blk)
    dpqkv, dconv5 = _qkv_conv_bwd("qkv_conv_bwd", dqkv0, dqkv1, p, gdn_conv_f)
    dpqkv_c, dconv5_c = _qkv_conv_bwd("qkv_conv_bwd_ctx", dqkv0_c, dqkv1_c, pc, gdn_conv_f)
    dp = jnp.concatenate([dpqkv, dpg, dpp, dps], axis=1)
    dpc = jnp.concatenate([dpqkv_c, jnp.zeros((tc, n_even_pad - nqkv - LANES), MXU_DTYPE), dps_c], axis=1)
    grad_x, dsh, dsc = _inproj_bwd("even_in_bwd", dp, x0, mod(0, 0), mod(0, 1), w_in, dx0a, 512)
    dmod[0][0], dmod[0][1] = dsh, dsc
    _, dsh_c, dsc_c = _inproj_bwd("even_in_bwd_ctx", dpc, ctx0, sh_c, sc_c, w_in, None, 256)
    dw_in = _matmul_tn("dw_in", [(ub0, dp), (ucb, dpc)])
    dw_out = _matmul_tn("dw_out", [(mix0, dyb0)])

    n_in, n_out = even_w_in.shape[1], even_w_out.shape[0]
    gflat = jnp.stack([_pack_flat([dw_in[:, k * n_in:(k + 1) * n_in], dw_out[k * n_out:(k + 1) * n_out]], half_mult,
                                  MXU_DTYPE).reshape(2, rh, LANES) for k in range(4)], axis=1)
    halves = [
        (gflat[0], gflat[1]),
        (dw_oin[:, :d // 2], dw_oin[:, d // 2:]),
        (dw_oout[:, :odd_w_out.shape[0] // 2], dw_oout[:, odd_w_out.shape[0] // 2:]),
        (dwu0, dwu1),
        (dwd0, dwd1),
    ]
    keeps = [jnp.where(ic == 0, h0, h1) for h0, h1 in halves]
    gives = [jnp.where(ic == 0, h1, h0) for h0, h1 in halves]
    gots = _sibling_exchange("grad_pair_exchange", gives)
    pairs = [_sum_pair(f"grad_pair_sum{i}", kp, gt_, MXU_DTYPE) for i, (kp, gt_) in enumerate(zip(keeps, gots))]
    parts = _chip_scatter("grad_chip_scatter", pairs)
    sums = [_sum_axis1(f"grad_chip_sum{i}", p_) for i, p_ in enumerate(parts)]
    g_even_in, g_even_out = _unpack_flat(sums[0].reshape(-1), flat_shapes)
    g_shards = {'even_w_in': g_even_in, 'even_w_out': g_even_out, 'odd_w_in': sums[1].reshape(odd_w_in.shape),
                'odd_w_out': sums[2].reshape(odd_w_out.shape), 'ffn_w_up': sums[3], 'ffn_w_down': sums[4]}

    dmod_rows = jnp.stack([jnp.concatenate(dmod[l], axis=1)[0] for l in range(depth)])
    dmod_c = jnp.concatenate([dsh_c[0], dsc_c[0], jnp.zeros((4 * d,), F32)])
    dmod_c_rows = jnp.stack([dmod_c] + [jnp.zeros_like(dmod_c)] * (depth - 1))
    small_g = {
        'ln_g': jnp.stack([jnp.stack([dlng[l][k][0] for k in range(2)]) for l in range(depth)]),
        'ln_b': jnp.stack([jnp.stack([dlnb[l][k][0] for k in range(2)]) for l in range(depth)]),
        'gdn_conv_w': dconv5 + dconv5_c,
        'gdn_a_log': (davec + davec_c)[0, n_scal // 2:n_scal].reshape(gdn_a_log.shape),
        'gdn_dt_bias': (ddtvec + ddtvec_c)[0, n_scal // 2:n_scal].reshape(gdn_dt_bias.shape),
        'gdn_norm_w': dnormw[0], 'pool_w': dpool_w, 'pool_scale': dpool_scale[0],
        'sconv_w': dsconv, 'conf_conv_w': dconf, 'conf_ln_g': dcg[0], 'conf_ln_b': dcb[0],
        'ffn_conv_w': jnp.stack(dconv9).reshape(depth, 3, 3, -1),
    }
    small_names = list(small_g)
    s3_items = [dmod_rows, dmod_c_rows] + [small_g[n] for n in small_names]
    s3 = _pack_flat(s3_items, 8 * LANES, F32).reshape(-1, LANES)
    g3 = _all_gather8("gather_small_grads", s3, True).reshape(8, -1, LANES)
    tot3 = _sum_axis1("small_grad_sum", g3[None]).reshape(-1)
    tot_items = _unpack_flat(tot3, [a.shape for a in s3_items])
    dmod_sum, dmod_c_sum = tot_items[0], tot_items[1]
    small_tot = dict(zip(small_names, tot_items[2:]))
    grad_ada_b = dmod_sum + dmod_c_sum
    g3f = g3.reshape(8, -1)
    rows_all = g3f[:, :depth * 6 * d].reshape(8, depth, 6 * d)
    cols = lax.dynamic_slice_in_dim(rows_all, chip * n_ada, n_ada, axis=2)
    crow = lax.dynamic_slice_in_dim(dmod_c_sum, chip * n_ada, n_ada, axis=1)
    dmod16 = jnp.concatenate([jnp.transpose(cols, (1, 0, 2)), crow[:, None, :], jnp.zeros((depth, 7, n_ada), F32)], axis=1)
    grad_ada_w, dsil = _ada_grads(c16, dmod16, ada_w)
    s4 = jnp.concatenate([dsil[0, 8][None], jnp.zeros((7, d), F32)], axis=0).reshape(-1, LANES)
    g4 = _all_gather8("gather_cctx", s4, True).reshape(8, 8, d)
    grad_c_ctx = _cctx_grad(g4[0::2, 0][:, None, :], c_ctx[None])[0]

    def my_cols(a, n):
        return lax.dynamic_slice_in_dim(a, chip * n, n, axis=a.ndim - 1)

    grads = dict(g_shards)
    grads['c_ctx'] = grad_c_ctx
    grads['ada_w'] = grad_ada_w
    grads['ada_b'] = grad_ada_b
    for n in ['ln_g', 'ln_b', 'gdn_conv_w', 'sconv_w', 'conf_conv_w', 'ffn_conv_w']:
        grads[n] = my_cols(small_tot[n], wts[n].shape[-1])
    for n in ['gdn_a_log', 'gdn_dt_bias', 'gdn_norm_w', 'pool_w', 'pool_scale', 'conf_ln_g', 'conf_ln_b']:
        grads[n] = small_tot[n]

    delta, new_m, new_v = {}, {}, {}
    big_adam = list(g_shards) + ['ada_w']
    for n in big_adam:
        shp = wts[n].shape
        as2d = lambda a: a.reshape(-1, shp[-1])
        dl, nm, nv = _adamw("adamw_" + n, as2d(wts[n]), as2d(grads[n]), as2d(mom[n]), as2d(var[n]))
        delta[n], new_m[n], new_v[n] = dl.reshape(shp), nm.reshape(shp), nv.reshape(shp)
    small_adam = [n for n in names if n not in big_adam]
    packs = [_pack_flat([src[n] for n in small_adam], 8 * LANES, F32).reshape(-1, LANES) for src in (wts, grads, mom, var)]
    outs = _adamw("adamw_small", *packs)
    shapes = [wts[n].shape for n in small_adam]
    for res, dst in zip(outs, (delta, new_m, new_v)):
        for n, a in zip(small_adam, _unpack_flat(res.reshape(-1), shapes)):
            dst[n] = a

    return (loss, grad_x[None], *[grads[n] for n in names], *[delta[n] for n in names],
            *[new_m[n] for n in names], *[new_v[n] for n in names])
```

```python
import functools
import math

import jax
import jax.numpy as jnp
from jax import lax
from jax.experimental import pallas as pl
from jax.experimental.pallas import tpu as pltpu

F32 = jnp.float32
MXU_DTYPE = jnp.bfloat16
HIGHEST = lax.Precision.HIGHEST
MESH = pl.DeviceIdType.MESH

D_MODEL = 1024
GRID_W = 64
GDN_HEADS = 4
HEAD_DIM = 128
GDN_CHUNK = 64
POOL_WINDOWS = (2, 4, 8, 16)
GDN_CONV = 5
SC_WIDTH = 3
CF_WIDTH = 31
ALPHA = 4.0 ** 0.25
LN_EPS = 1e-5
RMS_EPS = 1e-6
LANES = 128
VMEM_LIMIT_BYTES = 58 * 1024 * 1024

ADAM_LR, ADAM_B1, ADAM_B2, ADAM_EPS, ADAM_WD, ADAM_STEP = 0.001, 0.9, 0.999, 1e-08, 0.01, 10


def _blocked(name, fn, grid, ins, in_specs, out_shapes, out_specs, acc=(), acc_axis=None, scratch=()):
    n_in = len(ins)
    n_out = len(out_shapes)

    def body(*refs):
        vals = [r[...] for r in refs[:n_in]]
        res = fn(*vals, *refs[n_in + n_out:])
        if not isinstance(res, (tuple, list)):
            res = (res,)
        for k, (r, v) in enumerate(zip(refs[n_in:n_in + n_out], res)):
            if k in acc:
                first = pl.program_id(acc_axis) == 0

                @pl.when(first)
                def _(r=r, v=v):
                    r[...] = v.astype(r.dtype)

                @pl.when(jnp.logical_not(first))
                def _(r=r, v=v):
                    r[...] += v.astype(r.dtype)
            else:
                r[...] = v.astype(r.dtype)

    return pl.pallas_call(
        body, name=name, grid=grid, in_specs=in_specs, out_specs=out_specs, out_shape=out_shapes,
        scratch_shapes=list(scratch),
        compiler_params=pltpu.CompilerParams(dimension_semantics=("arbitrary",) * len(grid),
                                             vmem_limit_bytes=VMEM_LIMIT_BYTES),
    )(*ins)


def _sds(shape, dtype=F32):
    return jax.ShapeDtypeStruct(tuple(shape), dtype)


def _tok(tm, n, col=0):
    return pl.BlockSpec((tm, n), lambda t: (t, col))


def _res(shape):
    nd = len(shape)
    return pl.BlockSpec(tuple(shape), lambda t: (0,) * nd)


def _silu(x):
    return x * jax.nn.sigmoid(x)


def _layernorm(r, g, b):
    mu = jnp.mean(r, -1, keepdims=True)
    d = r - mu
    var = jnp.mean(d * d, -1, keepdims=True)
    return d * lax.rsqrt(var + LN_EPS) * g + b


def _mm_nn_impl(a, b):
    return jnp.dot(a.astype(MXU_DTYPE), b.astype(MXU_DTYPE), preferred_element_type=F32)


def _mm_nt_impl(a, b):
    return lax.dot_general(a.astype(MXU_DTYPE), b.astype(MXU_DTYPE), (((1,), (1,)), ((), ())), preferred_element_type=F32)


def _mm_tn_impl(a, b):
    return lax.dot_general(a.astype(MXU_DTYPE), b.astype(MXU_DTYPE), (((0,), (0,)), ((), ())), preferred_element_type=F32)


def _mm_vjp(mm, mm_da, mm_db):
    f = jax.custom_vjp(mm)
    f.defvjp(lambda a, b: (mm(a, b), (a, b)), lambda res, g: (mm_da(g, res[1]), mm_db(res[0], g)))
    return f


_mm = _mm_vjp(_mm_nn_impl, lambda g, b: _mm_nt_impl(g, b), lambda a, g: _mm_tn_impl(a, g))
_mm_nt = _mm_vjp(_mm_nt_impl, lambda g, b: _mm_nn_impl(g, b), lambda a, g: _mm_tn_impl(g, a))
_mm_tn = _mm_vjp(_mm_tn_impl, lambda g, b: _mm_nt_impl(b, g), lambda a, g: _mm_nn_impl(a, g))


def _row(w, k):
    rows = lax.broadcasted_iota(jnp.int32, w.shape, 0)
    return jnp.sum(jnp.where(rows == k, w, 0.0), axis=0, keepdims=True)


def _col_mask(shape, dc):
    col = lax.broadcasted_iota(jnp.int32, shape, 0) & (GRID_W - 1)
    return (col + dc >= 0) & (col + dc < GRID_W)


def _center(x_ext, halo):
    return x_ext[halo:x_ext.shape[0] - halo]


def _make_dwconv(taps, halo):
    assert all(abs(s) <= halo for s, _ in taps)

    def shifted(x_ext, s):
        r = x_ext if s == 0 else pltpu.roll(x_ext, (-s) % x_ext.shape[0], 0)
        return _center(r, halo)

    @jax.custom_vjp
    def conv(x_ext, w):
        acc = None
        for k, (s, dc) in enumerate(taps):
            r = shifted(x_ext, s)
            if dc != 0:
                r = jnp.where(_col_mask(r.shape, dc), r, 0.0)
            term = r * _row(w, k)
            acc = term if acc is None else acc + term
        return acc

    def fwd(x_ext, w):
        return conv(x_ext, w), (x_ext, w)

    def bwd(res, dy):
        x_ext, w = res
        n = x_ext.shape[0]
        rows = lax.broadcasted_iota(jnp.int32, w.shape, 0)
        pad = jnp.zeros((halo, dy.shape[1]), F32)
        dx = None
        dw = jnp.zeros(w.shape, F32)
        for k, (s, dc) in enumerate(taps):
            dym = dy if dc == 0 else jnp.where(_col_mask(dy.shape, dc), dy, 0.0)
            dw = dw + jnp.where(rows == k, jnp.sum(dym * shifted(x_ext, s), axis=0, keepdims=True), 0.0)
            t = jnp.concatenate([pad, dym * _row(w, k), pad], axis=0)
            if s != 0:
                t = pltpu.roll(t, s % n, 0)
            dx = t if dx is None else dx + t
        return dx, dw

    conv.defvjp(fwd, bwd)
    return conv


def _taps_1d(width):
    return tuple((k - width // 2, 0) for k in range(width))


HALO_SHORT = 8
HALO_CONF = 16
HALO_GRID = 72
_conv5 = _make_dwconv(_taps_1d(GDN_CONV), HALO_SHORT)
_conv3 = _make_dwconv(_taps_1d(SC_WIDTH), HALO_SHORT)
_conv31 = _make_dwconv(_taps_1d(CF_WIDTH), HALO_CONF)
_conv3x3 = _make_dwconv(tuple((dr * GRID_W + dc, dc) for dr in (-1, 0, 1) for dc in (-1, 0, 1)), HALO_GRID)
_pool_sums = {w: _make_dwconv(tuple((s, 0) for s in range(-(w // 2), w - w // 2)), HALO_SHORT) for w in POOL_WINDOWS}


def _all_gather8(name, blk, in_vmem):
    m_per, n = blk.shape
    space = pltpu.VMEM if in_vmem else pl.ANY

    def body(x_ref, out_ref, send_sems, recv_sems, local_sem):
        x, y, c = lax.axis_index("x"), lax.axis_index("y"), lax.axis_index("c")
        me, sibling = (x, y, c), (x, y, 1 - c)
        chips = [(1 - x, y), (x, 1 - y), (1 - x, 1 - y)]

        def rows(px, py, pc):
            return out_ref.at[pl.ds((4 * px + 2 * py + pc) * m_per, m_per), :]

        def copy(k, block, to, src=None):
            return pltpu.make_async_remote_copy(
                src_ref=rows(*block) if src is None else src, dst_ref=rows(*block),
                send_sem=send_sems.at[k], recv_sem=recv_sems.at[k], device_id=to, device_id_type=MESH)

        mine = pltpu.make_async_copy(x_ref, rows(*me), local_sem)
        mine.start()
        first = [copy(0, me, sibling, src=x_ref)]
        first += [copy(1 + j, me, (*chip, c), src=x_ref) for j, chip in enumerate(chips)]
        for cp in first:
            cp.start()
        passed = [copy(4 + j, (*chip, c), sibling) for j, chip in enumerate(chips)]
        for j, chip in enumerate(chips):
            copy(1 + j, (*chip, c), me).wait_recv()
            passed[j].start()
        copy(0, sibling, me).wait_recv()
        for j, chip in enumerate(chips):
            copy(4 + j, (*chip, 1 - c), me).wait_recv()
        for cp in first + passed:
            cp.wait_send()
        mine.wait()

    return pl.pallas_call(
        body, name=name, out_shape=_sds((8 * m_per, n), blk.dtype),
        in_specs=[pl.BlockSpec(memory_space=space)], out_specs=pl.BlockSpec(memory_space=space),
        scratch_shapes=[pltpu.SemaphoreType.DMA((7,)), pltpu.SemaphoreType.DMA((7,)), pltpu.SemaphoreType.DMA],
        compiler_params=pltpu.CompilerParams(vmem_limit_bytes=VMEM_LIMIT_BYTES),
    )(blk)


_ANY = pl.BlockSpec(memory_space=pl.ANY)


def _all_gather8_multi(name, blocks):
    n = len(blocks)

    def body(*refs):
        x_refs, out_refs = refs[:n], refs[n:2 * n]
        send_sems, recv_sems, local_sems = refs[2 * n:]
        x, y, c = lax.axis_index("x"), lax.axis_index("y"), lax.axis_index("c")
        me, sibling = (x, y, c), (x, y, 1 - c)
        chips = [(1 - x, y), (x, 1 - y), (1 - x, 1 - y)]

        def slot(i, px, py, pc):
            return out_refs[i].at[4 * px + 2 * py + pc]

        def copy(i, k, block, to, src=None):
            return pltpu.make_async_remote_copy(
                src_ref=slot(i, *block) if src is None else src, dst_ref=slot(i, *block),
                send_sem=send_sems.at[i, k], recv_sem=recv_sems.at[i, k], device_id=to, device_id_type=MESH)

        mine = [pltpu.make_async_copy(x_refs[i], slot(i, *me), local_sems.at[i]) for i in range(n)]
        first = []
        for i in range(n):
            first.append(copy(i, 0, me, sibling, src=x_refs[i]))
            first += [copy(i, 1 + j, me, (*chip, c), src=x_refs[i]) for j, chip in enumerate(chips)]
        for cp in mine + first:
            cp.start()
        passed = []
        for j, chip in enumerate(chips):
            for i in range(n):
                copy(i, 1 + j, (*chip, c), me).wait_recv()
                passed.append(copy(i, 4 + j, (*chip, c), sibling))
                passed[-1].start()
        for i in range(n):
            copy(i, 0, sibling, me).wait_recv()
            for j, chip in enumerate(chips):
                copy(i, 4 + j, (*chip, 1 - c), me).wait_recv()
        for cp in first + passed:
            cp.wait_send()
        for cp in mine:
            cp.wait()

    return pl.pallas_call(
        body, name=name, out_shape=[_sds((8,) + b.shape, b.dtype) for b in blocks],
        in_specs=[_ANY] * n, out_specs=[_ANY] * n,
        scratch_shapes=[pltpu.SemaphoreType.DMA((n, 7)), pltpu.SemaphoreType.DMA((n, 7)), pltpu.SemaphoreType.DMA((n,))],
    )(*blocks)


def _sibling_exchange(name, sends):
    n = len(sends)

    def body(*refs):
        x, y, c = lax.axis_index("x"), lax.axis_index("y"), lax.axis_index("c")
        send_sems, recv_sems = refs[2 * n:]
        copies = [pltpu.make_async_remote_copy(src_ref=refs[i], dst_ref=refs[n + i], send_sem=send_sems.at[i],
                                               recv_sem=recv_sems.at[i], device_id=(x, y, 1 - c), device_id_type=MESH)
                  for i in range(n)]
        for cp in copies:
            cp.start()
        for cp in copies:
            cp.wait()

    return pl.pallas_call(
        body, name=name, out_shape=[_sds(s.shape, s.dtype) for s in sends],
        in_specs=[_ANY] * n, out_specs=[_ANY] * n,
        scratch_shapes=[pltpu.SemaphoreType.DMA((n,)), pltpu.SemaphoreType.DMA((n,))],
    )(*sends)


def _chip_scatter(name, parts):
    n = len(parts)

    def body(*refs):
        p_refs, out_refs = refs[:n], refs[n:2 * n]
        send_sems, recv_sems, local_sems = refs[2 * n:]
        x, y, c = lax.axis_index("x"), lax.axis_index("y"), lax.axis_index("c")
        sibling = (x, y, 1 - c)
        me_chip = 2 * x + y
        chips = [(1 - x, y), (x, 1 - y), (1 - x, 1 - y)]

        def chip_id(chip):
            return 2 * chip[0] + chip[1]

        def copy(i, k, src, dst, to):
            return pltpu.make_async_remote_copy(src_ref=src, dst_ref=dst, send_sem=send_sems.at[i, k],
                                                recv_sem=recv_sems.at[i, k], device_id=to, device_id_type=MESH)

        mine = [pltpu.make_async_copy(p_refs[i].at[me_chip], out_refs[i].at[c, me_chip], local_sems.at[i]) for i in range(n)]
        first = []
        for i in range(n):
            first.append(copy(i, 0, p_refs[i].at[me_chip], out_refs[i].at[c, me_chip], sibling))
            first += [copy(i, 1 + j, p_refs[i].at[chip_id(chip)], out_refs[i].at[c, me_chip], (*chip, c))
                      for j, chip in enumerate(chips)]
        for cp in mine + first:
            cp.start()
        passed = []
        for j, chip in enumerate(chips):
            for i in range(n):
                landed = out_refs[i].at[c, chip_id(chip)]
                copy(i, 1 + j, p_refs[i].at[0], landed, sibling).wait_recv()
                passed.append(copy(i, 4 + j, landed, landed, sibling))
                passed[-1].start()
        for i in range(n):
            copy(i, 0, p_refs[i].at[0], out_refs[i].at[1 - c, me_chip], sibling).wait_recv()
            for j, chip in enumerate(chips):
                copy(i, 4 + j, p_refs[i].at[0], out_refs[i].at[1 - c, chip_id(chip)], sibling).wait_recv()
        for cp in first + passed:
            cp.wait_send()
        for cp in mine:
            cp.wait()

    return pl.pallas_call(
        body, name=name, out_shape=[_sds((2,) + p.shape, p.dtype) for p in parts],
        in_specs=[_ANY] * n, out_specs=[_ANY] * n,
        scratch_shapes=[pltpu.SemaphoreType.DMA((n, 7)), pltpu.SemaphoreType.DMA((n, 7)), pltpu.SemaphoreType.DMA((n,))],
    )(*parts)


def _pack_flat(arrays, multiple, dtype):
    flat = jnp.concatenate([a.reshape(-1).astype(dtype) for a in arrays])
    pad = (-flat.shape[0]) % multiple
    if pad:
        flat = jnp.concatenate([flat, jnp.zeros((pad,), dtype)])
    return flat


def _unpack_flat(flat, shapes):
    out, off = [], 0
    for s in shapes:
        n = math.prod(s)
        out.append(flat[off:off + n].reshape(s))
        off += n
    return out


def _row_tile(r, cap, mult=16):
    for cand in range(min(cap, r) // mult * mult, 0, -mult):
        if r % cand == 0:
            return cand
    return r


ELEMENTWISE_BLOCK_ELEMS = 1 << 20


def _sum_axis1(name, x, out_dtype=F32):
    h, k, r, n = x.shape
    tr = _row_tile(r, max(16, ELEMENTWISE_BLOCK_ELEMS // (k * n)))

    def fn(s):
        acc = s[0].astype(F32)
        for i in range(1, k):
            acc = acc + s[i].astype(F32)
        return acc

    return _blocked(name, fn, (h, r // tr), [x], [pl.BlockSpec((None, k, tr, n), lambda i, t: (i, 0, t, 0))],
                    [_sds((h, r, n), out_dtype)], [pl.BlockSpec((None, tr, n), lambda i, t: (i, t, 0))])[0]


def _sum_pair(name, a, b, out_dtype):
    lead, r, n = a.shape
    tr = _row_tile(r, max(16, ELEMENTWISE_BLOCK_ELEMS // n))
    spec = pl.BlockSpec((None, tr, n), lambda i, t: (i, t, 0))
    return _blocked(name, lambda x, y: x.astype(F32) + y.astype(F32), (lead, r // tr), [a, b], [spec, spec],
                    [_sds((lead, r, n), out_dtype)], [spec])[0]


def _adamw(name, w, g, m, v):
    r, c = w.shape
    tr = _row_tile(r, max(8, (1 << 21) // (4 * c)), 8)
    bc1 = 1.0 - ADAM_B1 ** ADAM_STEP
    bc2 = 1.0 - ADAM_B2 ** ADAM_STEP

    def fn(w, g, m, v):
        nm = ADAM_B1 * m + (1.0 - ADAM_B1) * g
        nv = ADAM_B2 * v + (1.0 - ADAM_B2) * (g * g)
        delta = -ADAM_LR * ((nm / bc1) / (jnp.sqrt(nv / bc2) + ADAM_EPS) + ADAM_WD * w)
        return delta, nm, nv

    spec = pl.BlockSpec((tr, c), lambda t: (t, 0))
    return _blocked(name, fn, (r // tr,), [w, g, m, v], [spec] * 4, [_sds((r, c))] * 3, [spec] * 3)


def _mod_rows(c16, ada_w):
    depth, d, n = ada_w.shape

    def fn(c, w):
        return _mm(_silu(c), w)

    return _blocked("mod_rows", fn, (depth,), [c16, ada_w],
                    [_res(c16.shape), pl.BlockSpec((None, d, n), lambda l: (l, 0, 0))],
                    [_sds((depth, 16, n))], [pl.BlockSpec((None, 16, n), lambda l: (l, 0, 0))])[0]


def _ada_grads(c16, dmod16, ada_w):
    depth, d, n = ada_w.shape

    def fn(c, dm, w):
        return _mm_tn(_silu(c), dm), _mm_nt(dm, w)

    return _blocked("ada_grads", fn, (depth,), [c16, dmod16, ada_w],
                    [_res(c16.shape), pl.BlockSpec((None, 16, n), lambda l: (l, 0, 0)),
                     pl.BlockSpec((None, d, n), lambda l: (l, 0, 0))],
                    [_sds((depth, d, n)), _sds((depth, 16, d))],
                    [pl.BlockSpec((None, d, n), lambda l: (l, 0, 0)), pl.BlockSpec((None, 16, d), lambda l: (l, 0, 0))])


def _cctx_grad(parts, c_ctx_row):
    def fn(p, c):
        tot = ((p[0] + p[1]) + p[2]) + p[3]
        _, vjp = jax.vjp(_silu, c)
        return vjp(tot)[0]

    return _blocked("cctx_grad", fn, (1,), [parts, c_ctx_row], [_res(parts.shape), _res(c_ctx_row.shape)],
                    [_sds(c_ctx_row.shape)], [_res(c_ctx_row.shape)])[0]


def _modulate(x, sh, sc):
    return x * (1.0 + sc) + sh


def _weight_operand(w):
    if isinstance(w, tuple):
        return w[0], w[1], tuple(b for b in w[1].block_shape if b is not None)
    return w, _res(w.shape), w.shape


def _matmul_cols(u, w):
    if w.ndim == 2:
        return jnp.dot(u, w, preferred_element_type=F32)
    return jnp.concatenate([jnp.dot(u, w[j], preferred_element_type=F32) for j in range(w.shape[0])], axis=1)


def _matmul_cols_t(dp, w):
    if w.ndim == 2:
        return _mm_nt(dp, w)
    nj = w.shape[2]
    return functools.reduce(lambda a, b: a + b, [_mm_nt(dp[:, j * nj:(j + 1) * nj], w[j]) for j in range(w.shape[0])])


def _rows_weight(w):
    return w if w.ndim == 2 else w.reshape(w.shape[0] * w.shape[1], w.shape[2])


def _inproj(name, x, sh, sc, w, tm):
    t, d = x.shape
    w, w_spec, w_blk = _weight_operand(w)
    n = w_blk[-1] * (w_blk[0] if len(w_blk) == 3 else 1)
    tm = min(tm, t)

    def fn(x, sh, sc, w):
        u = _modulate(x, sh, sc).astype(MXU_DTYPE)
        return _matmul_cols(u, w), u

    return _blocked(name, fn, (t // tm,), [x, sh, sc, w],
                    [_tok(tm, d), _res(sh.shape), _res(sc.shape), w_spec],
                    [_sds((t, n)), _sds((t, d), MXU_DTYPE)], [_tok(tm, n), _tok(tm, d)])


def _inproj_bwd(name, dp, x, sh, sc, w, add, tm):
    t, d = x.shape
    w, w_spec, w_blk = _weight_operand(w)
    n = w_blk[-1] * (w_blk[0] if len(w_blk) == 3 else 1)
    tm = min(tm, t)
    has_add = add is not None

    def fn(dp, x, sh, sc, w, *rest):
        du = _matmul_cols_t(dp, w)
        _, vjp = jax.vjp(_modulate, x, sh, sc)
        dx, dsh, dsc = vjp(du)
        if has_add:
            dx = dx + rest[0]
        return dx, dsh, dsc

    ins = [dp, x, sh, sc, w] + ([add] if has_add else [])
    specs = [_tok(tm, n), _tok(tm, d), _res(sh.shape), _res(sc.shape), w_spec] + ([_tok(tm, d)] if has_add else [])
    return _blocked(name, fn, (t // tm,), ins, specs,
                    [_sds((t, d)), _sds(sh.shape), _sds(sc.shape)], [_tok(tm, d), _res(sh.shape), _res(sc.shape)],
                    acc=(1, 2), acc_axis=0)


def _residual_ln(y, x, gt, lng, lnb):
    return _layernorm(ALPHA * x + gt * y, lng, lnb)


def _gdn_mix(o0, o1, pg, yp, normw):
    o = o0 + o1
    heads = []
    for h in range(GDN_HEADS):
        oh = o[:, h * HEAD_DIM:(h + 1) * HEAD_DIM]
        heads.append(oh * lax.rsqrt(jnp.mean(oh * oh, -1, keepdims=True) + RMS_EPS) * normw)
    on = jnp.concatenate(heads, axis=-1) * _silu(pg)
    return jnp.concatenate([on, yp], axis=-1)


def _even_out(o0, o1, p, ypool, x, normw, gt, lng, lnb, w, tm):
    t, d = x.shape
    tm = min(tm, t)
    gate_blk = 3

    def fn(o0, o1, pg, yp, x, normw, gt, lng, lnb, w):
        mix = _gdn_mix(o0, o1, pg, yp, normw).astype(MXU_DTYPE)
        y = jnp.dot(mix, w, preferred_element_type=F32)
        return _residual_ln(y, x, gt, lng, lnb), mix, y

    return _blocked("even_out", fn, (t // tm,), [o0, o1, p, ypool, x, normw, gt, lng, lnb, w],
                    [_tok(tm, 512), _tok(tm, 512), _tok(tm, 512, gate_blk), _tok(tm, 512), _tok(tm, d),
                     _res(normw.shape), _res(gt.shape), _res(lng.shape), _res(lnb.shape), _res(w.shape)],
                    [_sds((t, d)), _sds((t, d), MXU_DTYPE), _sds((t, d))], [_tok(tm, d)] * 3)


def _even_out_bwd(dx1, o0, o1, p, ypool, x, y, normw, gt, lng, lnb, w, tm):
    t, d = x.shape
    tm = min(tm, t)

    def fn(dx1, o0, o1, pg, yp, x, y, normw, gt, lng, lnb, w):
        _, vjp2 = jax.vjp(_residual_ln, y, x, gt, lng, lnb)
        dy, dx, dgt, dlng, dlnb = vjp2(dx1)
        dyb = dy.astype(MXU_DTYPE)
        dmix = _mm_nt(dyb, w)
        _, vjp1 = jax.vjp(_gdn_mix, o0, o1, pg, yp, normw)
        do, _, dpg, dyp, dnormw = vjp1(dmix)
        return do, dpg, dyp, dx, dyb, dnormw, dgt, dlng, dlnb

    return _blocked("even_out_bwd", fn, (t // tm,), [dx1, o0, o1, p, ypool, x, y, normw, gt, lng, lnb, w],
                    [_tok(tm, d), _tok(tm, 512), _tok(tm, 512), _tok(tm, 512, 3), _tok(tm, 512),
                     _tok(tm, d), _tok(tm, d), _res(normw.shape), _res(gt.shape), _res(lng.shape), _res(lnb.shape),
                     _res(w.shape)],
                    [_sds((t, 512)), _sds((t, 512), MXU_DTYPE), _sds((t, 512)), _sds((t, d)), _sds((t, d), MXU_DTYPE),
                     _sds(normw.shape), _sds(gt.shape), _sds(lng.shape), _sds(lnb.shape)],
                    [_tok(tm, 512), _tok(tm, 512), _tok(tm, 512), _tok(tm, d), _tok(tm, d),
                     _res(normw.shape), _res(gt.shape), _res(lng.shape), _res(lnb.shape)],
                    acc=(5, 6, 7, 8), acc_axis=0)


def _odd_mix(ysc, z0, cg, cb):
    z = _silu(_layernorm(z0, cg, cb))
    return jnp.concatenate([ysc, z], axis=-1)


def _odd_out(ysc, z0, x, cg, cb, gt, lng, lnb, w, tm):
    t, d = x.shape
    tm = min(tm, t)

    def fn(ysc, z0, x, cg, cb, gt, lng, lnb, w):
        mix = _odd_mix(ysc, z0, cg, cb).astype(MXU_DTYPE)
        y = jnp.dot(mix, w, preferred_element_type=F32)
        return _residual_ln(y, x, gt, lng, lnb), mix, y

    return _blocked("odd_out", fn, (t // tm,), [ysc, z0, x, cg, cb, gt, lng, lnb, w],
                    [_tok(tm, 512), _tok(tm, 512), _tok(tm, d), _res(cg.shape), _res(cb.shape), _res(gt.shape),
                     _res(lng.shape), _res(lnb.shape), _res(w.shape)],
                    [_sds((t, d)), _sds((t, d), MXU_DTYPE), _sds((t, d))], [_tok(tm, d)] * 3)


def _odd_out_bwd(dx3, ysc, z0, x, y, cg, cb, gt, lng, lnb, w, tm):
    t, d = x.shape
    tm = min(tm, t)

    def fn(dx3, ysc, z0, x, y, cg, cb, gt, lng, lnb, w):
        _, vjp2 = jax.vjp(_residual_ln, y, x, gt, lng, lnb)
        dy, dx, dgt, dlng, dlnb = vjp2(dx3)
        dyb = dy.astype(MXU_DTYPE)
        dmix = _mm_nt(dyb, w)
        _, vjp1 = jax.vjp(_odd_mix, ysc, z0, cg, cb)
        dysc, dz0, dcg, dcb = vjp1(dmix)
        return dysc, dz0, dx, dyb, dcg, dcb, dgt, dlng, dlnb

    return _blocked("odd_out_bwd", fn, (t // tm,), [dx3, ysc, z0, x, y, cg, cb, gt, lng, lnb, w],
                    [_tok(tm, d), _tok(tm, 512), _tok(tm, 512), _tok(tm, d), _tok(tm, d), _res(cg.shape), _res(cb.shape),
                     _res(gt.shape), _res(lng.shape), _res(lnb.shape), _res(w.shape)],
                    [_sds((t, 512)), _sds((t, 512)), _sds((t, d)), _sds((t, d), MXU_DTYPE),
                     _sds(cg.shape), _sds(cb.shape), _sds(gt.shape), _sds(lng.shape), _sds(lnb.shape)],
                    [_tok(tm, 512), _tok(tm, 512), _tok(tm, d), _tok(tm, d),
                     _res(cg.shape), _res(cb.shape), _res(gt.shape), _res(lng.shape), _res(lnb.shape)],
                    acc=(4, 5, 6, 7, 8), acc_axis=0)


def _ffn_down(name, s, x, gt, lng, lnb, w, tm):
    t, d = x.shape
    f = s.shape[1]
    w, w_spec, _ = _weight_operand(w)
    tm = min(tm, t)

    def fn(s, x, gt, lng, lnb, w):
        y = jnp.dot(s, _rows_weight(w), preferred_element_type=F32)
        return _residual_ln(y, x, gt, lng, lnb), y

    return _blocked(name, fn, (t // tm,), [s, x, gt, lng, lnb, w],
                    [_tok(tm, f), _tok(tm, d), _res(gt.shape), _res(lng.shape), _res(lnb.shape), w_spec],
                    [_sds((t, d)), _sds((t, d))], [_tok(tm, d)] * 2)


def _ffn_down_bwd(name, dx2, x, y, gt, lng, lnb, w, tm):
    t, d = x.shape
    w, w_spec, w_blk = _weight_operand(w)
    f = math.prod(w_blk[:-1])
    tm = min(tm, t)

    def fn(dx2, x, y, gt, lng, lnb, w):
        _, vjp2 = jax.vjp(_residual_ln, y, x, gt, lng, lnb)
        dy, dx, dgt, dlng, dlnb = vjp2(dx2)
        dyb = dy.astype(MXU_DTYPE)
        return _mm_nt(dyb, _rows_weight(w)), dx, dyb, dgt, dlng, dlnb

    return _blocked(name, fn, (t // tm,), [dx2, x, y, gt, lng, lnb, w],
                    [_tok(tm, d), _tok(tm, d), _tok(tm, d), _res(gt.shape), _res(lng.shape), _res(lnb.shape), w_spec],
                    [_sds((t, f), MXU_DTYPE), _sds((t, d)), _sds((t, d), MXU_DTYPE), _sds(gt.shape), _sds(lng.shape), _sds(lnb.shape)],
                    [_tok(tm, f), _tok(tm, d), _tok(tm, d), _res(gt.shape), _res(lng.shape), _res(lnb.shape)],
                    acc=(3, 4, 5), acc_axis=0)


def _loss_and_grad(x4, target, tm):
    t, d = x4.shape
    tm = min(tm, t)

    def fn(y, tg):
        e = y - tg
        part = 0.5 * jnp.sum(jnp.mean(e * e, axis=-1, keepdims=True), axis=0, keepdims=True)
        return jnp.broadcast_to(part, (1, LANES)), e * (1.0 / d)

    return _blocked("loss", fn, (t // tm,), [x4, target], [_tok(tm, d), _tok(tm, d)],
                    [_sds((1, LANES)), _sds((t, d))], [_res((1, LANES)), _tok(tm, d)], acc=(0,), acc_axis=0)


def _matmul_tn(name, pairs, chip_cols=None):
    k = pairs[0][0].shape[1]
    n = pairs[0][1].shape[1]
    tk = k if k <= 1024 else k // 2
    if chip_cols is None:
        tn = 512 if n % 512 == 0 else (384 if n % 384 == 0 else 128)
        out_spec, out_shape = pl.BlockSpec((tk, tn), lambda i, j: (i, j)), _sds((k, n), MXU_DTYPE)
    else:
        tn = chip_cols
        tk = min(tk, 512) if tn > 1024 else tk
        out_spec = pl.BlockSpec((None, tk, tn), lambda i, j: (j, i, 0))
        out_shape = _sds((n // chip_cols, k, chip_cols), MXU_DTYPE)

    def body(*refs):
        acc = None
        for p in range(len(pairs)):
            term = lax.dot_general(refs[2 * p][...], refs[2 * p + 1][...], (((0,), (0,)), ((), ())), preferred_element_type=F32)
            acc = term if acc is None else acc + term
        refs[-1][...] = acc.astype(refs[-1].dtype)

    in_specs, args = [], []
    for a, b in pairs:
        t = a.shape[0]
        in_specs += [pl.BlockSpec((t, tk), lambda i, j: (0, i)), pl.BlockSpec((t, tn), lambda i, j: (0, j))]
        args += [a, b]
    return pl.pallas_call(
        body, name=name, grid=(k // tk, n // tn), in_specs=in_specs,
        out_specs=out_spec, out_shape=out_shape,
        compiler_params=pltpu.CompilerParams(dimension_semantics=("arbitrary", "arbitrary"),
                                             vmem_limit_bytes=VMEM_LIMIT_BYTES),
    )(*args)


def _chan(t, col_of):
    return pl.BlockSpec((t, LANES), lambda i: (0, col_of(i)))


def _wblk(k, col_of=lambda i: i):
    return pl.BlockSpec((k, LANES), lambda i: (0, col_of(i)))


SEQ_TILE = 256
GRID_SEQ_TILE = 512


def _load_ext(ref, r0, rows, halo, total):
    lo, hi = max(r0 - halo, 0), min(r0 + rows + halo, total)
    parts = []
    if lo > r0 - halo:
        parts.append(jnp.zeros((lo - (r0 - halo), ref.shape[1]), F32))
    parts.append(ref[lo:hi, :].astype(F32))
    if hi < r0 + rows + halo:
        parts.append(jnp.zeros((r0 + rows + halo - hi, ref.shape[1]), F32))
    return parts[0] if len(parts) == 1 else jnp.concatenate(parts, axis=0)


def _seq_stage(name, tile_fn, halo, grid_n, seqs, pars, outs, tile_rows=SEQ_TILE):
    total = seqs[0][0].shape[0]
    rows = min(tile_rows, total)
    ns, npar = len(seqs), len(pars)

    def body(*refs):
        seq_refs, par_refs, out_refs = refs[:ns], refs[ns:ns + npar], refs[ns + npar:]
        par_vals = [r[...] for r in par_refs]
        for r0 in range(0, total, rows):
            exts = [_load_ext(r, r0, rows, halo, total) for r in seq_refs]
            res = tile_fn(r0, *exts, *par_vals)
            for o_ref, v in zip(out_refs, res):
                o_ref[r0:r0 + rows, :] = v.astype(o_ref.dtype)

    return pl.pallas_call(
        body, name=name, grid=(grid_n,), in_specs=[s for _, s in seqs] + [s for _, s in pars],
        out_specs=[s for _, s in outs], out_shape=[o for o, _ in outs],
        compiler_params=pltpu.CompilerParams(dimension_semantics=("arbitrary",), vmem_limit_bytes=VMEM_LIMIT_BYTES),
    )(*[a for a, _ in seqs], *[a for a, _ in pars])


def _seq_stage_bwd(name, tile_fn, halo, grid_n, douts, seqs, pars, dseq_outs, dpar_outs, tile_rows=SEQ_TILE):
    total = seqs[0][0].shape[0]
    rows = min(tile_rows, total)
    groups = [d if isinstance(d, list) else [d] for d in douts]
    douts = [d for g in groups for d in g]
    starts = [sum(len(g) for g in groups[:k]) for k in range(len(groups))]
    nd, ns, npar = len(douts), len(seqs), len(pars)
    widths = [s.block_shape[-1] for _, s in seqs]

    def dtile(dout_refs, k, r0):
        terms = [d[r0:r0 + rows, :].astype(F32) for d in dout_refs[starts[k]:starts[k] + len(groups[k])]]
        return functools.reduce(lambda a, b: a + b, terms)

    def body(*refs):
        dout_refs, seq_refs, par_refs = refs[:nd], refs[nd:nd + ns], refs[nd + ns:nd + ns + npar]
        dseq_refs = refs[nd + ns + npar:nd + 2 * ns + npar]
        dpar_refs = refs[nd + 2 * ns + npar:nd + 2 * ns + 2 * npar]
        accs = refs[nd + 2 * ns + 2 * npar:]
        par_vals = [r[...] for r in par_refs]
        for a in accs:
            for r0 in range(0, total, rows):
                a[r0:r0 + rows, :] = jnp.zeros((rows, a.shape[1]), F32)
        dpars = [jnp.zeros(p.shape, F32) for p in par_vals]
        for r0 in range(0, total, rows):
            exts = [_load_ext(r, r0, rows, halo, total) for r in seq_refs]
            _, vjp = jax.vjp(functools.partial(tile_fn, r0), *exts, *par_vals)
            cts = vjp(tuple(dtile(dout_refs, k, r0) for k in range(len(groups))))
            lo, hi = max(r0 - halo, 0), min(r0 + rows + halo, total)
            for a, dx in zip(accs, cts[:ns]):
                a[lo:hi, :] += dx[lo - (r0 - halo):hi - (r0 - halo)]
            dpars = [acc + g for acc, g in zip(dpars, cts[ns:])]
        for o_ref, a in zip(dseq_refs, accs):
            for r0 in range(0, total, rows):
                o_ref[r0:r0 + rows, :] = a[r0:r0 + rows, :].astype(o_ref.dtype)
        for o_ref, g in zip(dpar_refs, dpars):
            o_ref[...] = g

    return pl.pallas_call(
        body, name=name, grid=(grid_n,),
        in_specs=[s for _, s in douts] + [s for _, s in seqs] + [s for _, s in pars],
        out_specs=[s for _, s in dseq_outs] + [s for _, s in dpar_outs],
        out_shape=[o for o, _ in dseq_outs] + [o for o, _ in dpar_outs],
        scratch_shapes=[pltpu.VMEM((total, w), F32) for w in widths],
        compiler_params=pltpu.CompilerParams(dimension_semantics=("arbitrary",), vmem_limit_bytes=VMEM_LIMIT_BYTES),
    )(*[a for a, _ in douts], *[a for a, _ in seqs], *[a for a, _ in pars])


def _qkv_tile(r0, p_ext, w):
    kind = pl.program_id(0)
    a = _silu(_conv5(p_ext, w))
    nrm = a * lax.rsqrt(jnp.sum(a * a, -1, keepdims=True) + RMS_EPS)
    scale = jnp.where(kind < GDN_HEADS, HEAD_DIM ** -0.5, 1.0).astype(F32)
    sel = (kind < 2 * GDN_HEADS).astype(F32)
    return (sel * (nrm * scale) + (1.0 - sel) * a,)


def _qkv_conv(name, p, w):
    t = p.shape[0]
    nb = w.shape[1] // LANES
    ident = lambda i: i
    return _seq_stage(name, _qkv_tile, HALO_SHORT, nb, [(p, _chan(t, ident))], [(w, _wblk(GDN_CONV))],
                      [(_sds((t, nb * LANES)), _chan(t, ident))])[0]


def _qkv_conv_bwd(name, dqkv0, dqkv1, p, w):
    t = p.shape[0]
    nb = w.shape[1] // LANES
    ident = lambda i: i
    return _seq_stage_bwd(name, _qkv_tile, HALO_SHORT, nb, [[(dqkv0, _chan(t, ident)), (dqkv1, _chan(t, ident))]],
                          [(p, _chan(t, ident))],
                          [(w, _wblk(GDN_CONV))], [(_sds((t, nb * LANES), MXU_DTYPE), _chan(t, ident))],
                          [(_sds(w.shape), _wblk(GDN_CONV))])


def _gates_fn(s, avec, dtvec):
    lane = lax.broadcasted_iota(jnp.int32, s.shape, 1)
    beta = jax.nn.sigmoid(s)
    g = -jnp.exp(avec) * jax.nn.softplus(s + dtvec)
    nh = 2 * GDN_HEADS
    return jnp.where(lane < nh, beta, jnp.where(lane < 2 * nh, g, 0.0))


def _gates(name, p, avec, dtvec, col):
    t = p.shape[0]
    tm = min(512, t)
    return _blocked(name, _gates_fn, (t // tm,), [p, avec, dtvec], [_tok(tm, LANES, col), _res(avec.shape), _res(dtvec.shape)],
                    [_sds((t, LANES))], [_tok(tm, LANES)])[0]


def _gates_bwd(name, dgb, p, avec, dtvec, col):
    t = p.shape[0]
    tm = min(512, t)

    def fn(dgb, s, avec, dtvec):
        _, vjp = jax.vjp(_gates_fn, s, avec, dtvec)
        return vjp(dgb)

    return _blocked(name, fn, (t // tm,), [dgb, p, avec, dtvec],
                    [_tok(tm, LANES), _tok(tm, LANES, col), _res(avec.shape), _res(dtvec.shape)],
                    [_sds((t, LANES), MXU_DTYPE), _sds(avec.shape), _sds(dtvec.shape)],
                    [_tok(tm, LANES), _res(avec.shape), _res(dtvec.shape)], acc=(1, 2), acc_axis=0)


def _make_pool_tile(total):
    def tile(r0, x_ext, scale, *ws):
        ys = []
        for g, win in enumerate(POOL_WINDOWS):
            xg = x_ext[:, g * LANES:(g + 1) * LANES]
            rows = xg.shape[0] - 2 * HALO_SHORT
            pos = r0 + lax.broadcasted_iota(jnp.int32, (rows, LANES), 0)
            lo = jnp.clip(pos - win // 2, 0, total)
            hi = jnp.clip(pos - win // 2 + win, 0, total)
            window_sum = _pool_sums[win](xg, jnp.ones((win, LANES), F32))
            pooled = window_sum / (hi - lo).astype(F32) - _center(xg, HALO_SHORT)
            ys.append(_mm(pooled, ws[g]))
        return (jnp.concatenate(ys, axis=-1) * scale,)

    return tile


def _pool_specs(p, pool_w, pool_scale, col0):
    t = p.shape[0]
    n = len(POOL_WINDOWS) * LANES
    seq = (p, pl.BlockSpec((t, n), lambda i: (0, col0)))
    pars = [(pool_scale, _res(pool_scale.shape))]
    pars += [(pool_w, pl.BlockSpec((None, LANES, LANES), lambda i, g=g: (g, 0, 0))) for g in range(len(POOL_WINDOWS))]
    return t, n, seq, pars


def _pool(p, pool_w, pool_scale, col0):
    t, n, seq, pars = _pool_specs(p, pool_w, pool_scale, col0)
    return _seq_stage("pool", _make_pool_tile(t), HALO_SHORT, 1, [seq], pars, [(_sds((t, n)), _res((t, n)))])[0]


def _pool_bwd(dy, p, pool_w, pool_scale, col0):
    t, n, seq, pars = _pool_specs(p, pool_w, pool_scale, col0)
    wspec = (_sds((LANES, LANES)), _res((LANES, LANES)))
    return _seq_stage_bwd("pool_bwd", _make_pool_tile(t), HALO_SHORT, 1, [(dy, _res(dy.shape))], [seq], pars,
                          [(_sds((t, n), MXU_DTYPE), _res((t, n)))],
                          [(_sds(pool_scale.shape), _res(pool_scale.shape))] + [wspec] * len(POOL_WINDOWS))


def _ffn_conv_tile(r0, a_ext, gt_ext, w):
    return (_silu(_conv3x3(a_ext, w)) * _center(gt_ext, HALO_GRID),)


def _ffn_conv(name, h, w9):
    t = h.shape[0]
    nb = w9.shape[1] // LANES
    ident = lambda i: i
    return _seq_stage(name, _ffn_conv_tile, HALO_GRID, nb, [(h, _chan(t, ident)), (h, _chan(t, lambda i: nb + i))],
                      [(w9, _wblk(9))], [(_sds((t, nb * LANES), MXU_DTYPE), _chan(t, ident))], GRID_SEQ_TILE)[0]


def _ffn_conv_bwd(name, ds, h, w9):
    t = h.shape[0]
    nb = w9.shape[1] // LANES
    ident = lambda i: i
    o = (_sds((t, nb * LANES), MXU_DTYPE), _chan(t, ident))
    return _seq_stage_bwd(name, _ffn_conv_tile, HALO_GRID, nb, [(ds, _chan(t, ident))],
                          [(h, _chan(t, ident)), (h, _chan(t, lambda i: nb + i))], [(w9, _wblk(9))],
                          [o, o], [(_sds(w9.shape), _wblk(9))], GRID_SEQ_TILE)


def _sconv_tile(r0, gb_ext, gc_ext, h_ext, w):
    return (_center(gb_ext, HALO_SHORT) * _conv3(gc_ext * h_ext, w),)


def _sconv_specs(p1, w3):
    t = p1.shape[0]
    nb = w3.shape[1] // LANES
    seqs = [(p1, _chan(t, lambda i: i)), (p1, _chan(t, lambda i: nb + i)), (p1, _chan(t, lambda i: 2 * nb + i))]
    return t, nb, seqs, [(w3, _wblk(SC_WIDTH))]


def _sconv(p1, w3):
    t, nb, seqs, pars = _sconv_specs(p1, w3)
    return _seq_stage("sconv", _sconv_tile, HALO_SHORT, nb, seqs, pars, [(_sds((t, nb * LANES)), _chan(t, lambda i: i))])[0]


def _sconv_bwd(dysc, p1, w3):
    t, nb, seqs, pars = _sconv_specs(p1, w3)
    o = (_sds((t, nb * LANES), MXU_DTYPE), _chan(t, lambda i: i))
    return _seq_stage_bwd("sconv_bwd", _sconv_tile, HALO_SHORT, nb, [(dysc, _chan(t, lambda i: i))], seqs, pars,
                          [o, o, o], [(_sds(w3.shape), _wblk(SC_WIDTH))])


def _conf_tile(r0, ga_ext, gb_ext, w):
    return (_conv31(ga_ext * jax.nn.sigmoid(gb_ext), w),)


def _conf_specs(p1, w31, blk0):
    t = p1.shape[0]
    nb = w31.shape[1] // LANES
    seqs = [(p1, _chan(t, lambda i: blk0 + i)), (p1, _chan(t, lambda i: blk0 + nb + i))]
    return t, nb, seqs, [(w31, _wblk(CF_WIDTH))]


def _confconv(p1, w31, blk0):
    t, nb, seqs, pars = _conf_specs(p1, w31, blk0)
    return _seq_stage("confconv", _conf_tile, HALO_CONF, nb, seqs, pars, [(_sds((t, nb * LANES)), _chan(t, lambda i: i))])[0]


def _confconv_bwd(dz0, p1, w31, blk0):
    t, nb, seqs, pars = _conf_specs(p1, w31, blk0)
    o = (_sds((t, nb * LANES), MXU_DTYPE), _chan(t, lambda i: i))
    return _seq_stage_bwd("confconv_bwd", _conf_tile, HALO_CONF, nb, [(dz0, _chan(t, lambda i: i))], seqs, pars,
                          [o, o], [(_sds(w31.shape), _wblk(CF_WIDTH))])


def _split_bf16(x):
    head = x.astype(jnp.bfloat16)
    return head, (x - head.astype(F32)).astype(jnp.bfloat16)


def _bdot(a, b, ca, cb, hi):
    dims = (((ca,), (cb,)), ((0,), (0,)))
    if not hi:
        return lax.dot_general(a.astype(MXU_DTYPE), b.astype(MXU_DTYPE), dims, preferred_element_type=F32)
    if MXU_DTYPE == F32:
        return lax.dot_general(a, b, dims, preferred_element_type=F32, precision=HIGHEST)
    (ah, al), (bh, bl) = _split_bf16(a), _split_bf16(b)
    dot = functools.partial(lax.dot_general, dimension_numbers=dims, preferred_element_type=F32)
    return dot(ah, bh) + (dot(ah, bl) + dot(al, bh))


def _make_bmm(hi):
    nn_i = lambda a, b: _bdot(a, b, 2, 1, hi)
    nt_i = lambda a, b: _bdot(a, b, 2, 2, hi)
    tn_i = lambda a, b: _bdot(a, b, 1, 1, hi)
    nn = _mm_vjp(nn_i, lambda g, b: nt_i(g, b), lambda a, g: tn_i(a, g))
    nt = _mm_vjp(nt_i, lambda g, b: nn_i(g, b), lambda a, g: tn_i(g, a))
    tn = _mm_vjp(tn_i, lambda g, b: nt_i(b, g), lambda a, g: nn_i(a, g))
    return nn, nt, tn


_bmm, _bmm_nt, _bmm_tn = _make_bmm(False)
_bmm_hi, _bmm_hi_nt, _bmm_hi_tn = _make_bmm(True)


def _unit_tri_inverse(a):
    c = a.shape[-1]
    eye = (lax.broadcasted_iota(jnp.int32, a.shape, 1) == lax.broadcasted_iota(jnp.int32, a.shape, 2)).astype(F32)
    levels = max(1, int(math.ceil(math.log2(c))) - 1)
    p = eye - a
    m = _bdot(a, a, 2, 1, True)
    for level in range(levels):
        p_next = p + _bdot(p, m, 2, 1, True)
        if level + 1 < levels:
            m = _bdot(m, m, 2, 1, True)
        p = p_next
    return p


@jax.custom_vjp
def _known_inverse(a, tinv):
    return tinv


def _known_inverse_fwd(a, tinv):
    return tinv, tinv


def _known_inverse_bwd(tinv, dt):
    da = -_bdot(_bdot(tinv, dt, 1, 1, True), tinv, 2, 2, True)
    return da, jnp.zeros_like(tinv)


_known_inverse.defvjp(_known_inverse_fwd, _known_inverse_bwd)


def _gdn_chunk(q, k, v, gcol, grow, bcol, s, tinv=None):
    n, c, _ = q.shape
    shape = (n, c, c)
    fwd_dir = lax.broadcasted_iota(jnp.int32, shape, 0) < n // 2
    i = lax.broadcasted_iota(jnp.int32, shape, 1)
    j = lax.broadcasted_iota(jnp.int32, shape, 2)
    order = jnp.where(fwd_dir, i - j, j - i)
    incl = order >= 0
    gc_col = jnp.sum(jnp.where(incl, grow, 0.0), axis=2, keepdims=True)
    gc_row = jnp.sum(jnp.where(order <= 0, gcol, 0.0), axis=1, keepdims=True)
    gtot = jnp.sum(grow, axis=2, keepdims=True)
    decay = jnp.exp(jnp.where(incl, gc_col - gc_row, -1e30))
    kb = k * bcol
    a = jnp.where(order > 0, _bmm_nt(kb, k) * decay, 0.0)
    tinv = _unit_tri_inverse(a) if tinv is None else _known_inverse(a, tinv)
    e_col = jnp.exp(gc_col)
    u = _bmm_hi(tinv, v * bcol)
    w = _bmm_hi(tinv, kb * e_col)
    k_dec = k * jnp.exp(gtot - gc_col)
    v_new = u - _bmm(w, s)
    attn = jnp.where(incl, _bmm_nt(q, k) * decay, 0.0)
    o = _bmm(q * e_col, s) + _bmm(attn, v_new)
    s_next = s * jnp.exp(gtot) + _bmm_tn(k_dec, v_new)
    return o, s_next, tinv


N_CHAINS = 2 * GDN_HEADS
QKV_W = 3 * GDN_HEADS * HEAD_DIM


def _scan_specs(nc, step):
    c = GDN_CHUNK
    seq = [pl.BlockSpec((c, QKV_W), lambda i: (step(i), 0)), pl.BlockSpec((c, QKV_W), lambda i: (nc - 1 - step(i), 0))]
    heads = [pl.BlockSpec((c, GDN_HEADS * HEAD_DIM), lambda i: (step(i), 0)),
             pl.BlockSpec((c, GDN_HEADS * HEAD_DIM), lambda i: (nc - 1 - step(i), 0))]
    colv = pl.BlockSpec((N_CHAINS, None, c, 1), lambda i: (0, step(i), 0, 0))
    rowv = pl.BlockSpec((N_CHAINS, None, 1, c), lambda i: (0, step(i), 0, 0))
    state = pl.BlockSpec((N_CHAINS, HEAD_DIM, HEAD_DIM), lambda i: (0, 0, 0))
    saved = pl.BlockSpec((N_CHAINS, None, HEAD_DIM, HEAD_DIM), lambda i: (0, step(i), 0, 0))
    inv = pl.BlockSpec((N_CHAINS, None, c, c), lambda i: (0, step(i), 0, 0))
    return seq, heads, colv, rowv, state, saved, inv


def _head_cols(h, part):
    lo = (part * GDN_HEADS + h) * HEAD_DIM
    return slice(lo, lo + HEAD_DIM)


def _chain_stack(x0_ref, x1_ref, part):
    return jnp.stack([(x0_ref if n < GDN_HEADS else x1_ref)[:, _head_cols(n % GDN_HEADS, part)] for n in range(N_CHAINS)])


def _scan_fwd(name, qkv, bcol, gcol, grow, s0):
    t = qkv.shape[0]
    nc = t // GDN_CHUNK
    seq, heads, colv, rowv, state, saved, inv = _scan_specs(nc, lambda i: i)

    def body(qkv0_ref, qkv1_ref, b_ref, gc_ref, gr_ref, s0_ref, o0_ref, o1_ref, save_ref, tinv_ref, fin_ref, s_ref):
        @pl.when(pl.program_id(0) == 0)
        def _():
            s_ref[...] = s0_ref[...]

        s = s_ref[...]
        save_ref[...] = s
        q, k, v = [_chain_stack(qkv0_ref, qkv1_ref, part) for part in range(3)]
        o, s_next, tinv = _gdn_chunk(q, k, v, gc_ref[...], gr_ref[...], b_ref[...], s)
        for n in range(N_CHAINS):
            d, h = divmod(n, GDN_HEADS)
            (o0_ref if d == 0 else o1_ref)[:, _head_cols(h, 0)] = o[n]
        tinv_ref[...] = tinv
        s_ref[...] = s_next
        fin_ref[...] = s_next

    hw = GDN_HEADS * HEAD_DIM
    return pl.pallas_call(
        body, name=name, grid=(nc,), in_specs=seq + [colv, colv, rowv, state],
        out_specs=heads + [saved, inv, state],
        out_shape=[_sds((t, hw)), _sds((t, hw)), _sds((N_CHAINS, nc, HEAD_DIM, HEAD_DIM)),
                   _sds((N_CHAINS, nc, GDN_CHUNK, GDN_CHUNK)), _sds((N_CHAINS, HEAD_DIM, HEAD_DIM))],
        scratch_shapes=[pltpu.VMEM((N_CHAINS, HEAD_DIM, HEAD_DIM), F32)],
        compiler_params=pltpu.CompilerParams(dimension_semantics=("arbitrary",), vmem_limit_bytes=VMEM_LIMIT_BYTES),
    )(qkv, qkv, bcol, gcol, grow, s0)


def _scan_bwd(name, do, qkv, bcol, gcol, grow, saved_s, saved_inv, ds_last):
    t = qkv.shape[0]
    nc = t // GDN_CHUNK
    c = GDN_CHUNK
    seq, heads, colv, rowv, state, saved, inv = _scan_specs(nc, lambda i: nc - 1 - i)

    def body(do0_ref, do1_ref, qkv0_ref, qkv1_ref, b_ref, gc_ref, gr_ref, s_ref, tinv_ref, dsl_ref,
             dx0_ref, dx1_ref, db_ref, dgc_ref, dgr_ref, ds0_ref, ds_ref):
        @pl.when(pl.program_id(0) == 0)
        def _():
            ds_ref[...] = dsl_ref[...]

        tinv = tinv_ref[...]
        q, k, v = [_chain_stack(qkv0_ref, qkv1_ref, part) for part in range(3)]
        do = jnp.stack([(do0_ref if n < GDN_HEADS else do1_ref)[:, _head_cols(n % GDN_HEADS, 0)] for n in range(N_CHAINS)])
        _, vjp = jax.vjp(lambda *a: _gdn_chunk(*a, tinv=tinv)[:2], q, k, v, gc_ref[...], gr_ref[...], b_ref[...], s_ref[...])
        dq, dk, dv, dgc, dgr, db, ds = vjp((do, ds_ref[...]))
        for n in range(N_CHAINS):
            d, h = divmod(n, GDN_HEADS)
            dx_ref = dx0_ref if d == 0 else dx1_ref
            dx_ref[:, _head_cols(h, 0)] = dq[n]
            dx_ref[:, _head_cols(h, 1)] = dk[n]
            dx_ref[:, _head_cols(h, 2)] = dv[n]
        db_ref[...] = db
        dgc_ref[...] = dgc
        dgr_ref[...] = dgr
        ds_ref[...] = ds
        ds0_ref[...] = ds

    vec_c = _sds((N_CHAINS, nc, c, 1))
    vec_r = _sds((N_CHAINS, nc, 1, c))
    return pl.pallas_call(
        body, name=name, grid=(nc,),
        in_specs=heads + seq + [colv, colv, rowv, saved, inv, state],
        out_specs=seq + [colv, colv, rowv, state],
        out_shape=[_sds((t, QKV_W)), _sds((t, QKV_W)), vec_c, vec_c, vec_r, _sds((N_CHAINS, HEAD_DIM, HEAD_DIM))],
        scratch_shapes=[pltpu.VMEM((N_CHAINS, HEAD_DIM, HEAD_DIM), F32)],
        compiler_params=pltpu.CompilerParams(dimension_semantics=("arbitrary",), vmem_limit_bytes=VMEM_LIMIT_BYTES),
    )(do, do, qkv, qkv, bcol, gcol, grow, saved_s, saved_inv, ds_last)


def _gate_layouts(gb):
    t = gb.shape[0]
    nc = t // GDN_CHUNK
    nh = GDN_HEADS

    def by_scan_position(a):
        a = a.T.reshape(2 * nh, nc, GDN_CHUNK)
        return jnp.concatenate([a[:nh], a[nh:, ::-1]], axis=0)

    beta = by_scan_position(gb[:, :2 * nh])
    g = by_scan_position(gb[:, 2 * nh:4 * nh])
    return beta[..., None], g[..., None], g[:, :, None, :]


def _gate_layouts_bwd(dbcol, dgcol, dgrow):
    n2, nc, c, _ = dbcol.shape
    nh = n2 // 2
    t = nc * c

    def by_token(a):
        return jnp.concatenate([a[:nh], a[nh:, ::-1]], axis=0).reshape(n2, t).T

    dbeta = by_token(dbcol[..., 0])
    dg = by_token(dgcol[..., 0] + dgrow[:, :, 0, :])
    return jnp.concatenate([dbeta, dg, jnp.zeros((t, LANES - 2 * n2), F32)], axis=1)


def kernel(x, c, ctx, c_ctx, ada_w, ada_b, ln_g, ln_b, even_w_in, even_w_out, gdn_conv_w, gdn_a_log, gdn_dt_bias, gdn_norm_w, pool_w, pool_scale, odd_w_in, odd_w_out, sconv_w, conf_conv_w, conf_ln_g, conf_ln_b, ffn_w_up, ffn_conv_w, ffn_w_down, loss_target, m_c_ctx, m_ada_w, m_ada_b, m_ln_g, m_ln_b, m_even_w_in, m_even_w_out, m_gdn_conv_w, m_gdn_a_log, m_gdn_dt_bias, m_gdn_norm_w, m_pool_w, m_pool_scale, m_odd_w_in, m_odd_w_out, m_sconv_w, m_conf_conv_w, m_conf_ln_g, m_conf_ln_b, m_ffn_w_up, m_ffn_conv_w, m_ffn_w_down, v_c_ctx, v_ada_w, v_ada_b, v_ln_g, v_ln_b, v_even_w_in, v_even_w_out, v_gdn_conv_w, v_gdn_a_log, v_gdn_dt_bias, v_gdn_norm_w, v_pool_w, v_pool_scale, v_odd_w_in, v_odd_w_out, v_sconv_w, v_conf_conv_w, v_conf_ln_g, v_conf_ln_b, v_ffn_w_up, v_ffn_conv_w, v_ffn_w_down):
    names = ['c_ctx', 'ada_w', 'ada_b', 'ln_g', 'ln_b', 'even_w_in', 'even_w_out', 'gdn_conv_w', 'gdn_a_log',
             'gdn_dt_bias', 'gdn_norm_w', 'pool_w', 'pool_scale', 'odd_w_in', 'odd_w_out', 'sconv_w', 'conf_conv_w',
             'conf_ln_g', 'conf_ln_b', 'ffn_w_up', 'ffn_conv_w', 'ffn_w_down']
    loc = locals()
    wts = {n: loc[n] for n in names}
    mom = {n: loc['m_' + n] for n in names}
    var = {n: loc['v_' + n] for n in names}

    ix, iy, ic = lax.axis_index("x"), lax.axis_index("y"), lax.axis_index("c")
    chip = 2 * ix + iy
    dev = 2 * chip + ic
    d = D_MODEL
    x0, ctx0, tgt = x[0], ctx[0], loss_target[0]
    t, tc = x0.shape[0], ctx0.shape[0]
    depth = ada_w.shape[0]
    n_ada = ada_w.shape[2]

    small_sharded = [gdn_conv_w, sconv_w, conf_conv_w, ffn_conv_w, ln_g, ln_b]
    s1_items = [c] + small_sharded
    s1 = _pack_flat(s1_items, 8 * LANES, F32).reshape(-1, LANES)
    g1 = _all_gather8("gather_small_in", s1, True).reshape(8, -1)
    c_all = g1[:, :d]
    per_chip = [_unpack_flat(g1[2 * k], [a.shape for a in s1_items])[1:] for k in range(4)]
    gdn_conv_f, sconv_f, conf_conv_f, ffn_conv_f, ln_g_f, ln_b_f = [
        jnp.concatenate([per_chip[k][i] for k in range(4)], axis=-1) for i in range(len(small_sharded))]
    c16 = jnp.concatenate([c_all, c_ctx[None], jnp.zeros((7, d), F32)], axis=0)

    mod_part = _mod_rows(c16, ada_w)
    g2 = _all_gather8("gather_mod", mod_part.reshape(-1, LANES), True).reshape(4, 2, depth, 16, n_ada)[:, 0]
    mod_all = jnp.transpose(g2, (1, 2, 0, 3)).reshape(depth, 16, 4 * n_ada) + ada_b[:, None, :]
    mod_me = lax.dynamic_index_in_dim(mod_all, dev, axis=1, keepdims=False).reshape(depth, 6, 1, d)
    sh_c, sc_c = mod_all[0, 8, :d][None], mod_all[0, 8, d:2 * d][None]

    flat_names = ['even_w_in', 'even_w_out']
    flat_shapes = [wts[n].shape for n in flat_names]
    half_mult = 2 * 16 * LANES
    rh = -(-sum(math.prod(s) for s in flat_shapes) // half_mult) * half_mult // (2 * LANES)
    flat_pack = _pack_flat([wts[n] for n in flat_names], half_mult, MXU_DTYPE).reshape(2, rh, LANES)
    wg_flat = _all_gather8("gather_weights_even", lax.dynamic_index_in_dim(flat_pack, ic, axis=0, keepdims=False), True).reshape(4, -1)
    per_chip = [_unpack_flat(wg_flat[k], flat_shapes) for k in range(4)]
    n_even = 4 * even_w_in.shape[1]
    n_even_pad = -(-n_even // LANES) * LANES
    w_in = jnp.concatenate([per_chip[k][0] for k in range(4)] + [jnp.zeros((d, n_even_pad - n_even), MXU_DTYPE)], axis=1)
    w_out = jnp.concatenate([per_chip[k][1] for k in range(4)], axis=0)

    def row_half(a, core, axis):
        n = a.shape[axis] // 2
        return lax.dynamic_slice_in_dim(a, core * n, n, axis=axis)

    def layer_of(a, core):
        return lax.dynamic_index_in_dim(a, core, axis=0, keepdims=False)

    mine = [row_half(odd_w_in, ic, 0), row_half(odd_w_out, ic, 0), layer_of(ffn_w_up, ic), layer_of(ffn_w_down, ic)]
    g_oin, g_oout, g_up, g_down = [g.reshape((4, 2) + g.shape[1:]) for g in
                                   _all_gather8_multi("gather_weights", [m.astype(MXU_DTYPE) for m in mine])]
    w_oin = g_oin.reshape(4, d, -1)
    w_oout = g_oout.reshape(-1, d)
    w_up = [(g_up, pl.BlockSpec((4, None) + g_up.shape[2:], lambda t, l=l: (0, l, 0, 0))) for l in range(depth)]
    w_down = [(g_down, pl.BlockSpec((4, None) + g_down.shape[2:], lambda t, l=l: (0, l, 0, 0))) for l in range(depth)]
    scal_blk = (n_even // LANES)
    n_scal = n_even - scal_blk * LANES

    def mod(layer, k):
        return mod_me[layer, k]

    avec = jnp.zeros((1, LANES), F32).at[0, n_scal // 2:n_scal].set(gdn_a_log.reshape(-1))
    dtvec = jnp.zeros((1, LANES), F32).at[0, n_scal // 2:n_scal].set(gdn_dt_bias.reshape(-1))
    normw = gdn_norm_w[None]
    pscale = pool_scale[None]
    cg, cb = conf_ln_g[None], conf_ln_b[None]
    lng = lambda l, k: ln_g_f[l, k][None]
    lnb = lambda l, k: ln_b_f[l, k][None]
    convw9 = ffn_conv_f.reshape(depth, 9, -1)
    nqkv = gdn_conv_f.shape[1]

    p, ub0 = _inproj("even_in", x0, mod(0, 0), mod(0, 1), w_in, 512)
    pc, ucb = _inproj("even_in_ctx", ctx0, sh_c, sc_c, w_in, 256)
    qkv = _qkv_conv("qkv_conv", p, gdn_conv_f)
    qkv_c = _qkv_conv("qkv_conv_ctx", pc, gdn_conv_f)
    gb = _gates("gates", p, avec, dtvec, scal_blk)
    gb_c = _gates("gates_ctx", pc, avec, dtvec, scal_blk)
    lay = _gate_layouts(gb)
    lay_c = _gate_layouts(gb_c)
    s_zero = jnp.zeros((2 * GDN_HEADS, HEAD_DIM, HEAD_DIM), F32)
    _, _, save_c, inv_c, s_ctx = _scan_fwd("scan_ctx", qkv_c, *lay_c, s_zero)
    o0, o1, save_l, inv_l, _ = _scan_fwd("scan", qkv, *lay, s_ctx)
    pool_blk = (nqkv + GDN_HEADS * HEAD_DIM) // 512
    ypool = _pool(p, pool_w, pscale, pool_blk)
    x1, mix0, y0 = _even_out(o0, o1, p, ypool, x0, normw, mod(0, 2), lng(0, 0), lnb(0, 0), w_out, 256)

    def ffn_fwd(l, xin):
        h, ub = _inproj(f"ffn_up{l}", xin, mod(l, 3), mod(l, 4), w_up[l], 256)
        s = _ffn_conv(f"ffn_conv{l}", h, convw9[l])
        xo, y = _ffn_down(f"ffn_down{l}", s, xin, mod(l, 5), lng(l, 1), lnb(l, 1), w_down[l], 256)
        return xo, (h, ub, s, y)

    x2, ffn0 = ffn_fwd(0, x1)
    p1, ub1 = _inproj("odd_in", x2, mod(1, 0), mod(1, 1), w_oin, 512)
    nsc = sconv_f.shape[1] // LANES
    ysc = _sconv(p1, sconv_f)
    z0 = _confconv(p1, conf_conv_f, 3 * nsc)
    x3, mix1, y1 = _odd_out(ysc, z0, x2, cg, cb, mod(1, 2), lng(1, 0), lnb(1, 0), w_oout, 256)
    x4, ffn1 = ffn_fwd(1, x3)
    loss_part, dx4 = _loss_and_grad(x4, tgt, 512)
    loss = lax.psum(loss_part[0, 0], ("x", "y", "c"))

    dmod = [[None] * 6 for _ in range(depth)]
    dlng = [[None, None] for _ in range(depth)]
    dlnb = [[None, None] for _ in range(depth)]
    gbig = {}
    dconv9 = [None] * depth

    def ffn_bwd(l, dxo, xin, saved):
        h, ub, s, y = saved
        ds, dxa, dyb, dgt, dg_, db_ = _ffn_down_bwd(f"ffn_down_bwd{l}", dxo, xin, y, mod(l, 5), lng(l, 1), lnb(l, 1), w_down[l], 256)
        da, dgate, dw9 = _ffn_conv_bwd(f"ffn_conv_bwd{l}", ds, h, convw9[l])
        dh = jnp.concatenate([da, dgate], axis=1)
        dxin, dsh, dsc = _inproj_bwd(f"ffn_up_bwd{l}", dh, xin, mod(l, 3), mod(l, 4), w_up[l], dxa, 256)
        dmod[l][3], dmod[l][4], dmod[l][5] = dsh, dsc, dgt
        dlng[l][1], dlnb[l][1] = dg_, db_
        dconv9[l] = dw9
        dw_up = _matmul_tn(f"dw_up{l}", [(ub, dh)], chip_cols=ffn_w_up.shape[2])
        dw_down = _matmul_tn(f"dw_down{l}", [(s, dyb)]).reshape(4, -1, d)
        return dxin, dw_up, dw_down

    dx3, dwu1, dwd1 = ffn_bwd(1, dx4, x3, ffn1)
    dysc, dz0, dx2a, dyb1, dcg, dcb, dgt, dg_, db_ = _odd_out_bwd(dx3, ysc, z0, x2, y1, cg, cb, mod(1, 2), lng(1, 0), lnb(1, 0), w_oout, 256)
    dmod[1][2], dlng[1][0], dlnb[1][0] = dgt, dg_, db_
    d_gb, d_gc, d_h, dsconv = _sconv_bwd(dysc, p1, sconv_f)
    d_ga, d_gbb, dconf = _confconv_bwd(dz0, p1, conf_conv_f, 3 * nsc)
    dp1 = jnp.concatenate([d_gb, d_gc, d_h, d_ga, d_gbb], axis=1)
    dx2, dsh, dsc = _inproj_bwd("odd_in_bwd", dp1, x2, mod(1, 0), mod(1, 1), w_oin, dx2a, 512)
    dmod[1][0], dmod[1][1] = dsh, dsc
    dw_oout = _matmul_tn("dw_oout", [(mix1, dyb1)]).reshape(4, -1, d)
    dw_oin = _matmul_tn("dw_oin", [(ub1, dp1)], chip_cols=odd_w_in.shape[1])

    dx1, dwu0, dwd0 = ffn_bwd(0, dx2, x1, ffn0)

    do, dpg, dypool, dx0a, dyb0, dnormw, dgt, dg_, db_ = _even_out_bwd(dx1, o0, o1, p, ypool, x0, y0, normw, mod(0, 2), lng(0, 0), lnb(0, 0), w_out, 256)
    dmod[0][2], dlng[0][0], dlnb[0][0] = dgt, dg_, db_
    pool_cts = _pool_bwd(dypool, p, pool_w, pscale, pool_blk)
    dpp, dpool_scale, dpool_w = pool_cts[0], pool_cts[1], jnp.stack(pool_cts[2:])
    dqkv0, dqkv1, dbcol, dgcol, dgrow, ds0 = _scan_bwd("scan_bwd", do, qkv, *lay, save_l, inv_l, s_zero)
    zero_do = jnp.zeros((tc, GDN_HEADS * HEAD_DIM), F32)
    dqkv0_c, dqkv1_c, dbcol_c, dgcol_c, dgrow_c, _ = _scan_bwd("scan_bwd_ctx", zero_do, qkv_c, *lay_c, save_c, inv_c, ds0)
    dgb = _gate_layouts_bwd(dbcol, dgcol, dgrow)
    dgb_c = _gate_layouts_bwd(dbcol_c, dgcol_c, dgrow_c)
    dps, davec, ddtvec = _gates_bwd("gates_bwd", dgb, p, avec, dtvec, scal_blk)
    dps_c, davec_c, ddtvec_c = _gates_bwd("gates_bwd_ctx", dgb_c, pc, avec, dtvec, scal_blk)
    dpqkv, dconv5 = _qkv_conv_bwd("qkv_conv_bwd", dqkv0, dqkv1, p, gdn_conv_f)
    dpqkv_c, dconv5_c = _qkv_conv_bwd("qkv_conv_bwd_ctx", dqkv0_c, dqkv1_c, pc, gdn_conv_f)
    dp = jnp.concatenate([dpqkv, dpg, dpp, dps], axis=1)
    dpc = jnp.concatenate([dpqkv_c, jnp.zeros((tc, n_even_pad - nqkv - LANES), MXU_DTYPE), dps_c], axis=1)
    grad_x, dsh, dsc = _inproj_bwd("even_in_bwd", dp, x0, mod(0, 0), mod(0, 1), w_in, dx0a, 512)
    dmod[0][0], dmod[0][1] = dsh, dsc
    _, dsh_c, dsc_c = _inproj_bwd("even_in_bwd_ctx", dpc, ctx0, sh_c, sc_c, w_in, None, 256)
    dw_in = _matmul_tn("dw_in", [(ub0, dp), (ucb, dpc)])
    dw_out = _matmul_tn("dw_out", [(mix0, dyb0)])

    n_in, n_out = even_w_in.shape[1], even_w_out.shape[0]
    gflat = jnp.stack([_pack_flat([dw_in[:, k * n_in:(k + 1) * n_in], dw_out[k * n_out:(k + 1) * n_out]], half_mult,
                                  MXU_DTYPE).reshape(2, rh, LANES) for k in range(4)], axis=1)
    halves = [
        (gflat[0], gflat[1]),
        (dw_oin[:, :d // 2], dw_oin[:, d // 2:]),
        (dw_oout[:, :odd_w_out.shape[0] // 2], dw_oout[:, odd_w_out.shape[0] // 2:]),
        (dwu0, dwu1),
        (dwd0, dwd1),
    ]
    keeps = [jnp.where(ic == 0, h0, h1) for h0, h1 in halves]
    gives = [jnp.where(ic == 0, h1, h0) for h0, h1 in halves]
    gots = _sibling_exchange("grad_pair_exchange", gives)
    pairs = [_sum_pair(f"grad_pair_sum{i}", kp, gt_, MXU_DTYPE) for i, (kp, gt_) in enumerate(zip(keeps, gots))]
    parts = _chip_scatter("grad_chip_scatter", pairs)
    sums = [_sum_axis1(f"grad_chip_sum{i}", p_) for i, p_ in enumerate(parts)]
    g_even_in, g_even_out = _unpack_flat(sums[0].reshape(-1), flat_shapes)
    g_shards = {'even_w_in': g_even_in, 'even_w_out': g_even_out, 'odd_w_in': sums[1].reshape(odd_w_in.shape),
                'odd_w_out': sums[2].reshape(odd_w_out.shape), 'ffn_w_up': sums[3], 'ffn_w_down': sums[4]}

    dmod_rows = jnp.stack([jnp.concatenate(dmod[l], axis=1)[0] for l in range(depth)])
    dmod_c = jnp.concatenate([dsh_c[0], dsc_c[0], jnp.zeros((4 * d,), F32)])
    dmod_c_rows = jnp.stack([dmod_c] + [jnp.zeros_like(dmod_c)] * (depth - 1))
    small_g = {
        'ln_g': jnp.stack([jnp.stack([dlng[l][k][0] for k in range(2)]) for l in range(depth)]),
        'ln_b': jnp.stack([jnp.stack([dlnb[l][k][0] for k in range(2)]) for l in range(depth)]),
        'gdn_conv_w': dconv5 + dconv5_c,
        'gdn_a_log': (davec + davec_c)[0, n_scal // 2:n_scal].reshape(gdn_a_log.shape),
        'gdn_dt_bias': (ddtvec + ddtvec_c)[0, n_scal // 2:n_scal].reshape(gdn_dt_bias.shape),
        'gdn_norm_w': dnormw[0], 'pool_w': dpool_w, 'pool_scale': dpool_scale[0],
        'sconv_w': dsconv, 'conf_conv_w': dconf, 'conf_ln_g': dcg[0], 'conf_ln_b': dcb[0],
        'ffn_conv_w': jnp.stack(dconv9).reshape(depth, 3, 3, -1),
    }
    small_names = list(small_g)
    s3_items = [dmod_rows, dmod_c_rows] + [small_g[n] for n in small_names]
    s3 = _pack_flat(s3_items, 8 * LANES, F32).reshape(-1, LANES)
    g3 = _all_gather8("gather_small_grads", s3, True).reshape(8, -1, LANES)
    tot3 = _sum_axis1("small_grad_sum", g3[None]).reshape(-1)
    tot_items = _unpack_flat(tot3, [a.shape for a in s3_items])
    dmod_sum, dmod_c_sum = tot_items[0], tot_items[1]
    small_tot = dict(zip(small_names, tot_items[2:]))
    grad_ada_b = dmod_sum + dmod_c_sum
    g3f = g3.reshape(8, -1)
    rows_all = g3f[:, :depth * 6 * d].reshape(8, depth, 6 * d)
    cols = lax.dynamic_slice_in_dim(rows_all, chip * n_ada, n_ada, axis=2)
    crow = lax.dynamic_slice_in_dim(dmod_c_sum, chip * n_ada, n_ada, axis=1)
    dmod16 = jnp.concatenate([jnp.transpose(cols, (1, 0, 2)), crow[:, None, :], jnp.zeros((depth, 7, n_ada), F32)], axis=1)
    grad_ada_w, dsil = _ada_grads(c16, dmod16, ada_w)
    s4 = jnp.concatenate([dsil[0, 8][None], jnp.zeros((7, d), F32)], axis=0).reshape(-1, LANES)
    g4 = _all_gather8("gather_cctx", s4, True).reshape(8, 8, d)
    grad_c_ctx = _cctx_grad(g4[0::2, 0][:, None, :], c_ctx[None])[0]

    def my_cols(a, n):
        return lax.dynamic_slice_in_dim(a, chip * n, n, axis=a.ndim - 1)

    grads = dict(g_shards)
    grads['c_ctx'] = grad_c_ctx
    grads['ada_w'] = grad_ada_w
    grads['ada_b'] = grad_ada_b
    for n in ['ln_g', 'ln_b', 'gdn_conv_w', 'sconv_w', 'conf_conv_w', 'ffn_conv_w']:
        grads[n] = my_cols(small_tot[n], wts[n].shape[-1])
    for n in ['gdn_a_log', 'gdn_dt_bias', 'gdn_norm_w', 'pool_w', 'pool_scale', 'conf_ln_g', 'conf_ln_b']:
        grads[n] = small_tot[n]

    delta, new_m, new_v = {}, {}, {}
    big_adam = list(g_shards) + ['ada_w']
    for n in big_adam:
        shp = wts[n].shape
        as2d = lambda a: a.reshape(-1, shp[-1])
        dl, nm, nv = _adamw("adamw_" + n, as2d(wts[n]), as2d(grads[n]), as2d(mom[n]), as2d(var[n]))
        delta[n], new_m[n], new_v[n] = dl.reshape(shp), nm.reshape(shp), nv.reshape(shp)
    small_adam = [n for n in names if n not in big_adam]
    packs = [_pack_flat([src[n] for n in small_adam], 8 * LANES, F32).reshape(-1, LANES) for src in (wts, grads, mom, var)]
    outs = _adamw("adamw_small", *packs)
    shapes = [wts[n].shape for n in small_adam]
    for res, dst in zip(outs, (delta, new_m, new_v)):
        for n, a in zip(small_adam, _unpack_flat(res.reshape(-1), shapes)):
            dst[n] = a

    return (loss, grad_x[None], *[grads[n] for n in names], *[delta[n] for n in names],
            *[new_m[n] for n in names], *[new_v[n] for n in names])
```

```python
import functools
import math

import jax
import jax.numpy as jnp
from jax import lax
from jax.experimental import pallas as pl
from jax.experimental.pallas import tpu as pltpu

F32 = jnp.float32
MXU_DTYPE = jnp.bfloat16
HIGHEST = lax.Precision.HIGHEST
MESH = pl.DeviceIdType.MESH

D_MODEL = 1024
GRID_W = 64
GDN_HEADS = 4
HEAD_DIM = 128
GDN_CHUNK = 64
POOL_WINDOWS = (2, 4, 8, 16)
GDN_CONV = 5
SC_WIDTH = 3
CF_WIDTH = 31
ALPHA = 4.0 ** 0.25
LN_EPS = 1e-5
RMS_EPS = 1e-6
LANES = 128
VMEM_LIMIT_BYTES = 58 * 1024 * 1024

ADAM_LR, ADAM_B1, ADAM_B2, ADAM_EPS, ADAM_WD, ADAM_STEP = 0.001, 0.9, 0.999, 1e-08, 0.01, 10


def _blocked(name, fn, grid, ins, in_specs, out_shapes, out_specs, acc=(), acc_axis=None, scratch=()):
    n_in = len(ins)
    n_out = len(out_shapes)

    def body(*refs):
        vals = [r[...] for r in refs[:n_in]]
        res = fn(*vals, *refs[n_in + n_out:])
        if not isinstance(res, (tuple, list)):
            res = (res,)
        for k, (r, v) in enumerate(zip(refs[n_in:n_in + n_out], res)):
            if k in acc:
                first = pl.program_id(acc_axis) == 0

                @pl.when(first)
                def _(r=r, v=v):
                    r[...] = v.astype(r.dtype)

                @pl.when(jnp.logical_not(first))
                def _(r=r, v=v):
                    r[...] += v.astype(r.dtype)
            else:
                r[...] = v.astype(r.dtype)

    return pl.pallas_call(
        body, name=name, grid=grid, in_specs=in_specs, out_specs=out_specs, out_shape=out_shapes,
        scratch_shapes=list(scratch),
        compiler_params=pltpu.CompilerParams(dimension_semantics=("arbitrary",) * len(grid),
                                             vmem_limit_bytes=VMEM_LIMIT_BYTES),
    )(*ins)


def _sds(shape, dtype=F32):
    return jax.ShapeDtypeStruct(tuple(shape), dtype)


def _tok(tm, n, col=0):
    return pl.BlockSpec((tm, n), lambda t: (t, col))


def _res(shape):
    nd = len(shape)
    return pl.BlockSpec(tuple(shape), lambda t: (0,) * nd)


def _silu(x):
    return x * jax.nn.sigmoid(x)


def _layernorm(r, g, b):
    mu = jnp.mean(r, -1, keepdims=True)
    d = r - mu
    var = jnp.mean(d * d, -1, keepdims=True)
    return d * lax.rsqrt(var + LN_EPS) * g + b


def _mm_nn_impl(a, b):
    return jnp.dot(a.astype(MXU_DTYPE), b.astype(MXU_DTYPE), preferred_element_type=F32)


def _mm_nt_impl(a, b):
    return lax.dot_general(a.astype(MXU_DTYPE), b.astype(MXU_DTYPE), (((1,), (1,)), ((), ())), preferred_element_type=F32)


def _mm_tn_impl(a, b):
    return lax.dot_general(a.astype(MXU_DTYPE), b.astype(MXU_DTYPE), (((0,), (0,)), ((), ())), preferred_element_type=F32)


def _mm_vjp(mm, mm_da, mm_db):
    f = jax.custom_vjp(mm)
    f.defvjp(lambda a, b: (mm(a, b), (a, b)), lambda res, g: (mm_da(g, res[1]), mm_db(res[0], g)))
    return f


_mm = _mm_vjp(_mm_nn_impl, lambda g, b: _mm_nt_impl(g, b), lambda a, g: _mm_tn_impl(a, g))
_mm_nt = _mm_vjp(_mm_nt_impl, lambda g, b: _mm_nn_impl(g, b), lambda a, g: _mm_tn_impl(g, a))
_mm_tn = _mm_vjp(_mm_tn_impl, lambda g, b: _mm_nt_impl(b, g), lambda a, g: _mm_nn_impl(a, g))


def _row(w, k):
    rows = lax.broadcasted_iota(jnp.int32, w.shape, 0)
    return jnp.sum(jnp.where(rows == k, w, 0.0), axis=0, keepdims=True)


def _col_mask(shape, dc):
    col = lax.broadcasted_iota(jnp.int32, shape, 0) & (GRID_W - 1)
    return (col + dc >= 0) & (col + dc < GRID_W)


def _center(x_ext, halo):
    return x_ext[halo:x_ext.shape[0] - halo]


def _make_dwconv(taps, halo):
    assert all(abs(s) <= halo for s, _ in taps)

    def shifted(x_ext, s):
        r = x_ext if s == 0 else pltpu.roll(x_ext, (-s) % x_ext.shape[0], 0)
        return _center(r, halo)

    @jax.custom_vjp
    def conv(x_ext, w):
        acc = None
        for k, (s, dc) in enumerate(taps):
            r = shifted(x_ext, s)
            if dc != 0:
                r = jnp.where(_col_mask(r.shape, dc), r, 0.0)
            term = r * _row(w, k)
            acc = term if acc is None else acc + term
        return acc

    def fwd(x_ext, w):
        return conv(x_ext, w), (x_ext, w)

    def bwd(res, dy):
        x_ext, w = res
        n = x_ext.shape[0]
        rows = lax.broadcasted_iota(jnp.int32, w.shape, 0)
        pad = jnp.zeros((halo, dy.shape[1]), F32)
        dx = None
        dw = jnp.zeros(w.shape, F32)
        for k, (s, dc) in enumerate(taps):
            dym = dy if dc == 0 else jnp.where(_col_mask(dy.shape, dc), dy, 0.0)
            dw = dw + jnp.where(rows == k, jnp.sum(dym * shifted(x_ext, s), axis=0, keepdims=True), 0.0)
            t = jnp.concatenate([pad, dym * _row(w, k), pad], axis=0)
            if s != 0:
                t = pltpu.roll(t, s % n, 0)
            dx = t if dx is None else dx + t
        return dx, dw

    conv.defvjp(fwd, bwd)
    return conv


def _taps_1d(width):
    return tuple((k - width // 2, 0) for k in range(width))


HALO_SHORT = 8
HALO_CONF = 16
HALO_GRID = 72
_conv5 = _make_dwconv(_taps_1d(GDN_CONV), HALO_SHORT)
_conv3 = _make_dwconv(_taps_1d(SC_WIDTH), HALO_SHORT)
_conv31 = _make_dwconv(_taps_1d(CF_WIDTH), HALO_CONF)
_conv3x3 = _make_dwconv(tuple((dr * GRID_W + dc, dc) for dr in (-1, 0, 1) for dc in (-1, 0, 1)), HALO_GRID)
_pool_sums = {w: _make_dwconv(tuple((s, 0) for s in range(-(w // 2), w - w // 2)), HALO_SHORT) for w in POOL_WINDOWS}


def _all_gather8(name, blk, in_vmem):
    m_per, n = blk.shape
    space = pltpu.VMEM if in_vmem else pl.ANY

    def body(x_ref, out_ref, send_sems, recv_sems, local_sem):
        x, y, c = lax.axis_index("x"), lax.axis_index("y"), lax.axis_index("c")
        me, sibling = (x, y, c), (x, y, 1 - c)
        chips = [(1 - x, y), (x, 1 - y), (1 - x, 1 - y)]

        def rows(px, py, pc):
            return out_ref.at[pl.ds((4 * px + 2 * py + pc) * m_per, m_per), :]

        def copy(k, block, to, src=None):
            return pltpu.make_async_remote_copy(
                src_ref=rows(*block) if src is None else src, dst_ref=rows(*block),
                send_sem=send_sems.at[k], recv_sem=recv_sems.at[k], device_id=to, device_id_type=MESH)

        mine = pltpu.make_async_copy(x_ref, rows(*me), local_sem)
        mine.start()
        first = [copy(0, me, sibling, src=x_ref)]
        first += [copy(1 + j, me, (*chip, c), src=x_ref) for j, chip in enumerate(chips)]
        for cp in first:
            cp.start()
        passed = [copy(4 + j, (*chip, c), sibling) for j, chip in enumerate(chips)]
        for j, chip in enumerate(chips):
            copy(1 + j, (*chip, c), me).wait_recv()
            passed[j].start()
        copy(0, sibling, me).wait_recv()
        for j, chip in enumerate(chips):
            copy(4 + j, (*chip, 1 - c), me).wait_recv()
        for cp in first + passed:
            cp.wait_send()
        mine.wait()

    return pl.pallas_call(
        body, name=name, out_shape=_sds((8 * m_per, n), blk.dtype),
        in_specs=[pl.BlockSpec(memory_space=space)], out_specs=pl.BlockSpec(memory_space=space),
        scratch_shapes=[pltpu.SemaphoreType.DMA((7,)), pltpu.SemaphoreType.DMA((7,)), pltpu.SemaphoreType.DMA],
        compiler_params=pltpu.CompilerParams(vmem_limit_bytes=VMEM_LIMIT_BYTES),
    )(blk)


_ANY = pl.BlockSpec(memory_space=pl.ANY)


def _all_gather8_multi(name, blocks):
    n = len(blocks)

    def body(*refs):
        x_refs, out_refs = refs[:n], refs[n:2 * n]
        send_sems, recv_sems, local_sems = refs[2 * n:]
        x, y, c = lax.axis_index("x"), lax.axis_index("y"), lax.axis_index("c")
        me, sibling = (x, y, c), (x, y, 1 - c)
        chips = [(1 - x, y), (x, 1 - y), (1 - x, 1 - y)]

        def slot(i, px, py, pc):
            return out_refs[i].at[4 * px + 2 * py + pc]

        def copy(i, k, block, to, src=None):
            return pltpu.make_async_remote_copy(
                src_ref=slot(i, *block) if src is None else src, dst_ref=slot(i, *block),
                send_sem=send_sems.at[i, k], recv_sem=recv_sems.at[i, k], device_id=to, device_id_type=MESH)

        mine = [pltpu.make_async_copy(x_refs[i], slot(i, *me), local_sems.at[i]) for i in range(n)]
        first = []
        for i in range(n):
            first.append(copy(i, 0, me, sibling, src=x_refs[i]))
            first += [copy(i, 1 + j, me, (*chip, c), src=x_refs[i]) for j, chip in enumerate(chips)]
        for cp in mine + first:
            cp.start()
        passed = []
        for j, chip in enumerate(chips):
            for i in range(n):
                copy(i, 1 + j, (*chip, c), me).wait_recv()
                passed.append(copy(i, 4 + j, (*chip, c), sibling))
                passed[-1].start()
        for i in range(n):
            copy(i, 0, sibling, me).wait_recv()
            for j, chip in enumerate(chips):
                copy(i, 4 + j, (*chip, 1 - c), me).wait_recv()
        for cp in first + passed:
            cp.wait_send()
        for cp in mine:
            cp.wait()

    return pl.pallas_call(
        body, name=name, out_shape=[_sds((8,) + b.shape, b.dtype) for b in blocks],
        in_specs=[_ANY] * n, out_specs=[_ANY] * n,
        scratch_shapes=[pltpu.SemaphoreType.DMA((n, 7)), pltpu.SemaphoreType.DMA((n, 7)), pltpu.SemaphoreType.DMA((n,))],
    )(*blocks)


def _sibling_exchange(name, sends):
    n = len(sends)

    def body(*refs):
        x, y, c = lax.axis_index("x"), lax.axis_index("y"), lax.axis_index("c")
        send_sems, recv_sems = refs[2 * n:]
        copies = [pltpu.make_async_remote_copy(src_ref=refs[i], dst_ref=refs[n + i], send_sem=send_sems.at[i],
                                               recv_sem=recv_sems.at[i], device_id=(x, y, 1 - c), device_id_type=MESH)
                  for i in range(n)]
        for cp in copies:
            cp.start()
        for cp in copies:
            cp.wait()

    return pl.pallas_call(
        body, name=name, out_shape=[_sds(s.shape, s.dtype) for s in sends],
        in_specs=[_ANY] * n, out_specs=[_ANY] * n,
        scratch_shapes=[pltpu.SemaphoreType.DMA((n,)), pltpu.SemaphoreType.DMA((n,))],
    )(*sends)


def _chip_scatter(name, parts):
    n = len(parts)

    def body(*refs):
        p_refs, out_refs = refs[:n], refs[n:2 * n]
        send_sems, recv_sems, local_sems = refs[2 * n:]
        x, y, c = lax.axis_index("x"), lax.axis_index("y"), lax.axis_index("c")
        sibling = (x, y, 1 - c)
        me_chip = 2 * x + y
        chips = [(1 - x, y), (x, 1 - y), (1 - x, 1 - y)]

        def chip_id(chip):
            return 2 * chip[0] + chip[1]

        def copy(i, k, src, dst, to):
            return pltpu.make_async_remote_copy(src_ref=src, dst_ref=dst, send_sem=send_sems.at[i, k],
                                                recv_sem=recv_sems.at[i, k], device_id=to, device_id_type=MESH)

        mine = [pltpu.make_async_copy(p_refs[i].at[me_chip], out_refs[i].at[c, me_chip], local_sems.at[i]) for i in range(n)]
        first = []
        for i in range(n):
            first.append(copy(i, 0, p_refs[i].at[me_chip], out_refs[i].at[c, me_chip], sibling))
            first += [copy(i, 1 + j, p_refs[i].at[chip_id(chip)], out_refs[i].at[c, me_chip], (*chip, c))
                      for j, chip in enumerate(chips)]
        for cp in mine + first:
            cp.start()
        passed = []
        for j, chip in enumerate(chips):
            for i in range(n):
                landed = out_refs[i].at[c, chip_id(chip)]
                copy(i, 1 + j, p_refs[i].at[0], landed, sibling).wait_recv()
                passed.append(copy(i, 4 + j, landed, landed, sibling))
                passed[-1].start()
        for i in range(n):
            copy(i, 0, p_refs[i].at[0], out_refs[i].at[1 - c, me_chip], sibling).wait_recv()
            for j, chip in enumerate(chips):
                copy(i, 4 + j, p_refs[i].at[0], out_refs[i].at[1 - c, chip_id(chip)], sibling).wait_recv()
        for cp in first + passed:
            cp.wait_send()
        for cp in mine:
            cp.wait()

    return pl.pallas_call(
        body, name=name, out_shape=[_sds((2,) + p.shape, p.dtype) for p in parts],
        in_specs=[_ANY] * n, out_specs=[_ANY] * n,
        scratch_shapes=[pltpu.SemaphoreType.DMA((n, 7)), pltpu.SemaphoreType.DMA((n, 7)), pltpu.SemaphoreType.DMA((n,))],
    )(*parts)


_HBM = pl.BlockSpec(memory_space=pltpu.HBM)
_SEM = pl.BlockSpec(memory_space=pltpu.SEMAPHORE)
_SPLIT_CALL = pltpu.CompilerParams(has_side_effects=pltpu.SideEffectType.DATAFLOW_SIDE_EFFECTING)


def _peer_chips():
    x, y = lax.axis_index("x"), lax.axis_index("y")
    return [(1 - x, y), (x, 1 - y), (1 - x, 1 - y)]


def _chip_send_start(name, blocks, before):
    n, nb = len(blocks), len(before)

    def body(*refs):
        x_refs, land_refs = refs[:n], refs[n:2 * n]
        sems, token = refs[2 * n + nb:2 * n + nb + 6 * n], refs[-1]
        c = lax.axis_index("c")
        me_chip = 2 * lax.axis_index("x") + lax.axis_index("y")
        for i in range(n):
            for j, chip in enumerate(_peer_chips()):
                pltpu.make_async_remote_copy(src_ref=x_refs[i], dst_ref=land_refs[i].at[me_chip], send_sem=sems[6 * i + j],
                                             recv_sem=sems[6 * i + 3 + j], device_id=(*chip, c), device_id_type=MESH).start()
        token[...] = jnp.zeros(token.shape, token.dtype)

    lands = [lax.empty((4,) + b.shape, b.dtype) for b in blocks]
    hbm = [pltpu.with_memory_space_constraint(a, pltpu.HBM) for a in list(blocks) + lands]
    outs = pl.pallas_call(
        body, name=name,
        out_shape=tuple([pltpu.SemaphoreType.DMA(())] * (6 * n)) + tuple(pltpu.HBM(a.shape, a.dtype) for a in hbm) + (_sds((8, LANES)),),
        in_specs=tuple([_HBM] * (2 * n) + [_ANY] * nb),
        out_specs=tuple([_SEM] * (6 * n) + [_HBM] * (2 * n) + [pl.BlockSpec(memory_space=pltpu.VMEM)]),
        input_output_aliases={i: 6 * n + i for i in range(2 * n)}, compiler_params=_SPLIT_CALL,
    )(*hbm, *before)
    return outs[:6 * n], outs[6 * n:7 * n], outs[7 * n:8 * n], outs[8 * n]


def _chip_send_wait(name, sems, blocks, lands, after):
    n = len(blocks)

    def body(*refs):
        x_refs, land_refs, sems = refs[:n], refs[n:2 * n], refs[2 * n:8 * n]
        c = lax.axis_index("c")
        for i in range(n):
            for j, chip in enumerate(_peer_chips()):
                cp = pltpu.make_async_remote_copy(src_ref=x_refs[i], dst_ref=land_refs[i].at[2 * chip[0] + chip[1]],
                                                  send_sem=sems[6 * i + j], recv_sem=sems[6 * i + 3 + j],
                                                  device_id=(*chip, c), device_id_type=MESH)
                cp.wait_send()
                cp.wait_recv()

    outs = pl.pallas_call(
        body, name=name, out_shape=tuple(pltpu.HBM(a.shape, a.dtype) for a in list(blocks) + list(lands)),
        in_specs=tuple([_HBM] * (2 * n) + [_SEM] * (6 * n) + [_ANY]), out_specs=tuple([_HBM] * (2 * n)),
        input_output_aliases={i: i for i in range(2 * n)}, compiler_params=_SPLIT_CALL,
    )(*blocks, *lands, *sems, after)
    return outs[:n], outs[n:]


def _pack_flat(arrays, multiple, dtype):
    flat = jnp.concatenate([a.reshape(-1).astype(dtype) for a in arrays])
    pad = (-flat.shape[0]) % multiple
    if pad:
        flat = jnp.concatenate([flat, jnp.zeros((pad,), dtype)])
    return flat


def _unpack_flat(flat, shapes):
    out, off = [], 0
    for s in shapes:
        n = math.prod(s)
        out.append(flat[off:off + n].reshape(s))
        off += n
    return out


def _row_tile(r, cap, mult=16):
    for cand in range(min(cap, r) // mult * mult, 0, -mult):
        if r % cand == 0:
            return cand
    return r


ELEMENTWISE_BLOCK_ELEMS = 1 << 20


def _sum_axis1(name, x, out_dtype=F32):
    h, k, r, n = x.shape
    tr = _row_tile(r, max(16, ELEMENTWISE_BLOCK_ELEMS // (k * n)))

    def fn(s):
        acc = s[0].astype(F32)
        for i in range(1, k):
            acc = acc + s[i].astype(F32)
        return acc

    return _blocked(name, fn, (h, r // tr), [x], [pl.BlockSpec((None, k, tr, n), lambda i, t: (i, 0, t, 0))],
                    [_sds((h, r, n), out_dtype)], [pl.BlockSpec((None, tr, n), lambda i, t: (i, t, 0))])[0]


def _sum_pair(name, a, b, out_dtype):
    lead, r, n = a.shape
    tr = _row_tile(r, max(16, ELEMENTWISE_BLOCK_ELEMS // n))
    spec = pl.BlockSpec((None, tr, n), lambda i, t: (i, t, 0))
    return _blocked(name, lambda x, y: x.astype(F32) + y.astype(F32), (lead, r // tr), [a, b], [spec, spec],
                    [_sds((lead, r, n), out_dtype)], [spec])[0]


def _adamw(name, w, g, m, v):
    r, c = w.shape
    tr = _row_tile(r, max(8, (1 << 21) // (4 * c)), 8)
    bc1 = 1.0 - ADAM_B1 ** ADAM_STEP
    bc2 = 1.0 - ADAM_B2 ** ADAM_STEP

    def fn(w, g, m, v):
        nm = ADAM_B1 * m + (1.0 - ADAM_B1) * g
        nv = ADAM_B2 * v + (1.0 - ADAM_B2) * (g * g)
        delta = -ADAM_LR * ((nm / bc1) / (jnp.sqrt(nv / bc2) + ADAM_EPS) + ADAM_WD * w)
        return delta, nm, nv

    spec = pl.BlockSpec((tr, c), lambda t: (t, 0))
    return _blocked(name, fn, (r // tr,), [w, g, m, v], [spec] * 4, [_sds((r, c))] * 3, [spec] * 3)


def _mod_rows(c16, ada_w):
    depth, d, n = ada_w.shape

    def fn(c, w):
        return _mm(_silu(c), w)

    return _blocked("mod_rows", fn, (depth,), [c16, ada_w],
                    [_res(c16.shape), pl.BlockSpec((None, d, n), lambda l: (l, 0, 0))],
                    [_sds((depth, 16, n))], [pl.BlockSpec((None, 16, n), lambda l: (l, 0, 0))])[0]


def _ada_grads(c16, dmod16, ada_w):
    depth, d, n = ada_w.shape

    def fn(c, dm, w):
        return _mm_tn(_silu(c), dm), _mm_nt(dm, w)

    return _blocked("ada_grads", fn, (depth,), [c16, dmod16, ada_w],
                    [_res(c16.shape), pl.BlockSpec((None, 16, n), lambda l: (l, 0, 0)),
                     pl.BlockSpec((None, d, n), lambda l: (l, 0, 0))],
                    [_sds((depth, d, n)), _sds((depth, 16, d))],
                    [pl.BlockSpec((None, d, n), lambda l: (l, 0, 0)), pl.BlockSpec((None, 16, d), lambda l: (l, 0, 0))])


def _cctx_grad(parts, c_ctx_row):
    def fn(p, c):
        tot = ((p[0] + p[1]) + p[2]) + p[3]
        _, vjp = jax.vjp(_silu, c)
        return vjp(tot)[0]

    return _blocked("cctx_grad", fn, (1,), [parts, c_ctx_row], [_res(parts.shape), _res(c_ctx_row.shape)],
                    [_sds(c_ctx_row.shape)], [_res(c_ctx_row.shape)])[0]


def _modulate(x, sh, sc):
    return x * (1.0 + sc) + sh


def _weight_operand(w):
    if isinstance(w, tuple):
        return w[0], w[1], tuple(b for b in w[1].block_shape if b is not None)
    return w, _res(w.shape), w.shape


def _matmul_cols(u, w):
    if w.ndim == 2:
        return jnp.dot(u, w, preferred_element_type=F32)
    return jnp.concatenate([jnp.dot(u, w[j], preferred_element_type=F32) for j in range(w.shape[0])], axis=1)


def _matmul_cols_t(dp, w):
    if w.ndim == 2:
        return _mm_nt(dp, w)
    nj = w.shape[2]
    return functools.reduce(lambda a, b: a + b, [_mm_nt(dp[:, j * nj:(j + 1) * nj], w[j]) for j in range(w.shape[0])])


def _rows_weight(w):
    return w if w.ndim == 2 else w.reshape(w.shape[0] * w.shape[1], w.shape[2])


def _inproj(name, x, sh, sc, w, tm):
    t, d = x.shape
    w, w_spec, w_blk = _weight_operand(w)
    n = w_blk[-1] * (w_blk[0] if len(w_blk) == 3 else 1)
    tm = min(tm, t)

    def fn(x, sh, sc, w):
        u = _modulate(x, sh, sc).astype(MXU_DTYPE)
        return _matmul_cols(u, w), u

    return _blocked(name, fn, (t // tm,), [x, sh, sc, w],
                    [_tok(tm, d), _res(sh.shape), _res(sc.shape), w_spec],
                    [_sds((t, n)), _sds((t, d), MXU_DTYPE)], [_tok(tm, n), _tok(tm, d)])


def _inproj_bwd(name, dp, x, sh, sc, w, add, tm):
    t, d = x.shape
    w, w_spec, w_blk = _weight_operand(w)
    n = w_blk[-1] * (w_blk[0] if len(w_blk) == 3 else 1)
    tm = min(tm, t)
    has_add = add is not None

    def fn(dp, x, sh, sc, w, *rest):
        du = _matmul_cols_t(dp, w)
        _, vjp = jax.vjp(_modulate, x, sh, sc)
        dx, dsh, dsc = vjp(du)
        if has_add:
            dx = dx + rest[0]
        return dx, dsh, dsc

    ins = [dp, x, sh, sc, w] + ([add] if has_add else [])
    specs = [_tok(tm, n), _tok(tm, d), _res(sh.shape), _res(sc.shape), w_spec] + ([_tok(tm, d)] if has_add else [])
    return _blocked(name, fn, (t // tm,), ins, specs,
                    [_sds((t, d)), _sds(sh.shape), _sds(sc.shape)], [_tok(tm, d), _res(sh.shape), _res(sc.shape)],
                    acc=(1, 2), acc_axis=0)


def _residual_ln(y, x, gt, lng, lnb):
    return _layernorm(ALPHA * x + gt * y, lng, lnb)


def _gdn_mix(o0, o1, pg, yp, normw):
    o = o0 + o1
    heads = []
    for h in range(GDN_HEADS):
        oh = o[:, h * HEAD_DIM:(h + 1) * HEAD_DIM]
        heads.append(oh * lax.rsqrt(jnp.mean(oh * oh, -1, keepdims=True) + RMS_EPS) * normw)
    on = jnp.concatenate(heads, axis=-1) * _silu(pg)
    return jnp.concatenate([on, yp], axis=-1)


def _even_out(o0, o1, p, ypool, x, normw, gt, lng, lnb, w, tm):
    t, d = x.shape
    tm = min(tm, t)
    gate_blk = 3

    def fn(o0, o1, pg, yp, x, normw, gt, lng, lnb, w):
        mix = _gdn_mix(o0, o1, pg, yp, normw).astype(MXU_DTYPE)
        y = jnp.dot(mix, w, preferred_element_type=F32)
        return _residual_ln(y, x, gt, lng, lnb), mix, y

    return _blocked("even_out", fn, (t // tm,), [o0, o1, p, ypool, x, normw, gt, lng, lnb, w],
                    [_tok(tm, 512), _tok(tm, 512), _tok(tm, 512, gate_blk), _tok(tm, 512), _tok(tm, d),
                     _res(normw.shape), _res(gt.shape), _res(lng.shape), _res(lnb.shape), _res(w.shape)],
                    [_sds((t, d)), _sds((t, d), MXU_DTYPE), _sds((t, d))], [_tok(tm, d)] * 3)


def _even_out_bwd(dx1, o0, o1, p, ypool, x, y, normw, gt, lng, lnb, w, tm):
    t, d = x.shape
    tm = min(tm, t)

    def fn(dx1, o0, o1, pg, yp, x, y, normw, gt, lng, lnb, w):
        _, vjp2 = jax.vjp(_residual_ln, y, x, gt, lng, lnb)
        dy, dx, dgt, dlng, dlnb = vjp2(dx1)
        dyb = dy.astype(MXU_DTYPE)
        dmix = _mm_nt(dyb, w)
        _, vjp1 = jax.vjp(_gdn_mix, o0, o1, pg, yp, normw)
        do, _, dpg, dyp, dnormw = vjp1(dmix)
        return do, dpg, dyp, dx, dyb, dnormw, dgt, dlng, dlnb

    return _blocked("even_out_bwd", fn, (t // tm,), [dx1, o0, o1, p, ypool, x, y, normw, gt, lng, lnb, w],
                    [_tok(tm, d), _tok(tm, 512), _tok(tm, 512), _tok(tm, 512, 3), _tok(tm, 512),
                     _tok(tm, d), _tok(tm, d), _res(normw.shape), _res(gt.shape), _res(lng.shape), _res(lnb.shape),
                     _res(w.shape)],
                    [_sds((t, 512)), _sds((t, 512), MXU_DTYPE), _sds((t, 512)), _sds((t, d)), _sds((t, d), MXU_DTYPE),
                     _sds(normw.shape), _sds(gt.shape), _sds(lng.shape), _sds(lnb.shape)],
                    [_tok(tm, 512), _tok(tm, 512), _tok(tm, 512), _tok(tm, d), _tok(tm, d),
                     _res(normw.shape), _res(gt.shape), _res(lng.shape), _res(lnb.shape)],
                    acc=(5, 6, 7, 8), acc_axis=0)


def _odd_mix(ysc, z0, cg, cb):
    z = _silu(_layernorm(z0, cg, cb))
    return jnp.concatenate([ysc, z], axis=-1)


def _odd_out(ysc, z0, x, cg, cb, gt, lng, lnb, w, tm):
    t, d = x.shape
    tm = min(tm, t)

    def fn(ysc, z0, x, cg, cb, gt, lng, lnb, w):
        mix = _odd_mix(ysc, z0, cg, cb).astype(MXU_DTYPE)
        y = jnp.dot(mix, w, preferred_element_type=F32)
        return _residual_ln(y, x, gt, lng, lnb), mix, y

    return _blocked("odd_out", fn, (t // tm,), [ysc, z0, x, cg, cb, gt, lng, lnb, w],
                    [_tok(tm, 512), _tok(tm, 512), _tok(tm, d), _res(cg.shape), _res(cb.shape), _res(gt.shape),
                     _res(lng.shape), _res(lnb.shape), _res(w.shape)],
                    [_sds((t, d)), _sds((t, d), MXU_DTYPE), _sds((t, d))], [_tok(tm, d)] * 3)


def _odd_out_bwd(dx3, ysc, z0, x, y, cg, cb, gt, lng, lnb, w, tm):
    t, d = x.shape
    tm = min(tm, t)

    def fn(dx3, ysc, z0, x, y, cg, cb, gt, lng, lnb, w):
        _, vjp2 = jax.vjp(_residual_ln, y, x, gt, lng, lnb)
        dy, dx, dgt, dlng, dlnb = vjp2(dx3)
        dyb = dy.astype(MXU_DTYPE)
        dmix = _mm_nt(dyb, w)
        _, vjp1 = jax.vjp(_odd_mix, ysc, z0, cg, cb)
        dysc, dz0, dcg, dcb = vjp1(dmix)
        return dysc, dz0, dx, dyb, dcg, dcb, dgt, dlng, dlnb

    return _blocked("odd_out_bwd", fn, (t // tm,), [dx3, ysc, z0, x, y, cg, cb, gt, lng, lnb, w],
                    [_tok(tm, d), _tok(tm, 512), _tok(tm, 512), _tok(tm, d), _tok(tm, d), _res(cg.shape), _res(cb.shape),
                     _res(gt.shape), _res(lng.shape), _res(lnb.shape), _res(w.shape)],
                    [_sds((t, 512)), _sds((t, 512)), _sds((t, d)), _sds((t, d), MXU_DTYPE),
                     _sds(cg.shape), _sds(cb.shape), _sds(gt.shape), _sds(lng.shape), _sds(lnb.shape)],
                    [_tok(tm, 512), _tok(tm, 512), _tok(tm, d), _tok(tm, d),
                     _res(cg.shape), _res(cb.shape), _res(gt.shape), _res(lng.shape), _res(lnb.shape)],
                    acc=(4, 5, 6, 7, 8), acc_axis=0)


def _ffn_down(name, s, x, gt, lng, lnb, w, tm):
    t, d = x.shape
    f = s.shape[1]
    w, w_spec, _ = _weight_operand(w)
    tm = min(tm, t)

    def fn(s, x, gt, lng, lnb, w):
        y = jnp.dot(s, _rows_weight(w), preferred_element_type=F32)
        return _residual_ln(y, x, gt, lng, lnb), y

    return _blocked(name, fn, (t // tm,), [s, x, gt, lng, lnb, w],
                    [_tok(tm, f), _tok(tm, d), _res(gt.shape), _res(lng.shape), _res(lnb.shape), w_spec],
                    [_sds((t, d)), _sds((t, d))], [_tok(tm, d)] * 2)


def _ffn_down_bwd(name, dx2, x, y, gt, lng, lnb, w, tm):
    t, d = x.shape
    w, w_spec, w_blk = _weight_operand(w)
    f = math.prod(w_blk[:-1])
    tm = min(tm, t)

    def fn(dx2, x, y, gt, lng, lnb, w):
        _, vjp2 = jax.vjp(_residual_ln, y, x, gt, lng, lnb)
        dy, dx, dgt, dlng, dlnb = vjp2(dx2)
        dyb = dy.astype(MXU_DTYPE)
        return _mm_nt(dyb, _rows_weight(w)), dx, dyb, dgt, dlng, dlnb

    return _blocked(name, fn, (t // tm,), [dx2, x, y, gt, lng, lnb, w],
                    [_tok(tm, d), _tok(tm, d), _tok(tm, d), _res(gt.shape), _res(lng.shape), _res(lnb.shape), w_spec],
                    [_sds((t, f), MXU_DTYPE), _sds((t, d)), _sds((t, d), MXU_DTYPE), _sds(gt.shape), _sds(lng.shape), _sds(lnb.shape)],
                    [_tok(tm, f), _tok(tm, d), _tok(tm, d), _res(gt.shape), _res(lng.shape), _res(lnb.shape)],
                    acc=(3, 4, 5), acc_axis=0)


def _loss_and_grad(x4, target, tm):
    t, d = x4.shape
    tm = min(tm, t)

    def fn(y, tg):
        e = y - tg
        part = 0.5 * jnp.sum(jnp.mean(e * e, axis=-1, keepdims=True), axis=0, keepdims=True)
        return jnp.broadcast_to(part, (1, LANES)), e * (1.0 / d)

    return _blocked("loss", fn, (t // tm,), [x4, target], [_tok(tm, d), _tok(tm, d)],
                    [_sds((1, LANES)), _sds((t, d))], [_res((1, LANES)), _tok(tm, d)], acc=(0,), acc_axis=0)


def _matmul_tn(name, pairs, chip_cols=None):
    k = pairs[0][0].shape[1]
    n = pairs[0][1].shape[1]
    tk = k if k <= 1024 else k // 2
    if chip_cols is None:
        tn = 512 if n % 512 == 0 else (384 if n % 384 == 0 else 128)
        out_spec, out_shape = pl.BlockSpec((tk, tn), lambda i, j: (i, j)), _sds((k, n), MXU_DTYPE)
    else:
        tn = chip_cols
        tk = min(tk, 512) if tn > 1024 else tk
        out_spec = pl.BlockSpec((None, tk, tn), lambda i, j: (j, i, 0))
        out_shape = _sds((n // chip_cols, k, chip_cols), MXU_DTYPE)

    def body(*refs):
        acc = None
        for p in range(len(pairs)):
            term = lax.dot_general(refs[2 * p][...], refs[2 * p + 1][...], (((0,), (0,)), ((), ())), preferred_element_type=F32)
            acc = term if acc is None else acc + term
        refs[-1][...] = acc.astype(refs[-1].dtype)

    in_specs, args = [], []
    for a, b in pairs:
        t = a.shape[0]
        in_specs += [pl.BlockSpec((t, tk), lambda i, j: (0, i)), pl.BlockSpec((t, tn), lambda i, j: (0, j))]
        args += [a, b]
    return pl.pallas_call(
        body, name=name, grid=(k // tk, n // tn), in_specs=in_specs,
        out_specs=out_spec, out_shape=out_shape,
        compiler_params=pltpu.CompilerParams(dimension_semantics=("arbitrary", "arbitrary"),
                                             vmem_limit_bytes=VMEM_LIMIT_BYTES),
    )(*args)


def _chan(t, col_of):
    return pl.BlockSpec((t, LANES), lambda i: (0, col_of(i)))


def _wblk(k, col_of=lambda i: i):
    return pl.BlockSpec((k, LANES), lambda i: (0, col_of(i)))


SEQ_TILE = 256
GRID_SEQ_TILE = 512


def _load_ext(ref, r0, rows, halo, total):
    lo, hi = max(r0 - halo, 0), min(r0 + rows + halo, total)
    parts = []
    if lo > r0 - halo:
        parts.append(jnp.zeros((lo - (r0 - halo), ref.shape[1]), F32))
    parts.append(ref[lo:hi, :].astype(F32))
    if hi < r0 + rows + halo:
        parts.append(jnp.zeros((r0 + rows + halo - hi, ref.shape[1]), F32))
    return parts[0] if len(parts) == 1 else jnp.concatenate(parts, axis=0)


def _seq_stage(name, tile_fn, halo, grid_n, seqs, pars, outs, tile_rows=SEQ_TILE):
    total = seqs[0][0].shape[0]
    rows = min(tile_rows, total)
    ns, npar = len(seqs), len(pars)

    def body(*refs):
        seq_refs, par_refs, out_refs = refs[:ns], refs[ns:ns + npar], refs[ns + npar:]
        par_vals = [r[...] for r in par_refs]
        for r0 in range(0, total, rows):
            exts = [_load_ext(r, r0, rows, halo, total) for r in seq_refs]
            res = tile_fn(r0, *exts, *par_vals)
            for o_ref, v in zip(out_refs, res):
                o_ref[r0:r0 + rows, :] = v.astype(o_ref.dtype)

    return pl.pallas_call(
        body, name=name, grid=(grid_n,), in_specs=[s for _, s in seqs] + [s for _, s in pars],
        out_specs=[s for _, s in outs], out_shape=[o for o, _ in outs],
        compiler_params=pltpu.CompilerParams(dimension_semantics=("arbitrary",), vmem_limit_bytes=VMEM_LIMIT_BYTES),
    )(*[a for a, _ in seqs], *[a for a, _ in pars])


def _seq_stage_bwd(name, tile_fn, halo, grid_n, douts, seqs, pars, dseq_outs, dpar_outs, tile_rows=SEQ_TILE):
    total = seqs[0][0].shape[0]
    rows = min(tile_rows, total)
    groups = [d if isinstance(d, list) else [d] for d in douts]
    douts = [d for g in groups for d in g]
    starts = [sum(len(g) for g in groups[:k]) for k in range(len(groups))]
    nd, ns, npar = len(douts), len(seqs), len(pars)
    widths = [s.block_shape[-1] for _, s in seqs]

    def dtile(dout_refs, k, r0):
        terms = [d[r0:r0 + rows, :].astype(F32) for d in dout_refs[starts[k]:starts[k] + len(groups[k])]]
        return functools.reduce(lambda a, b: a + b, terms)

    def body(*refs):
        dout_refs, seq_refs, par_refs = refs[:nd], refs[nd:nd + ns], refs[nd + ns:nd + ns + npar]
        dseq_refs = refs[nd + ns + npar:nd + 2 * ns + npar]
        dpar_refs = refs[nd + 2 * ns + npar:nd + 2 * ns + 2 * npar]
        accs = refs[nd + 2 * ns + 2 * npar:]
        par_vals = [r[...] for r in par_refs]
        for a in accs:
            for r0 in range(0, total, rows):
                a[r0:r0 + rows, :] = jnp.zeros((rows, a.shape[1]), F32)
        dpars = [jnp.zeros(p.shape, F32) for p in par_vals]
        for r0 in range(0, total, rows):
            exts = [_load_ext(r, r0, rows, halo, total) for r in seq_refs]
            _, vjp = jax.vjp(functools.partial(tile_fn, r0), *exts, *par_vals)
            cts = vjp(tuple(dtile(dout_refs, k, r0) for k in range(len(groups))))
            lo, hi = max(r0 - halo, 0), min(r0 + rows + halo, total)
            for a, dx in zip(accs, cts[:ns]):
                a[lo:hi, :] += dx[lo - (r0 - halo):hi - (r0 - halo)]
            dpars = [acc + g for acc, g in zip(dpars, cts[ns:])]
        for o_ref, a in zip(dseq_refs, accs):
            for r0 in range(0, total, rows):
                o_ref[r0:r0 + rows, :] = a[r0:r0 + rows, :].astype(o_ref.dtype)
        for o_ref, g in zip(dpar_refs, dpars):
            o_ref[...] = g

    return pl.pallas_call(
        body, name=name, grid=(grid_n,),
        in_specs=[s for _, s in douts] + [s for _, s in seqs] + [s for _, s in pars],
        out_specs=[s for _, s in dseq_outs] + [s for _, s in dpar_outs],
        out_shape=[o for o, _ in dseq_outs] + [o for o, _ in dpar_outs],
        scratch_shapes=[pltpu.VMEM((total, w), F32) for w in widths],
        compiler_params=pltpu.CompilerParams(dimension_semantics=("arbitrary",), vmem_limit_bytes=VMEM_LIMIT_BYTES),
    )(*[a for a, _ in douts], *[a for a, _ in seqs], *[a for a, _ in pars])


def _qkv_tile(r0, p_ext, w):
    kind = pl.program_id(0)
    a = _silu(_conv5(p_ext, w))
    nrm = a * lax.rsqrt(jnp.sum(a * a, -1, keepdims=True) + RMS_EPS)
    scale = jnp.where(kind < GDN_HEADS, HEAD_DIM ** -0.5, 1.0).astype(F32)
    sel = (kind < 2 * GDN_HEADS).astype(F32)
    return (sel * (nrm * scale) + (1.0 - sel) * a,)


def _qkv_conv(name, p, w):
    t = p.shape[0]
    nb = w.shape[1] // LANES
    ident = lambda i: i
    return _seq_stage(name, _qkv_tile, HALO_SHORT, nb, [(p, _chan(t, ident))], [(w, _wblk(GDN_CONV))],
                      [(_sds((t, nb * LANES)), _chan(t, ident))])[0]


def _qkv_conv_bwd(name, dqkv0, dqkv1, p, w):
    t = p.shape[0]
    nb = w.shape[1] // LANES
    ident = lambda i: i
    return _seq_stage_bwd(name, _qkv_tile, HALO_SHORT, nb, [[(dqkv0, _chan(t, ident)), (dqkv1, _chan(t, ident))]],
                          [(p, _chan(t, ident))],
                          [(w, _wblk(GDN_CONV))], [(_sds((t, nb * LANES), MXU_DTYPE), _chan(t, ident))],
                          [(_sds(w.shape), _wblk(GDN_CONV))])


def _gates_fn(s, avec, dtvec):
    lane = lax.broadcasted_iota(jnp.int32, s.shape, 1)
    beta = jax.nn.sigmoid(s)
    g = -jnp.exp(avec) * jax.nn.softplus(s + dtvec)
    nh = 2 * GDN_HEADS
    return jnp.where(lane < nh, beta, jnp.where(lane < 2 * nh, g, 0.0))


def _gates(name, p, avec, dtvec, col):
    t = p.shape[0]
    tm = min(512, t)
    return _blocked(name, _gates_fn, (t // tm,), [p, avec, dtvec], [_tok(tm, LANES, col), _res(avec.shape), _res(dtvec.shape)],
                    [_sds((t, LANES))], [_tok(tm, LANES)])[0]


def _gates_bwd(name, dgb, p, avec, dtvec, col):
    t = p.shape[0]
    tm = min(512, t)

    def fn(dgb, s, avec, dtvec):
        _, vjp = jax.vjp(_gates_fn, s, avec, dtvec)
        return vjp(dgb)

    return _blocked(name, fn, (t // tm,), [dgb, p, avec, dtvec],
                    [_tok(tm, LANES), _tok(tm, LANES, col), _res(avec.shape), _res(dtvec.shape)],
                    [_sds((t, LANES), MXU_DTYPE), _sds(avec.shape), _sds(dtvec.shape)],
                    [_tok(tm, LANES), _res(avec.shape), _res(dtvec.shape)], acc=(1, 2), acc_axis=0)


def _make_pool_tile(total):
    def tile(r0, x_ext, scale, *ws):
        ys = []
        for g, win in enumerate(POOL_WINDOWS):
            xg = x_ext[:, g * LANES:(g + 1) * LANES]
            rows = xg.shape[0] - 2 * HALO_SHORT
            pos = r0 + lax.broadcasted_iota(jnp.int32, (rows, LANES), 0)
            lo = jnp.clip(pos - win // 2, 0, total)
            hi = jnp.clip(pos - win // 2 + win, 0, total)
            window_sum = _pool_sums[win](xg, jnp.ones((win, LANES), F32))
            pooled = window_sum / (hi - lo).astype(F32) - _center(xg, HALO_SHORT)
            ys.append(_mm(pooled, ws[g]))
        return (jnp.concatenate(ys, axis=-1) * scale,)

    return tile


def _pool_specs(p, pool_w, pool_scale, col0):
    t = p.shape[0]
    n = len(POOL_WINDOWS) * LANES
    seq = (p, pl.BlockSpec((t, n), lambda i: (0, col0)))
    pars = [(pool_scale, _res(pool_scale.shape))]
    pars += [(pool_w, pl.BlockSpec((None, LANES, LANES), lambda i, g=g: (g, 0, 0))) for g in range(len(POOL_WINDOWS))]
    return t, n, seq, pars


def _pool(p, pool_w, pool_scale, col0):
    t, n, seq, pars = _pool_specs(p, pool_w, pool_scale, col0)
    return _seq_stage("pool", _make_pool_tile(t), HALO_SHORT, 1, [seq], pars, [(_sds((t, n)), _res((t, n)))])[0]


def _pool_bwd(dy, p, pool_w, pool_scale, col0):
    t, n, seq, pars = _pool_specs(p, pool_w, pool_scale, col0)
    wspec = (_sds((LANES, LANES)), _res((LANES, LANES)))
    return _seq_stage_bwd("pool_bwd", _make_pool_tile(t), HALO_SHORT, 1, [(dy, _res(dy.shape))], [seq], pars,
                          [(_sds((t, n), MXU_DTYPE), _res((t, n)))],
                          [(_sds(pool_scale.shape), _res(pool_scale.shape))] + [wspec] * len(POOL_WINDOWS))


def _ffn_conv_tile(r0, a_ext, gt_ext, w):
    return (_silu(_conv3x3(a_ext, w)) * _center(gt_ext, HALO_GRID),)


def _ffn_conv(name, h, w9):
    t = h.shape[0]
    nb = w9.shape[1] // LANES
    ident = lambda i: i
    return _seq_stage(name, _ffn_conv_tile, HALO_GRID, nb, [(h, _chan(t, ident)), (h, _chan(t, lambda i: nb + i))],
                      [(w9, _wblk(9))], [(_sds((t, nb * LANES), MXU_DTYPE), _chan(t, ident))], GRID_SEQ_TILE)[0]


def _ffn_conv_bwd(name, ds, h, w9):
    t = h.shape[0]
    nb = w9.shape[1] // LANES
    ident = lambda i: i
    o = (_sds((t, nb * LANES), MXU_DTYPE), _chan(t, ident))
    return _seq_stage_bwd(name, _ffn_conv_tile, HALO_GRID, nb, [(ds, _chan(t, ident))],
                          [(h, _chan(t, ident)), (h, _chan(t, lambda i: nb + i))], [(w9, _wblk(9))],
                          [o, o], [(_sds(w9.shape), _wblk(9))], GRID_SEQ_TILE)


def _sconv_tile(r0, gb_ext, gc_ext, h_ext, w):
    return (_center(gb_ext, HALO_SHORT) * _conv3(gc_ext * h_ext, w),)


def _sconv_specs(p1, w3):
    t = p1.shape[0]
    nb = w3.shape[1] // LANES
    seqs = [(p1, _chan(t, lambda i: i)), (p1, _chan(t, lambda i: nb + i)), (p1, _chan(t, lambda i: 2 * nb + i))]
    return t, nb, seqs, [(w3, _wblk(SC_WIDTH))]


def _sconv(p1, w3):
    t, nb, seqs, pars = _sconv_specs(p1, w3)
    return _seq_stage("sconv", _sconv_tile, HALO_SHORT, nb, seqs, pars, [(_sds((t, nb * LANES)), _chan(t, lambda i: i))])[0]


def _sconv_bwd(dysc, p1, w3):
    t, nb, seqs, pars = _sconv_specs(p1, w3)
    o = (_sds((t, nb * LANES), MXU_DTYPE), _chan(t, lambda i: i))
    return _seq_stage_bwd("sconv_bwd", _sconv_tile, HALO_SHORT, nb, [(dysc, _chan(t, lambda i: i))], seqs, pars,
                          [o, o, o], [(_sds(w3.shape), _wblk(SC_WIDTH))])


def _conf_tile(r0, ga_ext, gb_ext, w):
    return (_conv31(ga_ext * jax.nn.sigmoid(gb_ext), w),)


def _conf_specs(p1, w31, blk0):
    t = p1.shape[0]
    nb = w31.shape[1] // LANES
    seqs = [(p1, _chan(t, lambda i: blk0 + i)), (p1, _chan(t, lambda i: blk0 + nb + i))]
    return t, nb, seqs, [(w31, _wblk(CF_WIDTH))]


def _confconv(p1, w31, blk0):
    t, nb, seqs, pars = _conf_specs(p1, w31, blk0)
    return _seq_stage("confconv", _conf_tile, HALO_CONF, nb, seqs, pars, [(_sds((t, nb * LANES)), _chan(t, lambda i: i))])[0]


def _confconv_bwd(dz0, p1, w31, blk0):
    t, nb, seqs, pars = _conf_specs(p1, w31, blk0)
    o = (_sds((t, nb * LANES), MXU_DTYPE), _chan(t, lambda i: i))
    return _seq_stage_bwd("confconv_bwd", _conf_tile, HALO_CONF, nb, [(dz0, _chan(t, lambda i: i))], seqs, pars,
                          [o, o], [(_sds(w31.shape), _wblk(CF_WIDTH))])


def _split_bf16(x):
    head = x.astype(jnp.bfloat16)
    return head, (x - head.astype(F32)).astype(jnp.bfloat16)


def _bdot(a, b, ca, cb, hi):
    dims = (((ca,), (cb,)), ((0,), (0,)))
    if not hi:
        return lax.dot_general(a.astype(MXU_DTYPE), b.astype(MXU_DTYPE), dims, preferred_element_type=F32)
    if MXU_DTYPE == F32:
        return lax.dot_general(a, b, dims, preferred_element_type=F32, precision=HIGHEST)
    (ah, al), (bh, bl) = _split_bf16(a), _split_bf16(b)
    dot = functools.partial(lax.dot_general, dimension_numbers=dims, preferred_element_type=F32)
    return dot(ah, bh) + (dot(ah, bl) + dot(al, bh))


def _make_bmm(hi):
    nn_i = lambda a, b: _bdot(a, b, 2, 1, hi)
    nt_i = lambda a, b: _bdot(a, b, 2, 2, hi)
    tn_i = lambda a, b: _bdot(a, b, 1, 1, hi)
    nn = _mm_vjp(nn_i, lambda g, b: nt_i(g, b), lambda a, g: tn_i(a, g))
    nt = _mm_vjp(nt_i, lambda g, b: nn_i(g, b), lambda a, g: tn_i(g, a))
    tn = _mm_vjp(tn_i, lambda g, b: nt_i(b, g), lambda a, g: nn_i(a, g))
    return nn, nt, tn


_bmm, _bmm_nt, _bmm_tn = _make_bmm(False)
_bmm_hi, _bmm_hi_nt, _bmm_hi_tn = _make_bmm(True)


def _unit_tri_inverse(a):
    c = a.shape[-1]
    eye = (lax.broadcasted_iota(jnp.int32, a.shape, 1) == lax.broadcasted_iota(jnp.int32, a.shape, 2)).astype(F32)
    levels = max(1, int(math.ceil(math.log2(c))) - 1)
    p = eye - a
    m = _bdot(a, a, 2, 1, True)
    for level in range(levels):
        p_next = p + _bdot(p, m, 2, 1, True)
        if level + 1 < levels:
            m = _bdot(m, m, 2, 1, True)
        p = p_next
    return p


@jax.custom_vjp
def _known_inverse(a, tinv):
    return tinv


def _known_inverse_fwd(a, tinv):
    return tinv, tinv


def _known_inverse_bwd(tinv, dt):
    da = -_bdot(_bdot(tinv, dt, 1, 1, True), tinv, 2, 2, True)
    return da, jnp.zeros_like(tinv)


_known_inverse.defvjp(_known_inverse_fwd, _known_inverse_bwd)


def _gdn_chunk(q, k, v, gcol, grow, bcol, s, tinv=None):
    n, c, _ = q.shape
    shape = (n, c, c)
    fwd_dir = lax.broadcasted_iota(jnp.int32, shape, 0) < n // 2
    i = lax.broadcasted_iota(jnp.int32, shape, 1)
    j = lax.broadcasted_iota(jnp.int32, shape, 2)
    order = jnp.where(fwd_dir, i - j, j - i)
    incl = order >= 0
    gc_col = jnp.sum(jnp.where(incl, grow, 0.0), axis=2, keepdims=True)
    gc_row = jnp.sum(jnp.where(order <= 0, gcol, 0.0), axis=1, keepdims=True)
    gtot = jnp.sum(grow, axis=2, keepdims=True)
    decay = jnp.exp(jnp.where(incl, gc_col - gc_row, -1e30))
    kb = k * bcol
    a = jnp.where(order > 0, _bmm_nt(kb, k) * decay, 0.0)
    tinv = _unit_tri_inverse(a) if tinv is None else _known_inverse(a, tinv)
    e_col = jnp.exp(gc_col)
    u = _bmm_hi(tinv, v * bcol)
    w = _bmm_hi(tinv, kb * e_col)
    k_dec = k * jnp.exp(gtot - gc_col)
    v_new = u - _bmm(w, s)
    attn = jnp.where(incl, _bmm_nt(q, k) * decay, 0.0)
    o = _bmm(q * e_col, s) + _bmm(attn, v_new)
    s_next = s * jnp.exp(gtot) + _bmm_tn(k_dec, v_new)
    return o, s_next, tinv


N_CHAINS = 2 * GDN_HEADS
QKV_W = 3 * GDN_HEADS * HEAD_DIM


def _scan_specs(nc, step):
    c = GDN_CHUNK
    seq = [pl.BlockSpec((c, QKV_W), lambda i: (step(i), 0)), pl.BlockSpec((c, QKV_W), lambda i: (nc - 1 - step(i), 0))]
    heads = [pl.BlockSpec((c, GDN_HEADS * HEAD_DIM), lambda i: (step(i), 0)),
             pl.BlockSpec((c, GDN_HEADS * HEAD_DIM), lambda i: (nc - 1 - step(i), 0))]
    colv = pl.BlockSpec((N_CHAINS, None, c, 1), lambda i: (0, step(i), 0, 0))
    rowv = pl.BlockSpec((N_CHAINS, None, 1, c), lambda i: (0, step(i), 0, 0))
    state = pl.BlockSpec((N_CHAINS, HEAD_DIM, HEAD_DIM), lambda i: (0, 0, 0))
    saved = pl.BlockSpec((N_CHAINS, None, HEAD_DIM, HEAD_DIM), lambda i: (0, step(i), 0, 0))
    inv = pl.BlockSpec((N_CHAINS, None, c, c), lambda i: (0, step(i), 0, 0))
    return seq, heads, colv, rowv, state, saved, inv


def _head_cols(h, part):
    lo = (part * GDN_HEADS + h) * HEAD_DIM
    return slice(lo, lo + HEAD_DIM)


def _chain_stack(x0_ref, x1_ref, part):
    return jnp.stack([(x0_ref if n < GDN_HEADS else x1_ref)[:, _head_cols(n % GDN_HEADS, part)] for n in range(N_CHAINS)])


def _scan_fwd(name, qkv, bcol, gcol, grow, s0):
    t = qkv.shape[0]
    nc = t // GDN_CHUNK
    seq, heads, colv, rowv, state, saved, inv = _scan_specs(nc, lambda i: i)

    def body(qkv0_ref, qkv1_ref, b_ref, gc_ref, gr_ref, s0_ref, o0_ref, o1_ref, save_ref, tinv_ref, fin_ref, s_ref):
        @pl.when(pl.program_id(0) == 0)
        def _():
            s_ref[...] = s0_ref[...]

        s = s_ref[...]
        save_ref[...] = s
        q, k, v = [_chain_stack(qkv0_ref, qkv1_ref, part) for part in range(3)]
        o, s_next, tinv = _gdn_chunk(q, k, v, gc_ref[...], gr_ref[...], b_ref[...], s)
        for n in range(N_CHAINS):
            d, h = divmod(n, GDN_HEADS)
            (o0_ref if d == 0 else o1_ref)[:, _head_cols(h, 0)] = o[n]
        tinv_ref[...] = tinv
        s_ref[...] = s_next
        fin_ref[...] = s_next

    hw = GDN_HEADS * HEAD_DIM
    return pl.pallas_call(
        body, name=name, grid=(nc,), in_specs=seq + [colv, colv, rowv, state],
        out_specs=heads + [saved, inv, state],
        out_shape=[_sds((t, hw)), _sds((t, hw)), _sds((N_CHAINS, nc, HEAD_DIM, HEAD_DIM)),
                   _sds((N_CHAINS, nc, GDN_CHUNK, GDN_CHUNK)), _sds((N_CHAINS, HEAD_DIM, HEAD_DIM))],
        scratch_shapes=[pltpu.VMEM((N_CHAINS, HEAD_DIM, HEAD_DIM), F32)],
        compiler_params=pltpu.CompilerParams(dimension_semantics=("arbitrary",), vmem_limit_bytes=VMEM_LIMIT_BYTES),
    )(qkv, qkv, bcol, gcol, grow, s0)


def _scan_bwd(name, do, qkv, bcol, gcol, grow, saved_s, saved_inv, ds_last):
    t = qkv.shape[0]
    nc = t // GDN_CHUNK
    c = GDN_CHUNK
    seq, heads, colv, rowv, state, saved, inv = _scan_specs(nc, lambda i: nc - 1 - i)

    def body(do0_ref, do1_ref, qkv0_ref, qkv1_ref, b_ref, gc_ref, gr_ref, s_ref, tinv_ref, dsl_ref,
             dx0_ref, dx1_ref, db_ref, dgc_ref, dgr_ref, ds0_ref, ds_ref):
        @pl.when(pl.program_id(0) == 0)
        def _():
            ds_ref[...] = dsl_ref[...]

        tinv = tinv_ref[...]
        q, k, v = [_chain_stack(qkv0_ref, qkv1_ref, part) for part in range(3)]
        do = jnp.stack([(do0_ref if n < GDN_HEADS else do1_ref)[:, _head_cols(n % GDN_HEADS, 0)] for n in range(N_CHAINS)])
        _, vjp = jax.vjp(lambda *a: _gdn_chunk(*a, tinv=tinv)[:2], q, k, v, gc_ref[...], gr_ref[...], b_ref[...], s_ref[...])
        dq, dk, dv, dgc, dgr, db, ds = vjp((do, ds_ref[...]))
        for n in range(N_CHAINS):
            d, h = divmod(n, GDN_HEADS)
            dx_ref = dx0_ref if d == 0 else dx1_ref
            dx_ref[:, _head_cols(h, 0)] = dq[n]
            dx_ref[:, _head_cols(h, 1)] = dk[n]
            dx_ref[:, _head_cols(h, 2)] = dv[n]
        db_ref[...] = db
        dgc_ref[...] = dgc
        dgr_ref[...] = dgr
        ds_ref[...] = ds
        ds0_ref[...] = ds

    vec_c = _sds((N_CHAINS, nc, c, 1))
    vec_r = _sds((N_CHAINS, nc, 1, c))
    return pl.pallas_call(
        body, name=name, grid=(nc,),
        in_specs=heads + seq + [colv, colv, rowv, saved, inv, state],
        out_specs=seq + [colv, colv, rowv, state],
        out_shape=[_sds((t, QKV_W)), _sds((t, QKV_W)), vec_c, vec_c, vec_r, _sds((N_CHAINS, HEAD_DIM, HEAD_DIM))],
        scratch_shapes=[pltpu.VMEM((N_CHAINS, HEAD_DIM, HEAD_DIM), F32)],
        compiler_params=pltpu.CompilerParams(dimension_semantics=("arbitrary",), vmem_limit_bytes=VMEM_LIMIT_BYTES),
    )(do, do, qkv, qkv, bcol, gcol, grow, saved_s, saved_inv, ds_last)


def _gate_layouts(gb):
    t = gb.shape[0]
    nc = t // GDN_CHUNK
    nh = GDN_HEADS

    def by_scan_position(a):
        a = a.T.reshape(2 * nh, nc, GDN_CHUNK)
        return jnp.concatenate([a[:nh], a[nh:, ::-1]], axis=0)

    beta = by_scan_position(gb[:, :2 * nh])
    g = by_scan_position(gb[:, 2 * nh:4 * nh])
    return beta[..., None], g[..., None], g[:, :, None, :]


def _gate_layouts_bwd(dbcol, dgcol, dgrow):
    n2, nc, c, _ = dbcol.shape
    nh = n2 // 2
    t = nc * c

    def by_token(a):
        return jnp.concatenate([a[:nh], a[nh:, ::-1]], axis=0).reshape(n2, t).T

    dbeta = by_token(dbcol[..., 0])
    dg = by_token(dgcol[..., 0] + dgrow[:, :, 0, :])
    return jnp.concatenate([dbeta, dg, jnp.zeros((t, LANES - 2 * n2), F32)], axis=1)


def kernel(x, c, ctx, c_ctx, ada_w, ada_b, ln_g, ln_b, even_w_in, even_w_out, gdn_conv_w, gdn_a_log, gdn_dt_bias, gdn_norm_w, pool_w, pool_scale, odd_w_in, odd_w_out, sconv_w, conf_conv_w, conf_ln_g, conf_ln_b, ffn_w_up, ffn_conv_w, ffn_w_down, loss_target, m_c_ctx, m_ada_w, m_ada_b, m_ln_g, m_ln_b, m_even_w_in, m_even_w_out, m_gdn_conv_w, m_gdn_a_log, m_gdn_dt_bias, m_gdn_norm_w, m_pool_w, m_pool_scale, m_odd_w_in, m_odd_w_out, m_sconv_w, m_conf_conv_w, m_conf_ln_g, m_conf_ln_b, m_ffn_w_up, m_ffn_conv_w, m_ffn_w_down, v_c_ctx, v_ada_w, v_ada_b, v_ln_g, v_ln_b, v_even_w_in, v_even_w_out, v_gdn_conv_w, v_gdn_a_log, v_gdn_dt_bias, v_gdn_norm_w, v_pool_w, v_pool_scale, v_odd_w_in, v_odd_w_out, v_sconv_w, v_conf_conv_w, v_conf_ln_g, v_conf_ln_b, v_ffn_w_up, v_ffn_conv_w, v_ffn_w_down):
    names = ['c_ctx', 'ada_w', 'ada_b', 'ln_g', 'ln_b', 'even_w_in', 'even_w_out', 'gdn_conv_w', 'gdn_a_log',
             'gdn_dt_bias', 'gdn_norm_w', 'pool_w', 'pool_scale', 'odd_w_in', 'odd_w_out', 'sconv_w', 'conf_conv_w',
             'conf_ln_g', 'conf_ln_b', 'ffn_w_up', 'ffn_conv_w', 'ffn_w_down']
    loc = locals()
    wts = {n: loc[n] for n in names}
    mom = {n: loc['m_' + n] for n in names}
    var = {n: loc['v_' + n] for n in names}

    ix, iy, ic = lax.axis_index("x"), lax.axis_index("y"), lax.axis_index("c")
    chip = 2 * ix + iy
    dev = 2 * chip + ic
    d = D_MODEL
    x0, ctx0, tgt = x[0], ctx[0], loss_target[0]
    t, tc = x0.shape[0], ctx0.shape[0]
    depth = ada_w.shape[0]
    n_ada = ada_w.shape[2]

    small_sharded = [gdn_conv_w, sconv_w, conf_conv_w, ffn_conv_w, ln_g, ln_b]
    s1_items = [c] + small_sharded
    s1 = _pack_flat(s1_items, 8 * LANES, F32).reshape(-1, LANES)
    g1 = _all_gather8("gather_small_in", s1, True).reshape(8, -1)
    c_all = g1[:, :d]
    per_chip = [_unpack_flat(g1[2 * k], [a.shape for a in s1_items])[1:] for k in range(4)]
    gdn_conv_f, sconv_f, conf_conv_f, ffn_conv_f, ln_g_f, ln_b_f = [
        jnp.concatenate([per_chip[k][i] for k in range(4)], axis=-1) for i in range(len(small_sharded))]
    c16 = jnp.concatenate([c_all, c_ctx[None], jnp.zeros((7, d), F32)], axis=0)

    mod_part = _mod_rows(c16, ada_w)
    g2 = _all_gather8("gather_mod", mod_part.reshape(-1, LANES), True).reshape(4, 2, depth, 16, n_ada)[:, 0]
    mod_all = jnp.transpose(g2, (1, 2, 0, 3)).reshape(depth, 16, 4 * n_ada) + ada_b[:, None, :]
    mod_me = lax.dynamic_index_in_dim(mod_all, dev, axis=1, keepdims=False).reshape(depth, 6, 1, d)
    sh_c, sc_c = mod_all[0, 8, :d][None], mod_all[0, 8, d:2 * d][None]

    flat_names = ['even_w_in', 'even_w_out']
    flat_shapes = [wts[n].shape for n in flat_names]
    half_mult = 2 * 16 * LANES
    rh = -(-sum(math.prod(s) for s in flat_shapes) // half_mult) * half_mult // (2 * LANES)
    flat_pack = _pack_flat([wts[n] for n in flat_names], half_mult, MXU_DTYPE).reshape(2, rh, LANES)
    wg_flat = _all_gather8("gather_weights_even", lax.dynamic_index_in_dim(flat_pack, ic, axis=0, keepdims=False), True).reshape(4, -1)
    per_chip = [_unpack_flat(wg_flat[k], flat_shapes) for k in range(4)]
    n_even = 4 * even_w_in.shape[1]
    n_even_pad = -(-n_even // LANES) * LANES
    w_in = jnp.concatenate([per_chip[k][0] for k in range(4)] + [jnp.zeros((d, n_even_pad - n_even), MXU_DTYPE)], axis=1)
    w_out = jnp.concatenate([per_chip[k][1] for k in range(4)], axis=0)

    def row_half(a, core, axis):
        n = a.shape[axis] // 2
        return lax.dynamic_slice_in_dim(a, core * n, n, axis=axis)

    def layer_of(a, core):
        return lax.dynamic_index_in_dim(a, core, axis=0, keepdims=False)

    mine = [row_half(odd_w_in, ic, 0), row_half(odd_w_out, ic, 0), layer_of(ffn_w_up, ic), layer_of(ffn_w_down, ic)]
    late_sems, late_mine, late_lands, late_token = _chip_send_start(
        "gather_late_start", [m.astype(MXU_DTYPE) for m in mine], [wg_flat, mod_all])
    scal_blk = (n_even // LANES)
    n_scal = n_even - scal_blk * LANES

    def mod(layer, k):
        return mod_me[layer, k]

    avec = jnp.zeros((1, LANES), F32).at[0, n_scal // 2:n_scal].set(gdn_a_log.reshape(-1))
    dtvec = jnp.zeros((1, LANES), F32).at[0, n_scal // 2:n_scal].set(gdn_dt_bias.reshape(-1))
    normw = gdn_norm_w[None]
    pscale = pool_scale[None]
    cg, cb = conf_ln_g[None], conf_ln_b[None]
    lng = lambda l, k: ln_g_f[l, k][None]
    lnb = lambda l, k: ln_b_f[l, k][None]
    convw9 = ffn_conv_f.reshape(depth, 9, -1)
    nqkv = gdn_conv_f.shape[1]

    shift0 = mod(0, 0) + late_token[:1, :1]
    p, ub0 = _inproj("even_in", x0, shift0, mod(0, 1), w_in, 512)
    pc, ucb = _inproj("even_in_ctx", ctx0, sh_c, sc_c, w_in, 256)
    qkv = _qkv_conv("qkv_conv", p, gdn_conv_f)
    qkv_c = _qkv_conv("qkv_conv_ctx", pc, gdn_conv_f)
    gb = _gates("gates", p, avec, dtvec, scal_blk)
    gb_c = _gates("gates_ctx", pc, avec, dtvec, scal_blk)
    lay = _gate_layouts(gb)
    lay_c = _gate_layouts(gb_c)
    s_zero = jnp.zeros((2 * GDN_HEADS, HEAD_DIM, HEAD_DIM), F32)
    _, _, save_c, inv_c, s_ctx = _scan_fwd("scan_ctx", qkv_c, *lay_c, s_zero)
    o0, o1, save_l, inv_l, _ = _scan_fwd("scan", qkv, *lay, s_ctx)
    pool_blk = (nqkv + GDN_HEADS * HEAD_DIM) // 512
    ypool = _pool(p, pool_w, pscale, pool_blk)
    x1, mix0, y0 = _even_out(o0, o1, p, ypool, x0, normw, mod(0, 2), lng(0, 0), lnb(0, 0), w_out, 256)

    late_mine, late_lands = _chip_send_wait("gather_late_wait", late_sems, late_mine, late_lands, x1)
    own4 = [lax.dynamic_update_slice(land, m[None], (chip,) + (0,) * m.ndim) for land, m in zip(late_lands, late_mine)]
    got4 = _sibling_exchange("gather_late_exchange", own4)

    def of_core(i, core):
        return jnp.where(ic == core, own4[i], got4[i])

    w_oin = jnp.concatenate([of_core(0, 0), of_core(0, 1)], axis=1)
    w_oout = jnp.concatenate([of_core(1, 0), of_core(1, 1)], axis=1).reshape(-1, d)
    w_up = [of_core(2, l) for l in range(depth)]
    w_down = [of_core(3, l) for l in range(depth)]

    def ffn_fwd(l, xin):
        h, ub = _inproj(f"ffn_up{l}", xin, mod(l, 3), mod(l, 4), w_up[l], 256)
        s = _ffn_conv(f"ffn_conv{l}", h, convw9[l])
        xo, y = _ffn_down(f"ffn_down{l}", s, xin, mod(l, 5), lng(l, 1), lnb(l, 1), w_down[l], 256)
        return xo, (h, ub, s, y)

    x2, ffn0 = ffn_fwd(0, x1)
    p1, ub1 = _inproj("odd_in", x2, mod(1, 0), mod(1, 1), w_oin, 512)
    nsc = sconv_f.shape[1] // LANES
    ysc = _sconv(p1, sconv_f)
    z0 = _confconv(p1, conf_conv_f, 3 * nsc)
    x3, mix1, y1 = _odd_out(ysc, z0, x2, cg, cb, mod(1, 2), lng(1, 0), lnb(1, 0), w_oout, 256)
    x4, ffn1 = ffn_fwd(1, x3)
    loss_part, dx4 = _loss_and_grad(x4, tgt, 512)
    loss = lax.psum(loss_part[0, 0], ("x", "y", "c"))

    dmod = [[None] * 6 for _ in range(depth)]
    dlng = [[None, None] for _ in range(depth)]
    dlnb = [[None, None] for _ in range(depth)]
    gbig = {}
    dconv9 = [None] * depth

    def ffn_bwd(l, dxo, xin, saved):
        h, ub, s, y = saved
        ds, dxa, dyb, dgt, dg_, db_ = _ffn_down_bwd(f"ffn_down_bwd{l}", dxo, xin, y, mod(l, 5), lng(l, 1), lnb(l, 1), w_down[l], 256)
        da, dgate, dw9 = _ffn_conv_bwd(f"ffn_conv_bwd{l}", ds, h, convw9[l])
        dh = jnp.concatenate([da, dgate], axis=1)
        dxin, dsh, dsc = _inproj_bwd(f"ffn_up_bwd{l}", dh, xin, mod(l, 3), mod(l, 4), w_up[l], dxa, 256)
        dmod[l][3], dmod[l][4], dmod[l][5] = dsh, dsc, dgt
        dlng[l][1], dlnb[l][1] = dg_, db_
        dconv9[l] = dw9
        dw_up = _matmul_tn(f"dw_up{l}", [(ub, dh)], chip_cols=ffn_w_up.shape[2])
        dw_down = _matmul_tn(f"dw_down{l}", [(s, dyb)]).reshape(4, -1, d)
        return dxin, dw_up, dw_down

    dx3, dwu1, dwd1 = ffn_bwd(1, dx4, x3, ffn1)
    dysc, dz0, dx2a, dyb1, dcg, dcb, dgt, dg_, db_ = _odd_out_bwd(dx3, ysc, z0, x2, y1, cg, cb, mod(1, 2), lng(1, 0), lnb(1, 0), w_oout, 256)
    dmod[1][2], dlng[1][0], dlnb[1][0] = dgt, dg_, db_
    d_gb, d_gc, d_h, dsconv = _sconv_bwd(dysc, p1, sconv_f)
    d_ga, d_gbb, dconf = _confconv_bwd(dz0, p1, conf_conv_f, 3 * nsc)
    dp1 = jnp.concatenate([d_gb, d_gc, d_h, d_ga, d_gbb], axis=1)
    dx2, dsh, dsc = _inproj_bwd("odd_in_bwd", dp1, x2, mod(1, 0), mod(1, 1), w_oin, dx2a, 512)
    dmod[1][0], dmod[1][1] = dsh, dsc
    dw_oout = _matmul_tn("dw_oout", [(mix1, dyb1)]).reshape(4, -1, d)
    dw_oin = _matmul_tn("dw_oin", [(ub1, dp1)], chip_cols=odd_w_in.shape[1])

    dx1, dwu0, dwd0 = ffn_bwd(0, dx2, x1, ffn0)

    do, dpg, dypool, dx0a, dyb0, dnormw, dgt, dg_, db_ = _even_out_bwd(dx1, o0, o1, p, ypool, x0, y0, normw, mod(0, 2), lng(0, 0), lnb(0, 0), w_out, 256)
    dmod[0][2], dlng[0][0], dlnb[0][0] = dgt, dg_, db_
    pool_cts = _pool_bwd(dypool, p, pool_w, pscale, pool_blk)
    dpp, dpool_scale, dpool_w = pool_cts[0], pool_cts[1], jnp.stack(pool_cts[2:])
    dqkv0, dqkv1, dbcol, dgcol, dgrow, ds0 = _scan_bwd("scan_bwd", do, qkv, *lay, save_l, inv_l, s_zero)
    zero_do = jnp.zeros((tc, GDN_HEADS * HEAD_DIM), F32)
    dqkv0_c, dqkv1_c, dbcol_c, dgcol_c, dgrow_c, _ = _scan_bwd("scan_bwd_ctx", zero_do, qkv_c, *lay_c, save_c, inv_c, ds0)
    dgb = _gate_layouts_bwd(dbcol, dgcol, dgrow)
    dgb_c = _gate_layouts_bwd(dbcol_c, dgcol_c, dgrow_c)
    dps, davec, ddtvec = _gates_bwd("gates_bwd", dgb, p, avec, dtvec, scal_blk)
    dps_c, davec_c, ddtvec_c = _gates_bwd("gates_bwd_ctx", dgb_c, pc, avec, dtvec, scal_blk)
    dpqkv, dconv5 = _qkv_conv_bwd("qkv_conv_bwd", dqkv0, dqkv1, p, gdn_conv_f)
    dpqkv_c, dconv5_c = _qkv_conv_bwd("qkv_conv_bwd_ctx", dqkv0_c, dqkv1_c, pc, gdn_conv_f)
    dp = jnp.concatenate([dpqkv, dpg, dpp, dps], axis=1)
    dpc = jnp.concatenate([dpqkv_c, jnp.zeros((tc, n_even_pad - nqkv - LANES), MXU_DTYPE), dps_c], axis=1)
    grad_x, dsh, dsc = _inproj_bwd("even_in_bwd", dp, x0, mod(0, 0), mod(0, 1), w_in, dx0a, 512)
    dmod[0][0], dmod[0][1] = dsh, dsc
    _, dsh_c, dsc_c = _inproj_bwd("even_in_bwd_ctx", dpc, ctx0, sh_c, sc_c, w_in, None, 256)
    dw_in = _matmul_tn("dw_in", [(ub0, dp), (ucb, dpc)])
    dw_out = _matmul_tn("dw_out", [(mix0, dyb0)])

    n_in, n_out = even_w_in.shape[1], even_w_out.shape[0]
    gflat = jnp.stack([_pack_flat([dw_in[:, k * n_in:(k + 1) * n_in], dw_out[k * n_out:(k + 1) * n_out]], half_mult,
                                  MXU_DTYPE).reshape(2, rh, LANES) for k in range(4)], axis=1)
    halves = [
        (gflat[0], gflat[1]),
        (dw_oin[:, :d // 2], dw_oin[:, d // 2:]),
        (dw_oout[:, :odd_w_out.shape[0] // 2], dw_oout[:, odd_w_out.shape[0] // 2:]),
        (dwu0, dwu1),
        (dwd0, dwd1),
    ]
    keeps = [jnp.where(ic == 0, h0, h1) for h0, h1 in halves]
    gives = [jnp.where(ic == 0, h1, h0) for h0, h1 in halves]
    gots = _sibling_exchange("grad_pair_exchange", gives)
    pairs = [_sum_pair(f"grad_pair_sum{i}", kp, gt_, MXU_DTYPE) for i, (kp, gt_) in enumerate(zip(keeps, gots))]
    parts = _chip_scatter("grad_chip_scatter", pairs)
    sums = [_sum_axis1(f"grad_chip_sum{i}", p_) for i, p_ in enumerate(parts)]
    g_even_in, g_even_out = _unpack_flat(sums[0].reshape(-1), flat_shapes)
    g_shards = {'even_w_in': g_even_in, 'even_w_out': g_even_out, 'odd_w_in': sums[1].reshape(odd_w_in.shape),
                'odd_w_out': sums[2].reshape(odd_w_out.shape), 'ffn_w_up': sums[3], 'ffn_w_down': sums[4]}

    dmod_rows = jnp.stack([jnp.concatenate(dmod[l], axis=1)[0] for l in range(depth)])
    dmod_c = jnp.concatenate([dsh_c[0], dsc_c[0], jnp.zeros((4 * d,), F32)])
    dmod_c_rows = jnp.stack([dmod_c] + [jnp.zeros_like(dmod_c)] * (depth - 1))
    small_g = {
        'ln_g': jnp.stack([jnp.stack([dlng[l][k][0] for k in range(2)]) for l in range(depth)]),
        'ln_b': jnp.stack([jnp.stack([dlnb[l][k][0] for k in range(2)]) for l in range(depth)]),
        'gdn_conv_w': dconv5 + dconv5_c,
        'gdn_a_log': (davec + davec_c)[0, n_scal // 2:n_scal].reshape(gdn_a_log.shape),
        'gdn_dt_bias': (ddtvec + ddtvec_c)[0, n_scal // 2:n_scal].reshape(gdn_dt_bias.shape),
        'gdn_norm_w': dnormw[0], 'pool_w': dpool_w, 'pool_scale': dpool_scale[0],
        'sconv_w': dsconv, 'conf_conv_w': dconf, 'conf_ln_g': dcg[0], 'conf_ln_b': dcb[0],
        'ffn_conv_w': jnp.stack(dconv9).reshape(depth, 3, 3, -1),
    }
    small_names = list(small_g)
    s3_items = [dmod_rows, dmod_c_rows] + [small_g[n] for n in small_names]
    s3 = _pack_flat(s3_items, 8 * LANES, F32).reshape(-1, LANES)
    g3 = _all_gather8("gather_small_grads", s3, True).reshape(8, -1, LANES)
    tot3 = _sum_axis1("small_grad_sum", g3[None]).reshape(-1)
    tot_items = _unpack_flat(tot3, [a.shape for a in s3_items])
    dmod_sum, dmod_c_sum = tot_items[0], tot_items[1]
    small_tot = dict(zip(small_names, tot_items[2:]))
    grad_ada_b = dmod_sum + dmod_c_sum
    g3f = g3.reshape(8, -1)
    rows_all = g3f[:, :depth * 6 * d].reshape(8, depth, 6 * d)
    cols = lax.dynamic_slice_in_dim(rows_all, chip * n_ada, n_ada, axis=2)
    crow = lax.dynamic_slice_in_dim(dmod_c_sum, chip * n_ada, n_ada, axis=1)
    dmod16 = jnp.concatenate([jnp.transpose(cols, (1, 0, 2)), crow[:, None, :], jnp.zeros((depth, 7, n_ada), F32)], axis=1)
    grad_ada_w, dsil = _ada_grads(c16, dmod16, ada_w)
    s4 = jnp.concatenate([dsil[0, 8][None], jnp.zeros((7, d), F32)], axis=0).reshape(-1, LANES)
    g4 = _all_gather8("gather_cctx", s4, True).reshape(8, 8, d)
    grad_c_ctx = _cctx_grad(g4[0::2, 0][:, None, :], c_ctx[None])[0]

    def my_cols(a, n):
        return lax.dynamic_slice_in_dim(a, chip * n, n, axis=a.ndim - 1)

    grads = dict(g_shards)
    grads['c_ctx'] = grad_c_ctx
    grads['ada_w'] = grad_ada_w
    grads['ada_b'] = grad_ada_b
    for n in ['ln_g', 'ln_b', 'gdn_conv_w', 'sconv_w', 'conf_conv_w', 'ffn_conv_w']:
        grads[n] = my_cols(small_tot[n], wts[n].shape[-1])
    for n in ['gdn_a_log', 'gdn_dt_bias', 'gdn_norm_w', 'pool_w', 'pool_scale', 'conf_ln_g', 'conf_ln_b']:
        grads[n] = small_tot[n]

    delta, new_m, new_v = {}, {}, {}
    big_adam = list(g_shards) + ['ada_w']
    for n in big_adam:
        shp = wts[n].shape
        as2d = lambda a: a.reshape(-1, shp[-1])
        dl, nm, nv = _adamw("adamw_" + n, as2d(wts[n]), as2d(grads[n]), as2d(mom[n]), as2d(var[n]))
        delta[n], new_m[n], new_v[n] = dl.reshape(shp), nm.reshape(shp), nv.reshape(shp)
    small_adam = [n for n in names if n not in big_adam]
    packs = [_pack_flat([src[n] for n in small_adam], 8 * LANES, F32).reshape(-1, LANES) for src in (wts, grads, mom, var)]
    outs = _adamw("adamw_small", *packs)
    shapes = [wts[n].shape for n in small_adam]
    for res, dst in zip(outs, (delta, new_m, new_v)):
        for n, a in zip(small_adam, _unpack_flat(res.reshape(-1), shapes)):
            dst[n] = a

    return (loss, grad_x[None], *[grads[n] for n in names], *[delta[n] for n in names],
            *[new_m[n] for n in names], *[new_v[n] for n in names])
```

```python
import functools
import math

import jax
import jax.numpy as jnp
from jax import lax
from jax.experimental import pallas as pl
from jax.experimental.pallas import tpu as pltpu

F32 = jnp.float32
MXU_DTYPE = jnp.bfloat16
HIGHEST = lax.Precision.HIGHEST
MESH = pl.DeviceIdType.MESH

D_MODEL = 1024
GRID_W = 64
GDN_HEADS = 4
HEAD_DIM = 128
GDN_CHUNK = 64
POOL_WINDOWS = (2, 4, 8, 16)
GDN_CONV = 5
SC_WIDTH = 3
CF_WIDTH = 31
ALPHA = 4.0 ** 0.25
LN_EPS = 1e-5
RMS_EPS = 1e-6
LANES = 128
VMEM_LIMIT_BYTES = 58 * 1024 * 1024

ADAM_LR, ADAM_B1, ADAM_B2, ADAM_EPS, ADAM_WD, ADAM_STEP = 0.001, 0.9, 0.999, 1e-08, 0.01, 10


def _blocked(name, fn, grid, ins, in_specs, out_shapes, out_specs, acc=(), acc_axis=None, scratch=()):
    n_in = len(ins)
    n_out = len(out_shapes)

    def body(*refs):
        vals = [r[...] for r in refs[:n_in]]
        res = fn(*vals, *refs[n_in + n_out:])
        if not isinstance(res, (tuple, list)):
            res = (res,)
        for k, (r, v) in enumerate(zip(refs[n_in:n_in + n_out], res)):
            if k in acc:
                first = pl.program_id(acc_axis) == 0

                @pl.when(first)
                def _(r=r, v=v):
                    r[...] = v.astype(r.dtype)

                @pl.when(jnp.logical_not(first))
                def _(r=r, v=v):
                    r[...] += v.astype(r.dtype)
            else:
                r[...] = v.astype(r.dtype)

    return pl.pallas_call(
        body, name=name, grid=grid, in_specs=in_specs, out_specs=out_specs, out_shape=out_shapes,
        scratch_shapes=list(scratch),
        compiler_params=pltpu.CompilerParams(dimension_semantics=("arbitrary",) * len(grid),
                                             vmem_limit_bytes=VMEM_LIMIT_BYTES),
    )(*ins)


def _sds(shape, dtype=F32):
    return jax.ShapeDtypeStruct(tuple(shape), dtype)


def _tok(tm, n, col=0):
    return pl.BlockSpec((tm, n), lambda t: (t, col))


def _res(shape):
    nd = len(shape)
    return pl.BlockSpec(tuple(shape), lambda t: (0,) * nd)


def _silu(x):
    return x * jax.nn.sigmoid(x)


def _layernorm(r, g, b):
    mu = jnp.mean(r, -1, keepdims=True)
    d = r - mu
    var = jnp.mean(d * d, -1, keepdims=True)
    return d * lax.rsqrt(var + LN_EPS) * g + b


def _mm_nn_impl(a, b):
    return jnp.dot(a.astype(MXU_DTYPE), b.astype(MXU_DTYPE), preferred_element_type=F32)


def _mm_nt_impl(a, b):
    return lax.dot_general(a.astype(MXU_DTYPE), b.astype(MXU_DTYPE), (((1,), (1,)), ((), ())), preferred_element_type=F32)


def _mm_tn_impl(a, b):
    return lax.dot_general(a.astype(MXU_DTYPE), b.astype(MXU_DTYPE), (((0,), (0,)), ((), ())), preferred_element_type=F32)


def _mm_vjp(mm, mm_da, mm_db):
    f = jax.custom_vjp(mm)
    f.defvjp(lambda a, b: (mm(a, b), (a, b)), lambda res, g: (mm_da(g, res[1]), mm_db(res[0], g)))
    return f


_mm = _mm_vjp(_mm_nn_impl, lambda g, b: _mm_nt_impl(g, b), lambda a, g: _mm_tn_impl(a, g))
_mm_nt = _mm_vjp(_mm_nt_impl, lambda g, b: _mm_nn_impl(g, b), lambda a, g: _mm_tn_impl(g, a))
_mm_tn = _mm_vjp(_mm_tn_impl, lambda g, b: _mm_nt_impl(b, g), lambda a, g: _mm_nn_impl(a, g))


def _row(w, k):
    rows = lax.broadcasted_iota(jnp.int32, w.shape, 0)
    return jnp.sum(jnp.where(rows == k, w, 0.0), axis=0, keepdims=True)


def _col_mask(shape, dc):
    col = lax.broadcasted_iota(jnp.int32, shape, 0) & (GRID_W - 1)
    return (col + dc >= 0) & (col + dc < GRID_W)


def _center(x_ext, halo):
    return x_ext[halo:x_ext.shape[0] - halo]


def _make_dwconv(taps, halo):
    assert all(abs(s) <= halo for s, _ in taps)

    def shifted(x_ext, s):
        r = x_ext if s == 0 else pltpu.roll(x_ext, (-s) % x_ext.shape[0], 0)
        return _center(r, halo)

    @jax.custom_vjp
    def conv(x_ext, w):
        acc = None
        for k, (s, dc) in enumerate(taps):
            r = shifted(x_ext, s)
            if dc != 0:
                r = jnp.where(_col_mask(r.shape, dc), r, 0.0)
            term = r * _row(w, k)
            acc = term if acc is None else acc + term
        return acc

    def fwd(x_ext, w):
        return conv(x_ext, w), (x_ext, w)

    def bwd(res, dy):
        x_ext, w = res
        n = x_ext.shape[0]
        rows = lax.broadcasted_iota(jnp.int32, w.shape, 0)
        pad = jnp.zeros((halo, dy.shape[1]), F32)
        dx = None
        dw = jnp.zeros(w.shape, F32)
        for k, (s, dc) in enumerate(taps):
            dym = dy if dc == 0 else jnp.where(_col_mask(dy.shape, dc), dy, 0.0)
            dw = dw + jnp.where(rows == k, jnp.sum(dym * shifted(x_ext, s), axis=0, keepdims=True), 0.0)
            t = jnp.concatenate([pad, dym * _row(w, k), pad], axis=0)
            if s != 0:
                t = pltpu.roll(t, s % n, 0)
            dx = t if dx is None else dx + t
        return dx, dw

    conv.defvjp(fwd, bwd)
    return conv


def _taps_1d(width):
    return tuple((k - width // 2, 0) for k in range(width))


HALO_SHORT = 8
HALO_CONF = 16
HALO_GRID = 72
_conv5 = _make_dwconv(_taps_1d(GDN_CONV), HALO_SHORT)
_conv3 = _make_dwconv(_taps_1d(SC_WIDTH), HALO_SHORT)
_conv31 = _make_dwconv(_taps_1d(CF_WIDTH), HALO_CONF)
_conv3x3 = _make_dwconv(tuple((dr * GRID_W + dc, dc) for dr in (-1, 0, 1) for dc in (-1, 0, 1)), HALO_GRID)
_pool_sums = {w: _make_dwconv(tuple((s, 0) for s in range(-(w // 2), w - w // 2)), HALO_SHORT) for w in POOL_WINDOWS}


def _all_gather8(name, blk, in_vmem):
    m_per, n = blk.shape
    space = pltpu.VMEM if in_vmem else pl.ANY

    def body(x_ref, out_ref, send_sems, recv_sems, local_sem):
        x, y, c = lax.axis_index("x"), lax.axis_index("y"), lax.axis_index("c")
        me, sibling = (x, y, c), (x, y, 1 - c)
        chips = [(1 - x, y), (x, 1 - y), (1 - x, 1 - y)]

        def rows(px, py, pc):
            return out_ref.at[pl.ds((4 * px + 2 * py + pc) * m_per, m_per), :]

        def copy(k, block, to, src=None):
            return pltpu.make_async_remote_copy(
                src_ref=rows(*block) if src is None else src, dst_ref=rows(*block),
                send_sem=send_sems.at[k], recv_sem=recv_sems.at[k], device_id=to, device_id_type=MESH)

        mine = pltpu.make_async_copy(x_ref, rows(*me), local_sem)
        mine.start()
        first = [copy(0, me, sibling, src=x_ref)]
        first += [copy(1 + j, me, (*chip, c), src=x_ref) for j, chip in enumerate(chips)]
        for cp in first:
            cp.start()
        passed = [copy(4 + j, (*chip, c), sibling) for j, chip in enumerate(chips)]
        for j, chip in enumerate(chips):
            copy(1 + j, (*chip, c), me).wait_recv()
            passed[j].start()
        copy(0, sibling, me).wait_recv()
        for j, chip in enumerate(chips):
            copy(4 + j, (*chip, 1 - c), me).wait_recv()
        for cp in first + passed:
            cp.wait_send()
        mine.wait()

    return pl.pallas_call(
        body, name=name, out_shape=_sds((8 * m_per, n), blk.dtype),
        in_specs=[pl.BlockSpec(memory_space=space)], out_specs=pl.BlockSpec(memory_space=space),
        scratch_shapes=[pltpu.SemaphoreType.DMA((7,)), pltpu.SemaphoreType.DMA((7,)), pltpu.SemaphoreType.DMA],
        compiler_params=pltpu.CompilerParams(vmem_limit_bytes=VMEM_LIMIT_BYTES),
    )(blk)


_ANY = pl.BlockSpec(memory_space=pl.ANY)


def _all_gather8_multi(name, blocks):
    n = len(blocks)

    def body(*refs):
        x_refs, out_refs = refs[:n], refs[n:2 * n]
        send_sems, recv_sems, local_sems = refs[2 * n:]
        x, y, c = lax.axis_index("x"), lax.axis_index("y"), lax.axis_index("c")
        me, sibling = (x, y, c), (x, y, 1 - c)
        chips = [(1 - x, y), (x, 1 - y), (1 - x, 1 - y)]

        def slot(i, px, py, pc):
            return out_refs[i].at[4 * px + 2 * py + pc]

        def copy(i, k, block, to, src=None):
            return pltpu.make_async_remote_copy(
                src_ref=slot(i, *block) if src is None else src, dst_ref=slot(i, *block),
                send_sem=send_sems.at[i, k], recv_sem=recv_sems.at[i, k], device_id=to, device_id_type=MESH)

        mine = [pltpu.make_async_copy(x_refs[i], slot(i, *me), local_sems.at[i]) for i in range(n)]
        first = []
        for i in range(n):
            first.append(copy(i, 0, me, sibling, src=x_refs[i]))
            first += [copy(i, 1 + j, me, (*chip, c), src=x_refs[i]) for j, chip in enumerate(chips)]
        for cp in mine + first:
            cp.start()
        passed = []
        for j, chip in enumerate(chips):
            for i in range(n):
                copy(i, 1 + j, (*chip, c), me).wait_recv()
                passed.append(copy(i, 4 + j, (*chip, c), sibling))
                passed[-1].start()
        for i in range(n):
            copy(i, 0, sibling, me).wait_recv()
            for j, chip in enumerate(chips):
                copy(i, 4 + j, (*chip, 1 - c), me).wait_recv()
        for cp in first + passed:
            cp.wait_send()
        for cp in mine:
            cp.wait()

    return pl.pallas_call(
        body, name=name, out_shape=[_sds((8,) + b.shape, b.dtype) for b in blocks],
        in_specs=[_ANY] * n, out_specs=[_ANY] * n,
        scratch_shapes=[pltpu.SemaphoreType.DMA((n, 7)), pltpu.SemaphoreType.DMA((n, 7)), pltpu.SemaphoreType.DMA((n,))],
    )(*blocks)


def _sibling_exchange(name, sends):
    n = len(sends)

    def body(*refs):
        x, y, c = lax.axis_index("x"), lax.axis_index("y"), lax.axis_index("c")
        send_sems, recv_sems = refs[2 * n:]
        copies = [pltpu.make_async_remote_copy(src_ref=refs[i], dst_ref=refs[n + i], send_sem=send_sems.at[i],
                                               recv_sem=recv_sems.at[i], device_id=(x, y, 1 - c), device_id_type=MESH)
                  for i in range(n)]
        for cp in copies:
            cp.start()
        for cp in copies:
            cp.wait()

    return pl.pallas_call(
        body, name=name, out_shape=[_sds(s.shape, s.dtype) for s in sends],
        in_specs=[_ANY] * n, out_specs=[_ANY] * n,
        scratch_shapes=[pltpu.SemaphoreType.DMA((n,)), pltpu.SemaphoreType.DMA((n,))],
    )(*sends)


def _chip_scatter(name, parts):
    n = len(parts)

    def body(*refs):
        p_refs, out_refs = refs[:n], refs[n:2 * n]
        send_sems, recv_sems, local_sems = refs[2 * n:]
        x, y, c = lax.axis_index("x"), lax.axis_index("y"), lax.axis_index("c")
        sibling = (x, y, 1 - c)
        me_chip = 2 * x + y
        chips = [(1 - x, y), (x, 1 - y), (1 - x, 1 - y)]

        def chip_id(chip):
            return 2 * chip[0] + chip[1]

        def copy(i, k, src, dst, to):
            return pltpu.make_async_remote_copy(src_ref=src, dst_ref=dst, send_sem=send_sems.at[i, k],
                                                recv_sem=recv_sems.at[i, k], device_id=to, device_id_type=MESH)

        mine = [pltpu.make_async_copy(p_refs[i].at[me_chip], out_refs[i].at[c, me_chip], local_sems.at[i]) for i in range(n)]
        first = []
        for i in range(n):
            first.append(copy(i, 0, p_refs[i].at[me_chip], out_refs[i].at[c, me_chip], sibling))
            first += [copy(i, 1 + j, p_refs[i].at[chip_id(chip)], out_refs[i].at[c, me_chip], (*chip, c))
                      for j, chip in enumerate(chips)]
        for cp in mine + first:
            cp.start()
        passed = []
        for j, chip in enumerate(chips):
            for i in range(n):
                landed = out_refs[i].at[c, chip_id(chip)]
                copy(i, 1 + j, p_refs[i].at[0], landed, sibling).wait_recv()
                passed.append(copy(i, 4 + j, landed, landed, sibling))
                passed[-1].start()
        for i in range(n):
            copy(i, 0, p_refs[i].at[0], out_refs[i].at[1 - c, me_chip], sibling).wait_recv()
            for j, chip in enumerate(chips):
                copy(i, 4 + j, p_refs[i].at[0], out_refs[i].at[1 - c, chip_id(chip)], sibling).wait_recv()
        for cp in first + passed:
            cp.wait_send()
        for cp in mine:
            cp.wait()

    return pl.pallas_call(
        body, name=name, out_shape=[_sds((2,) + p.shape, p.dtype) for p in parts],
        in_specs=[_ANY] * n, out_specs=[_ANY] * n,
        scratch_shapes=[pltpu.SemaphoreType.DMA((n, 7)), pltpu.SemaphoreType.DMA((n, 7)), pltpu.SemaphoreType.DMA((n,))],
    )(*parts)


_HBM = pl.BlockSpec(memory_space=pltpu.HBM)
_SEM = pl.BlockSpec(memory_space=pltpu.SEMAPHORE)
_SPLIT_CALL = pltpu.CompilerParams(has_side_effects=pltpu.SideEffectType.DATAFLOW_SIDE_EFFECTING)


def _peer_chips():
    x, y = lax.axis_index("x"), lax.axis_index("y")
    return [(1 - x, y), (x, 1 - y), (1 - x, 1 - y)]


def _chip_send_start(name, blocks, before):
    n, nb = len(blocks), len(before)

    def body(*refs):
        x_refs, land_refs = refs[:n], refs[n:2 * n]
        sems, token = refs[2 * n + nb:2 * n + nb + 6 * n], refs[-1]
        c = lax.axis_index("c")
        me_chip = 2 * lax.axis_index("x") + lax.axis_index("y")
        for i in range(n):
            for j, chip in enumerate(_peer_chips()):
                pltpu.make_async_remote_copy(src_ref=x_refs[i], dst_ref=land_refs[i].at[me_chip], send_sem=sems[6 * i + j],
                                             recv_sem=sems[6 * i + 3 + j], device_id=(*chip, c), device_id_type=MESH).start()
        token[...] = jnp.zeros(token.shape, token.dtype)

    lands = [lax.empty((4,) + b.shape, b.dtype) for b in blocks]
    hbm = [pltpu.with_memory_space_constraint(a, pltpu.HBM) for a in list(blocks) + lands]
    outs = pl.pallas_call(
        body, name=name,
        out_shape=tuple([pltpu.SemaphoreType.DMA(())] * (6 * n)) + tuple(pltpu.HBM(a.shape, a.dtype) for a in hbm) + (_sds((8, LANES)),),
        in_specs=tuple([_HBM] * (2 * n) + [_ANY] * nb),
        out_specs=tuple([_SEM] * (6 * n) + [_HBM] * (2 * n) + [pl.BlockSpec(memory_space=pltpu.VMEM)]),
        input_output_aliases={i: 6 * n + i for i in range(2 * n)}, compiler_params=_SPLIT_CALL,
    )(*hbm, *before)
    return outs[:6 * n], outs[6 * n:7 * n], outs[7 * n:8 * n], outs[8 * n]


def _chip_send_wait(name, sems, blocks, lands, after):
    n = len(blocks)

    def body(*refs):
        x_refs, land_refs, sems = refs[:n], refs[n:2 * n], refs[2 * n:8 * n]
        c = lax.axis_index("c")
        for i in range(n):
            for j, chip in enumerate(_peer_chips()):
                cp = pltpu.make_async_remote_copy(src_ref=x_refs[i], dst_ref=land_refs[i].at[2 * chip[0] + chip[1]],
                                                  send_sem=sems[6 * i + j], recv_sem=sems[6 * i + 3 + j],
                                                  device_id=(*chip, c), device_id_type=MESH)
                cp.wait_send()
                cp.wait_recv()

    outs = pl.pallas_call(
        body, name=name, out_shape=tuple(pltpu.HBM(a.shape, a.dtype) for a in list(blocks) + list(lands)),
        in_specs=tuple([_HBM] * (2 * n) + [_SEM] * (6 * n) + [_ANY]), out_specs=tuple([_HBM] * (2 * n)),
        input_output_aliases={i: i for i in range(2 * n)}, compiler_params=_SPLIT_CALL,
    )(*blocks, *lands, *sems, after)
    return outs[:n], outs[n:]


def _pack_flat(arrays, multiple, dtype):
    flat = jnp.concatenate([a.reshape(-1).astype(dtype) for a in arrays])
    pad = (-flat.shape[0]) % multiple
    if pad:
        flat = jnp.concatenate([flat, jnp.zeros((pad,), dtype)])
    return flat


def _unpack_flat(flat, shapes):
    out, off = [], 0
    for s in shapes:
        n = math.prod(s)
        out.append(flat[off:off + n].reshape(s))
        off += n
    return out


def _packed_rows(shape):
    return -(-math.prod(shape) // (8 * LANES)) * 8


def _pack_rows(arrays):
    parts = []
    for a in arrays:
        rows = _packed_rows(a.shape)
        flat = a.reshape(-1).astype(F32)
        if rows * LANES != flat.shape[0]:
            flat = jnp.pad(flat, (0, rows * LANES - flat.shape[0]))
        parts.append(flat.reshape(rows, LANES))
    return jnp.concatenate(parts, axis=0)


def _unpack_rows(packed, shapes):
    out, r0 = [], 0
    for s in shapes:
        rows = _packed_rows(s)
        out.append(packed[r0:r0 + rows].reshape(-1)[:math.prod(s)].reshape(s))
        r0 += rows
    return out


def _row_tile(r, cap, mult=16):
    for cand in range(min(cap, r) // mult * mult, 0, -mult):
        if r % cand == 0:
            return cand
    return r


ELEMENTWISE_BLOCK_ELEMS = 1 << 20


def _sum_axis1(name, x, out_dtype=F32):
    h, k, r, n = x.shape
    tr = _row_tile(r, max(16, ELEMENTWISE_BLOCK_ELEMS // (k * n)))

    def fn(s):
        acc = s[0].astype(F32)
        for i in range(1, k):
            acc = acc + s[i].astype(F32)
        return acc

    return _blocked(name, fn, (h, r // tr), [x], [pl.BlockSpec((None, k, tr, n), lambda i, t: (i, 0, t, 0))],
                    [_sds((h, r, n), out_dtype)], [pl.BlockSpec((None, tr, n), lambda i, t: (i, t, 0))])[0]


def _sum_pair(name, a, b, out_dtype):
    lead, r, n = a.shape
    tr = _row_tile(r, max(16, ELEMENTWISE_BLOCK_ELEMS // n))
    spec = pl.BlockSpec((None, tr, n), lambda i, t: (i, t, 0))
    return _blocked(name, lambda x, y: x.astype(F32) + y.astype(F32), (lead, r // tr), [a, b], [spec, spec],
                    [_sds((lead, r, n), out_dtype)], [spec])[0]


def _adamw(name, w, g, m, v):
    if w.ndim == 2:
        r, c = w.shape
        tr = _row_tile(r, max(8, (1 << 21) // (4 * c)), 8)
        grid, spec = (r // tr,), pl.BlockSpec((tr, c), lambda t: (t, 0))
    else:
        grid, spec = (1,), _res(w.shape)
    bc1 = 1.0 - ADAM_B1 ** ADAM_STEP
    bc2 = 1.0 - ADAM_B2 ** ADAM_STEP

    def fn(w, g, m, v):
        nm = ADAM_B1 * m + (1.0 - ADAM_B1) * g
        nv = ADAM_B2 * v + (1.0 - ADAM_B2) * (g * g)
        delta = -ADAM_LR * ((nm / bc1) / (jnp.sqrt(nv / bc2) + ADAM_EPS) + ADAM_WD * w)
        return delta, nm, nv

    return _blocked(name, fn, grid, [w, g, m, v], [spec] * 4, [_sds(w.shape)] * 3, [spec] * 3)


def _mod_rows(c16, ada_w):
    depth, d, n = ada_w.shape

    def fn(c, w):
        return _mm(_silu(c), w)

    return _blocked("mod_rows", fn, (depth,), [c16, ada_w],
                    [_res(c16.shape), pl.BlockSpec((None, d, n), lambda l: (l, 0, 0))],
                    [_sds((depth, 16, n))], [pl.BlockSpec((None, 16, n), lambda l: (l, 0, 0))])[0]


def _ada_grads(c16, dmod16, ada_w):
    depth, d, n = ada_w.shape

    def fn(c, dm, w):
        return _mm_tn(_silu(c), dm), _mm_nt(dm, w)

    return _blocked("ada_grads", fn, (depth,), [c16, dmod16, ada_w],
                    [_res(c16.shape), pl.BlockSpec((None, 16, n), lambda l: (l, 0, 0)),
                     pl.BlockSpec((None, d, n), lambda l: (l, 0, 0))],
                    [_sds((depth, d, n)), _sds((depth, 16, d))],
                    [pl.BlockSpec((None, d, n), lambda l: (l, 0, 0)), pl.BlockSpec((None, 16, d), lambda l: (l, 0, 0))])


def _cctx_grad(parts, c_ctx_row):
    def fn(p, c):
        tot = ((p[0] + p[1]) + p[2]) + p[3]
        _, vjp = jax.vjp(_silu, c)
        return vjp(tot)[0]

    return _blocked("cctx_grad", fn, (1,), [parts, c_ctx_row], [_res(parts.shape), _res(c_ctx_row.shape)],
                    [_sds(c_ctx_row.shape)], [_res(c_ctx_row.shape)])[0]


def _modulate(x, sh, sc):
    return x * (1.0 + sc) + sh


def _weight_operand(w):
    if isinstance(w, tuple):
        return w[0], w[1], tuple(b for b in w[1].block_shape if b is not None)
    return w, _res(w.shape), w.shape


def _matmul_cols(u, w):
    if w.ndim == 2:
        return jnp.dot(u, w, preferred_element_type=F32)
    return jnp.concatenate([jnp.dot(u, w[j], preferred_element_type=F32) for j in range(w.shape[0])], axis=1)


def _matmul_cols_t(dp, w):
    if w.ndim == 2:
        return _mm_nt(dp, w)
    nj = w.shape[2]
    return functools.reduce(lambda a, b: a + b, [_mm_nt(dp[:, j * nj:(j + 1) * nj], w[j]) for j in range(w.shape[0])])


def _rows_weight(w):
    return w if w.ndim == 2 else w.reshape(w.shape[0] * w.shape[1], w.shape[2])


def _inproj(name, x, sh, sc, w, tm):
    t, d = x.shape
    w, w_spec, w_blk = _weight_operand(w)
    n = w_blk[-1] * (w_blk[0] if len(w_blk) == 3 else 1)
    tm = min(tm, t)

    def fn(x, sh, sc, w):
        u = _modulate(x, sh, sc).astype(MXU_DTYPE)
        return _matmul_cols(u, w), u

    return _blocked(name, fn, (t // tm,), [x, sh, sc, w],
                    [_tok(tm, d), _res(sh.shape), _res(sc.shape), w_spec],
                    [_sds((t, n)), _sds((t, d), MXU_DTYPE)], [_tok(tm, n), _tok(tm, d)])


def _inproj_bwd(name, dp, x, sh, sc, w, add, tm):
    t, d = x.shape
    w, w_spec, w_blk = _weight_operand(w)
    n = w_blk[-1] * (w_blk[0] if len(w_blk) == 3 else 1)
    tm = min(tm, t)
    has_add = add is not None

    def fn(dp, x, sh, sc, w, *rest):
        du = _matmul_cols_t(dp, w)
        _, vjp = jax.vjp(_modulate, x, sh, sc)
        dx, dsh, dsc = vjp(du)
        if has_add:
            dx = dx + rest[0]
        return dx, dsh, dsc

    ins = [dp, x, sh, sc, w] + ([add] if has_add else [])
    specs = [_tok(tm, n), _tok(tm, d), _res(sh.shape), _res(sc.shape), w_spec] + ([_tok(tm, d)] if has_add else [])
    return _blocked(name, fn, (t // tm,), ins, specs,
                    [_sds((t, d)), _sds(sh.shape), _sds(sc.shape)], [_tok(tm, d), _res(sh.shape), _res(sc.shape)],
                    acc=(1, 2), acc_axis=0)


def _residual_ln(y, x, gt, lng, lnb):
    return _layernorm(ALPHA * x + gt * y, lng, lnb)


def _gdn_mix(o0, o1, pg, yp, normw):
    o = o0 + o1
    heads = []
    for h in range(GDN_HEADS):
        oh = o[:, h * HEAD_DIM:(h + 1) * HEAD_DIM]
        heads.append(oh * lax.rsqrt(jnp.mean(oh * oh, -1, keepdims=True) + RMS_EPS) * normw)
    on = jnp.concatenate(heads, axis=-1) * _silu(pg)
    return jnp.concatenate([on, yp], axis=-1)


def _even_out(o0, o1, p, ypool, x, normw, gt, lng, lnb, w, tm):
    t, d = x.shape
    tm = min(tm, t)
    gate_blk = 3

    def fn(o0, o1, pg, yp, x, normw, gt, lng, lnb, w):
        mix = _gdn_mix(o0, o1, pg, yp, normw).astype(MXU_DTYPE)
        y = jnp.dot(mix, w, preferred_element_type=F32)
        return _residual_ln(y, x, gt, lng, lnb), mix, y

    return _blocked("even_out", fn, (t // tm,), [o0, o1, p, ypool, x, normw, gt, lng, lnb, w],
                    [_tok(tm, 512), _tok(tm, 512), _tok(tm, 512, gate_blk), _tok(tm, 512), _tok(tm, d),
                     _res(normw.shape), _res(gt.shape), _res(lng.shape), _res(lnb.shape), _res(w.shape)],
                    [_sds((t, d)), _sds((t, d), MXU_DTYPE), _sds((t, d))], [_tok(tm, d)] * 3)


def _even_out_bwd(dx1, o0, o1, p, ypool, x, y, normw, gt, lng, lnb, w, tm):
    t, d = x.shape
    tm = min(tm, t)

    def fn(dx1, o0, o1, pg, yp, x, y, normw, gt, lng, lnb, w):
        _, vjp2 = jax.vjp(_residual_ln, y, x, gt, lng, lnb)
        dy, dx, dgt, dlng, dlnb = vjp2(dx1)
        dyb = dy.astype(MXU_DTYPE)
        dmix = _mm_nt(dyb, w)
        _, vjp1 = jax.vjp(_gdn_mix, o0, o1, pg, yp, normw)
        do, _, dpg, dyp, dnormw = vjp1(dmix)
        return do, dpg, dyp, dx, dyb, dnormw, dgt, dlng, dlnb

    return _blocked("even_out_bwd", fn, (t // tm,), [dx1, o0, o1, p, ypool, x, y, normw, gt, lng, lnb, w],
                    [_tok(tm, d), _tok(tm, 512), _tok(tm, 512), _tok(tm, 512, 3), _tok(tm, 512),
                     _tok(tm, d), _tok(tm, d), _res(normw.shape), _res(gt.shape), _res(lng.shape), _res(lnb.shape),
                     _res(w.shape)],
                    [_sds((t, 512)), _sds((t, 512), MXU_DTYPE), _sds((t, 512)), _sds((t, d)), _sds((t, d), MXU_DTYPE),
                     _sds(normw.shape), _sds(gt.shape), _sds(lng.shape), _sds(lnb.shape)],
                    [_tok(tm, 512), _tok(tm, 512), _tok(tm, 512), _tok(tm, d), _tok(tm, d),
                     _res(normw.shape), _res(gt.shape), _res(lng.shape), _res(lnb.shape)],
                    acc=(5, 6, 7, 8), acc_axis=0)


def _odd_mix(ysc, z0, cg, cb):
    z = _silu(_layernorm(z0, cg, cb))
    return jnp.concatenate([ysc, z], axis=-1)


def _odd_out(ysc, z0, x, cg, cb, gt, lng, lnb, w, tm):
    t, d = x.shape
    tm = min(tm, t)

    def fn(ysc, z0, x, cg, cb, gt, lng, lnb, w):
        mix = _odd_mix(ysc, z0, cg, cb).astype(MXU_DTYPE)
        y = jnp.dot(mix, w, preferred_element_type=F32)
        return _residual_ln(y, x, gt, lng, lnb), mix, y

    return _blocked("odd_out", fn, (t // tm,), [ysc, z0, x, cg, cb, gt, lng, lnb, w],
                    [_tok(tm, 512), _tok(tm, 512), _tok(tm, d), _res(cg.shape), _res(cb.shape), _res(gt.shape),
                     _res(lng.shape), _res(lnb.shape), _res(w.shape)],
                    [_sds((t, d)), _sds((t, d), MXU_DTYPE), _sds((t, d))], [_tok(tm, d)] * 3)


def _odd_out_bwd(dx3, ysc, z0, x, y, cg, cb, gt, lng, lnb, w, tm):
    t, d = x.shape
    tm = min(tm, t)

    def fn(dx3, ysc, z0, x, y, cg, cb, gt, lng, lnb, w):
        _, vjp2 = jax.vjp(_residual_ln, y, x, gt, lng, lnb)
        dy, dx, dgt, dlng, dlnb = vjp2(dx3)
        dyb = dy.astype(MXU_DTYPE)
        dmix = _mm_nt(dyb, w)
        _, vjp1 = jax.vjp(_odd_mix, ysc, z0, cg, cb)
        dysc, dz0, dcg, dcb = vjp1(dmix)
        return dysc, dz0, dx, dyb, dcg, dcb, dgt, dlng, dlnb

    return _blocked("odd_out_bwd", fn, (t // tm,), [dx3, ysc, z0, x, y, cg, cb, gt, lng, lnb, w],
                    [_tok(tm, d), _tok(tm, 512), _tok(tm, 512), _tok(tm, d), _tok(tm, d), _res(cg.shape), _res(cb.shape),
                     _res(gt.shape), _res(lng.shape), _res(lnb.shape), _res(w.shape)],
                    [_sds((t, 512)), _sds((t, 512)), _sds((t, d)), _sds((t, d), MXU_DTYPE),
                     _sds(cg.shape), _sds(cb.shape), _sds(gt.shape), _sds(lng.shape), _sds(lnb.shape)],
                    [_tok(tm, 512), _tok(tm, 512), _tok(tm, d), _tok(tm, d),
                     _res(cg.shape), _res(cb.shape), _res(gt.shape), _res(lng.shape), _res(lnb.shape)],
                    acc=(4, 5, 6, 7, 8), acc_axis=0)


def _ffn_down(name, s, x, gt, lng, lnb, w, tm):
    t, d = x.shape
    f = s.shape[1]
    w, w_spec, _ = _weight_operand(w)
    tm = min(tm, t)

    def fn(s, x, gt, lng, lnb, w):
        y = jnp.dot(s, _rows_weight(w), preferred_element_type=F32)
        return _residual_ln(y, x, gt, lng, lnb), y

    return _blocked(name, fn, (t // tm,), [s, x, gt, lng, lnb, w],
                    [_tok(tm, f), _tok(tm, d), _res(gt.shape), _res(lng.shape), _res(lnb.shape), w_spec],
                    [_sds((t, d)), _sds((t, d))], [_tok(tm, d)] * 2)


def _ffn_down_bwd(name, dx2, x, y, gt, lng, lnb, w, tm):
    t, d = x.shape
    w, w_spec, w_blk = _weight_operand(w)
    f = math.prod(w_blk[:-1])
    tm = min(tm, t)

    def fn(dx2, x, y, gt, lng, lnb, w):
        _, vjp2 = jax.vjp(_residual_ln, y, x, gt, lng, lnb)
        dy, dx, dgt, dlng, dlnb = vjp2(dx2)
        dyb = dy.astype(MXU_DTYPE)
        return _mm_nt(dyb, _rows_weight(w)), dx, dyb, dgt, dlng, dlnb

    return _blocked(name, fn, (t // tm,), [dx2, x, y, gt, lng, lnb, w],
                    [_tok(tm, d), _tok(tm, d), _tok(tm, d), _res(gt.shape), _res(lng.shape), _res(lnb.shape), w_spec],
                    [_sds((t, f), MXU_DTYPE), _sds((t, d)), _sds((t, d), MXU_DTYPE), _sds(gt.shape), _sds(lng.shape), _sds(lnb.shape)],
                    [_tok(tm, f), _tok(tm, d), _tok(tm, d), _res(gt.shape), _res(lng.shape), _res(lnb.shape)],
                    acc=(3, 4, 5), acc_axis=0)


def _loss_and_grad(x4, target, tm):
    t, d = x4.shape
    tm = min(tm, t)

    def fn(y, tg):
        e = y - tg
        part = 0.5 * jnp.sum(jnp.mean(e * e, axis=-1, keepdims=True), axis=0, keepdims=True)
        return jnp.broadcast_to(part, (1, LANES)), e * (1.0 / d)

    return _blocked("loss", fn, (t // tm,), [x4, target], [_tok(tm, d), _tok(tm, d)],
                    [_sds((1, LANES)), _sds((t, d))], [_res((1, LANES)), _tok(tm, d)], acc=(0,), acc_axis=0)


def _matmul_tn(name, pairs, chip_cols=None):
    k = pairs[0][0].shape[1]
    n = pairs[0][1].shape[1]
    tk = k if k <= 1024 else k // 2
    if chip_cols is None:
        tn = 512 if n % 512 == 0 else (384 if n % 384 == 0 else 128)
        out_spec, out_shape = pl.BlockSpec((tk, tn), lambda i, j: (i, j)), _sds((k, n), MXU_DTYPE)
    else:
        tn = chip_cols
        tk = min(tk, 512) if tn > 1024 else tk
        out_spec = pl.BlockSpec((None, tk, tn), lambda i, j: (j, i, 0))
        out_shape = _sds((n // chip_cols, k, chip_cols), MXU_DTYPE)

    def body(*refs):
        acc = None
        for p in range(len(pairs)):
            term = lax.dot_general(refs[2 * p][...], refs[2 * p + 1][...], (((0,), (0,)), ((), ())), preferred_element_type=F32)
            acc = term if acc is None else acc + term
        refs[-1][...] = acc.astype(refs[-1].dtype)

    in_specs, args = [], []
    for a, b in pairs:
        t = a.shape[0]
        in_specs += [pl.BlockSpec((t, tk), lambda i, j: (0, i)), pl.BlockSpec((t, tn), lambda i, j: (0, j))]
        args += [a, b]
    return pl.pallas_call(
        body, name=name, grid=(k // tk, n // tn), in_specs=in_specs,
        out_specs=out_spec, out_shape=out_shape,
        compiler_params=pltpu.CompilerParams(dimension_semantics=("arbitrary", "arbitrary"),
                                             vmem_limit_bytes=VMEM_LIMIT_BYTES),
    )(*args)


def _chan(t, col_of):
    return pl.BlockSpec((t, LANES), lambda i: (0, col_of(i)))


def _wblk(k, col_of=lambda i: i):
    return pl.BlockSpec((k, LANES), lambda i: (0, col_of(i)))


SEQ_TILE = 256
GRID_SEQ_TILE = 512


def _load_ext(ref, r0, rows, halo, total):
    lo, hi = max(r0 - halo, 0), min(r0 + rows + halo, total)
    parts = []
    if lo > r0 - halo:
        parts.append(jnp.zeros((lo - (r0 - halo), ref.shape[1]), F32))
    parts.append(ref[lo:hi, :].astype(F32))
    if hi < r0 + rows + halo:
        parts.append(jnp.zeros((r0 + rows + halo - hi, ref.shape[1]), F32))
    return parts[0] if len(parts) == 1 else jnp.concatenate(parts, axis=0)


def _seq_stage(name, tile_fn, halo, grid_n, seqs, pars, outs, tile_rows=SEQ_TILE):
    total = seqs[0][0].shape[0]
    rows = min(tile_rows, total)
    ns, npar = len(seqs), len(pars)

    def body(*refs):
        seq_refs, par_refs, out_refs = refs[:ns], refs[ns:ns + npar], refs[ns + npar:]
        par_vals = [r[...] for r in par_refs]
        for r0 in range(0, total, rows):
            exts = [_load_ext(r, r0, rows, halo, total) for r in seq_refs]
            res = tile_fn(r0, *exts, *par_vals)
            for o_ref, v in zip(out_refs, res):
                o_ref[r0:r0 + rows, :] = v.astype(o_ref.dtype)

    return pl.pallas_call(
        body, name=name, grid=(grid_n,), in_specs=[s for _, s in seqs] + [s for _, s in pars],
        out_specs=[s for _, s in outs], out_shape=[o for o, _ in outs],
        compiler_params=pltpu.CompilerParams(dimension_semantics=("arbitrary",), vmem_limit_bytes=VMEM_LIMIT_BYTES),
    )(*[a for a, _ in seqs], *[a for a, _ in pars])


def _seq_stage_bwd(name, tile_fn, halo, grid_n, douts, seqs, pars, dseq_outs, dpar_outs, tile_rows=SEQ_TILE):
    total = seqs[0][0].shape[0]
    rows = min(tile_rows, total)
    groups = [d if isinstance(d, list) else [d] for d in douts]
    douts = [d for g in groups for d in g]
    starts = [sum(len(g) for g in groups[:k]) for k in range(len(groups))]
    nd, ns, npar = len(douts), len(seqs), len(pars)
    widths = [s.block_shape[-1] for _, s in seqs]

    def dtile(dout_refs, k, r0):
        terms = [d[r0:r0 + rows, :].astype(F32) for d in dout_refs[starts[k]:starts[k] + len(groups[k])]]
        return functools.reduce(lambda a, b: a + b, terms)

    def body(*refs):
        dout_refs, seq_refs, par_refs = refs[:nd], refs[nd:nd + ns], refs[nd + ns:nd + ns + npar]
        dseq_refs = refs[nd + ns + npar:nd + 2 * ns + npar]
        dpar_refs = refs[nd + 2 * ns + npar:nd + 2 * ns + 2 * npar]
        accs = refs[nd + 2 * ns + 2 * npar:]
        par_vals = [r[...] for r in par_refs]
        for a in accs:
            for r0 in range(0, total, rows):
                a[r0:r0 + rows, :] = jnp.zeros((rows, a.shape[1]), F32)
        dpars = [jnp.zeros(p.shape, F32) for p in par_vals]
        for r0 in range(0, total, rows):
            exts = [_load_ext(r, r0, rows, halo, total) for r in seq_refs]
            _, vjp = jax.vjp(functools.partial(tile_fn, r0), *exts, *par_vals)
            cts = vjp(tuple(dtile(dout_refs, k, r0) for k in range(len(groups))))
            lo, hi = max(r0 - halo, 0), min(r0 + rows + halo, total)
            for a, dx in zip(accs, cts[:ns]):
                a[lo:hi, :] += dx[lo - (r0 - halo):hi - (r0 - halo)]
            dpars = [acc + g for acc, g in zip(dpars, cts[ns:])]
        for o_ref, a in zip(dseq_refs, accs):
            for r0 in range(0, total, rows):
                o_ref[r0:r0 + rows, :] = a[r0:r0 + rows, :].astype(o_ref.dtype)
        for o_ref, g in zip(dpar_refs, dpars):
            o_ref[...] = g

    return pl.pallas_call(
        body, name=name, grid=(grid_n,),
        in_specs=[s for _, s in douts] + [s for _, s in seqs] + [s for _, s in pars],
        out_specs=[s for _, s in dseq_outs] + [s for _, s in dpar_outs],
        out_shape=[o for o, _ in dseq_outs] + [o for o, _ in dpar_outs],
        scratch_shapes=[pltpu.VMEM((total, w), F32) for w in widths],
        compiler_params=pltpu.CompilerParams(dimension_semantics=("arbitrary",), vmem_limit_bytes=VMEM_LIMIT_BYTES),
    )(*[a for a, _ in douts], *[a for a, _ in seqs], *[a for a, _ in pars])


def _qkv_tile(r0, p_ext, w):
    kind = pl.program_id(0)
    a = _silu(_conv5(p_ext, w))
    nrm = a * lax.rsqrt(jnp.sum(a * a, -1, keepdims=True) + RMS_EPS)
    scale = jnp.where(kind < GDN_HEADS, HEAD_DIM ** -0.5, 1.0).astype(F32)
    sel = (kind < 2 * GDN_HEADS).astype(F32)
    return (sel * (nrm * scale) + (1.0 - sel) * a,)


def _qkv_conv(name, p, w):
    t = p.shape[0]
    nb = w.shape[1] // LANES
    ident = lambda i: i
    return _seq_stage(name, _qkv_tile, HALO_SHORT, nb, [(p, _chan(t, ident))], [(w, _wblk(GDN_CONV))],
                      [(_sds((t, nb * LANES)), _chan(t, ident))])[0]


def _qkv_conv_bwd(name, dqkv0, dqkv1, p, w):
    t = p.shape[0]
    nb = w.shape[1] // LANES
    ident = lambda i: i
    return _seq_stage_bwd(name, _qkv_tile, HALO_SHORT, nb, [[(dqkv0, _chan(t, ident)), (dqkv1, _chan(t, ident))]],
                          [(p, _chan(t, ident))],
                          [(w, _wblk(GDN_CONV))], [(_sds((t, nb * LANES), MXU_DTYPE), _chan(t, ident))],
                          [(_sds(w.shape), _wblk(GDN_CONV))])


def _gates_fn(s, avec, dtvec):
    lane = lax.broadcasted_iota(jnp.int32, s.shape, 1)
    beta = jax.nn.sigmoid(s)
    g = -jnp.exp(avec) * jax.nn.softplus(s + dtvec)
    nh = 2 * GDN_HEADS
    return jnp.where(lane < nh, beta, jnp.where(lane < 2 * nh, g, 0.0))


def _gates(name, p, avec, dtvec, col):
    t = p.shape[0]
    tm = min(512, t)
    return _blocked(name, _gates_fn, (t // tm,), [p, avec, dtvec], [_tok(tm, LANES, col), _res(avec.shape), _res(dtvec.shape)],
                    [_sds((t, LANES))], [_tok(tm, LANES)])[0]


def _gates_bwd(name, dgb, p, avec, dtvec, col):
    t = p.shape[0]
    tm = min(512, t)

    def fn(dgb, s, avec, dtvec):
        _, vjp = jax.vjp(_gates_fn, s, avec, dtvec)
        return vjp(dgb)

    return _blocked(name, fn, (t // tm,), [dgb, p, avec, dtvec],
                    [_tok(tm, LANES), _tok(tm, LANES, col), _res(avec.shape), _res(dtvec.shape)],
                    [_sds((t, LANES), MXU_DTYPE), _sds(avec.shape), _sds(dtvec.shape)],
                    [_tok(tm, LANES), _res(avec.shape), _res(dtvec.shape)], acc=(1, 2), acc_axis=0)


def _make_pool_tile(total):
    def tile(r0, x_ext, scale, *ws):
        ys = []
        for g, win in enumerate(POOL_WINDOWS):
            xg = x_ext[:, g * LANES:(g + 1) * LANES]
            rows = xg.shape[0] - 2 * HALO_SHORT
            pos = r0 + lax.broadcasted_iota(jnp.int32, (rows, LANES), 0)
            lo = jnp.clip(pos - win // 2, 0, total)
            hi = jnp.clip(pos - win // 2 + win, 0, total)
            window_sum = _pool_sums[win](xg, jnp.ones((win, LANES), F32))
            pooled = window_sum / (hi - lo).astype(F32) - _center(xg, HALO_SHORT)
            ys.append(_mm(pooled, ws[g]))
        return (jnp.concatenate(ys, axis=-1) * scale,)

    return tile


def _pool_specs(p, pool_w, pool_scale, col0):
    t = p.shape[0]
    n = len(POOL_WINDOWS) * LANES
    seq = (p, pl.BlockSpec((t, n), lambda i: (0, col0)))
    pars = [(pool_scale, _res(pool_scale.shape))]
    pars += [(pool_w, pl.BlockSpec((None, LANES, LANES), lambda i, g=g: (g, 0, 0))) for g in range(len(POOL_WINDOWS))]
    return t, n, seq, pars


def _pool(p, pool_w, pool_scale, col0):
    t, n, seq, pars = _pool_specs(p, pool_w, pool_scale, col0)
    return _seq_stage("pool", _make_pool_tile(t), HALO_SHORT, 1, [seq], pars, [(_sds((t, n)), _res((t, n)))])[0]


def _pool_bwd(dy, p, pool_w, pool_scale, col0):
    t, n, seq, pars = _pool_specs(p, pool_w, pool_scale, col0)
    wspec = (_sds((LANES, LANES)), _res((LANES, LANES)))
    return _seq_stage_bwd("pool_bwd", _make_pool_tile(t), HALO_SHORT, 1, [(dy, _res(dy.shape))], [seq], pars,
                          [(_sds((t, n), MXU_DTYPE), _res((t, n)))],
                          [(_sds(pool_scale.shape), _res(pool_scale.shape))] + [wspec] * len(POOL_WINDOWS))


def _ffn_conv_tile(r0, a_ext, gt_ext, w):
    return (_silu(_conv3x3(a_ext, w)) * _center(gt_ext, HALO_GRID),)


def _ffn_conv(name, h, w9):
    t = h.shape[0]
    nb = w9.shape[1] // LANES
    ident = lambda i: i
    return _seq_stage(name, _ffn_conv_tile, HALO_GRID, nb, [(h, _chan(t, ident)), (h, _chan(t, lambda i: nb + i))],
                      [(w9, _wblk(9))], [(_sds((t, nb * LANES), MXU_DTYPE), _chan(t, ident))], GRID_SEQ_TILE)[0]


def _ffn_conv_bwd(name, ds, h, w9):
    t = h.shape[0]
    nb = w9.shape[1] // LANES
    ident = lambda i: i
    o = (_sds((t, nb * LANES), MXU_DTYPE), _chan(t, ident))
    return _seq_stage_bwd(name, _ffn_conv_tile, HALO_GRID, nb, [(ds, _chan(t, ident))],
                          [(h, _chan(t, ident)), (h, _chan(t, lambda i: nb + i))], [(w9, _wblk(9))],
                          [o, o], [(_sds(w9.shape), _wblk(9))], GRID_SEQ_TILE)


def _sconv_tile(r0, gb_ext, gc_ext, h_ext, w):
    return (_center(gb_ext, HALO_SHORT) * _conv3(gc_ext * h_ext, w),)


def _sconv_specs(p1, w3):
    t = p1.shape[0]
    nb = w3.shape[1] // LANES
    seqs = [(p1, _chan(t, lambda i: i)), (p1, _chan(t, lambda i: nb + i)), (p1, _chan(t, lambda i: 2 * nb + i))]
    return t, nb, seqs, [(w3, _wblk(SC_WIDTH))]


def _sconv(p1, w3):
    t, nb, seqs, pars = _sconv_specs(p1, w3)
    return _seq_stage("sconv", _sconv_tile, HALO_SHORT, nb, seqs, pars, [(_sds((t, nb * LANES)), _chan(t, lambda i: i))])[0]


def _sconv_bwd(dysc, p1, w3):
    t, nb, seqs, pars = _sconv_specs(p1, w3)
    o = (_sds((t, nb * LANES), MXU_DTYPE), _chan(t, lambda i: i))
    return _seq_stage_bwd("sconv_bwd", _sconv_tile, HALO_SHORT, nb, [(dysc, _chan(t, lambda i: i))], seqs, pars,
                          [o, o, o], [(_sds(w3.shape), _wblk(SC_WIDTH))])


def _conf_tile(r0, ga_ext, gb_ext, w):
    return (_conv31(ga_ext * jax.nn.sigmoid(gb_ext), w),)


def _conf_specs(p1, w31, blk0):
    t = p1.shape[0]
    nb = w31.shape[1] // LANES
    seqs = [(p1, _chan(t, lambda i: blk0 + i)), (p1, _chan(t, lambda i: blk0 + nb + i))]
    return t, nb, seqs, [(w31, _wblk(CF_WIDTH))]


def _confconv(p1, w31, blk0):
    t, nb, seqs, pars = _conf_specs(p1, w31, blk0)
    return _seq_stage("confconv", _conf_tile, HALO_CONF, nb, seqs, pars, [(_sds((t, nb * LANES)), _chan(t, lambda i: i))])[0]


def _confconv_bwd(dz0, p1, w31, blk0):
    t, nb, seqs, pars = _conf_specs(p1, w31, blk0)
    o = (_sds((t, nb * LANES), MXU_DTYPE), _chan(t, lambda i: i))
    return _seq_stage_bwd("confconv_bwd", _conf_tile, HALO_CONF, nb, [(dz0, _chan(t, lambda i: i))], seqs, pars,
                          [o, o], [(_sds(w31.shape), _wblk(CF_WIDTH))])


def _split_bf16(x):
    head = x.astype(jnp.bfloat16)
    return head, (x - head.astype(F32)).astype(jnp.bfloat16)


def _bdot(a, b, ca, cb, hi):
    dims = (((ca,), (cb,)), ((0,), (0,)))
    if not hi:
        return lax.dot_general(a.astype(MXU_DTYPE), b.astype(MXU_DTYPE), dims, preferred_element_type=F32)
    if MXU_DTYPE == F32:
        return lax.dot_general(a, b, dims, preferred_element_type=F32, precision=HIGHEST)
    (ah, al), (bh, bl) = _split_bf16(a), _split_bf16(b)
    dot = functools.partial(lax.dot_general, dimension_numbers=dims, preferred_element_type=F32)
    return dot(ah, bh) + (dot(ah, bl) + dot(al, bh))


def _make_bmm(hi):
    nn_i = lambda a, b: _bdot(a, b, 2, 1, hi)
    nt_i = lambda a, b: _bdot(a, b, 2, 2, hi)
    tn_i = lambda a, b: _bdot(a, b, 1, 1, hi)
    nn = _mm_vjp(nn_i, lambda g, b: nt_i(g, b), lambda a, g: tn_i(a, g))
    nt = _mm_vjp(nt_i, lambda g, b: nn_i(g, b), lambda a, g: tn_i(g, a))
    tn = _mm_vjp(tn_i, lambda g, b: nt_i(b, g), lambda a, g: nn_i(a, g))
    return nn, nt, tn


_bmm, _bmm_nt, _bmm_tn = _make_bmm(False)
_bmm_hi, _bmm_hi_nt, _bmm_hi_tn = _make_bmm(True)


def _unit_tri_inverse(a):
    c = a.shape[-1]
    eye = (lax.broadcasted_iota(jnp.int32, a.shape, 1) == lax.broadcasted_iota(jnp.int32, a.shape, 2)).astype(F32)
    levels = max(1, int(math.ceil(math.log2(c))) - 1)
    p = eye - a
    m = _bdot(a, a, 2, 1, True)
    for level in range(levels):
        p_next = p + _bdot(p, m, 2, 1, True)
        if level + 1 < levels:
            m = _bdot(m, m, 2, 1, True)
        p = p_next
    return p


@jax.custom_vjp
def _known_inverse(a, tinv):
    return tinv


def _known_inverse_fwd(a, tinv):
    return tinv, tinv


def _known_inverse_bwd(tinv, dt):
    da = -_bdot(_bdot(tinv, dt, 1, 1, True), tinv, 2, 2, True)
    return da, jnp.zeros_like(tinv)


_known_inverse.defvjp(_known_inverse_fwd, _known_inverse_bwd)


def _gdn_chunk(q, k, v, gcol, grow, bcol, s, tinv=None):
    n, c, _ = q.shape
    shape = (n, c, c)
    fwd_dir = lax.broadcasted_iota(jnp.int32, shape, 0) < n // 2
    i = lax.broadcasted_iota(jnp.int32, shape, 1)
    j = lax.broadcasted_iota(jnp.int32, shape, 2)
    order = jnp.where(fwd_dir, i - j, j - i)
    incl = order >= 0
    gc_col = jnp.sum(jnp.where(incl, grow, 0.0), axis=2, keepdims=True)
    gc_row = jnp.sum(jnp.where(order <= 0, gcol, 0.0), axis=1, keepdims=True)
    gtot = jnp.sum(grow, axis=2, keepdims=True)
    decay = jnp.exp(jnp.where(incl, gc_col - gc_row, -1e30))
    kb = k * bcol
    a = jnp.where(order > 0, _bmm_nt(kb, k) * decay, 0.0)
    tinv = _unit_tri_inverse(a) if tinv is None else _known_inverse(a, tinv)
    e_col = jnp.exp(gc_col)
    u = _bmm_hi(tinv, v * bcol)
    w = _bmm_hi(tinv, kb * e_col)
    k_dec = k * jnp.exp(gtot - gc_col)
    v_new = u - _bmm(w, s)
    attn = jnp.where(incl, _bmm_nt(q, k) * decay, 0.0)
    o = _bmm(q * e_col, s) + _bmm(attn, v_new)
    s_next = s * jnp.exp(gtot) + _bmm_tn(k_dec, v_new)
    return o, s_next, tinv


N_CHAINS = 2 * GDN_HEADS
QKV_W = 3 * GDN_HEADS * HEAD_DIM


def _scan_specs(nc, step):
    c = GDN_CHUNK
    seq = [pl.BlockSpec((c, QKV_W), lambda i: (step(i), 0)), pl.BlockSpec((c, QKV_W), lambda i: (nc - 1 - step(i), 0))]
    heads = [pl.BlockSpec((c, GDN_HEADS * HEAD_DIM), lambda i: (step(i), 0)),
             pl.BlockSpec((c, GDN_HEADS * HEAD_DIM), lambda i: (nc - 1 - step(i), 0))]
    colv = pl.BlockSpec((N_CHAINS, None, c, 1), lambda i: (0, step(i), 0, 0))
    rowv = pl.BlockSpec((N_CHAINS, None, 1, c), lambda i: (0, step(i), 0, 0))
    state = pl.BlockSpec((N_CHAINS, HEAD_DIM, HEAD_DIM), lambda i: (0, 0, 0))
    saved = pl.BlockSpec((N_CHAINS, None, HEAD_DIM, HEAD_DIM), lambda i: (0, step(i), 0, 0))
    inv = pl.BlockSpec((N_CHAINS, None, c, c), lambda i: (0, step(i), 0, 0))
    return seq, heads, colv, rowv, state, saved, inv


def _head_cols(h, part):
    lo = (part * GDN_HEADS + h) * HEAD_DIM
    return slice(lo, lo + HEAD_DIM)


def _chain_stack(x0_ref, x1_ref, part):
    return jnp.stack([(x0_ref if n < GDN_HEADS else x1_ref)[:, _head_cols(n % GDN_HEADS, part)] for n in range(N_CHAINS)])


def _scan_fwd(name, qkv, bcol, gcol, grow, s0):
    t = qkv.shape[0]
    nc = t // GDN_CHUNK
    seq, heads, colv, rowv, state, saved, inv = _scan_specs(nc, lambda i: i)

    def body(qkv0_ref, qkv1_ref, b_ref, gc_ref, gr_ref, s0_ref, o0_ref, o1_ref, save_ref, tinv_ref, fin_ref, s_ref):
        @pl.when(pl.program_id(0) == 0)
        def _():
            s_ref[...] = s0_ref[...]

        s = s_ref[...]
        save_ref[...] = s
        q, k, v = [_chain_stack(qkv0_ref, qkv1_ref, part) for part in range(3)]
        o, s_next, tinv = _gdn_chunk(q, k, v, gc_ref[...], gr_ref[...], b_ref[...], s)
        for n in range(N_CHAINS):
            d, h = divmod(n, GDN_HEADS)
            (o0_ref if d == 0 else o1_ref)[:, _head_cols(h, 0)] = o[n]
        tinv_ref[...] = tinv
        s_ref[...] = s_next
        fin_ref[...] = s_next

    hw = GDN_HEADS * HEAD_DIM
    return pl.pallas_call(
        body, name=name, grid=(nc,), in_specs=seq + [colv, colv, rowv, state],
        out_specs=heads + [saved, inv, state],
        out_shape=[_sds((t, hw)), _sds((t, hw)), _sds((N_CHAINS, nc, HEAD_DIM, HEAD_DIM)),
                   _sds((N_CHAINS, nc, GDN_CHUNK, GDN_CHUNK)), _sds((N_CHAINS, HEAD_DIM, HEAD_DIM))],
        scratch_shapes=[pltpu.VMEM((N_CHAINS, HEAD_DIM, HEAD_DIM), F32)],
        compiler_params=pltpu.CompilerParams(dimension_semantics=("arbitrary",), vmem_limit_bytes=VMEM_LIMIT_BYTES),
    )(qkv, qkv, bcol, gcol, grow, s0)


def _scan_bwd(name, do, qkv, bcol, gcol, grow, saved_s, saved_inv, ds_last):
    t = qkv.shape[0]
    nc = t // GDN_CHUNK
    c = GDN_CHUNK
    seq, heads, colv, rowv, state, saved, inv = _scan_specs(nc, lambda i: nc - 1 - i)

    def body(do0_ref, do1_ref, qkv0_ref, qkv1_ref, b_ref, gc_ref, gr_ref, s_ref, tinv_ref, dsl_ref,
             dx0_ref, dx1_ref, db_ref, dgc_ref, dgr_ref, ds0_ref, ds_ref):
        @pl.when(pl.program_id(0) == 0)
        def _():
            ds_ref[...] = dsl_ref[...]

        tinv = tinv_ref[...]
        q, k, v = [_chain_stack(qkv0_ref, qkv1_ref, part) for part in range(3)]
        do = jnp.stack([(do0_ref if n < GDN_HEADS else do1_ref)[:, _head_cols(n % GDN_HEADS, 0)] for n in range(N_CHAINS)])
        _, vjp = jax.vjp(lambda *a: _gdn_chunk(*a, tinv=tinv)[:2], q, k, v, gc_ref[...], gr_ref[...], b_ref[...], s_ref[...])
        dq, dk, dv, dgc, dgr, db, ds = vjp((do, ds_ref[...]))
        for n in range(N_CHAINS):
            d, h = divmod(n, GDN_HEADS)
            dx_ref = dx0_ref if d == 0 else dx1_ref
            dx_ref[:, _head_cols(h, 0)] = dq[n]
            dx_ref[:, _head_cols(h, 1)] = dk[n]
            dx_ref[:, _head_cols(h, 2)] = dv[n]
        db_ref[...] = db
        dgc_ref[...] = dgc
        dgr_ref[...] = dgr
        ds_ref[...] = ds
        ds0_ref[...] = ds

    vec_c = _sds((N_CHAINS, nc, c, 1))
    vec_r = _sds((N_CHAINS, nc, 1, c))
    return pl.pallas_call(
        body, name=name, grid=(nc,),
        in_specs=heads + seq + [colv, colv, rowv, saved, inv, state],
        out_specs=seq + [colv, colv, rowv, state],
        out_shape=[_sds((t, QKV_W)), _sds((t, QKV_W)), vec_c, vec_c, vec_r, _sds((N_CHAINS, HEAD_DIM, HEAD_DIM))],
        scratch_shapes=[pltpu.VMEM((N_CHAINS, HEAD_DIM, HEAD_DIM), F32)],
        compiler_params=pltpu.CompilerParams(dimension_semantics=("arbitrary",), vmem_limit_bytes=VMEM_LIMIT_BYTES),
    )(do, do, qkv, qkv, bcol, gcol, grow, saved_s, saved_inv, ds_last)


def _gate_layouts(gb):
    t = gb.shape[0]
    nc = t // GDN_CHUNK
    nh = GDN_HEADS

    def by_scan_position(a):
        a = a.T.reshape(2 * nh, nc, GDN_CHUNK)
        return jnp.concatenate([a[:nh], a[nh:, ::-1]], axis=0)

    beta = by_scan_position(gb[:, :2 * nh])
    g = by_scan_position(gb[:, 2 * nh:4 * nh])
    return beta[..., None], g[..., None], g[:, :, None, :]


def _gate_layouts_bwd(dbcol, dgcol, dgrow):
    n2, nc, c, _ = dbcol.shape
    nh = n2 // 2
    t = nc * c

    def by_token(a):
        return jnp.concatenate([a[:nh], a[nh:, ::-1]], axis=0).reshape(n2, t).T

    dbeta = by_token(dbcol[..., 0])
    dg = by_token(dgcol[..., 0] + dgrow[:, :, 0, :])
    return jnp.concatenate([dbeta, dg, jnp.zeros((t, LANES - 2 * n2), F32)], axis=1)


def kernel(x, c, ctx, c_ctx, ada_w, ada_b, ln_g, ln_b, even_w_in, even_w_out, gdn_conv_w, gdn_a_log, gdn_dt_bias, gdn_norm_w, pool_w, pool_scale, odd_w_in, odd_w_out, sconv_w, conf_conv_w, conf_ln_g, conf_ln_b, ffn_w_up, ffn_conv_w, ffn_w_down, loss_target, m_c_ctx, m_ada_w, m_ada_b, m_ln_g, m_ln_b, m_even_w_in, m_even_w_out, m_gdn_conv_w, m_gdn_a_log, m_gdn_dt_bias, m_gdn_norm_w, m_pool_w, m_pool_scale, m_odd_w_in, m_odd_w_out, m_sconv_w, m_conf_conv_w, m_conf_ln_g, m_conf_ln_b, m_ffn_w_up, m_ffn_conv_w, m_ffn_w_down, v_c_ctx, v_ada_w, v_ada_b, v_ln_g, v_ln_b, v_even_w_in, v_even_w_out, v_gdn_conv_w, v_gdn_a_log, v_gdn_dt_bias, v_gdn_norm_w, v_pool_w, v_pool_scale, v_odd_w_in, v_odd_w_out, v_sconv_w, v_conf_conv_w, v_conf_ln_g, v_conf_ln_b, v_ffn_w_up, v_ffn_conv_w, v_ffn_w_down):
    names = ['c_ctx', 'ada_w', 'ada_b', 'ln_g', 'ln_b', 'even_w_in', 'even_w_out', 'gdn_conv_w', 'gdn_a_log',
             'gdn_dt_bias', 'gdn_norm_w', 'pool_w', 'pool_scale', 'odd_w_in', 'odd_w_out', 'sconv_w', 'conf_conv_w',
             'conf_ln_g', 'conf_ln_b', 'ffn_w_up', 'ffn_conv_w', 'ffn_w_down']
    loc = locals()
    wts = {n: loc[n] for n in names}
    mom = {n: loc['m_' + n] for n in names}
    var = {n: loc['v_' + n] for n in names}

    ix, iy, ic = lax.axis_index("x"), lax.axis_index("y"), lax.axis_index("c")
    chip = 2 * ix + iy
    dev = 2 * chip + ic
    d = D_MODEL
    x0, ctx0, tgt = x[0], ctx[0], loss_target[0]
    t, tc = x0.shape[0], ctx0.shape[0]
    depth = ada_w.shape[0]
    n_ada = ada_w.shape[2]

    small_sharded = [gdn_conv_w, sconv_w, conf_conv_w, ffn_conv_w, ln_g, ln_b]
    s1_items = [c] + small_sharded
    s1 = _pack_rows(s1_items)
    g1 = _all_gather8("gather_small_in", s1, True).reshape(8, -1, LANES)
    c_all = g1[:, :_packed_rows(c.shape)].reshape(8, d)
    per_chip = [_unpack_rows(g1[2 * k], [a.shape for a in s1_items])[1:] for k in range(4)]
    gdn_conv_f, sconv_f, conf_conv_f, ffn_conv_f, ln_g_f, ln_b_f = [
        jnp.concatenate([per_chip[k][i] for k in range(4)], axis=-1) for i in range(len(small_sharded))]
    c16 = jnp.concatenate([c_all, c_ctx[None], jnp.zeros((7, d), F32)], axis=0)

    mod_part = _mod_rows(c16, ada_w)
    g2 = _all_gather8("gather_mod", mod_part.reshape(-1, LANES), True).reshape(4, 2, depth, 16, n_ada)[:, 0]
    mod_all = jnp.transpose(g2, (1, 2, 0, 3)).reshape(depth, 16, 4 * n_ada) + ada_b[:, None, :]
    mod_me = lax.dynamic_index_in_dim(mod_all, dev, axis=1, keepdims=False).reshape(depth, 6, 1, d)
    sh_c, sc_c = mod_all[0, 8, :d][None], mod_all[0, 8, d:2 * d][None]

    flat_names = ['even_w_in', 'even_w_out']
    flat_shapes = [wts[n].shape for n in flat_names]
    half_mult = 2 * 16 * LANES
    rh = -(-sum(math.prod(s) for s in flat_shapes) // half_mult) * half_mult // (2 * LANES)
    flat_pack = _pack_flat([wts[n] for n in flat_names], half_mult, MXU_DTYPE).reshape(2, rh, LANES)
    wg_flat = _all_gather8("gather_weights_even", lax.dynamic_index_in_dim(flat_pack, ic, axis=0, keepdims=False), True).reshape(4, -1)
    per_chip = [_unpack_flat(wg_flat[k], flat_shapes) for k in range(4)]
    n_even = 4 * even_w_in.shape[1]
    n_even_pad = -(-n_even // LANES) * LANES
    w_in = jnp.concatenate([per_chip[k][0] for k in range(4)] + [jnp.zeros((d, n_even_pad - n_even), MXU_DTYPE)], axis=1)
    w_out = jnp.concatenate([per_chip[k][1] for k in range(4)], axis=0)

    def row_half(a, core, axis):
        n = a.shape[axis] // 2
        return lax.dynamic_slice_in_dim(a, core * n, n, axis=axis)

    def layer_of(a, core):
        return lax.dynamic_index_in_dim(a, core, axis=0, keepdims=False)

    mine = [row_half(odd_w_in, ic, 0), row_half(odd_w_out, ic, 0), layer_of(ffn_w_up, ic), layer_of(ffn_w_down, ic)]
    late_sems, late_mine, late_lands, late_token = _chip_send_start(
        "gather_late_start", [m.astype(MXU_DTYPE) for m in mine], [wg_flat, mod_all])
    scal_blk = (n_even // LANES)
    n_scal = n_even - scal_blk * LANES

    def mod(layer, k):
        return mod_me[layer, k]

    avec = jnp.zeros((1, LANES), F32).at[0, n_scal // 2:n_scal].set(gdn_a_log.reshape(-1))
    dtvec = jnp.zeros((1, LANES), F32).at[0, n_scal // 2:n_scal].set(gdn_dt_bias.reshape(-1))
    normw = gdn_norm_w[None]
    pscale = pool_scale[None]
    cg, cb = conf_ln_g[None], conf_ln_b[None]
    lng = lambda l, k: ln_g_f[l, k][None]
    lnb = lambda l, k: ln_b_f[l, k][None]
    convw9 = ffn_conv_f.reshape(depth, 9, -1)
    nqkv = gdn_conv_f.shape[1]

    shift0 = mod(0, 0) + late_token[:1, :1]
    p, ub0 = _inproj("even_in", x0, shift0, mod(0, 1), w_in, 512)
    pc, ucb = _inproj("even_in_ctx", ctx0, sh_c, sc_c, w_in, 256)
    qkv = _qkv_conv("qkv_conv", p, gdn_conv_f)
    qkv_c = _qkv_conv("qkv_conv_ctx", pc, gdn_conv_f)
    gb = _gates("gates", p, avec, dtvec, scal_blk)
    gb_c = _gates("gates_ctx", pc, avec, dtvec, scal_blk)
    lay = _gate_layouts(gb)
    lay_c = _gate_layouts(gb_c)
    s_zero = jnp.zeros((2 * GDN_HEADS, HEAD_DIM, HEAD_DIM), F32)
    _, _, save_c, inv_c, s_ctx = _scan_fwd("scan_ctx", qkv_c, *lay_c, s_zero)
    o0, o1, save_l, inv_l, _ = _scan_fwd("scan", qkv, *lay, s_ctx)
    pool_blk = (nqkv + GDN_HEADS * HEAD_DIM) // 512
    ypool = _pool(p, pool_w, pscale, pool_blk)
    x1, mix0, y0 = _even_out(o0, o1, p, ypool, x0, normw, mod(0, 2), lng(0, 0), lnb(0, 0), w_out, 256)

    late_mine, late_lands = _chip_send_wait("gather_late_wait", late_sems, late_mine, late_lands, x1)
    own4 = [lax.dynamic_update_slice(land, m[None], (chip,) + (0,) * m.ndim) for land, m in zip(late_lands, late_mine)]
    got4 = _sibling_exchange("gather_late_exchange", own4)

    def of_core(i, core):
        return jnp.where(ic == core, own4[i], got4[i])

    w_oin = jnp.concatenate([of_core(0, 0), of_core(0, 1)], axis=1)
    w_oout = jnp.concatenate([of_core(1, 0), of_core(1, 1)], axis=1).reshape(-1, d)
    w_up = [of_core(2, l) for l in range(depth)]
    w_down = [of_core(3, l) for l in range(depth)]

    def ffn_fwd(l, xin):
        h, ub = _inproj(f"ffn_up{l}", xin, mod(l, 3), mod(l, 4), w_up[l], 256)
        s = _ffn_conv(f"ffn_conv{l}", h, convw9[l])
        xo, y = _ffn_down(f"ffn_down{l}", s, xin, mod(l, 5), lng(l, 1), lnb(l, 1), w_down[l], 256)
        return xo, (h, ub, s, y)

    x2, ffn0 = ffn_fwd(0, x1)
    p1, ub1 = _inproj("odd_in", x2, mod(1, 0), mod(1, 1), w_oin, 512)
    nsc = sconv_f.shape[1] // LANES
    ysc = _sconv(p1, sconv_f)
    z0 = _confconv(p1, conf_conv_f, 3 * nsc)
    x3, mix1, y1 = _odd_out(ysc, z0, x2, cg, cb, mod(1, 2), lng(1, 0), lnb(1, 0), w_oout, 256)
    x4, ffn1 = ffn_fwd(1, x3)
    loss_part, dx4 = _loss_and_grad(x4, tgt, 512)
    loss = lax.psum(loss_part[0, 0], ("x", "y", "c"))

    dmod = [[None] * 6 for _ in range(depth)]
    dlng = [[None, None] for _ in range(depth)]
    dlnb = [[None, None] for _ in range(depth)]
    gbig = {}
    dconv9 = [None] * depth

    def ffn_bwd(l, dxo, xin, saved):
        h, ub, s, y = saved
        ds, dxa, dyb, dgt, dg_, db_ = _ffn_down_bwd(f"ffn_down_bwd{l}", dxo, xin, y, mod(l, 5), lng(l, 1), lnb(l, 1), w_down[l], 256)
        da, dgate, dw9 = _ffn_conv_bwd(f"ffn_conv_bwd{l}", ds, h, convw9[l])
        dh = jnp.concatenate([da, dgate], axis=1)
        dxin, dsh, dsc = _inproj_bwd(f"ffn_up_bwd{l}", dh, xin, mod(l, 3), mod(l, 4), w_up[l], dxa, 256)
        dmod[l][3], dmod[l][4], dmod[l][5] = dsh, dsc, dgt
        dlng[l][1], dlnb[l][1] = dg_, db_
        dconv9[l] = dw9
        dw_up = _matmul_tn(f"dw_up{l}", [(ub, dh)], chip_cols=ffn_w_up.shape[2])
        dw_down = _matmul_tn(f"dw_down{l}", [(s, dyb)]).reshape(4, -1, d)
        return dxin, dw_up, dw_down

    dx3, dwu1, dwd1 = ffn_bwd(1, dx4, x3, ffn1)
    dysc, dz0, dx2a, dyb1, dcg, dcb, dgt, dg_, db_ = _odd_out_bwd(dx3, ysc, z0, x2, y1, cg, cb, mod(1, 2), lng(1, 0), lnb(1, 0), w_oout, 256)
    dmod[1][2], dlng[1][0], dlnb[1][0] = dgt, dg_, db_
    d_gb, d_gc, d_h, dsconv = _sconv_bwd(dysc, p1, sconv_f)
    d_ga, d_gbb, dconf = _confconv_bwd(dz0, p1, conf_conv_f, 3 * nsc)
    dp1 = jnp.concatenate([d_gb, d_gc, d_h, d_ga, d_gbb], axis=1)
    dx2, dsh, dsc = _inproj_bwd("odd_in_bwd", dp1, x2, mod(1, 0), mod(1, 1), w_oin, dx2a, 512)
    dmod[1][0], dmod[1][1] = dsh, dsc
    dw_oout = _matmul_tn("dw_oout", [(mix1, dyb1)]).reshape(4, -1, d)
    dw_oin = _matmul_tn("dw_oin", [(ub1, dp1)], chip_cols=odd_w_in.shape[1])

    dx1, dwu0, dwd0 = ffn_bwd(0, dx2, x1, ffn0)

    do, dpg, dypool, dx0a, dyb0, dnormw, dgt, dg_, db_ = _even_out_bwd(dx1, o0, o1, p, ypool, x0, y0, normw, mod(0, 2), lng(0, 0), lnb(0, 0), w_out, 256)
    dmod[0][2], dlng[0][0], dlnb[0][0] = dgt, dg_, db_
    pool_cts = _pool_bwd(dypool, p, pool_w, pscale, pool_blk)
    dpp, dpool_scale, dpool_w = pool_cts[0], pool_cts[1], jnp.stack(pool_cts[2:])
    dqkv0, dqkv1, dbcol, dgcol, dgrow, ds0 = _scan_bwd("scan_bwd", do, qkv, *lay, save_l, inv_l, s_zero)
    zero_do = jnp.zeros((tc, GDN_HEADS * HEAD_DIM), F32)
    dqkv0_c, dqkv1_c, dbcol_c, dgcol_c, dgrow_c, _ = _scan_bwd("scan_bwd_ctx", zero_do, qkv_c, *lay_c, save_c, inv_c, ds0)
    dgb = _gate_layouts_bwd(dbcol, dgcol, dgrow)
    dgb_c = _gate_layouts_bwd(dbcol_c, dgcol_c, dgrow_c)
    dps, davec, ddtvec = _gates_bwd("gates_bwd", dgb, p, avec, dtvec, scal_blk)
    dps_c, davec_c, ddtvec_c = _gates_bwd("gates_bwd_ctx", dgb_c, pc, avec, dtvec, scal_blk)
    dpqkv, dconv5 = _qkv_conv_bwd("qkv_conv_bwd", dqkv0, dqkv1, p, gdn_conv_f)
    dpqkv_c, dconv5_c = _qkv_conv_bwd("qkv_conv_bwd_ctx", dqkv0_c, dqkv1_c, pc, gdn_conv_f)
    dp = jnp.concatenate([dpqkv, dpg, dpp, dps], axis=1)
    dpc = jnp.concatenate([dpqkv_c, jnp.zeros((tc, n_even_pad - nqkv - LANES), MXU_DTYPE), dps_c], axis=1)
    grad_x, dsh, dsc = _inproj_bwd("even_in_bwd", dp, x0, mod(0, 0), mod(0, 1), w_in, dx0a, 512)
    dmod[0][0], dmod[0][1] = dsh, dsc
    _, dsh_c, dsc_c = _inproj_bwd("even_in_bwd_ctx", dpc, ctx0, sh_c, sc_c, w_in, None, 256)
    dw_in = _matmul_tn("dw_in", [(ub0, dp), (ucb, dpc)])
    dw_out = _matmul_tn("dw_out", [(mix0, dyb0)])

    n_in, n_out = even_w_in.shape[1], even_w_out.shape[0]
    gflat = jnp.stack([_pack_flat([dw_in[:, k * n_in:(k + 1) * n_in], dw_out[k * n_out:(k + 1) * n_out]], half_mult,
                                  MXU_DTYPE).reshape(2, rh, LANES) for k in range(4)], axis=1)
    halves = [
        (gflat[0], gflat[1]),
        (dw_oin[:, :d // 2], dw_oin[:, d // 2:]),
        (dw_oout[:, :odd_w_out.shape[0] // 2], dw_oout[:, odd_w_out.shape[0] // 2:]),
        (dwu0, dwu1),
        (dwd0, dwd1),
    ]
    keeps = [jnp.where(ic == 0, h0, h1) for h0, h1 in halves]
    gives = [jnp.where(ic == 0, h1, h0) for h0, h1 in halves]
    gots = _sibling_exchange("grad_pair_exchange", gives)
    pairs = [_sum_pair(f"grad_pair_sum{i}", kp, gt_, MXU_DTYPE) for i, (kp, gt_) in enumerate(zip(keeps, gots))]
    parts = _chip_scatter("grad_chip_scatter", pairs)
    sums = [_sum_axis1(f"grad_chip_sum{i}", p_) for i, p_ in enumerate(parts)]
    g_even_in, g_even_out = _unpack_flat(sums[0].reshape(-1), flat_shapes)
    g_shards = {'even_w_in': g_even_in, 'even_w_out': g_even_out, 'odd_w_in': sums[1].reshape(odd_w_in.shape),
                'odd_w_out': sums[2].reshape(odd_w_out.shape), 'ffn_w_up': sums[3], 'ffn_w_down': sums[4]}

    dmod_rows = jnp.stack([jnp.concatenate(dmod[l], axis=1)[0] for l in range(depth)])
    dmod_c = jnp.concatenate([dsh_c[0], dsc_c[0], jnp.zeros((4 * d,), F32)])
    dmod_c_rows = jnp.stack([dmod_c] + [jnp.zeros_like(dmod_c)] * (depth - 1))
    small_g = {
        'ln_g': jnp.stack([jnp.stack([dlng[l][k][0] for k in range(2)]) for l in range(depth)]),
        'ln_b': jnp.stack([jnp.stack([dlnb[l][k][0] for k in range(2)]) for l in range(depth)]),
        'gdn_conv_w': dconv5 + dconv5_c,
        'gdn_a_log': (davec + davec_c)[0, n_scal // 2:n_scal].reshape(gdn_a_log.shape),
        'gdn_dt_bias': (ddtvec + ddtvec_c)[0, n_scal // 2:n_scal].reshape(gdn_dt_bias.shape),
        'gdn_norm_w': dnormw[0], 'pool_w': dpool_w, 'pool_scale': dpool_scale[0],
        'sconv_w': dsconv, 'conf_conv_w': dconf, 'conf_ln_g': dcg[0], 'conf_ln_b': dcb[0],
        'ffn_conv_w': jnp.stack(dconv9).reshape(depth, 3, 3, -1),
    }
    small_names = list(small_g)
    s3_items = [dmod_rows, dmod_c_rows] + [small_g[n] for n in small_names]
    s3 = _pack_rows(s3_items)
    g3 = _all_gather8("gather_small_grads", s3, True).reshape(8, -1, LANES)
    tot3 = _sum_axis1("small_grad_sum", g3[None])[0]
    tot_items = _unpack_rows(tot3, [a.shape for a in s3_items])
    dmod_sum, dmod_c_sum = tot_items[0], tot_items[1]
    small_tot = dict(zip(small_names, tot_items[2:]))
    grad_ada_b = dmod_sum + dmod_c_sum
    rows_all = g3[:, :_packed_rows(dmod_rows.shape)].reshape(8, depth, 6 * d)
    cols = lax.dynamic_slice_in_dim(rows_all, chip * n_ada, n_ada, axis=2)
    crow = lax.dynamic_slice_in_dim(dmod_c_sum, chip * n_ada, n_ada, axis=1)
    dmod16 = jnp.concatenate([jnp.transpose(cols, (1, 0, 2)), crow[:, None, :], jnp.zeros((depth, 7, n_ada), F32)], axis=1)
    grad_ada_w, dsil = _ada_grads(c16, dmod16, ada_w)
    s4 = jnp.concatenate([dsil[0, 8][None], jnp.zeros((7, d), F32)], axis=0).reshape(-1, LANES)
    g4 = _all_gather8("gather_cctx", s4, True).reshape(8, 8, d)
    grad_c_ctx = _cctx_grad(g4[0::2, 0][:, None, :], c_ctx[None])[0]

    def my_cols(a, n):
        return lax.dynamic_slice_in_dim(a, chip * n, n, axis=a.ndim - 1)

    grads = dict(g_shards)
    grads['c_ctx'] = grad_c_ctx
    grads['ada_w'] = grad_ada_w
    grads['ada_b'] = grad_ada_b
    for n in ['ln_g', 'ln_b', 'gdn_conv_w', 'sconv_w', 'conf_conv_w', 'ffn_conv_w']:
        grads[n] = my_cols(small_tot[n], wts[n].shape[-1])
    for n in ['gdn_a_log', 'gdn_dt_bias', 'gdn_norm_w', 'pool_w', 'pool_scale', 'conf_ln_g', 'conf_ln_b']:
        grads[n] = small_tot[n]

    delta, new_m, new_v = {}, {}, {}
    big_adam = list(g_shards) + ['ada_w']
    for n in big_adam:
        shp = wts[n].shape
        as2d = lambda a: a.reshape(-1, shp[-1])
        dl, nm, nv = _adamw("adamw_" + n, as2d(wts[n]), as2d(grads[n]), as2d(mom[n]), as2d(var[n]))
        delta[n], new_m[n], new_v[n] = dl.reshape(shp), nm.reshape(shp), nv.reshape(shp)
    small_adam = [n for n in names if n not in big_adam]
    for n in small_adam:
        shp = wts[n].shape
        whole = lambda a: a.reshape((1,) + shp) if len(shp) == 1 else a
        dl, nm, nv = _adamw("adamw_" + n, whole(wts[n]), whole(grads[n]), whole(mom[n]), whole(var[n]))
        delta[n], new_m[n], new_v[n] = dl.reshape(shp), nm.reshape(shp), nv.reshape(shp)

    return (loss, grad_x[None], *[grads[n] for n in names], *[delta[n] for n in names],
            *[new_m[n] for n in names], *[new_v[n] for n in names])
```

```python
import functools
import math

import jax
import jax.numpy as jnp
from jax import lax
from jax.experimental import pallas as pl
from jax.experimental.pallas import tpu as pltpu

F32 = jnp.float32
MXU_DTYPE = jnp.bfloat16
HIGHEST = lax.Precision.HIGHEST
MESH = pl.DeviceIdType.MESH

D_MODEL = 1024
GRID_W = 64
GDN_HEADS = 4
HEAD_DIM = 128
GDN_CHUNK = 64
POOL_WINDOWS = (2, 4, 8, 16)
GDN_CONV = 5
SC_WIDTH = 3
CF_WIDTH = 31
ALPHA = 4.0 ** 0.25
LN_EPS = 1e-5
RMS_EPS = 1e-6
LANES = 128
VMEM_LIMIT_BYTES = 58 * 1024 * 1024

ADAM_LR, ADAM_B1, ADAM_B2, ADAM_EPS, ADAM_WD, ADAM_STEP = 0.001, 0.9, 0.999, 1e-08, 0.01, 10


def _blocked(name, fn, grid, ins, in_specs, out_shapes, out_specs, acc=(), acc_axis=None, scratch=()):
    n_in = len(ins)
    n_out = len(out_shapes)

    def body(*refs):
        vals = [r[...] for r in refs[:n_in]]
        res = fn(*vals, *refs[n_in + n_out:])
        if not isinstance(res, (tuple, list)):
            res = (res,)
        for k, (r, v) in enumerate(zip(refs[n_in:n_in + n_out], res)):
            if k in acc:
                first = pl.program_id(acc_axis) == 0

                @pl.when(first)
                def _(r=r, v=v):
                    r[...] = v.astype(r.dtype)

                @pl.when(jnp.logical_not(first))
                def _(r=r, v=v):
                    r[...] += v.astype(r.dtype)
            else:
                r[...] = v.astype(r.dtype)

    return pl.pallas_call(
        body, name=name, grid=grid, in_specs=in_specs, out_specs=out_specs, out_shape=out_shapes,
        scratch_shapes=list(scratch),
        compiler_params=pltpu.CompilerParams(dimension_semantics=("arbitrary",) * len(grid),
                                             vmem_limit_bytes=VMEM_LIMIT_BYTES),
    )(*ins)


def _sds(shape, dtype=F32):
    return jax.ShapeDtypeStruct(tuple(shape), dtype)


def _tok(tm, n, col=0):
    return pl.BlockSpec((tm, n), lambda t: (t, col))


def _res(shape):
    nd = len(shape)
    return pl.BlockSpec(tuple(shape), lambda t: (0,) * nd)


def _silu(x):
    return x * jax.nn.sigmoid(x)


def _layernorm(r, g, b):
    mu = jnp.mean(r, -1, keepdims=True)
    d = r - mu
    var = jnp.mean(d * d, -1, keepdims=True)
    return d * lax.rsqrt(var + LN_EPS) * g + b


def _mm_nn_impl(a, b):
    return jnp.dot(a.astype(MXU_DTYPE), b.astype(MXU_DTYPE), preferred_element_type=F32)


def _mm_nt_impl(a, b):
    return lax.dot_general(a.astype(MXU_DTYPE), b.astype(MXU_DTYPE), (((1,), (1,)), ((), ())), preferred_element_type=F32)


def _mm_tn_impl(a, b):
    return lax.dot_general(a.astype(MXU_DTYPE), b.astype(MXU_DTYPE), (((0,), (0,)), ((), ())), preferred_element_type=F32)


def _mm_vjp(mm, mm_da, mm_db):
    f = jax.custom_vjp(mm)
    f.defvjp(lambda a, b: (mm(a, b), (a, b)), lambda res, g: (mm_da(g, res[1]), mm_db(res[0], g)))
    return f


_mm = _mm_vjp(_mm_nn_impl, lambda g, b: _mm_nt_impl(g, b), lambda a, g: _mm_tn_impl(a, g))
_mm_nt = _mm_vjp(_mm_nt_impl, lambda g, b: _mm_nn_impl(g, b), lambda a, g: _mm_tn_impl(g, a))
_mm_tn = _mm_vjp(_mm_tn_impl, lambda g, b: _mm_nt_impl(b, g), lambda a, g: _mm_nn_impl(a, g))


def _row(w, k):
    rows = lax.broadcasted_iota(jnp.int32, w.shape, 0)
    return jnp.sum(jnp.where(rows == k, w, 0.0), axis=0, keepdims=True)


def _col_mask(shape, dc):
    col = lax.broadcasted_iota(jnp.int32, shape, 0) & (GRID_W - 1)
    return (col + dc >= 0) & (col + dc < GRID_W)


def _center(x_ext, halo):
    return x_ext[halo:x_ext.shape[0] - halo]


def _make_dwconv(taps, halo):
    assert all(abs(s) <= halo for s, _ in taps)

    def shifted(x_ext, s):
        r = x_ext if s == 0 else pltpu.roll(x_ext, (-s) % x_ext.shape[0], 0)
        return _center(r, halo)

    @jax.custom_vjp
    def conv(x_ext, w):
        acc = None
        for k, (s, dc) in enumerate(taps):
            r = shifted(x_ext, s)
            if dc != 0:
                r = jnp.where(_col_mask(r.shape, dc), r, 0.0)
            term = r * _row(w, k)
            acc = term if acc is None else acc + term
        return acc

    def fwd(x_ext, w):
        return conv(x_ext, w), (x_ext, w)

    def bwd(res, dy):
        x_ext, w = res
        n = x_ext.shape[0]
        rows = lax.broadcasted_iota(jnp.int32, w.shape, 0)
        pad = jnp.zeros((halo, dy.shape[1]), F32)
        dx = None
        dw = jnp.zeros(w.shape, F32)
        for k, (s, dc) in enumerate(taps):
            dym = dy if dc == 0 else jnp.where(_col_mask(dy.shape, dc), dy, 0.0)
            dw = dw + jnp.where(rows == k, jnp.sum(dym * shifted(x_ext, s), axis=0, keepdims=True), 0.0)
            t = jnp.concatenate([pad, dym * _row(w, k), pad], axis=0)
            if s != 0:
                t = pltpu.roll(t, s % n, 0)
            dx = t if dx is None else dx + t
        return dx, dw

    conv.defvjp(fwd, bwd)
    return conv


def _taps_1d(width):
    return tuple((k - width // 2, 0) for k in range(width))


HALO_SHORT = 8
HALO_CONF = 16
HALO_GRID = 72
_conv5 = _make_dwconv(_taps_1d(GDN_CONV), HALO_SHORT)
_conv3 = _make_dwconv(_taps_1d(SC_WIDTH), HALO_SHORT)
_conv31 = _make_dwconv(_taps_1d(CF_WIDTH), HALO_CONF)
_conv3x3 = _make_dwconv(tuple((dr * GRID_W + dc, dc) for dr in (-1, 0, 1) for dc in (-1, 0, 1)), HALO_GRID)
_pool_sums = {w: _make_dwconv(tuple((s, 0) for s in range(-(w // 2), w - w // 2)), HALO_SHORT) for w in POOL_WINDOWS}


def _all_gather8(name, blk, in_vmem):
    m_per, n = blk.shape
    space = pltpu.VMEM if in_vmem else pl.ANY

    def body(x_ref, out_ref, send_sems, recv_sems, local_sem):
        x, y, c = lax.axis_index("x"), lax.axis_index("y"), lax.axis_index("c")
        me, sibling = (x, y, c), (x, y, 1 - c)
        chips = [(1 - x, y), (x, 1 - y), (1 - x, 1 - y)]

        def rows(px, py, pc):
            return out_ref.at[pl.ds((4 * px + 2 * py + pc) * m_per, m_per), :]

        def copy(k, block, to, src=None):
            return pltpu.make_async_remote_copy(
                src_ref=rows(*block) if src is None else src, dst_ref=rows(*block),
                send_sem=send_sems.at[k], recv_sem=recv_sems.at[k], device_id=to, device_id_type=MESH)

        mine = pltpu.make_async_copy(x_ref, rows(*me), local_sem)
        mine.start()
        first = [copy(0, me, sibling, src=x_ref)]
        first += [copy(1 + j, me, (*chip, c), src=x_ref) for j, chip in enumerate(chips)]
        for cp in first:
            cp.start()
        passed = [copy(4 + j, (*chip, c), sibling) for j, chip in enumerate(chips)]
        for j, chip in enumerate(chips):
            copy(1 + j, (*chip, c), me).wait_recv()
            passed[j].start()
        copy(0, sibling, me).wait_recv()
        for j, chip in enumerate(chips):
            copy(4 + j, (*chip, 1 - c), me).wait_recv()
        for cp in first + passed:
            cp.wait_send()
        mine.wait()

    return pl.pallas_call(
        body, name=name, out_shape=_sds((8 * m_per, n), blk.dtype),
        in_specs=[pl.BlockSpec(memory_space=space)], out_specs=pl.BlockSpec(memory_space=space),
        scratch_shapes=[pltpu.SemaphoreType.DMA((7,)), pltpu.SemaphoreType.DMA((7,)), pltpu.SemaphoreType.DMA],
        compiler_params=pltpu.CompilerParams(vmem_limit_bytes=VMEM_LIMIT_BYTES),
    )(blk)


_ANY = pl.BlockSpec(memory_space=pl.ANY)


def _all_gather8_multi(name, blocks):
    n = len(blocks)

    def body(*refs):
        x_refs, out_refs = refs[:n], refs[n:2 * n]
        send_sems, recv_sems, local_sems = refs[2 * n:]
        x, y, c = lax.axis_index("x"), lax.axis_index("y"), lax.axis_index("c")
        me, sibling = (x, y, c), (x, y, 1 - c)
        chips = [(1 - x, y), (x, 1 - y), (1 - x, 1 - y)]

        def slot(i, px, py, pc):
            return out_refs[i].at[4 * px + 2 * py + pc]

        def copy(i, k, block, to, src=None):
            return pltpu.make_async_remote_copy(
                src_ref=slot(i, *block) if src is None else src, dst_ref=slot(i, *block),
                send_sem=send_sems.at[i, k], recv_sem=recv_sems.at[i, k], device_id=to, device_id_type=MESH)

        mine = [pltpu.make_async_copy(x_refs[i], slot(i, *me), local_sems.at[i]) for i in range(n)]
        first = []
        for i in range(n):
            first.append(copy(i, 0, me, sibling, src=x_refs[i]))
            first += [copy(i, 1 + j, me, (*chip, c), src=x_refs[i]) for j, chip in enumerate(chips)]
        for cp in mine + first:
            cp.start()
        passed = []
        for j, chip in enumerate(chips):
            for i in range(n):
                copy(i, 1 + j, (*chip, c), me).wait_recv()
                passed.append(copy(i, 4 + j, (*chip, c), sibling))
                passed[-1].start()
        for i in range(n):
            copy(i, 0, sibling, me).wait_recv()
            for j, chip in enumerate(chips):
                copy(i, 4 + j, (*chip, 1 - c), me).wait_recv()
        for cp in first + passed:
            cp.wait_send()
        for cp in mine:
            cp.wait()

    return pl.pallas_call(
        body, name=name, out_shape=[_sds((8,) + b.shape, b.dtype) for b in blocks],
        in_specs=[_ANY] * n, out_specs=[_ANY] * n,
        scratch_shapes=[pltpu.SemaphoreType.DMA((n, 7)), pltpu.SemaphoreType.DMA((n, 7)), pltpu.SemaphoreType.DMA((n,))],
    )(*blocks)


def _sibling_exchange(name, sends):
    n = len(sends)

    def body(*refs):
        x, y, c = lax.axis_index("x"), lax.axis_index("y"), lax.axis_index("c")
        send_sems, recv_sems = refs[2 * n:]
        copies = [pltpu.make_async_remote_copy(src_ref=refs[i], dst_ref=refs[n + i], send_sem=send_sems.at[i],
                                               recv_sem=recv_sems.at[i], device_id=(x, y, 1 - c), device_id_type=MESH)
                  for i in range(n)]
        for cp in copies:
            cp.start()
        for cp in copies:
            cp.wait()

    return pl.pallas_call(
        body, name=name, out_shape=[_sds(s.shape, s.dtype) for s in sends],
        in_specs=[_ANY] * n, out_specs=[_ANY] * n,
        scratch_shapes=[pltpu.SemaphoreType.DMA((n,)), pltpu.SemaphoreType.DMA((n,))],
    )(*sends)


def _chip_scatter(name, parts):
    n = len(parts)

    def body(*refs):
        p_refs, out_refs = refs[:n], refs[n:2 * n]
        send_sems, recv_sems, local_sems = refs[2 * n:]
        x, y, c = lax.axis_index("x"), lax.axis_index("y"), lax.axis_index("c")
        sibling = (x, y, 1 - c)
        me_chip = 2 * x + y
        chips = [(1 - x, y), (x, 1 - y), (1 - x, 1 - y)]

        def chip_id(chip):
            return 2 * chip[0] + chip[1]

        def copy(i, k, src, dst, to):
            return pltpu.make_async_remote_copy(src_ref=src, dst_ref=dst, send_sem=send_sems.at[i, k],
                                                recv_sem=recv_sems.at[i, k], device_id=to, device_id_type=MESH)

        mine = [pltpu.make_async_copy(p_refs[i].at[me_chip], out_refs[i].at[c, me_chip], local_sems.at[i]) for i in range(n)]
        first = []
        for i in range(n):
            first.append(copy(i, 0, p_refs[i].at[me_chip], out_refs[i].at[c, me_chip], sibling))
            first += [copy(i, 1 + j, p_refs[i].at[chip_id(chip)], out_refs[i].at[c, me_chip], (*chip, c))
                      for j, chip in enumerate(chips)]
        for cp in mine + first:
            cp.start()
        passed = []
        for j, chip in enumerate(chips):
            for i in range(n):
                landed = out_refs[i].at[c, chip_id(chip)]
                copy(i, 1 + j, p_refs[i].at[0], landed, sibling).wait_recv()
                passed.append(copy(i, 4 + j, landed, landed, sibling))
                passed[-1].start()
        for i in range(n):
            copy(i, 0, p_refs[i].at[0], out_refs[i].at[1 - c, me_chip], sibling).wait_recv()
            for j, chip in enumerate(chips):
                copy(i, 4 + j, p_refs[i].at[0], out_refs[i].at[1 - c, chip_id(chip)], sibling).wait_recv()
        for cp in first + passed:
            cp.wait_send()
        for cp in mine:
            cp.wait()

    return pl.pallas_call(
        body, name=name, out_shape=[_sds((2,) + p.shape, p.dtype) for p in parts],
        in_specs=[_ANY] * n, out_specs=[_ANY] * n,
        scratch_shapes=[pltpu.SemaphoreType.DMA((n, 7)), pltpu.SemaphoreType.DMA((n, 7)), pltpu.SemaphoreType.DMA((n,))],
    )(*parts)


_HBM = pl.BlockSpec(memory_space=pltpu.HBM)
_SEM = pl.BlockSpec(memory_space=pltpu.SEMAPHORE)
_SPLIT_CALL = pltpu.CompilerParams(has_side_effects=pltpu.SideEffectType.DATAFLOW_SIDE_EFFECTING)


def _peer_chips():
    x, y = lax.axis_index("x"), lax.axis_index("y")
    return [(1 - x, y), (x, 1 - y), (1 - x, 1 - y)]


def _chip_send_start(name, blocks, before, per_peer=False):
    n, nb = len(blocks), len(before)

    def body(*refs):
        x_refs, land_refs = refs[:n], refs[n:2 * n]
        sems, token = refs[2 * n + nb:2 * n + nb + 6 * n], refs[-1]
        c = lax.axis_index("c")
        me_chip = 2 * lax.axis_index("x") + lax.axis_index("y")
        for i in range(n):
            for j, chip in enumerate(_peer_chips()):
                src = x_refs[i].at[2 * chip[0] + chip[1]] if per_peer else x_refs[i]
                pltpu.make_async_remote_copy(src_ref=src, dst_ref=land_refs[i].at[me_chip], send_sem=sems[6 * i + j],
                                             recv_sem=sems[6 * i + 3 + j], device_id=(*chip, c), device_id_type=MESH).start()
        token[...] = jnp.zeros(token.shape, token.dtype)

    lands = [lax.empty(b.shape if per_peer else (4,) + b.shape, b.dtype) for b in blocks]
    hbm = [pltpu.with_memory_space_constraint(a, pltpu.HBM) for a in list(blocks) + lands]
    outs = pl.pallas_call(
        body, name=name,
        out_shape=tuple([pltpu.SemaphoreType.DMA(())] * (6 * n)) + tuple(pltpu.HBM(a.shape, a.dtype) for a in hbm) + (_sds((8, LANES)),),
        in_specs=tuple([_HBM] * (2 * n) + [_ANY] * nb),
        out_specs=tuple([_SEM] * (6 * n) + [_HBM] * (2 * n) + [pl.BlockSpec(memory_space=pltpu.VMEM)]),
        input_output_aliases={i: 6 * n + i for i in range(2 * n)}, compiler_params=_SPLIT_CALL,
    )(*hbm, *before)
    return outs[:6 * n], outs[6 * n:7 * n], outs[7 * n:8 * n], outs[8 * n]


def _chip_send_wait(name, sems, blocks, lands, after, per_peer=False):
    n = len(blocks)

    def body(*refs):
        x_refs, land_refs, sems = refs[:n], refs[n:2 * n], refs[2 * n:8 * n]
        c = lax.axis_index("c")
        for i in range(n):
            for j, chip in enumerate(_peer_chips()):
                src = x_refs[i].at[2 * chip[0] + chip[1]] if per_peer else x_refs[i]
                cp = pltpu.make_async_remote_copy(src_ref=src, dst_ref=land_refs[i].at[2 * chip[0] + chip[1]],
                                                  send_sem=sems[6 * i + j], recv_sem=sems[6 * i + 3 + j],
                                                  device_id=(*chip, c), device_id_type=MESH)
                cp.wait_send()
                cp.wait_recv()

    outs = pl.pallas_call(
        body, name=name, out_shape=tuple(pltpu.HBM(a.shape, a.dtype) for a in list(blocks) + list(lands)),
        in_specs=tuple([_HBM] * (2 * n) + [_SEM] * (6 * n) + [_ANY]), out_specs=tuple([_HBM] * (2 * n)),
        input_output_aliases={i: i for i in range(2 * n)}, compiler_params=_SPLIT_CALL,
    )(*blocks, *lands, *sems, after)
    return outs[:n], outs[n:]


def _pack_flat(arrays, multiple, dtype):
    flat = jnp.concatenate([a.reshape(-1).astype(dtype) for a in arrays])
    pad = (-flat.shape[0]) % multiple
    if pad:
        flat = jnp.concatenate([flat, jnp.zeros((pad,), dtype)])
    return flat


def _unpack_flat(flat, shapes):
    out, off = [], 0
    for s in shapes:
        n = math.prod(s)
        out.append(flat[off:off + n].reshape(s))
        off += n
    return out


def _packed_rows(shape):
    return -(-math.prod(shape) // (8 * LANES)) * 8


def _pack_rows(arrays):
    parts = []
    for a in arrays:
        rows = _packed_rows(a.shape)
        flat = a.reshape(-1).astype(F32)
        if rows * LANES != flat.shape[0]:
            flat = jnp.pad(flat, (0, rows * LANES - flat.shape[0]))
        parts.append(flat.reshape(rows, LANES))
    return jnp.concatenate(parts, axis=0)


def _unpack_rows(packed, shapes):
    out, r0 = [], 0
    for s in shapes:
        rows = _packed_rows(s)
        out.append(packed[r0:r0 + rows].reshape(-1)[:math.prod(s)].reshape(s))
        r0 += rows
    return out


def _row_tile(r, cap, mult=16):
    for cand in range(min(cap, r) // mult * mult, 0, -mult):
        if r % cand == 0:
            return cand
    return r


ELEMENTWISE_BLOCK_ELEMS = 1 << 20


def _sum_axis1(name, x, out_dtype=F32):
    h, k, r, n = x.shape
    tr = _row_tile(r, max(16, ELEMENTWISE_BLOCK_ELEMS // (k * n)), 32 // x.dtype.itemsize)

    def fn(s):
        acc = s[0].astype(F32)
        for i in range(1, k):
            acc = acc + s[i].astype(F32)
        return acc

    return _blocked(name, fn, (h, r // tr), [x], [pl.BlockSpec((None, k, tr, n), lambda i, t: (i, 0, t, 0))],
                    [_sds((h, r, n), out_dtype)], [pl.BlockSpec((None, tr, n), lambda i, t: (i, t, 0))])[0]


def _sum_pair(name, a, b, out_dtype):
    lead, r, n = a.shape
    tr = _row_tile(r, max(16, ELEMENTWISE_BLOCK_ELEMS // n), 32 // min(a.dtype.itemsize, jnp.dtype(out_dtype).itemsize))
    spec = pl.BlockSpec((None, tr, n), lambda i, t: (i, t, 0))
    return _blocked(name, lambda x, y: x.astype(F32) + y.astype(F32), (lead, r // tr), [a, b], [spec, spec],
                    [_sds((lead, r, n), out_dtype)], [spec])[0]


def _adamw(name, w, g, m, v):
    if w.ndim == 2:
        r, c = w.shape
        tr = _row_tile(r, max(8, (1 << 21) // (4 * c)), 8)
        grid, spec = (r // tr,), pl.BlockSpec((tr, c), lambda t: (t, 0))
    else:
        grid, spec = (1,), _res(w.shape)
    bc1 = 1.0 - ADAM_B1 ** ADAM_STEP
    bc2 = 1.0 - ADAM_B2 ** ADAM_STEP

    def fn(w, g, m, v):
        nm = ADAM_B1 * m + (1.0 - ADAM_B1) * g
        nv = ADAM_B2 * v + (1.0 - ADAM_B2) * (g * g)
        delta = -ADAM_LR * ((nm / bc1) / (jnp.sqrt(nv / bc2) + ADAM_EPS) + ADAM_WD * w)
        return delta, nm, nv

    return _blocked(name, fn, grid, [w, g, m, v], [spec] * 4, [_sds(w.shape)] * 3, [spec] * 3)


def _mod_rows(c16, ada_w):
    depth, d, n = ada_w.shape

    def fn(c, w):
        return _mm(_silu(c), w)

    return _blocked("mod_rows", fn, (depth,), [c16, ada_w],
                    [_res(c16.shape), pl.BlockSpec((None, d, n), lambda l: (l, 0, 0))],
                    [_sds((depth, 16, n))], [pl.BlockSpec((None, 16, n), lambda l: (l, 0, 0))])[0]


def _ada_grads(c16, dmod16, ada_w):
    depth, d, n = ada_w.shape

    def fn(c, dm, w):
        return _mm_tn(_silu(c), dm), _mm_nt(dm, w)

    return _blocked("ada_grads", fn, (depth,), [c16, dmod16, ada_w],
                    [_res(c16.shape), pl.BlockSpec((None, 16, n), lambda l: (l, 0, 0)),
                     pl.BlockSpec((None, d, n), lambda l: (l, 0, 0))],
                    [_sds((depth, d, n)), _sds((depth, 16, d))],
                    [pl.BlockSpec((None, d, n), lambda l: (l, 0, 0)), pl.BlockSpec((None, 16, d), lambda l: (l, 0, 0))])


def _cctx_grad(parts, c_ctx_row):
    def fn(p, c):
        tot = ((p[0] + p[1]) + p[2]) + p[3]
        _, vjp = jax.vjp(_silu, c)
        return vjp(tot)[0]

    return _blocked("cctx_grad", fn, (1,), [parts, c_ctx_row], [_res(parts.shape), _res(c_ctx_row.shape)],
                    [_sds(c_ctx_row.shape)], [_res(c_ctx_row.shape)])[0]


def _modulate(x, sh, sc):
    return x * (1.0 + sc) + sh


def _weight_operand(w):
    if isinstance(w, tuple):
        return w[0], w[1], tuple(b for b in w[1].block_shape if b is not None)
    return w, _res(w.shape), w.shape


def _matmul_cols(u, w):
    if w.ndim == 2:
        return jnp.dot(u, w, preferred_element_type=F32)
    return jnp.concatenate([jnp.dot(u, w[j], preferred_element_type=F32) for j in range(w.shape[0])], axis=1)


def _matmul_cols_t(dp, w):
    if w.ndim == 2:
        return _mm_nt(dp, w)
    nj = w.shape[2]
    return functools.reduce(lambda a, b: a + b, [_mm_nt(dp[:, j * nj:(j + 1) * nj], w[j]) for j in range(w.shape[0])])


def _rows_weight(w):
    return w if w.ndim == 2 else w.reshape(w.shape[0] * w.shape[1], w.shape[2])


def _inproj(name, x, sh, sc, w, tm):
    t, d = x.shape
    w, w_spec, w_blk = _weight_operand(w)
    n = w_blk[-1] * (w_blk[0] if len(w_blk) == 3 else 1)
    tm = min(tm, t)

    def fn(x, sh, sc, w):
        u = _modulate(x, sh, sc).astype(MXU_DTYPE)
        return _matmul_cols(u, w), u

    return _blocked(name, fn, (t // tm,), [x, sh, sc, w],
                    [_tok(tm, d), _res(sh.shape), _res(sc.shape), w_spec],
                    [_sds((t, n)), _sds((t, d), MXU_DTYPE)], [_tok(tm, n), _tok(tm, d)])


def _inproj_bwd(name, dp, x, sh, sc, w, add, tm):
    t, d = x.shape
    w, w_spec, w_blk = _weight_operand(w)
    n = w_blk[-1] * (w_blk[0] if len(w_blk) == 3 else 1)
    tm = min(tm, t)
    has_add = add is not None

    def fn(dp, x, sh, sc, w, *rest):
        du = _matmul_cols_t(dp, w)
        _, vjp = jax.vjp(_modulate, x, sh, sc)
        dx, dsh, dsc = vjp(du)
        if has_add:
            dx = dx + rest[0]
        return dx, dsh, dsc

    ins = [dp, x, sh, sc, w] + ([add] if has_add else [])
    specs = [_tok(tm, n), _tok(tm, d), _res(sh.shape), _res(sc.shape), w_spec] + ([_tok(tm, d)] if has_add else [])
    return _blocked(name, fn, (t // tm,), ins, specs,
                    [_sds((t, d)), _sds(sh.shape), _sds(sc.shape)], [_tok(tm, d), _res(sh.shape), _res(sc.shape)],
                    acc=(1, 2), acc_axis=0)


def _residual_ln(y, x, gt, lng, lnb):
    return _layernorm(ALPHA * x + gt * y, lng, lnb)


def _gdn_mix(o0, o1, pg, yp, normw):
    o = o0 + o1
    heads = []
    for h in range(GDN_HEADS):
        oh = o[:, h * HEAD_DIM:(h + 1) * HEAD_DIM]
        heads.append(oh * lax.rsqrt(jnp.mean(oh * oh, -1, keepdims=True) + RMS_EPS) * normw)
    on = jnp.concatenate(heads, axis=-1) * _silu(pg)
    return jnp.concatenate([on, yp], axis=-1)


def _even_out(o0, o1, p, ypool, x, normw, gt, lng, lnb, w, tm):
    t, d = x.shape
    tm = min(tm, t)
    gate_blk = 3

    def fn(o0, o1, pg, yp, x, normw, gt, lng, lnb, w):
        mix = _gdn_mix(o0, o1, pg, yp, normw).astype(MXU_DTYPE)
        y = jnp.dot(mix, w, preferred_element_type=F32)
        return _residual_ln(y, x, gt, lng, lnb), mix, y

    return _blocked("even_out", fn, (t // tm,), [o0, o1, p, ypool, x, normw, gt, lng, lnb, w],
                    [_tok(tm, 512), _tok(tm, 512), _tok(tm, 512, gate_blk), _tok(tm, 512), _tok(tm, d),
                     _res(normw.shape), _res(gt.shape), _res(lng.shape), _res(lnb.shape), _res(w.shape)],
                    [_sds((t, d)), _sds((t, d), MXU_DTYPE), _sds((t, d))], [_tok(tm, d)] * 3)


def _even_out_bwd(dx1, o0, o1, p, ypool, x, y, normw, gt, lng, lnb, w, tm):
    t, d = x.shape
    tm = min(tm, t)

    def fn(dx1, o0, o1, pg, yp, x, y, normw, gt, lng, lnb, w):
        _, vjp2 = jax.vjp(_residual_ln, y, x, gt, lng, lnb)
        dy, dx, dgt, dlng, dlnb = vjp2(dx1)
        dyb = dy.astype(MXU_DTYPE)
        dmix = _mm_nt(dyb, w)
        _, vjp1 = jax.vjp(_gdn_mix, o0, o1, pg, yp, normw)
        do, _, dpg, dyp, dnormw = vjp1(dmix)
        return do, dpg, dyp, dx, dyb, dnormw, dgt, dlng, dlnb

    return _blocked("even_out_bwd", fn, (t // tm,), [dx1, o0, o1, p, ypool, x, y, normw, gt, lng, lnb, w],
                    [_tok(tm, d), _tok(tm, 512), _tok(tm, 512), _tok(tm, 512, 3), _tok(tm, 512),
                     _tok(tm, d), _tok(tm, d), _res(normw.shape), _res(gt.shape), _res(lng.shape), _res(lnb.shape),
                     _res(w.shape)],
                    [_sds((t, 512)), _sds((t, 512), MXU_DTYPE), _sds((t, 512)), _sds((t, d)), _sds((t, d), MXU_DTYPE),
                     _sds(normw.shape), _sds(gt.shape), _sds(lng.shape), _sds(lnb.shape)],
                    [_tok(tm, 512), _tok(tm, 512), _tok(tm, 512), _tok(tm, d), _tok(tm, d),
                     _res(normw.shape), _res(gt.shape), _res(lng.shape), _res(lnb.shape)],
                    acc=(5, 6, 7, 8), acc_axis=0)


def _odd_mix(ysc, z0, cg, cb):
    z = _silu(_layernorm(z0, cg, cb))
    return jnp.concatenate([ysc, z], axis=-1)


def _odd_out(ysc, z0, x, cg, cb, gt, lng, lnb, w, tm):
    t, d = x.shape
    tm = min(tm, t)

    def fn(ysc, z0, x, cg, cb, gt, lng, lnb, w):
        mix = _odd_mix(ysc, z0, cg, cb).astype(MXU_DTYPE)
        y = jnp.dot(mix, w, preferred_element_type=F32)
        return _residual_ln(y, x, gt, lng, lnb), mix, y

    return _blocked("odd_out", fn, (t // tm,), [ysc, z0, x, cg, cb, gt, lng, lnb, w],
                    [_tok(tm, 512), _tok(tm, 512), _tok(tm, d), _res(cg.shape), _res(cb.shape), _res(gt.shape),
                     _res(lng.shape), _res(lnb.shape), _res(w.shape)],
                    [_sds((t, d)), _sds((t, d), MXU_DTYPE), _sds((t, d))], [_tok(tm, d)] * 3)


def _odd_out_bwd(dx3, ysc, z0, x, y, cg, cb, gt, lng, lnb, w, tm):
    t, d = x.shape
    tm = min(tm, t)

    def fn(dx3, ysc, z0, x, y, cg, cb, gt, lng, lnb, w):
        _, vjp2 = jax.vjp(_residual_ln, y, x, gt, lng, lnb)
        dy, dx, dgt, dlng, dlnb = vjp2(dx3)
        dyb = dy.astype(MXU_DTYPE)
        dmix = _mm_nt(dyb, w)
        _, vjp1 = jax.vjp(_odd_mix, ysc, z0, cg, cb)
        dysc, dz0, dcg, dcb = vjp1(dmix)
        return dysc, dz0, dx, dyb, dcg, dcb, dgt, dlng, dlnb

    return _blocked("odd_out_bwd", fn, (t // tm,), [dx3, ysc, z0, x, y, cg, cb, gt, lng, lnb, w],
                    [_tok(tm, d), _tok(tm, 512), _tok(tm, 512), _tok(tm, d), _tok(tm, d), _res(cg.shape), _res(cb.shape),
                     _res(gt.shape), _res(lng.shape), _res(lnb.shape), _res(w.shape)],
                    [_sds((t, 512)), _sds((t, 512)), _sds((t, d)), _sds((t, d), MXU_DTYPE),
                     _sds(cg.shape), _sds(cb.shape), _sds(gt.shape), _sds(lng.shape), _sds(lnb.shape)],
                    [_tok(tm, 512), _tok(tm, 512), _tok(tm, d), _tok(tm, d),
                     _res(cg.shape), _res(cb.shape), _res(gt.shape), _res(lng.shape), _res(lnb.shape)],
                    acc=(4, 5, 6, 7, 8), acc_axis=0)


def _ffn_down(name, s, x, gt, lng, lnb, w, tm):
    t, d = x.shape
    f = s.shape[1]
    w, w_spec, _ = _weight_operand(w)
    tm = min(tm, t)

    def fn(s, x, gt, lng, lnb, w):
        y = jnp.dot(s, _rows_weight(w), preferred_element_type=F32)
        return _residual_ln(y, x, gt, lng, lnb), y

    return _blocked(name, fn, (t // tm,), [s, x, gt, lng, lnb, w],
                    [_tok(tm, f), _tok(tm, d), _res(gt.shape), _res(lng.shape), _res(lnb.shape), w_spec],
                    [_sds((t, d)), _sds((t, d))], [_tok(tm, d)] * 2)


def _ffn_down_bwd(name, dx2, x, y, gt, lng, lnb, w, tm):
    t, d = x.shape
    w, w_spec, w_blk = _weight_operand(w)
    f = math.prod(w_blk[:-1])
    tm = min(tm, t)

    def fn(dx2, x, y, gt, lng, lnb, w):
        _, vjp2 = jax.vjp(_residual_ln, y, x, gt, lng, lnb)
        dy, dx, dgt, dlng, dlnb = vjp2(dx2)
        dyb = dy.astype(MXU_DTYPE)
        return _mm_nt(dyb, _rows_weight(w)), dx, dyb, dgt, dlng, dlnb

    return _blocked(name, fn, (t // tm,), [dx2, x, y, gt, lng, lnb, w],
                    [_tok(tm, d), _tok(tm, d), _tok(tm, d), _res(gt.shape), _res(lng.shape), _res(lnb.shape), w_spec],
                    [_sds((t, f), MXU_DTYPE), _sds((t, d)), _sds((t, d), MXU_DTYPE), _sds(gt.shape), _sds(lng.shape), _sds(lnb.shape)],
                    [_tok(tm, f), _tok(tm, d), _tok(tm, d), _res(gt.shape), _res(lng.shape), _res(lnb.shape)],
                    acc=(3, 4, 5), acc_axis=0)


def _loss_and_grad(x4, target, tm):
    t, d = x4.shape
    tm = min(tm, t)

    def fn(y, tg):
        e = y - tg
        part = 0.5 * jnp.sum(jnp.mean(e * e, axis=-1, keepdims=True), axis=0, keepdims=True)
        return jnp.broadcast_to(part, (1, LANES)), e * (1.0 / d)

    return _blocked("loss", fn, (t // tm,), [x4, target], [_tok(tm, d), _tok(tm, d)],
                    [_sds((1, LANES)), _sds((t, d))], [_res((1, LANES)), _tok(tm, d)], acc=(0,), acc_axis=0)


def _matmul_tn(name, pairs, chip_cols=None):
    k = pairs[0][0].shape[1]
    n = pairs[0][1].shape[1]
    tk = k if k <= 1024 else k // 2
    if chip_cols is None:
        tn = 512 if n % 512 == 0 else (384 if n % 384 == 0 else 128)
        out_spec, out_shape = pl.BlockSpec((tk, tn), lambda i, j: (i, j)), _sds((k, n), MXU_DTYPE)
    else:
        tn = chip_cols
        tk = min(tk, 512) if tn > 1024 else tk
        out_spec = pl.BlockSpec((None, tk, tn), lambda i, j: (j, i, 0))
        out_shape = _sds((n // chip_cols, k, chip_cols), MXU_DTYPE)

    def body(*refs):
        acc = None
        for p in range(len(pairs)):
            term = lax.dot_general(refs[2 * p][...], refs[2 * p + 1][...], (((0,), (0,)), ((), ())), preferred_element_type=F32)
            acc = term if acc is None else acc + term
        refs[-1][...] = acc.astype(refs[-1].dtype)

    in_specs, args = [], []
    for a, b in pairs:
        t = a.shape[0]
        in_specs += [pl.BlockSpec((t, tk), lambda i, j: (0, i)), pl.BlockSpec((t, tn), lambda i, j: (0, j))]
        args += [a, b]
    return pl.pallas_call(
        body, name=name, grid=(k // tk, n // tn), in_specs=in_specs,
        out_specs=out_spec, out_shape=out_shape,
        compiler_params=pltpu.CompilerParams(dimension_semantics=("arbitrary", "arbitrary"),
                                             vmem_limit_bytes=VMEM_LIMIT_BYTES),
    )(*args)


def _chan(t, col_of):
    return pl.BlockSpec((t, LANES), lambda i: (0, col_of(i)))


def _wblk(k, col_of=lambda i: i):
    return pl.BlockSpec((k, LANES), lambda i: (0, col_of(i)))


SEQ_TILE = 256
GRID_SEQ_TILE = 512


def _load_ext(ref, r0, rows, halo, total):
    lo, hi = max(r0 - halo, 0), min(r0 + rows + halo, total)
    parts = []
    if lo > r0 - halo:
        parts.append(jnp.zeros((lo - (r0 - halo), ref.shape[1]), F32))
    parts.append(ref[lo:hi, :].astype(F32))
    if hi < r0 + rows + halo:
        parts.append(jnp.zeros((r0 + rows + halo - hi, ref.shape[1]), F32))
    return parts[0] if len(parts) == 1 else jnp.concatenate(parts, axis=0)


def _seq_stage(name, tile_fn, halo, grid_n, seqs, pars, outs, tile_rows=SEQ_TILE):
    total = seqs[0][0].shape[0]
    rows = min(tile_rows, total)
    ns, npar = len(seqs), len(pars)

    def body(*refs):
        seq_refs, par_refs, out_refs = refs[:ns], refs[ns:ns + npar], refs[ns + npar:]
        par_vals = [r[...] for r in par_refs]
        for r0 in range(0, total, rows):
            exts = [_load_ext(r, r0, rows, halo, total) for r in seq_refs]
            res = tile_fn(r0, *exts, *par_vals)
            for o_ref, v in zip(out_refs, res):
                o_ref[r0:r0 + rows, :] = v.astype(o_ref.dtype)

    return pl.pallas_call(
        body, name=name, grid=(grid_n,), in_specs=[s for _, s in seqs] + [s for _, s in pars],
        out_specs=[s for _, s in outs], out_shape=[o for o, _ in outs],
        compiler_params=pltpu.CompilerParams(dimension_semantics=("arbitrary",), vmem_limit_bytes=VMEM_LIMIT_BYTES),
    )(*[a for a, _ in seqs], *[a for a, _ in pars])


def _seq_stage_bwd(name, tile_fn, halo, grid_n, douts, seqs, pars, dseq_outs, dpar_outs, tile_rows=SEQ_TILE):
    total = seqs[0][0].shape[0]
    rows = min(tile_rows, total)
    groups = [d if isinstance(d, list) else [d] for d in douts]
    douts = [d for g in groups for d in g]
    starts = [sum(len(g) for g in groups[:k]) for k in range(len(groups))]
    nd, ns, npar = len(douts), len(seqs), len(pars)
    widths = [s.block_shape[-1] for _, s in seqs]

    def dtile(dout_refs, k, r0):
        terms = [d[r0:r0 + rows, :].astype(F32) for d in dout_refs[starts[k]:starts[k] + len(groups[k])]]
        return functools.reduce(lambda a, b: a + b, terms)

    def body(*refs):
        dout_refs, seq_refs, par_refs = refs[:nd], refs[nd:nd + ns], refs[nd + ns:nd + ns + npar]
        dseq_refs = refs[nd + ns + npar:nd + 2 * ns + npar]
        dpar_refs = refs[nd + 2 * ns + npar:nd + 2 * ns + 2 * npar]
        accs = refs[nd + 2 * ns + 2 * npar:]
        par_vals = [r[...] for r in par_refs]
        for a in accs:
            for r0 in range(0, total, rows):
                a[r0:r0 + rows, :] = jnp.zeros((rows, a.shape[1]), F32)
        dpars = [jnp.zeros(p.shape, F32) for p in par_vals]
        for r0 in range(0, total, rows):
            exts = [_load_ext(r, r0, rows, halo, total) for r in seq_refs]
            _, vjp = jax.vjp(functools.partial(tile_fn, r0), *exts, *par_vals)
            cts = vjp(tuple(dtile(dout_refs, k, r0) for k in range(len(groups))))
            lo, hi = max(r0 - halo, 0), min(r0 + rows + halo, total)
            for a, dx in zip(accs, cts[:ns]):
                a[lo:hi, :] += dx[lo - (r0 - halo):hi - (r0 - halo)]
            dpars = [acc + g for acc, g in zip(dpars, cts[ns:])]
        for o_ref, a in zip(dseq_refs, accs):
            for r0 in range(0, total, rows):
                o_ref[r0:r0 + rows, :] = a[r0:r0 + rows, :].astype(o_ref.dtype)
        for o_ref, g in zip(dpar_refs, dpars):
            o_ref[...] = g

    return pl.pallas_call(
        body, name=name, grid=(grid_n,),
        in_specs=[s for _, s in douts] + [s for _, s in seqs] + [s for _, s in pars],
        out_specs=[s for _, s in dseq_outs] + [s for _, s in dpar_outs],
        out_shape=[o for o, _ in dseq_outs] + [o for o, _ in dpar_outs],
        scratch_shapes=[pltpu.VMEM((total, w), F32) for w in widths],
        compiler_params=pltpu.CompilerParams(dimension_semantics=("arbitrary",), vmem_limit_bytes=VMEM_LIMIT_BYTES),
    )(*[a for a, _ in douts], *[a for a, _ in seqs], *[a for a, _ in pars])


def _qkv_tile(r0, p_ext, w):
    kind = pl.program_id(0)
    a = _silu(_conv5(p_ext, w))
    nrm = a * lax.rsqrt(jnp.sum(a * a, -1, keepdims=True) + RMS_EPS)
    scale = jnp.where(kind < GDN_HEADS, HEAD_DIM ** -0.5, 1.0).astype(F32)
    sel = (kind < 2 * GDN_HEADS).astype(F32)
    return (sel * (nrm * scale) + (1.0 - sel) * a,)


def _qkv_conv(name, p, w):
    t = p.shape[0]
    nb = w.shape[1] // LANES
    ident = lambda i: i
    return _seq_stage(name, _qkv_tile, HALO_SHORT, nb, [(p, _chan(t, ident))], [(w, _wblk(GDN_CONV))],
                      [(_sds((t, nb * LANES)), _chan(t, ident))])[0]


def _qkv_conv_bwd(name, dqkv0, dqkv1, p, w):
    t = p.shape[0]
    nb = w.shape[1] // LANES
    ident = lambda i: i
    return _seq_stage_bwd(name, _qkv_tile, HALO_SHORT, nb, [[(dqkv0, _chan(t, ident)), (dqkv1, _chan(t, ident))]],
                          [(p, _chan(t, ident))],
                          [(w, _wblk(GDN_CONV))], [(_sds((t, nb * LANES), MXU_DTYPE), _chan(t, ident))],
                          [(_sds(w.shape), _wblk(GDN_CONV))])


def _gates_fn(s, avec, dtvec):
    lane = lax.broadcasted_iota(jnp.int32, s.shape, 1)
    beta = jax.nn.sigmoid(s)
    g = -jnp.exp(avec) * jax.nn.softplus(s + dtvec)
    nh = 2 * GDN_HEADS
    return jnp.where(lane < nh, beta, jnp.where(lane < 2 * nh, g, 0.0))


def _gates(name, p, avec, dtvec, col):
    t = p.shape[0]
    tm = min(512, t)
    return _blocked(name, _gates_fn, (t // tm,), [p, avec, dtvec], [_tok(tm, LANES, col), _res(avec.shape), _res(dtvec.shape)],
                    [_sds((t, LANES))], [_tok(tm, LANES)])[0]


def _gates_bwd(name, dgb, p, avec, dtvec, col):
    t = p.shape[0]
    tm = min(512, t)

    def fn(dgb, s, avec, dtvec):
        _, vjp = jax.vjp(_gates_fn, s, avec, dtvec)
        return vjp(dgb)

    return _blocked(name, fn, (t // tm,), [dgb, p, avec, dtvec],
                    [_tok(tm, LANES), _tok(tm, LANES, col), _res(avec.shape), _res(dtvec.shape)],
                    [_sds((t, LANES), MXU_DTYPE), _sds(avec.shape), _sds(dtvec.shape)],
                    [_tok(tm, LANES), _res(avec.shape), _res(dtvec.shape)], acc=(1, 2), acc_axis=0)


def _make_pool_tile(total):
    def tile(r0, x_ext, scale, *ws):
        ys = []
        for g, win in enumerate(POOL_WINDOWS):
            xg = x_ext[:, g * LANES:(g + 1) * LANES]
            rows = xg.shape[0] - 2 * HALO_SHORT
            pos = r0 + lax.broadcasted_iota(jnp.int32, (rows, LANES), 0)
            lo = jnp.clip(pos - win // 2, 0, total)
            hi = jnp.clip(pos - win // 2 + win, 0, total)
            window_sum = _pool_sums[win](xg, jnp.ones((win, LANES), F32))
            pooled = window_sum / (hi - lo).astype(F32) - _center(xg, HALO_SHORT)
            ys.append(_mm(pooled, ws[g]))
        return (jnp.concatenate(ys, axis=-1) * scale,)

    return tile


def _pool_specs(p, pool_w, pool_scale, col0):
    t = p.shape[0]
    n = len(POOL_WINDOWS) * LANES
    seq = (p, pl.BlockSpec((t, n), lambda i: (0, col0)))
    pars = [(pool_scale, _res(pool_scale.shape))]
    pars += [(pool_w, pl.BlockSpec((None, LANES, LANES), lambda i, g=g: (g, 0, 0))) for g in range(len(POOL_WINDOWS))]
    return t, n, seq, pars


def _pool(p, pool_w, pool_scale, col0):
    t, n, seq, pars = _pool_specs(p, pool_w, pool_scale, col0)
    return _seq_stage("pool", _make_pool_tile(t), HALO_SHORT, 1, [seq], pars, [(_sds((t, n)), _res((t, n)))])[0]


def _pool_bwd(dy, p, pool_w, pool_scale, col0):
    t, n, seq, pars = _pool_specs(p, pool_w, pool_scale, col0)
    wspec = (_sds((LANES, LANES)), _res((LANES, LANES)))
    return _seq_stage_bwd("pool_bwd", _make_pool_tile(t), HALO_SHORT, 1, [(dy, _res(dy.shape))], [seq], pars,
                          [(_sds((t, n), MXU_DTYPE), _res((t, n)))],
                          [(_sds(pool_scale.shape), _res(pool_scale.shape))] + [wspec] * len(POOL_WINDOWS))


def _ffn_conv_tile(r0, a_ext, gt_ext, w):
    return (_silu(_conv3x3(a_ext, w)) * _center(gt_ext, HALO_GRID),)


def _ffn_conv(name, h, w9):
    t = h.shape[0]
    nb = w9.shape[1] // LANES
    ident = lambda i: i
    return _seq_stage(name, _ffn_conv_tile, HALO_GRID, nb, [(h, _chan(t, ident)), (h, _chan(t, lambda i: nb + i))],
                      [(w9, _wblk(9))], [(_sds((t, nb * LANES), MXU_DTYPE), _chan(t, ident))], GRID_SEQ_TILE)[0]


def _ffn_conv_bwd(name, ds, h, w9):
    t = h.shape[0]
    nb = w9.shape[1] // LANES
    ident = lambda i: i
    o = (_sds((t, nb * LANES), MXU_DTYPE), _chan(t, ident))
    return _seq_stage_bwd(name, _ffn_conv_tile, HALO_GRID, nb, [(ds, _chan(t, ident))],
                          [(h, _chan(t, ident)), (h, _chan(t, lambda i: nb + i))], [(w9, _wblk(9))],
                          [o, o], [(_sds(w9.shape), _wblk(9))], GRID_SEQ_TILE)


def _sconv_tile(r0, gb_ext, gc_ext, h_ext, w):
    return (_center(gb_ext, HALO_SHORT) * _conv3(gc_ext * h_ext, w),)


def _sconv_specs(p1, w3):
    t = p1.shape[0]
    nb = w3.shape[1] // LANES
    seqs = [(p1, _chan(t, lambda i: i)), (p1, _chan(t, lambda i: nb + i)), (p1, _chan(t, lambda i: 2 * nb + i))]
    return t, nb, seqs, [(w3, _wblk(SC_WIDTH))]


def _sconv(p1, w3):
    t, nb, seqs, pars = _sconv_specs(p1, w3)
    return _seq_stage("sconv", _sconv_tile, HALO_SHORT, nb, seqs, pars, [(_sds((t, nb * LANES)), _chan(t, lambda i: i))])[0]


def _sconv_bwd(dysc, p1, w3):
    t, nb, seqs, pars = _sconv_specs(p1, w3)
    o = (_sds((t, nb * LANES), MXU_DTYPE), _chan(t, lambda i: i))
    return _seq_stage_bwd("sconv_bwd", _sconv_tile, HALO_SHORT, nb, [(dysc, _chan(t, lambda i: i))], seqs, pars,
                          [o, o, o], [(_sds(w3.shape), _wblk(SC_WIDTH))])


def _conf_tile(r0, ga_ext, gb_ext, w):
    return (_conv31(ga_ext * jax.nn.sigmoid(gb_ext), w),)


def _conf_specs(p1, w31, blk0):
    t = p1.shape[0]
    nb = w31.shape[1] // LANES
    seqs = [(p1, _chan(t, lambda i: blk0 + i)), (p1, _chan(t, lambda i: blk0 + nb + i))]
    return t, nb, seqs, [(w31, _wblk(CF_WIDTH))]


def _confconv(p1, w31, blk0):
    t, nb, seqs, pars = _conf_specs(p1, w31, blk0)
    return _seq_stage("confconv", _conf_tile, HALO_CONF, nb, seqs, pars, [(_sds((t, nb * LANES)), _chan(t, lambda i: i))])[0]


def _confconv_bwd(dz0, p1, w31, blk0):
    t, nb, seqs, pars = _conf_specs(p1, w31, blk0)
    o = (_sds((t, nb * LANES), MXU_DTYPE), _chan(t, lambda i: i))
    return _seq_stage_bwd("confconv_bwd", _conf_tile, HALO_CONF, nb, [(dz0, _chan(t, lambda i: i))], seqs, pars,
                          [o, o], [(_sds(w31.shape), _wblk(CF_WIDTH))])


def _split_bf16(x):
    head = x.astype(jnp.bfloat16)
    return head, (x - head.astype(F32)).astype(jnp.bfloat16)


def _bdot(a, b, ca, cb, hi):
    dims = (((ca,), (cb,)), ((0,), (0,)))
    if not hi:
        return lax.dot_general(a.astype(MXU_DTYPE), b.astype(MXU_DTYPE), dims, preferred_element_type=F32)
    if MXU_DTYPE == F32:
        return lax.dot_general(a, b, dims, preferred_element_type=F32, precision=HIGHEST)
    (ah, al), (bh, bl) = _split_bf16(a), _split_bf16(b)
    dot = functools.partial(lax.dot_general, dimension_numbers=dims, preferred_element_type=F32)
    return dot(ah, bh) + (dot(ah, bl) + dot(al, bh))


def _make_bmm(hi):
    nn_i = lambda a, b: _bdot(a, b, 2, 1, hi)
    nt_i = lambda a, b: _bdot(a, b, 2, 2, hi)
    tn_i = lambda a, b: _bdot(a, b, 1, 1, hi)
    nn = _mm_vjp(nn_i, lambda g, b: nt_i(g, b), lambda a, g: tn_i(a, g))
    nt = _mm_vjp(nt_i, lambda g, b: nn_i(g, b), lambda a, g: tn_i(g, a))
    tn = _mm_vjp(tn_i, lambda g, b: nt_i(b, g), lambda a, g: nn_i(a, g))
    return nn, nt, tn


_bmm, _bmm_nt, _bmm_tn = _make_bmm(False)
_bmm_hi, _bmm_hi_nt, _bmm_hi_tn = _make_bmm(True)


def _unit_tri_inverse(a):
    c = a.shape[-1]
    eye = (lax.broadcasted_iota(jnp.int32, a.shape, 1) == lax.broadcasted_iota(jnp.int32, a.shape, 2)).astype(F32)
    levels = max(1, int(math.ceil(math.log2(c))) - 1)
    p = eye - a
    m = _bdot(a, a, 2, 1, True)
    for level in range(levels):
        p_next = p + _bdot(p, m, 2, 1, True)
        if level + 1 < levels:
            m = _bdot(m, m, 2, 1, True)
        p = p_next
    return p


@jax.custom_vjp
def _known_inverse(a, tinv):
    return tinv


def _known_inverse_fwd(a, tinv):
    return tinv, tinv


def _known_inverse_bwd(tinv, dt):
    da = -_bdot(_bdot(tinv, dt, 1, 1, True), tinv, 2, 2, True)
    return da, jnp.zeros_like(tinv)


_known_inverse.defvjp(_known_inverse_fwd, _known_inverse_bwd)


def _gdn_chunk(q, k, v, gcol, grow, bcol, s, tinv=None):
    n, c, _ = q.shape
    shape = (n, c, c)
    fwd_dir = lax.broadcasted_iota(jnp.int32, shape, 0) < n // 2
    i = lax.broadcasted_iota(jnp.int32, shape, 1)
    j = lax.broadcasted_iota(jnp.int32, shape, 2)
    order = jnp.where(fwd_dir, i - j, j - i)
    incl = order >= 0
    gc_col = jnp.sum(jnp.where(incl, grow, 0.0), axis=2, keepdims=True)
    gc_row = jnp.sum(jnp.where(order <= 0, gcol, 0.0), axis=1, keepdims=True)
    gtot = jnp.sum(grow, axis=2, keepdims=True)
    decay = jnp.exp(jnp.where(incl, gc_col - gc_row, -1e30))
    kb = k * bcol
    a = jnp.where(order > 0, _bmm_nt(kb, k) * decay, 0.0)
    tinv = _unit_tri_inverse(a) if tinv is None else _known_inverse(a, tinv)
    e_col = jnp.exp(gc_col)
    u = _bmm_hi(tinv, v * bcol)
    w = _bmm_hi(tinv, kb * e_col)
    k_dec = k * jnp.exp(gtot - gc_col)
    v_new = u - _bmm(w, s)
    attn = jnp.where(incl, _bmm_nt(q, k) * decay, 0.0)
    o = _bmm(q * e_col, s) + _bmm(attn, v_new)
    s_next = s * jnp.exp(gtot) + _bmm_tn(k_dec, v_new)
    return o, s_next, tinv


N_CHAINS = 2 * GDN_HEADS
QKV_W = 3 * GDN_HEADS * HEAD_DIM


def _scan_specs(nc, step):
    c = GDN_CHUNK
    seq = [pl.BlockSpec((c, QKV_W), lambda i: (step(i), 0)), pl.BlockSpec((c, QKV_W), lambda i: (nc - 1 - step(i), 0))]
    heads = [pl.BlockSpec((c, GDN_HEADS * HEAD_DIM), lambda i: (step(i), 0)),
             pl.BlockSpec((c, GDN_HEADS * HEAD_DIM), lambda i: (nc - 1 - step(i), 0))]
    colv = pl.BlockSpec((N_CHAINS, None, c, 1), lambda i: (0, step(i), 0, 0))
    rowv = pl.BlockSpec((N_CHAINS, None, 1, c), lambda i: (0, step(i), 0, 0))
    state = pl.BlockSpec((N_CHAINS, HEAD_DIM, HEAD_DIM), lambda i: (0, 0, 0))
    saved = pl.BlockSpec((N_CHAINS, None, HEAD_DIM, HEAD_DIM), lambda i: (0, step(i), 0, 0))
    inv = pl.BlockSpec((N_CHAINS, None, c, c), lambda i: (0, step(i), 0, 0))
    return seq, heads, colv, rowv, state, saved, inv


def _head_cols(h, part):
    lo = (part * GDN_HEADS + h) * HEAD_DIM
    return slice(lo, lo + HEAD_DIM)


def _chain_stack(x0_ref, x1_ref, part):
    return jnp.stack([(x0_ref if n < GDN_HEADS else x1_ref)[:, _head_cols(n % GDN_HEADS, part)] for n in range(N_CHAINS)])


def _scan_fwd(name, qkv, bcol, gcol, grow, s0):
    t = qkv.shape[0]
    nc = t // GDN_CHUNK
    seq, heads, colv, rowv, state, saved, inv = _scan_specs(nc, lambda i: i)

    def body(qkv0_ref, qkv1_ref, b_ref, gc_ref, gr_ref, s0_ref, o0_ref, o1_ref, save_ref, tinv_ref, fin_ref, s_ref):
        @pl.when(pl.program_id(0) == 0)
        def _():
            s_ref[...] = s0_ref[...]

        s = s_ref[...]
        save_ref[...] = s
        q, k, v = [_chain_stack(qkv0_ref, qkv1_ref, part) for part in range(3)]
        o, s_next, tinv = _gdn_chunk(q, k, v, gc_ref[...], gr_ref[...], b_ref[...], s)
        for n in range(N_CHAINS):
            d, h = divmod(n, GDN_HEADS)
            (o0_ref if d == 0 else o1_ref)[:, _head_cols(h, 0)] = o[n]
        tinv_ref[...] = tinv
        s_ref[...] = s_next
        fin_ref[...] = s_next

    hw = GDN_HEADS * HEAD_DIM
    return pl.pallas_call(
        body, name=name, grid=(nc,), in_specs=seq + [colv, colv, rowv, state],
        out_specs=heads + [saved, inv, state],
        out_shape=[_sds((t, hw)), _sds((t, hw)), _sds((N_CHAINS, nc, HEAD_DIM, HEAD_DIM)),
                   _sds((N_CHAINS, nc, GDN_CHUNK, GDN_CHUNK)), _sds((N_CHAINS, HEAD_DIM, HEAD_DIM))],
        scratch_shapes=[pltpu.VMEM((N_CHAINS, HEAD_DIM, HEAD_DIM), F32)],
        compiler_params=pltpu.CompilerParams(dimension_semantics=("arbitrary",), vmem_limit_bytes=VMEM_LIMIT_BYTES),
    )(qkv, qkv, bcol, gcol, grow, s0)


def _scan_bwd(name, do, qkv, bcol, gcol, grow, saved_s, saved_inv, ds_last):
    t = qkv.shape[0]
    nc = t // GDN_CHUNK
    c = GDN_CHUNK
    seq, heads, colv, rowv, state, saved, inv = _scan_specs(nc, lambda i: nc - 1 - i)

    def body(do0_ref, do1_ref, qkv0_ref, qkv1_ref, b_ref, gc_ref, gr_ref, s_ref, tinv_ref, dsl_ref,
             dx0_ref, dx1_ref, db_ref, dgc_ref, dgr_ref, ds0_ref, ds_ref):
        @pl.when(pl.program_id(0) == 0)
        def _():
            ds_ref[...] = dsl_ref[...]

        tinv = tinv_ref[...]
        q, k, v = [_chain_stack(qkv0_ref, qkv1_ref, part) for part in range(3)]
        do = jnp.stack([(do0_ref if n < GDN_HEADS else do1_ref)[:, _head_cols(n % GDN_HEADS, 0)] for n in range(N_CHAINS)])
        _, vjp = jax.vjp(lambda *a: _gdn_chunk(*a, tinv=tinv)[:2], q, k, v, gc_ref[...], gr_ref[...], b_ref[...], s_ref[...])
        dq, dk, dv, dgc, dgr, db, ds = vjp((do, ds_ref[...]))
        for n in range(N_CHAINS):
            d, h = divmod(n, GDN_HEADS)
            dx_ref = dx0_ref if d == 0 else dx1_ref
            dx_ref[:, _head_cols(h, 0)] = dq[n]
            dx_ref[:, _head_cols(h, 1)] = dk[n]
            dx_ref[:, _head_cols(h, 2)] = dv[n]
        db_ref[...] = db
        dgc_ref[...] = dgc
        dgr_ref[...] = dgr
        ds_ref[...] = ds
        ds0_ref[...] = ds

    vec_c = _sds((N_CHAINS, nc, c, 1))
    vec_r = _sds((N_CHAINS, nc, 1, c))
    return pl.pallas_call(
        body, name=name, grid=(nc,),
        in_specs=heads + seq + [colv, colv, rowv, saved, inv, state],
        out_specs=seq + [colv, colv, rowv, state],
        out_shape=[_sds((t, QKV_W)), _sds((t, QKV_W)), vec_c, vec_c, vec_r, _sds((N_CHAINS, HEAD_DIM, HEAD_DIM))],
        scratch_shapes=[pltpu.VMEM((N_CHAINS, HEAD_DIM, HEAD_DIM), F32)],
        compiler_params=pltpu.CompilerParams(dimension_semantics=("arbitrary",), vmem_limit_bytes=VMEM_LIMIT_BYTES),
    )(do, do, qkv, qkv, bcol, gcol, grow, saved_s, saved_inv, ds_last)


def _gate_layouts(gb):
    t = gb.shape[0]
    nc = t // GDN_CHUNK
    nh = GDN_HEADS

    def by_scan_position(a):
        a = a.T.reshape(2 * nh, nc, GDN_CHUNK)
        return jnp.concatenate([a[:nh], a[nh:, ::-1]], axis=0)

    beta = by_scan_position(gb[:, :2 * nh])
    g = by_scan_position(gb[:, 2 * nh:4 * nh])
    return beta[..., None], g[..., None], g[:, :, None, :]


def _gate_layouts_bwd(dbcol, dgcol, dgrow):
    n2, nc, c, _ = dbcol.shape
    nh = n2 // 2
    t = nc * c

    def by_token(a):
        return jnp.concatenate([a[:nh], a[nh:, ::-1]], axis=0).reshape(n2, t).T

    dbeta = by_token(dbcol[..., 0])
    dg = by_token(dgcol[..., 0] + dgrow[:, :, 0, :])
    return jnp.concatenate([dbeta, dg, jnp.zeros((t, LANES - 2 * n2), F32)], axis=1)


def kernel(x, c, ctx, c_ctx, ada_w, ada_b, ln_g, ln_b, even_w_in, even_w_out, gdn_conv_w, gdn_a_log, gdn_dt_bias, gdn_norm_w, pool_w, pool_scale, odd_w_in, odd_w_out, sconv_w, conf_conv_w, conf_ln_g, conf_ln_b, ffn_w_up, ffn_conv_w, ffn_w_down, loss_target, m_c_ctx, m_ada_w, m_ada_b, m_ln_g, m_ln_b, m_even_w_in, m_even_w_out, m_gdn_conv_w, m_gdn_a_log, m_gdn_dt_bias, m_gdn_norm_w, m_pool_w, m_pool_scale, m_odd_w_in, m_odd_w_out, m_sconv_w, m_conf_conv_w, m_conf_ln_g, m_conf_ln_b, m_ffn_w_up, m_ffn_conv_w, m_ffn_w_down, v_c_ctx, v_ada_w, v_ada_b, v_ln_g, v_ln_b, v_even_w_in, v_even_w_out, v_gdn_conv_w, v_gdn_a_log, v_gdn_dt_bias, v_gdn_norm_w, v_pool_w, v_pool_scale, v_odd_w_in, v_odd_w_out, v_sconv_w, v_conf_conv_w, v_conf_ln_g, v_conf_ln_b, v_ffn_w_up, v_ffn_conv_w, v_ffn_w_down):
    names = ['c_ctx', 'ada_w', 'ada_b', 'ln_g', 'ln_b', 'even_w_in', 'even_w_out', 'gdn_conv_w', 'gdn_a_log',
             'gdn_dt_bias', 'gdn_norm_w', 'pool_w', 'pool_scale', 'odd_w_in', 'odd_w_out', 'sconv_w', 'conf_conv_w',
             'conf_ln_g', 'conf_ln_b', 'ffn_w_up', 'ffn_conv_w', 'ffn_w_down']
    loc = locals()
    wts = {n: loc[n] for n in names}
    mom = {n: loc['m_' + n] for n in names}
    var = {n: loc['v_' + n] for n in names}

    ix, iy, ic = lax.axis_index("x"), lax.axis_index("y"), lax.axis_index("c")
    chip = 2 * ix + iy
    dev = 2 * chip + ic
    d = D_MODEL
    x0, ctx0, tgt = x[0], ctx[0], loss_target[0]
    t, tc = x0.shape[0], ctx0.shape[0]
    depth = ada_w.shape[0]
    n_ada = ada_w.shape[2]

    small_sharded = [gdn_conv_w, sconv_w, conf_conv_w, ffn_conv_w, ln_g, ln_b]
    s1_items = [c] + small_sharded
    s1 = _pack_rows(s1_items)
    g1 = _all_gather8("gather_small_in", s1, True).reshape(8, -1, LANES)
    c_all = g1[:, :_packed_rows(c.shape)].reshape(8, d)
    per_chip = [_unpack_rows(g1[2 * k], [a.shape for a in s1_items])[1:] for k in range(4)]
    gdn_conv_f, sconv_f, conf_conv_f, ffn_conv_f, ln_g_f, ln_b_f = [
        jnp.concatenate([per_chip[k][i] for k in range(4)], axis=-1) for i in range(len(small_sharded))]
    c16 = jnp.concatenate([c_all, c_ctx[None], jnp.zeros((7, d), F32)], axis=0)

    mod_part = _mod_rows(c16, ada_w)
    g2 = _all_gather8("gather_mod", mod_part.reshape(-1, LANES), True).reshape(4, 2, depth, 16, n_ada)[:, 0]
    mod_all = jnp.transpose(g2, (1, 2, 0, 3)).reshape(depth, 16, 4 * n_ada) + ada_b[:, None, :]
    mod_me = lax.dynamic_index_in_dim(mod_all, dev, axis=1, keepdims=False).reshape(depth, 6, 1, d)
    sh_c, sc_c = mod_all[0, 8, :d][None], mod_all[0, 8, d:2 * d][None]

    flat_names = ['even_w_in', 'even_w_out']
    flat_shapes = [wts[n].shape for n in flat_names]
    half_mult = 2 * 16 * LANES
    rh = -(-sum(math.prod(s) for s in flat_shapes) // half_mult) * half_mult // (2 * LANES)
    flat_pack = _pack_flat([wts[n] for n in flat_names], half_mult, MXU_DTYPE).reshape(2, rh, LANES)
    wg_flat = _all_gather8("gather_weights_even", lax.dynamic_index_in_dim(flat_pack, ic, axis=0, keepdims=False), True).reshape(4, -1)
    per_chip = [_unpack_flat(wg_flat[k], flat_shapes) for k in range(4)]
    n_even = 4 * even_w_in.shape[1]
    n_even_pad = -(-n_even // LANES) * LANES
    w_in = jnp.concatenate([per_chip[k][0] for k in range(4)] + [jnp.zeros((d, n_even_pad - n_even), MXU_DTYPE)], axis=1)
    w_out = jnp.concatenate([per_chip[k][1] for k in range(4)], axis=0)

    def row_half(a, core, axis):
        n = a.shape[axis] // 2
        return lax.dynamic_slice_in_dim(a, core * n, n, axis=axis)

    def layer_of(a, core):
        return lax.dynamic_index_in_dim(a, core, axis=0, keepdims=False)

    mine = [row_half(odd_w_in, ic, 0), row_half(odd_w_out, ic, 0), layer_of(ffn_w_up, ic), layer_of(ffn_w_down, ic)]
    late_sems, late_mine, late_lands, late_token = _chip_send_start(
        "gather_late_start", [m.astype(MXU_DTYPE) for m in mine], [wg_flat, mod_all])
    scal_blk = (n_even // LANES)
    n_scal = n_even - scal_blk * LANES

    def mod(layer, k):
        return mod_me[layer, k]

    avec = jnp.zeros((1, LANES), F32).at[0, n_scal // 2:n_scal].set(gdn_a_log.reshape(-1))
    dtvec = jnp.zeros((1, LANES), F32).at[0, n_scal // 2:n_scal].set(gdn_dt_bias.reshape(-1))
    normw = gdn_norm_w[None]
    pscale = pool_scale[None]
    cg, cb = conf_ln_g[None], conf_ln_b[None]
    lng = lambda l, k: ln_g_f[l, k][None]
    lnb = lambda l, k: ln_b_f[l, k][None]
    convw9 = ffn_conv_f.reshape(depth, 9, -1)
    nqkv = gdn_conv_f.shape[1]

    shift0 = mod(0, 0) + late_token[:1, :1]
    p, ub0 = _inproj("even_in", x0, shift0, mod(0, 1), w_in, 512)
    pc, ucb = _inproj("even_in_ctx", ctx0, sh_c, sc_c, w_in, 256)
    qkv = _qkv_conv("qkv_conv", p, gdn_conv_f)
    qkv_c = _qkv_conv("qkv_conv_ctx", pc, gdn_conv_f)
    gb = _gates("gates", p, avec, dtvec, scal_blk)
    gb_c = _gates("gates_ctx", pc, avec, dtvec, scal_blk)
    lay = _gate_layouts(gb)
    lay_c = _gate_layouts(gb_c)
    s_zero = jnp.zeros((2 * GDN_HEADS, HEAD_DIM, HEAD_DIM), F32)
    _, _, save_c, inv_c, s_ctx = _scan_fwd("scan_ctx", qkv_c, *lay_c, s_zero)
    o0, o1, save_l, inv_l, _ = _scan_fwd("scan", qkv, *lay, s_ctx)
    pool_blk = (nqkv + GDN_HEADS * HEAD_DIM) // 512
    ypool = _pool(p, pool_w, pscale, pool_blk)
    x1, mix0, y0 = _even_out(o0, o1, p, ypool, x0, normw, mod(0, 2), lng(0, 0), lnb(0, 0), w_out, 256)

    late_mine, late_lands = _chip_send_wait("gather_late_wait", late_sems, late_mine, late_lands, x1)
    own4 = [lax.dynamic_update_slice(land, m[None], (chip,) + (0,) * m.ndim) for land, m in zip(late_lands, late_mine)]
    got4 = _sibling_exchange("gather_late_exchange", own4)

    def of_core(i, core):
        return jnp.where(ic == core, own4[i], got4[i])

    w_oin = jnp.concatenate([of_core(0, 0), of_core(0, 1)], axis=1)
    w_oout = jnp.concatenate([of_core(1, 0), of_core(1, 1)], axis=1).reshape(-1, d)
    w_up = [of_core(2, l) for l in range(depth)]
    w_down = [of_core(3, l) for l in range(depth)]

    def ffn_fwd(l, xin):
        h, ub = _inproj(f"ffn_up{l}", xin, mod(l, 3), mod(l, 4), w_up[l], 256)
        s = _ffn_conv(f"ffn_conv{l}", h, convw9[l])
        xo, y = _ffn_down(f"ffn_down{l}", s, xin, mod(l, 5), lng(l, 1), lnb(l, 1), w_down[l], 256)
        return xo, (h, ub, s, y)

    x2, ffn0 = ffn_fwd(0, x1)
    p1, ub1 = _inproj("odd_in", x2, mod(1, 0), mod(1, 1), w_oin, 512)
    nsc = sconv_f.shape[1] // LANES
    ysc = _sconv(p1, sconv_f)
    z0 = _confconv(p1, conf_conv_f, 3 * nsc)
    x3, mix1, y1 = _odd_out(ysc, z0, x2, cg, cb, mod(1, 2), lng(1, 0), lnb(1, 0), w_oout, 256)
    x4, ffn1 = ffn_fwd(1, x3)
    loss_part, dx4 = _loss_and_grad(x4, tgt, 512)
    loss = lax.psum(loss_part[0, 0], ("x", "y", "c"))

    dmod = [[None] * 6 for _ in range(depth)]
    dlng = [[None, None] for _ in range(depth)]
    dlnb = [[None, None] for _ in range(depth)]
    gbig = {}
    dconv9 = [None] * depth

    def ffn_bwd(l, dxo, xin, saved):
        h, ub, s, y = saved
        ds, dxa, dyb, dgt, dg_, db_ = _ffn_down_bwd(f"ffn_down_bwd{l}", dxo, xin, y, mod(l, 5), lng(l, 1), lnb(l, 1), w_down[l], 256)
        da, dgate, dw9 = _ffn_conv_bwd(f"ffn_conv_bwd{l}", ds, h, convw9[l])
        dh = jnp.concatenate([da, dgate], axis=1)
        dxin, dsh, dsc = _inproj_bwd(f"ffn_up_bwd{l}", dh, xin, mod(l, 3), mod(l, 4), w_up[l], dxa, 256)
        dmod[l][3], dmod[l][4], dmod[l][5] = dsh, dsc, dgt
        dlng[l][1], dlnb[l][1] = dg_, db_
        dconv9[l] = dw9
        dw_up = _matmul_tn(f"dw_up{l}", [(ub, dh)], chip_cols=ffn_w_up.shape[2])
        dw_down = _matmul_tn(f"dw_down{l}", [(s, dyb)]).reshape(4, -1, d)
        return dxin, dw_up, dw_down

    dx3, dwu1, dwd1 = ffn_bwd(1, dx4, x3, ffn1)
    dysc, dz0, dx2a, dyb1, dcg, dcb, dgt, dg_, db_ = _odd_out_bwd(dx3, ysc, z0, x2, y1, cg, cb, mod(1, 2), lng(1, 0), lnb(1, 0), w_oout, 256)
    dmod[1][2], dlng[1][0], dlnb[1][0] = dgt, dg_, db_
    d_gb, d_gc, d_h, dsconv = _sconv_bwd(dysc, p1, sconv_f)
    d_ga, d_gbb, dconf = _confconv_bwd(dz0, p1, conf_conv_f, 3 * nsc)
    dp1 = jnp.concatenate([d_gb, d_gc, d_h, d_ga, d_gbb], axis=1)
    dx2, dsh, dsc = _inproj_bwd("odd_in_bwd", dp1, x2, mod(1, 0), mod(1, 1), w_oin, dx2a, 512)
    dmod[1][0], dmod[1][1] = dsh, dsc
    dw_oout = _matmul_tn("dw_oout", [(mix1, dyb1)]).reshape(4, -1, d)
    dw_oin = _matmul_tn("dw_oin", [(ub1, dp1)], chip_cols=odd_w_in.shape[1])

    dx1, dwu0, dwd0 = ffn_bwd(0, dx2, x1, ffn0)

    tiled_halves = [
        (dw_oin[:, :d // 2], dw_oin[:, d // 2:]),
        (dw_oout[:, :odd_w_out.shape[0] // 2], dw_oout[:, odd_w_out.shape[0] // 2:]),
        (dwu0, dwu1),
        (dwd0, dwd1),
    ]
    keeps = [jnp.where(ic == 0, h0, h1) for h0, h1 in tiled_halves]
    gives = [jnp.where(ic == 0, h1, h0) for h0, h1 in tiled_halves]
    gots = _sibling_exchange("grad_pair_exchange", gives)
    pairs = [_sum_pair(f"grad_pair_sum{i}", kp, gt_, MXU_DTYPE) for i, (kp, gt_) in enumerate(zip(keeps, gots))]
    gs_sems, gs_pairs, gs_lands, gs_token = _chip_send_start("grad_scatter_start", pairs, [], per_peer=True)
    gate0 = mod(0, 2) + gs_token[:1, :1]

    do, dpg, dypool, dx0a, dyb0, dnormw, dgt, dg_, db_ = _even_out_bwd(dx1, o0, o1, p, ypool, x0, y0, normw, gate0, lng(0, 0), lnb(0, 0), w_out, 256)
    dmod[0][2], dlng[0][0], dlnb[0][0] = dgt, dg_, db_
    pool_cts = _pool_bwd(dypool, p, pool_w, pscale, pool_blk)
    dpp, dpool_scale, dpool_w = pool_cts[0], pool_cts[1], jnp.stack(pool_cts[2:])
    dqkv0, dqkv1, dbcol, dgcol, dgrow, ds0 = _scan_bwd("scan_bwd", do, qkv, *lay, save_l, inv_l, s_zero)
    zero_do = jnp.zeros((tc, GDN_HEADS * HEAD_DIM), F32)
    dqkv0_c, dqkv1_c, dbcol_c, dgcol_c, dgrow_c, _ = _scan_bwd("scan_bwd_ctx", zero_do, qkv_c, *lay_c, save_c, inv_c, ds0)
    dgb = _gate_layouts_bwd(dbcol, dgcol, dgrow)
    dgb_c = _gate_layouts_bwd(dbcol_c, dgcol_c, dgrow_c)
    dps, davec, ddtvec = _gates_bwd("gates_bwd", dgb, p, avec, dtvec, scal_blk)
    dps_c, davec_c, ddtvec_c = _gates_bwd("gates_bwd_ctx", dgb_c, pc, avec, dtvec, scal_blk)
    dpqkv, dconv5 = _qkv_conv_bwd("qkv_conv_bwd", dqkv0, dqkv1, p, gdn_conv_f)
    dpqkv_c, dconv5_c = _qkv_conv_bwd("qkv_conv_bwd_ctx", dqkv0_c, dqkv1_c, pc, gdn_conv_f)
    dp = jnp.concatenate([dpqkv, dpg, dpp, dps], axis=1)
    dpc = jnp.concatenate([dpqkv_c, jnp.zeros((tc, n_even_pad - nqkv - LANES), MXU_DTYPE), dps_c], axis=1)
    grad_x, dsh, dsc = _inproj_bwd("even_in_bwd", dp, x0, mod(0, 0), mod(0, 1), w_in, dx0a, 512)
    dmod[0][0], dmod[0][1] = dsh, dsc
    _, dsh_c, dsc_c = _inproj_bwd("even_in_bwd_ctx", dpc, ctx0, sh_c, sc_c, w_in, None, 256)
    dw_in = _matmul_tn("dw_in", [(ub0, dp), (ucb, dpc)])
    dw_out = _matmul_tn("dw_out", [(mix0, dyb0)])

    gs_pairs, gs_lands = _chip_send_wait("grad_scatter_wait", gs_sems, gs_pairs, gs_lands, dw_in, per_peer=True)
    own4 = [lax.dynamic_update_slice(land, lax.dynamic_index_in_dim(pr, chip, axis=0, keepdims=True), (chip,) + (0,) * (pr.ndim - 1))
            for land, pr in zip(gs_lands, gs_pairs)]
    n_in, n_out = even_w_in.shape[1], even_w_out.shape[0]
    gflat = jnp.stack([_pack_flat([dw_in[:, k * n_in:(k + 1) * n_in], dw_out[k * n_out:(k + 1) * n_out]], half_mult,
                                  MXU_DTYPE).reshape(2, rh, LANES) for k in range(4)], axis=1)
    got_flat, *got4 = _sibling_exchange("grad_late_exchange", [jnp.where(ic == 0, gflat[1], gflat[0])] + own4)
    pair_flat = _sum_pair("grad_pair_sum_flat", jnp.where(ic == 0, gflat[0], gflat[1]), got_flat, MXU_DTYPE)
    sum_flat = _sum_axis1("grad_chip_sum_flat", _chip_scatter("grad_chip_scatter", [pair_flat])[0])

    def both_halves(i):
        mine_sum = _sum_axis1(f"grad_chip_sum{i}_own", own4[i][None])[0]
        sib_sum = _sum_axis1(f"grad_chip_sum{i}_sibling", got4[i][None])[0]
        return jnp.where(ic == 0, jnp.stack([mine_sum, sib_sum]), jnp.stack([sib_sum, mine_sum]))

    g_even_in, g_even_out = _unpack_flat(sum_flat.reshape(-1), flat_shapes)
    g_shards = {'even_w_in': g_even_in, 'even_w_out': g_even_out, 'odd_w_in': both_halves(0).reshape(odd_w_in.shape),
                'odd_w_out': both_halves(1).reshape(odd_w_out.shape), 'ffn_w_up': both_halves(2), 'ffn_w_down': both_halves(3)}

    dmod_rows = jnp.stack([jnp.concatenate(dmod[l], axis=1)[0] for l in range(depth)])
    dmod_c = jnp.concatenate([dsh_c[0], dsc_c[0], jnp.zeros((4 * d,), F32)])
    dmod_c_rows = jnp.stack([dmod_c] + [jnp.zeros_like(dmod_c)] * (depth - 1))
    small_g = {
        'ln_g': jnp.stack([jnp.stack([dlng[l][k][0] for k in range(2)]) for l in range(depth)]),
        'ln_b': jnp.stack([jnp.stack([dlnb[l][k][0] for k in range(2)]) for l in range(depth)]),
        'gdn_conv_w': dconv5 + dconv5_c,
        'gdn_a_log': (davec + davec_c)[0, n_scal // 2:n_scal].reshape(gdn_a_log.shape),
        'gdn_dt_bias': (ddtvec + ddtvec_c)[0, n_scal // 2:n_scal].reshape(gdn_dt_bias.shape),
        'gdn_norm_w': dnormw[0], 'pool_w': dpool_w, 'pool_scale': dpool_scale[0],
        'sconv_w': dsconv, 'conf_conv_w': dconf, 'conf_ln_g': dcg[0], 'conf_ln_b': dcb[0],
        'ffn_conv_w': jnp.stack(dconv9).reshape(depth, 3, 3, -1),
    }
    small_names = list(small_g)
    s3_items = [dmod_rows, dmod_c_rows] + [small_g[n] for n in small_names]
    s3 = _pack_rows(s3_items)
    g3 = _all_gather8("gather_small_grads", s3, True).reshape(8, -1, LANES)
    tot3 = _sum_axis1("small_grad_sum", g3[None])[0]
    tot_items = _unpack_rows(tot3, [a.shape for a in s3_items])
    dmod_sum, dmod_c_sum = tot_items[0], tot_items[1]
    small_tot = dict(zip(small_names, tot_items[2:]))
    grad_ada_b = dmod_sum + dmod_c_sum
    rows_all = g3[:, :_packed_rows(dmod_rows.shape)].reshape(8, depth, 6 * d)
    cols = lax.dynamic_slice_in_dim(rows_all, chip * n_ada, n_ada, axis=2)
    crow = lax.dynamic_slice_in_dim(dmod_c_sum, chip * n_ada, n_ada, axis=1)
    dmod16 = jnp.concatenate([jnp.transpose(cols, (1, 0, 2)), crow[:, None, :], jnp.zeros((depth, 7, n_ada), F32)], axis=1)
    grad_ada_w, dsil = _ada_grads(c16, dmod16, ada_w)
    s4 = jnp.concatenate([dsil[0, 8][None], jnp.zeros((7, d), F32)], axis=0).reshape(-1, LANES)
    g4 = _all_gather8("gather_cctx", s4, True).reshape(8, 8, d)
    grad_c_ctx = _cctx_grad(g4[0::2, 0][:, None, :], c_ctx[None])[0]

    def my_cols(a, n):
        return lax.dynamic_slice_in_dim(a, chip * n, n, axis=a.ndim - 1)

    grads = dict(g_shards)
    grads['c_ctx'] = grad_c_ctx
    grads['ada_w'] = grad_ada_w
    grads['ada_b'] = grad_ada_b
    for n in ['ln_g', 'ln_b', 'gdn_conv_w', 'sconv_w', 'conf_conv_w', 'ffn_conv_w']:
        grads[n] = my_cols(small_tot[n], wts[n].shape[-1])
    for n in ['gdn_a_log', 'gdn_dt_bias', 'gdn_norm_w', 'pool_w', 'pool_scale', 'conf_ln_g', 'conf_ln_b']:
        grads[n] = small_tot[n]

    delta, new_m, new_v = {}, {}, {}
    big_adam = list(g_shards) + ['ada_w']
    for n in big_adam:
        shp = wts[n].shape
        as2d = lambda a: a.reshape(-1, shp[-1])
        dl, nm, nv = _adamw("adamw_" + n, as2d(wts[n]), as2d(grads[n]), as2d(mom[n]), as2d(var[n]))
        delta[n], new_m[n], new_v[n] = dl.reshape(shp), nm.reshape(shp), nv.reshape(shp)
    small_adam = [n for n in names if n not in big_adam]
    for n in small_adam:
        shp = wts[n].shape
        whole = lambda a: a.reshape((1,) + shp) if len(shp) == 1 else a
        dl, nm, nv = _adamw("adamw_" + n, whole(wts[n]), whole(grads[n]), whole(mom[n]), whole(var[n]))
        delta[n], new_m[n], new_v[n] = dl.reshape(shp), nm.reshape(shp), nv.reshape(shp)

    return (loss, grad_x[None], *[grads[n] for n in names], *[delta[n] for n in names],
            *[new_m[n] for n in names], *[new_v[n] for n in names])
```

```python
import functools
import math

import jax
import jax.numpy as jnp
from jax import lax
from jax.experimental import pallas as pl
from jax.experimental.pallas import tpu as pltpu

F32 = jnp.float32
MXU_DTYPE = jnp.bfloat16
HIGHEST = lax.Precision.HIGHEST
MESH = pl.DeviceIdType.MESH

D_MODEL = 1024
GRID_W = 64
GDN_HEADS = 4
HEAD_DIM = 128
GDN_CHUNK = 64
POOL_WINDOWS = (2, 4, 8, 16)
GDN_CONV = 5
SC_WIDTH = 3
CF_WIDTH = 31
ALPHA = 4.0 ** 0.25
LN_EPS = 1e-5
RMS_EPS = 1e-6
LANES = 128
VMEM_LIMIT_BYTES = 58 * 1024 * 1024

ADAM_LR, ADAM_B1, ADAM_B2, ADAM_EPS, ADAM_WD, ADAM_STEP = 0.001, 0.9, 0.999, 1e-08, 0.01, 10


def _blocked(name, fn, grid, ins, in_specs, out_shapes, out_specs, acc=(), acc_axis=None, scratch=()):
    n_in = len(ins)
    n_out = len(out_shapes)

    def body(*refs):
        vals = [r[...] for r in refs[:n_in]]
        res = fn(*vals, *refs[n_in + n_out:])
        if not isinstance(res, (tuple, list)):
            res = (res,)
        for k, (r, v) in enumerate(zip(refs[n_in:n_in + n_out], res)):
            if k in acc:
                first = pl.program_id(acc_axis) == 0

                @pl.when(first)
                def _(r=r, v=v):
                    r[...] = v.astype(r.dtype)

                @pl.when(jnp.logical_not(first))
                def _(r=r, v=v):
                    r[...] += v.astype(r.dtype)
            else:
                r[...] = v.astype(r.dtype)

    return pl.pallas_call(
        body, name=name, grid=grid, in_specs=in_specs, out_specs=out_specs, out_shape=out_shapes,
        scratch_shapes=list(scratch),
        compiler_params=pltpu.CompilerParams(dimension_semantics=("arbitrary",) * len(grid),
                                             vmem_limit_bytes=VMEM_LIMIT_BYTES),
    )(*ins)


def _sds(shape, dtype=F32):
    return jax.ShapeDtypeStruct(tuple(shape), dtype)


def _tok(tm, n, col=0):
    return pl.BlockSpec((tm, n), lambda t: (t, col))


def _res(shape):
    nd = len(shape)
    return pl.BlockSpec(tuple(shape), lambda t: (0,) * nd)


def _silu(x):
    return x * jax.nn.sigmoid(x)


def _layernorm(r, g, b):
    mu = jnp.mean(r, -1, keepdims=True)
    d = r - mu
    var = jnp.mean(d * d, -1, keepdims=True)
    return d * lax.rsqrt(var + LN_EPS) * g + b


def _mm_nn_impl(a, b):
    return jnp.dot(a.astype(MXU_DTYPE), b.astype(MXU_DTYPE), preferred_element_type=F32)


def _mm_nt_impl(a, b):
    return lax.dot_general(a.astype(MXU_DTYPE), b.astype(MXU_DTYPE), (((1,), (1,)), ((), ())), preferred_element_type=F32)


def _mm_tn_impl(a, b):
    return lax.dot_general(a.astype(MXU_DTYPE), b.astype(MXU_DTYPE), (((0,), (0,)), ((), ())), preferred_element_type=F32)


def _mm_vjp(mm, mm_da, mm_db):
    f = jax.custom_vjp(mm)
    f.defvjp(lambda a, b: (mm(a, b), (a, b)), lambda res, g: (mm_da(g, res[1]), mm_db(res[0], g)))
    return f


_mm = _mm_vjp(_mm_nn_impl, lambda g, b: _mm_nt_impl(g, b), lambda a, g: _mm_tn_impl(a, g))
_mm_nt = _mm_vjp(_mm_nt_impl, lambda g, b: _mm_nn_impl(g, b), lambda a, g: _mm_tn_impl(g, a))
_mm_tn = _mm_vjp(_mm_tn_impl, lambda g, b: _mm_nt_impl(b, g), lambda a, g: _mm_nn_impl(a, g))


def _row(w, k):
    rows = lax.broadcasted_iota(jnp.int32, w.shape, 0)
    return jnp.sum(jnp.where(rows == k, w, 0.0), axis=0, keepdims=True)


def _col_mask(shape, dc):
    col = lax.broadcasted_iota(jnp.int32, shape, 0) & (GRID_W - 1)
    return (col + dc >= 0) & (col + dc < GRID_W)


def _center(x_ext, halo):
    return x_ext[halo:x_ext.shape[0] - halo]


def _make_dwconv(taps, halo):
    assert all(abs(s) <= halo for s, _ in taps)

    def shifted(x_ext, s):
        r = x_ext if s == 0 else pltpu.roll(x_ext, (-s) % x_ext.shape[0], 0)
        return _center(r, halo)

    @jax.custom_vjp
    def conv(x_ext, w):
        acc = None
        for k, (s, dc) in enumerate(taps):
            r = shifted(x_ext, s)
            if dc != 0:
                r = jnp.where(_col_mask(r.shape, dc), r, 0.0)
            term = r * _row(w, k)
            acc = term if acc is None else acc + term
        return acc

    def fwd(x_ext, w):
        return conv(x_ext, w), (x_ext, w)

    def bwd(res, dy):
        x_ext, w = res
        n = x_ext.shape[0]
        rows = lax.broadcasted_iota(jnp.int32, w.shape, 0)
        pad = jnp.zeros((halo, dy.shape[1]), F32)
        dx = None
        dw = jnp.zeros(w.shape, F32)
        for k, (s, dc) in enumerate(taps):
            dym = dy if dc == 0 else jnp.where(_col_mask(dy.shape, dc), dy, 0.0)
            dw = dw + jnp.where(rows == k, jnp.sum(dym * shifted(x_ext, s), axis=0, keepdims=True), 0.0)
            t = jnp.concatenate([pad, dym * _row(w, k), pad], axis=0)
            if s != 0:
                t = pltpu.roll(t, s % n, 0)
            dx = t if dx is None else dx + t
        return dx, dw

    conv.defvjp(fwd, bwd)

    @jax.custom_vjp
    def known(x_ext, w, y):
        return y

    known.defvjp(lambda x_ext, w, y: (y, (x_ext, w)), lambda res, dy: (*bwd(res, dy), jnp.zeros_like(dy)))
    return conv, known


def _taps_1d(width):
    return tuple((k - width // 2, 0) for k in range(width))


HALO_SHORT = 8
HALO_CONF = 16
HALO_GRID = 72
_conv5 = _make_dwconv(_taps_1d(GDN_CONV), HALO_SHORT)[0]
_conv3 = _make_dwconv(_taps_1d(SC_WIDTH), HALO_SHORT)[0]
_conv31 = _make_dwconv(_taps_1d(CF_WIDTH), HALO_CONF)[0]
_conv3x3, _conv3x3_known = _make_dwconv(tuple((dr * GRID_W + dc, dc) for dr in (-1, 0, 1) for dc in (-1, 0, 1)), HALO_GRID)
_pool_sums = {w: _make_dwconv(tuple((s, 0) for s in range(-(w // 2), w - w // 2)), HALO_SHORT)[0] for w in POOL_WINDOWS}


def _all_gather8(name, blk, in_vmem):
    m_per, n = blk.shape
    space = pltpu.VMEM if in_vmem else pl.ANY

    def body(x_ref, out_ref, send_sems, recv_sems, local_sem):
        x, y, c = lax.axis_index("x"), lax.axis_index("y"), lax.axis_index("c")
        me, sibling = (x, y, c), (x, y, 1 - c)
        chips = [(1 - x, y), (x, 1 - y), (1 - x, 1 - y)]

        def rows(px, py, pc):
            return out_ref.at[pl.ds((4 * px + 2 * py + pc) * m_per, m_per), :]

        def copy(k, block, to, src=None):
            return pltpu.make_async_remote_copy(
                src_ref=rows(*block) if src is None else src, dst_ref=rows(*block),
                send_sem=send_sems.at[k], recv_sem=recv_sems.at[k], device_id=to, device_id_type=MESH)

        mine = pltpu.make_async_copy(x_ref, rows(*me), local_sem)
        mine.start()
        first = [copy(0, me, sibling, src=x_ref)]
        first += [copy(1 + j, me, (*chip, c), src=x_ref) for j, chip in enumerate(chips)]
        for cp in first:
            cp.start()
        passed = [copy(4 + j, (*chip, c), sibling) for j, chip in enumerate(chips)]
        for j, chip in enumerate(chips):
            copy(1 + j, (*chip, c), me).wait_recv()
            passed[j].start()
        copy(0, sibling, me).wait_recv()
        for j, chip in enumerate(chips):
            copy(4 + j, (*chip, 1 - c), me).wait_recv()
        for cp in first + passed:
            cp.wait_send()
        mine.wait()

    return pl.pallas_call(
        body, name=name, out_shape=_sds((8 * m_per, n), blk.dtype),
        in_specs=[pl.BlockSpec(memory_space=space)], out_specs=pl.BlockSpec(memory_space=space),
        scratch_shapes=[pltpu.SemaphoreType.DMA((7,)), pltpu.SemaphoreType.DMA((7,)), pltpu.SemaphoreType.DMA],
        compiler_params=pltpu.CompilerParams(vmem_limit_bytes=VMEM_LIMIT_BYTES),
    )(blk)


_ANY = pl.BlockSpec(memory_space=pl.ANY)


def _sibling_exchange(name, sends):
    n = len(sends)

    def body(*refs):
        x, y, c = lax.axis_index("x"), lax.axis_index("y"), lax.axis_index("c")
        send_sems, recv_sems = refs[2 * n:]
        copies = [pltpu.make_async_remote_copy(src_ref=refs[i], dst_ref=refs[n + i], send_sem=send_sems.at[i],
                                               recv_sem=recv_sems.at[i], device_id=(x, y, 1 - c), device_id_type=MESH)
                  for i in range(n)]
        for cp in copies:
            cp.start()
        for cp in copies:
            cp.wait()

    return pl.pallas_call(
        body, name=name, out_shape=[_sds(s.shape, s.dtype) for s in sends],
        in_specs=[_ANY] * n, out_specs=[_ANY] * n,
        scratch_shapes=[pltpu.SemaphoreType.DMA((n,)), pltpu.SemaphoreType.DMA((n,))],
    )(*sends)


def _chip_scatter(name, parts):
    n = len(parts)

    def body(*refs):
        p_refs, out_refs = refs[:n], refs[n:2 * n]
        send_sems, recv_sems, local_sems = refs[2 * n:]
        x, y, c = lax.axis_index("x"), lax.axis_index("y"), lax.axis_index("c")
        sibling = (x, y, 1 - c)
        me_chip = 2 * x + y
        chips = [(1 - x, y), (x, 1 - y), (1 - x, 1 - y)]

        def chip_id(chip):
            return 2 * chip[0] + chip[1]

        def copy(i, k, src, dst, to):
            return pltpu.make_async_remote_copy(src_ref=src, dst_ref=dst, send_sem=send_sems.at[i, k],
                                                recv_sem=recv_sems.at[i, k], device_id=to, device_id_type=MESH)

        mine = [pltpu.make_async_copy(p_refs[i].at[me_chip], out_refs[i].at[c, me_chip], local_sems.at[i]) for i in range(n)]
        first = []
        for i in range(n):
            first.append(copy(i, 0, p_refs[i].at[me_chip], out_refs[i].at[c, me_chip], sibling))
            first += [copy(i, 1 + j, p_refs[i].at[chip_id(chip)], out_refs[i].at[c, me_chip], (*chip, c))
                      for j, chip in enumerate(chips)]
        for cp in mine + first:
            cp.start()
        passed = []
        for j, chip in enumerate(chips):
            for i in range(n):
                landed = out_refs[i].at[c, chip_id(chip)]
                copy(i, 1 + j, p_refs[i].at[0], landed, sibling).wait_recv()
                passed.append(copy(i, 4 + j, landed, landed, sibling))
                passed[-1].start()
        for i in range(n):
            copy(i, 0, p_refs[i].at[0], out_refs[i].at[1 - c, me_chip], sibling).wait_recv()
            for j, chip in enumerate(chips):
                copy(i, 4 + j, p_refs[i].at[0], out_refs[i].at[1 - c, chip_id(chip)], sibling).wait_recv()
        for cp in first + passed:
            cp.wait_send()
        for cp in mine:
            cp.wait()

    return pl.pallas_call(
        body, name=name, out_shape=[_sds((2,) + p.shape, p.dtype) for p in parts],
        in_specs=[_ANY] * n, out_specs=[_ANY] * n,
        scratch_shapes=[pltpu.SemaphoreType.DMA((n, 7)), pltpu.SemaphoreType.DMA((n, 7)), pltpu.SemaphoreType.DMA((n,))],
    )(*parts)


_HBM = pl.BlockSpec(memory_space=pltpu.HBM)
_SEM = pl.BlockSpec(memory_space=pltpu.SEMAPHORE)
_SPLIT_CALL = pltpu.CompilerParams(has_side_effects=pltpu.SideEffectType.DATAFLOW_SIDE_EFFECTING)


def _peer_chips():
    x, y = lax.axis_index("x"), lax.axis_index("y")
    return [(1 - x, y), (x, 1 - y), (1 - x, 1 - y)]


def _chip_send_start(name, blocks, before, per_peer=False):
    n, nb = len(blocks), len(before)

    def body(*refs):
        x_refs, land_refs = refs[:n], refs[n:2 * n]
        sems, token = refs[2 * n + nb:2 * n + nb + 6 * n], refs[-1]
        c = lax.axis_index("c")
        me_chip = 2 * lax.axis_index("x") + lax.axis_index("y")
        for i in range(n):
            for j, chip in enumerate(_peer_chips()):
                src = x_refs[i].at[2 * chip[0] + chip[1]] if per_peer else x_refs[i]
                pltpu.make_async_remote_copy(src_ref=src, dst_ref=land_refs[i].at[me_chip], send_sem=sems[6 * i + j],
                                             recv_sem=sems[6 * i + 3 + j], device_id=(*chip, c), device_id_type=MESH).start()
        token[...] = jnp.zeros(token.shape, token.dtype)

    lands = [lax.empty(b.shape if per_peer else (4,) + b.shape, b.dtype) for b in blocks]
    hbm = [pltpu.with_memory_space_constraint(a, pltpu.HBM) for a in list(blocks) + lands]
    outs = pl.pallas_call(
        body, name=name,
        out_shape=tuple([pltpu.SemaphoreType.DMA(())] * (6 * n)) + tuple(pltpu.HBM(a.shape, a.dtype) for a in hbm) + (_sds((8, LANES)),),
        in_specs=tuple([_HBM] * (2 * n) + [_ANY] * nb),
        out_specs=tuple([_SEM] * (6 * n) + [_HBM] * (2 * n) + [pl.BlockSpec(memory_space=pltpu.VMEM)]),
        input_output_aliases={i: 6 * n + i for i in range(2 * n)}, compiler_params=_SPLIT_CALL,
    )(*hbm, *before)
    return outs[:6 * n], outs[6 * n:7 * n], outs[7 * n:8 * n], outs[8 * n]


def _chip_send_wait(name, sems, blocks, lands, after, per_peer=False):
    n = len(blocks)

    def body(*refs):
        x_refs, land_refs, sems = refs[:n], refs[n:2 * n], refs[2 * n:8 * n]
        c = lax.axis_index("c")
        for i in range(n):
            for j, chip in enumerate(_peer_chips()):
                src = x_refs[i].at[2 * chip[0] + chip[1]] if per_peer else x_refs[i]
                cp = pltpu.make_async_remote_copy(src_ref=src, dst_ref=land_refs[i].at[2 * chip[0] + chip[1]],
                                                  send_sem=sems[6 * i + j], recv_sem=sems[6 * i + 3 + j],
                                                  device_id=(*chip, c), device_id_type=MESH)
                cp.wait_send()
                cp.wait_recv()

    outs = pl.pallas_call(
        body, name=name, out_shape=tuple(pltpu.HBM(a.shape, a.dtype) for a in list(blocks) + list(lands)),
        in_specs=tuple([_HBM] * (2 * n) + [_SEM] * (6 * n) + [_ANY]), out_specs=tuple([_HBM] * (2 * n)),
        input_output_aliases={i: i for i in range(2 * n)}, compiler_params=_SPLIT_CALL,
    )(*blocks, *lands, *sems, after)
    return outs[:n], outs[n:]


def _pack_flat(arrays, multiple, dtype):
    flat = jnp.concatenate([a.reshape(-1).astype(dtype) for a in arrays])
    pad = (-flat.shape[0]) % multiple
    if pad:
        flat = jnp.concatenate([flat, jnp.zeros((pad,), dtype)])
    return flat


def _unpack_flat(flat, shapes):
    out, off = [], 0
    for s in shapes:
        n = math.prod(s)
        out.append(flat[off:off + n].reshape(s))
        off += n
    return out


def _packed_rows(shape):
    return -(-math.prod(shape) // (8 * LANES)) * 8


def _pack_rows(arrays):
    parts = []
    for a in arrays:
        rows = _packed_rows(a.shape)
        flat = a.reshape(-1).astype(F32)
        if rows * LANES != flat.shape[0]:
            flat = jnp.pad(flat, (0, rows * LANES - flat.shape[0]))
        parts.append(flat.reshape(rows, LANES))
    return jnp.concatenate(parts, axis=0)


def _unpack_rows(packed, shapes):
    out, r0 = [], 0
    for s in shapes:
        rows = _packed_rows(s)
        out.append(packed[r0:r0 + rows].reshape(-1)[:math.prod(s)].reshape(s))
        r0 += rows
    return out


def _row_tile(r, cap, mult=16):
    for cand in range(min(cap, r) // mult * mult, 0, -mult):
        if r % cand == 0:
            return cand
    return r


ELEMENTWISE_BLOCK_ELEMS = 1 << 20


def _sum_axis1(name, x, out_dtype=F32):
    h, k, r, n = x.shape
    tr = _row_tile(r, max(16, ELEMENTWISE_BLOCK_ELEMS // (k * n)), 32 // x.dtype.itemsize)

    def fn(s):
        acc = s[0].astype(F32)
        for i in range(1, k):
            acc = acc + s[i].astype(F32)
        return acc

    return _blocked(name, fn, (h, r // tr), [x], [pl.BlockSpec((None, k, tr, n), lambda i, t: (i, 0, t, 0))],
                    [_sds((h, r, n), out_dtype)], [pl.BlockSpec((None, tr, n), lambda i, t: (i, t, 0))])[0]


def _sum_pair(name, a, b, out_dtype):
    lead, r, n = a.shape
    tr = _row_tile(r, max(16, ELEMENTWISE_BLOCK_ELEMS // n), 32 // min(a.dtype.itemsize, jnp.dtype(out_dtype).itemsize))
    spec = pl.BlockSpec((None, tr, n), lambda i, t: (i, t, 0))
    return _blocked(name, lambda x, y: x.astype(F32) + y.astype(F32), (lead, r // tr), [a, b], [spec, spec],
                    [_sds((lead, r, n), out_dtype)], [spec])[0]


def _adamw(name, w, g, m, v):
    if w.ndim == 2:
        r, c = w.shape
        tr = _row_tile(r, max(8, (1 << 21) // (4 * c)), 8)
        grid, spec = (r // tr,), pl.BlockSpec((tr, c), lambda t: (t, 0))
    else:
        grid, spec = (1,), _res(w.shape)
    bc1 = 1.0 - ADAM_B1 ** ADAM_STEP
    bc2 = 1.0 - ADAM_B2 ** ADAM_STEP

    def fn(w, g, m, v):
        nm = ADAM_B1 * m + (1.0 - ADAM_B1) * g
        nv = ADAM_B2 * v + (1.0 - ADAM_B2) * (g * g)
        delta = -ADAM_LR * ((nm / bc1) / (jnp.sqrt(nv / bc2) + ADAM_EPS) + ADAM_WD * w)
        return delta, nm, nv

    return _blocked(name, fn, grid, [w, g, m, v], [spec] * 4, [_sds(w.shape)] * 3, [spec] * 3)


def _mod_rows(c16, ada_w):
    depth, d, n = ada_w.shape

    def fn(c, w):
        return _mm(_silu(c), w)

    return _blocked("mod_rows", fn, (depth,), [c16, ada_w],
                    [_res(c16.shape), pl.BlockSpec((None, d, n), lambda l: (l, 0, 0))],
                    [_sds((depth, 16, n))], [pl.BlockSpec((None, 16, n), lambda l: (l, 0, 0))])[0]


def _ada_grads(c16, dmod16, ada_w):
    depth, d, n = ada_w.shape

    def fn(c, dm, w):
        return _mm_tn(_silu(c), dm), _mm_nt(dm, w)

    return _blocked("ada_grads", fn, (depth,), [c16, dmod16, ada_w],
                    [_res(c16.shape), pl.BlockSpec((None, 16, n), lambda l: (l, 0, 0)),
                     pl.BlockSpec((None, d, n), lambda l: (l, 0, 0))],
                    [_sds((depth, d, n)), _sds((depth, 16, d))],
                    [pl.BlockSpec((None, d, n), lambda l: (l, 0, 0)), pl.BlockSpec((None, 16, d), lambda l: (l, 0, 0))])


def _cctx_grad(parts, c_ctx_row):
    def fn(p, c):
        tot = ((p[0] + p[1]) + p[2]) + p[3]
        _, vjp = jax.vjp(_silu, c)
        return vjp(tot)[0]

    return _blocked("cctx_grad", fn, (1,), [parts, c_ctx_row], [_res(parts.shape), _res(c_ctx_row.shape)],
                    [_sds(c_ctx_row.shape)], [_res(c_ctx_row.shape)])[0]


def _modulate(x, sh, sc):
    return x * (1.0 + sc) + sh


def _weight_operand(w):
    if isinstance(w, tuple):
        return w[0], w[1], tuple(b for b in w[1].block_shape if b is not None)
    return w, _res(w.shape), w.shape


def _matmul_cols(u, w):
    if w.ndim == 2:
        return jnp.dot(u, w, preferred_element_type=F32)
    return jnp.concatenate([jnp.dot(u, w[j], preferred_element_type=F32) for j in range(w.shape[0])], axis=1)


def _matmul_cols_t(dp, w):
    if w.ndim == 2:
        return _mm_nt(dp, w)
    nj = w.shape[2]
    return functools.reduce(lambda a, b: a + b, [_mm_nt(dp[:, j * nj:(j + 1) * nj], w[j]) for j in range(w.shape[0])])


def _rows_weight(w):
    return w if w.ndim == 2 else w.reshape(w.shape[0] * w.shape[1], w.shape[2])


def _inproj(name, x, sh, sc, w, tm):
    t, d = x.shape
    w, w_spec, w_blk = _weight_operand(w)
    n = w_blk[-1] * (w_blk[0] if len(w_blk) == 3 else 1)
    tm = min(tm, t)

    def fn(x, sh, sc, w):
        u = _modulate(x, sh, sc).astype(MXU_DTYPE)
        return _matmul_cols(u, w), u

    return _blocked(name, fn, (t // tm,), [x, sh, sc, w],
                    [_tok(tm, d), _res(sh.shape), _res(sc.shape), w_spec],
                    [_sds((t, n)), _sds((t, d), MXU_DTYPE)], [_tok(tm, n), _tok(tm, d)])


def _inproj_bwd(name, dp, x, sh, sc, w, add, tm):
    t, d = x.shape
    w, w_spec, w_blk = _weight_operand(w)
    n = w_blk[-1] * (w_blk[0] if len(w_blk) == 3 else 1)
    tm = min(tm, t)
    has_add = add is not None

    def fn(dp, x, sh, sc, w, *rest):
        du = _matmul_cols_t(dp, w)
        _, vjp = jax.vjp(_modulate, x, sh, sc)
        dx, dsh, dsc = vjp(du)
        if has_add:
            dx = dx + rest[0]
        return dx, dsh, dsc

    ins = [dp, x, sh, sc, w] + ([add] if has_add else [])
    specs = [_tok(tm, n), _tok(tm, d), _res(sh.shape), _res(sc.shape), w_spec] + ([_tok(tm, d)] if has_add else [])
    return _blocked(name, fn, (t // tm,), ins, specs,
                    [_sds((t, d)), _sds(sh.shape), _sds(sc.shape)], [_tok(tm, d), _res(sh.shape), _res(sc.shape)],
                    acc=(1, 2), acc_axis=0)


def _residual_ln(y, x, gt, lng, lnb):
    return _layernorm(ALPHA * x + gt * y, lng, lnb)


def _gdn_mix(o0, o1, pg, yp, normw):
    o = o0 + o1
    heads = []
    for h in range(GDN_HEADS):
        oh = o[:, h * HEAD_DIM:(h + 1) * HEAD_DIM]
        heads.append(oh * lax.rsqrt(jnp.mean(oh * oh, -1, keepdims=True) + RMS_EPS) * normw)
    on = jnp.concatenate(heads, axis=-1) * _silu(pg)
    return jnp.concatenate([on, yp], axis=-1)


def _even_out(o0, o1, p, ypool, x, normw, gt, lng, lnb, w, tm):
    t, d = x.shape
    tm = min(tm, t)
    gate_blk = 3

    def fn(o0, o1, pg, yp, x, normw, gt, lng, lnb, w):
        mix = _gdn_mix(o0, o1, pg, yp, normw).astype(MXU_DTYPE)
        y = jnp.dot(mix, w, preferred_element_type=F32)
        return _residual_ln(y, x, gt, lng, lnb), mix, y

    return _blocked("even_out", fn, (t // tm,), [o0, o1, p, ypool, x, normw, gt, lng, lnb, w],
                    [_tok(tm, 512), _tok(tm, 512), _tok(tm, 512, gate_blk), _tok(tm, 512), _tok(tm, d),
                     _res(normw.shape), _res(gt.shape), _res(lng.shape), _res(lnb.shape), _res(w.shape)],
                    [_sds((t, d)), _sds((t, d), MXU_DTYPE), _sds((t, d))], [_tok(tm, d)] * 3)


def _even_out_bwd(dx1, o0, o1, p, ypool, x, y, normw, gt, lng, lnb, w, tm):
    t, d = x.shape
    tm = min(tm, t)

    def fn(dx1, o0, o1, pg, yp, x, y, normw, gt, lng, lnb, w):
        _, vjp2 = jax.vjp(_residual_ln, y, x, gt, lng, lnb)
        dy, dx, dgt, dlng, dlnb = vjp2(dx1)
        dyb = dy.astype(MXU_DTYPE)
        dmix = _mm_nt(dyb, w)
        _, vjp1 = jax.vjp(_gdn_mix, o0, o1, pg, yp, normw)
        do, _, dpg, dyp, dnormw = vjp1(dmix)
        return do, dpg, dyp, dx, dyb, dnormw, dgt, dlng, dlnb

    return _blocked("even_out_bwd", fn, (t // tm,), [dx1, o0, o1, p, ypool, x, y, normw, gt, lng, lnb, w],
                    [_tok(tm, d), _tok(tm, 512), _tok(tm, 512), _tok(tm, 512, 3), _tok(tm, 512),
                     _tok(tm, d), _tok(tm, d), _res(normw.shape), _res(gt.shape), _res(lng.shape), _res(lnb.shape),
                     _res(w.shape)],
                    [_sds((t, 512)), _sds((t, 512), MXU_DTYPE), _sds((t, 512)), _sds((t, d)), _sds((t, d), MXU_DTYPE),
                     _sds(normw.shape), _sds(gt.shape), _sds(lng.shape), _sds(lnb.shape)],
                    [_tok(tm, 512), _tok(tm, 512), _tok(tm, 512), _tok(tm, d), _tok(tm, d),
                     _res(normw.shape), _res(gt.shape), _res(lng.shape), _res(lnb.shape)],
                    acc=(5, 6, 7, 8), acc_axis=0)


def _odd_mix(ysc, z0, cg, cb):
    z = _silu(_layernorm(z0, cg, cb))
    return jnp.concatenate([ysc, z], axis=-1)


def _odd_out(ysc, z0, x, cg, cb, gt, lng, lnb, w, tm):
    t, d = x.shape
    tm = min(tm, t)

    def fn(ysc, z0, x, cg, cb, gt, lng, lnb, w):
        mix = _odd_mix(ysc, z0, cg, cb).astype(MXU_DTYPE)
        y = jnp.dot(mix, w, preferred_element_type=F32)
        return _residual_ln(y, x, gt, lng, lnb), mix, y

    return _blocked("odd_out", fn, (t // tm,), [ysc, z0, x, cg, cb, gt, lng, lnb, w],
                    [_tok(tm, 512), _tok(tm, 512), _tok(tm, d), _res(cg.shape), _res(cb.shape), _res(gt.shape),
                     _res(lng.shape), _res(lnb.shape), _res(w.shape)],
                    [_sds((t, d)), _sds((t, d), MXU_DTYPE), _sds((t, d))], [_tok(tm, d)] * 3)


def _odd_out_bwd(dx3, ysc, z0, x, y, cg, cb, gt, lng, lnb, w, tm):
    t, d = x.shape
    tm = min(tm, t)

    def fn(dx3, ysc, z0, x, y, cg, cb, gt, lng, lnb, w):
        _, vjp2 = jax.vjp(_residual_ln, y, x, gt, lng, lnb)
        dy, dx, dgt, dlng, dlnb = vjp2(dx3)
        dyb = dy.astype(MXU_DTYPE)
        dmix = _mm_nt(dyb, w)
        _, vjp1 = jax.vjp(_odd_mix, ysc, z0, cg, cb)
        dysc, dz0, dcg, dcb = vjp1(dmix)
        return dysc, dz0, dx, dyb, dcg, dcb, dgt, dlng, dlnb

    return _blocked("odd_out_bwd", fn, (t // tm,), [dx3, ysc, z0, x, y, cg, cb, gt, lng, lnb, w],
                    [_tok(tm, d), _tok(tm, 512), _tok(tm, 512), _tok(tm, d), _tok(tm, d), _res(cg.shape), _res(cb.shape),
                     _res(gt.shape), _res(lng.shape), _res(lnb.shape), _res(w.shape)],
                    [_sds((t, 512)), _sds((t, 512)), _sds((t, d)), _sds((t, d), MXU_DTYPE),
                     _sds(cg.shape), _sds(cb.shape), _sds(gt.shape), _sds(lng.shape), _sds(lnb.shape)],
                    [_tok(tm, 512), _tok(tm, 512), _tok(tm, d), _tok(tm, d),
                     _res(cg.shape), _res(cb.shape), _res(gt.shape), _res(lng.shape), _res(lnb.shape)],
                    acc=(4, 5, 6, 7, 8), acc_axis=0)


def _ffn_down(name, s, x, gt, lng, lnb, w, tm):
    t, d = x.shape
    f = s.shape[1]
    w, w_spec, _ = _weight_operand(w)
    tm = min(tm, t)

    def fn(s, x, gt, lng, lnb, w):
        y = jnp.dot(s, _rows_weight(w), preferred_element_type=F32)
        return _residual_ln(y, x, gt, lng, lnb), y

    return _blocked(name, fn, (t // tm,), [s, x, gt, lng, lnb, w],
                    [_tok(tm, f), _tok(tm, d), _res(gt.shape), _res(lng.shape), _res(lnb.shape), w_spec],
                    [_sds((t, d)), _sds((t, d))], [_tok(tm, d)] * 2)


def _ffn_down_bwd(name, dx2, x, y, gt, lng, lnb, w, tm):
    t, d = x.shape
    w, w_spec, w_blk = _weight_operand(w)
    f = math.prod(w_blk[:-1])
    tm = min(tm, t)

    def fn(dx2, x, y, gt, lng, lnb, w):
        _, vjp2 = jax.vjp(_residual_ln, y, x, gt, lng, lnb)
        dy, dx, dgt, dlng, dlnb = vjp2(dx2)
        dyb = dy.astype(MXU_DTYPE)
        return _mm_nt(dyb, _rows_weight(w)), dx, dyb, dgt, dlng, dlnb

    return _blocked(name, fn, (t // tm,), [dx2, x, y, gt, lng, lnb, w],
                    [_tok(tm, d), _tok(tm, d), _tok(tm, d), _res(gt.shape), _res(lng.shape), _res(lnb.shape), w_spec],
                    [_sds((t, f), MXU_DTYPE), _sds((t, d)), _sds((t, d), MXU_DTYPE), _sds(gt.shape), _sds(lng.shape), _sds(lnb.shape)],
                    [_tok(tm, f), _tok(tm, d), _tok(tm, d), _res(gt.shape), _res(lng.shape), _res(lnb.shape)],
                    acc=(3, 4, 5), acc_axis=0)


def _loss_and_grad(x4, target, tm):
    t, d = x4.shape
    tm = min(tm, t)

    def fn(y, tg):
        e = y - tg
        part = 0.5 * jnp.sum(jnp.mean(e * e, axis=-1, keepdims=True), axis=0, keepdims=True)
        return jnp.broadcast_to(part, (1, LANES)), e * (1.0 / d)

    return _blocked("loss", fn, (t // tm,), [x4, target], [_tok(tm, d), _tok(tm, d)],
                    [_sds((1, LANES)), _sds((t, d))], [_res((1, LANES)), _tok(tm, d)], acc=(0,), acc_axis=0)


def _matmul_tn(name, pairs, chip_cols=None):
    k = pairs[0][0].shape[1]
    n = pairs[0][1].shape[1]
    tk = k if k <= 1024 else k // 2
    if chip_cols is None:
        tn = 512 if n % 512 == 0 else (384 if n % 384 == 0 else 128)
        out_spec, out_shape = pl.BlockSpec((tk, tn), lambda i, j: (i, j)), _sds((k, n), MXU_DTYPE)
    else:
        tn = chip_cols
        tk = min(tk, 512) if tn > 1024 else tk
        out_spec = pl.BlockSpec((None, tk, tn), lambda i, j: (j, i, 0))
        out_shape = _sds((n // chip_cols, k, chip_cols), MXU_DTYPE)

    def body(*refs):
        acc = None
        for p in range(len(pairs)):
            term = lax.dot_general(refs[2 * p][...], refs[2 * p + 1][...], (((0,), (0,)), ((), ())), preferred_element_type=F32)
            acc = term if acc is None else acc + term
        refs[-1][...] = acc.astype(refs[-1].dtype)

    in_specs, args = [], []
    for a, b in pairs:
        t = a.shape[0]
        in_specs += [pl.BlockSpec((t, tk), lambda i, j: (0, i)), pl.BlockSpec((t, tn), lambda i, j: (0, j))]
        args += [a, b]
    return pl.pallas_call(
        body, name=name, grid=(k // tk, n // tn), in_specs=in_specs,
        out_specs=out_spec, out_shape=out_shape,
        compiler_params=pltpu.CompilerParams(dimension_semantics=("arbitrary", "arbitrary"),
                                             vmem_limit_bytes=VMEM_LIMIT_BYTES),
    )(*args)


def _chan(t, col_of):
    return pl.BlockSpec((t, LANES), lambda i: (0, col_of(i)))


def _wblk(k, col_of=lambda i: i):
    return pl.BlockSpec((k, LANES), lambda i: (0, col_of(i)))


SEQ_TILE = 256
GRID_SEQ_TILE = 512


def _load_ext(ref, r0, rows, halo, total):
    lo, hi = max(r0 - halo, 0), min(r0 + rows + halo, total)
    parts = []
    if lo > r0 - halo:
        parts.append(jnp.zeros((lo - (r0 - halo), ref.shape[1]), F32))
    parts.append(ref[lo:hi, :].astype(F32))
    if hi < r0 + rows + halo:
        parts.append(jnp.zeros((r0 + rows + halo - hi, ref.shape[1]), F32))
    return parts[0] if len(parts) == 1 else jnp.concatenate(parts, axis=0)


def _seq_stage(name, tile_fn, halo, grid_n, seqs, pars, outs, tile_rows=SEQ_TILE):
    total = seqs[0][0].shape[0]
    rows = min(tile_rows, total)
    ns, npar = len(seqs), len(pars)

    def body(*refs):
        seq_refs, par_refs, out_refs = refs[:ns], refs[ns:ns + npar], refs[ns + npar:]
        par_vals = [r[...] for r in par_refs]
        for r0 in range(0, total, rows):
            exts = [_load_ext(r, r0, rows, halo, total) for r in seq_refs]
            res = tile_fn(r0, *exts, *par_vals)
            for o_ref, v in zip(out_refs, res):
                o_ref[r0:r0 + rows, :] = v.astype(o_ref.dtype)

    return pl.pallas_call(
        body, name=name, grid=(grid_n,), in_specs=[s for _, s in seqs] + [s for _, s in pars],
        out_specs=[s for _, s in outs], out_shape=[o for o, _ in outs],
        compiler_params=pltpu.CompilerParams(dimension_semantics=("arbitrary",), vmem_limit_bytes=VMEM_LIMIT_BYTES),
    )(*[a for a, _ in seqs], *[a for a, _ in pars])


def _seq_stage_bwd(name, tile_fn, halo, grid_n, douts, seqs, pars, dseq_outs, dpar_outs, tile_rows=SEQ_TILE, n_const=0):
    total = seqs[0][0].shape[0]
    rows = min(tile_rows, total)
    groups = [d if isinstance(d, list) else [d] for d in douts]
    douts = [d for g in groups for d in g]
    starts = [sum(len(g) for g in groups[:k]) for k in range(len(groups))]
    nd, ns, npar = len(douts), len(seqs), len(pars)
    nsd = ns - n_const
    widths = [s.block_shape[-1] for _, s in seqs[:nsd]]

    def dtile(dout_refs, k, r0):
        terms = [d[r0:r0 + rows, :].astype(F32) for d in dout_refs[starts[k]:starts[k] + len(groups[k])]]
        return functools.reduce(lambda a, b: a + b, terms)

    def body(*refs):
        dout_refs, seq_refs, par_refs = refs[:nd], refs[nd:nd + ns], refs[nd + ns:nd + ns + npar]
        n_in = nd + ns + npar
        dseq_refs = refs[n_in:n_in + nsd]
        dpar_refs = refs[n_in + nsd:n_in + nsd + npar]
        accs = refs[n_in + nsd + npar:]
        par_vals = [r[...] for r in par_refs]
        for a in accs:
            for r0 in range(0, total, rows):
                a[r0:r0 + rows, :] = jnp.zeros((rows, a.shape[1]), F32)
        dpars = [jnp.zeros(p.shape, F32) for p in par_vals]
        for r0 in range(0, total, rows):
            exts = [_load_ext(r, r0, rows, halo, total) for r in seq_refs]
            saved = exts[nsd:]
            _, vjp = jax.vjp(lambda *a: tile_fn(r0, *a[:nsd], *saved, *a[nsd:]), *exts[:nsd], *par_vals)
            cts = vjp(tuple(dtile(dout_refs, k, r0) for k in range(len(groups))))
            lo, hi = max(r0 - halo, 0), min(r0 + rows + halo, total)
            for a, dx in zip(accs, cts[:nsd]):
                a[lo:hi, :] += dx[lo - (r0 - halo):hi - (r0 - halo)]
            dpars = [acc + g for acc, g in zip(dpars, cts[nsd:])]
        for o_ref, a in zip(dseq_refs, accs):
            for r0 in range(0, total, rows):
                o_ref[r0:r0 + rows, :] = a[r0:r0 + rows, :].astype(o_ref.dtype)
        for o_ref, g in zip(dpar_refs, dpars):
            o_ref[...] = g

    return pl.pallas_call(
        body, name=name, grid=(grid_n,),
        in_specs=[s for _, s in douts] + [s for _, s in seqs] + [s for _, s in pars],
        out_specs=[s for _, s in dseq_outs] + [s for _, s in dpar_outs],
        out_shape=[o for o, _ in dseq_outs] + [o for o, _ in dpar_outs],
        scratch_shapes=[pltpu.VMEM((total, w), F32) for w in widths],
        compiler_params=pltpu.CompilerParams(dimension_semantics=("arbitrary",), vmem_limit_bytes=VMEM_LIMIT_BYTES),
    )(*[a for a, _ in douts], *[a for a, _ in seqs], *[a for a, _ in pars])


def _qkv_tile(r0, p_ext, w):
    kind = pl.program_id(0)
    a = _silu(_conv5(p_ext, w))
    nrm = a * lax.rsqrt(jnp.sum(a * a, -1, keepdims=True) + RMS_EPS)
    scale = jnp.where(kind < GDN_HEADS, HEAD_DIM ** -0.5, 1.0).astype(F32)
    sel = (kind < 2 * GDN_HEADS).astype(F32)
    return (sel * (nrm * scale) + (1.0 - sel) * a,)


def _qkv_conv(name, p, w):
    t = p.shape[0]
    nb = w.shape[1] // LANES
    ident = lambda i: i
    return _seq_stage(name, _qkv_tile, HALO_SHORT, nb, [(p, _chan(t, ident))], [(w, _wblk(GDN_CONV))],
                      [(_sds((t, nb * LANES)), _chan(t, ident))])[0]


def _qkv_conv_bwd(name, dqkv0, dqkv1, p, w):
    t = p.shape[0]
    nb = w.shape[1] // LANES
    ident = lambda i: i
    return _seq_stage_bwd(name, _qkv_tile, HALO_SHORT, nb, [[(dqkv0, _chan(t, ident)), (dqkv1, _chan(t, ident))]],
                          [(p, _chan(t, ident))],
                          [(w, _wblk(GDN_CONV))], [(_sds((t, nb * LANES), MXU_DTYPE), _chan(t, ident))],
                          [(_sds(w.shape), _wblk(GDN_CONV))])


def _gates_fn(s, avec, dtvec):
    lane = lax.broadcasted_iota(jnp.int32, s.shape, 1)
    beta = jax.nn.sigmoid(s)
    g = -jnp.exp(avec) * jax.nn.softplus(s + dtvec)
    nh = 2 * GDN_HEADS
    return jnp.where(lane < nh, beta, jnp.where(lane < 2 * nh, g, 0.0))


def _gates(name, p, avec, dtvec, col):
    t = p.shape[0]
    tm = min(512, t)
    return _blocked(name, _gates_fn, (t // tm,), [p, avec, dtvec], [_tok(tm, LANES, col), _res(avec.shape), _res(dtvec.shape)],
                    [_sds((t, LANES))], [_tok(tm, LANES)])[0]


def _gates_bwd(name, dgb, p, avec, dtvec, col):
    t = p.shape[0]
    tm = min(512, t)

    def fn(dgb, s, avec, dtvec):
        _, vjp = jax.vjp(_gates_fn, s, avec, dtvec)
        return vjp(dgb)

    return _blocked(name, fn, (t // tm,), [dgb, p, avec, dtvec],
                    [_tok(tm, LANES), _tok(tm, LANES, col), _res(avec.shape), _res(dtvec.shape)],
                    [_sds((t, LANES), MXU_DTYPE), _sds(avec.shape), _sds(dtvec.shape)],
                    [_tok(tm, LANES), _res(avec.shape), _res(dtvec.shape)], acc=(1, 2), acc_axis=0)


def _make_pool_tile(total):
    def tile(r0, x_ext, scale, *ws):
        ys = []
        for g, win in enumerate(POOL_WINDOWS):
            xg = x_ext[:, g * LANES:(g + 1) * LANES]
            rows = xg.shape[0] - 2 * HALO_SHORT
            pos = r0 + lax.broadcasted_iota(jnp.int32, (rows, LANES), 0)
            lo = jnp.clip(pos - win // 2, 0, total)
            hi = jnp.clip(pos - win // 2 + win, 0, total)
            window_sum = _pool_sums[win](xg, jnp.ones((win, LANES), F32))
            pooled = window_sum / (hi - lo).astype(F32) - _center(xg, HALO_SHORT)
            ys.append(_mm(pooled, ws[g]))
        return (jnp.concatenate(ys, axis=-1) * scale,)

    return tile


def _pool_specs(p, pool_w, pool_scale, col0):
    t = p.shape[0]
    n = len(POOL_WINDOWS) * LANES
    seq = (p, pl.BlockSpec((t, n), lambda i: (0, col0)))
    pars = [(pool_scale, _res(pool_scale.shape))]
    pars += [(pool_w, pl.BlockSpec((None, LANES, LANES), lambda i, g=g: (g, 0, 0))) for g in range(len(POOL_WINDOWS))]
    return t, n, seq, pars


def _pool(p, pool_w, pool_scale, col0):
    t, n, seq, pars = _pool_specs(p, pool_w, pool_scale, col0)
    return _seq_stage("pool", _make_pool_tile(t), HALO_SHORT, 1, [seq], pars, [(_sds((t, n)), _res((t, n)))])[0]


def _pool_bwd(dy, p, pool_w, pool_scale, col0):
    t, n, seq, pars = _pool_specs(p, pool_w, pool_scale, col0)
    wspec = (_sds((LANES, LANES)), _res((LANES, LANES)))
    return _seq_stage_bwd("pool_bwd", _make_pool_tile(t), HALO_SHORT, 1, [(dy, _res(dy.shape))], [seq], pars,
                          [(_sds((t, n), MXU_DTYPE), _res((t, n)))],
                          [(_sds(pool_scale.shape), _res(pool_scale.shape))] + [wspec] * len(POOL_WINDOWS))


def _ffn_conv_tile(r0, a_ext, gt_ext, w):
    cv = _conv3x3(a_ext, w)
    return _silu(cv) * _center(gt_ext, HALO_GRID), cv


def _ffn_conv_known_tile(r0, a_ext, gt_ext, cv_ext, w):
    cv = _conv3x3_known(a_ext, w, _center(cv_ext, HALO_GRID))
    return (_silu(cv) * _center(gt_ext, HALO_GRID),)


def _ffn_conv(name, h, w9):
    t = h.shape[0]
    nb = w9.shape[1] // LANES
    ident = lambda i: i
    return _seq_stage(name, _ffn_conv_tile, HALO_GRID, nb, [(h, _chan(t, ident)), (h, _chan(t, lambda i: nb + i))],
                      [(w9, _wblk(9))], [(_sds((t, nb * LANES), MXU_DTYPE), _chan(t, ident)), (_sds((t, nb * LANES)), _chan(t, ident))],
                      GRID_SEQ_TILE)


def _ffn_conv_bwd(name, ds, h, cv, w9):
    t = h.shape[0]
    nb = w9.shape[1] // LANES
    ident = lambda i: i
    o = (_sds((t, nb * LANES), MXU_DTYPE), _chan(t, ident))
    return _seq_stage_bwd(name, _ffn_conv_known_tile, HALO_GRID, nb, [(ds, _chan(t, ident))],
                          [(h, _chan(t, ident)), (h, _chan(t, lambda i: nb + i)), (cv, _chan(t, ident))], [(w9, _wblk(9))],
                          [o, o], [(_sds(w9.shape), _wblk(9))], GRID_SEQ_TILE, n_const=1)


def _sconv_tile(r0, gb_ext, gc_ext, h_ext, w):
    return (_center(gb_ext, HALO_SHORT) * _conv3(gc_ext * h_ext, w),)


def _sconv_specs(p1, w3):
    t = p1.shape[0]
    nb = w3.shape[1] // LANES
    seqs = [(p1, _chan(t, lambda i: i)), (p1, _chan(t, lambda i: nb + i)), (p1, _chan(t, lambda i: 2 * nb + i))]
    return t, nb, seqs, [(w3, _wblk(SC_WIDTH))]


def _sconv(p1, w3):
    t, nb, seqs, pars = _sconv_specs(p1, w3)
    return _seq_stage("sconv", _sconv_tile, HALO_SHORT, nb, seqs, pars, [(_sds((t, nb * LANES)), _chan(t, lambda i: i))])[0]


def _sconv_bwd(dysc, p1, w3):
    t, nb, seqs, pars = _sconv_specs(p1, w3)
    o = (_sds((t, nb * LANES), MXU_DTYPE), _chan(t, lambda i: i))
    return _seq_stage_bwd("sconv_bwd", _sconv_tile, HALO_SHORT, nb, [(dysc, _chan(t, lambda i: i))], seqs, pars,
                          [o, o, o], [(_sds(w3.shape), _wblk(SC_WIDTH))])


def _conf_tile(r0, ga_ext, gb_ext, w):
    return (_conv31(ga_ext * jax.nn.sigmoid(gb_ext), w),)


def _conf_specs(p1, w31, blk0):
    t = p1.shape[0]
    nb = w31.shape[1] // LANES
    seqs = [(p1, _chan(t, lambda i: blk0 + i)), (p1, _chan(t, lambda i: blk0 + nb + i))]
    return t, nb, seqs, [(w31, _wblk(CF_WIDTH))]


def _confconv(p1, w31, blk0):
    t, nb, seqs, pars = _conf_specs(p1, w31, blk0)
    return _seq_stage("confconv", _conf_tile, HALO_CONF, nb, seqs, pars, [(_sds((t, nb * LANES)), _chan(t, lambda i: i))])[0]


def _confconv_bwd(dz0, p1, w31, blk0):
    t, nb, seqs, pars = _conf_specs(p1, w31, blk0)
    o = (_sds((t, nb * LANES), MXU_DTYPE), _chan(t, lambda i: i))
    return _seq_stage_bwd("confconv_bwd", _conf_tile, HALO_CONF, nb, [(dz0, _chan(t, lambda i: i))], seqs, pars,
                          [o, o], [(_sds(w31.shape), _wblk(CF_WIDTH))])


def _split_bf16(x):
    head = x.astype(jnp.bfloat16)
    return head, (x - head.astype(F32)).astype(jnp.bfloat16)


def _bdot(a, b, ca, cb, hi):
    dims = (((ca,), (cb,)), ((0,), (0,)))
    if not hi:
        return lax.dot_general(a.astype(MXU_DTYPE), b.astype(MXU_DTYPE), dims, preferred_element_type=F32)
    if MXU_DTYPE == F32:
        return lax.dot_general(a, b, dims, preferred_element_type=F32, precision=HIGHEST)
    (ah, al), (bh, bl) = _split_bf16(a), _split_bf16(b)
    dot = functools.partial(lax.dot_general, dimension_numbers=dims, preferred_element_type=F32)
    return dot(ah, bh) + (dot(ah, bl) + dot(al, bh))


def _make_bmm(hi):
    nn_i = lambda a, b: _bdot(a, b, 2, 1, hi)
    nt_i = lambda a, b: _bdot(a, b, 2, 2, hi)
    tn_i = lambda a, b: _bdot(a, b, 1, 1, hi)
    nn = _mm_vjp(nn_i, lambda g, b: nt_i(g, b), lambda a, g: tn_i(a, g))
    nt = _mm_vjp(nt_i, lambda g, b: nn_i(g, b), lambda a, g: tn_i(g, a))
    tn = _mm_vjp(tn_i, lambda g, b: nt_i(b, g), lambda a, g: nn_i(a, g))
    return nn, nt, tn


_bmm, _bmm_nt, _bmm_tn = _make_bmm(False)
_bmm_hi, _bmm_hi_nt, _bmm_hi_tn = _make_bmm(True)


def _unit_tri_inverse(a):
    c = a.shape[-1]
    eye = (lax.broadcasted_iota(jnp.int32, a.shape, 1) == lax.broadcasted_iota(jnp.int32, a.shape, 2)).astype(F32)
    levels = max(1, int(math.ceil(math.log2(c))) - 1)
    p = eye - a
    m = _bdot(a, a, 2, 1, True)
    for level in range(levels):
        p_next = p + _bdot(p, m, 2, 1, True)
        if level + 1 < levels:
            m = _bdot(m, m, 2, 1, True)
        p = p_next
    return p


@jax.custom_vjp
def _known_inverse(a, tinv):
    return tinv


def _known_inverse_fwd(a, tinv):
    return tinv, tinv


def _known_inverse_bwd(tinv, dt):
    da = -_bdot(_bdot(tinv, dt, 1, 1, True), tinv, 2, 2, True)
    return da, jnp.zeros_like(tinv)


_known_inverse.defvjp(_known_inverse_fwd, _known_inverse_bwd)


def _gdn_chunk(q, k, v, gcol, grow, bcol, s, tinv=None):
    n, c, _ = q.shape
    shape = (n, c, c)
    fwd_dir = lax.broadcasted_iota(jnp.int32, shape, 0) < n // 2
    i = lax.broadcasted_iota(jnp.int32, shape, 1)
    j = lax.broadcasted_iota(jnp.int32, shape, 2)
    order = jnp.where(fwd_dir, i - j, j - i)
    incl = order >= 0
    gc_col = jnp.sum(jnp.where(incl, grow, 0.0), axis=2, keepdims=True)
    gc_row = jnp.sum(jnp.where(order <= 0, gcol, 0.0), axis=1, keepdims=True)
    gtot = jnp.sum(grow, axis=2, keepdims=True)
    decay = jnp.exp(jnp.where(incl, gc_col - gc_row, -1e30))
    kb = k * bcol
    a = jnp.where(order > 0, _bmm_nt(kb, k) * decay, 0.0)
    tinv = _unit_tri_inverse(a) if tinv is None else _known_inverse(a, tinv)
    e_col = jnp.exp(gc_col)
    u = _bmm_hi(tinv, v * bcol)
    w = _bmm_hi(tinv, kb * e_col)
    k_dec = k * jnp.exp(gtot - gc_col)
    v_new = u - _bmm(w, s)
    attn = jnp.where(incl, _bmm_nt(q, k) * decay, 0.0)
    o = _bmm(q * e_col, s) + _bmm(attn, v_new)
    s_next = s * jnp.exp(gtot) + _bmm_tn(k_dec, v_new)
    return o, s_next, tinv


N_CHAINS = 2 * GDN_HEADS
QKV_W = 3 * GDN_HEADS * HEAD_DIM


def _scan_specs(nc, step):
    c = GDN_CHUNK
    seq = [pl.BlockSpec((c, QKV_W), lambda i: (step(i), 0)), pl.BlockSpec((c, QKV_W), lambda i: (nc - 1 - step(i), 0))]
    heads = [pl.BlockSpec((c, GDN_HEADS * HEAD_DIM), lambda i: (step(i), 0)),
             pl.BlockSpec((c, GDN_HEADS * HEAD_DIM), lambda i: (nc - 1 - step(i), 0))]
    colv = pl.BlockSpec((N_CHAINS, None, c, 1), lambda i: (0, step(i), 0, 0))
    rowv = pl.BlockSpec((N_CHAINS, None, 1, c), lambda i: (0, step(i), 0, 0))
    state = pl.BlockSpec((N_CHAINS, HEAD_DIM, HEAD_DIM), lambda i: (0, 0, 0))
    saved = pl.BlockSpec((N_CHAINS, None, HEAD_DIM, HEAD_DIM), lambda i: (0, step(i), 0, 0))
    inv = pl.BlockSpec((N_CHAINS, None, c, c), lambda i: (0, step(i), 0, 0))
    return seq, heads, colv, rowv, state, saved, inv


def _head_cols(h, part):
    lo = (part * GDN_HEADS + h) * HEAD_DIM
    return slice(lo, lo + HEAD_DIM)


def _chain_stack(x0_ref, x1_ref, part):
    return jnp.stack([(x0_ref if n < GDN_HEADS else x1_ref)[:, _head_cols(n % GDN_HEADS, part)] for n in range(N_CHAINS)])


def _scan_fwd(name, qkv, bcol, gcol, grow, s0):
    t = qkv.shape[0]
    nc = t // GDN_CHUNK
    seq, heads, colv, rowv, state, saved, inv = _scan_specs(nc, lambda i: i)

    def body(qkv0_ref, qkv1_ref, b_ref, gc_ref, gr_ref, s0_ref, o0_ref, o1_ref, save_ref, tinv_ref, fin_ref, s_ref):
        @pl.when(pl.program_id(0) == 0)
        def _():
            s_ref[...] = s0_ref[...]

        s = s_ref[...]
        save_ref[...] = s
        q, k, v = [_chain_stack(qkv0_ref, qkv1_ref, part) for part in range(3)]
        o, s_next, tinv = _gdn_chunk(q, k, v, gc_ref[...], gr_ref[...], b_ref[...], s)
        for n in range(N_CHAINS):
            d, h = divmod(n, GDN_HEADS)
            (o0_ref if d == 0 else o1_ref)[:, _head_cols(h, 0)] = o[n]
        tinv_ref[...] = tinv
        s_ref[...] = s_next
        fin_ref[...] = s_next

    hw = GDN_HEADS * HEAD_DIM
    return pl.pallas_call(
        body, name=name, grid=(nc,), in_specs=seq + [colv, colv, rowv, state],
        out_specs=heads + [saved, inv, state],
        out_shape=[_sds((t, hw)), _sds((t, hw)), _sds((N_CHAINS, nc, HEAD_DIM, HEAD_DIM)),
                   _sds((N_CHAINS, nc, GDN_CHUNK, GDN_CHUNK)), _sds((N_CHAINS, HEAD_DIM, HEAD_DIM))],
        scratch_shapes=[pltpu.VMEM((N_CHAINS, HEAD_DIM, HEAD_DIM), F32)],
        compiler_params=pltpu.CompilerParams(dimension_semantics=("arbitrary",), vmem_limit_bytes=VMEM_LIMIT_BYTES),
    )(qkv, qkv, bcol, gcol, grow, s0)


def _scan_bwd(name, do, qkv, bcol, gcol, grow, saved_s, saved_inv, ds_last):
    t = qkv.shape[0]
    nc = t // GDN_CHUNK
    c = GDN_CHUNK
    seq, heads, colv, rowv, state, saved, inv = _scan_specs(nc, lambda i: nc - 1 - i)

    def body(do0_ref, do1_ref, qkv0_ref, qkv1_ref, b_ref, gc_ref, gr_ref, s_ref, tinv_ref, dsl_ref,
             dx0_ref, dx1_ref, db_ref, dgc_ref, dgr_ref, ds0_ref, ds_ref):
        @pl.when(pl.program_id(0) == 0)
        def _():
            ds_ref[...] = dsl_ref[...]

        tinv = tinv_ref[...]
        q, k, v = [_chain_stack(qkv0_ref, qkv1_ref, part) for part in range(3)]
        do = jnp.stack([(do0_ref if n < GDN_HEADS else do1_ref)[:, _head_cols(n % GDN_HEADS, 0)] for n in range(N_CHAINS)])
        _, vjp = jax.vjp(lambda *a: _gdn_chunk(*a, tinv=tinv)[:2], q, k, v, gc_ref[...], gr_ref[...], b_ref[...], s_ref[...])
        dq, dk, dv, dgc, dgr, db, ds = vjp((do, ds_ref[...]))
        for n in range(N_CHAINS):
            d, h = divmod(n, GDN_HEADS)
            dx_ref = dx0_ref if d == 0 else dx1_ref
            dx_ref[:, _head_cols(h, 0)] = dq[n]
            dx_ref[:, _head_cols(h, 1)] = dk[n]
            dx_ref[:, _head_cols(h, 2)] = dv[n]
        db_ref[...] = db
        dgc_ref[...] = dgc
        dgr_ref[...] = dgr
        ds_ref[...] = ds
        ds0_ref[...] = ds

    vec_c = _sds((N_CHAINS, nc, c, 1))
    vec_r = _sds((N_CHAINS, nc, 1, c))
    return pl.pallas_call(
        body, name=name, grid=(nc,),
        in_specs=heads + seq + [colv, colv, rowv, saved, inv, state],
        out_specs=seq + [colv, colv, rowv, state],
        out_shape=[_sds((t, QKV_W)), _sds((t, QKV_W)), vec_c, vec_c, vec_r, _sds((N_CHAINS, HEAD_DIM, HEAD_DIM))],
        scratch_shapes=[pltpu.VMEM((N_CHAINS, HEAD_DIM, HEAD_DIM), F32)],
        compiler_params=pltpu.CompilerParams(dimension_semantics=("arbitrary",), vmem_limit_bytes=VMEM_LIMIT_BYTES),
    )(do, do, qkv, qkv, bcol, gcol, grow, saved_s, saved_inv, ds_last)


def _gate_layouts(gb):
    t = gb.shape[0]
    nc = t // GDN_CHUNK
    nh = GDN_HEADS

    def by_scan_position(a):
        a = a.T.reshape(2 * nh, nc, GDN_CHUNK)
        return jnp.concatenate([a[:nh], a[nh:, ::-1]], axis=0)

    beta = by_scan_position(gb[:, :2 * nh])
    g = by_scan_position(gb[:, 2 * nh:4 * nh])
    return beta[..., None], g[..., None], g[:, :, None, :]


def _gate_layouts_bwd(dbcol, dgcol, dgrow):
    n2, nc, c, _ = dbcol.shape
    nh = n2 // 2
    t = nc * c

    def by_token(a):
        return jnp.concatenate([a[:nh], a[nh:, ::-1]], axis=0).reshape(n2, t).T

    dbeta = by_token(dbcol[..., 0])
    dg = by_token(dgcol[..., 0] + dgrow[:, :, 0, :])
    return jnp.concatenate([dbeta, dg, jnp.zeros((t, LANES - 2 * n2), F32)], axis=1)


def kernel(x, c, ctx, c_ctx, ada_w, ada_b, ln_g, ln_b, even_w_in, even_w_out, gdn_conv_w, gdn_a_log, gdn_dt_bias, gdn_norm_w, pool_w, pool_scale, odd_w_in, odd_w_out, sconv_w, conf_conv_w, conf_ln_g, conf_ln_b, ffn_w_up, ffn_conv_w, ffn_w_down, loss_target, m_c_ctx, m_ada_w, m_ada_b, m_ln_g, m_ln_b, m_even_w_in, m_even_w_out, m_gdn_conv_w, m_gdn_a_log, m_gdn_dt_bias, m_gdn_norm_w, m_pool_w, m_pool_scale, m_odd_w_in, m_odd_w_out, m_sconv_w, m_conf_conv_w, m_conf_ln_g, m_conf_ln_b, m_ffn_w_up, m_ffn_conv_w, m_ffn_w_down, v_c_ctx, v_ada_w, v_ada_b, v_ln_g, v_ln_b, v_even_w_in, v_even_w_out, v_gdn_conv_w, v_gdn_a_log, v_gdn_dt_bias, v_gdn_norm_w, v_pool_w, v_pool_scale, v_odd_w_in, v_odd_w_out, v_sconv_w, v_conf_conv_w, v_conf_ln_g, v_conf_ln_b, v_ffn_w_up, v_ffn_conv_w, v_ffn_w_down):
    names = ['c_ctx', 'ada_w', 'ada_b', 'ln_g', 'ln_b', 'even_w_in', 'even_w_out', 'gdn_conv_w', 'gdn_a_log',
             'gdn_dt_bias', 'gdn_norm_w', 'pool_w', 'pool_scale', 'odd_w_in', 'odd_w_out', 'sconv_w', 'conf_conv_w',
             'conf_ln_g', 'conf_ln_b', 'ffn_w_up', 'ffn_conv_w', 'ffn_w_down']
    loc = locals()
    wts = {n: loc[n] for n in names}
    mom = {n: loc['m_' + n] for n in names}
    var = {n: loc['v_' + n] for n in names}

    ix, iy, ic = lax.axis_index("x"), lax.axis_index("y"), lax.axis_index("c")
    chip = 2 * ix + iy
    dev = 2 * chip + ic
    d = D_MODEL
    x0, ctx0, tgt = x[0], ctx[0], loss_target[0]
    t, tc = x0.shape[0], ctx0.shape[0]
    depth = ada_w.shape[0]
    n_ada = ada_w.shape[2]

    small_sharded = [gdn_conv_w, sconv_w, conf_conv_w, ffn_conv_w, ln_g, ln_b]
    s1_items = [c] + small_sharded
    s1 = _pack_rows(s1_items)
    g1 = _all_gather8("gather_small_in", s1, True).reshape(8, -1, LANES)
    c_all = g1[:, :_packed_rows(c.shape)].reshape(8, d)
    per_chip = [_unpack_rows(g1[2 * k], [a.shape for a in s1_items])[1:] for k in range(4)]
    gdn_conv_f, sconv_f, conf_conv_f, ffn_conv_f, ln_g_f, ln_b_f = [
        jnp.concatenate([per_chip[k][i] for k in range(4)], axis=-1) for i in range(len(small_sharded))]
    c16 = jnp.concatenate([c_all, c_ctx[None], jnp.zeros((7, d), F32)], axis=0)

    mod_part = _mod_rows(c16, ada_w)
    g2 = _all_gather8("gather_mod", mod_part.reshape(-1, LANES), True).reshape(4, 2, depth, 16, n_ada)[:, 0]
    mod_all = jnp.transpose(g2, (1, 2, 0, 3)).reshape(depth, 16, 4 * n_ada) + ada_b[:, None, :]
    mod_me = lax.dynamic_index_in_dim(mod_all, dev, axis=1, keepdims=False).reshape(depth, 6, 1, d)
    sh_c, sc_c = mod_all[0, 8, :d][None], mod_all[0, 8, d:2 * d][None]

    flat_names = ['even_w_in', 'even_w_out']
    flat_shapes = [wts[n].shape for n in flat_names]
    half_mult = 2 * 16 * LANES
    rh = -(-sum(math.prod(s) for s in flat_shapes) // half_mult) * half_mult // (2 * LANES)
    flat_pack = _pack_flat([wts[n] for n in flat_names], half_mult, MXU_DTYPE).reshape(2, rh, LANES)
    wg_flat = _all_gather8("gather_weights_even", lax.dynamic_index_in_dim(flat_pack, ic, axis=0, keepdims=False), True).reshape(4, -1)
    per_chip = [_unpack_flat(wg_flat[k], flat_shapes) for k in range(4)]
    n_even = 4 * even_w_in.shape[1]
    n_even_pad = -(-n_even // LANES) * LANES
    w_in = jnp.concatenate([per_chip[k][0] for k in range(4)] + [jnp.zeros((d, n_even_pad - n_even), MXU_DTYPE)], axis=1)
    w_out = jnp.concatenate([per_chip[k][1] for k in range(4)], axis=0)

    def row_half(a, core, axis):
        n = a.shape[axis] // 2
        return lax.dynamic_slice_in_dim(a, core * n, n, axis=axis)

    def layer_of(a, core):
        return lax.dynamic_index_in_dim(a, core, axis=0, keepdims=False)

    mine = [row_half(odd_w_in, ic, 0), row_half(odd_w_out, ic, 0), layer_of(ffn_w_up, ic), layer_of(ffn_w_down, ic)]
    late_sems, late_mine, late_lands, late_token = _chip_send_start(
        "gather_late_start", [m.astype(MXU_DTYPE) for m in mine], [wg_flat, mod_all])
    scal_blk = (n_even // LANES)
    n_scal = n_even - scal_blk * LANES

    def mod(layer, k):
        return mod_me[layer, k]

    avec = jnp.zeros((1, LANES), F32).at[0, n_scal // 2:n_scal].set(gdn_a_log.reshape(-1))
    dtvec = jnp.zeros((1, LANES), F32).at[0, n_scal // 2:n_scal].set(gdn_dt_bias.reshape(-1))
    normw = gdn_norm_w[None]
    pscale = pool_scale[None]
    cg, cb = conf_ln_g[None], conf_ln_b[None]
    lng = lambda l, k: ln_g_f[l, k][None]
    lnb = lambda l, k: ln_b_f[l, k][None]
    convw9 = ffn_conv_f.reshape(depth, 9, -1)
    nqkv = gdn_conv_f.shape[1]

    shift0 = mod(0, 0) + late_token[:1, :1]
    p, ub0 = _inproj("even_in", x0, shift0, mod(0, 1), w_in, 512)
    pc, ucb = _inproj("even_in_ctx", ctx0, sh_c, sc_c, w_in, 256)
    qkv = _qkv_conv("qkv_conv", p, gdn_conv_f)
    qkv_c = _qkv_conv("qkv_conv_ctx", pc, gdn_conv_f)
    gb = _gates("gates", p, avec, dtvec, scal_blk)
    gb_c = _gates("gates_ctx", pc, avec, dtvec, scal_blk)
    lay = _gate_layouts(gb)
    lay_c = _gate_layouts(gb_c)
    s_zero = jnp.zeros((2 * GDN_HEADS, HEAD_DIM, HEAD_DIM), F32)
    _, _, save_c, inv_c, s_ctx = _scan_fwd("scan_ctx", qkv_c, *lay_c, s_zero)
    o0, o1, save_l, inv_l, _ = _scan_fwd("scan", qkv, *lay, s_ctx)
    pool_blk = (nqkv + GDN_HEADS * HEAD_DIM) // 512
    ypool = _pool(p, pool_w, pscale, pool_blk)
    x1, mix0, y0 = _even_out(o0, o1, p, ypool, x0, normw, mod(0, 2), lng(0, 0), lnb(0, 0), w_out, 256)

    late_mine, late_lands = _chip_send_wait("gather_late_wait", late_sems, late_mine, late_lands, x1)
    own4 = [lax.dynamic_update_slice(land, m[None], (chip,) + (0,) * m.ndim) for land, m in zip(late_lands, late_mine)]
    got4 = _sibling_exchange("gather_late_exchange", own4)

    def of_core(i, core):
        return jnp.where(ic == core, own4[i], got4[i])

    w_oin = jnp.concatenate([of_core(0, 0), of_core(0, 1)], axis=1)
    w_oout = jnp.concatenate([of_core(1, 0), of_core(1, 1)], axis=1).reshape(-1, d)
    w_up = [of_core(2, l) for l in range(depth)]
    w_down = [of_core(3, l) for l in range(depth)]

    def ffn_fwd(l, xin):
        h, ub = _inproj(f"ffn_up{l}", xin, mod(l, 3), mod(l, 4), w_up[l], 256)
        s, cv = _ffn_conv(f"ffn_conv{l}", h, convw9[l])
        xo, y = _ffn_down(f"ffn_down{l}", s, xin, mod(l, 5), lng(l, 1), lnb(l, 1), w_down[l], 256)
        return xo, (h, cv, ub, s, y)

    x2, ffn0 = ffn_fwd(0, x1)
    p1, ub1 = _inproj("odd_in", x2, mod(1, 0), mod(1, 1), w_oin, 512)
    nsc = sconv_f.shape[1] // LANES
    ysc = _sconv(p1, sconv_f)
    z0 = _confconv(p1, conf_conv_f, 3 * nsc)
    x3, mix1, y1 = _odd_out(ysc, z0, x2, cg, cb, mod(1, 2), lng(1, 0), lnb(1, 0), w_oout, 256)
    x4, ffn1 = ffn_fwd(1, x3)
    loss_part, dx4 = _loss_and_grad(x4, tgt, 512)
    loss = lax.psum(loss_part[0, 0], ("x", "y", "c"))

    dmod = [[None] * 6 for _ in range(depth)]
    dlng = [[None, None] for _ in range(depth)]
    dlnb = [[None, None] for _ in range(depth)]
    gbig = {}
    dconv9 = [None] * depth

    def ffn_bwd(l, dxo, xin, saved):
        h, cv, ub, s, y = saved
        ds, dxa, dyb, dgt, dg_, db_ = _ffn_down_bwd(f"ffn_down_bwd{l}", dxo, xin, y, mod(l, 5), lng(l, 1), lnb(l, 1), w_down[l], 256)
        da, dgate, dw9 = _ffn_conv_bwd(f"ffn_conv_bwd{l}", ds, h, cv, convw9[l])
        dh = jnp.concatenate([da, dgate], axis=1)
        dxin, dsh, dsc = _inproj_bwd(f"ffn_up_bwd{l}", dh, xin, mod(l, 3), mod(l, 4), w_up[l], dxa, 256)
        dmod[l][3], dmod[l][4], dmod[l][5] = dsh, dsc, dgt
        dlng[l][1], dlnb[l][1] = dg_, db_
        dconv9[l] = dw9
        dw_up = _matmul_tn(f"dw_up{l}", [(ub, dh)], chip_cols=ffn_w_up.shape[2])
        dw_down = _matmul_tn(f"dw_down{l}", [(s, dyb)]).reshape(4, -1, d)
        return dxin, dw_up, dw_down

    dx3, dwu1, dwd1 = ffn_bwd(1, dx4, x3, ffn1)
    dysc, dz0, dx2a, dyb1, dcg, dcb, dgt, dg_, db_ = _odd_out_bwd(dx3, ysc, z0, x2, y1, cg, cb, mod(1, 2), lng(1, 0), lnb(1, 0), w_oout, 256)
    dmod[1][2], dlng[1][0], dlnb[1][0] = dgt, dg_, db_
    d_gb, d_gc, d_h, dsconv = _sconv_bwd(dysc, p1, sconv_f)
    d_ga, d_gbb, dconf = _confconv_bwd(dz0, p1, conf_conv_f, 3 * nsc)
    dp1 = jnp.concatenate([d_gb, d_gc, d_h, d_ga, d_gbb], axis=1)
    dx2, dsh, dsc = _inproj_bwd("odd_in_bwd", dp1, x2, mod(1, 0), mod(1, 1), w_oin, dx2a, 512)
    dmod[1][0], dmod[1][1] = dsh, dsc
    dw_oout = _matmul_tn("dw_oout", [(mix1, dyb1)]).reshape(4, -1, d)
    dw_oin = _matmul_tn("dw_oin", [(ub1, dp1)], chip_cols=odd_w_in.shape[1])

    dx1, dwu0, dwd0 = ffn_bwd(0, dx2, x1, ffn0)

    tiled_halves = [
        (dw_oin[:, :d // 2], dw_oin[:, d // 2:]),
        (dw_oout[:, :odd_w_out.shape[0] // 2], dw_oout[:, odd_w_out.shape[0] // 2:]),
        (dwu0, dwu1),
        (dwd0, dwd1),
    ]
    keeps = [jnp.where(ic == 0, h0, h1) for h0, h1 in tiled_halves]
    gives = [jnp.where(ic == 0, h1, h0) for h0, h1 in tiled_halves]
    gots = _sibling_exchange("grad_pair_exchange", gives)
    pairs = [_sum_pair(f"grad_pair_sum{i}", kp, gt_, MXU_DTYPE) for i, (kp, gt_) in enumerate(zip(keeps, gots))]
    gs_sems, gs_pairs, gs_lands, gs_token = _chip_send_start("grad_scatter_start", pairs, [], per_peer=True)
    gate0 = mod(0, 2) + gs_token[:1, :1]

    do, dpg, dypool, dx0a, dyb0, dnormw, dgt, dg_, db_ = _even_out_bwd(dx1, o0, o1, p, ypool, x0, y0, normw, gate0, lng(0, 0), lnb(0, 0), w_out, 256)
    dmod[0][2], dlng[0][0], dlnb[0][0] = dgt, dg_, db_
    pool_cts = _pool_bwd(dypool, p, pool_w, pscale, pool_blk)
    dpp, dpool_scale, dpool_w = pool_cts[0], pool_cts[1], jnp.stack(pool_cts[2:])
    dqkv0, dqkv1, dbcol, dgcol, dgrow, ds0 = _scan_bwd("scan_bwd", do, qkv, *lay, save_l, inv_l, s_zero)
    zero_do = jnp.zeros((tc, GDN_HEADS * HEAD_DIM), F32)
    dqkv0_c, dqkv1_c, dbcol_c, dgcol_c, dgrow_c, _ = _scan_bwd("scan_bwd_ctx", zero_do, qkv_c, *lay_c, save_c, inv_c, ds0)
    dgb = _gate_layouts_bwd(dbcol, dgcol, dgrow)
    dgb_c = _gate_layouts_bwd(dbcol_c, dgcol_c, dgrow_c)
    dps, davec, ddtvec = _gates_bwd("gates_bwd", dgb, p, avec, dtvec, scal_blk)
    dps_c, davec_c, ddtvec_c = _gates_bwd("gates_bwd_ctx", dgb_c, pc, avec, dtvec, scal_blk)
    dpqkv, dconv5 = _qkv_conv_bwd("qkv_conv_bwd", dqkv0, dqkv1, p, gdn_conv_f)
    dpqkv_c, dconv5_c = _qkv_conv_bwd("qkv_conv_bwd_ctx", dqkv0_c, dqkv1_c, pc, gdn_conv_f)
    dp = jnp.concatenate([dpqkv, dpg, dpp, dps], axis=1)
    dpc = jnp.concatenate([dpqkv_c, jnp.zeros((tc, n_even_pad - nqkv - LANES), MXU_DTYPE), dps_c], axis=1)
    grad_x, dsh, dsc = _inproj_bwd("even_in_bwd", dp, x0, mod(0, 0), mod(0, 1), w_in, dx0a, 512)
    dmod[0][0], dmod[0][1] = dsh, dsc
    _, dsh_c, dsc_c = _inproj_bwd("even_in_bwd_ctx", dpc, ctx0, sh_c, sc_c, w_in, None, 256)
    dw_in = _matmul_tn("dw_in", [(ub0, dp), (ucb, dpc)])
    dw_out = _matmul_tn("dw_out", [(mix0, dyb0)])

    gs_pairs, gs_lands = _chip_send_wait("grad_scatter_wait", gs_sems, gs_pairs, gs_lands, dw_in, per_peer=True)
    own4 = [lax.dynamic_update_slice(land, lax.dynamic_index_in_dim(pr, chip, axis=0, keepdims=True), (chip,) + (0,) * (pr.ndim - 1))
            for land, pr in zip(gs_lands, gs_pairs)]
    n_in, n_out = even_w_in.shape[1], even_w_out.shape[0]
    gflat = jnp.stack([_pack_flat([dw_in[:, k * n_in:(k + 1) * n_in], dw_out[k * n_out:(k + 1) * n_out]], half_mult,
                                  MXU_DTYPE).reshape(2, rh, LANES) for k in range(4)], axis=1)
    got_flat, *got4 = _sibling_exchange("grad_late_exchange", [jnp.where(ic == 0, gflat[1], gflat[0])] + own4)
    pair_flat = _sum_pair("grad_pair_sum_flat", jnp.where(ic == 0, gflat[0], gflat[1]), got_flat, MXU_DTYPE)
    sum_flat = _sum_axis1("grad_chip_sum_flat", _chip_scatter("grad_chip_scatter", [pair_flat])[0])

    def both_halves(i):
        mine_sum = _sum_axis1(f"grad_chip_sum{i}_own", own4[i][None])[0]
        sib_sum = _sum_axis1(f"grad_chip_sum{i}_sibling", got4[i][None])[0]
        return jnp.where(ic == 0, jnp.stack([mine_sum, sib_sum]), jnp.stack([sib_sum, mine_sum]))

    g_even_in, g_even_out = _unpack_flat(sum_flat.reshape(-1), flat_shapes)
    g_shards = {'even_w_in': g_even_in, 'even_w_out': g_even_out, 'odd_w_in': both_halves(0).reshape(odd_w_in.shape),
                'odd_w_out': both_halves(1).reshape(odd_w_out.shape), 'ffn_w_up': both_halves(2), 'ffn_w_down': both_halves(3)}

    dmod_rows = jnp.stack([jnp.concatenate(dmod[l], axis=1)[0] for l in range(depth)])
    dmod_c = jnp.concatenate([dsh_c[0], dsc_c[0], jnp.zeros((4 * d,), F32)])
    dmod_c_rows = jnp.stack([dmod_c] + [jnp.zeros_like(dmod_c)] * (depth - 1))
    small_g = {
        'ln_g': jnp.stack([jnp.stack([dlng[l][k][0] for k in range(2)]) for l in range(depth)]),
        'ln_b': jnp.stack([jnp.stack([dlnb[l][k][0] for k in range(2)]) for l in range(depth)]),
        'gdn_conv_w': dconv5 + dconv5_c,
        'gdn_a_log': (davec + davec_c)[0, n_scal // 2:n_scal].reshape(gdn_a_log.shape),
        'gdn_dt_bias': (ddtvec + ddtvec_c)[0, n_scal // 2:n_scal].reshape(gdn_dt_bias.shape),
        'gdn_norm_w': dnormw[0], 'pool_w': dpool_w, 'pool_scale': dpool_scale[0],
        'sconv_w': dsconv, 'conf_conv_w': dconf, 'conf_ln_g': dcg[0], 'conf_ln_b': dcb[0],
        'ffn_conv_w': jnp.stack(dconv9).reshape(depth, 3, 3, -1),
    }
    small_names = list(small_g)
    s3_items = [dmod_rows, dmod_c_rows] + [small_g[n] for n in small_names]
    s3 = _pack_rows(s3_items)
    g3 = _all_gather8("gather_small_grads", s3, True).reshape(8, -1, LANES)
    tot3 = _sum_axis1("small_grad_sum", g3[None])[0]
    tot_items = _unpack_rows(tot3, [a.shape for a in s3_items])
    dmod_sum, dmod_c_sum = tot_items[0], tot_items[1]
    small_tot = dict(zip(small_names, tot_items[2:]))
    grad_ada_b = dmod_sum + dmod_c_sum
    rows_all = g3[:, :_packed_rows(dmod_rows.shape)].reshape(8, depth, 6 * d)
    cols = lax.dynamic_slice_in_dim(rows_all, chip * n_ada, n_ada, axis=2)
    crow = lax.dynamic_slice_in_dim(dmod_c_sum, chip * n_ada, n_ada, axis=1)
    dmod16 = jnp.concatenate([jnp.transpose(cols, (1, 0, 2)), crow[:, None, :], jnp.zeros((depth, 7, n_ada), F32)], axis=1)
    grad_ada_w, dsil = _ada_grads(c16, dmod16, ada_w)
    s4 = jnp.concatenate([dsil[0, 8][None], jnp.zeros((7, d), F32)], axis=0).reshape(-1, LANES)
    g4 = _all_gather8("gather_cctx", s4, True).reshape(8, 8, d)
    grad_c_ctx = _cctx_grad(g4[0::2, 0][:, None, :], c_ctx[None])[0]

    def my_cols(a, n):
        return lax.dynamic_slice_in_dim(a, chip * n, n, axis=a.ndim - 1)

    grads = dict(g_shards)
    grads['c_ctx'] = grad_c_ctx
    grads['ada_w'] = grad_ada_w
    grads['ada_b'] = grad_ada_b
    for n in ['ln_g', 'ln_b', 'gdn_conv_w', 'sconv_w', 'conf_conv_w', 'ffn_conv_w']:
        grads[n] = my_cols(small_tot[n], wts[n].shape[-1])
    for n in ['gdn_a_log', 'gdn_dt_bias', 'gdn_norm_w', 'pool_w', 'pool_scale', 'conf_ln_g', 'conf_ln_b']:
        grads[n] = small_tot[n]

    delta, new_m, new_v = {}, {}, {}
    big_adam = list(g_shards) + ['ada_w']
    for n in big_adam:
        shp = wts[n].shape
        as2d = lambda a: a.reshape(-1, shp[-1])
        dl, nm, nv = _adamw("adamw_" + n, as2d(wts[n]), as2d(grads[n]), as2d(mom[n]), as2d(var[n]))
        delta[n], new_m[n], new_v[n] = dl.reshape(shp), nm.reshape(shp), nv.reshape(shp)
    small_adam = [n for n in names if n not in big_adam]
    for n in small_adam:
        shp = wts[n].shape
        whole = lambda a: a.reshape((1,) + shp) if len(shp) == 1 else a
        dl, nm, nv = _adamw("adamw_" + n, whole(wts[n]), whole(grads[n]), whole(mom[n]), whole(var[n]))
        delta[n], new_m[n], new_v[n] = dl.reshape(shp), nm.reshape(shp), nv.reshape(shp)

    return (loss, grad_x[None], *[grads[n] for n in names], *[delta[n] for n in names],
            *[new_m[n] for n in names], *[new_v[n] for n in names])
```

```python
import functools
import math

import jax
import jax.numpy as jnp
from jax import lax
from jax.experimental import pallas as pl
from jax.experimental.pallas import tpu as pltpu

F32 = jnp.float32
MXU_DTYPE = jnp.bfloat16
HIGHEST = lax.Precision.HIGHEST
MESH = pl.DeviceIdType.MESH

D_MODEL = 1024
GRID_W = 64
GDN_HEADS = 4
HEAD_DIM = 128
GDN_CHUNK = 64
POOL_WINDOWS = (2, 4, 8, 16)
GDN_CONV = 5
SC_WIDTH = 3
CF_WIDTH = 31
ALPHA = 4.0 ** 0.25
LN_EPS = 1e-5
RMS_EPS = 1e-6
LANES = 128
VMEM_LIMIT_BYTES = 58 * 1024 * 1024

ADAM_LR, ADAM_B1, ADAM_B2, ADAM_EPS, ADAM_WD, ADAM_STEP = 0.001, 0.9, 0.999, 1e-08, 0.01, 10


def _blocked(name, fn, grid, ins, in_specs, out_shapes, out_specs, acc=(), acc_axis=None, scratch=()):
    n_in = len(ins)
    n_out = len(out_shapes)

    def body(*refs):
        vals = [r[...] for r in refs[:n_in]]
        res = fn(*vals, *refs[n_in + n_out:])
        if not isinstance(res, (tuple, list)):
            res = (res,)
        for k, (r, v) in enumerate(zip(refs[n_in:n_in + n_out], res)):
            if k in acc:
                first = pl.program_id(acc_axis) == 0

                @pl.when(first)
                def _(r=r, v=v):
                    r[...] = v.astype(r.dtype)

                @pl.when(jnp.logical_not(first))
                def _(r=r, v=v):
                    r[...] += v.astype(r.dtype)
            else:
                r[...] = v.astype(r.dtype)

    return pl.pallas_call(
        body, name=name, grid=grid, in_specs=in_specs, out_specs=out_specs, out_shape=out_shapes,
        scratch_shapes=list(scratch),
        compiler_params=pltpu.CompilerParams(dimension_semantics=("arbitrary",) * len(grid),
                                             vmem_limit_bytes=VMEM_LIMIT_BYTES),
    )(*ins)


def _sds(shape, dtype=F32):
    return jax.ShapeDtypeStruct(tuple(shape), dtype)


def _tok(tm, n, col=0):
    return pl.BlockSpec((tm, n), lambda t: (t, col))


def _res(shape):
    nd = len(shape)
    return pl.BlockSpec(tuple(shape), lambda t: (0,) * nd)


def _silu(x):
    return x * jax.nn.sigmoid(x)


def _layernorm(r, g, b):
    mu = jnp.mean(r, -1, keepdims=True)
    d = r - mu
    var = jnp.mean(d * d, -1, keepdims=True)
    return d * lax.rsqrt(var + LN_EPS) * g + b


def _mm_nn_impl(a, b):
    return jnp.dot(a.astype(MXU_DTYPE), b.astype(MXU_DTYPE), preferred_element_type=F32)


def _mm_nt_impl(a, b):
    return lax.dot_general(a.astype(MXU_DTYPE), b.astype(MXU_DTYPE), (((1,), (1,)), ((), ())), preferred_element_type=F32)


def _mm_tn_impl(a, b):
    return lax.dot_general(a.astype(MXU_DTYPE), b.astype(MXU_DTYPE), (((0,), (0,)), ((), ())), preferred_element_type=F32)


def _mm_vjp(mm, mm_da, mm_db):
    f = jax.custom_vjp(mm)
    f.defvjp(lambda a, b: (mm(a, b), (a, b)), lambda res, g: (mm_da(g, res[1]), mm_db(res[0], g)))
    return f


_mm = _mm_vjp(_mm_nn_impl, lambda g, b: _mm_nt_impl(g, b), lambda a, g: _mm_tn_impl(a, g))
_mm_nt = _mm_vjp(_mm_nt_impl, lambda g, b: _mm_nn_impl(g, b), lambda a, g: _mm_tn_impl(g, a))
_mm_tn = _mm_vjp(_mm_tn_impl, lambda g, b: _mm_nt_impl(b, g), lambda a, g: _mm_nn_impl(a, g))


def _row(w, k):
    rows = lax.broadcasted_iota(jnp.int32, w.shape, 0)
    return jnp.sum(jnp.where(rows == k, w, 0.0), axis=0, keepdims=True)


def _col_mask(shape, dc):
    col = lax.broadcasted_iota(jnp.int32, shape, 0) & (GRID_W - 1)
    return (col + dc >= 0) & (col + dc < GRID_W)


def _center(x_ext, halo):
    return x_ext[halo:x_ext.shape[0] - halo]


def _make_dwconv(taps, halo):
    assert all(abs(s) <= halo for s, _ in taps)

    def shifted(x_ext, s):
        r = x_ext if s == 0 else pltpu.roll(x_ext, (-s) % x_ext.shape[0], 0)
        return _center(r, halo)

    @jax.custom_vjp
    def conv(x_ext, w):
        acc = None
        for k, (s, dc) in enumerate(taps):
            r = shifted(x_ext, s)
            if dc != 0:
                r = jnp.where(_col_mask(r.shape, dc), r, 0.0)
            term = r * _row(w, k)
            acc = term if acc is None else acc + term
        return acc

    def fwd(x_ext, w):
        return conv(x_ext, w), (x_ext, w)

    def bwd(res, dy):
        x_ext, w = res
        n = x_ext.shape[0]
        rows = lax.broadcasted_iota(jnp.int32, w.shape, 0)
        pad = jnp.zeros((halo, dy.shape[1]), F32)
        dx = None
        dw = jnp.zeros(w.shape, F32)
        for k, (s, dc) in enumerate(taps):
            dym = dy if dc == 0 else jnp.where(_col_mask(dy.shape, dc), dy, 0.0)
            dw = dw + jnp.where(rows == k, jnp.sum(dym * shifted(x_ext, s), axis=0, keepdims=True), 0.0)
            t = jnp.concatenate([pad, dym * _row(w, k), pad], axis=0)
            if s != 0:
                t = pltpu.roll(t, s % n, 0)
            dx = t if dx is None else dx + t
        return dx, dw

    conv.defvjp(fwd, bwd)

    @jax.custom_vjp
    def known(x_ext, w, y):
        return y

    known.defvjp(lambda x_ext, w, y: (y, (x_ext, w)), lambda res, dy: (*bwd(res, dy), jnp.zeros_like(dy)))
    return conv, known


def _taps_1d(width):
    return tuple((k - width // 2, 0) for k in range(width))


HALO_SHORT = 8
HALO_CONF = 16
HALO_GRID = 72
_conv5 = _make_dwconv(_taps_1d(GDN_CONV), HALO_SHORT)[0]
_conv3 = _make_dwconv(_taps_1d(SC_WIDTH), HALO_SHORT)[0]
_conv31 = _make_dwconv(_taps_1d(CF_WIDTH), HALO_CONF)[0]
_conv3x3, _conv3x3_known = _make_dwconv(tuple((dr * GRID_W + dc, dc) for dr in (-1, 0, 1) for dc in (-1, 0, 1)), HALO_GRID)
_pool_sums = {w: _make_dwconv(tuple((s, 0) for s in range(-(w // 2), w - w // 2)), HALO_SHORT)[0] for w in POOL_WINDOWS}


def _all_gather8(name, blk, in_vmem):
    m_per, n = blk.shape
    space = pltpu.VMEM if in_vmem else pl.ANY

    def body(x_ref, out_ref, send_sems, recv_sems, local_sem):
        x, y, c = lax.axis_index("x"), lax.axis_index("y"), lax.axis_index("c")
        me, sibling = (x, y, c), (x, y, 1 - c)
        chips = [(1 - x, y), (x, 1 - y), (1 - x, 1 - y)]

        def rows(px, py, pc):
            return out_ref.at[pl.ds((4 * px + 2 * py + pc) * m_per, m_per), :]

        def copy(k, block, to, src=None):
            return pltpu.make_async_remote_copy(
                src_ref=rows(*block) if src is None else src, dst_ref=rows(*block),
                send_sem=send_sems.at[k], recv_sem=recv_sems.at[k], device_id=to, device_id_type=MESH)

        mine = pltpu.make_async_copy(x_ref, rows(*me), local_sem)
        mine.start()
        first = [copy(0, me, sibling, src=x_ref)]
        first += [copy(1 + j, me, (*chip, c), src=x_ref) for j, chip in enumerate(chips)]
        for cp in first:
            cp.start()
        passed = [copy(4 + j, (*chip, c), sibling) for j, chip in enumerate(chips)]
        for j, chip in enumerate(chips):
            copy(1 + j, (*chip, c), me).wait_recv()
            passed[j].start()
        copy(0, sibling, me).wait_recv()
        for j, chip in enumerate(chips):
            copy(4 + j, (*chip, 1 - c), me).wait_recv()
        for cp in first + passed:
            cp.wait_send()
        mine.wait()

    return pl.pallas_call(
        body, name=name, out_shape=_sds((8 * m_per, n), blk.dtype),
        in_specs=[pl.BlockSpec(memory_space=space)], out_specs=pl.BlockSpec(memory_space=space),
        scratch_shapes=[pltpu.SemaphoreType.DMA((7,)), pltpu.SemaphoreType.DMA((7,)), pltpu.SemaphoreType.DMA],
        compiler_params=pltpu.CompilerParams(vmem_limit_bytes=VMEM_LIMIT_BYTES),
    )(blk)


_ANY = pl.BlockSpec(memory_space=pl.ANY)


def _sibling_exchange(name, sends):
    n = len(sends)

    def body(*refs):
        x, y, c = lax.axis_index("x"), lax.axis_index("y"), lax.axis_index("c")
        send_sems, recv_sems = refs[2 * n:]
        copies = [pltpu.make_async_remote_copy(src_ref=refs[i], dst_ref=refs[n + i], send_sem=send_sems.at[i],
                                               recv_sem=recv_sems.at[i], device_id=(x, y, 1 - c), device_id_type=MESH)
                  for i in range(n)]
        for cp in copies:
            cp.start()
        for cp in copies:
            cp.wait()

    return pl.pallas_call(
        body, name=name, out_shape=[_sds(s.shape, s.dtype) for s in sends],
        in_specs=[_ANY] * n, out_specs=[_ANY] * n,
        scratch_shapes=[pltpu.SemaphoreType.DMA((n,)), pltpu.SemaphoreType.DMA((n,))],
    )(*sends)


def _chip_scatter(name, parts):
    n = len(parts)

    def body(*refs):
        p_refs, out_refs = refs[:n], refs[n:2 * n]
        send_sems, recv_sems, local_sems = refs[2 * n:]
        x, y, c = lax.axis_index("x"), lax.axis_index("y"), lax.axis_index("c")
        sibling = (x, y, 1 - c)
        me_chip = 2 * x + y
        chips = [(1 - x, y), (x, 1 - y), (1 - x, 1 - y)]

        def chip_id(chip):
            return 2 * chip[0] + chip[1]

        def copy(i, k, src, dst, to):
            return pltpu.make_async_remote_copy(src_ref=src, dst_ref=dst, send_sem=send_sems.at[i, k],
                                                recv_sem=recv_sems.at[i, k], device_id=to, device_id_type=MESH)

        mine = [pltpu.make_async_copy(p_refs[i].at[me_chip], out_refs[i].at[c, me_chip], local_sems.at[i]) for i in range(n)]
        first = []
        for i in range(n):
            first.append(copy(i, 0, p_refs[i].at[me_chip], out_refs[i].at[c, me_chip], sibling))
            first += [copy(i, 1 + j, p_refs[i].at[chip_id(chip)], out_refs[i].at[c, me_chip], (*chip, c))
                      for j, chip in enumerate(chips)]
        for cp in mine + first:
            cp.start()
        passed = []
        for j, chip in enumerate(chips):
            for i in range(n):
                landed = out_refs[i].at[c, chip_id(chip)]
                copy(i, 1 + j, p_refs[i].at[0], landed, sibling).wait_recv()
                passed.append(copy(i, 4 + j, landed, landed, sibling))
                passed[-1].start()
        for i in range(n):
            copy(i, 0, p_refs[i].at[0], out_refs[i].at[1 - c, me_chip], sibling).wait_recv()
            for j, chip in enumerate(chips):
                copy(i, 4 + j, p_refs[i].at[0], out_refs[i].at[1 - c, chip_id(chip)], sibling).wait_recv()
        for cp in first + passed:
            cp.wait_send()
        for cp in mine:
            cp.wait()

    return pl.pallas_call(
        body, name=name, out_shape=[_sds((2,) + p.shape, p.dtype) for p in parts],
        in_specs=[_ANY] * n, out_specs=[_ANY] * n,
        scratch_shapes=[pltpu.SemaphoreType.DMA((n, 7)), pltpu.SemaphoreType.DMA((n, 7)), pltpu.SemaphoreType.DMA((n,))],
    )(*parts)


_HBM = pl.BlockSpec(memory_space=pltpu.HBM)
_SEM = pl.BlockSpec(memory_space=pltpu.SEMAPHORE)
_SPLIT_CALL = pltpu.CompilerParams(has_side_effects=pltpu.SideEffectType.DATAFLOW_SIDE_EFFECTING)


def _peer_chips():
    x, y = lax.axis_index("x"), lax.axis_index("y")
    return [(1 - x, y), (x, 1 - y), (1 - x, 1 - y)]


def _chip_send_start(name, blocks, before, per_peer=False):
    n, nb = len(blocks), len(before)

    def body(*refs):
        x_refs, land_refs = refs[:n], refs[n:2 * n]
        sems, token = refs[2 * n + nb:2 * n + nb + 6 * n], refs[-1]
        c = lax.axis_index("c")
        me_chip = 2 * lax.axis_index("x") + lax.axis_index("y")
        for i in range(n):
            for j, chip in enumerate(_peer_chips()):
                src = x_refs[i].at[2 * chip[0] + chip[1]] if per_peer else x_refs[i]
                pltpu.make_async_remote_copy(src_ref=src, dst_ref=land_refs[i].at[me_chip], send_sem=sems[6 * i + j],
                                             recv_sem=sems[6 * i + 3 + j], device_id=(*chip, c), device_id_type=MESH).start()
        token[...] = jnp.zeros(token.shape, token.dtype)

    lands = [lax.empty(b.shape if per_peer else (4,) + b.shape, b.dtype) for b in blocks]
    hbm = [pltpu.with_memory_space_constraint(a, pltpu.HBM) for a in list(blocks) + lands]
    outs = pl.pallas_call(
        body, name=name,
        out_shape=tuple([pltpu.SemaphoreType.DMA(())] * (6 * n)) + tuple(pltpu.HBM(a.shape, a.dtype) for a in hbm) + (_sds((8, LANES)),),
        in_specs=tuple([_HBM] * (2 * n) + [_ANY] * nb),
        out_specs=tuple([_SEM] * (6 * n) + [_HBM] * (2 * n) + [pl.BlockSpec(memory_space=pltpu.VMEM)]),
        input_output_aliases={i: 6 * n + i for i in range(2 * n)}, compiler_params=_SPLIT_CALL,
    )(*hbm, *before)
    return outs[:6 * n], outs[6 * n:7 * n], outs[7 * n:8 * n], outs[8 * n]


def _chip_send_wait(name, sems, blocks, lands, after, per_peer=False):
    n = len(blocks)

    def body(*refs):
        x_refs, land_refs, sems = refs[:n], refs[n:2 * n], refs[2 * n:8 * n]
        c = lax.axis_index("c")
        for i in range(n):
            for j, chip in enumerate(_peer_chips()):
                src = x_refs[i].at[2 * chip[0] + chip[1]] if per_peer else x_refs[i]
                cp = pltpu.make_async_remote_copy(src_ref=src, dst_ref=land_refs[i].at[2 * chip[0] + chip[1]],
                                                  send_sem=sems[6 * i + j], recv_sem=sems[6 * i + 3 + j],
                                                  device_id=(*chip, c), device_id_type=MESH)
                cp.wait_send()
                cp.wait_recv()

    outs = pl.pallas_call(
        body, name=name, out_shape=tuple(pltpu.HBM(a.shape, a.dtype) for a in list(blocks) + list(lands)),
        in_specs=tuple([_HBM] * (2 * n) + [_SEM] * (6 * n) + [_ANY]), out_specs=tuple([_HBM] * (2 * n)),
        input_output_aliases={i: i for i in range(2 * n)}, compiler_params=_SPLIT_CALL,
    )(*blocks, *lands, *sems, after)
    return outs[:n], outs[n:]


def _pack_flat(arrays, multiple, dtype):
    flat = jnp.concatenate([a.reshape(-1).astype(dtype) for a in arrays])
    pad = (-flat.shape[0]) % multiple
    if pad:
        flat = jnp.concatenate([flat, jnp.zeros((pad,), dtype)])
    return flat


def _unpack_flat(flat, shapes):
    out, off = [], 0
    for s in shapes:
        n = math.prod(s)
        out.append(flat[off:off + n].reshape(s))
        off += n
    return out


def _packed_rows(shape):
    return -(-math.prod(shape) // (8 * LANES)) * 8


def _pack_rows(arrays):
    parts = []
    for a in arrays:
        rows = _packed_rows(a.shape)
        flat = a.reshape(-1).astype(F32)
        if rows * LANES != flat.shape[0]:
            flat = jnp.pad(flat, (0, rows * LANES - flat.shape[0]))
        parts.append(flat.reshape(rows, LANES))
    return jnp.concatenate(parts, axis=0)


def _unpack_rows(packed, shapes):
    out, r0 = [], 0
    for s in shapes:
        rows = _packed_rows(s)
        out.append(packed[r0:r0 + rows].reshape(-1)[:math.prod(s)].reshape(s))
        r0 += rows
    return out


def _row_tile(r, cap, mult=16):
    for cand in range(min(cap, r) // mult * mult, 0, -mult):
        if r % cand == 0:
            return cand
    return r


ELEMENTWISE_BLOCK_ELEMS = 1 << 20


def _sum_axis1(name, x, out_dtype=F32):
    h, k, r, n = x.shape
    tr = _row_tile(r, max(16, ELEMENTWISE_BLOCK_ELEMS // (k * n)), 32 // x.dtype.itemsize)

    def fn(s):
        acc = s[0].astype(F32)
        for i in range(1, k):
            acc = acc + s[i].astype(F32)
        return acc

    return _blocked(name, fn, (h, r // tr), [x], [pl.BlockSpec((None, k, tr, n), lambda i, t: (i, 0, t, 0))],
                    [_sds((h, r, n), out_dtype)], [pl.BlockSpec((None, tr, n), lambda i, t: (i, t, 0))])[0]


def _sum_pair(name, a, b, out_dtype):
    lead, r, n = a.shape
    tr = _row_tile(r, max(16, ELEMENTWISE_BLOCK_ELEMS // n), 32 // min(a.dtype.itemsize, jnp.dtype(out_dtype).itemsize))
    spec = pl.BlockSpec((None, tr, n), lambda i, t: (i, t, 0))
    return _blocked(name, lambda x, y: x.astype(F32) + y.astype(F32), (lead, r // tr), [a, b], [spec, spec],
                    [_sds((lead, r, n), out_dtype)], [spec])[0]


def _adamw(name, w, g, m, v):
    if w.ndim == 2:
        r, c = w.shape
        tr = _row_tile(r, max(8, (1 << 21) // (4 * c)), 8)
        grid, spec = (r // tr,), pl.BlockSpec((tr, c), lambda t: (t, 0))
    else:
        grid, spec = (1,), _res(w.shape)
    bc1 = 1.0 - ADAM_B1 ** ADAM_STEP
    bc2 = 1.0 - ADAM_B2 ** ADAM_STEP

    def fn(w, g, m, v):
        nm = ADAM_B1 * m + (1.0 - ADAM_B1) * g
        nv = ADAM_B2 * v + (1.0 - ADAM_B2) * (g * g)
        delta = -ADAM_LR * ((nm / bc1) / (jnp.sqrt(nv / bc2) + ADAM_EPS) + ADAM_WD * w)
        return delta, nm, nv

    return _blocked(name, fn, grid, [w, g, m, v], [spec] * 4, [_sds(w.shape)] * 3, [spec] * 3)


def _mod_rows(c16, ada_w):
    depth, d, n = ada_w.shape

    def fn(c, w):
        return _mm(_silu(c), w)

    return _blocked("mod_rows", fn, (depth,), [c16, ada_w],
                    [_res(c16.shape), pl.BlockSpec((None, d, n), lambda l: (l, 0, 0))],
                    [_sds((depth, 16, n))], [pl.BlockSpec((None, 16, n), lambda l: (l, 0, 0))])[0]


def _ada_grads(c16, dmod16, ada_w):
    depth, d, n = ada_w.shape

    def fn(c, dm, w):
        return _mm_tn(_silu(c), dm), _mm_nt(dm, w)

    return _blocked("ada_grads", fn, (depth,), [c16, dmod16, ada_w],
                    [_res(c16.shape), pl.BlockSpec((None, 16, n), lambda l: (l, 0, 0)),
                     pl.BlockSpec((None, d, n), lambda l: (l, 0, 0))],
                    [_sds((depth, d, n)), _sds((depth, 16, d))],
                    [pl.BlockSpec((None, d, n), lambda l: (l, 0, 0)), pl.BlockSpec((None, 16, d), lambda l: (l, 0, 0))])


def _cctx_grad(parts, c_ctx_row):
    def fn(p, c):
        tot = ((p[0] + p[1]) + p[2]) + p[3]
        _, vjp = jax.vjp(_silu, c)
        return vjp(tot)[0]

    return _blocked("cctx_grad", fn, (1,), [parts, c_ctx_row], [_res(parts.shape), _res(c_ctx_row.shape)],
                    [_sds(c_ctx_row.shape)], [_res(c_ctx_row.shape)])[0]


def _modulate(x, sh, sc):
    return x * (1.0 + sc) + sh


def _weight_operand(w):
    if isinstance(w, tuple):
        return w[0], w[1], tuple(b for b in w[1].block_shape if b is not None)
    return w, _res(w.shape), w.shape


def _matmul_cols(u, w):
    if w.ndim == 2:
        return jnp.dot(u, w, preferred_element_type=F32)
    return jnp.concatenate([jnp.dot(u, w[j], preferred_element_type=F32) for j in range(w.shape[0])], axis=1)


def _matmul_cols_t(dp, w):
    if w.ndim == 2:
        return _mm_nt(dp, w)
    nj = w.shape[2]
    return functools.reduce(lambda a, b: a + b, [_mm_nt(dp[:, j * nj:(j + 1) * nj], w[j]) for j in range(w.shape[0])])


def _rows_weight(w):
    return w if w.ndim == 2 else w.reshape(w.shape[0] * w.shape[1], w.shape[2])


def _inproj(name, x, sh, sc, w, tm):
    t, d = x.shape
    w, w_spec, w_blk = _weight_operand(w)
    n = w_blk[-1] * (w_blk[0] if len(w_blk) == 3 else 1)
    tm = min(tm, t)

    def fn(x, sh, sc, w):
        u = _modulate(x, sh, sc).astype(MXU_DTYPE)
        return _matmul_cols(u, w), u

    return _blocked(name, fn, (t // tm,), [x, sh, sc, w],
                    [_tok(tm, d), _res(sh.shape), _res(sc.shape), w_spec],
                    [_sds((t, n)), _sds((t, d), MXU_DTYPE)], [_tok(tm, n), _tok(tm, d)])


def _inproj_bwd(name, dp, x, sh, sc, w, add, tm):
    t, d = x.shape
    w, w_spec, w_blk = _weight_operand(w)
    n = w_blk[-1] * (w_blk[0] if len(w_blk) == 3 else 1)
    tm = min(tm, t)
    has_add = add is not None

    def fn(dp, x, sh, sc, w, *rest):
        du = _matmul_cols_t(dp, w)
        _, vjp = jax.vjp(_modulate, x, sh, sc)
        dx, dsh, dsc = vjp(du)
        if has_add:
            dx = dx + rest[0]
        return dx, dsh, dsc

    ins = [dp, x, sh, sc, w] + ([add] if has_add else [])
    specs = [_tok(tm, n), _tok(tm, d), _res(sh.shape), _res(sc.shape), w_spec] + ([_tok(tm, d)] if has_add else [])
    return _blocked(name, fn, (t // tm,), ins, specs,
                    [_sds((t, d)), _sds(sh.shape), _sds(sc.shape)], [_tok(tm, d), _res(sh.shape), _res(sc.shape)],
                    acc=(1, 2), acc_axis=0)


def _residual_ln(y, x, gt, lng, lnb):
    return _layernorm(ALPHA * x + gt * y, lng, lnb)


def _gdn_mix(o0, o1, pg, yp, normw):
    o = o0 + o1
    heads = []
    for h in range(GDN_HEADS):
        oh = o[:, h * HEAD_DIM:(h + 1) * HEAD_DIM]
        heads.append(oh * lax.rsqrt(jnp.mean(oh * oh, -1, keepdims=True) + RMS_EPS) * normw)
    on = jnp.concatenate(heads, axis=-1) * _silu(pg)
    return jnp.concatenate([on, yp], axis=-1)


def _even_out(o0, o1, p, ypool, x, normw, gt, lng, lnb, w, tm):
    t, d = x.shape
    tm = min(tm, t)
    gate_blk = 3

    def fn(o0, o1, pg, yp, x, normw, gt, lng, lnb, w):
        mix = _gdn_mix(o0, o1, pg, yp, normw).astype(MXU_DTYPE)
        y = jnp.dot(mix, w, preferred_element_type=F32)
        return _residual_ln(y, x, gt, lng, lnb), mix, y

    return _blocked("even_out", fn, (t // tm,), [o0, o1, p, ypool, x, normw, gt, lng, lnb, w],
                    [_tok(tm, 512), _tok(tm, 512), _tok(tm, 512, gate_blk), _tok(tm, 512), _tok(tm, d),
                     _res(normw.shape), _res(gt.shape), _res(lng.shape), _res(lnb.shape), _res(w.shape)],
                    [_sds((t, d)), _sds((t, d), MXU_DTYPE), _sds((t, d))], [_tok(tm, d)] * 3)


def _even_out_bwd(dx1, o0, o1, p, ypool, x, y, normw, gt, lng, lnb, w, tm):
    t, d = x.shape
    tm = min(tm, t)

    def fn(dx1, o0, o1, pg, yp, x, y, normw, gt, lng, lnb, w):
        _, vjp2 = jax.vjp(_residual_ln, y, x, gt, lng, lnb)
        dy, dx, dgt, dlng, dlnb = vjp2(dx1)
        dyb = dy.astype(MXU_DTYPE)
        dmix = _mm_nt(dyb, w)
        _, vjp1 = jax.vjp(_gdn_mix, o0, o1, pg, yp, normw)
        do, _, dpg, dyp, dnormw = vjp1(dmix)
        return do, dpg, dyp, dx, dyb, dnormw, dgt, dlng, dlnb

    return _blocked("even_out_bwd", fn, (t // tm,), [dx1, o0, o1, p, ypool, x, y, normw, gt, lng, lnb, w],
                    [_tok(tm, d), _tok(tm, 512), _tok(tm, 512), _tok(tm, 512, 3), _tok(tm, 512),
                     _tok(tm, d), _tok(tm, d), _res(normw.shape), _res(gt.shape), _res(lng.shape), _res(lnb.shape),
                     _res(w.shape)],
                    [_sds((t, 512)), _sds((t, 512), MXU_DTYPE), _sds((t, 512)), _sds((t, d)), _sds((t, d), MXU_DTYPE),
                     _sds(normw.shape), _sds(gt.shape), _sds(lng.shape), _sds(lnb.shape)],
                    [_tok(tm, 512), _tok(tm, 512), _tok(tm, 512), _tok(tm, d), _tok(tm, d),
                     _res(normw.shape), _res(gt.shape), _res(lng.shape), _res(lnb.shape)],
                    acc=(5, 6, 7, 8), acc_axis=0)


def _odd_mix(ysc, z0, cg, cb):
    z = _silu(_layernorm(z0, cg, cb))
    return jnp.concatenate([ysc, z], axis=-1)


def _odd_out(ysc, z0, x, cg, cb, gt, lng, lnb, w, tm):
    t, d = x.shape
    tm = min(tm, t)

    def fn(ysc, z0, x, cg, cb, gt, lng, lnb, w):
        mix = _odd_mix(ysc, z0, cg, cb).astype(MXU_DTYPE)
        y = jnp.dot(mix, w, preferred_element_type=F32)
        return _residual_ln(y, x, gt, lng, lnb), mix, y

    return _blocked("odd_out", fn, (t // tm,), [ysc, z0, x, cg, cb, gt, lng, lnb, w],
                    [_tok(tm, 512), _tok(tm, 512), _tok(tm, d), _res(cg.shape), _res(cb.shape), _res(gt.shape),
                     _res(lng.shape), _res(lnb.shape), _res(w.shape)],
                    [_sds((t, d)), _sds((t, d), MXU_DTYPE), _sds((t, d))], [_tok(tm, d)] * 3)


def _odd_out_bwd(dx3, ysc, z0, x, y, cg, cb, gt, lng, lnb, w, tm):
    t, d = x.shape
    tm = min(tm, t)

    def fn(dx3, ysc, z0, x, y, cg, cb, gt, lng, lnb, w):
        _, vjp2 = jax.vjp(_residual_ln, y, x, gt, lng, lnb)
        dy, dx, dgt, dlng, dlnb = vjp2(dx3)
        dyb = dy.astype(MXU_DTYPE)
        dmix = _mm_nt(dyb, w)
        _, vjp1 = jax.vjp(_odd_mix, ysc, z0, cg, cb)
        dysc, dz0, dcg, dcb = vjp1(dmix)
        return dysc, dz0, dx, dyb, dcg, dcb, dgt, dlng, dlnb

    return _blocked("odd_out_bwd", fn, (t // tm,), [dx3, ysc, z0, x, y, cg, cb, gt, lng, lnb, w],
                    [_tok(tm, d), _tok(tm, 512), _tok(tm, 512), _tok(tm, d), _tok(tm, d), _res(cg.shape), _res(cb.shape),
                     _res(gt.shape), _res(lng.shape), _res(lnb.shape), _res(w.shape)],
                    [_sds((t, 512)), _sds((t, 512)), _sds((t, d)), _sds((t, d), MXU_DTYPE),
                     _sds(cg.shape), _sds(cb.shape), _sds(gt.shape), _sds(lng.shape), _sds(lnb.shape)],
                    [_tok(tm, 512), _tok(tm, 512), _tok(tm, d), _tok(tm, d),
                     _res(cg.shape), _res(cb.shape), _res(gt.shape), _res(lng.shape), _res(lnb.shape)],
                    acc=(4, 5, 6, 7, 8), acc_axis=0)


def _ffn_down(name, s, x, gt, lng, lnb, w, tm):
    t, d = x.shape
    f = s.shape[1]
    w, w_spec, _ = _weight_operand(w)
    tm = min(tm, t)

    def fn(s, x, gt, lng, lnb, w):
        y = jnp.dot(s, _rows_weight(w), preferred_element_type=F32)
        return _residual_ln(y, x, gt, lng, lnb), y

    return _blocked(name, fn, (t // tm,), [s, x, gt, lng, lnb, w],
                    [_tok(tm, f), _tok(tm, d), _res(gt.shape), _res(lng.shape), _res(lnb.shape), w_spec],
                    [_sds((t, d)), _sds((t, d))], [_tok(tm, d)] * 2)


def _ffn_down_bwd(name, dx2, x, y, gt, lng, lnb, w, tm):
    t, d = x.shape
    w, w_spec, w_blk = _weight_operand(w)
    f = math.prod(w_blk[:-1])
    tm = min(tm, t)

    def fn(dx2, x, y, gt, lng, lnb, w):
        _, vjp2 = jax.vjp(_residual_ln, y, x, gt, lng, lnb)
        dy, dx, dgt, dlng, dlnb = vjp2(dx2)
        dyb = dy.astype(MXU_DTYPE)
        return _mm_nt(dyb, _rows_weight(w)), dx, dyb, dgt, dlng, dlnb

    return _blocked(name, fn, (t // tm,), [dx2, x, y, gt, lng, lnb, w],
                    [_tok(tm, d), _tok(tm, d), _tok(tm, d), _res(gt.shape), _res(lng.shape), _res(lnb.shape), w_spec],
                    [_sds((t, f), MXU_DTYPE), _sds((t, d)), _sds((t, d), MXU_DTYPE), _sds(gt.shape), _sds(lng.shape), _sds(lnb.shape)],
                    [_tok(tm, f), _tok(tm, d), _tok(tm, d), _res(gt.shape), _res(lng.shape), _res(lnb.shape)],
                    acc=(3, 4, 5), acc_axis=0)


def _loss_and_grad(x4, target, tm):
    t, d = x4.shape
    tm = min(tm, t)

    def fn(y, tg):
        e = y - tg
        part = 0.5 * jnp.sum(jnp.mean(e * e, axis=-1, keepdims=True), axis=0, keepdims=True)
        return jnp.broadcast_to(part, (1, LANES)), e * (1.0 / d)

    return _blocked("loss", fn, (t // tm,), [x4, target], [_tok(tm, d), _tok(tm, d)],
                    [_sds((1, LANES)), _sds((t, d))], [_res((1, LANES)), _tok(tm, d)], acc=(0,), acc_axis=0)


def _matmul_tn(name, pairs, chip_cols=None):
    k = pairs[0][0].shape[1]
    n = pairs[0][1].shape[1]
    tk = k if k <= 1024 else k // 2
    if chip_cols is None:
        tn = 512 if n % 512 == 0 else (384 if n % 384 == 0 else 128)
        out_spec, out_shape = pl.BlockSpec((tk, tn), lambda i, j: (i, j)), _sds((k, n), MXU_DTYPE)
    else:
        tn = chip_cols
        tk = min(tk, 512) if tn > 1024 else tk
        out_spec = pl.BlockSpec((None, tk, tn), lambda i, j: (j, i, 0))
        out_shape = _sds((n // chip_cols, k, chip_cols), MXU_DTYPE)

    def body(*refs):
        acc = None
        for p in range(len(pairs)):
            term = lax.dot_general(refs[2 * p][...], refs[2 * p + 1][...], (((0,), (0,)), ((), ())), preferred_element_type=F32)
            acc = term if acc is None else acc + term
        refs[-1][...] = acc.astype(refs[-1].dtype)

    in_specs, args = [], []
    for a, b in pairs:
        t = a.shape[0]
        in_specs += [pl.BlockSpec((t, tk), lambda i, j: (0, i)), pl.BlockSpec((t, tn), lambda i, j: (0, j))]
        args += [a, b]
    return pl.pallas_call(
        body, name=name, grid=(k // tk, n // tn), in_specs=in_specs,
        out_specs=out_spec, out_shape=out_shape,
        compiler_params=pltpu.CompilerParams(dimension_semantics=("arbitrary", "arbitrary"),
                                             vmem_limit_bytes=VMEM_LIMIT_BYTES),
    )(*args)


def _chan(t, col_of):
    return pl.BlockSpec((t, LANES), lambda i: (0, col_of(i)))


def _wblk(k, col_of=lambda i: i):
    return pl.BlockSpec((k, LANES), lambda i: (0, col_of(i)))


SEQ_TILE = 256
GRID_SEQ_TILE = 512


def _load_ext(ref, r0, rows, halo, total):
    lo, hi = max(r0 - halo, 0), min(r0 + rows + halo, total)
    parts = []
    if lo > r0 - halo:
        parts.append(jnp.zeros((lo - (r0 - halo), ref.shape[1]), F32))
    parts.append(ref[lo:hi, :].astype(F32))
    if hi < r0 + rows + halo:
        parts.append(jnp.zeros((r0 + rows + halo - hi, ref.shape[1]), F32))
    return parts[0] if len(parts) == 1 else jnp.concatenate(parts, axis=0)


def _seq_stage(name, tile_fn, halo, grid_n, seqs, pars, outs, tile_rows=SEQ_TILE):
    total = seqs[0][0].shape[0]
    rows = min(tile_rows, total)
    ns, npar = len(seqs), len(pars)

    def body(*refs):
        seq_refs, par_refs, out_refs = refs[:ns], refs[ns:ns + npar], refs[ns + npar:]
        par_vals = [r[...] for r in par_refs]
        for r0 in range(0, total, rows):
            exts = [_load_ext(r, r0, rows, halo, total) for r in seq_refs]
            res = tile_fn(r0, *exts, *par_vals)
            for o_ref, v in zip(out_refs, res):
                o_ref[r0:r0 + rows, :] = v.astype(o_ref.dtype)

    return pl.pallas_call(
        body, name=name, grid=(grid_n,), in_specs=[s for _, s in seqs] + [s for _, s in pars],
        out_specs=[s for _, s in outs], out_shape=[o for o, _ in outs],
        compiler_params=pltpu.CompilerParams(dimension_semantics=("arbitrary",), vmem_limit_bytes=VMEM_LIMIT_BYTES),
    )(*[a for a, _ in seqs], *[a for a, _ in pars])


def _seq_stage_bwd(name, tile_fn, halo, grid_n, douts, seqs, pars, dseq_outs, dpar_outs, tile_rows=SEQ_TILE, n_const=0):
    total = seqs[0][0].shape[0]
    rows = min(tile_rows, total)
    groups = [d if isinstance(d, list) else [d] for d in douts]
    douts = [d for g in groups for d in g]
    starts = [sum(len(g) for g in groups[:k]) for k in range(len(groups))]
    nd, ns, npar = len(douts), len(seqs), len(pars)
    nsd = ns - n_const
    widths = [s.block_shape[-1] for _, s in seqs[:nsd]]

    def dtile(dout_refs, k, r0):
        terms = [d[r0:r0 + rows, :].astype(F32) for d in dout_refs[starts[k]:starts[k] + len(groups[k])]]
        return functools.reduce(lambda a, b: a + b, terms)

    def body(*refs):
        dout_refs, seq_refs, par_refs = refs[:nd], refs[nd:nd + ns], refs[nd + ns:nd + ns + npar]
        n_in = nd + ns + npar
        dseq_refs = refs[n_in:n_in + nsd]
        dpar_refs = refs[n_in + nsd:n_in + nsd + npar]
        accs = refs[n_in + nsd + npar:]
        par_vals = [r[...] for r in par_refs]
        for a in accs:
            for r0 in range(0, total, rows):
                a[r0:r0 + rows, :] = jnp.zeros((rows, a.shape[1]), F32)
        dpars = [jnp.zeros(p.shape, F32) for p in par_vals]
        for r0 in range(0, total, rows):
            exts = [_load_ext(r, r0, rows, halo, total) for r in seq_refs]
            saved = exts[nsd:]
            _, vjp = jax.vjp(lambda *a: tile_fn(r0, *a[:nsd], *saved, *a[nsd:]), *exts[:nsd], *par_vals)
            cts = vjp(tuple(dtile(dout_refs, k, r0) for k in range(len(groups))))
            lo, hi = max(r0 - halo, 0), min(r0 + rows + halo, total)
            for a, dx in zip(accs, cts[:nsd]):
                a[lo:hi, :] += dx[lo - (r0 - halo):hi - (r0 - halo)]
            dpars = [acc + g for acc, g in zip(dpars, cts[nsd:])]
        for o_ref, a in zip(dseq_refs, accs):
            for r0 in range(0, total, rows):
                o_ref[r0:r0 + rows, :] = a[r0:r0 + rows, :].astype(o_ref.dtype)
        for o_ref, g in zip(dpar_refs, dpars):
            o_ref[...] = g

    return pl.pallas_call(
        body, name=name, grid=(grid_n,),
        in_specs=[s for _, s in douts] + [s for _, s in seqs] + [s for _, s in pars],
        out_specs=[s for _, s in dseq_outs] + [s for _, s in dpar_outs],
        out_shape=[o for o, _ in dseq_outs] + [o for o, _ in dpar_outs],
        scratch_shapes=[pltpu.VMEM((total, w), F32) for w in widths],
        compiler_params=pltpu.CompilerParams(dimension_semantics=("arbitrary",), vmem_limit_bytes=VMEM_LIMIT_BYTES),
    )(*[a for a, _ in douts], *[a for a, _ in seqs], *[a for a, _ in pars])


def _qkv_tile(r0, p_ext, w):
    kind = pl.program_id(0)
    a = _silu(_conv5(p_ext, w))
    nrm = a * lax.rsqrt(jnp.sum(a * a, -1, keepdims=True) + RMS_EPS)
    scale = jnp.where(kind < GDN_HEADS, HEAD_DIM ** -0.5, 1.0).astype(F32)
    sel = (kind < 2 * GDN_HEADS).astype(F32)
    return (sel * (nrm * scale) + (1.0 - sel) * a,)


def _qkv_conv(name, p, w):
    t = p.shape[0]
    nb = w.shape[1] // LANES
    ident = lambda i: i
    return _seq_stage(name, _qkv_tile, HALO_SHORT, nb, [(p, _chan(t, ident))], [(w, _wblk(GDN_CONV))],
                      [(_sds((t, nb * LANES)), _chan(t, ident))])[0]


def _qkv_conv_bwd(name, dqkv0, dqkv1, p, w):
    t = p.shape[0]
    nb = w.shape[1] // LANES
    ident = lambda i: i
    return _seq_stage_bwd(name, _qkv_tile, HALO_SHORT, nb, [[(dqkv0, _chan(t, ident)), (dqkv1, _chan(t, ident))]],
                          [(p, _chan(t, ident))],
                          [(w, _wblk(GDN_CONV))], [(_sds((t, nb * LANES), MXU_DTYPE), _chan(t, ident))],
                          [(_sds(w.shape), _wblk(GDN_CONV))])


def _gates_fn(s, avec, dtvec):
    lane = lax.broadcasted_iota(jnp.int32, s.shape, 1)
    beta = jax.nn.sigmoid(s)
    g = -jnp.exp(avec) * jax.nn.softplus(s + dtvec)
    nh = 2 * GDN_HEADS
    return jnp.where(lane < nh, beta, jnp.where(lane < 2 * nh, g, 0.0))


def _gates(name, p, avec, dtvec, col):
    t = p.shape[0]
    tm = min(512, t)
    return _blocked(name, _gates_fn, (t // tm,), [p, avec, dtvec], [_tok(tm, LANES, col), _res(avec.shape), _res(dtvec.shape)],
                    [_sds((t, LANES))], [_tok(tm, LANES)])[0]


def _gates_bwd(name, dgb, p, avec, dtvec, col):
    t = p.shape[0]
    tm = min(512, t)

    def fn(dgb, s, avec, dtvec):
        _, vjp = jax.vjp(_gates_fn, s, avec, dtvec)
        return vjp(dgb)

    return _blocked(name, fn, (t // tm,), [dgb, p, avec, dtvec],
                    [_tok(tm, LANES), _tok(tm, LANES, col), _res(avec.shape), _res(dtvec.shape)],
                    [_sds((t, LANES), MXU_DTYPE), _sds(avec.shape), _sds(dtvec.shape)],
                    [_tok(tm, LANES), _res(avec.shape), _res(dtvec.shape)], acc=(1, 2), acc_axis=0)


def _make_pool_tile(total):
    def tile(r0, x_ext, scale, *ws):
        ys = []
        for g, win in enumerate(POOL_WINDOWS):
            xg = x_ext[:, g * LANES:(g + 1) * LANES]
            rows = xg.shape[0] - 2 * HALO_SHORT
            pos = r0 + lax.broadcasted_iota(jnp.int32, (rows, LANES), 0)
            lo = jnp.clip(pos - win // 2, 0, total)
            hi = jnp.clip(pos - win // 2 + win, 0, total)
            window_sum = _pool_sums[win](xg, jnp.ones((win, LANES), F32))
            pooled = window_sum / (hi - lo).astype(F32) - _center(xg, HALO_SHORT)
            ys.append(_mm(pooled, ws[g]))
        return (jnp.concatenate(ys, axis=-1) * scale,)

    return tile


def _pool_specs(p, pool_w, pool_scale, col0):
    t = p.shape[0]
    n = len(POOL_WINDOWS) * LANES
    seq = (p, pl.BlockSpec((t, n), lambda i: (0, col0)))
    pars = [(pool_scale, _res(pool_scale.shape))]
    pars += [(pool_w, pl.BlockSpec((None, LANES, LANES), lambda i, g=g: (g, 0, 0))) for g in range(len(POOL_WINDOWS))]
    return t, n, seq, pars


def _pool(p, pool_w, pool_scale, col0):
    t, n, seq, pars = _pool_specs(p, pool_w, pool_scale, col0)
    return _seq_stage("pool", _make_pool_tile(t), HALO_SHORT, 1, [seq], pars, [(_sds((t, n)), _res((t, n)))])[0]


def _pool_bwd(dy, p, pool_w, pool_scale, col0):
    t, n, seq, pars = _pool_specs(p, pool_w, pool_scale, col0)
    wspec = (_sds((LANES, LANES)), _res((LANES, LANES)))
    return _seq_stage_bwd("pool_bwd", _make_pool_tile(t), HALO_SHORT, 1, [(dy, _res(dy.shape))], [seq], pars,
                          [(_sds((t, n), MXU_DTYPE), _res((t, n)))],
                          [(_sds(pool_scale.shape), _res(pool_scale.shape))] + [wspec] * len(POOL_WINDOWS))


def _ffn_conv_tile(r0, a_ext, gt_ext, w):
    cv = _conv3x3(a_ext, w)
    return _silu(cv) * _center(gt_ext, HALO_GRID), cv


def _ffn_conv_known_tile(r0, a_ext, gt_ext, cv_ext, w):
    cv = _conv3x3_known(a_ext, w, _center(cv_ext, HALO_GRID))
    return (_silu(cv) * _center(gt_ext, HALO_GRID),)


def _ffn_conv(name, h, w9):
    t = h.shape[0]
    nb = w9.shape[1] // LANES
    ident = lambda i: i
    return _seq_stage(name, _ffn_conv_tile, HALO_GRID, nb, [(h, _chan(t, ident)), (h, _chan(t, lambda i: nb + i))],
                      [(w9, _wblk(9))], [(_sds((t, nb * LANES), MXU_DTYPE), _chan(t, ident)), (_sds((t, nb * LANES)), _chan(t, ident))],
                      GRID_SEQ_TILE)


def _ffn_conv_bwd(name, ds, h, cv, w9):
    t = h.shape[0]
    nb = w9.shape[1] // LANES
    ident = lambda i: i
    o = (_sds((t, nb * LANES), MXU_DTYPE), _chan(t, ident))
    return _seq_stage_bwd(name, _ffn_conv_known_tile, HALO_GRID, nb, [(ds, _chan(t, ident))],
                          [(h, _chan(t, ident)), (h, _chan(t, lambda i: nb + i)), (cv, _chan(t, ident))], [(w9, _wblk(9))],
                          [o, o], [(_sds(w9.shape), _wblk(9))], GRID_SEQ_TILE, n_const=1)


def _sconv_tile(r0, gb_ext, gc_ext, h_ext, w):
    return (_center(gb_ext, HALO_SHORT) * _conv3(gc_ext * h_ext, w),)


def _sconv_specs(p1, w3):
    t = p1.shape[0]
    nb = w3.shape[1] // LANES
    seqs = [(p1, _chan(t, lambda i: i)), (p1, _chan(t, lambda i: nb + i)), (p1, _chan(t, lambda i: 2 * nb + i))]
    return t, nb, seqs, [(w3, _wblk(SC_WIDTH))]


def _sconv(p1, w3):
    t, nb, seqs, pars = _sconv_specs(p1, w3)
    return _seq_stage("sconv", _sconv_tile, HALO_SHORT, nb, seqs, pars, [(_sds((t, nb * LANES)), _chan(t, lambda i: i))])[0]


def _sconv_bwd(dysc, p1, w3):
    t, nb, seqs, pars = _sconv_specs(p1, w3)
    o = (_sds((t, nb * LANES), MXU_DTYPE), _chan(t, lambda i: i))
    return _seq_stage_bwd("sconv_bwd", _sconv_tile, HALO_SHORT, nb, [(dysc, _chan(t, lambda i: i))], seqs, pars,
                          [o, o, o], [(_sds(w3.shape), _wblk(SC_WIDTH))])


def _conf_tile(r0, ga_ext, gb_ext, w):
    return (_conv31(ga_ext * jax.nn.sigmoid(gb_ext), w),)


def _conf_specs(p1, w31, blk0):
    t = p1.shape[0]
    nb = w31.shape[1] // LANES
    seqs = [(p1, _chan(t, lambda i: blk0 + i)), (p1, _chan(t, lambda i: blk0 + nb + i))]
    return t, nb, seqs, [(w31, _wblk(CF_WIDTH))]


def _confconv(p1, w31, blk0):
    t, nb, seqs, pars = _conf_specs(p1, w31, blk0)
    return _seq_stage("confconv", _conf_tile, HALO_CONF, nb, seqs, pars, [(_sds((t, nb * LANES)), _chan(t, lambda i: i))])[0]


def _confconv_bwd(dz0, p1, w31, blk0):
    t, nb, seqs, pars = _conf_specs(p1, w31, blk0)
    o = (_sds((t, nb * LANES), MXU_DTYPE), _chan(t, lambda i: i))
    return _seq_stage_bwd("confconv_bwd", _conf_tile, HALO_CONF, nb, [(dz0, _chan(t, lambda i: i))], seqs, pars,
                          [o, o], [(_sds(w31.shape), _wblk(CF_WIDTH))])


def _split_bf16(x):
    head = x.astype(jnp.bfloat16)
    return head, (x - head.astype(F32)).astype(jnp.bfloat16)


def _bdot(a, b, ca, cb, hi):
    dims = (((ca,), (cb,)), ((0,), (0,)))
    if not hi:
        return lax.dot_general(a.astype(MXU_DTYPE), b.astype(MXU_DTYPE), dims, preferred_element_type=F32)
    if MXU_DTYPE == F32:
        return lax.dot_general(a, b, dims, preferred_element_type=F32, precision=HIGHEST)
    (ah, al), (bh, bl) = _split_bf16(a), _split_bf16(b)
    dot = functools.partial(lax.dot_general, dimension_numbers=dims, preferred_element_type=F32)
    return dot(ah, bh) + (dot(ah, bl) + dot(al, bh))


def _make_bmm(hi):
    nn_i = lambda a, b: _bdot(a, b, 2, 1, hi)
    nt_i = lambda a, b: _bdot(a, b, 2, 2, hi)
    tn_i = lambda a, b: _bdot(a, b, 1, 1, hi)
    nn = _mm_vjp(nn_i, lambda g, b: nt_i(g, b), lambda a, g: tn_i(a, g))
    nt = _mm_vjp(nt_i, lambda g, b: nn_i(g, b), lambda a, g: tn_i(g, a))
    tn = _mm_vjp(tn_i, lambda g, b: nt_i(b, g), lambda a, g: nn_i(a, g))
    return nn, nt, tn


_bmm, _bmm_nt, _bmm_tn = _make_bmm(False)
_bmm_hi, _bmm_hi_nt, _bmm_hi_tn = _make_bmm(True)


def _unit_tri_inverse(a):
    c = a.shape[-1]
    eye = (lax.broadcasted_iota(jnp.int32, a.shape, 1) == lax.broadcasted_iota(jnp.int32, a.shape, 2)).astype(F32)
    levels = max(1, int(math.ceil(math.log2(c))) - 1)
    p = eye - a
    m = _bdot(a, a, 2, 1, True)
    for level in range(levels):
        p_next = p + _bdot(p, m, 2, 1, True)
        if level + 1 < levels:
            m = _bdot(m, m, 2, 1, True)
        p = p_next
    return p


@jax.custom_vjp
def _known_inverse(a, tinv):
    return tinv


def _known_inverse_fwd(a, tinv):
    return tinv, tinv


def _known_inverse_bwd(tinv, dt):
    da = -_bdot(_bdot(tinv, dt, 1, 1, True), tinv, 2, 2, True)
    return da, jnp.zeros_like(tinv)


_known_inverse.defvjp(_known_inverse_fwd, _known_inverse_bwd)


def _gdn_chunk(q, k, v, gcol, grow, bcol, s, tinv=None):
    n, c, _ = q.shape
    shape = (n, c, c)
    fwd_dir = lax.broadcasted_iota(jnp.int32, shape, 0) < n // 2
    i = lax.broadcasted_iota(jnp.int32, shape, 1)
    j = lax.broadcasted_iota(jnp.int32, shape, 2)
    order = jnp.where(fwd_dir, i - j, j - i)
    incl = order >= 0
    gc_col = jnp.sum(jnp.where(incl, grow, 0.0), axis=2, keepdims=True)
    gc_row = jnp.sum(jnp.where(order <= 0, gcol, 0.0), axis=1, keepdims=True)
    gtot = jnp.sum(grow, axis=2, keepdims=True)
    decay = jnp.exp(jnp.where(incl, gc_col - gc_row, -1e30))
    kb = k * bcol
    a = jnp.where(order > 0, _bmm_nt(kb, k) * decay, 0.0)
    tinv = _unit_tri_inverse(a) if tinv is None else _known_inverse(a, tinv)
    e_col = jnp.exp(gc_col)
    u = _bmm_hi(tinv, v * bcol)
    w = _bmm_hi(tinv, kb * e_col)
    k_dec = k * jnp.exp(gtot - gc_col)
    v_new = u - _bmm(w, s)
    attn = jnp.where(incl, _bmm_nt(q, k) * decay, 0.0)
    o = _bmm(q * e_col, s) + _bmm(attn, v_new)
    s_next = s * jnp.exp(gtot) + _bmm_tn(k_dec, v_new)
    return o, s_next, tinv


N_CHAINS = 2 * GDN_HEADS
QKV_W = 3 * GDN_HEADS * HEAD_DIM


def _scan_specs(nc, step):
    c = GDN_CHUNK
    seq = [pl.BlockSpec((c, QKV_W), lambda i: (step(i), 0)), pl.BlockSpec((c, QKV_W), lambda i: (nc - 1 - step(i), 0))]
    heads = [pl.BlockSpec((c, GDN_HEADS * HEAD_DIM), lambda i: (step(i), 0)),
             pl.BlockSpec((c, GDN_HEADS * HEAD_DIM), lambda i: (nc - 1 - step(i), 0))]
    colv = pl.BlockSpec((N_CHAINS, None, c, 1), lambda i: (0, step(i), 0, 0))
    rowv = pl.BlockSpec((N_CHAINS, None, 1, c), lambda i: (0, step(i), 0, 0))
    state = pl.BlockSpec((N_CHAINS, HEAD_DIM, HEAD_DIM), lambda i: (0, 0, 0))
    saved = pl.BlockSpec((N_CHAINS, None, HEAD_DIM, HEAD_DIM), lambda i: (0, step(i), 0, 0))
    inv = pl.BlockSpec((N_CHAINS, None, c, c), lambda i: (0, step(i), 0, 0))
    return seq, heads, colv, rowv, state, saved, inv


def _head_cols(h, part):
    lo = (part * GDN_HEADS + h) * HEAD_DIM
    return slice(lo, lo + HEAD_DIM)


def _chain_stack(x0_ref, x1_ref, part):
    return jnp.stack([(x0_ref if n < GDN_HEADS else x1_ref)[:, _head_cols(n % GDN_HEADS, part)] for n in range(N_CHAINS)])


def _scan_fwd(name, qkv, bcol, gcol, grow, s0):
    t = qkv.shape[0]
    nc = t // GDN_CHUNK
    seq, heads, colv, rowv, state, saved, inv = _scan_specs(nc, lambda i: i)

    def body(qkv0_ref, qkv1_ref, b_ref, gc_ref, gr_ref, s0_ref, o0_ref, o1_ref, save_ref, tinv_ref, fin_ref, s_ref):
        @pl.when(pl.program_id(0) == 0)
        def _():
            s_ref[...] = s0_ref[...]

        s = s_ref[...]
        save_ref[...] = s
        q, k, v = [_chain_stack(qkv0_ref, qkv1_ref, part) for part in range(3)]
        o, s_next, tinv = _gdn_chunk(q, k, v, gc_ref[...], gr_ref[...], b_ref[...], s)
        for n in range(N_CHAINS):
            d, h = divmod(n, GDN_HEADS)
            (o0_ref if d == 0 else o1_ref)[:, _head_cols(h, 0)] = o[n]
        tinv_ref[...] = tinv
        s_ref[...] = s_next
        fin_ref[...] = s_next

    hw = GDN_HEADS * HEAD_DIM
    return pl.pallas_call(
        body, name=name, grid=(nc,), in_specs=seq + [colv, colv, rowv, state],
        out_specs=heads + [saved, inv, state],
        out_shape=[_sds((t, hw)), _sds((t, hw)), _sds((N_CHAINS, nc, HEAD_DIM, HEAD_DIM)),
                   _sds((N_CHAINS, nc, GDN_CHUNK, GDN_CHUNK)), _sds((N_CHAINS, HEAD_DIM, HEAD_DIM))],
        scratch_shapes=[pltpu.VMEM((N_CHAINS, HEAD_DIM, HEAD_DIM), F32)],
        compiler_params=pltpu.CompilerParams(dimension_semantics=("arbitrary",), vmem_limit_bytes=VMEM_LIMIT_BYTES),
    )(qkv, qkv, bcol, gcol, grow, s0)


def _scan_bwd(name, do, qkv, bcol, gcol, grow, saved_s, saved_inv, ds_last):
    t = qkv.shape[0]
    nc = t // GDN_CHUNK
    c = GDN_CHUNK
    seq, heads, colv, rowv, state, saved, inv = _scan_specs(nc, lambda i: nc - 1 - i)

    def body(do0_ref, do1_ref, qkv0_ref, qkv1_ref, b_ref, gc_ref, gr_ref, s_ref, tinv_ref, dsl_ref,
             dx0_ref, dx1_ref, db_ref, dgc_ref, dgr_ref, ds0_ref, ds_ref):
        @pl.when(pl.program_id(0) == 0)
        def _():
            ds_ref[...] = dsl_ref[...]

        tinv = tinv_ref[...]
        q, k, v = [_chain_stack(qkv0_ref, qkv1_ref, part) for part in range(3)]
        do = jnp.stack([(do0_ref if n < GDN_HEADS else do1_ref)[:, _head_cols(n % GDN_HEADS, 0)] for n in range(N_CHAINS)])
        _, vjp = jax.vjp(lambda *a: _gdn_chunk(*a, tinv=tinv)[:2], q, k, v, gc_ref[...], gr_ref[...], b_ref[...], s_ref[...])
        dq, dk, dv, dgc, dgr, db, ds = vjp((do, ds_ref[...]))
        for n in range(N_CHAINS):
            d, h = divmod(n, GDN_HEADS)
            dx_ref = dx0_ref if d == 0 else dx1_ref
            dx_ref[:, _head_cols(h, 0)] = dq[n]
            dx_ref[:, _head_cols(h, 1)] = dk[n]
            dx_ref[:, _head_cols(h, 2)] = dv[n]
        db_ref[...] = db
        dgc_ref[...] = dgc
        dgr_ref[...] = dgr
        ds_ref[...] = ds
        ds0_ref[...] = ds

    vec_c = _sds((N_CHAINS, nc, c, 1))
    vec_r = _sds((N_CHAINS, nc, 1, c))
    return pl.pallas_call(
        body, name=name, grid=(nc,),
        in_specs=heads + seq + [colv, colv, rowv, saved, inv, state],
        out_specs=seq + [colv, colv, rowv, state],
        out_shape=[_sds((t, QKV_W)), _sds((t, QKV_W)), vec_c, vec_c, vec_r, _sds((N_CHAINS, HEAD_DIM, HEAD_DIM))],
        scratch_shapes=[pltpu.VMEM((N_CHAINS, HEAD_DIM, HEAD_DIM), F32)],
        compiler_params=pltpu.CompilerParams(dimension_semantics=("arbitrary",), vmem_limit_bytes=VMEM_LIMIT_BYTES),
    )(do, do, qkv, qkv, bcol, gcol, grow, saved_s, saved_inv, ds_last)


def _gate_layouts(gb):
    t = gb.shape[0]
    nc = t // GDN_CHUNK
    nh = GDN_HEADS

    def by_scan_position(a):
        a = a.T.reshape(2 * nh, nc, GDN_CHUNK)
        return jnp.concatenate([a[:nh], a[nh:, ::-1]], axis=0)

    beta = by_scan_position(gb[:, :2 * nh])
    g = by_scan_position(gb[:, 2 * nh:4 * nh])
    return beta[..., None], g[..., None], g[:, :, None, :]


def _gate_layouts_bwd(dbcol, dgcol, dgrow):
    n2, nc, c, _ = dbcol.shape
    nh = n2 // 2
    t = nc * c

    def by_token(a):
        return jnp.concatenate([a[:nh], a[nh:, ::-1]], axis=0).reshape(n2, t).T

    dbeta = by_token(dbcol[..., 0])
    dg = by_token(dgcol[..., 0] + dgrow[:, :, 0, :])
    return jnp.concatenate([dbeta, dg, jnp.zeros((t, LANES - 2 * n2), F32)], axis=1)


def kernel(x, c, ctx, c_ctx, ada_w, ada_b, ln_g, ln_b, even_w_in, even_w_out, gdn_conv_w, gdn_a_log, gdn_dt_bias, gdn_norm_w, pool_w, pool_scale, odd_w_in, odd_w_out, sconv_w, conf_conv_w, conf_ln_g, conf_ln_b, ffn_w_up, ffn_conv_w, ffn_w_down, loss_target, m_c_ctx, m_ada_w, m_ada_b, m_ln_g, m_ln_b, m_even_w_in, m_even_w_out, m_gdn_conv_w, m_gdn_a_log, m_gdn_dt_bias, m_gdn_norm_w, m_pool_w, m_pool_scale, m_odd_w_in, m_odd_w_out, m_sconv_w, m_conf_conv_w, m_conf_ln_g, m_conf_ln_b, m_ffn_w_up, m_ffn_conv_w, m_ffn_w_down, v_c_ctx, v_ada_w, v_ada_b, v_ln_g, v_ln_b, v_even_w_in, v_even_w_out, v_gdn_conv_w, v_gdn_a_log, v_gdn_dt_bias, v_gdn_norm_w, v_pool_w, v_pool_scale, v_odd_w_in, v_odd_w_out, v_sconv_w, v_conf_conv_w, v_conf_ln_g, v_conf_ln_b, v_ffn_w_up, v_ffn_conv_w, v_ffn_w_down):
    names = ['c_ctx', 'ada_w', 'ada_b', 'ln_g', 'ln_b', 'even_w_in', 'even_w_out', 'gdn_conv_w', 'gdn_a_log',
             'gdn_dt_bias', 'gdn_norm_w', 'pool_w', 'pool_scale', 'odd_w_in', 'odd_w_out', 'sconv_w', 'conf_conv_w',
             'conf_ln_g', 'conf_ln_b', 'ffn_w_up', 'ffn_conv_w', 'ffn_w_down']
    loc = locals()
    wts = {n: loc[n] for n in names}
    mom = {n: loc['m_' + n] for n in names}
    var = {n: loc['v_' + n] for n in names}

    ix, iy, ic = lax.axis_index("x"), lax.axis_index("y"), lax.axis_index("c")
    chip = 2 * ix + iy
    dev = 2 * chip + ic
    d = D_MODEL
    x0, ctx0, tgt = x[0], ctx[0], loss_target[0]
    t, tc = x0.shape[0], ctx0.shape[0]
    depth = ada_w.shape[0]
    n_ada = ada_w.shape[2]

    small_sharded = [gdn_conv_w, sconv_w, conf_conv_w, ffn_conv_w, ln_g, ln_b]
    s1_items = [c] + small_sharded
    s1 = _pack_rows(s1_items)
    g1 = _all_gather8("gather_small_in", s1, True).reshape(8, -1, LANES)
    c_all = g1[:, :_packed_rows(c.shape)].reshape(8, d)
    per_chip = [_unpack_rows(g1[2 * k], [a.shape for a in s1_items])[1:] for k in range(4)]
    gdn_conv_f, sconv_f, conf_conv_f, ffn_conv_f, ln_g_f, ln_b_f = [
        jnp.concatenate([per_chip[k][i] for k in range(4)], axis=-1) for i in range(len(small_sharded))]
    c16 = jnp.concatenate([c_all, c_ctx[None], jnp.zeros((7, d), F32)], axis=0)

    mod_part = _mod_rows(c16, ada_w)
    g2 = _all_gather8("gather_mod", mod_part.reshape(-1, LANES), True).reshape(4, 2, depth, 16, n_ada)[:, 0]
    mod_all = jnp.transpose(g2, (1, 2, 0, 3)).reshape(depth, 16, 4 * n_ada) + ada_b[:, None, :]
    mod_me = lax.dynamic_index_in_dim(mod_all, dev, axis=1, keepdims=False).reshape(depth, 6, 1, d)
    sh_c, sc_c = mod_all[0, 8, :d][None], mod_all[0, 8, d:2 * d][None]

    flat_names = ['even_w_in', 'even_w_out']
    flat_shapes = [wts[n].shape for n in flat_names]
    half_mult = 2 * 16 * LANES
    rh = -(-sum(math.prod(s) for s in flat_shapes) // half_mult) * half_mult // (2 * LANES)
    flat_pack = _pack_flat([wts[n] for n in flat_names], half_mult, MXU_DTYPE).reshape(2, rh, LANES)
    wg_flat = _all_gather8("gather_weights_even", lax.dynamic_index_in_dim(flat_pack, ic, axis=0, keepdims=False), True).reshape(4, -1)
    per_chip = [_unpack_flat(wg_flat[k], flat_shapes) for k in range(4)]
    n_even = 4 * even_w_in.shape[1]
    n_even_pad = -(-n_even // LANES) * LANES
    w_in = jnp.concatenate([per_chip[k][0] for k in range(4)] + [jnp.zeros((d, n_even_pad - n_even), MXU_DTYPE)], axis=1)
    w_out = jnp.concatenate([per_chip[k][1] for k in range(4)], axis=0)

    def row_half(a, core, axis):
        n = a.shape[axis] // 2
        return lax.dynamic_slice_in_dim(a, core * n, n, axis=axis)

    def layer_of(a, core):
        return lax.dynamic_index_in_dim(a, core, axis=0, keepdims=False)

    mine = [row_half(odd_w_in, ic, 0), row_half(odd_w_out, ic, 0), layer_of(ffn_w_up, ic), layer_of(ffn_w_down, ic)]
    late_sems, late_mine, late_lands, late_token = _chip_send_start(
        "gather_late_start", [m.astype(MXU_DTYPE) for m in mine], [wg_flat, mod_all])
    scal_blk = (n_even // LANES)
    n_scal = n_even - scal_blk * LANES

    def mod(layer, k):
        return mod_me[layer, k]

    avec = jnp.zeros((1, LANES), F32).at[0, n_scal // 2:n_scal].set(gdn_a_log.reshape(-1))
    dtvec = jnp.zeros((1, LANES), F32).at[0, n_scal // 2:n_scal].set(gdn_dt_bias.reshape(-1))
    normw = gdn_norm_w[None]
    pscale = pool_scale[None]
    cg, cb = conf_ln_g[None], conf_ln_b[None]
    lng = lambda l, k: ln_g_f[l, k][None]
    lnb = lambda l, k: ln_b_f[l, k][None]
    convw9 = ffn_conv_f.reshape(depth, 9, -1)
    nqkv = gdn_conv_f.shape[1]

    shift0 = mod(0, 0) + late_token[:1, :1]
    p, ub0 = _inproj("even_in", x0, shift0, mod(0, 1), w_in, 512)
    pc, ucb = _inproj("even_in_ctx", ctx0, sh_c, sc_c, w_in, 256)
    qkv = _qkv_conv("qkv_conv", p, gdn_conv_f)
    qkv_c = _qkv_conv("qkv_conv_ctx", pc, gdn_conv_f)
    gb = _gates("gates", p, avec, dtvec, scal_blk)
    gb_c = _gates("gates_ctx", pc, avec, dtvec, scal_blk)
    lay = _gate_layouts(gb)
    lay_c = _gate_layouts(gb_c)
    s_zero = jnp.zeros((2 * GDN_HEADS, HEAD_DIM, HEAD_DIM), F32)
    _, _, save_c, inv_c, s_ctx = _scan_fwd("scan_ctx", qkv_c, *lay_c, s_zero)
    o0, o1, save_l, inv_l, _ = _scan_fwd("scan", qkv, *lay, s_ctx)
    pool_blk = (nqkv + GDN_HEADS * HEAD_DIM) // 512
    ypool = _pool(p, pool_w, pscale, pool_blk)
    x1, mix0, y0 = _even_out(o0, o1, p, ypool, x0, normw, mod(0, 2), lng(0, 0), lnb(0, 0), w_out, 512)

    late_mine, late_lands = _chip_send_wait("gather_late_wait", late_sems, late_mine, late_lands, x1)
    own4 = [lax.dynamic_update_slice(land, m[None], (chip,) + (0,) * m.ndim) for land, m in zip(late_lands, late_mine)]
    got4 = _sibling_exchange("gather_late_exchange", own4)

    def of_core(i, core):
        return jnp.where(ic == core, own4[i], got4[i])

    w_oin = jnp.concatenate([of_core(0, 0), of_core(0, 1)], axis=1)
    w_oout = jnp.concatenate([of_core(1, 0), of_core(1, 1)], axis=1).reshape(-1, d)
    w_up = [of_core(2, l) for l in range(depth)]
    w_down = [of_core(3, l) for l in range(depth)]

    def ffn_fwd(l, xin):
        h, ub = _inproj(f"ffn_up{l}", xin, mod(l, 3), mod(l, 4), w_up[l], 256)
        s, cv = _ffn_conv(f"ffn_conv{l}", h, convw9[l])
        xo, y = _ffn_down(f"ffn_down{l}", s, xin, mod(l, 5), lng(l, 1), lnb(l, 1), w_down[l], 512)
        return xo, (h, cv, ub, s, y)

    x2, ffn0 = ffn_fwd(0, x1)
    p1, ub1 = _inproj("odd_in", x2, mod(1, 0), mod(1, 1), w_oin, 512)
    nsc = sconv_f.shape[1] // LANES
    ysc = _sconv(p1, sconv_f)
    z0 = _confconv(p1, conf_conv_f, 3 * nsc)
    x3, mix1, y1 = _odd_out(ysc, z0, x2, cg, cb, mod(1, 2), lng(1, 0), lnb(1, 0), w_oout, 512)
    x4, ffn1 = ffn_fwd(1, x3)
    loss_part, dx4 = _loss_and_grad(x4, tgt, 512)
    loss = lax.psum(loss_part[0, 0], ("x", "y", "c"))

    dmod = [[None] * 6 for _ in range(depth)]
    dlng = [[None, None] for _ in range(depth)]
    dlnb = [[None, None] for _ in range(depth)]
    gbig = {}
    dconv9 = [None] * depth

    def ffn_bwd(l, dxo, xin, saved):
        h, cv, ub, s, y = saved
        ds, dxa, dyb, dgt, dg_, db_ = _ffn_down_bwd(f"ffn_down_bwd{l}", dxo, xin, y, mod(l, 5), lng(l, 1), lnb(l, 1), w_down[l], 512)
        da, dgate, dw9 = _ffn_conv_bwd(f"ffn_conv_bwd{l}", ds, h, cv, convw9[l])
        dh = jnp.concatenate([da, dgate], axis=1)
        dxin, dsh, dsc = _inproj_bwd(f"ffn_up_bwd{l}", dh, xin, mod(l, 3), mod(l, 4), w_up[l], dxa, 256)
        dmod[l][3], dmod[l][4], dmod[l][5] = dsh, dsc, dgt
        dlng[l][1], dlnb[l][1] = dg_, db_
        dconv9[l] = dw9
        dw_up = _matmul_tn(f"dw_up{l}", [(ub, dh)], chip_cols=ffn_w_up.shape[2])
        dw_down = _matmul_tn(f"dw_down{l}", [(s, dyb)]).reshape(4, -1, d)
        return dxin, dw_up, dw_down

    dx3, dwu1, dwd1 = ffn_bwd(1, dx4, x3, ffn1)
    dysc, dz0, dx2a, dyb1, dcg, dcb, dgt, dg_, db_ = _odd_out_bwd(dx3, ysc, z0, x2, y1, cg, cb, mod(1, 2), lng(1, 0), lnb(1, 0), w_oout, 512)
    dmod[1][2], dlng[1][0], dlnb[1][0] = dgt, dg_, db_
    d_gb, d_gc, d_h, dsconv = _sconv_bwd(dysc, p1, sconv_f)
    d_ga, d_gbb, dconf = _confconv_bwd(dz0, p1, conf_conv_f, 3 * nsc)
    dp1 = jnp.concatenate([d_gb, d_gc, d_h, d_ga, d_gbb], axis=1)
    dx2, dsh, dsc = _inproj_bwd("odd_in_bwd", dp1, x2, mod(1, 0), mod(1, 1), w_oin, dx2a, 512)
    dmod[1][0], dmod[1][1] = dsh, dsc
    dw_oout = _matmul_tn("dw_oout", [(mix1, dyb1)]).reshape(4, -1, d)
    dw_oin = _matmul_tn("dw_oin", [(ub1, dp1)], chip_cols=odd_w_in.shape[1])

    dx1, dwu0, dwd0 = ffn_bwd(0, dx2, x1, ffn0)

    tiled_halves = [
        (dw_oin[:, :d // 2], dw_oin[:, d // 2:]),
        (dw_oout[:, :odd_w_out.shape[0] // 2], dw_oout[:, odd_w_out.shape[0] // 2:]),
        (dwu0, dwu1),
        (dwd0, dwd1),
    ]
    keeps = [jnp.where(ic == 0, h0, h1) for h0, h1 in tiled_halves]
    gives = [jnp.where(ic == 0, h1, h0) for h0, h1 in tiled_halves]
    gots = _sibling_exchange("grad_pair_exchange", gives)
    pairs = [_sum_pair(f"grad_pair_sum{i}", kp, gt_, MXU_DTYPE) for i, (kp, gt_) in enumerate(zip(keeps, gots))]
    gs_sems, gs_pairs, gs_lands, gs_token = _chip_send_start("grad_scatter_start", pairs, [], per_peer=True)
    gate0 = mod(0, 2) + gs_token[:1, :1]

    do, dpg, dypool, dx0a, dyb0, dnormw, dgt, dg_, db_ = _even_out_bwd(dx1, o0, o1, p, ypool, x0, y0, normw, gate0, lng(0, 0), lnb(0, 0), w_out, 512)
    dmod[0][2], dlng[0][0], dlnb[0][0] = dgt, dg_, db_
    pool_cts = _pool_bwd(dypool, p, pool_w, pscale, pool_blk)
    dpp, dpool_scale, dpool_w = pool_cts[0], pool_cts[1], jnp.stack(pool_cts[2:])
    dqkv0, dqkv1, dbcol, dgcol, dgrow, ds0 = _scan_bwd("scan_bwd", do, qkv, *lay, save_l, inv_l, s_zero)
    zero_do = jnp.zeros((tc, GDN_HEADS * HEAD_DIM), F32)
    dqkv0_c, dqkv1_c, dbcol_c, dgcol_c, dgrow_c, _ = _scan_bwd("scan_bwd_ctx", zero_do, qkv_c, *lay_c, save_c, inv_c, ds0)
    dgb = _gate_layouts_bwd(dbcol, dgcol, dgrow)
    dgb_c = _gate_layouts_bwd(dbcol_c, dgcol_c, dgrow_c)
    dps, davec, ddtvec = _gates_bwd("gates_bwd", dgb, p, avec, dtvec, scal_blk)
    dps_c, davec_c, ddtvec_c = _gates_bwd("gates_bwd_ctx", dgb_c, pc, avec, dtvec, scal_blk)
    dpqkv, dconv5 = _qkv_conv_bwd("qkv_conv_bwd", dqkv0, dqkv1, p, gdn_conv_f)
    dpqkv_c, dconv5_c = _qkv_conv_bwd("qkv_conv_bwd_ctx", dqkv0_c, dqkv1_c, pc, gdn_conv_f)
    dp = jnp.concatenate([dpqkv, dpg, dpp, dps], axis=1)
    dpc = jnp.concatenate([dpqkv_c, jnp.zeros((tc, n_even_pad - nqkv - LANES), MXU_DTYPE), dps_c], axis=1)
    grad_x, dsh, dsc = _inproj_bwd("even_in_bwd", dp, x0, mod(0, 0), mod(0, 1), w_in, dx0a, 512)
    dmod[0][0], dmod[0][1] = dsh, dsc
    _, dsh_c, dsc_c = _inproj_bwd("even_in_bwd_ctx", dpc, ctx0, sh_c, sc_c, w_in, None, 256)
    dw_in = _matmul_tn("dw_in", [(ub0, dp), (ucb, dpc)])
    dw_out = _matmul_tn("dw_out", [(mix0, dyb0)])

    gs_pairs, gs_lands = _chip_send_wait("grad_scatter_wait", gs_sems, gs_pairs, gs_lands, dw_in, per_peer=True)
    own4 = [lax.dynamic_update_slice(land, lax.dynamic_index_in_dim(pr, chip, axis=0, keepdims=True), (chip,) + (0,) * (pr.ndim - 1))
            for land, pr in zip(gs_lands, gs_pairs)]
    n_in, n_out = even_w_in.shape[1], even_w_out.shape[0]
    gflat = jnp.stack([_pack_flat([dw_in[:, k * n_in:(k + 1) * n_in], dw_out[k * n_out:(k + 1) * n_out]], half_mult,
                                  MXU_DTYPE).reshape(2, rh, LANES) for k in range(4)], axis=1)
    got_flat, *got4 = _sibling_exchange("grad_late_exchange", [jnp.where(ic == 0, gflat[1], gflat[0])] + own4)
    pair_flat = _sum_pair("grad_pair_sum_flat", jnp.where(ic == 0, gflat[0], gflat[1]), got_flat, MXU_DTYPE)
    sum_flat = _sum_axis1("grad_chip_sum_flat", _chip_scatter("grad_chip_scatter", [pair_flat])[0])

    def both_halves(i):
        mine_sum = _sum_axis1(f"grad_chip_sum{i}_own", own4[i][None])[0]
        sib_sum = _sum_axis1(f"grad_chip_sum{i}_sibling", got4[i][None])[0]
        return jnp.where(ic == 0, jnp.stack([mine_sum, sib_sum]), jnp.stack([sib_sum, mine_sum]))

    g_even_in, g_even_out = _unpack_flat(sum_flat.reshape(-1), flat_shapes)
    g_shards = {'even_w_in': g_even_in, 'even_w_out': g_even_out, 'odd_w_in': both_halves(0).reshape(odd_w_in.shape),
                'odd_w_out': both_halves(1).reshape(odd_w_out.shape), 'ffn_w_up': both_halves(2), 'ffn_w_down': both_halves(3)}

    dmod_rows = jnp.stack([jnp.concatenate(dmod[l], axis=1)[0] for l in range(depth)])
    dmod_c = jnp.concatenate([dsh_c[0], dsc_c[0], jnp.zeros((4 * d,), F32)])
    dmod_c_rows = jnp.stack([dmod_c] + [jnp.zeros_like(dmod_c)] * (depth - 1))
    small_g = {
        'ln_g': jnp.stack([jnp.stack([dlng[l][k][0] for k in range(2)]) for l in range(depth)]),
        'ln_b': jnp.stack([jnp.stack([dlnb[l][k][0] for k in range(2)]) for l in range(depth)]),
        'gdn_conv_w': dconv5 + dconv5_c,
        'gdn_a_log': (davec + davec_c)[0, n_scal // 2:n_scal].reshape(gdn_a_log.shape),
        'gdn_dt_bias': (ddtvec + ddtvec_c)[0, n_scal // 2:n_scal].reshape(gdn_dt_bias.shape),
        'gdn_norm_w': dnormw[0], 'pool_w': dpool_w, 'pool_scale': dpool_scale[0],
        'sconv_w': dsconv, 'conf_conv_w': dconf, 'conf_ln_g': dcg[0], 'conf_ln_b': dcb[0],
        'ffn_conv_w': jnp.stack(dconv9).reshape(depth, 3, 3, -1),
    }
    small_names = list(small_g)
    s3_items = [dmod_rows, dmod_c_rows] + [small_g[n] for n in small_names]
    s3 = _pack_rows(s3_items)
    g3 = _all_gather8("gather_small_grads", s3, True).reshape(8, -1, LANES)
    tot3 = _sum_axis1("small_grad_sum", g3[None])[0]
    tot_items = _unpack_rows(tot3, [a.shape for a in s3_items])
    dmod_sum, dmod_c_sum = tot_items[0], tot_items[1]
    small_tot = dict(zip(small_names, tot_items[2:]))
    grad_ada_b = dmod_sum + dmod_c_sum
    rows_all = g3[:, :_packed_rows(dmod_rows.shape)].reshape(8, depth, 6 * d)
    cols = lax.dynamic_slice_in_dim(rows_all, chip * n_ada, n_ada, axis=2)
    crow = lax.dynamic_slice_in_dim(dmod_c_sum, chip * n_ada, n_ada, axis=1)
    dmod16 = jnp.concatenate([jnp.transpose(cols, (1, 0, 2)), crow[:, None, :], jnp.zeros((depth, 7, n_ada), F32)], axis=1)
    grad_ada_w, dsil = _ada_grads(c16, dmod16, ada_w)
    s4 = jnp.concatenate([dsil[0, 8][None], jnp.zeros((7, d), F32)], axis=0).reshape(-1, LANES)
    g4 = _all_gather8("gather_cctx", s4, True).reshape(8, 8, d)
    grad_c_ctx = _cctx_grad(g4[0::2, 0][:, None, :], c_ctx[None])[0]

    def my_cols(a, n):
        return lax.dynamic_slice_in_dim(a, chip * n, n, axis=a.ndim - 1)

    grads = dict(g_shards)
    grads['c_ctx'] = grad_c_ctx
    grads['ada_w'] = grad_ada_w
    grads['ada_b'] = grad_ada_b
    for n in ['ln_g', 'ln_b', 'gdn_conv_w', 'sconv_w', 'conf_conv_w', 'ffn_conv_w']:
        grads[n] = my_cols(small_tot[n], wts[n].shape[-1])
    for n in ['gdn_a_log', 'gdn_dt_bias', 'gdn_norm_w', 'pool_w', 'pool_scale', 'conf_ln_g', 'conf_ln_b']:
        grads[n] = small_tot[n]

    delta, new_m, new_v = {}, {}, {}
    big_adam = list(g_shards) + ['ada_w']
    for n in big_adam:
        shp = wts[n].shape
        as2d = lambda a: a.reshape(-1, shp[-1])
        dl, nm, nv = _adamw("adamw_" + n, as2d(wts[n]), as2d(grads[n]), as2d(mom[n]), as2d(var[n]))
        delta[n], new_m[n], new_v[n] = dl.reshape(shp), nm.reshape(shp), nv.reshape(shp)
    small_adam = [n for n in names if n not in big_adam]
    for n in small_adam:
        shp = wts[n].shape
        whole = lambda a: a.reshape((1,) + shp) if len(shp) == 1 else a
        dl, nm, nv = _adamw("adamw_" + n, whole(wts[n]), whole(grads[n]), whole(mom[n]), whole(var[n]))
        delta[n], new_m[n], new_v[n] = dl.reshape(shp), nm.reshape(shp), nv.reshape(shp)

    return (loss, grad_x[None], *[grads[n] for n in names], *[delta[n] for n in names],
            *[new_m[n] for n in names], *[new_v[n] for n in names])
```

```python
import functools
import math

import jax
import jax.numpy as jnp
from jax import lax
from jax.experimental import pallas as pl
from jax.experimental.pallas import tpu as pltpu

F32 = jnp.float32
MXU_DTYPE = jnp.bfloat16
HIGHEST = lax.Precision.HIGHEST
MESH = pl.DeviceIdType.MESH

D_MODEL = 1024
GRID_W = 64
GDN_HEADS = 4
HEAD_DIM = 128
GDN_CHUNK = 64
POOL_WINDOWS = (2, 4, 8, 16)
GDN_CONV = 5
SC_WIDTH = 3
CF_WIDTH = 31
ALPHA = 4.0 ** 0.25
LN_EPS = 1e-5
RMS_EPS = 1e-6
LANES = 128
VMEM_LIMIT_BYTES = 58 * 1024 * 1024

ADAM_LR, ADAM_B1, ADAM_B2, ADAM_EPS, ADAM_WD, ADAM_STEP = 0.001, 0.9, 0.999, 1e-08, 0.01, 10


def _blocked(name, fn, grid, ins, in_specs, out_shapes, out_specs, acc=(), acc_axis=None, scratch=()):
    n_in = len(ins)
    n_out = len(out_shapes)

    def body(*refs):
        vals = [r[...] for r in refs[:n_in]]
        res = fn(*vals, *refs[n_in + n_out:])
        if not isinstance(res, (tuple, list)):
            res = (res,)
        for k, (r, v) in enumerate(zip(refs[n_in:n_in + n_out], res)):
            if k in acc:
                first = pl.program_id(acc_axis) == 0

                @pl.when(first)
                def _(r=r, v=v):
                    r[...] = v.astype(r.dtype)

                @pl.when(jnp.logical_not(first))
                def _(r=r, v=v):
                    r[...] += v.astype(r.dtype)
            else:
                r[...] = v.astype(r.dtype)

    return pl.pallas_call(
        body, name=name, grid=grid, in_specs=in_specs, out_specs=out_specs, out_shape=out_shapes,
        scratch_shapes=list(scratch),
        compiler_params=pltpu.CompilerParams(dimension_semantics=("arbitrary",) * len(grid),
                                             vmem_limit_bytes=VMEM_LIMIT_BYTES),
    )(*ins)


def _sds(shape, dtype=F32):
    return jax.ShapeDtypeStruct(tuple(shape), dtype)


def _tok(tm, n, col=0):
    return pl.BlockSpec((tm, n), lambda t: (t, col))


def _res(shape):
    nd = len(shape)
    return pl.BlockSpec(tuple(shape), lambda t: (0,) * nd)


def _silu(x):
    return x * jax.nn.sigmoid(x)


def _layernorm(r, g, b):
    mu = jnp.mean(r, -1, keepdims=True)
    d = r - mu
    var = jnp.mean(d * d, -1, keepdims=True)
    return d * lax.rsqrt(var + LN_EPS) * g + b


def _mm_nn_impl(a, b):
    return jnp.dot(a.astype(MXU_DTYPE), b.astype(MXU_DTYPE), preferred_element_type=F32)


def _mm_nt_impl(a, b):
    return lax.dot_general(a.astype(MXU_DTYPE), b.astype(MXU_DTYPE), (((1,), (1,)), ((), ())), preferred_element_type=F32)


def _mm_tn_impl(a, b):
    return lax.dot_general(a.astype(MXU_DTYPE), b.astype(MXU_DTYPE), (((0,), (0,)), ((), ())), preferred_element_type=F32)


def _mm_vjp(mm, mm_da, mm_db):
    f = jax.custom_vjp(mm)
    f.defvjp(lambda a, b: (mm(a, b), (a, b)), lambda res, g: (mm_da(g, res[1]), mm_db(res[0], g)))
    return f


_mm = _mm_vjp(_mm_nn_impl, lambda g, b: _mm_nt_impl(g, b), lambda a, g: _mm_tn_impl(a, g))
_mm_nt = _mm_vjp(_mm_nt_impl, lambda g, b: _mm_nn_impl(g, b), lambda a, g: _mm_tn_impl(g, a))
_mm_tn = _mm_vjp(_mm_tn_impl, lambda g, b: _mm_nt_impl(b, g), lambda a, g: _mm_nn_impl(a, g))


def _row(w, k):
    rows = lax.broadcasted_iota(jnp.int32, w.shape, 0)
    return jnp.sum(jnp.where(rows == k, w, 0.0), axis=0, keepdims=True)


def _col_mask(shape, dc):
    col = lax.broadcasted_iota(jnp.int32, shape, 0) & (GRID_W - 1)
    return (col + dc >= 0) & (col + dc < GRID_W)


def _center(x_ext, halo):
    return x_ext[halo:x_ext.shape[0] - halo]


def _make_dwconv(taps, halo):
    assert all(abs(s) <= halo for s, _ in taps)

    def shifted(x_ext, s):
        r = x_ext if s == 0 else pltpu.roll(x_ext, (-s) % x_ext.shape[0], 0)
        return _center(r, halo)

    @jax.custom_vjp
    def conv(x_ext, w):
        acc = None
        for k, (s, dc) in enumerate(taps):
            r = shifted(x_ext, s)
            if dc != 0:
                r = jnp.where(_col_mask(r.shape, dc), r, 0.0)
            term = r * _row(w, k)
            acc = term if acc is None else acc + term
        return acc

    def fwd(x_ext, w):
        return conv(x_ext, w), (x_ext, w)

    def bwd(res, dy):
        x_ext, w = res
        n = x_ext.shape[0]
        rows = lax.broadcasted_iota(jnp.int32, w.shape, 0)
        pad = jnp.zeros((halo, dy.shape[1]), F32)
        dx = None
        dw = jnp.zeros(w.shape, F32)
        for k, (s, dc) in enumerate(taps):
            dym = dy if dc == 0 else jnp.where(_col_mask(dy.shape, dc), dy, 0.0)
            dw = dw + jnp.where(rows == k, jnp.sum(dym * shifted(x_ext, s), axis=0, keepdims=True), 0.0)
            t = jnp.concatenate([pad, dym * _row(w, k), pad], axis=0)
            if s != 0:
                t = pltpu.roll(t, s % n, 0)
            dx = t if dx is None else dx + t
        return dx, dw

    conv.defvjp(fwd, bwd)

    @jax.custom_vjp
    def known(x_ext, w, y):
        return y

    known.defvjp(lambda x_ext, w, y: (y, (x_ext, w)), lambda res, dy: (*bwd(res, dy), jnp.zeros_like(dy)))
    return conv, known


def _taps_1d(width):
    return tuple((k - width // 2, 0) for k in range(width))


HALO_SHORT = 8
HALO_CONF = 16
HALO_GRID = 72
_conv5 = _make_dwconv(_taps_1d(GDN_CONV), HALO_SHORT)[0]
_conv3 = _make_dwconv(_taps_1d(SC_WIDTH), HALO_SHORT)[0]
_conv31 = _make_dwconv(_taps_1d(CF_WIDTH), HALO_CONF)[0]
_conv3x3, _conv3x3_known = _make_dwconv(tuple((dr * GRID_W + dc, dc) for dr in (-1, 0, 1) for dc in (-1, 0, 1)), HALO_GRID)
_pool_sums = {w: _make_dwconv(tuple((s, 0) for s in range(-(w // 2), w - w // 2)), HALO_SHORT)[0] for w in POOL_WINDOWS}


def _all_gather8(name, blk, in_vmem):
    m_per, n = blk.shape
    space = pltpu.VMEM if in_vmem else pl.ANY

    def body(x_ref, out_ref, send_sems, recv_sems, local_sem):
        x, y, c = lax.axis_index("x"), lax.axis_index("y"), lax.axis_index("c")
        me, sibling = (x, y, c), (x, y, 1 - c)
        chips = [(1 - x, y), (x, 1 - y), (1 - x, 1 - y)]

        def rows(px, py, pc):
            return out_ref.at[pl.ds((4 * px + 2 * py + pc) * m_per, m_per), :]

        def copy(k, block, to, src=None):
            return pltpu.make_async_remote_copy(
                src_ref=rows(*block) if src is None else src, dst_ref=rows(*block),
                send_sem=send_sems.at[k], recv_sem=recv_sems.at[k], device_id=to, device_id_type=MESH)

        mine = pltpu.make_async_copy(x_ref, rows(*me), local_sem)
        mine.start()
        first = [copy(0, me, sibling, src=x_ref)]
        first += [copy(1 + j, me, (*chip, c), src=x_ref) for j, chip in enumerate(chips)]
        for cp in first:
            cp.start()
        passed = [copy(4 + j, (*chip, c), sibling) for j, chip in enumerate(chips)]
        for j, chip in enumerate(chips):
            copy(1 + j, (*chip, c), me).wait_recv()
            passed[j].start()
        copy(0, sibling, me).wait_recv()
        for j, chip in enumerate(chips):
            copy(4 + j, (*chip, 1 - c), me).wait_recv()
        for cp in first + passed:
            cp.wait_send()
        mine.wait()

    return pl.pallas_call(
        body, name=name, out_shape=_sds((8 * m_per, n), blk.dtype),
        in_specs=[pl.BlockSpec(memory_space=space)], out_specs=pl.BlockSpec(memory_space=space),
        scratch_shapes=[pltpu.SemaphoreType.DMA((7,)), pltpu.SemaphoreType.DMA((7,)), pltpu.SemaphoreType.DMA],
        compiler_params=pltpu.CompilerParams(vmem_limit_bytes=VMEM_LIMIT_BYTES),
    )(blk)


_ANY = pl.BlockSpec(memory_space=pl.ANY)


def _sibling_exchange(name, sends):
    n = len(sends)

    def body(*refs):
        x, y, c = lax.axis_index("x"), lax.axis_index("y"), lax.axis_index("c")
        send_sems, recv_sems = refs[2 * n:]
        copies = [pltpu.make_async_remote_copy(src_ref=refs[i], dst_ref=refs[n + i], send_sem=send_sems.at[i],
                                               recv_sem=recv_sems.at[i], device_id=(x, y, 1 - c), device_id_type=MESH)
                  for i in range(n)]
        for cp in copies:
            cp.start()
        for cp in copies:
            cp.wait()

    return pl.pallas_call(
        body, name=name, out_shape=[_sds(s.shape, s.dtype) for s in sends],
        in_specs=[_ANY] * n, out_specs=[_ANY] * n,
        scratch_shapes=[pltpu.SemaphoreType.DMA((n,)), pltpu.SemaphoreType.DMA((n,))],
    )(*sends)


def _chip_scatter(name, parts):
    n = len(parts)

    def body(*refs):
        p_refs, out_refs = refs[:n], refs[n:2 * n]
        send_sems, recv_sems, local_sems = refs[2 * n:]
        x, y, c = lax.axis_index("x"), lax.axis_index("y"), lax.axis_index("c")
        sibling = (x, y, 1 - c)
        me_chip = 2 * x + y
        chips = [(1 - x, y), (x, 1 - y), (1 - x, 1 - y)]

        def chip_id(chip):
            return 2 * chip[0] + chip[1]

        def copy(i, k, src, dst, to):
            return pltpu.make_async_remote_copy(src_ref=src, dst_ref=dst, send_sem=send_sems.at[i, k],
                                                recv_sem=recv_sems.at[i, k], device_id=to, device_id_type=MESH)

        mine = [pltpu.make_async_copy(p_refs[i].at[me_chip], out_refs[i].at[c, me_chip], local_sems.at[i]) for i in range(n)]
        first = []
        for i in range(n):
            first.append(copy(i, 0, p_refs[i].at[me_chip], out_refs[i].at[c, me_chip], sibling))
            first += [copy(i, 1 + j, p_refs[i].at[chip_id(chip)], out_refs[i].at[c, me_chip], (*chip, c))
                      for j, chip in enumerate(chips)]
        for cp in mine + first:
            cp.start()
        passed = []
        for j, chip in enumerate(chips):
            for i in range(n):
                landed = out_refs[i].at[c, chip_id(chip)]
                copy(i, 1 + j, p_refs[i].at[0], landed, sibling).wait_recv()
                passed.append(copy(i, 4 + j, landed, landed, sibling))
                passed[-1].start()
        for i in range(n):
            copy(i, 0, p_refs[i].at[0], out_refs[i].at[1 - c, me_chip], sibling).wait_recv()
            for j, chip in enumerate(chips):
                copy(i, 4 + j, p_refs[i].at[0], out_refs[i].at[1 - c, chip_id(chip)], sibling).wait_recv()
        for cp in first + passed:
            cp.wait_send()
        for cp in mine:
            cp.wait()

    return pl.pallas_call(
        body, name=name, out_shape=[_sds((2,) + p.shape, p.dtype) for p in parts],
        in_specs=[_ANY] * n, out_specs=[_ANY] * n,
        scratch_shapes=[pltpu.SemaphoreType.DMA((n, 7)), pltpu.SemaphoreType.DMA((n, 7)), pltpu.SemaphoreType.DMA((n,))],
    )(*parts)


_HBM = pl.BlockSpec(memory_space=pltpu.HBM)
_SEM = pl.BlockSpec(memory_space=pltpu.SEMAPHORE)
_SPLIT_CALL = pltpu.CompilerParams(has_side_effects=pltpu.SideEffectType.DATAFLOW_SIDE_EFFECTING)


def _peer_chips():
    x, y = lax.axis_index("x"), lax.axis_index("y")
    return [(1 - x, y), (x, 1 - y), (1 - x, 1 - y)]


def _chip_send_start(name, blocks, before, per_peer=False):
    n, nb = len(blocks), len(before)

    def body(*refs):
        x_refs, land_refs = refs[:n], refs[n:2 * n]
        sems, token = refs[2 * n + nb:2 * n + nb + 6 * n], refs[-1]
        c = lax.axis_index("c")
        me_chip = 2 * lax.axis_index("x") + lax.axis_index("y")
        for i in range(n):
            for j, chip in enumerate(_peer_chips()):
                src = x_refs[i].at[2 * chip[0] + chip[1]] if per_peer else x_refs[i]
                pltpu.make_async_remote_copy(src_ref=src, dst_ref=land_refs[i].at[me_chip], send_sem=sems[6 * i + j],
                                             recv_sem=sems[6 * i + 3 + j], device_id=(*chip, c), device_id_type=MESH).start()
        token[...] = jnp.zeros(token.shape, token.dtype)

    lands = [lax.empty(b.shape if per_peer else (4,) + b.shape, b.dtype) for b in blocks]
    hbm = [pltpu.with_memory_space_constraint(a, pltpu.HBM) for a in list(blocks) + lands]
    outs = pl.pallas_call(
        body, name=name,
        out_shape=tuple([pltpu.SemaphoreType.DMA(())] * (6 * n)) + tuple(pltpu.HBM(a.shape, a.dtype) for a in hbm) + (_sds((8, LANES)),),
        in_specs=tuple([_HBM] * (2 * n) + [_ANY] * nb),
        out_specs=tuple([_SEM] * (6 * n) + [_HBM] * (2 * n) + [pl.BlockSpec(memory_space=pltpu.VMEM)]),
        input_output_aliases={i: 6 * n + i for i in range(2 * n)}, compiler_params=_SPLIT_CALL,
    )(*hbm, *before)
    return outs[:6 * n], outs[6 * n:7 * n], outs[7 * n:8 * n], outs[8 * n]


def _chip_send_wait(name, sems, blocks, lands, after, per_peer=False):
    n = len(blocks)

    def body(*refs):
        x_refs, land_refs, sems = refs[:n], refs[n:2 * n], refs[2 * n:8 * n]
        c = lax.axis_index("c")
        for i in range(n):
            for j, chip in enumerate(_peer_chips()):
                src = x_refs[i].at[2 * chip[0] + chip[1]] if per_peer else x_refs[i]
                cp = pltpu.make_async_remote_copy(src_ref=src, dst_ref=land_refs[i].at[2 * chip[0] + chip[1]],
                                                  send_sem=sems[6 * i + j], recv_sem=sems[6 * i + 3 + j],
                                                  device_id=(*chip, c), device_id_type=MESH)
                cp.wait_send()
                cp.wait_recv()

    outs = pl.pallas_call(
        body, name=name, out_shape=tuple(pltpu.HBM(a.shape, a.dtype) for a in list(blocks) + list(lands)),
        in_specs=tuple([_HBM] * (2 * n) + [_SEM] * (6 * n) + [_ANY]), out_specs=tuple([_HBM] * (2 * n)),
        input_output_aliases={i: i for i in range(2 * n)}, compiler_params=_SPLIT_CALL,
    )(*blocks, *lands, *sems, after)
    return outs[:n], outs[n:]


def _pack_flat(arrays, multiple, dtype):
    flat = jnp.concatenate([a.reshape(-1).astype(dtype) for a in arrays])
    pad = (-flat.shape[0]) % multiple
    if pad:
        flat = jnp.concatenate([flat, jnp.zeros((pad,), dtype)])
    return flat


def _unpack_flat(flat, shapes):
    out, off = [], 0
    for s in shapes:
        n = math.prod(s)
        out.append(flat[off:off + n].reshape(s))
        off += n
    return out


def _packed_rows(shape):
    return -(-math.prod(shape) // (8 * LANES)) * 8


def _pack_rows(arrays):
    parts = []
    for a in arrays:
        rows = _packed_rows(a.shape)
        flat = a.reshape(-1).astype(F32)
        if rows * LANES != flat.shape[0]:
            flat = jnp.pad(flat, (0, rows * LANES - flat.shape[0]))
        parts.append(flat.reshape(rows, LANES))
    return jnp.concatenate(parts, axis=0)


def _unpack_rows(packed, shapes):
    out, r0 = [], 0
    for s in shapes:
        rows = _packed_rows(s)
        out.append(packed[r0:r0 + rows].reshape(-1)[:math.prod(s)].reshape(s))
        r0 += rows
    return out


def _row_tile(r, cap, mult=16):
    for cand in range(min(cap, r) // mult * mult, 0, -mult):
        if r % cand == 0:
            return cand
    return r


ELEMENTWISE_BLOCK_ELEMS = 1 << 20


def _sum_axis1(name, x, out_dtype=F32):
    h, k, r, n = x.shape
    tr = _row_tile(r, max(16, ELEMENTWISE_BLOCK_ELEMS // (k * n)), 32 // x.dtype.itemsize)

    def fn(s):
        acc = s[0].astype(F32)
        for i in range(1, k):
            acc = acc + s[i].astype(F32)
        return acc

    return _blocked(name, fn, (h, r // tr), [x], [pl.BlockSpec((None, k, tr, n), lambda i, t: (i, 0, t, 0))],
                    [_sds((h, r, n), out_dtype)], [pl.BlockSpec((None, tr, n), lambda i, t: (i, t, 0))])[0]


def _sum_pair(name, a, b, out_dtype):
    lead, r, n = a.shape
    tr = _row_tile(r, max(16, ELEMENTWISE_BLOCK_ELEMS // n), 32 // min(a.dtype.itemsize, jnp.dtype(out_dtype).itemsize))
    spec = pl.BlockSpec((None, tr, n), lambda i, t: (i, t, 0))
    return _blocked(name, lambda x, y: x.astype(F32) + y.astype(F32), (lead, r // tr), [a, b], [spec, spec],
                    [_sds((lead, r, n), out_dtype)], [spec])[0]


def _adamw(name, w, g, m, v):
    if w.ndim == 2:
        r, c = w.shape
        tr = _row_tile(r, max(8, (1 << 21) // (4 * c)), 8)
        grid, spec = (r // tr,), pl.BlockSpec((tr, c), lambda t: (t, 0))
    else:
        grid, spec = (1,), _res(w.shape)
    bc1 = 1.0 - ADAM_B1 ** ADAM_STEP
    bc2 = 1.0 - ADAM_B2 ** ADAM_STEP

    def fn(w, g, m, v):
        nm = ADAM_B1 * m + (1.0 - ADAM_B1) * g
        nv = ADAM_B2 * v + (1.0 - ADAM_B2) * (g * g)
        delta = -ADAM_LR * ((nm / bc1) / (jnp.sqrt(nv / bc2) + ADAM_EPS) + ADAM_WD * w)
        return delta, nm, nv

    return _blocked(name, fn, grid, [w, g, m, v], [spec] * 4, [_sds(w.shape)] * 3, [spec] * 3)


def _mod_rows(c16, ada_w):
    depth, d, n = ada_w.shape

    def fn(c, w):
        return _mm(_silu(c), w)

    return _blocked("mod_rows", fn, (depth,), [c16, ada_w],
                    [_res(c16.shape), pl.BlockSpec((None, d, n), lambda l: (l, 0, 0))],
                    [_sds((depth, 16, n))], [pl.BlockSpec((None, 16, n), lambda l: (l, 0, 0))])[0]


def _ada_grads(c16, dmod16, ada_w):
    depth, d, n = ada_w.shape

    def fn(c, dm, w):
        return _mm_tn(_silu(c), dm), _mm_nt(dm, w)

    return _blocked("ada_grads", fn, (depth,), [c16, dmod16, ada_w],
                    [_res(c16.shape), pl.BlockSpec((None, 16, n), lambda l: (l, 0, 0)),
                     pl.BlockSpec((None, d, n), lambda l: (l, 0, 0))],
                    [_sds((depth, d, n)), _sds((depth, 16, d))],
                    [pl.BlockSpec((None, d, n), lambda l: (l, 0, 0)), pl.BlockSpec((None, 16, d), lambda l: (l, 0, 0))])


def _cctx_grad(parts, c_ctx_row):
    def fn(p, c):
        tot = ((p[0] + p[1]) + p[2]) + p[3]
        _, vjp = jax.vjp(_silu, c)
        return vjp(tot)[0]

    return _blocked("cctx_grad", fn, (1,), [parts, c_ctx_row], [_res(parts.shape), _res(c_ctx_row.shape)],
                    [_sds(c_ctx_row.shape)], [_res(c_ctx_row.shape)])[0]


def _modulate(x, sh, sc):
    return x * (1.0 + sc) + sh


def _weight_operand(w):
    if isinstance(w, tuple):
        return w[0], w[1], tuple(b for b in w[1].block_shape if b is not None)
    return w, _res(w.shape), w.shape


def _matmul_cols(u, w):
    if w.ndim == 2:
        return jnp.dot(u, w, preferred_element_type=F32)
    return jnp.concatenate([jnp.dot(u, w[j], preferred_element_type=F32) for j in range(w.shape[0])], axis=1)


def _matmul_cols_t(dp, w):
    if w.ndim == 2:
        return _mm_nt(dp, w)
    nj = w.shape[2]
    return functools.reduce(lambda a, b: a + b, [_mm_nt(dp[:, j * nj:(j + 1) * nj], w[j]) for j in range(w.shape[0])])


def _rows_weight(w):
    return w if w.ndim == 2 else w.reshape(w.shape[0] * w.shape[1], w.shape[2])


def _inproj(name, x, sh, sc, w, tm):
    t, d = x.shape
    w, w_spec, w_blk = _weight_operand(w)
    n = w_blk[-1] * (w_blk[0] if len(w_blk) == 3 else 1)
    tm = min(tm, t)

    def fn(x, sh, sc, w):
        u = _modulate(x, sh, sc).astype(MXU_DTYPE)
        return _matmul_cols(u, w), u

    return _blocked(name, fn, (t // tm,), [x, sh, sc, w],
                    [_tok(tm, d), _res(sh.shape), _res(sc.shape), w_spec],
                    [_sds((t, n)), _sds((t, d), MXU_DTYPE)], [_tok(tm, n), _tok(tm, d)])


def _inproj_bwd(name, dp, x, sh, sc, w, add, tm):
    t, d = x.shape
    w, w_spec, w_blk = _weight_operand(w)
    n = w_blk[-1] * (w_blk[0] if len(w_blk) == 3 else 1)
    tm = min(tm, t)
    has_add = add is not None

    def fn(dp, x, sh, sc, w, *rest):
        du = _matmul_cols_t(dp, w)
        _, vjp = jax.vjp(_modulate, x, sh, sc)
        dx, dsh, dsc = vjp(du)
        if has_add:
            dx = dx + rest[0]
        return dx, dsh, dsc

    ins = [dp, x, sh, sc, w] + ([add] if has_add else [])
    specs = [_tok(tm, n), _tok(tm, d), _res(sh.shape), _res(sc.shape), w_spec] + ([_tok(tm, d)] if has_add else [])
    return _blocked(name, fn, (t // tm,), ins, specs,
                    [_sds((t, d)), _sds(sh.shape), _sds(sc.shape)], [_tok(tm, d), _res(sh.shape), _res(sc.shape)],
                    acc=(1, 2), acc_axis=0)


def _residual_ln(y, x, gt, lng, lnb):
    return _layernorm(ALPHA * x + gt * y, lng, lnb)


def _gdn_mix(o0, o1, pg, yp, normw):
    o = o0 + o1
    heads = []
    for h in range(GDN_HEADS):
        oh = o[:, h * HEAD_DIM:(h + 1) * HEAD_DIM]
        heads.append(oh * lax.rsqrt(jnp.mean(oh * oh, -1, keepdims=True) + RMS_EPS) * normw)
    on = jnp.concatenate(heads, axis=-1) * _silu(pg)
    return jnp.concatenate([on, yp], axis=-1)


def _even_out(o0, o1, p, ypool, x, normw, gt, lng, lnb, w, tm):
    t, d = x.shape
    tm = min(tm, t)
    gate_blk = 3

    def fn(o0, o1, pg, yp, x, normw, gt, lng, lnb, w):
        mix = _gdn_mix(o0, o1, pg, yp, normw).astype(MXU_DTYPE)
        y = jnp.dot(mix, w, preferred_element_type=F32)
        return _residual_ln(y, x, gt, lng, lnb), mix, y

    return _blocked("even_out", fn, (t // tm,), [o0, o1, p, ypool, x, normw, gt, lng, lnb, w],
                    [_tok(tm, 512), _tok(tm, 512), _tok(tm, 512, gate_blk), _tok(tm, 512), _tok(tm, d),
                     _res(normw.shape), _res(gt.shape), _res(lng.shape), _res(lnb.shape), _res(w.shape)],
                    [_sds((t, d)), _sds((t, d), MXU_DTYPE), _sds((t, d))], [_tok(tm, d)] * 3)


def _even_out_bwd(dx1, o0, o1, p, ypool, x, y, normw, gt, lng, lnb, w, tm):
    t, d = x.shape
    tm = min(tm, t)

    def fn(dx1, o0, o1, pg, yp, x, y, normw, gt, lng, lnb, w):
        _, vjp2 = jax.vjp(_residual_ln, y, x, gt, lng, lnb)
        dy, dx, dgt, dlng, dlnb = vjp2(dx1)
        dyb = dy.astype(MXU_DTYPE)
        dmix = _mm_nt(dyb, w)
        _, vjp1 = jax.vjp(_gdn_mix, o0, o1, pg, yp, normw)
        do, _, dpg, dyp, dnormw = vjp1(dmix)
        return do, dpg, dyp, dx, dyb, dnormw, dgt, dlng, dlnb

    return _blocked("even_out_bwd", fn, (t // tm,), [dx1, o0, o1, p, ypool, x, y, normw, gt, lng, lnb, w],
                    [_tok(tm, d), _tok(tm, 512), _tok(tm, 512), _tok(tm, 512, 3), _tok(tm, 512),
                     _tok(tm, d), _tok(tm, d), _res(normw.shape), _res(gt.shape), _res(lng.shape), _res(lnb.shape),
                     _res(w.shape)],
                    [_sds((t, 512)), _sds((t, 512), MXU_DTYPE), _sds((t, 512)), _sds((t, d)), _sds((t, d), MXU_DTYPE),
                     _sds(normw.shape), _sds(gt.shape), _sds(lng.shape), _sds(lnb.shape)],
                    [_tok(tm, 512), _tok(tm, 512), _tok(tm, 512), _tok(tm, d), _tok(tm, d),
                     _res(normw.shape), _res(gt.shape), _res(lng.shape), _res(lnb.shape)],
                    acc=(5, 6, 7, 8), acc_axis=0)


def _odd_mix(ysc, z0, cg, cb):
    z = _silu(_layernorm(z0, cg, cb))
    return jnp.concatenate([ysc, z], axis=-1)


def _odd_out(ysc, z0, x, cg, cb, gt, lng, lnb, w, tm):
    t, d = x.shape
    tm = min(tm, t)

    def fn(ysc, z0, x, cg, cb, gt, lng, lnb, w):
        mix = _odd_mix(ysc, z0, cg, cb).astype(MXU_DTYPE)
        y = jnp.dot(mix, w, preferred_element_type=F32)
        return _residual_ln(y, x, gt, lng, lnb), mix, y

    return _blocked("odd_out", fn, (t // tm,), [ysc, z0, x, cg, cb, gt, lng, lnb, w],
                    [_tok(tm, 512), _tok(tm, 512), _tok(tm, d), _res(cg.shape), _res(cb.shape), _res(gt.shape),
                     _res(lng.shape), _res(lnb.shape), _res(w.shape)],
                    [_sds((t, d)), _sds((t, d), MXU_DTYPE), _sds((t, d))], [_tok(tm, d)] * 3)


def _odd_out_bwd(dx3, ysc, z0, x, y, cg, cb, gt, lng, lnb, w, tm):
    t, d = x.shape
    tm = min(tm, t)

    def fn(dx3, ysc, z0, x, y, cg, cb, gt, lng, lnb, w):
        _, vjp2 = jax.vjp(_residual_ln, y, x, gt, lng, lnb)
        dy, dx, dgt, dlng, dlnb = vjp2(dx3)
        dyb = dy.astype(MXU_DTYPE)
        dmix = _mm_nt(dyb, w)
        _, vjp1 = jax.vjp(_odd_mix, ysc, z0, cg, cb)
        dysc, dz0, dcg, dcb = vjp1(dmix)
        return dysc, dz0, dx, dyb, dcg, dcb, dgt, dlng, dlnb

    return _blocked("odd_out_bwd", fn, (t // tm,), [dx3, ysc, z0, x, y, cg, cb, gt, lng, lnb, w],
                    [_tok(tm, d), _tok(tm, 512), _tok(tm, 512), _tok(tm, d), _tok(tm, d), _res(cg.shape), _res(cb.shape),
                     _res(gt.shape), _res(lng.shape), _res(lnb.shape), _res(w.shape)],
                    [_sds((t, 512)), _sds((t, 512)), _sds((t, d)), _sds((t, d), MXU_DTYPE),
                     _sds(cg.shape), _sds(cb.shape), _sds(gt.shape), _sds(lng.shape), _sds(lnb.shape)],
                    [_tok(tm, 512), _tok(tm, 512), _tok(tm, d), _tok(tm, d),
                     _res(cg.shape), _res(cb.shape), _res(gt.shape), _res(lng.shape), _res(lnb.shape)],
                    acc=(4, 5, 6, 7, 8), acc_axis=0)


def _ffn_down(name, s, x, gt, lng, lnb, w, tm, target=None):
    t, d = x.shape
    f = s.shape[1]
    w, w_spec, _ = _weight_operand(w)
    tm = min(tm, t)

    def fn(s, x, gt, lng, lnb, w, *tg):
        y = jnp.dot(s, _rows_weight(w), preferred_element_type=F32)
        out = _residual_ln(y, x, gt, lng, lnb)
        if not tg:
            return out, y
        e = out - tg[0]
        part = 0.5 * jnp.sum(jnp.mean(e * e, axis=-1, keepdims=True), axis=0, keepdims=True)
        return jnp.broadcast_to(part, (1, LANES)), e * (1.0 / d), y

    ins = [s, x, gt, lng, lnb, w]
    specs = [_tok(tm, f), _tok(tm, d), _res(gt.shape), _res(lng.shape), _res(lnb.shape), w_spec]
    if target is None:
        return _blocked(name, fn, (t // tm,), ins, specs, [_sds((t, d)), _sds((t, d))], [_tok(tm, d)] * 2)
    return _blocked(name, fn, (t // tm,), ins + [target], specs + [_tok(tm, d)],
                    [_sds((1, LANES)), _sds((t, d)), _sds((t, d))], [_res((1, LANES)), _tok(tm, d), _tok(tm, d)],
                    acc=(0,), acc_axis=0)


def _ffn_down_bwd(name, dx2, x, y, gt, lng, lnb, w, tm):
    t, d = x.shape
    w, w_spec, w_blk = _weight_operand(w)
    f = math.prod(w_blk[:-1])
    tm = min(tm, t)

    def fn(dx2, x, y, gt, lng, lnb, w):
        _, vjp2 = jax.vjp(_residual_ln, y, x, gt, lng, lnb)
        dy, dx, dgt, dlng, dlnb = vjp2(dx2)
        dyb = dy.astype(MXU_DTYPE)
        return _mm_nt(dyb, _rows_weight(w)), dx, dyb, dgt, dlng, dlnb

    return _blocked(name, fn, (t // tm,), [dx2, x, y, gt, lng, lnb, w],
                    [_tok(tm, d), _tok(tm, d), _tok(tm, d), _res(gt.shape), _res(lng.shape), _res(lnb.shape), w_spec],
                    [_sds((t, f), MXU_DTYPE), _sds((t, d)), _sds((t, d), MXU_DTYPE), _sds(gt.shape), _sds(lng.shape), _sds(lnb.shape)],
                    [_tok(tm, f), _tok(tm, d), _tok(tm, d), _res(gt.shape), _res(lng.shape), _res(lnb.shape)],
                    acc=(3, 4, 5), acc_axis=0)


def _matmul_tn(name, pairs, chip_cols=None):
    k = pairs[0][0].shape[1]
    n = pairs[0][1].shape[1]
    tk = k if k <= 1024 else k // 2
    if chip_cols is None:
        tn = 512 if n % 512 == 0 else (384 if n % 384 == 0 else 128)
        out_spec, out_shape = pl.BlockSpec((tk, tn), lambda i, j: (i, j)), _sds((k, n), MXU_DTYPE)
    else:
        tn = chip_cols
        tk = min(tk, 512) if tn > 1024 else tk
        out_spec = pl.BlockSpec((None, tk, tn), lambda i, j: (j, i, 0))
        out_shape = _sds((n // chip_cols, k, chip_cols), MXU_DTYPE)

    def body(*refs):
        acc = None
        for p in range(len(pairs)):
            term = lax.dot_general(refs[2 * p][...], refs[2 * p + 1][...], (((0,), (0,)), ((), ())), preferred_element_type=F32)
            acc = term if acc is None else acc + term
        refs[-1][...] = acc.astype(refs[-1].dtype)

    in_specs, args = [], []
    for a, b in pairs:
        t = a.shape[0]
        in_specs += [pl.BlockSpec((t, tk), lambda i, j: (0, i)), pl.BlockSpec((t, tn), lambda i, j: (0, j))]
        args += [a, b]
    return pl.pallas_call(
        body, name=name, grid=(k // tk, n // tn), in_specs=in_specs,
        out_specs=out_spec, out_shape=out_shape,
        compiler_params=pltpu.CompilerParams(dimension_semantics=("arbitrary", "arbitrary"),
                                             vmem_limit_bytes=VMEM_LIMIT_BYTES),
    )(*args)


def _chan(t, col_of):
    return pl.BlockSpec((t, LANES), lambda i: (0, col_of(i)))


def _wblk(k, col_of=lambda i: i):
    return pl.BlockSpec((k, LANES), lambda i: (0, col_of(i)))


SEQ_TILE = 256
GRID_SEQ_TILE = 512


def _load_ext(ref, r0, rows, halo, total):
    lo, hi = max(r0 - halo, 0), min(r0 + rows + halo, total)
    parts = []
    if lo > r0 - halo:
        parts.append(jnp.zeros((lo - (r0 - halo), ref.shape[1]), F32))
    parts.append(ref[lo:hi, :].astype(F32))
    if hi < r0 + rows + halo:
        parts.append(jnp.zeros((r0 + rows + halo - hi, ref.shape[1]), F32))
    return parts[0] if len(parts) == 1 else jnp.concatenate(parts, axis=0)


def _seq_stage(name, tile_fn, halo, grid_n, seqs, pars, outs, tile_rows=SEQ_TILE):
    total = seqs[0][0].shape[0]
    rows = min(tile_rows, total)
    ns, npar = len(seqs), len(pars)

    def body(*refs):
        seq_refs, par_refs, out_refs = refs[:ns], refs[ns:ns + npar], refs[ns + npar:]
        par_vals = [r[...] for r in par_refs]
        for r0 in range(0, total, rows):
            exts = [_load_ext(r, r0, rows, halo, total) for r in seq_refs]
            res = tile_fn(r0, *exts, *par_vals)
            for o_ref, v in zip(out_refs, res):
                o_ref[r0:r0 + rows, :] = v.astype(o_ref.dtype)

    return pl.pallas_call(
        body, name=name, grid=(grid_n,), in_specs=[s for _, s in seqs] + [s for _, s in pars],
        out_specs=[s for _, s in outs], out_shape=[o for o, _ in outs],
        compiler_params=pltpu.CompilerParams(dimension_semantics=("arbitrary",), vmem_limit_bytes=VMEM_LIMIT_BYTES),
    )(*[a for a, _ in seqs], *[a for a, _ in pars])


def _seq_stage_bwd(name, tile_fn, halo, grid_n, douts, seqs, pars, dseq_outs, dpar_outs, tile_rows=SEQ_TILE, n_const=0):
    total = seqs[0][0].shape[0]
    rows = min(tile_rows, total)
    groups = [d if isinstance(d, list) else [d] for d in douts]
    douts = [d for g in groups for d in g]
    starts = [sum(len(g) for g in groups[:k]) for k in range(len(groups))]
    nd, ns, npar = len(douts), len(seqs), len(pars)
    nsd = ns - n_const
    widths = [s.block_shape[-1] for _, s in seqs[:nsd]]

    def dtile(dout_refs, k, r0):
        terms = [d[r0:r0 + rows, :].astype(F32) for d in dout_refs[starts[k]:starts[k] + len(groups[k])]]
        return functools.reduce(lambda a, b: a + b, terms)

    def body(*refs):
        dout_refs, seq_refs, par_refs = refs[:nd], refs[nd:nd + ns], refs[nd + ns:nd + ns + npar]
        n_in = nd + ns + npar
        dseq_refs = refs[n_in:n_in + nsd]
        dpar_refs = refs[n_in + nsd:n_in + nsd + npar]
        accs = refs[n_in + nsd + npar:]
        par_vals = [r[...] for r in par_refs]
        for a in accs:
            for r0 in range(0, total, rows):
                a[r0:r0 + rows, :] = jnp.zeros((rows, a.shape[1]), F32)
        dpars = [jnp.zeros(p.shape, F32) for p in par_vals]
        for r0 in range(0, total, rows):
            exts = [_load_ext(r, r0, rows, halo, total) for r in seq_refs]
            saved = exts[nsd:]
            _, vjp = jax.vjp(lambda *a: tile_fn(r0, *a[:nsd], *saved, *a[nsd:]), *exts[:nsd], *par_vals)
            cts = vjp(tuple(dtile(dout_refs, k, r0) for k in range(len(groups))))
            lo, hi = max(r0 - halo, 0), min(r0 + rows + halo, total)
            for a, dx in zip(accs, cts[:nsd]):
                a[lo:hi, :] += dx[lo - (r0 - halo):hi - (r0 - halo)]
            dpars = [acc + g for acc, g in zip(dpars, cts[nsd:])]
        for o_ref, a in zip(dseq_refs, accs):
            for r0 in range(0, total, rows):
                o_ref[r0:r0 + rows, :] = a[r0:r0 + rows, :].astype(o_ref.dtype)
        for o_ref, g in zip(dpar_refs, dpars):
            o_ref[...] = g

    return pl.pallas_call(
        body, name=name, grid=(grid_n,),
        in_specs=[s for _, s in douts] + [s for _, s in seqs] + [s for _, s in pars],
        out_specs=[s for _, s in dseq_outs] + [s for _, s in dpar_outs],
        out_shape=[o for o, _ in dseq_outs] + [o for o, _ in dpar_outs],
        scratch_shapes=[pltpu.VMEM((total, w), F32) for w in widths],
        compiler_params=pltpu.CompilerParams(dimension_semantics=("arbitrary",), vmem_limit_bytes=VMEM_LIMIT_BYTES),
    )(*[a for a, _ in douts], *[a for a, _ in seqs], *[a for a, _ in pars])


def _qkv_tile(r0, p_ext, w):
    kind = pl.program_id(0)
    a = _silu(_conv5(p_ext, w))
    nrm = a * lax.rsqrt(jnp.sum(a * a, -1, keepdims=True) + RMS_EPS)
    scale = jnp.where(kind < GDN_HEADS, HEAD_DIM ** -0.5, 1.0).astype(F32)
    sel = (kind < 2 * GDN_HEADS).astype(F32)
    return (sel * (nrm * scale) + (1.0 - sel) * a,)


def _qkv_conv(name, p, w):
    t = p.shape[0]
    nb = w.shape[1] // LANES
    ident = lambda i: i
    return _seq_stage(name, _qkv_tile, HALO_SHORT, nb, [(p, _chan(t, ident))], [(w, _wblk(GDN_CONV))],
                      [(_sds((t, nb * LANES)), _chan(t, ident))])[0]


def _qkv_conv_bwd(name, dqkv0, dqkv1, p, w):
    t = p.shape[0]
    nb = w.shape[1] // LANES
    ident = lambda i: i
    return _seq_stage_bwd(name, _qkv_tile, HALO_SHORT, nb, [[(dqkv0, _chan(t, ident)), (dqkv1, _chan(t, ident))]],
                          [(p, _chan(t, ident))],
                          [(w, _wblk(GDN_CONV))], [(_sds((t, nb * LANES), MXU_DTYPE), _chan(t, ident))],
                          [(_sds(w.shape), _wblk(GDN_CONV))])


def _gates_fn(s, avec, dtvec):
    lane = lax.broadcasted_iota(jnp.int32, s.shape, 1)
    beta = jax.nn.sigmoid(s)
    g = -jnp.exp(avec) * jax.nn.softplus(s + dtvec)
    nh = 2 * GDN_HEADS
    return jnp.where(lane < nh, beta, jnp.where(lane < 2 * nh, g, 0.0))


def _gates(name, p, avec, dtvec, col):
    t = p.shape[0]
    tm = min(512, t)
    return _blocked(name, _gates_fn, (t // tm,), [p, avec, dtvec], [_tok(tm, LANES, col), _res(avec.shape), _res(dtvec.shape)],
                    [_sds((t, LANES))], [_tok(tm, LANES)])[0]


def _gates_bwd(name, dgb, p, avec, dtvec, col):
    t = p.shape[0]
    tm = min(512, t)

    def fn(dgb, s, avec, dtvec):
        _, vjp = jax.vjp(_gates_fn, s, avec, dtvec)
        return vjp(dgb)

    return _blocked(name, fn, (t // tm,), [dgb, p, avec, dtvec],
                    [_tok(tm, LANES), _tok(tm, LANES, col), _res(avec.shape), _res(dtvec.shape)],
                    [_sds((t, LANES), MXU_DTYPE), _sds(avec.shape), _sds(dtvec.shape)],
                    [_tok(tm, LANES), _res(avec.shape), _res(dtvec.shape)], acc=(1, 2), acc_axis=0)


def _make_pool_tile(total):
    def tile(r0, x_ext, scale, *ws):
        ys = []
        for g, win in enumerate(POOL_WINDOWS):
            xg = x_ext[:, g * LANES:(g + 1) * LANES]
            rows = xg.shape[0] - 2 * HALO_SHORT
            pos = r0 + lax.broadcasted_iota(jnp.int32, (rows, LANES), 0)
            lo = jnp.clip(pos - win // 2, 0, total)
            hi = jnp.clip(pos - win // 2 + win, 0, total)
            window_sum = _pool_sums[win](xg, jnp.ones((win, LANES), F32))
            pooled = window_sum / (hi - lo).astype(F32) - _center(xg, HALO_SHORT)
            ys.append(_mm(pooled, ws[g]))
        return (jnp.concatenate(ys, axis=-1) * scale,)

    return tile


def _pool_specs(p, pool_w, pool_scale, col0):
    t = p.shape[0]
    n = len(POOL_WINDOWS) * LANES
    seq = (p, pl.BlockSpec((t, n), lambda i: (0, col0)))
    pars = [(pool_scale, _res(pool_scale.shape))]
    pars += [(pool_w, pl.BlockSpec((None, LANES, LANES), lambda i, g=g: (g, 0, 0))) for g in range(len(POOL_WINDOWS))]
    return t, n, seq, pars


def _pool(p, pool_w, pool_scale, col0):
    t, n, seq, pars = _pool_specs(p, pool_w, pool_scale, col0)
    return _seq_stage("pool", _make_pool_tile(t), HALO_SHORT, 1, [seq], pars, [(_sds((t, n)), _res((t, n)))])[0]


def _pool_bwd(dy, p, pool_w, pool_scale, col0):
    t, n, seq, pars = _pool_specs(p, pool_w, pool_scale, col0)
    wspec = (_sds((LANES, LANES)), _res((LANES, LANES)))
    return _seq_stage_bwd("pool_bwd", _make_pool_tile(t), HALO_SHORT, 1, [(dy, _res(dy.shape))], [seq], pars,
                          [(_sds((t, n), MXU_DTYPE), _res((t, n)))],
                          [(_sds(pool_scale.shape), _res(pool_scale.shape))] + [wspec] * len(POOL_WINDOWS))


def _ffn_conv_tile(r0, a_ext, gt_ext, w):
    cv = _conv3x3(a_ext, w)
    return _silu(cv) * _center(gt_ext, HALO_GRID), cv


def _ffn_conv_known_tile(r0, a_ext, gt_ext, cv_ext, w):
    cv = _conv3x3_known(a_ext, w, _center(cv_ext, HALO_GRID))
    return (_silu(cv) * _center(gt_ext, HALO_GRID),)


def _ffn_conv(name, h, w9):
    t = h.shape[0]
    nb = w9.shape[1] // LANES
    ident = lambda i: i
    return _seq_stage(name, _ffn_conv_tile, HALO_GRID, nb, [(h, _chan(t, ident)), (h, _chan(t, lambda i: nb + i))],
                      [(w9, _wblk(9))], [(_sds((t, nb * LANES), MXU_DTYPE), _chan(t, ident)), (_sds((t, nb * LANES)), _chan(t, ident))],
                      GRID_SEQ_TILE)


def _ffn_conv_bwd(name, ds, h, cv, w9):
    t = h.shape[0]
    nb = w9.shape[1] // LANES
    ident = lambda i: i
    o = (_sds((t, nb * LANES), MXU_DTYPE), _chan(t, ident))
    return _seq_stage_bwd(name, _ffn_conv_known_tile, HALO_GRID, nb, [(ds, _chan(t, ident))],
                          [(h, _chan(t, ident)), (h, _chan(t, lambda i: nb + i)), (cv, _chan(t, ident))], [(w9, _wblk(9))],
                          [o, o], [(_sds(w9.shape), _wblk(9))], GRID_SEQ_TILE, n_const=1)


def _sconv_tile(r0, gb_ext, gc_ext, h_ext, w):
    return (_center(gb_ext, HALO_SHORT) * _conv3(gc_ext * h_ext, w),)


def _sconv_specs(p1, w3):
    t = p1.shape[0]
    nb = w3.shape[1] // LANES
    seqs = [(p1, _chan(t, lambda i: i)), (p1, _chan(t, lambda i: nb + i)), (p1, _chan(t, lambda i: 2 * nb + i))]
    return t, nb, seqs, [(w3, _wblk(SC_WIDTH))]


def _sconv(p1, w3):
    t, nb, seqs, pars = _sconv_specs(p1, w3)
    return _seq_stage("sconv", _sconv_tile, HALO_SHORT, nb, seqs, pars, [(_sds((t, nb * LANES)), _chan(t, lambda i: i))])[0]


def _sconv_bwd(dysc, p1, w3):
    t, nb, seqs, pars = _sconv_specs(p1, w3)
    o = (_sds((t, nb * LANES), MXU_DTYPE), _chan(t, lambda i: i))
    return _seq_stage_bwd("sconv_bwd", _sconv_tile, HALO_SHORT, nb, [(dysc, _chan(t, lambda i: i))], seqs, pars,
                          [o, o, o], [(_sds(w3.shape), _wblk(SC_WIDTH))])


def _conf_tile(r0, ga_ext, gb_ext, w):
    return (_conv31(ga_ext * jax.nn.sigmoid(gb_ext), w),)


def _conf_specs(p1, w31, blk0):
    t = p1.shape[0]
    nb = w31.shape[1] // LANES
    seqs = [(p1, _chan(t, lambda i: blk0 + i)), (p1, _chan(t, lambda i: blk0 + nb + i))]
    return t, nb, seqs, [(w31, _wblk(CF_WIDTH))]


def _confconv(p1, w31, blk0):
    t, nb, seqs, pars = _conf_specs(p1, w31, blk0)
    return _seq_stage("confconv", _conf_tile, HALO_CONF, nb, seqs, pars, [(_sds((t, nb * LANES)), _chan(t, lambda i: i))])[0]


def _confconv_bwd(dz0, p1, w31, blk0):
    t, nb, seqs, pars = _conf_specs(p1, w31, blk0)
    o = (_sds((t, nb * LANES), MXU_DTYPE), _chan(t, lambda i: i))
    return _seq_stage_bwd("confconv_bwd", _conf_tile, HALO_CONF, nb, [(dz0, _chan(t, lambda i: i))], seqs, pars,
                          [o, o], [(_sds(w31.shape), _wblk(CF_WIDTH))])


def _split_bf16(x):
    head = x.astype(jnp.bfloat16)
    return head, (x - head.astype(F32)).astype(jnp.bfloat16)


def _bdot(a, b, ca, cb, hi):
    dims = (((ca,), (cb,)), ((0,), (0,)))
    if not hi:
        return lax.dot_general(a.astype(MXU_DTYPE), b.astype(MXU_DTYPE), dims, preferred_element_type=F32)
    if MXU_DTYPE == F32:
        return lax.dot_general(a, b, dims, preferred_element_type=F32, precision=HIGHEST)
    (ah, al), (bh, bl) = _split_bf16(a), _split_bf16(b)
    dot = functools.partial(lax.dot_general, dimension_numbers=dims, preferred_element_type=F32)
    return dot(ah, bh) + (dot(ah, bl) + dot(al, bh))


def _make_bmm(hi):
    nn_i = lambda a, b: _bdot(a, b, 2, 1, hi)
    nt_i = lambda a, b: _bdot(a, b, 2, 2, hi)
    tn_i = lambda a, b: _bdot(a, b, 1, 1, hi)
    nn = _mm_vjp(nn_i, lambda g, b: nt_i(g, b), lambda a, g: tn_i(a, g))
    nt = _mm_vjp(nt_i, lambda g, b: nn_i(g, b), lambda a, g: tn_i(g, a))
    tn = _mm_vjp(tn_i, lambda g, b: nt_i(b, g), lambda a, g: nn_i(a, g))
    return nn, nt, tn


_bmm, _bmm_nt, _bmm_tn = _make_bmm(False)
_bmm_hi, _bmm_hi_nt, _bmm_hi_tn = _make_bmm(True)


def _unit_tri_inverse(a):
    c = a.shape[-1]
    eye = (lax.broadcasted_iota(jnp.int32, a.shape, 1) == lax.broadcasted_iota(jnp.int32, a.shape, 2)).astype(F32)
    levels = max(1, int(math.ceil(math.log2(c))) - 1)
    p = eye - a
    m = _bdot(a, a, 2, 1, True)
    for level in range(levels):
        p_next = p + _bdot(p, m, 2, 1, True)
        if level + 1 < levels:
            m = _bdot(m, m, 2, 1, True)
        p = p_next
    return p


@jax.custom_vjp
def _known_inverse(a, tinv):
    return tinv


def _known_inverse_fwd(a, tinv):
    return tinv, tinv


def _known_inverse_bwd(tinv, dt):
    da = -_bdot(_bdot(tinv, dt, 1, 1, True), tinv, 2, 2, True)
    return da, jnp.zeros_like(tinv)


_known_inverse.defvjp(_known_inverse_fwd, _known_inverse_bwd)


def _gdn_chunk(q, k, v, gcol, grow, bcol, s, tinv=None):
    n, c, _ = q.shape
    shape = (n, c, c)
    fwd_dir = lax.broadcasted_iota(jnp.int32, shape, 0) < n // 2
    i = lax.broadcasted_iota(jnp.int32, shape, 1)
    j = lax.broadcasted_iota(jnp.int32, shape, 2)
    order = jnp.where(fwd_dir, i - j, j - i)
    incl = order >= 0
    gc_col = jnp.sum(jnp.where(incl, grow, 0.0), axis=2, keepdims=True)
    gc_row = jnp.sum(jnp.where(order <= 0, gcol, 0.0), axis=1, keepdims=True)
    gtot = jnp.sum(grow, axis=2, keepdims=True)
    decay = jnp.exp(jnp.where(incl, gc_col - gc_row, -1e30))
    kb = k * bcol
    a = jnp.where(order > 0, _bmm_nt(kb, k) * decay, 0.0)
    tinv = _unit_tri_inverse(a) if tinv is None else _known_inverse(a, tinv)
    e_col = jnp.exp(gc_col)
    u = _bmm_hi(tinv, v * bcol)
    w = _bmm_hi(tinv, kb * e_col)
    k_dec = k * jnp.exp(gtot - gc_col)
    v_new = u - _bmm(w, s)
    attn = jnp.where(incl, _bmm_nt(q, k) * decay, 0.0)
    o = _bmm(q * e_col, s) + _bmm(attn, v_new)
    s_next = s * jnp.exp(gtot) + _bmm_tn(k_dec, v_new)
    return o, s_next, tinv


N_CHAINS = 2 * GDN_HEADS
QKV_W = 3 * GDN_HEADS * HEAD_DIM


def _scan_specs(nc, step):
    c = GDN_CHUNK
    seq = [pl.BlockSpec((c, QKV_W), lambda i: (step(i), 0)), pl.BlockSpec((c, QKV_W), lambda i: (nc - 1 - step(i), 0))]
    heads = [pl.BlockSpec((c, GDN_HEADS * HEAD_DIM), lambda i: (step(i), 0)),
             pl.BlockSpec((c, GDN_HEADS * HEAD_DIM), lambda i: (nc - 1 - step(i), 0))]
    colv = pl.BlockSpec((N_CHAINS, None, c, 1), lambda i: (0, step(i), 0, 0))
    rowv = pl.BlockSpec((N_CHAINS, None, 1, c), lambda i: (0, step(i), 0, 0))
    state = pl.BlockSpec((N_CHAINS, HEAD_DIM, HEAD_DIM), lambda i: (0, 0, 0))
    saved = pl.BlockSpec((N_CHAINS, None, HEAD_DIM, HEAD_DIM), lambda i: (0, step(i), 0, 0))
    inv = pl.BlockSpec((N_CHAINS, None, c, c), lambda i: (0, step(i), 0, 0))
    return seq, heads, colv, rowv, state, saved, inv


def _head_cols(h, part):
    lo = (part * GDN_HEADS + h) * HEAD_DIM
    return slice(lo, lo + HEAD_DIM)


def _chain_stack(x0_ref, x1_ref, part):
    return jnp.stack([(x0_ref if n < GDN_HEADS else x1_ref)[:, _head_cols(n % GDN_HEADS, part)] for n in range(N_CHAINS)])


def _scan_fwd(name, qkv, bcol, gcol, grow, s0):
    t = qkv.shape[0]
    nc = t // GDN_CHUNK
    seq, heads, colv, rowv, state, saved, inv = _scan_specs(nc, lambda i: i)

    def body(qkv0_ref, qkv1_ref, b_ref, gc_ref, gr_ref, s0_ref, o0_ref, o1_ref, save_ref, tinv_ref, fin_ref, s_ref):
        @pl.when(pl.program_id(0) == 0)
        def _():
            s_ref[...] = s0_ref[...]

        s = s_ref[...]
        save_ref[...] = s
        q, k, v = [_chain_stack(qkv0_ref, qkv1_ref, part) for part in range(3)]
        o, s_next, tinv = _gdn_chunk(q, k, v, gc_ref[...], gr_ref[...], b_ref[...], s)
        for n in range(N_CHAINS):
            d, h = divmod(n, GDN_HEADS)
            (o0_ref if d == 0 else o1_ref)[:, _head_cols(h, 0)] = o[n]
        tinv_ref[...] = tinv
        s_ref[...] = s_next
        fin_ref[...] = s_next

    hw = GDN_HEADS * HEAD_DIM
    return pl.pallas_call(
        body, name=name, grid=(nc,), in_specs=seq + [colv, colv, rowv, state],
        out_specs=heads + [saved, inv, state],
        out_shape=[_sds((t, hw)), _sds((t, hw)), _sds((N_CHAINS, nc, HEAD_DIM, HEAD_DIM)),
                   _sds((N_CHAINS, nc, GDN_CHUNK, GDN_CHUNK)), _sds((N_CHAINS, HEAD_DIM, HEAD_DIM))],
        scratch_shapes=[pltpu.VMEM((N_CHAINS, HEAD_DIM, HEAD_DIM), F32)],
        compiler_params=pltpu.CompilerParams(dimension_semantics=("arbitrary",), vmem_limit_bytes=VMEM_LIMIT_BYTES),
    )(qkv, qkv, bcol, gcol, grow, s0)


def _scan_bwd(name, do, qkv, bcol, gcol, grow, saved_s, saved_inv, ds_last):
    t = qkv.shape[0]
    nc = t // GDN_CHUNK
    c = GDN_CHUNK
    seq, heads, colv, rowv, state, saved, inv = _scan_specs(nc, lambda i: nc - 1 - i)

    def body(do0_ref, do1_ref, qkv0_ref, qkv1_ref, b_ref, gc_ref, gr_ref, s_ref, tinv_ref, dsl_ref,
             dx0_ref, dx1_ref, db_ref, dgc_ref, dgr_ref, ds0_ref, ds_ref):
        @pl.when(pl.program_id(0) == 0)
        def _():
            ds_ref[...] = dsl_ref[...]

        tinv = tinv_ref[...]
        q, k, v = [_chain_stack(qkv0_ref, qkv1_ref, part) for part in range(3)]
        do = jnp.stack([(do0_ref if n < GDN_HEADS else do1_ref)[:, _head_cols(n % GDN_HEADS, 0)] for n in range(N_CHAINS)])
        _, vjp = jax.vjp(lambda *a: _gdn_chunk(*a, tinv=tinv)[:2], q, k, v, gc_ref[...], gr_ref[...], b_ref[...], s_ref[...])
        dq, dk, dv, dgc, dgr, db, ds = vjp((do, ds_ref[...]))
        for n in range(N_CHAINS):
            d, h = divmod(n, GDN_HEADS)
            dx_ref = dx0_ref if d == 0 else dx1_ref
            dx_ref[:, _head_cols(h, 0)] = dq[n]
            dx_ref[:, _head_cols(h, 1)] = dk[n]
            dx_ref[:, _head_cols(h, 2)] = dv[n]
        db_ref[...] = db
        dgc_ref[...] = dgc
        dgr_ref[...] = dgr
        ds_ref[...] = ds
        ds0_ref[...] = ds

    vec_c = _sds((N_CHAINS, nc, c, 1))
    vec_r = _sds((N_CHAINS, nc, 1, c))
    return pl.pallas_call(
        body, name=name, grid=(nc,),
        in_specs=heads + seq + [colv, colv, rowv, saved, inv, state],
        out_specs=seq + [colv, colv, rowv, state],
        out_shape=[_sds((t, QKV_W)), _sds((t, QKV_W)), vec_c, vec_c, vec_r, _sds((N_CHAINS, HEAD_DIM, HEAD_DIM))],
        scratch_shapes=[pltpu.VMEM((N_CHAINS, HEAD_DIM, HEAD_DIM), F32)],
        compiler_params=pltpu.CompilerParams(dimension_semantics=("arbitrary",), vmem_limit_bytes=VMEM_LIMIT_BYTES),
    )(do, do, qkv, qkv, bcol, gcol, grow, saved_s, saved_inv, ds_last)


def _gate_layouts(gb):
    t = gb.shape[0]
    nc = t // GDN_CHUNK
    nh = GDN_HEADS

    def by_scan_position(a):
        a = a.T.reshape(2 * nh, nc, GDN_CHUNK)
        return jnp.concatenate([a[:nh], a[nh:, ::-1]], axis=0)

    beta = by_scan_position(gb[:, :2 * nh])
    g = by_scan_position(gb[:, 2 * nh:4 * nh])
    return beta[..., None], g[..., None], g[:, :, None, :]


def _gate_layouts_bwd(dbcol, dgcol, dgrow):
    n2, nc, c, _ = dbcol.shape
    nh = n2 // 2
    t = nc * c

    def by_token(a):
        return jnp.concatenate([a[:nh], a[nh:, ::-1]], axis=0).reshape(n2, t).T

    dbeta = by_token(dbcol[..., 0])
    dg = by_token(dgcol[..., 0] + dgrow[:, :, 0, :])
    return jnp.concatenate([dbeta, dg, jnp.zeros((t, LANES - 2 * n2), F32)], axis=1)


def kernel(x, c, ctx, c_ctx, ada_w, ada_b, ln_g, ln_b, even_w_in, even_w_out, gdn_conv_w, gdn_a_log, gdn_dt_bias, gdn_norm_w, pool_w, pool_scale, odd_w_in, odd_w_out, sconv_w, conf_conv_w, conf_ln_g, conf_ln_b, ffn_w_up, ffn_conv_w, ffn_w_down, loss_target, m_c_ctx, m_ada_w, m_ada_b, m_ln_g, m_ln_b, m_even_w_in, m_even_w_out, m_gdn_conv_w, m_gdn_a_log, m_gdn_dt_bias, m_gdn_norm_w, m_pool_w, m_pool_scale, m_odd_w_in, m_odd_w_out, m_sconv_w, m_conf_conv_w, m_conf_ln_g, m_conf_ln_b, m_ffn_w_up, m_ffn_conv_w, m_ffn_w_down, v_c_ctx, v_ada_w, v_ada_b, v_ln_g, v_ln_b, v_even_w_in, v_even_w_out, v_gdn_conv_w, v_gdn_a_log, v_gdn_dt_bias, v_gdn_norm_w, v_pool_w, v_pool_scale, v_odd_w_in, v_odd_w_out, v_sconv_w, v_conf_conv_w, v_conf_ln_g, v_conf_ln_b, v_ffn_w_up, v_ffn_conv_w, v_ffn_w_down):
    names = ['c_ctx', 'ada_w', 'ada_b', 'ln_g', 'ln_b', 'even_w_in', 'even_w_out', 'gdn_conv_w', 'gdn_a_log',
             'gdn_dt_bias', 'gdn_norm_w', 'pool_w', 'pool_scale', 'odd_w_in', 'odd_w_out', 'sconv_w', 'conf_conv_w',
             'conf_ln_g', 'conf_ln_b', 'ffn_w_up', 'ffn_conv_w', 'ffn_w_down']
    loc = locals()
    wts = {n: loc[n] for n in names}
    mom = {n: loc['m_' + n] for n in names}
    var = {n: loc['v_' + n] for n in names}

    ix, iy, ic = lax.axis_index("x"), lax.axis_index("y"), lax.axis_index("c")
    chip = 2 * ix + iy
    dev = 2 * chip + ic
    d = D_MODEL
    x0, ctx0, tgt = x[0], ctx[0], loss_target[0]
    t, tc = x0.shape[0], ctx0.shape[0]
    depth = ada_w.shape[0]
    n_ada = ada_w.shape[2]

    small_sharded = [gdn_conv_w, sconv_w, conf_conv_w, ffn_conv_w, ln_g, ln_b]
    s1_items = [c] + small_sharded
    s1 = _pack_rows(s1_items)
    g1 = _all_gather8("gather_small_in", s1, True).reshape(8, -1, LANES)
    c_all = g1[:, :_packed_rows(c.shape)].reshape(8, d)
    per_chip = [_unpack_rows(g1[2 * k], [a.shape for a in s1_items])[1:] for k in range(4)]
    gdn_conv_f, sconv_f, conf_conv_f, ffn_conv_f, ln_g_f, ln_b_f = [
        jnp.concatenate([per_chip[k][i] for k in range(4)], axis=-1) for i in range(len(small_sharded))]
    c16 = jnp.concatenate([c_all, c_ctx[None], jnp.zeros((7, d), F32)], axis=0)

    mod_part = _mod_rows(c16, ada_w)
    g2 = _all_gather8("gather_mod", mod_part.reshape(-1, LANES), True).reshape(4, 2, depth, 16, n_ada)[:, 0]
    mod_all = jnp.transpose(g2, (1, 2, 0, 3)).reshape(depth, 16, 4 * n_ada) + ada_b[:, None, :]
    mod_me = lax.dynamic_index_in_dim(mod_all, dev, axis=1, keepdims=False).reshape(depth, 6, 1, d)
    sh_c, sc_c = mod_all[0, 8, :d][None], mod_all[0, 8, d:2 * d][None]

    flat_names = ['even_w_in', 'even_w_out']
    flat_shapes = [wts[n].shape for n in flat_names]
    half_mult = 2 * 16 * LANES
    rh = -(-sum(math.prod(s) for s in flat_shapes) // half_mult) * half_mult // (2 * LANES)
    flat_pack = _pack_flat([wts[n] for n in flat_names], half_mult, MXU_DTYPE).reshape(2, rh, LANES)
    wg_flat = _all_gather8("gather_weights_even", lax.dynamic_index_in_dim(flat_pack, ic, axis=0, keepdims=False), True).reshape(4, -1)
    per_chip = [_unpack_flat(wg_flat[k], flat_shapes) for k in range(4)]
    n_even = 4 * even_w_in.shape[1]
    n_even_pad = -(-n_even // LANES) * LANES
    w_in = jnp.concatenate([per_chip[k][0] for k in range(4)] + [jnp.zeros((d, n_even_pad - n_even), MXU_DTYPE)], axis=1)
    w_out = jnp.concatenate([per_chip[k][1] for k in range(4)], axis=0)

    def row_half(a, core, axis):
        n = a.shape[axis] // 2
        return lax.dynamic_slice_in_dim(a, core * n, n, axis=axis)

    def layer_of(a, core):
        return lax.dynamic_index_in_dim(a, core, axis=0, keepdims=False)

    mine = [row_half(odd_w_in, ic, 0), row_half(odd_w_out, ic, 0), layer_of(ffn_w_up, ic), layer_of(ffn_w_down, ic)]
    late_sems, late_mine, late_lands, late_token = _chip_send_start(
        "gather_late_start", [m.astype(MXU_DTYPE) for m in mine], [wg_flat, mod_all])
    scal_blk = (n_even // LANES)
    n_scal = n_even - scal_blk * LANES

    def mod(layer, k):
        return mod_me[layer, k]

    avec = jnp.zeros((1, LANES), F32).at[0, n_scal // 2:n_scal].set(gdn_a_log.reshape(-1))
    dtvec = jnp.zeros((1, LANES), F32).at[0, n_scal // 2:n_scal].set(gdn_dt_bias.reshape(-1))
    normw = gdn_norm_w[None]
    pscale = pool_scale[None]
    cg, cb = conf_ln_g[None], conf_ln_b[None]
    lng = lambda l, k: ln_g_f[l, k][None]
    lnb = lambda l, k: ln_b_f[l, k][None]
    convw9 = ffn_conv_f.reshape(depth, 9, -1)
    nqkv = gdn_conv_f.shape[1]

    shift0 = mod(0, 0) + late_token[:1, :1]
    p, ub0 = _inproj("even_in", x0, shift0, mod(0, 1), w_in, 512)
    pc, ucb = _inproj("even_in_ctx", ctx0, sh_c, sc_c, w_in, 256)
    qkv = _qkv_conv("qkv_conv", p, gdn_conv_f)
    qkv_c = _qkv_conv("qkv_conv_ctx", pc, gdn_conv_f)
    gb = _gates("gates", p, avec, dtvec, scal_blk)
    gb_c = _gates("gates_ctx", pc, avec, dtvec, scal_blk)
    lay = _gate_layouts(gb)
    lay_c = _gate_layouts(gb_c)
    s_zero = jnp.zeros((2 * GDN_HEADS, HEAD_DIM, HEAD_DIM), F32)
    _, _, save_c, inv_c, s_ctx = _scan_fwd("scan_ctx", qkv_c, *lay_c, s_zero)
    o0, o1, save_l, inv_l, _ = _scan_fwd("scan", qkv, *lay, s_ctx)
    pool_blk = (nqkv + GDN_HEADS * HEAD_DIM) // 512
    ypool = _pool(p, pool_w, pscale, pool_blk)
    x1, mix0, y0 = _even_out(o0, o1, p, ypool, x0, normw, mod(0, 2), lng(0, 0), lnb(0, 0), w_out, 512)

    late_mine, late_lands = _chip_send_wait("gather_late_wait", late_sems, late_mine, late_lands, x1)
    own4 = [lax.dynamic_update_slice(land, m[None], (chip,) + (0,) * m.ndim) for land, m in zip(late_lands, late_mine)]
    got4 = _sibling_exchange("gather_late_exchange", own4)

    def of_core(i, core):
        return jnp.where(ic == core, own4[i], got4[i])

    w_oin = jnp.concatenate([of_core(0, 0), of_core(0, 1)], axis=1)
    w_oout = jnp.concatenate([of_core(1, 0), of_core(1, 1)], axis=1).reshape(-1, d)
    w_up = [of_core(2, l) for l in range(depth)]
    w_down = [of_core(3, l) for l in range(depth)]

    def ffn_fwd(l, xin, target=None):
        h, ub = _inproj(f"ffn_up{l}", xin, mod(l, 3), mod(l, 4), w_up[l], 256)
        s, cv = _ffn_conv(f"ffn_conv{l}", h, convw9[l])
        *head, y = _ffn_down(f"ffn_down{l}", s, xin, mod(l, 5), lng(l, 1), lnb(l, 1), w_down[l], 512, target)
        return (head[0] if target is None else head), (h, cv, ub, s, y)

    x2, ffn0 = ffn_fwd(0, x1)
    p1, ub1 = _inproj("odd_in", x2, mod(1, 0), mod(1, 1), w_oin, 512)
    nsc = sconv_f.shape[1] // LANES
    ysc = _sconv(p1, sconv_f)
    z0 = _confconv(p1, conf_conv_f, 3 * nsc)
    x3, mix1, y1 = _odd_out(ysc, z0, x2, cg, cb, mod(1, 2), lng(1, 0), lnb(1, 0), w_oout, 512)
    (loss_part, dx4), ffn1 = ffn_fwd(1, x3, tgt)
    loss = lax.psum(loss_part[0, 0], ("x", "y", "c"))

    dmod = [[None] * 6 for _ in range(depth)]
    dlng = [[None, None] for _ in range(depth)]
    dlnb = [[None, None] for _ in range(depth)]
    gbig = {}
    dconv9 = [None] * depth

    def ffn_bwd(l, dxo, xin, saved):
        h, cv, ub, s, y = saved
        ds, dxa, dyb, dgt, dg_, db_ = _ffn_down_bwd(f"ffn_down_bwd{l}", dxo, xin, y, mod(l, 5), lng(l, 1), lnb(l, 1), w_down[l], 512)
        da, dgate, dw9 = _ffn_conv_bwd(f"ffn_conv_bwd{l}", ds, h, cv, convw9[l])
        dh = jnp.concatenate([da, dgate], axis=1)
        dxin, dsh, dsc = _inproj_bwd(f"ffn_up_bwd{l}", dh, xin, mod(l, 3), mod(l, 4), w_up[l], dxa, 256)
        dmod[l][3], dmod[l][4], dmod[l][5] = dsh, dsc, dgt
        dlng[l][1], dlnb[l][1] = dg_, db_
        dconv9[l] = dw9
        dw_up = _matmul_tn(f"dw_up{l}", [(ub, dh)], chip_cols=ffn_w_up.shape[2])
        dw_down = _matmul_tn(f"dw_down{l}", [(s, dyb)]).reshape(4, -1, d)
        return dxin, dw_up, dw_down

    dx3, dwu1, dwd1 = ffn_bwd(1, dx4, x3, ffn1)
    dysc, dz0, dx2a, dyb1, dcg, dcb, dgt, dg_, db_ = _odd_out_bwd(dx3, ysc, z0, x2, y1, cg, cb, mod(1, 2), lng(1, 0), lnb(1, 0), w_oout, 512)
    dmod[1][2], dlng[1][0], dlnb[1][0] = dgt, dg_, db_
    d_gb, d_gc, d_h, dsconv = _sconv_bwd(dysc, p1, sconv_f)
    d_ga, d_gbb, dconf = _confconv_bwd(dz0, p1, conf_conv_f, 3 * nsc)
    dp1 = jnp.concatenate([d_gb, d_gc, d_h, d_ga, d_gbb], axis=1)
    dx2, dsh, dsc = _inproj_bwd("odd_in_bwd", dp1, x2, mod(1, 0), mod(1, 1), w_oin, dx2a, 512)
    dmod[1][0], dmod[1][1] = dsh, dsc
    dw_oout = _matmul_tn("dw_oout", [(mix1, dyb1)]).reshape(4, -1, d)
    dw_oin = _matmul_tn("dw_oin", [(ub1, dp1)], chip_cols=odd_w_in.shape[1])

    dx1, dwu0, dwd0 = ffn_bwd(0, dx2, x1, ffn0)

    tiled_halves = [
        (dw_oin[:, :d // 2], dw_oin[:, d // 2:]),
        (dw_oout[:, :odd_w_out.shape[0] // 2], dw_oout[:, odd_w_out.shape[0] // 2:]),
        (dwu0, dwu1),
        (dwd0, dwd1),
    ]
    keeps = [jnp.where(ic == 0, h0, h1) for h0, h1 in tiled_halves]
    gives = [jnp.where(ic == 0, h1, h0) for h0, h1 in tiled_halves]
    gots = _sibling_exchange("grad_pair_exchange", gives)
    pairs = [_sum_pair(f"grad_pair_sum{i}", kp, gt_, MXU_DTYPE) for i, (kp, gt_) in enumerate(zip(keeps, gots))]
    gs_sems, gs_pairs, gs_lands, gs_token = _chip_send_start("grad_scatter_start", pairs, [], per_peer=True)
    gate0 = mod(0, 2) + gs_token[:1, :1]

    do, dpg, dypool, dx0a, dyb0, dnormw, dgt, dg_, db_ = _even_out_bwd(dx1, o0, o1, p, ypool, x0, y0, normw, gate0, lng(0, 0), lnb(0, 0), w_out, 512)
    dmod[0][2], dlng[0][0], dlnb[0][0] = dgt, dg_, db_
    pool_cts = _pool_bwd(dypool, p, pool_w, pscale, pool_blk)
    dpp, dpool_scale, dpool_w = pool_cts[0], pool_cts[1], jnp.stack(pool_cts[2:])
    dqkv0, dqkv1, dbcol, dgcol, dgrow, ds0 = _scan_bwd("scan_bwd", do, qkv, *lay, save_l, inv_l, s_zero)
    zero_do = jnp.zeros((tc, GDN_HEADS * HEAD_DIM), F32)
    dqkv0_c, dqkv1_c, dbcol_c, dgcol_c, dgrow_c, _ = _scan_bwd("scan_bwd_ctx", zero_do, qkv_c, *lay_c, save_c, inv_c, ds0)
    dgb = _gate_layouts_bwd(dbcol, dgcol, dgrow)
    dgb_c = _gate_layouts_bwd(dbcol_c, dgcol_c, dgrow_c)
    dps, davec, ddtvec = _gates_bwd("gates_bwd", dgb, p, avec, dtvec, scal_blk)
    dps_c, davec_c, ddtvec_c = _gates_bwd("gates_bwd_ctx", dgb_c, pc, avec, dtvec, scal_blk)
    dpqkv, dconv5 = _qkv_conv_bwd("qkv_conv_bwd", dqkv0, dqkv1, p, gdn_conv_f)
    dpqkv_c, dconv5_c = _qkv_conv_bwd("qkv_conv_bwd_ctx", dqkv0_c, dqkv1_c, pc, gdn_conv_f)
    dp = jnp.concatenate([dpqkv, dpg, dpp, dps], axis=1)
    dpc = jnp.concatenate([dpqkv_c, jnp.zeros((tc, n_even_pad - nqkv - LANES), MXU_DTYPE), dps_c], axis=1)
    grad_x, dsh, dsc = _inproj_bwd("even_in_bwd", dp, x0, mod(0, 0), mod(0, 1), w_in, dx0a, 512)
    dmod[0][0], dmod[0][1] = dsh, dsc
    _, dsh_c, dsc_c = _inproj_bwd("even_in_bwd_ctx", dpc, ctx0, sh_c, sc_c, w_in, None, 256)
    dw_in = _matmul_tn("dw_in", [(ub0, dp), (ucb, dpc)])
    dw_out = _matmul_tn("dw_out", [(mix0, dyb0)])

    gs_pairs, gs_lands = _chip_send_wait("grad_scatter_wait", gs_sems, gs_pairs, gs_lands, dw_in, per_peer=True)
    own4 = [lax.dynamic_update_slice(land, lax.dynamic_index_in_dim(pr, chip, axis=0, keepdims=True), (chip,) + (0,) * (pr.ndim - 1))
            for land, pr in zip(gs_lands, gs_pairs)]
    n_in, n_out = even_w_in.shape[1], even_w_out.shape[0]
    gflat = jnp.stack([_pack_flat([dw_in[:, k * n_in:(k + 1) * n_in], dw_out[k * n_out:(k + 1) * n_out]], half_mult,
                                  MXU_DTYPE).reshape(2, rh, LANES) for k in range(4)], axis=1)
    got_flat, *got4 = _sibling_exchange("grad_late_exchange", [jnp.where(ic == 0, gflat[1], gflat[0])] + own4)
    pair_flat = _sum_pair("grad_pair_sum_flat", jnp.where(ic == 0, gflat[0], gflat[1]), got_flat, MXU_DTYPE)
    sum_flat = _sum_axis1("grad_chip_sum_flat", _chip_scatter("grad_chip_scatter", [pair_flat])[0])

    def both_halves(i):
        mine_sum = _sum_axis1(f"grad_chip_sum{i}_own", own4[i][None])[0]
        sib_sum = _sum_axis1(f"grad_chip_sum{i}_sibling", got4[i][None])[0]
        return jnp.where(ic == 0, jnp.stack([mine_sum, sib_sum]), jnp.stack([sib_sum, mine_sum]))

    g_even_in, g_even_out = _unpack_flat(sum_flat.reshape(-1), flat_shapes)
    g_shards = {'even_w_in': g_even_in, 'even_w_out': g_even_out, 'odd_w_in': both_halves(0).reshape(odd_w_in.shape),
                'odd_w_out': both_halves(1).reshape(odd_w_out.shape), 'ffn_w_up': both_halves(2), 'ffn_w_down': both_halves(3)}

    dmod_rows = jnp.stack([jnp.concatenate(dmod[l], axis=1)[0] for l in range(depth)])
    dmod_c = jnp.concatenate([dsh_c[0], dsc_c[0], jnp.zeros((4 * d,), F32)])
    dmod_c_rows = jnp.stack([dmod_c] + [jnp.zeros_like(dmod_c)] * (depth - 1))
    small_g = {
        'ln_g': jnp.stack([jnp.stack([dlng[l][k][0] for k in range(2)]) for l in range(depth)]),
        'ln_b': jnp.stack([jnp.stack([dlnb[l][k][0] for k in range(2)]) for l in range(depth)]),
        'gdn_conv_w': dconv5 + dconv5_c,
        'gdn_a_log': (davec + davec_c)[0, n_scal // 2:n_scal].reshape(gdn_a_log.shape),
        'gdn_dt_bias': (ddtvec + ddtvec_c)[0, n_scal // 2:n_scal].reshape(gdn_dt_bias.shape),
        'gdn_norm_w': dnormw[0], 'pool_w': dpool_w, 'pool_scale': dpool_scale[0],
        'sconv_w': dsconv, 'conf_conv_w': dconf, 'conf_ln_g': dcg[0], 'conf_ln_b': dcb[0],
        'ffn_conv_w': jnp.stack(dconv9).reshape(depth, 3, 3, -1),
    }
    small_names = list(small_g)
    s3_items = [dmod_rows, dmod_c_rows] + [small_g[n] for n in small_names]
    s3 = _pack_rows(s3_items)
    g3 = _all_gather8("gather_small_grads", s3, True).reshape(8, -1, LANES)
    tot3 = _sum_axis1("small_grad_sum", g3[None])[0]
    tot_items = _unpack_rows(tot3, [a.shape for a in s3_items])
    dmod_sum, dmod_c_sum = tot_items[0], tot_items[1]
    small_tot = dict(zip(small_names, tot_items[2:]))
    grad_ada_b = dmod_sum + dmod_c_sum
    rows_all = g3[:, :_packed_rows(dmod_rows.shape)].reshape(8, depth, 6 * d)
    cols = lax.dynamic_slice_in_dim(rows_all, chip * n_ada, n_ada, axis=2)
    crow = lax.dynamic_slice_in_dim(dmod_c_sum, chip * n_ada, n_ada, axis=1)
    dmod16 = jnp.concatenate([jnp.transpose(cols, (1, 0, 2)), crow[:, None, :], jnp.zeros((depth, 7, n_ada), F32)], axis=1)
    grad_ada_w, dsil = _ada_grads(c16, dmod16, ada_w)
    s4 = jnp.concatenate([dsil[0, 8][None], jnp.zeros((7, d), F32)], axis=0).reshape(-1, LANES)
    g4 = _all_gather8("gather_cctx", s4, True).reshape(8, 8, d)
    grad_c_ctx = _cctx_grad(g4[0::2, 0][:, None, :], c_ctx[None])[0]

    def my_cols(a, n):
        return lax.dynamic_slice_in_dim(a, chip * n, n, axis=a.ndim - 1)

    grads = dict(g_shards)
    grads['c_ctx'] = grad_c_ctx
    grads['ada_w'] = grad_ada_w
    grads['ada_b'] = grad_ada_b
    for n in ['ln_g', 'ln_b', 'gdn_conv_w', 'sconv_w', 'conf_conv_w', 'ffn_conv_w']:
        grads[n] = my_cols(small_tot[n], wts[n].shape[-1])
    for n in ['gdn_a_log', 'gdn_dt_bias', 'gdn_norm_w', 'pool_w', 'pool_scale', 'conf_ln_g', 'conf_ln_b']:
        grads[n] = small_tot[n]

    delta, new_m, new_v = {}, {}, {}
    big_adam = list(g_shards) + ['ada_w']
    for n in big_adam:
        shp = wts[n].shape
        as2d = lambda a: a.reshape(-1, shp[-1])
        dl, nm, nv = _adamw("adamw_" + n, as2d(wts[n]), as2d(grads[n]), as2d(mom[n]), as2d(var[n]))
        delta[n], new_m[n], new_v[n] = dl.reshape(shp), nm.reshape(shp), nv.reshape(shp)
    small_adam = [n for n in names if n not in big_adam]
    for n in small_adam:
        shp = wts[n].shape
        whole = lambda a: a.reshape((1,) + shp) if len(shp) == 1 else a
        dl, nm, nv = _adamw("adamw_" + n, whole(wts[n]), whole(grads[n]), whole(mom[n]), whole(var[n]))
        delta[n], new_m[n], new_v[n] = dl.reshape(shp), nm.reshape(shp), nv.reshape(shp)

    return (loss, grad_x[None], *[grads[n] for n in names], *[delta[n] for n in names],
            *[new_m[n] for n in names], *[new_v[n] for n in names])
```
